```python
import math
import jax, jax.numpy as jnp
from jax import lax
import numpy as np

D_MODEL = 1024
BATCH = 2
SEQ = 8192
DEPTH = 2

HEAD_DIM = 64
GRID_W = 64
RMS_EPS = 1e-6
NEG = -1e30
A_PATTERNS = ((128, 1), (512, 4), (2048, 16))
A_GROUPS = 3
A_HEADS = 4
A_W = A_GROUPS * A_HEADS * HEAD_DIM
A_OUT = A_HEADS * HEAD_DIM
B_HEADS = 8
B_W = B_HEADS * HEAD_DIM
B_WIN_ROWS = 8
B_WIN_COLS = 16
B_QCOLS = 16
B_KCOLS = 32
C_Q_HEADS = 8
C_KV_HEADS = 2
C_QW = C_Q_HEADS * HEAD_DIM
C_KVW = C_KV_HEADS * HEAD_DIM
C_QBLOCK = 128
ROPE_THETA = 10000.0
ROPE_AXIS_DIM = HEAD_DIM // 2
T5_BUCKETS = 32
T5_MAX_DIST = 1024
N_BRANCH = 3
GATE_W = N_BRANCH * D_MODEL
IN_SIZES = (A_W, A_W, A_W, B_W, B_W, B_W, C_QW, C_KVW, C_KVW, GATE_W)
IN_W = sum(IN_SIZES)
D_FF = math.ceil(8 * D_MODEL / 3 / 256) * 256

kernel_name = "hybrid_dilated_natten_gqa_gated_encoder"


def rms_norm(x, g):
    xf = x.astype(jnp.float32)
    y = xf * lax.rsqrt(jnp.mean(xf * xf, axis=-1, keepdims=True) + RMS_EPS)
    return (y * g.astype(jnp.float32)).astype(x.dtype)


def t5_bucket(rel):
    half = T5_BUCKETS // 2
    max_exact = half // 2
    ret = jnp.where(rel > 0, half, 0)
    n = jnp.abs(rel)
    nf = jnp.maximum(n, 1).astype(jnp.float32)
    large = max_exact + (jnp.log(nf / max_exact) / math.log(T5_MAX_DIST / max_exact)
                         * (half - max_exact)).astype(jnp.int32)
    large = jnp.minimum(large, half - 1)
    return ret + jnp.where(n < max_exact, n, large)


def t5_dilated_bias(table_g, window, rate):
    R = window // (2 * rate)
    i = jnp.arange(R)[:, None]
    j = jnp.arange(3 * R)[None, :]
    rel = (j - R - i) * rate
    return jnp.transpose(table_g[t5_bucket(rel)], (2, 0, 1)).astype(jnp.float32)


def dilated_window_attn(q, k, v, bias, window, rate):
    B, S, H, hd = q.shape
    R = window // (2 * rate)
    L = S // rate
    nb = -(-L // R)
    Lp = nb * R

    def to_sub(t):
        t = t.reshape(B, L, rate, H, hd).transpose(0, 2, 1, 3, 4)
        return jnp.pad(t, ((0, 0), (0, 0), (0, Lp - L), (0, 0), (0, 0)))

    def key_blocks(t):
        tp = jnp.pad(to_sub(t), ((0, 0), (0, 0), (R, R), (0, 0), (0, 0)))
        tp = tp.reshape(B, rate, nb + 2, R, H, hd)
        return jnp.concatenate([tp[:, :, :-2], tp[:, :, 1:-1], tp[:, :, 2:]], axis=3)

    qs = to_sub(q).reshape(B, rate, nb, R, H, hd)
    kb, vb = key_blocks(k), key_blocks(v)
    s = jnp.einsum('brnqhd,brnkhd->brnhqk', qs, kb,
                   preferred_element_type=jnp.float32) * (hd ** -0.5) + bias
    i = jnp.arange(R)[:, None]
    j = jnp.arange(3 * R)[None, :]
    in_win = jnp.abs(j - R - i) <= R
    kpos = jnp.arange(nb)[:, None] * R + jnp.arange(3 * R)[None, :] - R
    valid = (kpos >= 0) & (kpos < L)
    mask = in_win[None, :, :] & valid[:, None, :]
    s = jnp.where(mask[:, None], s, NEG)
    m = jnp.max(s, axis=-1, keepdims=True)
    e = jnp.exp(s - m)
    den = jnp.sum(e, axis=-1, keepdims=True)
    p = (e / den).astype(v.dtype)
    lse = (m + jnp.log(den))[..., 0]
    o = jnp.einsum('brnhqk,brnkhd->brnqhd', p, vb)
    o = o.reshape(B, rate, Lp, H, hd)[:, :, :L].transpose(0, 2, 1, 3, 4).reshape(B, S, H, hd)
    lse = lse.transpose(0, 1, 2, 4, 3).reshape(B, rate, Lp, H)[:, :, :L]
    lse = lse.transpose(0, 2, 1, 3).reshape(B, S, H)
    return o, lse


def neighbourhood_attn(q, k, v, rpb):
    B, S, H, hd = q.shape
    rows = S // GRID_W
    kh = min(B_WIN_ROWS, rows)
    qg = q.reshape(B, rows, GRID_W, H, hd)
    kg = k.reshape(B, rows, GRID_W, H, hd)
    vg = v.reshape(B, rows, GRID_W, H, hd)
    n_cb = GRID_W // B_QCOLS
    mb = jnp.arange(n_cb)
    cs = jnp.clip(mb * B_QCOLS - B_WIN_COLS // 2, 0, GRID_W - B_KCOLS)
    kcol = cs[:, None] + jnp.arange(B_KCOLS)[None, :]
    qcol = mb[:, None] * B_QCOLS + jnp.arange(B_QCOLS)[None, :]
    c0 = jnp.clip(qcol - B_WIN_COLS // 2, 0, GRID_W - B_WIN_COLS)
    col_ok = (kcol[:, None, :] >= c0[..., None]) & (kcol[:, None, :] < c0[..., None] + B_WIN_COLS)
    dc_idx = jnp.clip(kcol[:, None, :] - qcol[..., None] + B_WIN_COLS - 1, 0, 2 * B_WIN_COLS - 2)
    scale = hd ** -0.5

    def one_row(i):
        rs = jnp.clip(i - kh // 2, 0, rows - kh)
        kr = lax.dynamic_slice_in_dim(kg, rs, kh, axis=1)[:, :, kcol]
        vr = lax.dynamic_slice_in_dim(vg, rs, kh, axis=1)[:, :, kcol]
        qr = lax.dynamic_index_in_dim(qg, i, axis=1, keepdims=False).reshape(B, n_cb, B_QCOLS, H, hd)
        s = jnp.einsum('bmqhd,bamchd->bhmqac', qr, kr, preferred_element_type=jnp.float32) * scale
        dr_idx = rs + jnp.arange(kh) - i + B_WIN_ROWS - 1
        bias = rpb[:, dr_idx[None, None, :, None], dc_idx[:, :, None, :]]
        s = jnp.where(col_ok[:, :, None, :], s + bias.astype(jnp.float32), NEG)
        p = jax.nn.softmax(s.reshape(B, H, n_cb, B_QCOLS, kh * B_KCOLS), axis=-1)
        p = p.reshape(B, H, n_cb, B_QCOLS, kh, B_KCOLS).astype(v.dtype)
        o = jnp.einsum('bhmqac,bamchd->bmqhd', p, vr)
        return o.reshape(B, GRID_W, H, hd)

    o = lax.map(one_row, jnp.arange(rows))
    return o.transpose(1, 0, 2, 3, 4).reshape(B, S, H, hd)


def axial_rope_tables(S):
    t = jnp.arange(S)
    inv = ROPE_THETA ** (-jnp.arange(0, ROPE_AXIS_DIM, 2, dtype=jnp.float32) / ROPE_AXIS_DIM)
    ang_r = (t // GRID_W).astype(jnp.float32)[:, None] * inv[None, :]
    ang_c = (t % GRID_W).astype(jnp.float32)[:, None] * inv[None, :]
    return jnp.cos(ang_r), jnp.sin(ang_r), jnp.cos(ang_c), jnp.sin(ang_c)


def rotate(x, cos, sin):
    x1, x2 = jnp.split(x, 2, axis=-1)
    c = cos[None, :, None, :]
    s = sin[None, :, None, :]
    return jnp.concatenate([x1 * c - x2 * s, x1 * s + x2 * c], axis=-1)


def apply_axial_rope(x, tabs):
    cos_r, sin_r, cos_c, sin_c = tabs
    xf = x.astype(jnp.float32)
    out = jnp.concatenate([rotate(xf[..., :ROPE_AXIS_DIM], cos_r, sin_r),
                           rotate(xf[..., ROPE_AXIS_DIM:], cos_c, sin_c)], axis=-1)
    return out.astype(x.dtype)


def gqa_blocked(q, k, v):
    B, S, Hq, hd = q.shape
    Hkv = k.shape[2]
    G = Hq // Hkv
    nqb = S // C_QBLOCK
    qb = q.reshape(B, nqb, C_QBLOCK, Hkv, G, hd).transpose(1, 0, 2, 3, 4, 5)
    scale = hd ** -0.5

    def blk(qi):
        s = jnp.einsum('bqkgd,bskd->bkgqs', qi, k, preferred_element_type=jnp.float32) * scale
        p = jax.nn.softmax(s, axis=-1).astype(v.dtype)
        return jnp.einsum('bkgqs,bskd->bqkgd', p, v)

    o = lax.map(blk, qb)
    return o.transpose(1, 0, 2, 3, 4, 5).reshape(B, S, Hq, hd)


def hybrid_layer(x, t5_biases, rope_tabs, g1, w_in, qk_g, rpb, p_a, p_b, p_c, w_o, g2, w_up, w_down):
    B, S, D = x.shape
    h = rms_norm(x, g1)
    z = h @ w_in
    cuts = np.cumsum(IN_SIZES)[:-1].tolist()
    qa, ka, va, qb, kb, vb, qc, kc, vc, zg = jnp.split(z, cuts, axis=-1)
    qa = rms_norm(qa.reshape(B, S, A_GROUPS, A_HEADS, HEAD_DIM), qk_g[0])
    ka = rms_norm(ka.reshape(B, S, A_GROUPS, A_HEADS, HEAD_DIM), qk_g[1])
    va = va.reshape(B, S, A_GROUPS, A_HEADS, HEAD_DIM)
    outs, lses = [], []
    for g, (win, rate) in enumerate(A_PATTERNS):
        o_g, l_g = dilated_window_attn(qa[:, :, g], ka[:, :, g], va[:, :, g], t5_biases[g], win, rate)
        outs.append(o_g)
        lses.append(l_g)
    wts = jax.nn.softmax(jnp.stack(lses, axis=2), axis=2)
    o_a = jnp.sum(wts[..., None].astype(x.dtype) * jnp.stack(outs, axis=2), axis=2).reshape(B, S, A_OUT)
    qb = rms_norm(qb.reshape(B, S, B_HEADS, HEAD_DIM), qk_g[2])
    kb = rms_norm(kb.reshape(B, S, B_HEADS, HEAD_DIM), qk_g[3])
    o_b = neighbourhood_attn(qb, kb, vb.reshape(B, S, B_HEADS, HEAD_DIM), rpb).reshape(B, S, B_W)
    qc = apply_axial_rope(rms_norm(qc.reshape(B, S, C_Q_HEADS, HEAD_DIM), qk_g[4]), rope_tabs)
    kc = apply_axial_rope(rms_norm(kc.reshape(B, S, C_KV_HEADS, HEAD_DIM), qk_g[5]), rope_tabs)
    o_c = gqa_blocked(qc, kc, vc.reshape(B, S, C_KV_HEADS, HEAD_DIM)).reshape(B, S, C_QW)
    gates = jax.nn.sigmoid(zg.reshape(B, S, N_BRANCH, D))
    merged = gates[:, :, 0] * (o_a @ p_a) + gates[:, :, 1] * (o_b @ p_b) + gates[:, :, 2] * (o_c @ p_c)
    x = x + merged @ w_o
    u = rms_norm(x, g2) @ w_up
    a, b = jnp.split(u, 2, axis=-1)
    return x + (jax.nn.silu(a) * b) @ w_down


def setup_inputs(seed: int = 0) -> dict:
    key = jax.random.key(seed)
    ks = jax.random.split(key, 14)
    f32 = jnp.float32
    nrm = lambda k, shape, s: jax.random.normal(k, shape, f32) * s
    return {
        "x": nrm(ks[0], (BATCH, SEQ, D_MODEL), 1.0),
        "rel_bias_table": nrm(ks[1], (T5_BUCKETS, A_GROUPS * A_HEADS), 0.3),
        "norm1": 1.0 + nrm(ks[2], (DEPTH, D_MODEL), 0.05),
        "w_in": nrm(ks[3], (DEPTH, D_MODEL, IN_W), D_MODEL ** -0.5),
        "qk_gain": 1.0 + nrm(ks[4], (DEPTH, 6, HEAD_DIM), 0.05),
        "nat_rpb": nrm(ks[5], (DEPTH, B_HEADS, 2 * B_WIN_ROWS - 1, 2 * B_WIN_COLS - 1), 0.3),
        "w_br_a": nrm(ks[6], (DEPTH, A_OUT, D_MODEL), A_OUT ** -0.5),
        "w_br_b": nrm(ks[7], (DEPTH, B_W, D_MODEL), B_W ** -0.5),
        "w_br_c": nrm(ks[8], (DEPTH, C_QW, D_MODEL), C_QW ** -0.5),
        "w_o": nrm(ks[9], (DEPTH, D_MODEL, D_MODEL), D_MODEL ** -0.5),
        "norm2": 1.0 + nrm(ks[10], (DEPTH, D_MODEL), 0.05),
        "w_up": nrm(ks[11], (DEPTH, D_MODEL, 2 * D_FF), D_MODEL ** -0.5),
        "w_down": nrm(ks[12], (DEPTH, D_FF, D_MODEL), D_FF ** -0.5),
    }


def reference(x, rel_bias_table, norm1, w_in, qk_gain, nat_rpb, w_br_a, w_br_b, w_br_c, w_o,
              norm2, w_up, w_down):
    S = x.shape[1]
    t5_biases = [t5_dilated_bias(rel_bias_table[:, g * A_HEADS:(g + 1) * A_HEADS], win, rate)
                 for g, (win, rate) in enumerate(A_PATTERNS)]
    rope_tabs = axial_rope_tables(S)
    for l in range(DEPTH):
        x = hybrid_layer(x, t5_biases, rope_tabs, norm1[l], w_in[l], qk_gain[l], nat_rpb[l],
                         w_br_a[l], w_br_b[l], w_br_c[l], w_o[l], norm2[l], w_up[l], w_down[l])
    return x
```

```python
import functools
import math

import jax
import jax.numpy as jnp
import numpy as np
from jax import lax
from jax.experimental import pallas as pl
from jax.experimental.pallas import tpu as pltpu

_MXU_DTYPE = jnp.bfloat16

D_MODEL = 1024
HEAD_DIM = 64
GRID_W = 64
RMS_EPS = 1e-6
NEG = -1e30
A_PATTERNS = ((128, 1), (512, 4), (2048, 16))
A_GROUPS = 3
A_HEADS = 4
A_W = A_GROUPS * A_HEADS * HEAD_DIM
A_OUT = A_HEADS * HEAD_DIM
A_RADIUS = 64
B_HEADS = 8
B_W = B_HEADS * HEAD_DIM
B_WIN_ROWS = 8
B_WIN_COLS = 16
C_Q_HEADS = 8
C_KV_HEADS = 2
C_GROUP = C_Q_HEADS // C_KV_HEADS
C_QW = C_Q_HEADS * HEAD_DIM
C_KVW = C_KV_HEADS * HEAD_DIM
ROPE_THETA = 10000.0
ROPE_AXIS_DIM = HEAD_DIM // 2
T5_BUCKETS = 32
T5_MAX_DIST = 1024
N_BRANCH = 3
D_FF = math.ceil(8 * D_MODEL / 3 / 256) * 256
QK_SCALE = HEAD_DIM ** -0.5

VMEM_LIMIT_BYTES = 56 * 1024 * 1024
LANES = 128

_OFF = np.cumsum([0, A_W, A_W, A_W, B_W, B_W, B_W, C_QW, C_KVW, C_KVW]).tolist()
(_QA, _KA, _VA, _QB, _KB, _VB, _QC, _KC, _VC, _ZG) = _OFF

ZA_W = 3 * A_W
ZB_W = 3 * B_W


def _params(sem):
    return pltpu.CompilerParams(dimension_semantics=sem, vmem_limit_bytes=VMEM_LIMIT_BYTES)


def _const_spec(shape):
    nd = len(shape)
    return pl.BlockSpec(shape, lambda *_: (0,) * nd, pipeline_mode=pl.Buffered(1))


def _rms(x, g):
    return x * lax.rsqrt(jnp.mean(x * x, axis=-1, keepdims=True) + RMS_EPS) * g


def _dot(a, b):
    return jnp.dot(a, b, preferred_element_type=jnp.float32)


def _dot_nt(a, b):
    return lax.dot_general(a, b, (((1,), (1,)), ((), ())), preferred_element_type=jnp.float32)


PROJ_TM = 512
_N_NORM = 2 * A_W + 2 * B_W
_N_ROPE = C_QW + C_KVW
_N_PLAIN = A_W + B_W + C_KVW


def _proj_kernel(x_ref, g_ref, w_ref, gain_ref, ones_ref, cos_ref, s1_ref, s2_ref,
                 za_ref, zb_ref, qc_ref, kc_ref, vc_ref):
    h = _rms(x_ref[...], g_ref[...]).astype(_MXU_DTYPE)

    def head_norm(acc, c0, width):
        sq = (acc * acc).astype(_MXU_DTYPE)
        ms = _dot(sq, ones_ref[:width, :width]) * (1.0 / HEAD_DIM)
        return acc * lax.rsqrt(ms + RMS_EPS) * gain_ref[:, c0:c0 + width]

    plan = []
    for t in range(A_W // 256):
        plan.append((za_ref, t * 256, t * 256))
    for t in range(A_W // 256):
        plan.append((za_ref, A_W + t * 256, A_W + t * 256))
    for t in range(B_W // 256):
        plan.append((zb_ref, t * 256, 2 * A_W + t * 256))
    for t in range(B_W // 256):
        plan.append((zb_ref, B_W + t * 256, 2 * A_W + B_W + t * 256))
    for dst, dc, wc in plan:
        acc = _dot(h, w_ref[:, wc:wc + 256])
        dst[:, dc:dc + 256] = head_norm(acc, wc, 256).astype(dst.dtype)

    cos = cos_ref[...]
    s1 = s1_ref[...]
    s2 = s2_ref[...]
    for t in range(_N_ROPE // LANES):
        wc = _N_NORM + t * LANES
        y = head_norm(_dot(h, w_ref[:, wc:wc + LANES]), wc, LANES)
        y = y * cos + pltpu.roll(y, LANES - 16, 1) * s1 + pltpu.roll(y, 16, 1) * s2
        y = y.astype(qc_ref.dtype)
        if t < C_QW // LANES:
            qc_ref[0, 2 * t] = y[:, :HEAD_DIM]
            qc_ref[0, 2 * t + 1] = y[:, HEAD_DIM:]
        else:
            kc_ref[0, 0] = y[:, :HEAD_DIM]
            kc_ref[0, 1] = y[:, HEAD_DIM:]

    base = _N_NORM + _N_ROPE
    for t in range(A_W // 256):
        acc = _dot(h, w_ref[:, base + t * 256:base + (t + 1) * 256])
        za_ref[:, 2 * A_W + t * 256:2 * A_W + (t + 1) * 256] = acc.astype(za_ref.dtype)
    for t in range(B_W // 256):
        wc = base + A_W + t * 256
        acc = _dot(h, w_ref[:, wc:wc + 256])
        zb_ref[:, 2 * B_W + t * 256:2 * B_W + (t + 1) * 256] = acc.astype(zb_ref.dtype)
    wc = base + A_W + B_W
    acc = _dot(h, w_ref[:, wc:wc + LANES]).astype(vc_ref.dtype)
    vc_ref[0, 0] = acc[:, :HEAD_DIM]
    vc_ref[0, 1] = acc[:, HEAD_DIM:]


def _qkv_projection(x2, g1, w_qkv, gain_row, ones_bd, rope, B, S):
    M = x2.shape[0]
    tm = PROJ_TM
    n_w = w_qkv.shape[1]
    per_seq = S // tm
    cos_t, s1_t, s2_t = rope
    row = lambda i: (i, 0)
    pos = lambda i: (i % per_seq, 0)
    hm = lambda i: (i // per_seq, 0, i % per_seq, 0)
    return pl.pallas_call(
        _proj_kernel,
        grid=(M // tm,),
        in_specs=[
            pl.BlockSpec((tm, D_MODEL), row),
            _const_spec((1, D_MODEL)),
            _const_spec((D_MODEL, n_w)),
            _const_spec((1, _N_NORM + _N_ROPE)),
            _const_spec((256, 256)),
            pl.BlockSpec((tm, LANES), pos),
            pl.BlockSpec((tm, LANES), pos),
            pl.BlockSpec((tm, LANES), pos),
        ],
        out_specs=[
            pl.BlockSpec((tm, ZA_W), row),
            pl.BlockSpec((tm, ZB_W), row),
            pl.BlockSpec((1, C_Q_HEADS, tm, HEAD_DIM), hm),
            pl.BlockSpec((1, C_KV_HEADS, tm, HEAD_DIM), hm),
            pl.BlockSpec((1, C_KV_HEADS, tm, HEAD_DIM), hm),
        ],
        out_shape=[
            jax.ShapeDtypeStruct((M, ZA_W), _MXU_DTYPE),
            jax.ShapeDtypeStruct((M, ZB_W), _MXU_DTYPE),
            jax.ShapeDtypeStruct((B, C_Q_HEADS, S, HEAD_DIM), _MXU_DTYPE),
            jax.ShapeDtypeStruct((B, C_KV_HEADS, S, HEAD_DIM), _MXU_DTYPE),
            jax.ShapeDtypeStruct((B, C_KV_HEADS, S, HEAD_DIM), _MXU_DTYPE),
        ],
        compiler_params=_params(("arbitrary",)),
        name="qkv_projection",
    )(x2, g1, w_qkv, gain_row, ones_bd, cos_t, s1_t, s2_t)


A_SUB = 2 * A_RADIUS
A_KEYS = 4 * A_RADIUS


def _mixer_a_kernel(q_ref, kp_ref, ko_ref, kn_ref, vp_ref, vo_ref, vn_ref, bias_ref,
                    o_ref, lse_ref, *, tl, seq_len):
    l0 = pl.program_id(2) * tl
    q = q_ref[0]
    kw = jnp.concatenate([kp_ref[0], ko_ref[0], kn_ref[0]], axis=0)
    vw = jnp.concatenate([vp_ref[0], vo_ref[0], vn_ref[0]], axis=0)
    n_sub = tl // A_SUB
    col = lax.broadcasted_iota(jnp.int32, (A_SUB, A_KEYS), 1)
    for h in range(A_HEADS):
        hs = slice(h * HEAD_DIM, (h + 1) * HEAD_DIM)
        bias = bias_ref[h]
        for sb in range(n_sub):
            i0 = sb * A_SUB
            s = _dot_nt(q[i0:i0 + A_SUB, hs], kw[i0:i0 + A_KEYS, hs]) + bias
            if sb == 0 or sb == n_sub - 1:
                kpos = col + (l0 + i0 - A_RADIUS)
                s = jnp.where((kpos >= 0) & (kpos < seq_len), s, NEG)
            m = jnp.max(s, axis=-1, keepdims=True)
            e = jnp.exp(s - m)
            den = jnp.sum(e, axis=-1, keepdims=True)
            o = _dot(e.astype(_MXU_DTYPE), vw[i0:i0 + A_KEYS, hs]) / den
            o_ref[0, i0:i0 + A_SUB, hs] = o
            lse_ref[0, i0:i0 + A_SUB, hs] = jnp.broadcast_to(m + jnp.log(den), (A_SUB, HEAD_DIM))


def _mixer_a_group(za, bias, g, rate, B, S):
    L = S // rate
    tl = min(512, L)
    nblk = L // A_RADIUS
    per = tl // A_RADIUS
    cpb = ZA_W // A_OUT
    za3 = za.reshape(B, L, rate * ZA_W)

    def own(sec):
        return lambda b, r, l: (b, l, r * cpb + sec * A_GROUPS + g)

    def prev(sec):
        return lambda b, r, l: (b, jnp.maximum(l * per - 1, 0), r * cpb + sec * A_GROUPS + g)

    def nxt(sec):
        return lambda b, r, l: (b, jnp.minimum((l + 1) * per, nblk - 1), r * cpb + sec * A_GROUPS + g)

    edge = (1, A_RADIUS, A_OUT)
    full = (1, tl, A_OUT)
    out_map = lambda b, r, l: (b, l, r)
    o, lse = pl.pallas_call(
        functools.partial(_mixer_a_kernel, tl=tl, seq_len=L),
        grid=(B, rate, L // tl),
        in_specs=[
            pl.BlockSpec(full, own(0)),
            pl.BlockSpec(edge, prev(1)), pl.BlockSpec(full, own(1)), pl.BlockSpec(edge, nxt(1)),
            pl.BlockSpec(edge, prev(2)), pl.BlockSpec(full, own(2)), pl.BlockSpec(edge, nxt(2)),
            _const_spec((A_HEADS, A_SUB, A_KEYS)),
        ],
        out_specs=[pl.BlockSpec(full, out_map), pl.BlockSpec(full, out_map)],
        out_shape=[jax.ShapeDtypeStruct((B, L, rate * A_OUT), jnp.float32)] * 2,
        compiler_params=_params(("arbitrary",) * 3),
        name=f"mixer_a_rate{rate}",
    )(za3, za3, za3, za3, za3, za3, za3, bias)
    return o.reshape(B * S, A_OUT), lse.reshape(B * S, A_OUT)


B_TILE_ROWS = 8
B_TILE = B_TILE_ROWS * GRID_W
B_KEYS = B_WIN_ROWS * GRID_W


def _mixer_b_kernel(q_ref, kp_ref, ko_ref, kn_ref, vp_ref, vo_ref, vn_ref, bias_ref,
                    o_ref, kw_ref, vw_ref, *, rows):
    i0 = pl.program_id(1) * B_TILE_ROWS
    kw_ref[0:B_TILE] = kp_ref[0]
    kw_ref[B_TILE:2 * B_TILE] = ko_ref[0]
    kw_ref[2 * B_TILE:3 * B_TILE] = kn_ref[0]
    vw_ref[0:B_TILE] = vp_ref[0]
    vw_ref[B_TILE:2 * B_TILE] = vo_ref[0]
    vw_ref[2 * B_TILE:3 * B_TILE] = vn_ref[0]

    def one_row(qi, carry):
        i = i0 + qi
        rs = jnp.clip(i - B_WIN_ROWS // 2, 0, rows - B_WIN_ROWS)
        variant = i - rs
        off = pl.multiple_of((rs - (i0 - B_TILE_ROWS)) * GRID_W, GRID_W)
        qoff = pl.multiple_of(qi * GRID_W, GRID_W)
        q = q_ref[0, pl.ds(qoff, GRID_W), :]
        kw = kw_ref[pl.ds(off, B_KEYS), :]
        vw = vw_ref[pl.ds(off, B_KEYS), :]
        for h in range(B_HEADS):
            hs = slice(h * HEAD_DIM, (h + 1) * HEAD_DIM)
            s = _dot_nt(q[:, hs], kw[:, hs]) + bias_ref[variant, h]
            m = jnp.max(s, axis=-1, keepdims=True)
            e = jnp.exp(s - m)
            den = jnp.sum(e, axis=-1, keepdims=True)
            o = _dot(e.astype(_MXU_DTYPE), vw[:, hs]) / den
            o_ref[0, pl.ds(qoff, GRID_W), hs] = o.astype(o_ref.dtype)
        return carry

    lax.fori_loop(0, B_TILE_ROWS, one_row, 0)


def _mixer_b(zb, bias_var, B, S):
    rows = S // GRID_W
    nt = rows // B_TILE_ROWS
    zb3 = zb.reshape(B, S, ZB_W)
    blk = (1, B_TILE, B_W)

    def own(sec):
        return lambda b, t: (b, t, sec)

    def prev(sec):
        return lambda b, t: (b, jnp.maximum(t - 1, 0), sec)

    def nxt(sec):
        return lambda b, t: (b, jnp.minimum(t + 1, nt - 1), sec)

    o = pl.pallas_call(
        functools.partial(_mixer_b_kernel, rows=rows),
        grid=(B, nt),
        in_specs=[
            pl.BlockSpec(blk, own(0)),
            pl.BlockSpec(blk, prev(1)), pl.BlockSpec(blk, own(1)), pl.BlockSpec(blk, nxt(1)),
            pl.BlockSpec(blk, prev(2)), pl.BlockSpec(blk, own(2)), pl.BlockSpec(blk, nxt(2)),
            _const_spec(bias_var.shape),
        ],
        out_specs=pl.BlockSpec(blk, lambda b, t: (b, t, 0)),
        out_shape=jax.ShapeDtypeStruct((B, S, B_W), _MXU_DTYPE),
        scratch_shapes=[pltpu.VMEM((3 * B_TILE, B_W), _MXU_DTYPE),
                        pltpu.VMEM((3 * B_TILE, B_W), _MXU_DTYPE)],
        compiler_params=_params(("arbitrary",) * 2),
        name="mixer_b",
    )(zb3, zb3, zb3, zb3, zb3, zb3, zb3, bias_var)
    return o.reshape(B * S, B_W)


C_TQ = 256
C_TK = 512


def _mixer_c_kernel(q_ref, k_ref, v_ref, o_ref, m_ref, l_ref, acc_ref, *, n_kv):
    rows = C_GROUP * C_TQ
    q = q_ref[0].reshape(rows, HEAD_DIM)
    m_ref[...] = jnp.full(m_ref.shape, -jnp.inf, jnp.float32)
    l_ref[...] = jnp.zeros(l_ref.shape, jnp.float32)
    acc_ref[...] = jnp.zeros(acc_ref.shape, jnp.float32)

    def step(j, carry):
        koff = pl.multiple_of(j * C_TK, C_TK)
        k = k_ref[0, 0, pl.ds(koff, C_TK), :]
        v = v_ref[0, 0, pl.ds(koff, C_TK), :]
        s = _dot_nt(q, k)
        m_prev = m_ref[...]
        m_new = jnp.maximum(m_prev, jnp.max(s, axis=-1, keepdims=True))
        alpha = jnp.exp(m_prev - m_new)
        p = jnp.exp(s - m_new)
        l_ref[...] = alpha * l_ref[...] + jnp.sum(p, axis=-1, keepdims=True)
        acc_ref[...] = alpha * acc_ref[...] + _dot(p.astype(_MXU_DTYPE), v)
        m_ref[...] = m_new
        return carry

    lax.fori_loop(0, n_kv, step, 0)
    o = acc_ref[...] / l_ref[...]
    for g in range(C_GROUP):
        o_ref[0, :, g * HEAD_DIM:(g + 1) * HEAD_DIM] = o[g * C_TQ:(g + 1) * C_TQ].astype(o_ref.dtype)


def _mixer_c(qc, kc, vc, B, S):
    rows = C_GROUP * C_TQ
    o = pl.pallas_call(
        functools.partial(_mixer_c_kernel, n_kv=S // C_TK),
        grid=(B, C_KV_HEADS, S // C_TQ),
        in_specs=[
            pl.BlockSpec((1, C_GROUP, C_TQ, HEAD_DIM), lambda b, kv, i: (b, kv, i, 0)),
            pl.BlockSpec((1, 1, S, HEAD_DIM), lambda b, kv, i: (b, kv, 0, 0)),
            pl.BlockSpec((1, 1, S, HEAD_DIM), lambda b, kv, i: (b, kv, 0, 0)),
        ],
        out_specs=pl.BlockSpec((1, C_TQ, C_GROUP * HEAD_DIM), lambda b, kv, i: (b, i, kv)),
        out_shape=jax.ShapeDtypeStruct((B, S, C_QW), _MXU_DTYPE),
        scratch_shapes=[pltpu.VMEM((rows, 1), jnp.float32),
                        pltpu.VMEM((rows, 1), jnp.float32),
                        pltpu.VMEM((rows, HEAD_DIM), jnp.float32)],
        compiler_params=_params(("arbitrary",) * 3),
        name="mixer_c",
    )(qc, kc, vc)
    return o.reshape(B * S, C_QW)


MERGE_TM = 512


def _merge_kernel(x_ref, g_ref, oa0_ref, oa1_ref, oa2_ref, l0_ref, l1_ref, l2_ref, ob_ref, oc_ref,
                  wg_ref, pa_ref, pb_ref, pc_ref, wo_ref, out_ref):
    x = x_ref[...]
    h = _rms(x, g_ref[...]).astype(_MXU_DTYPE)
    l0, l1, l2 = l0_ref[...], l1_ref[...], l2_ref[...]
    mx = jnp.maximum(jnp.maximum(l0, l1), l2)
    w0, w1, w2 = jnp.exp(l0 - mx), jnp.exp(l1 - mx), jnp.exp(l2 - mx)
    o_a = (w0 * oa0_ref[...] + w1 * oa1_ref[...] + w2 * oa2_ref[...]) / (w0 + w1 + w2)
    merged = jax.nn.sigmoid(_dot(h, wg_ref[:, 0:D_MODEL])) * _dot(o_a.astype(_MXU_DTYPE), pa_ref[...])
    merged += jax.nn.sigmoid(_dot(h, wg_ref[:, D_MODEL:2 * D_MODEL])) * _dot(ob_ref[...], pb_ref[...])
    merged += jax.nn.sigmoid(_dot(h, wg_ref[:, 2 * D_MODEL:3 * D_MODEL])) * _dot(oc_ref[...], pc_ref[...])
    out_ref[...] = x + _dot(merged.astype(_MXU_DTYPE), wo_ref[...])


def _merge(x2, g1, oa, lse, ob, oc, wg, pa, pb, pc, wo):
    M = x2.shape[0]
    tm = MERGE_TM
    row = lambda i: (i, 0)
    tile = lambda w: pl.BlockSpec((tm, w), row)
    return pl.pallas_call(
        _merge_kernel,
        grid=(M // tm,),
        in_specs=[tile(D_MODEL), _const_spec((1, D_MODEL)),
                  tile(A_OUT), tile(A_OUT), tile(A_OUT), tile(A_OUT), tile(A_OUT), tile(A_OUT),
                  tile(B_W), tile(C_QW),
                  _const_spec(wg.shape), _const_spec(pa.shape), _const_spec(pb.shape),
                  _const_spec(pc.shape), _const_spec(wo.shape)],
        out_specs=tile(D_MODEL),
        out_shape=jax.ShapeDtypeStruct((M, D_MODEL), jnp.float32),
        compiler_params=_params(("arbitrary",)),
        name="gated_merge",
    )(x2, g1, oa[0], oa[1], oa[2], lse[0], lse[1], lse[2], ob, oc, wg, pa, pb, pc, wo)


FFN_TM = 512
FFN_CHUNK = 256


def _ffn_kernel(x_ref, g_ref, wup_ref, wdown_ref, out_ref, act_ref):
    x = x_ref[...]
    xn = _rms(x, g_ref[...]).astype(_MXU_DTYPE)
    for c in range(D_FF // FFN_CHUNK):
        cs = slice(c * FFN_CHUNK, (c + 1) * FFN_CHUNK)
        a = _dot(xn, wup_ref[:, cs])
        b = _dot(xn, wup_ref[:, D_FF + c * FFN_CHUNK:D_FF + (c + 1) * FFN_CHUNK])
        act_ref[:, cs] = (a * jax.nn.sigmoid(a) * b).astype(act_ref.dtype)
    out_ref[...] = x + _dot(act_ref[...], wdown_ref[...])


def _ffn(x2, g2, wup, wdown):
    M = x2.shape[0]
    tm = FFN_TM
    row = lambda i: (i, 0)
    return pl.pallas_call(
        _ffn_kernel,
        grid=(M // tm,),
        in_specs=[pl.BlockSpec((tm, D_MODEL), row), _const_spec((1, D_MODEL)),
                  _const_spec(wup.shape), _const_spec(wdown.shape)],
        out_specs=pl.BlockSpec((tm, D_MODEL), row),
        out_shape=jax.ShapeDtypeStruct((M, D_MODEL), jnp.float32),
        scratch_shapes=[pltpu.VMEM((tm, D_FF), _MXU_DTYPE)],
        compiler_params=_params(("arbitrary",)),
        name="swiglu_ffn",
    )(x2, g2, wup, wdown)


def _t5_bucket(rel):
    half = T5_BUCKETS // 2
    max_exact = half // 2
    ret = jnp.where(rel > 0, half, 0)
    n = jnp.abs(rel)
    nf = jnp.maximum(n, 1).astype(jnp.float32)
    large = max_exact + (jnp.log(nf / max_exact) / math.log(T5_MAX_DIST / max_exact)
                         * (half - max_exact)).astype(jnp.int32)
    large = jnp.minimum(large, half - 1)
    return ret + jnp.where(n < max_exact, n, large)


def _mixer_a_bias(table_g, rate):
    i = jnp.arange(A_SUB)[:, None]
    j = jnp.arange(A_KEYS)[None, :]
    step = j - A_RADIUS - i
    bias = jnp.transpose(table_g[_t5_bucket(step * rate)], (2, 0, 1)).astype(jnp.float32)
    return jnp.where((jnp.abs(step) <= A_RADIUS)[None], bias, NEG)


def _mixer_b_bias(rpb):
    c = np.arange(GRID_W)
    c0 = np.clip(c - B_WIN_COLS // 2, 0, GRID_W - B_WIN_COLS)
    col_ok = (c[None, :] >= c0[:, None]) & (c[None, :] < c0[:, None] + B_WIN_COLS)
    dc = np.clip(c[None, :] - c[:, None] + B_WIN_COLS - 1, 0, 2 * B_WIN_COLS - 2)
    full = jnp.where(col_ok[None, None], rpb[:, :, dc].astype(jnp.float32), NEG)
    dr = np.arange(B_WIN_ROWS)[None, :] - np.arange(B_WIN_ROWS)[:, None] + B_WIN_ROWS - 1
    var = full[:, dr]
    var = jnp.transpose(var, (1, 0, 3, 2, 4))
    return var.reshape(B_WIN_ROWS, B_HEADS, GRID_W, B_WIN_ROWS * GRID_W)


def _rope_tables(S):
    t = jnp.arange(S)
    inv = ROPE_THETA ** (-jnp.arange(0, ROPE_AXIS_DIM, 2, dtype=jnp.float32) / ROPE_AXIS_DIM)
    ang_r = (t // GRID_W).astype(jnp.float32)[:, None] * inv[None, :]
    ang_c = (t % GRID_W).astype(jnp.float32)[:, None] * inv[None, :]
    d = np.arange(LANES) % HEAD_DIM
    f = d % (ROPE_AXIS_DIM // 2)
    is_col = (d >= ROPE_AXIS_DIM)[None, :]
    ang = jnp.where(is_col, ang_c[:, f], ang_r[:, f])
    first = ((d % ROPE_AXIS_DIM) < ROPE_AXIS_DIM // 2)[None, :]
    sin = jnp.sin(ang)
    return jnp.cos(ang), jnp.where(first, -sin, 0.0), jnp.where(first, 0.0, sin)


def _block_diag_ones():
    i = np.arange(256) // HEAD_DIM
    return jnp.asarray(i[:, None] == i[None, :], _MXU_DTYPE)


def kernel(x, rel_bias_table, norm1, w_in, qk_gain, nat_rpb, w_br_a, w_br_b, w_br_c, w_o,
           norm2, w_up, w_down):
    B, S, D = x.shape
    depth = w_in.shape[0]
    cd = _MXU_DTYPE
    x2 = x.reshape(B * S, D)
    rope = _rope_tables(S)
    ones_bd = _block_diag_ones()
    a_bias = [_mixer_a_bias(rel_bias_table[:, g * A_HEADS:(g + 1) * A_HEADS], rate)
              for g, (_, rate) in enumerate(A_PATTERNS)]
    sec = lambda w, lo, n: w[:, lo:lo + n]
    for l in range(depth):
        w = w_in[l]
        w_qkv = jnp.concatenate(
            [sec(w, _QA, A_W), sec(w, _KA, A_W), sec(w, _QB, B_W), sec(w, _KB, B_W),
             sec(w, _QC, C_QW), sec(w, _KC, C_KVW),
             sec(w, _VA, A_W), sec(w, _VB, B_W), sec(w, _VC, C_KVW)], axis=1).astype(cd)
        gq = qk_gain[l]
        tile = lambda g, n, s: jnp.tile(g * s, n)
        gain_row = jnp.concatenate([
            tile(gq[0], A_W // HEAD_DIM, QK_SCALE), tile(gq[1], A_W // HEAD_DIM, 1.0),
            tile(gq[2], B_HEADS, QK_SCALE), tile(gq[3], B_HEADS, 1.0),
            tile(gq[4], C_Q_HEADS, QK_SCALE), tile(gq[5], C_KV_HEADS, 1.0)])[None, :]
        g1 = norm1[l][None, :]
        za, zb, qc, kc, vc = _qkv_projection(x2, g1, w_qkv, gain_row, ones_bd, rope, B, S)
        oa, lse = [], []
        for g, (_, rate) in enumerate(A_PATTERNS):
            o_g, l_g = _mixer_a_group(za, a_bias[g], g, rate, B, S)
            oa.append(o_g)
            lse.append(l_g)
        ob = _mixer_b(zb, _mixer_b_bias(nat_rpb[l]), B, S)
        oc = _mixer_c(qc, kc, vc, B, S)
        x2 = _merge(x2, g1, oa, lse, ob, oc, sec(w, _ZG, N_BRANCH * D).astype(cd),
                    w_br_a[l].astype(cd), w_br_b[l].astype(cd), w_br_c[l].astype(cd), w_o[l].astype(cd))
        x2 = _ffn(x2, norm2[l][None, :], w_up[l].astype(cd), w_down[l].astype(cd))
    return x2.reshape(B, S, D)
```

```python
import functools
import math

import jax
import jax.numpy as jnp
import numpy as np
from jax import lax
from jax.experimental import pallas as pl
from jax.experimental.pallas import tpu as pltpu

_MXU_DTYPE = jnp.bfloat16

D_MODEL = 1024
HEAD_DIM = 64
GRID_W = 64
RMS_EPS = 1e-6
NEG = -1e30
A_PATTERNS = ((128, 1), (512, 4), (2048, 16))
A_GROUPS = 3
A_HEADS = 4
A_W = A_GROUPS * A_HEADS * HEAD_DIM
A_OUT = A_HEADS * HEAD_DIM
A_RADIUS = 64
B_HEADS = 8
B_W = B_HEADS * HEAD_DIM
B_WIN_ROWS = 8
B_WIN_COLS = 16
C_Q_HEADS = 8
C_KV_HEADS = 2
C_GROUP = C_Q_HEADS // C_KV_HEADS
C_QW = C_Q_HEADS * HEAD_DIM
C_KVW = C_KV_HEADS * HEAD_DIM
ROPE_THETA = 10000.0
ROPE_AXIS_DIM = HEAD_DIM // 2
T5_BUCKETS = 32
T5_MAX_DIST = 1024
N_BRANCH = 3
D_FF = math.ceil(8 * D_MODEL / 3 / 256) * 256
QK_SCALE = HEAD_DIM ** -0.5
LOG2_E = math.log2(math.e)
C_VT_ROWS = HEAD_DIM + 16

VMEM_LIMIT_BYTES = 56 * 1024 * 1024
LANES = 128

_OFF = np.cumsum([0, A_W, A_W, A_W, B_W, B_W, B_W, C_QW, C_KVW, C_KVW]).tolist()
(_QA, _KA, _VA, _QB, _KB, _VB, _QC, _KC, _VC, _ZG) = _OFF

ZA_W = 3 * A_W
ZB_W = 3 * B_W


def _params(sem):
    return pltpu.CompilerParams(dimension_semantics=sem, vmem_limit_bytes=VMEM_LIMIT_BYTES)


def _const_spec(shape):
    nd = len(shape)
    return pl.BlockSpec(shape, lambda *_: (0,) * nd, pipeline_mode=pl.Buffered(1))


def _rms(x, g):
    return x * lax.rsqrt(jnp.mean(x * x, axis=-1, keepdims=True) + RMS_EPS) * g


def _dot(a, b):
    return jnp.dot(a, b, preferred_element_type=jnp.float32)


def _dot_nt(a, b):
    return lax.dot_general(a, b, (((1,), (1,)), ((), ())), preferred_element_type=jnp.float32)


PROJ_TM = 512
_N_NORM = 2 * A_W + 2 * B_W
_N_ROPE = C_QW + C_KVW
_N_PLAIN = A_W + B_W + C_KVW


def _proj_kernel(x_ref, g_ref, w_ref, gain_ref, ones_ref, cos_ref, s1_ref, s2_ref,
                 za_ref, zb_ref, qc_ref, kc_ref, vc_ref):
    h = _rms(x_ref[...], g_ref[...]).astype(_MXU_DTYPE)

    def head_norm(acc, c0, width):
        sq = (acc * acc).astype(_MXU_DTYPE)
        ms = _dot(sq, ones_ref[:width, :width]) * (1.0 / HEAD_DIM)
        return acc * lax.rsqrt(ms + RMS_EPS) * gain_ref[:, c0:c0 + width]

    plan = []
    for t in range(A_W // 256):
        plan.append((za_ref, t * 256, t * 256))
    for t in range(A_W // 256):
        plan.append((za_ref, A_W + t * 256, A_W + t * 256))
    for t in range(B_W // 256):
        plan.append((zb_ref, t * 256, 2 * A_W + t * 256))
    for t in range(B_W // 256):
        plan.append((zb_ref, B_W + t * 256, 2 * A_W + B_W + t * 256))
    for dst, dc, wc in plan:
        acc = _dot(h, w_ref[:, wc:wc + 256])
        dst[:, dc:dc + 256] = head_norm(acc, wc, 256).astype(dst.dtype)

    cos = cos_ref[...]
    s1 = s1_ref[...]
    s2 = s2_ref[...]
    for t in range(_N_ROPE // LANES):
        wc = _N_NORM + t * LANES
        y = head_norm(_dot(h, w_ref[:, wc:wc + LANES]), wc, LANES)
        y = y * cos + pltpu.roll(y, LANES - 16, 1) * s1 + pltpu.roll(y, 16, 1) * s2
        if t < C_QW // LANES:
            yt = y.T.astype(qc_ref.dtype)
            qc_ref[0, 2 * t] = yt[:HEAD_DIM]
            qc_ref[0, 2 * t + 1] = yt[HEAD_DIM:]
        else:
            y = y.astype(kc_ref.dtype)
            kc_ref[0, 0] = y[:, :HEAD_DIM]
            kc_ref[0, 1] = y[:, HEAD_DIM:]

    base = _N_NORM + _N_ROPE
    for t in range(A_W // 256):
        acc = _dot(h, w_ref[:, base + t * 256:base + (t + 1) * 256])
        za_ref[:, 2 * A_W + t * 256:2 * A_W + (t + 1) * 256] = acc.astype(za_ref.dtype)
    for t in range(B_W // 256):
        wc = base + A_W + t * 256
        acc = _dot(h, w_ref[:, wc:wc + 256])
        zb_ref[:, 2 * B_W + t * 256:2 * B_W + (t + 1) * 256] = acc.astype(zb_ref.dtype)
    wc = base + A_W + B_W
    acc_t = _dot(h, w_ref[:, wc:wc + LANES]).T.astype(vc_ref.dtype)
    tail = lax.broadcasted_iota(jnp.int32, (C_VT_ROWS - HEAD_DIM, acc_t.shape[1]), 0) == 0
    tail = tail.astype(vc_ref.dtype)
    for kv in range(C_KV_HEADS):
        vc_ref[0, kv, :HEAD_DIM] = acc_t[kv * HEAD_DIM:(kv + 1) * HEAD_DIM]
        vc_ref[0, kv, HEAD_DIM:] = tail


def _qkv_projection(x2, g1, w_qkv, gain_row, ones_bd, rope, B, S):
    M = x2.shape[0]
    tm = PROJ_TM
    n_w = w_qkv.shape[1]
    per_seq = S // tm
    cos_t, s1_t, s2_t = rope
    row = lambda i: (i, 0)
    pos = lambda i: (i % per_seq, 0)
    hm = lambda i: (i // per_seq, 0, i % per_seq, 0)
    hm_t = lambda i: (i // per_seq, 0, 0, i % per_seq)
    return pl.pallas_call(
        _proj_kernel,
        grid=(M // tm,),
        in_specs=[
            pl.BlockSpec((tm, D_MODEL), row),
            _const_spec((1, D_MODEL)),
            _const_spec((D_MODEL, n_w)),
            _const_spec((1, _N_NORM + _N_ROPE)),
            _const_spec((256, 256)),
            pl.BlockSpec((tm, LANES), pos),
            pl.BlockSpec((tm, LANES), pos),
            pl.BlockSpec((tm, LANES), pos),
        ],
        out_specs=[
            pl.BlockSpec((tm, ZA_W), row),
            pl.BlockSpec((tm, ZB_W), row),
            pl.BlockSpec((1, C_Q_HEADS, HEAD_DIM, tm), hm_t),
            pl.BlockSpec((1, C_KV_HEADS, tm, HEAD_DIM), hm),
            pl.BlockSpec((1, C_KV_HEADS, C_VT_ROWS, tm), hm_t),
        ],
        out_shape=[
            jax.ShapeDtypeStruct((M, ZA_W), _MXU_DTYPE),
            jax.ShapeDtypeStruct((M, ZB_W), _MXU_DTYPE),
            jax.ShapeDtypeStruct((B, C_Q_HEADS, HEAD_DIM, S), _MXU_DTYPE),
            jax.ShapeDtypeStruct((B, C_KV_HEADS, S, HEAD_DIM), _MXU_DTYPE),
            jax.ShapeDtypeStruct((B, C_KV_HEADS, C_VT_ROWS, S), _MXU_DTYPE),
        ],
        compiler_params=_params(("arbitrary",)),
        name="qkv_projection",
    )(x2, g1, w_qkv, gain_row, ones_bd, cos_t, s1_t, s2_t)


A_SUB = 2 * A_RADIUS
A_KEYS = 4 * A_RADIUS


def _mixer_a_kernel(q_ref, kp_ref, ko_ref, kn_ref, vp_ref, vo_ref, vn_ref, bias_ref,
                    o_ref, lse_ref, *, tl, seq_len):
    l0 = pl.program_id(2) * tl
    q = q_ref[0]
    kw = jnp.concatenate([kp_ref[0], ko_ref[0], kn_ref[0]], axis=0)
    vw = jnp.concatenate([vp_ref[0], vo_ref[0], vn_ref[0]], axis=0)
    n_sub = tl // A_SUB
    col = lax.broadcasted_iota(jnp.int32, (A_SUB, A_KEYS), 1)
    for h in range(A_HEADS):
        hs = slice(h * HEAD_DIM, (h + 1) * HEAD_DIM)
        bias = bias_ref[h]
        for sb in range(n_sub):
            i0 = sb * A_SUB
            s = _dot_nt(q[i0:i0 + A_SUB, hs], kw[i0:i0 + A_KEYS, hs]) + bias
            if sb == 0 or sb == n_sub - 1:
                kpos = col + (l0 + i0 - A_RADIUS)
                s = jnp.where((kpos >= 0) & (kpos < seq_len), s, NEG)
            m = jnp.max(s, axis=-1, keepdims=True)
            e = jnp.exp(s - m)
            den = jnp.sum(e, axis=-1, keepdims=True)
            o = _dot(e.astype(_MXU_DTYPE), vw[i0:i0 + A_KEYS, hs]) / den
            o_ref[0, i0:i0 + A_SUB, hs] = o
            lse_ref[0, i0:i0 + A_SUB, hs] = jnp.broadcast_to(m + jnp.log(den), (A_SUB, HEAD_DIM))


def _mixer_a_group(za, bias, g, rate, B, S):
    L = S // rate
    tl = min(512, L)
    nblk = L // A_RADIUS
    per = tl // A_RADIUS
    cpb = ZA_W // A_OUT
    za3 = za.reshape(B, L, rate * ZA_W)

    def own(sec):
        return lambda b, r, l: (b, l, r * cpb + sec * A_GROUPS + g)

    def prev(sec):
        return lambda b, r, l: (b, jnp.maximum(l * per - 1, 0), r * cpb + sec * A_GROUPS + g)

    def nxt(sec):
        return lambda b, r, l: (b, jnp.minimum((l + 1) * per, nblk - 1), r * cpb + sec * A_GROUPS + g)

    edge = (1, A_RADIUS, A_OUT)
    full = (1, tl, A_OUT)
    out_map = lambda b, r, l: (b, l, r)
    o, lse = pl.pallas_call(
        functools.partial(_mixer_a_kernel, tl=tl, seq_len=L),
        grid=(B, rate, L // tl),
        in_specs=[
            pl.BlockSpec(full, own(0)),
            pl.BlockSpec(edge, prev(1)), pl.BlockSpec(full, own(1)), pl.BlockSpec(edge, nxt(1)),
            pl.BlockSpec(edge, prev(2)), pl.BlockSpec(full, own(2)), pl.BlockSpec(edge, nxt(2)),
            _const_spec((A_HEADS, A_SUB, A_KEYS)),
        ],
        out_specs=[pl.BlockSpec(full, out_map), pl.BlockSpec(full, out_map)],
        out_shape=[jax.ShapeDtypeStruct((B, L, rate * A_OUT), jnp.float32)] * 2,
        compiler_params=_params(("arbitrary",) * 3),
        name=f"mixer_a_rate{rate}",
    )(za3, za3, za3, za3, za3, za3, za3, bias)
    return o.reshape(B * S, A_OUT), lse.reshape(B * S, A_OUT)


B_TILE_ROWS = 8
B_TILE = B_TILE_ROWS * GRID_W
B_KEYS = B_WIN_ROWS * GRID_W


def _mixer_b_kernel(q_ref, kp_ref, ko_ref, kn_ref, vp_ref, vo_ref, vn_ref, bias_ref,
                    o_ref, kw_ref, vw_ref, *, rows):
    i0 = pl.program_id(1) * B_TILE_ROWS
    kw_ref[0:B_TILE] = kp_ref[0]
    kw_ref[B_TILE:2 * B_TILE] = ko_ref[0]
    kw_ref[2 * B_TILE:3 * B_TILE] = kn_ref[0]
    vw_ref[0:B_TILE] = vp_ref[0]
    vw_ref[B_TILE:2 * B_TILE] = vo_ref[0]
    vw_ref[2 * B_TILE:3 * B_TILE] = vn_ref[0]

    def one_row(qi, carry):
        i = i0 + qi
        rs = jnp.clip(i - B_WIN_ROWS // 2, 0, rows - B_WIN_ROWS)
        variant = i - rs
        off = pl.multiple_of((rs - (i0 - B_TILE_ROWS)) * GRID_W, GRID_W)
        qoff = pl.multiple_of(qi * GRID_W, GRID_W)
        q = q_ref[0, pl.ds(qoff, GRID_W), :]
        kw = kw_ref[pl.ds(off, B_KEYS), :]
        vw = vw_ref[pl.ds(off, B_KEYS), :]
        for h in range(B_HEADS):
            hs = slice(h * HEAD_DIM, (h + 1) * HEAD_DIM)
            s = _dot_nt(q[:, hs], kw[:, hs]) + bias_ref[variant, h]
            m = jnp.max(s, axis=-1, keepdims=True)
            e = jnp.exp(s - m)
            den = jnp.sum(e, axis=-1, keepdims=True)
            o = _dot(e.astype(_MXU_DTYPE), vw[:, hs]) / den
            o_ref[0, pl.ds(qoff, GRID_W), hs] = o.astype(o_ref.dtype)
        return carry

    lax.fori_loop(0, B_TILE_ROWS, one_row, 0)


def _mixer_b(zb, bias_var, B, S):
    rows = S // GRID_W
    nt = rows // B_TILE_ROWS
    zb3 = zb.reshape(B, S, ZB_W)
    blk = (1, B_TILE, B_W)

    def own(sec):
        return lambda b, t: (b, t, sec)

    def prev(sec):
        return lambda b, t: (b, jnp.maximum(t - 1, 0), sec)

    def nxt(sec):
        return lambda b, t: (b, jnp.minimum(t + 1, nt - 1), sec)

    o = pl.pallas_call(
        functools.partial(_mixer_b_kernel, rows=rows),
        grid=(B, nt),
        in_specs=[
            pl.BlockSpec(blk, own(0)),
            pl.BlockSpec(blk, prev(1)), pl.BlockSpec(blk, own(1)), pl.BlockSpec(blk, nxt(1)),
            pl.BlockSpec(blk, prev(2)), pl.BlockSpec(blk, own(2)), pl.BlockSpec(blk, nxt(2)),
            _const_spec(bias_var.shape),
        ],
        out_specs=pl.BlockSpec(blk, lambda b, t: (b, t, 0)),
        out_shape=jax.ShapeDtypeStruct((B, S, B_W), _MXU_DTYPE),
        scratch_shapes=[pltpu.VMEM((3 * B_TILE, B_W), _MXU_DTYPE),
                        pltpu.VMEM((3 * B_TILE, B_W), _MXU_DTYPE)],
        compiler_params=_params(("arbitrary",) * 2),
        name="mixer_b",
    )(zb3, zb3, zb3, zb3, zb3, zb3, zb3, bias_var)
    return o.reshape(B * S, B_W)


C_TQ = 512
C_TK = 512
C_UNIT = 512


def _mixer_c_kernel(q_ref, k_ref, vt_ref, o_ref, qcat_ref, m_ref, acc_ref, s_ref, *, n_kv):
    m_ref[...] = jnp.full(m_ref.shape, -jnp.inf, jnp.float32)
    acc_ref[...] = jnp.zeros(acc_ref.shape, jnp.float32)
    for g in range(C_GROUP):
        qcat_ref[:, g * C_TQ:(g + 1) * C_TQ] = q_ref[0, g]
    n_units = C_GROUP * C_TQ // C_UNIT

    def keys(j):
        return k_ref[0, 0, pl.ds(pl.multiple_of(j * C_TK, C_TK), C_TK), :]

    def scores(k, u):
        return _dot(k, qcat_ref[:, u * C_UNIT:(u + 1) * C_UNIT])

    s_ref[...] = scores(keys(0), 0)

    def step(j, carry):
        k = keys(j)
        vt = vt_ref[0, 0, :, pl.ds(pl.multiple_of(j * C_TK, C_TK), C_TK)]
        s_next = s_ref[...]
        for u in range(n_units):
            s = s_next
            if u + 1 < n_units:
                s_next = scores(k, u + 1)
            else:
                s_ref[...] = scores(keys(jnp.minimum(j + 1, n_kv - 1)), 0)
            cols = slice(u * C_UNIT, (u + 1) * C_UNIT)
            m_prev = m_ref[:, cols]
            m_new = jnp.maximum(m_prev, jnp.max(s, axis=0, keepdims=True))
            alpha = jnp.exp2(m_prev - m_new)
            p = jnp.exp2(s - m_new).astype(_MXU_DTYPE)
            acc_ref[:, cols] = alpha * acc_ref[:, cols] + _dot(vt, p)
            m_ref[:, cols] = m_new
        return carry

    lax.fori_loop(0, n_kv, step, 0)
    o_t = jnp.concatenate(
        [acc_ref[:HEAD_DIM, g * C_TQ:(g + 1) * C_TQ] / acc_ref[HEAD_DIM:HEAD_DIM + 1, g * C_TQ:(g + 1) * C_TQ]
         for g in range(C_GROUP)], axis=0)
    o_ref[0] = o_t.T.astype(o_ref.dtype)


def _mixer_c(qc_t, kc, vc_t, B, S):
    o = pl.pallas_call(
        functools.partial(_mixer_c_kernel, n_kv=S // C_TK),
        grid=(B, C_KV_HEADS, S // C_TQ),
        in_specs=[
            pl.BlockSpec((1, C_GROUP, HEAD_DIM, C_TQ), lambda b, kv, i: (b, kv, 0, i)),
            pl.BlockSpec((1, 1, S, HEAD_DIM), lambda b, kv, i: (b, kv, 0, 0)),
            pl.BlockSpec((1, 1, C_VT_ROWS, S), lambda b, kv, i: (b, kv, 0, 0)),
        ],
        out_specs=pl.BlockSpec((1, C_TQ, C_GROUP * HEAD_DIM), lambda b, kv, i: (b, i, kv)),
        out_shape=jax.ShapeDtypeStruct((B, S, C_QW), _MXU_DTYPE),
        scratch_shapes=[pltpu.VMEM((HEAD_DIM, C_GROUP * C_TQ), _MXU_DTYPE),
                        pltpu.VMEM((1, C_GROUP * C_TQ), jnp.float32),
                        pltpu.VMEM((C_VT_ROWS, C_GROUP * C_TQ), jnp.float32),
                        pltpu.VMEM((C_TK, C_UNIT), jnp.float32)],
        compiler_params=_params(("arbitrary",) * 3),
        name="mixer_c",
    )(qc_t, kc, vc_t)
    return o.reshape(B * S, C_QW)


MERGE_TM = 512


def _merge_kernel(x_ref, g_ref, oa0_ref, oa1_ref, oa2_ref, l0_ref, l1_ref, l2_ref, ob_ref, oc_ref,
                  wg_ref, pa_ref, pb_ref, pc_ref, wo_ref, out_ref):
    x = x_ref[...]
    h = _rms(x, g_ref[...]).astype(_MXU_DTYPE)
    l0, l1, l2 = l0_ref[...], l1_ref[...], l2_ref[...]
    mx = jnp.maximum(jnp.maximum(l0, l1), l2)
    w0, w1, w2 = jnp.exp(l0 - mx), jnp.exp(l1 - mx), jnp.exp(l2 - mx)
    o_a = (w0 * oa0_ref[...] + w1 * oa1_ref[...] + w2 * oa2_ref[...]) / (w0 + w1 + w2)
    merged = jax.nn.sigmoid(_dot(h, wg_ref[:, 0:D_MODEL])) * _dot(o_a.astype(_MXU_DTYPE), pa_ref[...])
    merged += jax.nn.sigmoid(_dot(h, wg_ref[:, D_MODEL:2 * D_MODEL])) * _dot(ob_ref[...], pb_ref[...])
    merged += jax.nn.sigmoid(_dot(h, wg_ref[:, 2 * D_MODEL:3 * D_MODEL])) * _dot(oc_ref[...], pc_ref[...])
    out_ref[...] = x + _dot(merged.astype(_MXU_DTYPE), wo_ref[...])


def _merge(x2, g1, oa, lse, ob, oc, wg, pa, pb, pc, wo):
    M = x2.shape[0]
    tm = MERGE_TM
    row = lambda i: (i, 0)
    tile = lambda w: pl.BlockSpec((tm, w), row)
    return pl.pallas_call(
        _merge_kernel,
        grid=(M // tm,),
        in_specs=[tile(D_MODEL), _const_spec((1, D_MODEL)),
                  tile(A_OUT), tile(A_OUT), tile(A_OUT), tile(A_OUT), tile(A_OUT), tile(A_OUT),
                  tile(B_W), tile(C_QW),
                  _const_spec(wg.shape), _const_spec(pa.shape), _const_spec(pb.shape),
                  _const_spec(pc.shape), _const_spec(wo.shape)],
        out_specs=tile(D_MODEL),
        out_shape=jax.ShapeDtypeStruct((M, D_MODEL), jnp.float32),
        compiler_params=_params(("arbitrary",)),
        name="gated_merge",
    )(x2, g1, oa[0], oa[1], oa[2], lse[0], lse[1], lse[2], ob, oc, wg, pa, pb, pc, wo)


FFN_TM = 512
FFN_CHUNK = 256


def _ffn_kernel(x_ref, g_ref, wup_ref, wdown_ref, out_ref, act_ref):
    x = x_ref[...]
    xn = _rms(x, g_ref[...]).astype(_MXU_DTYPE)
    for c in range(D_FF // FFN_CHUNK):
        cs = slice(c * FFN_CHUNK, (c + 1) * FFN_CHUNK)
        a = _dot(xn, wup_ref[:, cs])
        b = _dot(xn, wup_ref[:, D_FF + c * FFN_CHUNK:D_FF + (c + 1) * FFN_CHUNK])
        act_ref[:, cs] = (a * jax.nn.sigmoid(a) * b).astype(act_ref.dtype)
    out_ref[...] = x + _dot(act_ref[...], wdown_ref[...])


def _ffn(x2, g2, wup, wdown):
    M = x2.shape[0]
    tm = FFN_TM
    row = lambda i: (i, 0)
    return pl.pallas_call(
        _ffn_kernel,
        grid=(M // tm,),
        in_specs=[pl.BlockSpec((tm, D_MODEL), row), _const_spec((1, D_MODEL)),
                  _const_spec(wup.shape), _const_spec(wdown.shape)],
        out_specs=pl.BlockSpec((tm, D_MODEL), row),
        out_shape=jax.ShapeDtypeStruct((M, D_MODEL), jnp.float32),
        scratch_shapes=[pltpu.VMEM((tm, D_FF), _MXU_DTYPE)],
        compiler_params=_params(("arbitrary",)),
        name="swiglu_ffn",
    )(x2, g2, wup, wdown)


def _t5_bucket(rel):
    half = T5_BUCKETS // 2
    max_exact = half // 2
    ret = jnp.where(rel > 0, half, 0)
    n = jnp.abs(rel)
    nf = jnp.maximum(n, 1).astype(jnp.float32)
    large = max_exact + (jnp.log(nf / max_exact) / math.log(T5_MAX_DIST / max_exact)
                         * (half - max_exact)).astype(jnp.int32)
    large = jnp.minimum(large, half - 1)
    return ret + jnp.where(n < max_exact, n, large)


def _mixer_a_bias(table_g, rate):
    i = jnp.arange(A_SUB)[:, None]
    j = jnp.arange(A_KEYS)[None, :]
    step = j - A_RADIUS - i
    bias = jnp.transpose(table_g[_t5_bucket(step * rate)], (2, 0, 1)).astype(jnp.float32)
    return jnp.where((jnp.abs(step) <= A_RADIUS)[None], bias, NEG)


def _mixer_b_bias(rpb):
    c = np.arange(GRID_W)
    c0 = np.clip(c - B_WIN_COLS // 2, 0, GRID_W - B_WIN_COLS)
    col_ok = (c[None, :] >= c0[:, None]) & (c[None, :] < c0[:, None] + B_WIN_COLS)
    dc = np.clip(c[None, :] - c[:, None] + B_WIN_COLS - 1, 0, 2 * B_WIN_COLS - 2)
    full = jnp.where(col_ok[None, None], rpb[:, :, dc].astype(jnp.float32), NEG)
    dr = np.arange(B_WIN_ROWS)[None, :] - np.arange(B_WIN_ROWS)[:, None] + B_WIN_ROWS - 1
    var = full[:, dr]
    var = jnp.transpose(var, (1, 0, 3, 2, 4))
    return var.reshape(B_WIN_ROWS, B_HEADS, GRID_W, B_WIN_ROWS * GRID_W)


def _rope_tables(S):
    t = jnp.arange(S)
    inv = ROPE_THETA ** (-jnp.arange(0, ROPE_AXIS_DIM, 2, dtype=jnp.float32) / ROPE_AXIS_DIM)
    ang_r = (t // GRID_W).astype(jnp.float32)[:, None] * inv[None, :]
    ang_c = (t % GRID_W).astype(jnp.float32)[:, None] * inv[None, :]
    d = np.arange(LANES) % HEAD_DIM
    f = d % (ROPE_AXIS_DIM // 2)
    is_col = (d >= ROPE_AXIS_DIM)[None, :]
    ang = jnp.where(is_col, ang_c[:, f], ang_r[:, f])
    first = ((d % ROPE_AXIS_DIM) < ROPE_AXIS_DIM // 2)[None, :]
    sin = jnp.sin(ang)
    return jnp.cos(ang), jnp.where(first, -sin, 0.0), jnp.where(first, 0.0, sin)


def _block_diag_ones():
    i = np.arange(256) // HEAD_DIM
    return jnp.asarray(i[:, None] == i[None, :], _MXU_DTYPE)


def kernel(x, rel_bias_table, norm1, w_in, qk_gain, nat_rpb, w_br_a, w_br_b, w_br_c, w_o,
           norm2, w_up, w_down):
    B, S, D = x.shape
    depth = w_in.shape[0]
    cd = _MXU_DTYPE
    x2 = x.reshape(B * S, D)
    rope = _rope_tables(S)
    ones_bd = _block_diag_ones()
    a_bias = [_mixer_a_bias(rel_bias_table[:, g * A_HEADS:(g + 1) * A_HEADS], rate)
              for g, (_, rate) in enumerate(A_PATTERNS)]
    sec = lambda w, lo, n: w[:, lo:lo + n]
    for l in range(depth):
        w = w_in[l]
        w_qkv = jnp.concatenate(
            [sec(w, _QA, A_W), sec(w, _KA, A_W), sec(w, _QB, B_W), sec(w, _KB, B_W),
             sec(w, _QC, C_QW), sec(w, _KC, C_KVW),
             sec(w, _VA, A_W), sec(w, _VB, B_W), sec(w, _VC, C_KVW)], axis=1).astype(cd)
        gq = qk_gain[l]
        tile = lambda g, n, s: jnp.tile(g * s, n)
        gain_row = jnp.concatenate([
            tile(gq[0], A_W // HEAD_DIM, QK_SCALE), tile(gq[1], A_W // HEAD_DIM, 1.0),
            tile(gq[2], B_HEADS, QK_SCALE), tile(gq[3], B_HEADS, 1.0),
            tile(gq[4], C_Q_HEADS, QK_SCALE * LOG2_E), tile(gq[5], C_KV_HEADS, 1.0)])[None, :]
        g1 = norm1[l][None, :]
        za, zb, qc, kc, vc = _qkv_projection(x2, g1, w_qkv, gain_row, ones_bd, rope, B, S)
        oa, lse = [], []
        for g, (_, rate) in enumerate(A_PATTERNS):
            o_g, l_g = _mixer_a_group(za, a_bias[g], g, rate, B, S)
            oa.append(o_g)
            lse.append(l_g)
        ob = _mixer_b(zb, _mixer_b_bias(nat_rpb[l]), B, S)
        oc = _mixer_c(qc, kc, vc, B, S)
        x2 = _merge(x2, g1, oa, lse, ob, oc, sec(w, _ZG, N_BRANCH * D).astype(cd),
                    w_br_a[l].astype(cd), w_br_b[l].astype(cd), w_br_c[l].astype(cd), w_o[l].astype(cd))
        x2 = _ffn(x2, norm2[l][None, :], w_up[l].astype(cd), w_down[l].astype(cd))
    return x2.reshape(B, S, D)
```

```python
import functools
import math

import jax
import jax.numpy as jnp
import numpy as np
from jax import lax
from jax.experimental import pallas as pl
from jax.experimental.pallas import tpu as pltpu

_MXU_DTYPE = jnp.bfloat16

D_MODEL = 1024
HEAD_DIM = 64
GRID_W = 64
RMS_EPS = 1e-6
NEG = -1e30
A_PATTERNS = ((128, 1), (512, 4), (2048, 16))
A_GROUPS = 3
A_HEADS = 4
A_W = A_GROUPS * A_HEADS * HEAD_DIM
A_OUT = A_HEADS * HEAD_DIM
A_RADIUS = 64
B_HEADS = 8
B_W = B_HEADS * HEAD_DIM
B_WIN_ROWS = 8
B_WIN_COLS = 16
C_Q_HEADS = 8
C_KV_HEADS = 2
C_GROUP = C_Q_HEADS // C_KV_HEADS
C_QW = C_Q_HEADS * HEAD_DIM
C_KVW = C_KV_HEADS * HEAD_DIM
ROPE_THETA = 10000.0
ROPE_AXIS_DIM = HEAD_DIM // 2
T5_BUCKETS = 32
T5_MAX_DIST = 1024
N_BRANCH = 3
D_FF = math.ceil(8 * D_MODEL / 3 / 256) * 256
QK_SCALE = HEAD_DIM ** -0.5
LOG2_E = math.log2(math.e)

VMEM_LIMIT_BYTES = 56 * 1024 * 1024
LANES = 128
BF16_SUBLANES = 16
VT_ROWS = HEAD_DIM + BF16_SUBLANES

_OFF = np.cumsum([0, A_W, A_W, A_W, B_W, B_W, B_W, C_QW, C_KVW, C_KVW]).tolist()
(_QA, _KA, _VA, _QB, _KB, _VB, _QC, _KC, _VC, _ZG) = _OFF

ZA_W = 3 * A_OUT
B_PAIRS = B_HEADS // 2


def _params(sem):
    return pltpu.CompilerParams(dimension_semantics=sem, vmem_limit_bytes=VMEM_LIMIT_BYTES)


def _const_spec(shape):
    nd = len(shape)
    return pl.BlockSpec(shape, lambda *_: (0,) * nd, pipeline_mode=pl.Buffered(1))


def _rms(x, g):
    return x * lax.rsqrt(jnp.mean(x * x, axis=-1, keepdims=True) + RMS_EPS) * g


def _dot(a, b):
    return jnp.dot(a, b, preferred_element_type=jnp.float32)


def _dot_nt(a, b):
    return lax.dot_general(a, b, (((1,), (1,)), ((), ())), preferred_element_type=jnp.float32)


def _ones_tail(width, dtype):
    return (lax.broadcasted_iota(jnp.int32, (VT_ROWS - HEAD_DIM, width), 0) == 0).astype(dtype)


PROJ_TM = 512
_N_NORM = 2 * A_W + 2 * B_W
_N_ROPE = C_QW + C_KVW
_N_PLAIN = A_W + B_W + C_KVW


def _proj_kernel(x_ref, g_ref, w_ref, gain_ref, ones_ref, cos_ref, s1_ref, s2_ref,
                 za0_ref, za1_ref, za2_ref, qbt_ref, kb_ref, vbt_ref, qc_ref, kc_ref, vc_ref, dil_ref):
    tm = x_ref.shape[0]
    h = _rms(x_ref[...], g_ref[...]).astype(_MXU_DTYPE)
    za_refs = (za0_ref, za1_ref, za2_ref)

    def head_norm(acc, c0, width):
        sq = (acc * acc).astype(_MXU_DTYPE)
        ms = _dot(sq, ones_ref[:width, :width]) * (1.0 / HEAD_DIM)
        return acc * lax.rsqrt(ms + RMS_EPS) * gain_ref[:, c0:c0 + width]

    def store_group(g, section, val):
        cols = slice(section * A_OUT, (section + 1) * A_OUT)
        rate = A_PATTERNS[g][1]
        if rate == 1:
            za0_ref[:, cols] = val.astype(za0_ref.dtype)
            return
        for j in range(A_OUT // LANES):
            dil_ref[j] = val[:, j * LANES:(j + 1) * LANES]
        for r in range(rate):
            picked = [dil_ref[j, pl.ds(r, tm // rate, stride=rate), :] for j in range(A_OUT // LANES)]
            za_refs[g][0, r, :, cols] = jnp.concatenate(picked, axis=1).astype(za_refs[g].dtype)

    for section in range(2):
        for g in range(A_GROUPS):
            wc = section * A_W + g * A_OUT
            store_group(g, section, head_norm(_dot(h, w_ref[:, wc:wc + A_OUT]), wc, A_OUT))
    for t in range(B_W // 256):
        wc = 2 * A_W + t * 256
        yt = head_norm(_dot(h, w_ref[:, wc:wc + 256]), wc, 256).T.astype(qbt_ref.dtype)
        qbt_ref[0, 2 * t] = yt[:LANES]
        qbt_ref[0, 2 * t + 1] = yt[LANES:]
    for t in range(B_W // 256):
        wc = 2 * A_W + B_W + t * 256
        kb_ref[:, t * 256:(t + 1) * 256] = head_norm(_dot(h, w_ref[:, wc:wc + 256]), wc, 256).astype(kb_ref.dtype)

    cos = cos_ref[...]
    s1 = s1_ref[...]
    s2 = s2_ref[...]
    for t in range(_N_ROPE // LANES):
        wc = _N_NORM + t * LANES
        y = head_norm(_dot(h, w_ref[:, wc:wc + LANES]), wc, LANES)
        y = y * cos + pltpu.roll(y, LANES - 16, 1) * s1 + pltpu.roll(y, 16, 1) * s2
        if t < C_QW // LANES:
            yt = y.T.astype(qc_ref.dtype)
            qc_ref[0, 2 * t] = yt[:HEAD_DIM]
            qc_ref[0, 2 * t + 1] = yt[HEAD_DIM:]
        else:
            y = y.astype(kc_ref.dtype)
            kc_ref[0, 0] = y[:, :HEAD_DIM]
            kc_ref[0, 1] = y[:, HEAD_DIM:]

    base = _N_NORM + _N_ROPE
    for g in range(A_GROUPS):
        store_group(g, 2, _dot(h, w_ref[:, base + g * A_OUT:base + (g + 1) * A_OUT]))
    for t in range(B_W // 256):
        wc = base + A_W + t * 256
        vt = _dot(h, w_ref[:, wc:wc + 256]).T.astype(vbt_ref.dtype)
        vbt_ref[0, 2 * t] = vt[:LANES]
        vbt_ref[0, 2 * t + 1] = vt[LANES:]
    wc = base + A_W + B_W
    acc_t = _dot(h, w_ref[:, wc:wc + LANES]).T.astype(vc_ref.dtype)
    tail = _ones_tail(tm, vc_ref.dtype)
    for kv in range(C_KV_HEADS):
        vc_ref[0, kv, :HEAD_DIM] = acc_t[kv * HEAD_DIM:(kv + 1) * HEAD_DIM]
        vc_ref[0, kv, HEAD_DIM:] = tail


def _qkv_projection(x2, g1, w_qkv, gain_row, ones_bd, rope, B, S):
    M = x2.shape[0]
    tm = PROJ_TM
    n_w = w_qkv.shape[1]
    per_seq = S // tm
    cos_t, s1_t, s2_t = rope
    r1, r2 = A_PATTERNS[1][1], A_PATTERNS[2][1]
    row = lambda i: (i, 0)
    pos = lambda i: (i % per_seq, 0)
    hm = lambda i: (i // per_seq, 0, i % per_seq, 0)
    hm_t = lambda i: (i // per_seq, 0, 0, i % per_seq)
    cd = _MXU_DTYPE
    return pl.pallas_call(
        _proj_kernel,
        grid=(M // tm,),
        in_specs=[
            pl.BlockSpec((tm, D_MODEL), row),
            _const_spec((1, D_MODEL)),
            _const_spec((D_MODEL, n_w)),
            _const_spec((1, _N_NORM + _N_ROPE)),
            _const_spec((256, 256)),
            pl.BlockSpec((tm, LANES), pos),
            pl.BlockSpec((tm, LANES), pos),
            pl.BlockSpec((tm, LANES), pos),
        ],
        out_specs=[
            pl.BlockSpec((tm, ZA_W), row),
            pl.BlockSpec((1, r1, tm // r1, ZA_W), hm),
            pl.BlockSpec((1, r2, tm // r2, ZA_W), hm),
            pl.BlockSpec((1, B_PAIRS, LANES, tm), hm_t),
            pl.BlockSpec((tm, B_W), row),
            pl.BlockSpec((1, B_PAIRS, LANES, tm), hm_t),
            pl.BlockSpec((1, C_Q_HEADS, HEAD_DIM, tm), hm_t),
            pl.BlockSpec((1, C_KV_HEADS, tm, HEAD_DIM), hm),
            pl.BlockSpec((1, C_KV_HEADS, VT_ROWS, tm), hm_t),
        ],
        out_shape=[
            jax.ShapeDtypeStruct((M, ZA_W), cd),
            jax.ShapeDtypeStruct((B, r1, S // r1, ZA_W), cd),
            jax.ShapeDtypeStruct((B, r2, S // r2, ZA_W), cd),
            jax.ShapeDtypeStruct((B, B_PAIRS, LANES, S), cd),
            jax.ShapeDtypeStruct((M, B_W), cd),
            jax.ShapeDtypeStruct((B, B_PAIRS, LANES, S), cd),
            jax.ShapeDtypeStruct((B, C_Q_HEADS, HEAD_DIM, S), cd),
            jax.ShapeDtypeStruct((B, C_KV_HEADS, S, HEAD_DIM), cd),
            jax.ShapeDtypeStruct((B, C_KV_HEADS, VT_ROWS, S), cd),
        ],
        scratch_shapes=[pltpu.VMEM((A_OUT // LANES, tm, LANES), jnp.float32)],
        compiler_params=_params(("arbitrary",)),
        name="qkv_projection",
    )(x2, g1, w_qkv, gain_row, ones_bd, cos_t, s1_t, s2_t)


A_SUB = 2 * A_RADIUS
A_KEYS = 4 * A_RADIUS


def _mixer_a_kernel(q_ref, kp_ref, ko_ref, kn_ref, vp_ref, vo_ref, vn_ref, bias_ref,
                    o_ref, lse_ref, *, tl, seq_len):
    l0 = pl.program_id(2) * tl
    q = q_ref[0, 0]
    kw = jnp.concatenate([kp_ref[0, 0], ko_ref[0, 0], kn_ref[0, 0]], axis=0)
    vw = jnp.concatenate([vp_ref[0, 0], vo_ref[0, 0], vn_ref[0, 0]], axis=0)
    n_sub = tl // A_SUB
    col = lax.broadcasted_iota(jnp.int32, (A_SUB, A_KEYS), 1)
    for h in range(A_HEADS):
        hs = slice(h * HEAD_DIM, (h + 1) * HEAD_DIM)
        bias = bias_ref[h]
        for sb in range(n_sub):
            i0 = sb * A_SUB
            s = _dot_nt(q[i0:i0 + A_SUB, hs], kw[i0:i0 + A_KEYS, hs]) + bias
            if sb == 0 or sb == n_sub - 1:
                kpos = col + (l0 + i0 - A_RADIUS)
                s = jnp.where((kpos >= 0) & (kpos < seq_len), s, NEG)
            m = jnp.max(s, axis=-1, keepdims=True)
            e = jnp.exp(s - m)
            den = jnp.sum(e, axis=-1, keepdims=True)
            o = _dot(e.astype(_MXU_DTYPE), vw[i0:i0 + A_KEYS, hs]) / den
            o_ref[0, 0, i0:i0 + A_SUB, hs] = o
            lse_ref[0, 0, i0:i0 + A_SUB, hs] = jnp.broadcast_to(m + jnp.log(den), (A_SUB, HEAD_DIM))


def _mixer_a_group(za_g, bias, rate, B, S):
    L = S // rate
    tl = min(512, L)
    nblk = L // A_RADIUS
    per = tl // A_RADIUS

    def own(section):
        return lambda b, r, l: (b, r, l, section)

    def prev(section):
        return lambda b, r, l: (b, r, jnp.maximum(l * per - 1, 0), section)

    def nxt(section):
        return lambda b, r, l: (b, r, jnp.minimum((l + 1) * per, nblk - 1), section)

    edge = (1, 1, A_RADIUS, A_OUT)
    full = (1, 1, tl, A_OUT)
    return pl.pallas_call(
        functools.partial(_mixer_a_kernel, tl=tl, seq_len=L),
        grid=(B, rate, L // tl),
        in_specs=[
            pl.BlockSpec(full, own(0)),
            pl.BlockSpec(edge, prev(1)), pl.BlockSpec(full, own(1)), pl.BlockSpec(edge, nxt(1)),
            pl.BlockSpec(edge, prev(2)), pl.BlockSpec(full, own(2)), pl.BlockSpec(edge, nxt(2)),
            _const_spec((A_HEADS, A_SUB, A_KEYS)),
        ],
        out_specs=[pl.BlockSpec(full, own(0)), pl.BlockSpec(full, own(0))],
        out_shape=[jax.ShapeDtypeStruct((B, rate, L, A_OUT), jnp.float32)] * 2,
        compiler_params=_params(("arbitrary",) * 3),
        name=f"mixer_a_rate{rate}",
    )(za_g, za_g, za_g, za_g, za_g, za_g, za_g, bias)


B_UNIT_ROWS = 4
B_UNIT = B_UNIT_ROWS * GRID_W
B_SPAN_ROWS = B_UNIT_ROWS + B_WIN_ROWS
B_SPAN = B_SPAN_ROWS * GRID_W
B_HALO = (B_WIN_ROWS // 2) * GRID_W
B_TILE_ROWS = 16
B_TILE = B_TILE_ROWS * GRID_W


def _mixer_b_kernel(q_ref, kp_ref, ko_ref, kn_ref, vp_ref, vo_ref, vn_ref, bias_ref,
                    o_ref, kw_ref, vw_ref, *, rows):
    i0 = pl.program_id(1) * B_TILE_ROWS
    kw_ref[0:B_HALO] = kp_ref[0]
    kw_ref[B_HALO:B_HALO + B_TILE] = ko_ref[0]
    kw_ref[B_HALO + B_TILE:] = kn_ref[0]
    vw_ref[:, :, 0:B_HALO] = vp_ref[0]
    vw_ref[:, :, B_HALO:B_HALO + B_TILE] = vo_ref[0]
    vw_ref[:, :, B_HALO + B_TILE:] = vn_ref[0]
    tail = _ones_tail(B_SPAN, _MXU_DTYPE)
    upper = lax.broadcasted_iota(jnp.int32, (LANES, B_UNIT), 0) < HEAD_DIM

    def unit(u, carry):
        i0u = i0 + u * B_UNIT_ROWS
        r0 = jnp.clip(i0u - B_WIN_ROWS // 2, 0, rows - B_SPAN_ROWS)
        off = pl.multiple_of((r0 - (i0 - B_WIN_ROWS // 2)) * GRID_W, LANES)
        variant = jnp.where(i0u == 0, 0, jnp.where(i0u == rows - B_UNIT_ROWS, 2, 1))
        qoff = pl.multiple_of(u * B_UNIT, B_UNIT)

        def scores(h):
            pair, odd = divmod(h, 2)
            qt = q_ref[0, pair, :, pl.ds(qoff, B_UNIT)]
            qt = jnp.where(upper != bool(odd), qt, jnp.zeros_like(qt))
            return _dot(kw_ref[pl.ds(off, B_SPAN), pair * LANES:(pair + 1) * LANES], qt)

        outs = []
        s_next = scores(0)
        for h in range(B_HEADS):
            pair, odd = divmod(h, 2)
            s = s_next + bias_ref[variant, h]
            if h + 1 < B_HEADS:
                s_next = scores(h + 1)
            p = jnp.exp2(s - jnp.max(s, axis=0, keepdims=True)).astype(_MXU_DTYPE)
            vt = vw_ref[pair, odd * HEAD_DIM:(odd + 1) * HEAD_DIM, pl.ds(off, B_SPAN)]
            acc = _dot(jnp.concatenate([vt, tail], axis=0), p)
            outs.append(acc[:HEAD_DIM] / acc[HEAD_DIM:HEAD_DIM + 1])
        o_ref[0, pl.ds(qoff, B_UNIT), :] = jnp.concatenate(outs, axis=0).T.astype(o_ref.dtype)
        return carry

    lax.fori_loop(0, B_TILE_ROWS // B_UNIT_ROWS, unit, 0)


def _mixer_b(qbt, kb, vbt, bias, B, S):
    rows = S // GRID_W
    nt = rows // B_TILE_ROWS
    per = B_TILE // B_HALO
    nh = S // B_HALO
    kb3 = kb.reshape(B, S, B_W)
    prev = lambda t: jnp.maximum(t * per - 1, 0)
    nxt = lambda t: jnp.minimum((t + 1) * per, nh - 1)
    k_edge, k_own = (1, B_HALO, B_W), (1, B_TILE, B_W)
    t_edge, t_own = (1, B_PAIRS, LANES, B_HALO), (1, B_PAIRS, LANES, B_TILE)
    o = pl.pallas_call(
        functools.partial(_mixer_b_kernel, rows=rows),
        grid=(B, nt),
        in_specs=[
            pl.BlockSpec(t_own, lambda b, t: (b, 0, 0, t)),
            pl.BlockSpec(k_edge, lambda b, t: (b, prev(t), 0)),
            pl.BlockSpec(k_own, lambda b, t: (b, t, 0)),
            pl.BlockSpec(k_edge, lambda b, t: (b, nxt(t), 0)),
            pl.BlockSpec(t_edge, lambda b, t: (b, 0, 0, prev(t))),
            pl.BlockSpec(t_own, lambda b, t: (b, 0, 0, t)),
            pl.BlockSpec(t_edge, lambda b, t: (b, 0, 0, nxt(t))),
            _const_spec(bias.shape),
        ],
        out_specs=pl.BlockSpec(k_own, lambda b, t: (b, t, 0)),
        out_shape=jax.ShapeDtypeStruct((B, S, B_W), _MXU_DTYPE),
        scratch_shapes=[pltpu.VMEM((B_TILE + 2 * B_HALO, B_W), _MXU_DTYPE),
                        pltpu.VMEM((B_PAIRS, LANES, B_TILE + 2 * B_HALO), _MXU_DTYPE)],
        compiler_params=_params(("arbitrary",) * 2),
        name="mixer_b",
    )(qbt, kb3, kb3, kb3, vbt, vbt, vbt, bias)
    return o.reshape(B * S, B_W)


C_TQ = 512
C_TK = 512
C_UNIT = 512


def _mixer_c_kernel(q_ref, k_ref, vt_ref, o_ref, qcat_ref, m_ref, acc_ref, s_ref, *, n_kv):
    m_ref[...] = jnp.full(m_ref.shape, -jnp.inf, jnp.float32)
    acc_ref[...] = jnp.zeros(acc_ref.shape, jnp.float32)
    for g in range(C_GROUP):
        qcat_ref[:, g * C_TQ:(g + 1) * C_TQ] = q_ref[0, g]
    n_units = C_GROUP * C_TQ // C_UNIT

    def keys(j):
        return k_ref[0, 0, pl.ds(pl.multiple_of(j * C_TK, C_TK), C_TK), :]

    def scores(k, u):
        return _dot(k, qcat_ref[:, u * C_UNIT:(u + 1) * C_UNIT])

    s_ref[...] = scores(keys(0), 0)

    def step(j, carry):
        k = keys(j)
        vt = vt_ref[0, 0, :, pl.ds(pl.multiple_of(j * C_TK, C_TK), C_TK)]
        s_next = s_ref[...]
        for u in range(n_units):
            s = s_next
            if u + 1 < n_units:
                s_next = scores(k, u + 1)
            else:
                s_ref[...] = scores(keys(jnp.minimum(j + 1, n_kv - 1)), 0)
            cols = slice(u * C_UNIT, (u + 1) * C_UNIT)
            m_prev = m_ref[:, cols]
            m_new = jnp.maximum(m_prev, jnp.max(s, axis=0, keepdims=True))
            alpha = jnp.exp2(m_prev - m_new)
            p = jnp.exp2(s - m_new).astype(_MXU_DTYPE)
            acc_ref[:, cols] = alpha * acc_ref[:, cols] + _dot(vt, p)
            m_ref[:, cols] = m_new
        return carry

    lax.fori_loop(0, n_kv, step, 0)
    o_t = jnp.concatenate(
        [acc_ref[:HEAD_DIM, g * C_TQ:(g + 1) * C_TQ] / acc_ref[HEAD_DIM:HEAD_DIM + 1, g * C_TQ:(g + 1) * C_TQ]
         for g in range(C_GROUP)], axis=0)
    o_ref[0] = o_t.T.astype(o_ref.dtype)


def _mixer_c(qc_t, kc, vc_t, B, S):
    o = pl.pallas_call(
        functools.partial(_mixer_c_kernel, n_kv=S // C_TK),
        grid=(B, C_KV_HEADS, S // C_TQ),
        in_specs=[
            pl.BlockSpec((1, C_GROUP, HEAD_DIM, C_TQ), lambda b, kv, i: (b, kv, 0, i)),
            pl.BlockSpec((1, 1, S, HEAD_DIM), lambda b, kv, i: (b, kv, 0, 0)),
            pl.BlockSpec((1, 1, VT_ROWS, S), lambda b, kv, i: (b, kv, 0, 0)),
        ],
        out_specs=pl.BlockSpec((1, C_TQ, C_GROUP * HEAD_DIM), lambda b, kv, i: (b, i, kv)),
        out_shape=jax.ShapeDtypeStruct((B, S, C_QW), _MXU_DTYPE),
        scratch_shapes=[pltpu.VMEM((HEAD_DIM, C_GROUP * C_TQ), _MXU_DTYPE),
                        pltpu.VMEM((1, C_GROUP * C_TQ), jnp.float32),
                        pltpu.VMEM((VT_ROWS, C_GROUP * C_TQ), jnp.float32),
                        pltpu.VMEM((C_TK, C_UNIT), jnp.float32)],
        compiler_params=_params(("arbitrary",) * 3),
        name="mixer_c",
    )(qc_t, kc, vc_t)
    return o.reshape(B * S, C_QW)


MERGE_TM = 512


def _merge_kernel(x_ref, g_ref, oa0_ref, oa1_ref, oa2_ref, l0_ref, l1_ref, l2_ref, ob_ref, oc_ref,
                  wg_ref, pa_ref, pb_ref, pc_ref, wo_ref, out_ref, *scratch):
    tm = x_ref.shape[0]

    def token_major(ref, scr):
        rate = ref.shape[1]
        halves = range(A_OUT // LANES)
        for r in range(rate):
            for j in halves:
                scr[j, pl.ds(r, tm // rate, stride=rate), :] = ref[0, r, :, j * LANES:(j + 1) * LANES]
        return jnp.concatenate([scr[j] for j in halves], axis=1)

    x = x_ref[...]
    h = _rms(x, g_ref[...]).astype(_MXU_DTYPE)
    oa0, l0 = oa0_ref[...], l0_ref[...]
    oa1, l1 = token_major(oa1_ref, scratch[0]), token_major(l1_ref, scratch[1])
    oa2, l2 = token_major(oa2_ref, scratch[2]), token_major(l2_ref, scratch[3])
    mx = jnp.maximum(jnp.maximum(l0, l1), l2)
    w0, w1, w2 = jnp.exp(l0 - mx), jnp.exp(l1 - mx), jnp.exp(l2 - mx)
    o_a = (w0 * oa0 + w1 * oa1 + w2 * oa2) / (w0 + w1 + w2)
    merged = jax.nn.sigmoid(_dot(h, wg_ref[:, 0:D_MODEL])) * _dot(o_a.astype(_MXU_DTYPE), pa_ref[...])
    merged += jax.nn.sigmoid(_dot(h, wg_ref[:, D_MODEL:2 * D_MODEL])) * _dot(ob_ref[...], pb_ref[...])
    merged += jax.nn.sigmoid(_dot(h, wg_ref[:, 2 * D_MODEL:3 * D_MODEL])) * _dot(oc_ref[...], pc_ref[...])
    out_ref[...] = x + _dot(merged.astype(_MXU_DTYPE), wo_ref[...])


def _merge(x2, g1, oa, lse, ob, oc, wg, pa, pb, pc, wo, B, S):
    M = x2.shape[0]
    tm = MERGE_TM
    per_seq = S // tm
    row = lambda i: (i, 0)
    tile = lambda w: pl.BlockSpec((tm, w), row)

    def dilated(rate):
        return pl.BlockSpec((1, rate, tm // rate, A_OUT), lambda i: (i // per_seq, 0, i % per_seq, 0))

    r1, r2 = A_PATTERNS[1][1], A_PATTERNS[2][1]
    return pl.pallas_call(
        _merge_kernel,
        grid=(M // tm,),
        in_specs=[tile(D_MODEL), _const_spec((1, D_MODEL)),
                  tile(A_OUT), dilated(r1), dilated(r2), tile(A_OUT), dilated(r1), dilated(r2),
                  tile(B_W), tile(C_QW),
                  _const_spec(wg.shape), _const_spec(pa.shape), _const_spec(pb.shape),
                  _const_spec(pc.shape), _const_spec(wo.shape)],
        out_specs=tile(D_MODEL),
        out_shape=jax.ShapeDtypeStruct((M, D_MODEL), jnp.float32),
        scratch_shapes=[pltpu.VMEM((A_OUT // LANES, tm, LANES), jnp.float32)] * 4,
        compiler_params=_params(("arbitrary",)),
        name="gated_merge",
    )(x2, g1, oa[0].reshape(M, A_OUT), oa[1], oa[2], lse[0].reshape(M, A_OUT), lse[1], lse[2],
      ob, oc, wg, pa, pb, pc, wo)


FFN_TM = 512
FFN_CHUNK = 256


def _ffn_kernel(x_ref, g_ref, wup_ref, wdown_ref, out_ref, act_ref):
    x = x_ref[...]
    xn = _rms(x, g_ref[...]).astype(_MXU_DTYPE)
    for c in range(D_FF // FFN_CHUNK):
        cs = slice(c * FFN_CHUNK, (c + 1) * FFN_CHUNK)
        a = _dot(xn, wup_ref[:, cs])
        b = _dot(xn, wup_ref[:, D_FF + c * FFN_CHUNK:D_FF + (c + 1) * FFN_CHUNK])
        act_ref[:, cs] = (a * jax.nn.sigmoid(a) * b).astype(act_ref.dtype)
    out_ref[...] = x + _dot(act_ref[...], wdown_ref[...])


def _ffn(x2, g2, wup, wdown):
    M = x2.shape[0]
    tm = FFN_TM
    row = lambda i: (i, 0)
    return pl.pallas_call(
        _ffn_kernel,
        grid=(M // tm,),
        in_specs=[pl.BlockSpec((tm, D_MODEL), row), _const_spec((1, D_MODEL)),
                  _const_spec(wup.shape), _const_spec(wdown.shape)],
        out_specs=pl.BlockSpec((tm, D_MODEL), row),
        out_shape=jax.ShapeDtypeStruct((M, D_MODEL), jnp.float32),
        scratch_shapes=[pltpu.VMEM((tm, D_FF), _MXU_DTYPE)],
        compiler_params=_params(("arbitrary",)),
        name="swiglu_ffn",
    )(x2, g2, wup, wdown)


def _t5_bucket(rel):
    half = T5_BUCKETS // 2
    max_exact = half // 2
    ret = jnp.where(rel > 0, half, 0)
    n = jnp.abs(rel)
    nf = jnp.maximum(n, 1).astype(jnp.float32)
    large = max_exact + (jnp.log(nf / max_exact) / math.log(T5_MAX_DIST / max_exact)
                         * (half - max_exact)).astype(jnp.int32)
    large = jnp.minimum(large, half - 1)
    return ret + jnp.where(n < max_exact, n, large)


def _mixer_a_bias(table_g, rate):
    i = jnp.arange(A_SUB)[:, None]
    j = jnp.arange(A_KEYS)[None, :]
    step = j - A_RADIUS - i
    bias = jnp.transpose(table_g[_t5_bucket(step * rate)], (2, 0, 1)).astype(jnp.float32)
    return jnp.where((jnp.abs(step) <= A_RADIUS)[None], bias, NEG)


def _mixer_b_bias(rpb, rows):
    c = np.arange(GRID_W)
    c0 = np.clip(c - B_WIN_COLS // 2, 0, GRID_W - B_WIN_COLS)
    col_ok = (c[:, None] >= c0[None, :]) & (c[:, None] < c0[None, :] + B_WIN_COLS)
    dc = np.clip(c[:, None] - c[None, :] + B_WIN_COLS - 1, 0, 2 * B_WIN_COLS - 2)
    n_dr = 2 * B_WIN_ROWS - 1
    blocks = jnp.where(col_ok[None, None], rpb[:, :, dc].astype(jnp.float32) * LOG2_E, NEG)
    blocks = jnp.concatenate([blocks, jnp.full((B_HEADS, 1, GRID_W, GRID_W), NEG, jnp.float32)], axis=1)
    first_query_row = np.array([0, B_WIN_ROWS // 2, rows - B_UNIT_ROWS])
    i = first_query_row[:, None, None] + np.arange(B_UNIT_ROWS)[None, None, :]
    r0 = np.clip(first_query_row - B_WIN_ROWS // 2, 0, rows - B_SPAN_ROWS)
    ik = r0[:, None, None] + np.arange(B_SPAN_ROWS)[None, :, None]
    rs = np.clip(i - B_WIN_ROWS // 2, 0, rows - B_WIN_ROWS)
    row_ok = (ik >= rs) & (ik < rs + B_WIN_ROWS)
    sel = np.where(row_ok, ik - i + B_WIN_ROWS - 1, n_dr)
    big = jnp.take(blocks, jnp.asarray(sel.reshape(-1), jnp.int32), axis=1)
    big = big.reshape(B_HEADS, 3, B_SPAN_ROWS, B_UNIT_ROWS, GRID_W, GRID_W)
    big = jnp.transpose(big, (1, 0, 2, 4, 3, 5))
    return big.reshape(3, B_HEADS, B_SPAN, B_UNIT)


def _rope_tables(S):
    t = jnp.arange(S)
    inv = ROPE_THETA ** (-jnp.arange(0, ROPE_AXIS_DIM, 2, dtype=jnp.float32) / ROPE_AXIS_DIM)
    d = np.arange(LANES) % HEAD_DIM
    inv_lane = inv[d % (ROPE_AXIS_DIM // 2)][None, :]
    is_col = (d >= ROPE_AXIS_DIM)[None, :]
    pos = jnp.where(is_col, (t % GRID_W)[:, None], (t // GRID_W)[:, None]).astype(jnp.float32)
    ang = pos * inv_lane
    first = ((d % ROPE_AXIS_DIM) < ROPE_AXIS_DIM // 2)[None, :]
    sin = jnp.sin(ang)
    return jnp.cos(ang), jnp.where(first, -sin, 0.0), jnp.where(first, 0.0, sin)


def _block_diag_ones():
    i = np.arange(256) // HEAD_DIM
    return jnp.asarray(i[:, None] == i[None, :], _MXU_DTYPE)


def kernel(x, rel_bias_table, norm1, w_in, qk_gain, nat_rpb, w_br_a, w_br_b, w_br_c, w_o,
           norm2, w_up, w_down):
    B, S, D = x.shape
    depth = w_in.shape[0]
    cd = _MXU_DTYPE
    M = B * S
    x2 = x.reshape(M, D)
    rope = _rope_tables(S)
    ones_bd = _block_diag_ones()
    a_bias = [_mixer_a_bias(rel_bias_table[:, g * A_HEADS:(g + 1) * A_HEADS], rate)
              for g, (_, rate) in enumerate(A_PATTERNS)]
    sec = lambda w, lo, n: w[:, lo:lo + n]
    for l in range(depth):
        w = w_in[l]
        w_qkv = jnp.concatenate(
            [sec(w, _QA, A_W), sec(w, _KA, A_W), sec(w, _QB, B_W), sec(w, _KB, B_W),
             sec(w, _QC, C_QW), sec(w, _KC, C_KVW),
             sec(w, _VA, A_W), sec(w, _VB, B_W), sec(w, _VC, C_KVW)], axis=1).astype(cd)
        gq = qk_gain[l]
        tile = lambda g, n, s: jnp.tile(g * s, n)
        gain_row = jnp.concatenate([
            tile(gq[0], A_W // HEAD_DIM, QK_SCALE), tile(gq[1], A_W // HEAD_DIM, 1.0),
            tile(gq[2], B_HEADS, QK_SCALE * LOG2_E), tile(gq[3], B_HEADS, 1.0),
            tile(gq[4], C_Q_HEADS, QK_SCALE * LOG2_E), tile(gq[5], C_KV_HEADS, 1.0)])[None, :]
        g1 = norm1[l][None, :]
        za0, za1, za2, qbt, kb, vbt, qc, kc, vc = _qkv_projection(x2, g1, w_qkv, gain_row, ones_bd, rope, B, S)
        za = (za0.reshape(B, 1, S, ZA_W), za1, za2)
        oa, lse = [], []
        for g, (_, rate) in enumerate(A_PATTERNS):
            o_g, l_g = _mixer_a_group(za[g], a_bias[g], rate, B, S)
            oa.append(o_g)
            lse.append(l_g)
        ob = _mixer_b(qbt, kb, vbt, _mixer_b_bias(nat_rpb[l], S // GRID_W), B, S)
        oc = _mixer_c(qc, kc, vc, B, S)
        x2 = _merge(x2, g1, oa, lse, ob, oc, sec(w, _ZG, N_BRANCH * D).astype(cd),
                    w_br_a[l].astype(cd), w_br_b[l].astype(cd), w_br_c[l].astype(cd), w_o[l].astype(cd), B, S)
        x2 = _ffn(x2, norm2[l][None, :], w_up[l].astype(cd), w_down[l].astype(cd))
    return x2.reshape(B, S, D)
```

```python
import functools
import math

import jax
import jax.numpy as jnp
import numpy as np
from jax import lax
from jax.experimental import pallas as pl
from jax.experimental.pallas import tpu as pltpu

_MXU_DTYPE = jnp.bfloat16

D_MODEL = 1024
HEAD_DIM = 64
GRID_W = 64
RMS_EPS = 1e-6
NEG = -1e30
A_PATTERNS = ((128, 1), (512, 4), (2048, 16))
A_GROUPS = 3
A_HEADS = 4
A_W = A_GROUPS * A_HEADS * HEAD_DIM
A_OUT = A_HEADS * HEAD_DIM
A_RADIUS = 64
B_HEADS = 8
B_W = B_HEADS * HEAD_DIM
B_WIN_ROWS = 8
B_WIN_COLS = 16
C_Q_HEADS = 8
C_KV_HEADS = 2
C_GROUP = C_Q_HEADS // C_KV_HEADS
C_QW = C_Q_HEADS * HEAD_DIM
C_KVW = C_KV_HEADS * HEAD_DIM
ROPE_THETA = 10000.0
ROPE_AXIS_DIM = HEAD_DIM // 2
T5_BUCKETS = 32
T5_MAX_DIST = 1024
N_BRANCH = 3
D_FF = math.ceil(8 * D_MODEL / 3 / 256) * 256
QK_SCALE = HEAD_DIM ** -0.5
LOG2_E = math.log2(math.e)

VMEM_LIMIT_BYTES = 56 * 1024 * 1024
LANES = 128
BF16_SUBLANES = 16
VT_ROWS = HEAD_DIM + BF16_SUBLANES

_OFF = np.cumsum([0, A_W, A_W, A_W, B_W, B_W, B_W, C_QW, C_KVW, C_KVW]).tolist()
(_QA, _KA, _VA, _QB, _KB, _VB, _QC, _KC, _VC, _ZG) = _OFF

ZA_W = 3 * A_OUT
B_PAIRS = B_HEADS // 2


def _params(sem):
    return pltpu.CompilerParams(dimension_semantics=sem, vmem_limit_bytes=VMEM_LIMIT_BYTES)


def _const_spec(shape):
    nd = len(shape)
    return pl.BlockSpec(shape, lambda *_: (0,) * nd, pipeline_mode=pl.Buffered(1))


def _rms(x, g):
    return x * lax.rsqrt(jnp.mean(x * x, axis=-1, keepdims=True) + RMS_EPS) * g


def _dot(a, b):
    return jnp.dot(a, b, preferred_element_type=jnp.float32)


def _dot_nt(a, b):
    return lax.dot_general(a, b, (((1,), (1,)), ((), ())), preferred_element_type=jnp.float32)


def _ones_tail(width, dtype):
    return (lax.broadcasted_iota(jnp.int32, (VT_ROWS - HEAD_DIM, width), 0) == 0).astype(dtype)


PROJ_TM = 512
_N_NORM = 2 * A_W + 2 * B_W
_N_ROPE = C_QW + C_KVW
_N_PLAIN = A_W + B_W + C_KVW


def _proj_kernel(x_ref, g_ref, w_ref, gain_ref, ones_ref, cos_ref, s1_ref, s2_ref,
                 za0_ref, za1_ref, za2_ref, qbt_ref, kb_ref, vbt_ref, qc_ref, kc_ref, vc_ref, ksq_ref,
                 dil_ref):
    tm = x_ref.shape[0]
    h = _rms(x_ref[...], g_ref[...]).astype(_MXU_DTYPE)
    za_refs = (za0_ref, za1_ref, za2_ref)

    def head_norm(acc, c0, width):
        sq = (acc * acc).astype(_MXU_DTYPE)
        ms = _dot(sq, ones_ref[:width, :width]) * (1.0 / HEAD_DIM)
        return acc * lax.rsqrt(ms + RMS_EPS) * gain_ref[:, c0:c0 + width]

    def store_group(g, section, val):
        cols = slice(section * A_OUT, (section + 1) * A_OUT)
        rate = A_PATTERNS[g][1]
        if rate == 1:
            za0_ref[:, cols] = val.astype(za0_ref.dtype)
            return
        for j in range(A_OUT // LANES):
            dil_ref[j] = val[:, j * LANES:(j + 1) * LANES]
        for r in range(rate):
            picked = [dil_ref[j, pl.ds(r, tm // rate, stride=rate), :] for j in range(A_OUT // LANES)]
            za_refs[g][0, r, :, cols] = jnp.concatenate(picked, axis=1).astype(za_refs[g].dtype)

    def store_pairs(ref, t, val):
        vt = val.T.astype(ref.dtype)
        ref[0, 2 * t] = vt[:LANES]
        ref[0, 2 * t + 1] = vt[LANES:]

    def rotary(t, acc, wc):
        y = head_norm(acc, wc, LANES)
        y = y * cos_ref[...] + pltpu.roll(y, LANES - 16, 1) * s1_ref[...] + pltpu.roll(y, 16, 1) * s2_ref[...]
        if t < C_QW // LANES:
            yt = y.T.astype(qc_ref.dtype)
            qc_ref[0, 2 * t] = yt[:HEAD_DIM]
            qc_ref[0, 2 * t + 1] = yt[HEAD_DIM:]
        else:
            y = y.astype(kc_ref.dtype)
            kc_ref[0, 0] = y[:, :HEAD_DIM]
            kc_ref[0, 1] = y[:, HEAD_DIM:]
            norms = _dot(y * y, ones_ref[:LANES, :LANES])
            ksq_ref[0] = jnp.max(norms, axis=0, keepdims=True)

    def store_vc(acc):
        acc_t = acc.T.astype(vc_ref.dtype)
        tail = _ones_tail(tm, vc_ref.dtype)
        for kv in range(C_KV_HEADS):
            vc_ref[0, kv, :HEAD_DIM] = acc_t[kv * HEAD_DIM:(kv + 1) * HEAD_DIM]
            vc_ref[0, kv, HEAD_DIM:] = tail

    jobs = []
    for section in range(2):
        for g in range(A_GROUPS):
            wc = section * A_W + g * A_OUT
            jobs.append((wc, A_OUT, lambda acc, wc=wc, g=g, section=section:
                         store_group(g, section, head_norm(acc, wc, A_OUT))))
    for t in range(B_W // 256):
        wc = 2 * A_W + t * 256
        jobs.append((wc, 256, lambda acc, wc=wc, t=t: store_pairs(qbt_ref, t, head_norm(acc, wc, 256))))
    for t in range(B_W // 256):
        wc = 2 * A_W + B_W + t * 256
        def store_kb(acc, wc=wc, t=t):
            kb_ref[:, t * 256:(t + 1) * 256] = head_norm(acc, wc, 256).astype(kb_ref.dtype)
        jobs.append((wc, 256, store_kb))
    for t in range(_N_ROPE // LANES):
        wc = _N_NORM + t * LANES
        jobs.append((wc, LANES, lambda acc, wc=wc, t=t: rotary(t, acc, wc)))
    base = _N_NORM + _N_ROPE
    for g in range(A_GROUPS):
        jobs.append((base + g * A_OUT, A_OUT, lambda acc, g=g: store_group(g, 2, acc)))
    for t in range(B_W // 256):
        jobs.append((base + A_W + t * 256, 256, lambda acc, t=t: store_pairs(vbt_ref, t, acc)))
    jobs.append((base + A_W + B_W, LANES, store_vc))

    product = lambda n: _dot(h, w_ref[:, jobs[n][0]:jobs[n][0] + jobs[n][1]])
    acc_next = product(0)
    for n, (_, _, consume) in enumerate(jobs):
        acc = acc_next
        if n + 1 < len(jobs):
            acc_next = product(n + 1)
        consume(acc)


def _qkv_projection(x2, g1, w_qkv, gain_row, ones_bd, rope, B, S):
    M = x2.shape[0]
    tm = PROJ_TM
    n_w = w_qkv.shape[1]
    per_seq = S // tm
    cos_t, s1_t, s2_t = rope
    r1, r2 = A_PATTERNS[1][1], A_PATTERNS[2][1]
    row = lambda i: (i, 0)
    pos = lambda i: (i % per_seq, 0)
    hm = lambda i: (i // per_seq, 0, i % per_seq, 0)
    hm_t = lambda i: (i // per_seq, 0, 0, i % per_seq)
    cd = _MXU_DTYPE
    return pl.pallas_call(
        _proj_kernel,
        grid=(M // tm,),
        in_specs=[
            pl.BlockSpec((tm, D_MODEL), row),
            _const_spec((1, D_MODEL)),
            _const_spec((D_MODEL, n_w)),
            _const_spec((1, _N_NORM + _N_ROPE)),
            _const_spec((256, 256)),
            pl.BlockSpec((tm, LANES), pos),
            pl.BlockSpec((tm, LANES), pos),
            pl.BlockSpec((tm, LANES), pos),
        ],
        out_specs=[
            pl.BlockSpec((tm, ZA_W), row),
            pl.BlockSpec((1, r1, tm // r1, ZA_W), hm),
            pl.BlockSpec((1, r2, tm // r2, ZA_W), hm),
            pl.BlockSpec((1, B_PAIRS, LANES, tm), hm_t),
            pl.BlockSpec((tm, B_W), row),
            pl.BlockSpec((1, B_PAIRS, LANES, tm), hm_t),
            pl.BlockSpec((1, C_Q_HEADS, HEAD_DIM, tm), hm_t),
            pl.BlockSpec((1, C_KV_HEADS, tm, HEAD_DIM), hm),
            pl.BlockSpec((1, C_KV_HEADS, VT_ROWS, tm), hm_t),
            pl.BlockSpec((1, 1, LANES), lambda i: (i, 0, 0)),
        ],
        out_shape=[
            jax.ShapeDtypeStruct((M, ZA_W), cd),
            jax.ShapeDtypeStruct((B, r1, S // r1, ZA_W), cd),
            jax.ShapeDtypeStruct((B, r2, S // r2, ZA_W), cd),
            jax.ShapeDtypeStruct((B, B_PAIRS, LANES, S), cd),
            jax.ShapeDtypeStruct((M, B_W), cd),
            jax.ShapeDtypeStruct((B, B_PAIRS, LANES, S), cd),
            jax.ShapeDtypeStruct((B, C_Q_HEADS, HEAD_DIM, S), cd),
            jax.ShapeDtypeStruct((B, C_KV_HEADS, S, HEAD_DIM), cd),
            jax.ShapeDtypeStruct((B, C_KV_HEADS, VT_ROWS, S), cd),
            jax.ShapeDtypeStruct((M // tm, 1, LANES), jnp.float32),
        ],
        scratch_shapes=[pltpu.VMEM((A_OUT // LANES, tm, LANES), jnp.float32)],
        compiler_params=_params(("arbitrary",)),
        name="qkv_projection",
    )(x2, g1, w_qkv, gain_row, ones_bd, cos_t, s1_t, s2_t)


A_SUB = 2 * A_RADIUS
A_KEYS = 4 * A_RADIUS


def _mixer_a_kernel(q_ref, kp_ref, ko_ref, kn_ref, vp_ref, vo_ref, vn_ref, bias_ref,
                    o_ref, lse_ref, *, tl, seq_len):
    l0 = pl.program_id(2) * tl
    q = q_ref[0, 0]
    kw = jnp.concatenate([kp_ref[0, 0], ko_ref[0, 0], kn_ref[0, 0]], axis=0)
    vw = jnp.concatenate([vp_ref[0, 0], vo_ref[0, 0], vn_ref[0, 0]], axis=0)
    n_sub = tl // A_SUB
    col = lax.broadcasted_iota(jnp.int32, (A_SUB, A_KEYS), 1)
    for h in range(A_HEADS):
        hs = slice(h * HEAD_DIM, (h + 1) * HEAD_DIM)
        bias = bias_ref[h]
        for sb in range(n_sub):
            i0 = sb * A_SUB
            s = _dot_nt(q[i0:i0 + A_SUB, hs], kw[i0:i0 + A_KEYS, hs]) + bias
            if sb == 0 or sb == n_sub - 1:
                kpos = col + (l0 + i0 - A_RADIUS)
                s = jnp.where((kpos >= 0) & (kpos < seq_len), s, NEG)
            m = jnp.max(s, axis=-1, keepdims=True)
            e = jnp.exp(s - m)
            den = jnp.sum(e, axis=-1, keepdims=True)
            o = _dot(e.astype(_MXU_DTYPE), vw[i0:i0 + A_KEYS, hs]) / den
            o_ref[0, 0, i0:i0 + A_SUB, hs] = o
            lse_ref[0, 0, i0:i0 + A_SUB, hs] = jnp.broadcast_to(m + jnp.log(den), (A_SUB, HEAD_DIM))


def _mixer_a_group(za_g, bias, rate, B, S):
    L = S // rate
    tl = min(512, L)
    nblk = L // A_RADIUS
    per = tl // A_RADIUS

    def own(section):
        return lambda b, r, l: (b, r, l, section)

    def prev(section):
        return lambda b, r, l: (b, r, jnp.maximum(l * per - 1, 0), section)

    def nxt(section):
        return lambda b, r, l: (b, r, jnp.minimum((l + 1) * per, nblk - 1), section)

    edge = (1, 1, A_RADIUS, A_OUT)
    full = (1, 1, tl, A_OUT)
    return pl.pallas_call(
        functools.partial(_mixer_a_kernel, tl=tl, seq_len=L),
        grid=(B, rate, L // tl),
        in_specs=[
            pl.BlockSpec(full, own(0)),
            pl.BlockSpec(edge, prev(1)), pl.BlockSpec(full, own(1)), pl.BlockSpec(edge, nxt(1)),
            pl.BlockSpec(edge, prev(2)), pl.BlockSpec(full, own(2)), pl.BlockSpec(edge, nxt(2)),
            _const_spec((A_HEADS, A_SUB, A_KEYS)),
        ],
        out_specs=[pl.BlockSpec(full, own(0)), pl.BlockSpec(full, own(0))],
        out_shape=[jax.ShapeDtypeStruct((B, rate, L, A_OUT), jnp.float32)] * 2,
        compiler_params=_params(("arbitrary",) * 3),
        name=f"mixer_a_rate{rate}",
    )(za_g, za_g, za_g, za_g, za_g, za_g, za_g, bias)


B_UNIT_ROWS = 4
B_UNIT = B_UNIT_ROWS * GRID_W
B_SPAN_ROWS = B_UNIT_ROWS + B_WIN_ROWS
B_SPAN = B_SPAN_ROWS * GRID_W
B_HALO = (B_WIN_ROWS // 2) * GRID_W
B_TILE_ROWS = 16
B_TILE = B_TILE_ROWS * GRID_W


def _mixer_b_kernel(q_ref, kp_ref, ko_ref, kn_ref, vp_ref, vo_ref, vn_ref, bias_ref,
                    o_ref, kw_ref, vw_ref, *, rows):
    i0 = pl.program_id(1) * B_TILE_ROWS
    kw_ref[0:B_HALO] = kp_ref[0]
    kw_ref[B_HALO:B_HALO + B_TILE] = ko_ref[0]
    kw_ref[B_HALO + B_TILE:] = kn_ref[0]
    vw_ref[:, :, 0:B_HALO] = vp_ref[0]
    vw_ref[:, :, B_HALO:B_HALO + B_TILE] = vo_ref[0]
    vw_ref[:, :, B_HALO + B_TILE:] = vn_ref[0]
    tail = _ones_tail(B_SPAN, _MXU_DTYPE)
    upper = lax.broadcasted_iota(jnp.int32, (LANES, B_UNIT), 0) < HEAD_DIM

    def unit(u, carry):
        i0u = i0 + u * B_UNIT_ROWS
        r0 = jnp.clip(i0u - B_WIN_ROWS // 2, 0, rows - B_SPAN_ROWS)
        off = pl.multiple_of((r0 - (i0 - B_WIN_ROWS // 2)) * GRID_W, LANES)
        variant = jnp.where(i0u == 0, 0, jnp.where(i0u == rows - B_UNIT_ROWS, 2, 1))
        qoff = pl.multiple_of(u * B_UNIT, B_UNIT)

        def scores(h):
            pair, odd = divmod(h, 2)
            qt = q_ref[0, pair, :, pl.ds(qoff, B_UNIT)]
            qt = jnp.where(upper != bool(odd), qt, jnp.zeros_like(qt))
            return _dot(kw_ref[pl.ds(off, B_SPAN), pair * LANES:(pair + 1) * LANES], qt)

        outs = []
        s_next = scores(0)
        for h in range(B_HEADS):
            pair, odd = divmod(h, 2)
            s = s_next + bias_ref[variant, h]
            if h + 1 < B_HEADS:
                s_next = scores(h + 1)
            p = jnp.exp2(s - jnp.max(s, axis=0, keepdims=True)).astype(_MXU_DTYPE)
            vt = vw_ref[pair, odd * HEAD_DIM:(odd + 1) * HEAD_DIM, pl.ds(off, B_SPAN)]
            acc = _dot(jnp.concatenate([vt, tail], axis=0), p)
            outs.append(acc[:HEAD_DIM] / acc[HEAD_DIM:HEAD_DIM + 1])
        o_ref[0, pl.ds(qoff, B_UNIT), :] = jnp.concatenate(outs, axis=0).T.astype(o_ref.dtype)
        return carry

    lax.fori_loop(0, B_TILE_ROWS // B_UNIT_ROWS, unit, 0)


def _mixer_b(qbt, kb, vbt, bias, B, S):
    rows = S // GRID_W
    nt = rows // B_TILE_ROWS
    per = B_TILE // B_HALO
    nh = S // B_HALO
    kb3 = kb.reshape(B, S, B_W)
    prev = lambda t: jnp.maximum(t * per - 1, 0)
    nxt = lambda t: jnp.minimum((t + 1) * per, nh - 1)
    k_edge, k_own = (1, B_HALO, B_W), (1, B_TILE, B_W)
    t_edge, t_own = (1, B_PAIRS, LANES, B_HALO), (1, B_PAIRS, LANES, B_TILE)
    o = pl.pallas_call(
        functools.partial(_mixer_b_kernel, rows=rows),
        grid=(B, nt),
        in_specs=[
            pl.BlockSpec(t_own, lambda b, t: (b, 0, 0, t)),
            pl.BlockSpec(k_edge, lambda b, t: (b, prev(t), 0)),
            pl.BlockSpec(k_own, lambda b, t: (b, t, 0)),
            pl.BlockSpec(k_edge, lambda b, t: (b, nxt(t), 0)),
            pl.BlockSpec(t_edge, lambda b, t: (b, 0, 0, prev(t))),
            pl.BlockSpec(t_own, lambda b, t: (b, 0, 0, t)),
            pl.BlockSpec(t_edge, lambda b, t: (b, 0, 0, nxt(t))),
            _const_spec(bias.shape),
        ],
        out_specs=pl.BlockSpec(k_own, lambda b, t: (b, t, 0)),
        out_shape=jax.ShapeDtypeStruct((B, S, B_W), _MXU_DTYPE),
        scratch_shapes=[pltpu.VMEM((B_TILE + 2 * B_HALO, B_W), _MXU_DTYPE),
                        pltpu.VMEM((B_PAIRS, LANES, B_TILE + 2 * B_HALO), _MXU_DTYPE)],
        compiler_params=_params(("arbitrary",) * 2),
        name="mixer_b",
    )(qbt, kb3, kb3, kb3, vbt, vbt, vbt, bias)
    return o.reshape(B * S, B_W)


C_TQ = 512
C_TK = 512
C_UNIT = 512
C_SHIFT_LIMIT = 60.0


def _mixer_c_kernel(q_ref, k_ref, vt_ref, ksq_ref, o_ref, qcat_ref, m_ref, acc_ref, s_ref, *, n_kv):
    acc_ref[...] = jnp.zeros(acc_ref.shape, jnp.float32)
    for g in range(C_GROUP):
        qcat_ref[:, g * C_TQ:(g + 1) * C_TQ] = q_ref[0, g]
    n_units = C_GROUP * C_TQ // C_UNIT

    def keys(j):
        return k_ref[0, 0, pl.ds(pl.multiple_of(j * C_TK, C_TK), C_TK), :]

    def scores(k, u):
        return _dot(k, qcat_ref[:, u * C_UNIT:(u + 1) * C_UNIT])

    def sweep(update):
        s_ref[...] = scores(keys(0), 0)

        def step(j, carry):
            k = keys(j)
            vt = vt_ref[0, 0, :, pl.ds(pl.multiple_of(j * C_TK, C_TK), C_TK)]
            s_next = s_ref[...]
            for u in range(n_units):
                s = s_next
                if u + 1 < n_units:
                    s_next = scores(k, u + 1)
                else:
                    s_ref[...] = scores(keys(jnp.minimum(j + 1, n_kv - 1)), 0)
                update(s, vt, slice(u * C_UNIT, (u + 1) * C_UNIT))
            return carry

        lax.fori_loop(0, n_kv, step, 0)

    def fixed_shift(s, vt, cols):
        acc_ref[:, cols] += _dot(vt, jnp.exp2(s - m_ref[:, cols]).astype(_MXU_DTYPE))

    def running_max(s, vt, cols):
        m_prev = m_ref[:, cols]
        m_new = jnp.maximum(m_prev, jnp.max(s, axis=0, keepdims=True))
        alpha = jnp.exp2(m_prev - m_new)
        p = jnp.exp2(s - m_new).astype(_MXU_DTYPE)
        acc_ref[:, cols] = alpha * acc_ref[:, cols] + _dot(vt, p)
        m_ref[:, cols] = m_new

    qf = qcat_ref[...].astype(jnp.float32)
    bound = jnp.sqrt(jnp.sum(qf * qf, axis=0, keepdims=True) * ksq_ref[0, 0, :, :1])
    small = jnp.max(bound) <= C_SHIFT_LIMIT

    @pl.when(small)
    def _():
        m_ref[...] = bound
        sweep(fixed_shift)

    @pl.when(jnp.logical_not(small))
    def _():
        m_ref[...] = jnp.full(m_ref.shape, -jnp.inf, jnp.float32)
        sweep(running_max)

    o_t = jnp.concatenate(
        [acc_ref[:HEAD_DIM, g * C_TQ:(g + 1) * C_TQ] / acc_ref[HEAD_DIM:HEAD_DIM + 1, g * C_TQ:(g + 1) * C_TQ]
         for g in range(C_GROUP)], axis=0)
    o_ref[0] = o_t.T.astype(o_ref.dtype)


def _mixer_c(qc_t, kc, vc_t, ksq, B, S):
    o = pl.pallas_call(
        functools.partial(_mixer_c_kernel, n_kv=S // C_TK),
        grid=(B, C_KV_HEADS, S // C_TQ),
        in_specs=[
            pl.BlockSpec((1, C_GROUP, HEAD_DIM, C_TQ), lambda b, kv, i: (b, kv, 0, i)),
            pl.BlockSpec((1, 1, S, HEAD_DIM), lambda b, kv, i: (b, kv, 0, 0)),
            pl.BlockSpec((1, 1, VT_ROWS, S), lambda b, kv, i: (b, kv, 0, 0)),
            pl.BlockSpec((1, 1, 1, LANES), lambda b, kv, i: (b, kv, 0, 0)),
        ],
        out_specs=pl.BlockSpec((1, C_TQ, C_GROUP * HEAD_DIM), lambda b, kv, i: (b, i, kv)),
        out_shape=jax.ShapeDtypeStruct((B, S, C_QW), _MXU_DTYPE),
        scratch_shapes=[pltpu.VMEM((HEAD_DIM, C_GROUP * C_TQ), _MXU_DTYPE),
                        pltpu.VMEM((1, C_GROUP * C_TQ), jnp.float32),
                        pltpu.VMEM((VT_ROWS, C_GROUP * C_TQ), jnp.float32),
                        pltpu.VMEM((C_TK, C_UNIT), jnp.float32)],
        compiler_params=_params(("arbitrary",) * 3),
        name="mixer_c",
    )(qc_t, kc, vc_t, ksq)
    return o.reshape(B * S, C_QW)


MERGE_TM = 512


def _merge_kernel(x_ref, g_ref, oa0_ref, oa1_ref, oa2_ref, l0_ref, l1_ref, l2_ref, ob_ref, oc_ref,
                  wg_ref, pa_ref, pb_ref, pc_ref, wo_ref, out_ref, *scratch):
    tm = x_ref.shape[0]

    def token_major(ref, scr):
        rate = ref.shape[1]
        halves = range(A_OUT // LANES)
        for r in range(rate):
            for j in halves:
                scr[j, pl.ds(r, tm // rate, stride=rate), :] = ref[0, r, :, j * LANES:(j + 1) * LANES]
        return jnp.concatenate([scr[j] for j in halves], axis=1)

    x = x_ref[...]
    h = _rms(x, g_ref[...]).astype(_MXU_DTYPE)
    oa0, l0 = oa0_ref[...], l0_ref[...]
    oa1, l1 = token_major(oa1_ref, scratch[0]), token_major(l1_ref, scratch[1])
    oa2, l2 = token_major(oa2_ref, scratch[2]), token_major(l2_ref, scratch[3])
    mx = jnp.maximum(jnp.maximum(l0, l1), l2)
    w0, w1, w2 = jnp.exp(l0 - mx), jnp.exp(l1 - mx), jnp.exp(l2 - mx)
    o_a = (w0 * oa0 + w1 * oa1 + w2 * oa2) / (w0 + w1 + w2)
    merged = jax.nn.sigmoid(_dot(h, wg_ref[:, 0:D_MODEL])) * _dot(o_a.astype(_MXU_DTYPE), pa_ref[...])
    merged += jax.nn.sigmoid(_dot(h, wg_ref[:, D_MODEL:2 * D_MODEL])) * _dot(ob_ref[...], pb_ref[...])
    merged += jax.nn.sigmoid(_dot(h, wg_ref[:, 2 * D_MODEL:3 * D_MODEL])) * _dot(oc_ref[...], pc_ref[...])
    out_ref[...] = x + _dot(merged.astype(_MXU_DTYPE), wo_ref[...])


def _merge(x2, g1, oa, lse, ob, oc, wg, pa, pb, pc, wo, B, S):
    M = x2.shape[0]
    tm = MERGE_TM
    per_seq = S // tm
    row = lambda i: (i, 0)
    tile = lambda w: pl.BlockSpec((tm, w), row)

    def dilated(rate):
        return pl.BlockSpec((1, rate, tm // rate, A_OUT), lambda i: (i // per_seq, 0, i % per_seq, 0))

    r1, r2 = A_PATTERNS[1][1], A_PATTERNS[2][1]
    return pl.pallas_call(
        _merge_kernel,
        grid=(M // tm,),
        in_specs=[tile(D_MODEL), _const_spec((1, D_MODEL)),
                  tile(A_OUT), dilated(r1), dilated(r2), tile(A_OUT), dilated(r1), dilated(r2),
                  tile(B_W), tile(C_QW),
                  _const_spec(wg.shape), _const_spec(pa.shape), _const_spec(pb.shape),
                  _const_spec(pc.shape), _const_spec(wo.shape)],
        out_specs=tile(D_MODEL),
        out_shape=jax.ShapeDtypeStruct((M, D_MODEL), jnp.float32),
        scratch_shapes=[pltpu.VMEM((A_OUT // LANES, tm, LANES), jnp.float32)] * 4,
        compiler_params=_params(("arbitrary",)),
        name="gated_merge",
    )(x2, g1, oa[0].reshape(M, A_OUT), oa[1], oa[2], lse[0].reshape(M, A_OUT), lse[1], lse[2],
      ob, oc, wg, pa, pb, pc, wo)


FFN_TM = 512
FFN_CHUNK = 256


def _ffn_kernel(x_ref, g_ref, wup_ref, wdown_ref, out_ref, act_ref):
    x = x_ref[...]
    xn = _rms(x, g_ref[...]).astype(_MXU_DTYPE)
    for c in range(D_FF // FFN_CHUNK):
        cs = slice(c * FFN_CHUNK, (c + 1) * FFN_CHUNK)
        a = _dot(xn, wup_ref[:, cs])
        b = _dot(xn, wup_ref[:, D_FF + c * FFN_CHUNK:D_FF + (c + 1) * FFN_CHUNK])
        act_ref[:, cs] = (a * jax.nn.sigmoid(a) * b).astype(act_ref.dtype)
    out_ref[...] = x + _dot(act_ref[...], wdown_ref[...])


def _ffn(x2, g2, wup, wdown):
    M = x2.shape[0]
    tm = FFN_TM
    row = lambda i: (i, 0)
    return pl.pallas_call(
        _ffn_kernel,
        grid=(M // tm,),
        in_specs=[pl.BlockSpec((tm, D_MODEL), row), _const_spec((1, D_MODEL)),
                  _const_spec(wup.shape), _const_spec(wdown.shape)],
        out_specs=pl.BlockSpec((tm, D_MODEL), row),
        out_shape=jax.ShapeDtypeStruct((M, D_MODEL), jnp.float32),
        scratch_shapes=[pltpu.VMEM((tm, D_FF), _MXU_DTYPE)],
        compiler_params=_params(("arbitrary",)),
        name="swiglu_ffn",
    )(x2, g2, wup, wdown)


def _t5_bucket(rel):
    half = T5_BUCKETS // 2
    max_exact = half // 2
    ret = jnp.where(rel > 0, half, 0)
    n = jnp.abs(rel)
    nf = jnp.maximum(n, 1).astype(jnp.float32)
    large = max_exact + (jnp.log(nf / max_exact) / math.log(T5_MAX_DIST / max_exact)
                         * (half - max_exact)).astype(jnp.int32)
    large = jnp.minimum(large, half - 1)
    return ret + jnp.where(n < max_exact, n, large)


def _mixer_a_bias(table_g, rate):
    i = jnp.arange(A_SUB)[:, None]
    j = jnp.arange(A_KEYS)[None, :]
    step = j - A_RADIUS - i
    onehot = (_t5_bucket(step * rate)[:, :, None] == jnp.arange(T5_BUCKETS)).astype(jnp.float32)
    bias = jnp.einsum("ijb,bh->hij", onehot, table_g.astype(jnp.float32), precision=lax.Precision.HIGHEST)
    return jnp.where((jnp.abs(step) <= A_RADIUS)[None], bias, NEG)


def _mixer_b_bias(rpb, rows):
    c = np.arange(GRID_W)
    c0 = np.clip(c - B_WIN_COLS // 2, 0, GRID_W - B_WIN_COLS)
    col_ok = (c[:, None] >= c0[None, :]) & (c[:, None] < c0[None, :] + B_WIN_COLS)
    dc = np.clip(c[:, None] - c[None, :] + B_WIN_COLS - 1, 0, 2 * B_WIN_COLS - 2)
    n_dr, n_dc = 2 * B_WIN_ROWS - 1, 2 * B_WIN_COLS - 1
    first_query_row = np.array([0, B_WIN_ROWS // 2, rows - B_UNIT_ROWS])
    i = first_query_row[:, None, None] + np.arange(B_UNIT_ROWS)[None, None, :]
    r0 = np.clip(first_query_row - B_WIN_ROWS // 2, 0, rows - B_SPAN_ROWS)
    ik = r0[:, None, None] + np.arange(B_SPAN_ROWS)[None, :, None]
    rs = np.clip(i - B_WIN_ROWS // 2, 0, rows - B_WIN_ROWS)
    row_ok = (ik >= rs) & (ik < rs + B_WIN_ROWS)
    dr = np.clip(ik - i + B_WIN_ROWS - 1, 0, n_dr - 1)
    pick_r = ((dr[..., None] == np.arange(n_dr)) & row_ok[..., None]).astype(np.float32)
    pick_c = ((dc[..., None] == np.arange(n_dc)) & col_ok[..., None]).astype(np.float32)
    big = jnp.einsum("vagd,hdk,xyk->vhaxgy", pick_r, rpb.astype(jnp.float32) * LOG2_E, pick_c,
                     precision=lax.Precision.HIGHEST)
    ok = row_ok[:, None, :, None, :, None] & col_ok[None, None, None, :, None, :]
    return jnp.where(ok, big, NEG).reshape(3, B_HEADS, B_SPAN, B_UNIT)


def _rope_tables(S):
    t = jnp.arange(S)
    inv = ROPE_THETA ** (-jnp.arange(0, ROPE_AXIS_DIM, 2, dtype=jnp.float32) / ROPE_AXIS_DIM)
    d = np.arange(LANES) % HEAD_DIM
    inv_lane = inv[d % (ROPE_AXIS_DIM // 2)][None, :]
    is_col = (d >= ROPE_AXIS_DIM)[None, :]
    pos = jnp.where(is_col, (t % GRID_W)[:, None], (t // GRID_W)[:, None]).astype(jnp.float32)
    ang = pos * inv_lane
    first = ((d % ROPE_AXIS_DIM) < ROPE_AXIS_DIM // 2)[None, :]
    sin = jnp.sin(ang)
    return jnp.cos(ang), jnp.where(first, -sin, 0.0), jnp.where(first, 0.0, sin)


def _block_diag_ones():
    i = np.arange(256) // HEAD_DIM
    return jnp.asarray(i[:, None] == i[None, :], _MXU_DTYPE)


def kernel(x, rel_bias_table, norm1, w_in, qk_gain, nat_rpb, w_br_a, w_br_b, w_br_c, w_o,
           norm2, w_up, w_down):
    B, S, D = x.shape
    depth = w_in.shape[0]
    cd = _MXU_DTYPE
    M = B * S
    x2 = x.reshape(M, D)
    rope = _rope_tables(S)
    ones_bd = _block_diag_ones()
    a_bias = [_mixer_a_bias(rel_bias_table[:, g * A_HEADS:(g + 1) * A_HEADS], rate)
              for g, (_, rate) in enumerate(A_PATTERNS)]
    sec = lambda w, lo, n: w[:, lo:lo + n]
    for l in range(depth):
        w = w_in[l]
        w_qkv = jnp.concatenate(
            [sec(w, _QA, A_W), sec(w, _KA, A_W), sec(w, _QB, B_W), sec(w, _KB, B_W),
             sec(w, _QC, C_QW), sec(w, _KC, C_KVW),
             sec(w, _VA, A_W), sec(w, _VB, B_W), sec(w, _VC, C_KVW)], axis=1).astype(cd)
        gq = qk_gain[l]
        tile = lambda g, n, s: jnp.tile(g * s, n)
        gain_row = jnp.concatenate([
            tile(gq[0], A_W // HEAD_DIM, QK_SCALE), tile(gq[1], A_W // HEAD_DIM, 1.0),
            tile(gq[2], B_HEADS, QK_SCALE * LOG2_E), tile(gq[3], B_HEADS, 1.0),
            tile(gq[4], C_Q_HEADS, QK_SCALE * LOG2_E), tile(gq[5], C_KV_HEADS, 1.0)])[None, :]
        g1 = norm1[l][None, :]
        za0, za1, za2, qbt, kb, vbt, qc, kc, vc, ksq = _qkv_projection(
            x2, g1, w_qkv, gain_row, ones_bd, rope, B, S)
        ksq = jnp.max(ksq.reshape(B, -1, C_KV_HEADS, HEAD_DIM), axis=(1, 3))
        ksq = jnp.broadcast_to(ksq[:, :, None, None], (B, C_KV_HEADS, 1, LANES))
        za = (za0.reshape(B, 1, S, ZA_W), za1, za2)
        oa, lse = [], []
        for g, (_, rate) in enumerate(A_PATTERNS):
            o_g, l_g = _mixer_a_group(za[g], a_bias[g], rate, B, S)
            oa.append(o_g)
            lse.append(l_g)
        ob = _mixer_b(qbt, kb, vbt, _mixer_b_bias(nat_rpb[l], S // GRID_W), B, S)
        oc = _mixer_c(qc, kc, vc, ksq, B, S)
        x2 = _merge(x2, g1, oa, lse, ob, oc, sec(w, _ZG, N_BRANCH * D).astype(cd),
                    w_br_a[l].astype(cd), w_br_b[l].astype(cd), w_br_c[l].astype(cd), w_o[l].astype(cd), B, S)
        x2 = _ffn(x2, norm2[l][None, :], w_up[l].astype(cd), w_down[l].astype(cd))
    return x2.reshape(B, S, D)
```

```python
import functools
import math

import jax
import jax.numpy as jnp
import numpy as np
from jax import lax
from jax.experimental import pallas as pl
from jax.experimental.pallas import tpu as pltpu

_MXU_DTYPE = jnp.bfloat16

D_MODEL = 1024
HEAD_DIM = 64
GRID_W = 64
RMS_EPS = 1e-6
NEG = -1e30
A_PATTERNS = ((128, 1), (512, 4), (2048, 16))
A_GROUPS = 3
A_HEADS = 4
A_W = A_GROUPS * A_HEADS * HEAD_DIM
A_OUT = A_HEADS * HEAD_DIM
A_RADIUS = 64
B_HEADS = 8
B_W = B_HEADS * HEAD_DIM
B_WIN_ROWS = 8
B_WIN_COLS = 16
C_Q_HEADS = 8
C_KV_HEADS = 2
C_GROUP = C_Q_HEADS // C_KV_HEADS
C_QW = C_Q_HEADS * HEAD_DIM
C_KVW = C_KV_HEADS * HEAD_DIM
ROPE_THETA = 10000.0
ROPE_AXIS_DIM = HEAD_DIM // 2
T5_BUCKETS = 32
T5_MAX_DIST = 1024
N_BRANCH = 3
D_FF = math.ceil(8 * D_MODEL / 3 / 256) * 256
QK_SCALE = HEAD_DIM ** -0.5
LOG2_E = math.log2(math.e)

VMEM_LIMIT_BYTES = 56 * 1024 * 1024
LANES = 128
BF16_SUBLANES = 16
VT_ROWS = HEAD_DIM + BF16_SUBLANES

_OFF = np.cumsum([0, A_W, A_W, A_W, B_W, B_W, B_W, C_QW, C_KVW, C_KVW]).tolist()
(_QA, _KA, _VA, _QB, _KB, _VB, _QC, _KC, _VC, _ZG) = _OFF

ZA_W = 3 * A_OUT
B_PAIRS = B_HEADS // 2


def _params(sem):
    return pltpu.CompilerParams(dimension_semantics=sem, vmem_limit_bytes=VMEM_LIMIT_BYTES)


def _const_spec(shape):
    nd = len(shape)
    return pl.BlockSpec(shape, lambda *_: (0,) * nd, pipeline_mode=pl.Buffered(1))


def _rms(x, g):
    return x * lax.rsqrt(jnp.mean(x * x, axis=-1, keepdims=True) + RMS_EPS) * g


def _dot(a, b):
    return jnp.dot(a, b, preferred_element_type=jnp.float32)


def _dot_nt(a, b):
    return lax.dot_general(a, b, (((1,), (1,)), ((), ())), preferred_element_type=jnp.float32)


def _ones_tail(width, dtype):
    return (lax.broadcasted_iota(jnp.int32, (VT_ROWS - HEAD_DIM, width), 0) == 0).astype(dtype)


PROJ_TM = 512
_N_NORM = 2 * A_W + 2 * B_W
_N_ROPE = C_QW + C_KVW
_N_PLAIN = A_W + B_W + C_KVW


def _proj_kernel(x_ref, g_ref, w_ref, gain_ref, ones_ref, cos_ref, s1_ref, s2_ref,
                 za0_ref, za1_ref, za2_ref, qbt_ref, kb_ref, vbt_ref, qc_ref, kc_ref, vc_ref, ksq_ref,
                 dil_ref):
    tm = x_ref.shape[0]
    h = _rms(x_ref[...], g_ref[...]).astype(_MXU_DTYPE)
    za_refs = (za0_ref, za1_ref, za2_ref)

    def head_norm(acc, c0, width):
        sq = (acc * acc).astype(_MXU_DTYPE)
        ms = _dot(sq, ones_ref[:width, :width]) * (1.0 / HEAD_DIM)
        return acc * lax.rsqrt(ms + RMS_EPS) * gain_ref[:, c0:c0 + width]

    def store_group(g, section, val):
        cols = slice(section * A_OUT, (section + 1) * A_OUT)
        rate = A_PATTERNS[g][1]
        if rate == 1:
            za0_ref[:, cols] = val.astype(za0_ref.dtype)
            return
        for j in range(A_OUT // LANES):
            dil_ref[j] = val[:, j * LANES:(j + 1) * LANES]
        for r in range(rate):
            picked = [dil_ref[j, pl.ds(r, tm // rate, stride=rate), :] for j in range(A_OUT // LANES)]
            za_refs[g][0, r, :, cols] = jnp.concatenate(picked, axis=1).astype(za_refs[g].dtype)

    def store_pairs(ref, t, val):
        vt = val.T.astype(ref.dtype)
        ref[0, 2 * t] = vt[:LANES]
        ref[0, 2 * t + 1] = vt[LANES:]

    def rotary(t, acc, wc):
        y = head_norm(acc, wc, LANES)
        y = y * cos_ref[...] + pltpu.roll(y, LANES - 16, 1) * s1_ref[...] + pltpu.roll(y, 16, 1) * s2_ref[...]
        if t < C_QW // LANES:
            yt = y.T.astype(qc_ref.dtype)
            qc_ref[0, 2 * t] = yt[:HEAD_DIM]
            qc_ref[0, 2 * t + 1] = yt[HEAD_DIM:]
        else:
            y = y.astype(kc_ref.dtype)
            kc_ref[0, 0] = y[:, :HEAD_DIM]
            kc_ref[0, 1] = y[:, HEAD_DIM:]
            norms = _dot(y * y, ones_ref[:LANES, :LANES])
            ksq_ref[0] = jnp.max(norms, axis=0, keepdims=True)

    def store_vc(acc):
        acc_t = acc.T.astype(vc_ref.dtype)
        tail = _ones_tail(tm, vc_ref.dtype)
        for kv in range(C_KV_HEADS):
            vc_ref[0, kv, :HEAD_DIM] = acc_t[kv * HEAD_DIM:(kv + 1) * HEAD_DIM]
            vc_ref[0, kv, HEAD_DIM:] = tail

    jobs = []
    for section in range(2):
        for g in range(A_GROUPS):
            wc = section * A_W + g * A_OUT
            jobs.append((wc, A_OUT, lambda acc, wc=wc, g=g, section=section:
                         store_group(g, section, head_norm(acc, wc, A_OUT))))
    for t in range(B_W // 256):
        wc = 2 * A_W + t * 256
        jobs.append((wc, 256, lambda acc, wc=wc, t=t: store_pairs(qbt_ref, t, head_norm(acc, wc, 256))))
    for t in range(B_W // 256):
        wc = 2 * A_W + B_W + t * 256
        def store_kb(acc, wc=wc, t=t):
            kb_ref[:, t * 256:(t + 1) * 256] = head_norm(acc, wc, 256).astype(kb_ref.dtype)
        jobs.append((wc, 256, store_kb))
    for t in range(_N_ROPE // LANES):
        wc = _N_NORM + t * LANES
        jobs.append((wc, LANES, lambda acc, wc=wc, t=t: rotary(t, acc, wc)))
    base = _N_NORM + _N_ROPE
    for g in range(A_GROUPS):
        jobs.append((base + g * A_OUT, A_OUT, lambda acc, g=g: store_group(g, 2, acc)))
    for t in range(B_W // 256):
        jobs.append((base + A_W + t * 256, 256, lambda acc, t=t: store_pairs(vbt_ref, t, acc)))
    jobs.append((base + A_W + B_W, LANES, store_vc))

    product = lambda n: _dot(h, w_ref[:, jobs[n][0]:jobs[n][0] + jobs[n][1]])
    acc_next = product(0)
    for n, (_, _, consume) in enumerate(jobs):
        acc = acc_next
        if n + 1 < len(jobs):
            acc_next = product(n + 1)
        consume(acc)


def _qkv_projection(x2, g1, w_qkv, gain_row, ones_bd, rope, B, S):
    M = x2.shape[0]
    tm = PROJ_TM
    n_w = w_qkv.shape[1]
    per_seq = S // tm
    cos_t, s1_t, s2_t = rope
    r1, r2 = A_PATTERNS[1][1], A_PATTERNS[2][1]
    row = lambda i: (i, 0)
    pos = lambda i: (i % per_seq, 0)
    hm = lambda i: (i // per_seq, 0, i % per_seq, 0)
    hm_t = lambda i: (i // per_seq, 0, 0, i % per_seq)
    cd = _MXU_DTYPE
    return pl.pallas_call(
        _proj_kernel,
        grid=(M // tm,),
        in_specs=[
            pl.BlockSpec((tm, D_MODEL), row),
            _const_spec((1, D_MODEL)),
            _const_spec((D_MODEL, n_w)),
            _const_spec((1, _N_NORM + _N_ROPE)),
            _const_spec((256, 256)),
            pl.BlockSpec((tm, LANES), pos),
            pl.BlockSpec((tm, LANES), pos),
            pl.BlockSpec((tm, LANES), pos),
        ],
        out_specs=[
            pl.BlockSpec((tm, ZA_W), row),
            pl.BlockSpec((1, r1, tm // r1, ZA_W), hm),
            pl.BlockSpec((1, r2, tm // r2, ZA_W), hm),
            pl.BlockSpec((1, B_PAIRS, LANES, tm), hm_t),
            pl.BlockSpec((tm, B_W), row),
            pl.BlockSpec((1, B_PAIRS, LANES, tm), hm_t),
            pl.BlockSpec((1, C_Q_HEADS, HEAD_DIM, tm), hm_t),
            pl.BlockSpec((1, C_KV_HEADS, tm, HEAD_DIM), hm),
            pl.BlockSpec((1, C_KV_HEADS, VT_ROWS, tm), hm_t),
            pl.BlockSpec((1, 1, LANES), lambda i: (i, 0, 0)),
        ],
        out_shape=[
            jax.ShapeDtypeStruct((M, ZA_W), cd),
            jax.ShapeDtypeStruct((B, r1, S // r1, ZA_W), cd),
            jax.ShapeDtypeStruct((B, r2, S // r2, ZA_W), cd),
            jax.ShapeDtypeStruct((B, B_PAIRS, LANES, S), cd),
            jax.ShapeDtypeStruct((M, B_W), cd),
            jax.ShapeDtypeStruct((B, B_PAIRS, LANES, S), cd),
            jax.ShapeDtypeStruct((B, C_Q_HEADS, HEAD_DIM, S), cd),
            jax.ShapeDtypeStruct((B, C_KV_HEADS, S, HEAD_DIM), cd),
            jax.ShapeDtypeStruct((B, C_KV_HEADS, VT_ROWS, S), cd),
            jax.ShapeDtypeStruct((M // tm, 1, LANES), jnp.float32),
        ],
        scratch_shapes=[pltpu.VMEM((A_OUT // LANES, tm, LANES), jnp.float32)],
        compiler_params=_params(("arbitrary",)),
        name="qkv_projection",
    )(x2, g1, w_qkv, gain_row, ones_bd, cos_t, s1_t, s2_t)


A_SUB = 2 * A_RADIUS
A_KEYS = 4 * A_RADIUS


def _mixer_a_kernel(q_ref, kp_ref, ko_ref, kn_ref, vp_ref, vo_ref, vn_ref, bias_ref,
                    o_ref, lse_ref, *, tl, seq_len):
    l0 = pl.program_id(2) * tl
    q = q_ref[0, 0]
    kw = jnp.concatenate([kp_ref[0, 0], ko_ref[0, 0], kn_ref[0, 0]], axis=0)
    vw = jnp.concatenate([vp_ref[0, 0], vo_ref[0, 0], vn_ref[0, 0]], axis=0)
    n_sub = tl // A_SUB
    col = lax.broadcasted_iota(jnp.int32, (A_SUB, A_KEYS), 1)
    for h in range(A_HEADS):
        hs = slice(h * HEAD_DIM, (h + 1) * HEAD_DIM)
        bias = bias_ref[h]
        for sb in range(n_sub):
            i0 = sb * A_SUB
            s = _dot_nt(q[i0:i0 + A_SUB, hs], kw[i0:i0 + A_KEYS, hs]) + bias
            if sb == 0 or sb == n_sub - 1:
                kpos = col + (l0 + i0 - A_RADIUS)
                s = jnp.where((kpos >= 0) & (kpos < seq_len), s, NEG)
            m = jnp.max(s, axis=-1, keepdims=True)
            e = jnp.exp(s - m)
            den = jnp.sum(e, axis=-1, keepdims=True)
            o = _dot(e.astype(_MXU_DTYPE), vw[i0:i0 + A_KEYS, hs]) / den
            o_ref[0, 0, i0:i0 + A_SUB, hs] = o
            lse_ref[0, 0, i0:i0 + A_SUB, hs] = jnp.broadcast_to(m + jnp.log(den), (A_SUB, HEAD_DIM))


def _mixer_a_group(za_g, bias, rate, B, S):
    L = S // rate
    tl = min(512, L)
    nblk = L // A_RADIUS
    per = tl // A_RADIUS

    def own(section):
        return lambda b, r, l: (b, r, l, section)

    def prev(section):
        return lambda b, r, l: (b, r, jnp.maximum(l * per - 1, 0), section)

    def nxt(section):
        return lambda b, r, l: (b, r, jnp.minimum((l + 1) * per, nblk - 1), section)

    edge = (1, 1, A_RADIUS, A_OUT)
    full = (1, 1, tl, A_OUT)
    return pl.pallas_call(
        functools.partial(_mixer_a_kernel, tl=tl, seq_len=L),
        grid=(B, rate, L // tl),
        in_specs=[
            pl.BlockSpec(full, own(0)),
            pl.BlockSpec(edge, prev(1)), pl.BlockSpec(full, own(1)), pl.BlockSpec(edge, nxt(1)),
            pl.BlockSpec(edge, prev(2)), pl.BlockSpec(full, own(2)), pl.BlockSpec(edge, nxt(2)),
            _const_spec((A_HEADS, A_SUB, A_KEYS)),
        ],
        out_specs=[pl.BlockSpec(full, own(0)), pl.BlockSpec(full, own(0))],
        out_shape=[jax.ShapeDtypeStruct((B, rate, L, A_OUT), jnp.float32)] * 2,
        compiler_params=_params(("arbitrary",) * 3),
        name=f"mixer_a_rate{rate}",
    )(za_g, za_g, za_g, za_g, za_g, za_g, za_g, bias)


B_UNIT_ROWS = 4
B_UNIT = B_UNIT_ROWS * GRID_W
B_SPAN_ROWS = B_UNIT_ROWS + B_WIN_ROWS
B_SPAN = B_SPAN_ROWS * GRID_W
B_HALO = (B_WIN_ROWS // 2) * GRID_W
B_TILE_ROWS = 16
B_TILE = B_TILE_ROWS * GRID_W


def _mixer_b_kernel(q_ref, kp_ref, ko_ref, kn_ref, vp_ref, vo_ref, vn_ref, bias_ref,
                    o_ref, kw_ref, vw_ref, *, rows):
    i0 = pl.program_id(1) * B_TILE_ROWS
    kw_ref[0:B_HALO] = kp_ref[0]
    kw_ref[B_HALO:B_HALO + B_TILE] = ko_ref[0]
    kw_ref[B_HALO + B_TILE:] = kn_ref[0]
    vw_ref[:, :, 0:B_HALO] = vp_ref[0]
    vw_ref[:, :, B_HALO:B_HALO + B_TILE] = vo_ref[0]
    vw_ref[:, :, B_HALO + B_TILE:] = vn_ref[0]
    tail = _ones_tail(B_SPAN, _MXU_DTYPE)
    upper = lax.broadcasted_iota(jnp.int32, (LANES, B_UNIT), 0) < HEAD_DIM

    def unit(u, carry):
        i0u = i0 + u * B_UNIT_ROWS
        r0 = jnp.clip(i0u - B_WIN_ROWS // 2, 0, rows - B_SPAN_ROWS)
        off = pl.multiple_of((r0 - (i0 - B_WIN_ROWS // 2)) * GRID_W, LANES)
        variant = jnp.where(i0u == 0, 0, jnp.where(i0u == rows - B_UNIT_ROWS, 2, 1))
        qoff = pl.multiple_of(u * B_UNIT, B_UNIT)

        def scores(pair):
            qt = q_ref[0, pair, :, pl.ds(qoff, B_UNIT)]
            zero = jnp.zeros_like(qt)
            qt2 = jnp.concatenate([jnp.where(upper, qt, zero), jnp.where(upper, zero, qt)], axis=1)
            return _dot(kw_ref[pl.ds(off, B_SPAN), pair * LANES:(pair + 1) * LANES], qt2)

        outs = []
        s_next = scores(0)
        for pair in range(B_PAIRS):
            s_pair = s_next
            if pair + 1 < B_PAIRS:
                s_next = scores(pair + 1)
            for odd in range(2):
                s = s_pair[:, odd * B_UNIT:(odd + 1) * B_UNIT] + bias_ref[variant, 2 * pair + odd]
                p = jnp.exp2(s - jnp.max(s, axis=0, keepdims=True)).astype(_MXU_DTYPE)
                vt = vw_ref[pair, odd * HEAD_DIM:(odd + 1) * HEAD_DIM, pl.ds(off, B_SPAN)]
                acc = _dot(jnp.concatenate([vt, tail], axis=0), p)
                outs.append(acc[:HEAD_DIM] / acc[HEAD_DIM:HEAD_DIM + 1])
        o_ref[0, pl.ds(qoff, B_UNIT), :] = jnp.concatenate(outs, axis=0).T.astype(o_ref.dtype)
        return carry

    lax.fori_loop(0, B_TILE_ROWS // B_UNIT_ROWS, unit, 0)


def _mixer_b(qbt, kb, vbt, bias, B, S):
    rows = S // GRID_W
    nt = rows // B_TILE_ROWS
    per = B_TILE // B_HALO
    nh = S // B_HALO
    kb3 = kb.reshape(B, S, B_W)
    prev = lambda t: jnp.maximum(t * per - 1, 0)
    nxt = lambda t: jnp.minimum((t + 1) * per, nh - 1)
    k_edge, k_own = (1, B_HALO, B_W), (1, B_TILE, B_W)
    t_edge, t_own = (1, B_PAIRS, LANES, B_HALO), (1, B_PAIRS, LANES, B_TILE)
    o = pl.pallas_call(
        functools.partial(_mixer_b_kernel, rows=rows),
        grid=(B, nt),
        in_specs=[
            pl.BlockSpec(t_own, lambda b, t: (b, 0, 0, t)),
            pl.BlockSpec(k_edge, lambda b, t: (b, prev(t), 0)),
            pl.BlockSpec(k_own, lambda b, t: (b, t, 0)),
            pl.BlockSpec(k_edge, lambda b, t: (b, nxt(t), 0)),
            pl.BlockSpec(t_edge, lambda b, t: (b, 0, 0, prev(t))),
            pl.BlockSpec(t_own, lambda b, t: (b, 0, 0, t)),
            pl.BlockSpec(t_edge, lambda b, t: (b, 0, 0, nxt(t))),
            _const_spec(bias.shape),
        ],
        out_specs=pl.BlockSpec(k_own, lambda b, t: (b, t, 0)),
        out_shape=jax.ShapeDtypeStruct((B, S, B_W), _MXU_DTYPE),
        scratch_shapes=[pltpu.VMEM((B_TILE + 2 * B_HALO, B_W), _MXU_DTYPE),
                        pltpu.VMEM((B_PAIRS, LANES, B_TILE + 2 * B_HALO), _MXU_DTYPE)],
        compiler_params=_params(("arbitrary",) * 2),
        name="mixer_b",
    )(qbt, kb3, kb3, kb3, vbt, vbt, vbt, bias)
    return o.reshape(B * S, B_W)


C_TQ = 1024
C_TK = 512
C_UNIT = 512
C_SHIFT_LIMIT = 60.0


def _mixer_c_kernel(q_ref, k_ref, vt_ref, ksq_ref, o_ref, qcat_ref, m_ref, acc_ref, s_ref, *, n_kv):
    acc_ref[...] = jnp.zeros(acc_ref.shape, jnp.float32)
    for g in range(C_GROUP):
        qcat_ref[:, g * C_TQ:(g + 1) * C_TQ] = q_ref[0, g]
    n_units = C_GROUP * C_TQ // C_UNIT

    def keys(j):
        return k_ref[0, 0, pl.ds(pl.multiple_of(j * C_TK, C_TK), C_TK), :]

    def scores(k, u):
        return _dot(k, qcat_ref[:, u * C_UNIT:(u + 1) * C_UNIT])

    def sweep(update):
        s_ref[...] = scores(keys(0), 0)

        def step(j, carry):
            k = keys(j)
            vt = vt_ref[0, 0, :, pl.ds(pl.multiple_of(j * C_TK, C_TK), C_TK)]
            s_next = s_ref[...]
            for u in range(n_units):
                s = s_next
                if u + 1 < n_units:
                    s_next = scores(k, u + 1)
                else:
                    s_ref[...] = scores(keys(jnp.minimum(j + 1, n_kv - 1)), 0)
                update(s, vt, slice(u * C_UNIT, (u + 1) * C_UNIT))
            return carry

        lax.fori_loop(0, n_kv, step, 0)

    def fixed_shift(s, vt, cols):
        acc_ref[:, cols] += _dot(vt, jnp.exp2(s - m_ref[:, cols]).astype(_MXU_DTYPE))

    def running_max(s, vt, cols):
        m_prev = m_ref[:, cols]
        m_new = jnp.maximum(m_prev, jnp.max(s, axis=0, keepdims=True))
        alpha = jnp.exp2(m_prev - m_new)
        p = jnp.exp2(s - m_new).astype(_MXU_DTYPE)
        acc_ref[:, cols] = alpha * acc_ref[:, cols] + _dot(vt, p)
        m_ref[:, cols] = m_new

    qf = qcat_ref[...].astype(jnp.float32)
    bound = jnp.sqrt(jnp.sum(qf * qf, axis=0, keepdims=True) * ksq_ref[0, 0, :, :1])
    small = jnp.max(bound) <= C_SHIFT_LIMIT

    @pl.when(small)
    def _():
        m_ref[...] = bound
        sweep(fixed_shift)

    @pl.when(jnp.logical_not(small))
    def _():
        m_ref[...] = jnp.full(m_ref.shape, -jnp.inf, jnp.float32)
        sweep(running_max)

    o_t = jnp.concatenate(
        [acc_ref[:HEAD_DIM, g * C_TQ:(g + 1) * C_TQ] / acc_ref[HEAD_DIM:HEAD_DIM + 1, g * C_TQ:(g + 1) * C_TQ]
         for g in range(C_GROUP)], axis=0)
    o_ref[0] = o_t.T.astype(o_ref.dtype)


def _mixer_c(qc_t, kc, vc_t, ksq, B, S):
    o = pl.pallas_call(
        functools.partial(_mixer_c_kernel, n_kv=S // C_TK),
        grid=(B, C_KV_HEADS, S // C_TQ),
        in_specs=[
            pl.BlockSpec((1, C_GROUP, HEAD_DIM, C_TQ), lambda b, kv, i: (b, kv, 0, i)),
            pl.BlockSpec((1, 1, S, HEAD_DIM), lambda b, kv, i: (b, kv, 0, 0)),
            pl.BlockSpec((1, 1, VT_ROWS, S), lambda b, kv, i: (b, kv, 0, 0)),
            pl.BlockSpec((1, 1, 1, LANES), lambda b, kv, i: (b, kv, 0, 0)),
        ],
        out_specs=pl.BlockSpec((1, C_TQ, C_GROUP * HEAD_DIM), lambda b, kv, i: (b, i, kv)),
        out_shape=jax.ShapeDtypeStruct((B, S, C_QW), _MXU_DTYPE),
        scratch_shapes=[pltpu.VMEM((HEAD_DIM, C_GROUP * C_TQ), _MXU_DTYPE),
                        pltpu.VMEM((1, C_GROUP * C_TQ), jnp.float32),
                        pltpu.VMEM((VT_ROWS, C_GROUP * C_TQ), jnp.float32),
                        pltpu.VMEM((C_TK, C_UNIT), jnp.float32)],
        compiler_params=_params(("arbitrary",) * 3),
        name="mixer_c",
    )(qc_t, kc, vc_t, ksq)
    return o.reshape(B * S, C_QW)


MERGE_TM = 512


def _merge_kernel(x_ref, g_ref, oa0_ref, oa1_ref, oa2_ref, l0_ref, l1_ref, l2_ref, ob_ref, oc_ref,
                  wg_ref, pa_ref, pb_ref, pc_ref, wo_ref, out_ref, *scratch):
    tm = x_ref.shape[0]

    def token_major(ref, scr):
        rate = ref.shape[1]
        halves = range(A_OUT // LANES)
        for r in range(rate):
            for j in halves:
                scr[j, pl.ds(r, tm // rate, stride=rate), :] = ref[0, r, :, j * LANES:(j + 1) * LANES]
        return jnp.concatenate([scr[j] for j in halves], axis=1)

    x = x_ref[...]
    h = _rms(x, g_ref[...]).astype(_MXU_DTYPE)
    oa0, l0 = oa0_ref[...], l0_ref[...]
    oa1, l1 = token_major(oa1_ref, scratch[0]), token_major(l1_ref, scratch[1])
    oa2, l2 = token_major(oa2_ref, scratch[2]), token_major(l2_ref, scratch[3])
    mx = jnp.maximum(jnp.maximum(l0, l1), l2)
    w0, w1, w2 = jnp.exp(l0 - mx), jnp.exp(l1 - mx), jnp.exp(l2 - mx)
    o_a = (w0 * oa0 + w1 * oa1 + w2 * oa2) / (w0 + w1 + w2)
    merged = jax.nn.sigmoid(_dot(h, wg_ref[:, 0:D_MODEL])) * _dot(o_a.astype(_MXU_DTYPE), pa_ref[...])
    merged += jax.nn.sigmoid(_dot(h, wg_ref[:, D_MODEL:2 * D_MODEL])) * _dot(ob_ref[...], pb_ref[...])
    merged += jax.nn.sigmoid(_dot(h, wg_ref[:, 2 * D_MODEL:3 * D_MODEL])) * _dot(oc_ref[...], pc_ref[...])
    out_ref[...] = x + _dot(merged.astype(_MXU_DTYPE), wo_ref[...])


def _merge(x2, g1, oa, lse, ob, oc, wg, pa, pb, pc, wo, B, S):
    M = x2.shape[0]
    tm = MERGE_TM
    per_seq = S // tm
    row = lambda i: (i, 0)
    tile = lambda w: pl.BlockSpec((tm, w), row)

    def dilated(rate):
        return pl.BlockSpec((1, rate, tm // rate, A_OUT), lambda i: (i // per_seq, 0, i % per_seq, 0))

    r1, r2 = A_PATTERNS[1][1], A_PATTERNS[2][1]
    return pl.pallas_call(
        _merge_kernel,
        grid=(M // tm,),
        in_specs=[tile(D_MODEL), _const_spec((1, D_MODEL)),
                  tile(A_OUT), dilated(r1), dilated(r2), tile(A_OUT), dilated(r1), dilated(r2),
                  tile(B_W), tile(C_QW),
                  _const_spec(wg.shape), _const_spec(pa.shape), _const_spec(pb.shape),
                  _const_spec(pc.shape), _const_spec(wo.shape)],
        out_specs=tile(D_MODEL),
        out_shape=jax.ShapeDtypeStruct((M, D_MODEL), jnp.float32),
        scratch_shapes=[pltpu.VMEM((A_OUT // LANES, tm, LANES), jnp.float32)] * 4,
        compiler_params=_params(("arbitrary",)),
        name="gated_merge",
    )(x2, g1, oa[0].reshape(M, A_OUT), oa[1], oa[2], lse[0].reshape(M, A_OUT), lse[1], lse[2],
      ob, oc, wg, pa, pb, pc, wo)


FFN_TM = 512
FFN_CHUNK = 256


def _ffn_kernel(x_ref, g_ref, wup_ref, wdown_ref, out_ref, act_ref):
    x = x_ref[...]
    xn = _rms(x, g_ref[...]).astype(_MXU_DTYPE)
    for c in range(D_FF // FFN_CHUNK):
        cs = slice(c * FFN_CHUNK, (c + 1) * FFN_CHUNK)
        a = _dot(xn, wup_ref[:, cs])
        b = _dot(xn, wup_ref[:, D_FF + c * FFN_CHUNK:D_FF + (c + 1) * FFN_CHUNK])
        act_ref[:, cs] = (a * jax.nn.sigmoid(a) * b).astype(act_ref.dtype)
    out_ref[...] = x + _dot(act_ref[...], wdown_ref[...])


def _ffn(x2, g2, wup, wdown):
    M = x2.shape[0]
    tm = FFN_TM
    row = lambda i: (i, 0)
    return pl.pallas_call(
        _ffn_kernel,
        grid=(M // tm,),
        in_specs=[pl.BlockSpec((tm, D_MODEL), row), _const_spec((1, D_MODEL)),
                  _const_spec(wup.shape), _const_spec(wdown.shape)],
        out_specs=pl.BlockSpec((tm, D_MODEL), row),
        out_shape=jax.ShapeDtypeStruct((M, D_MODEL), jnp.float32),
        scratch_shapes=[pltpu.VMEM((tm, D_FF), _MXU_DTYPE)],
        compiler_params=_params(("arbitrary",)),
        name="swiglu_ffn",
    )(x2, g2, wup, wdown)


def _t5_bucket(rel):
    half = T5_BUCKETS // 2
    max_exact = half // 2
    ret = jnp.where(rel > 0, half, 0)
    n = jnp.abs(rel)
    nf = jnp.maximum(n, 1).astype(jnp.float32)
    large = max_exact + (jnp.log(nf / max_exact) / math.log(T5_MAX_DIST / max_exact)
                         * (half - max_exact)).astype(jnp.int32)
    large = jnp.minimum(large, half - 1)
    return ret + jnp.where(n < max_exact, n, large)


def _mixer_a_bias(table_g, rate):
    i = jnp.arange(A_SUB)[:, None]
    j = jnp.arange(A_KEYS)[None, :]
    step = j - A_RADIUS - i
    onehot = (_t5_bucket(step * rate)[:, :, None] == jnp.arange(T5_BUCKETS)).astype(jnp.float32)
    bias = jnp.einsum("ijb,bh->hij", onehot, table_g.astype(jnp.float32), precision=lax.Precision.HIGHEST)
    return jnp.where((jnp.abs(step) <= A_RADIUS)[None], bias, NEG)


def _mixer_b_bias(rpb, rows):
    c = np.arange(GRID_W)
    c0 = np.clip(c - B_WIN_COLS // 2, 0, GRID_W - B_WIN_COLS)
    col_ok = (c[:, None] >= c0[None, :]) & (c[:, None] < c0[None, :] + B_WIN_COLS)
    dc = np.clip(c[:, None] - c[None, :] + B_WIN_COLS - 1, 0, 2 * B_WIN_COLS - 2)
    n_dr, n_dc = 2 * B_WIN_ROWS - 1, 2 * B_WIN_COLS - 1
    first_query_row = np.array([0, B_WIN_ROWS // 2, rows - B_UNIT_ROWS])
    i = first_query_row[:, None, None] + np.arange(B_UNIT_ROWS)[None, None, :]
    r0 = np.clip(first_query_row - B_WIN_ROWS // 2, 0, rows - B_SPAN_ROWS)
    ik = r0[:, None, None] + np.arange(B_SPAN_ROWS)[None, :, None]
    rs = np.clip(i - B_WIN_ROWS // 2, 0, rows - B_WIN_ROWS)
    row_ok = (ik >= rs) & (ik < rs + B_WIN_ROWS)
    dr = np.clip(ik - i + B_WIN_ROWS - 1, 0, n_dr - 1)
    pick_r = ((dr[..., None] == np.arange(n_dr)) & row_ok[..., None]).astype(np.float32)
    pick_c = ((dc[..., None] == np.arange(n_dc)) & col_ok[..., None]).astype(np.float32)
    big = jnp.einsum("vagd,hdk,xyk->vhaxgy", pick_r, rpb.astype(jnp.float32) * LOG2_E, pick_c,
                     precision=lax.Precision.HIGHEST)
    ok = row_ok[:, None, :, None, :, None] & col_ok[None, None, None, :, None, :]
    return jnp.where(ok, big, NEG).reshape(3, B_HEADS, B_SPAN, B_UNIT)


def _rope_tables(S):
    t = jnp.arange(S)
    inv = ROPE_THETA ** (-jnp.arange(0, ROPE_AXIS_DIM, 2, dtype=jnp.float32) / ROPE_AXIS_DIM)
    d = np.arange(LANES) % HEAD_DIM
    inv_lane = inv[d % (ROPE_AXIS_DIM // 2)][None, :]
    is_col = (d >= ROPE_AXIS_DIM)[None, :]
    pos = jnp.where(is_col, (t % GRID_W)[:, None], (t // GRID_W)[:, None]).astype(jnp.float32)
    ang = pos * inv_lane
    first = ((d % ROPE_AXIS_DIM) < ROPE_AXIS_DIM // 2)[None, :]
    sin = jnp.sin(ang)
    return jnp.cos(ang), jnp.where(first, -sin, 0.0), jnp.where(first, 0.0, sin)


def _block_diag_ones():
    i = np.arange(256) // HEAD_DIM
    return jnp.asarray(i[:, None] == i[None, :], _MXU_DTYPE)


def kernel(x, rel_bias_table, norm1, w_in, qk_gain, nat_rpb, w_br_a, w_br_b, w_br_c, w_o,
           norm2, w_up, w_down):
    B, S, D = x.shape
    depth = w_in.shape[0]
    cd = _MXU_DTYPE
    M = B * S
    x2 = x.reshape(M, D)
    rope = _rope_tables(S)
    ones_bd = _block_diag_ones()
    a_bias = [_mixer_a_bias(rel_bias_table[:, g * A_HEADS:(g + 1) * A_HEADS], rate)
              for g, (_, rate) in enumerate(A_PATTERNS)]
    sec = lambda w, lo, n: w[:, lo:lo + n]
    for l in range(depth):
        w = w_in[l]
        w_qkv = jnp.concatenate(
            [sec(w, _QA, A_W), sec(w, _KA, A_W), sec(w, _QB, B_W), sec(w, _KB, B_W),
             sec(w, _QC, C_QW), sec(w, _KC, C_KVW),
             sec(w, _VA, A_W), sec(w, _VB, B_W), sec(w, _VC, C_KVW)], axis=1).astype(cd)
        gq = qk_gain[l]
        tile = lambda g, n, s: jnp.tile(g * s, n)
        gain_row = jnp.concatenate([
            tile(gq[0], A_W // HEAD_DIM, QK_SCALE), tile(gq[1], A_W // HEAD_DIM, 1.0),
            tile(gq[2], B_HEADS, QK_SCALE * LOG2_E), tile(gq[3], B_HEADS, 1.0),
            tile(gq[4], C_Q_HEADS, QK_SCALE * LOG2_E), tile(gq[5], C_KV_HEADS, 1.0)])[None, :]
        g1 = norm1[l][None, :]
        za0, za1, za2, qbt, kb, vbt, qc, kc, vc, ksq = _qkv_projection(
            x2, g1, w_qkv, gain_row, ones_bd, rope, B, S)
        ksq = jnp.max(ksq.reshape(B, -1, C_KV_HEADS, HEAD_DIM), axis=(1, 3))
        ksq = jnp.broadcast_to(ksq[:, :, None, None], (B, C_KV_HEADS, 1, LANES))
        za = (za0.reshape(B, 1, S, ZA_W), za1, za2)
        oa, lse = [], []
        for g, (_, rate) in enumerate(A_PATTERNS):
            o_g, l_g = _mixer_a_group(za[g], a_bias[g], rate, B, S)
            oa.append(o_g)
            lse.append(l_g)
        ob = _mixer_b(qbt, kb, vbt, _mixer_b_bias(nat_rpb[l], S // GRID_W), B, S)
        oc = _mixer_c(qc, kc, vc, ksq, B, S)
        x2 = _merge(x2, g1, oa, lse, ob, oc, sec(w, _ZG, N_BRANCH * D).astype(cd),
                    w_br_a[l].astype(cd), w_br_b[l].astype(cd), w_br_c[l].astype(cd), w_o[l].astype(cd), B, S)
        x2 = _ffn(x2, norm2[l][None, :], w_up[l].astype(cd), w_down[l].astype(cd))
    return x2.reshape(B, S, D)
```

```python
import functools
import math

import jax
import jax.numpy as jnp
import numpy as np
from jax import lax
from jax.experimental import pallas as pl
from jax.experimental.pallas import tpu as pltpu

_MXU_DTYPE = jnp.bfloat16

D_MODEL = 1024
HEAD_DIM = 64
GRID_W = 64
RMS_EPS = 1e-6
NEG = -1e30
A_PATTERNS = ((128, 1), (512, 4), (2048, 16))
A_GROUPS = 3
A_HEADS = 4
A_W = A_GROUPS * A_HEADS * HEAD_DIM
A_OUT = A_HEADS * HEAD_DIM
A_RADIUS = 64
B_HEADS = 8
B_W = B_HEADS * HEAD_DIM
B_WIN_ROWS = 8
B_WIN_COLS = 16
C_Q_HEADS = 8
C_KV_HEADS = 2
C_GROUP = C_Q_HEADS // C_KV_HEADS
C_QW = C_Q_HEADS * HEAD_DIM
C_KVW = C_KV_HEADS * HEAD_DIM
ROPE_THETA = 10000.0
ROPE_AXIS_DIM = HEAD_DIM // 2
T5_BUCKETS = 32
T5_MAX_DIST = 1024
N_BRANCH = 3
D_FF = math.ceil(8 * D_MODEL / 3 / 256) * 256
QK_SCALE = HEAD_DIM ** -0.5
LOG2_E = math.log2(math.e)

VMEM_LIMIT_BYTES = 56 * 1024 * 1024
LANES = 128
BF16_SUBLANES = 16
VT_ROWS = HEAD_DIM + BF16_SUBLANES

_OFF = np.cumsum([0, A_W, A_W, A_W, B_W, B_W, B_W, C_QW, C_KVW, C_KVW]).tolist()
(_QA, _KA, _VA, _QB, _KB, _VB, _QC, _KC, _VC, _ZG) = _OFF

ZA_W = 3 * A_OUT
B_PAIRS = B_HEADS // 2


def _params(sem):
    return pltpu.CompilerParams(dimension_semantics=sem, vmem_limit_bytes=VMEM_LIMIT_BYTES)


def _const_spec(shape):
    nd = len(shape)
    return pl.BlockSpec(shape, lambda *_: (0,) * nd, pipeline_mode=pl.Buffered(1))


def _rms(x, g):
    return x * lax.rsqrt(jnp.mean(x * x, axis=-1, keepdims=True) + RMS_EPS) * g


def _dot(a, b):
    return jnp.dot(a, b, preferred_element_type=jnp.float32)


def _dot_nt(a, b):
    return lax.dot_general(a, b, (((1,), (1,)), ((), ())), preferred_element_type=jnp.float32)


def _ones_tail(width, dtype):
    return (lax.broadcasted_iota(jnp.int32, (VT_ROWS - HEAD_DIM, width), 0) == 0).astype(dtype)


PROJ_TM = 512
_N_NORM = 2 * A_W + 2 * B_W
_N_ROPE = C_QW + C_KVW
_N_PLAIN = A_W + B_W + C_KVW


def _proj_kernel(x_ref, g_ref, w_ref, gain_ref, ones_ref, cos_ref, s1_ref, s2_ref,
                 za0_ref, za1_ref, za2_ref, qbt_ref, kb_ref, vbt_ref, qc_ref, kc_ref, vc_ref, ksq_ref,
                 dil_ref):
    tm = x_ref.shape[0]
    h = _rms(x_ref[...], g_ref[...]).astype(_MXU_DTYPE)
    za_refs = (za0_ref, za1_ref, za2_ref)

    def head_norm(acc, c0, width):
        sq = (acc * acc).astype(_MXU_DTYPE)
        ms = _dot(sq, ones_ref[:width, :width]) * (1.0 / HEAD_DIM)
        return acc * lax.rsqrt(ms + RMS_EPS) * gain_ref[:, c0:c0 + width]

    def store_group(g, section, val):
        cols = slice(section * A_OUT, (section + 1) * A_OUT)
        rate = A_PATTERNS[g][1]
        if rate == 1:
            za0_ref[:, cols] = val.astype(za0_ref.dtype)
            return
        for j in range(A_OUT // LANES):
            dil_ref[j] = val[:, j * LANES:(j + 1) * LANES]
        for r in range(rate):
            picked = [dil_ref[j, pl.ds(r, tm // rate, stride=rate), :] for j in range(A_OUT // LANES)]
            za_refs[g][0, r, :, cols] = jnp.concatenate(picked, axis=1).astype(za_refs[g].dtype)

    def store_pairs(ref, t, val):
        vt = val.T.astype(ref.dtype)
        ref[0, 2 * t] = vt[:LANES]
        ref[0, 2 * t + 1] = vt[LANES:]

    def rotary(t, acc, wc):
        y = head_norm(acc, wc, LANES)
        y = y * cos_ref[...] + pltpu.roll(y, LANES - 16, 1) * s1_ref[...] + pltpu.roll(y, 16, 1) * s2_ref[...]
        if t < C_QW // LANES:
            yt = y.T.astype(qc_ref.dtype)
            qc_ref[0, 2 * t] = yt[:HEAD_DIM]
            qc_ref[0, 2 * t + 1] = yt[HEAD_DIM:]
        else:
            y = y.astype(kc_ref.dtype)
            kc_ref[0, 0] = y[:, :HEAD_DIM]
            kc_ref[0, 1] = y[:, HEAD_DIM:]
            norms = _dot(y * y, ones_ref[:LANES, :LANES])
            ksq_ref[0] = jnp.max(norms, axis=0, keepdims=True)

    def store_vc(acc):
        acc_t = acc.T.astype(vc_ref.dtype)
        tail = _ones_tail(tm, vc_ref.dtype)
        for kv in range(C_KV_HEADS):
            vc_ref[0, kv, :HEAD_DIM] = acc_t[kv * HEAD_DIM:(kv + 1) * HEAD_DIM]
            vc_ref[0, kv, HEAD_DIM:] = tail

    jobs = []
    for section in range(2):
        for g in range(A_GROUPS):
            wc = section * A_W + g * A_OUT
            jobs.append((wc, A_OUT, lambda acc, wc=wc, g=g, section=section:
                         store_group(g, section, head_norm(acc, wc, A_OUT))))
    for t in range(B_W // 256):
        wc = 2 * A_W + t * 256
        jobs.append((wc, 256, lambda acc, wc=wc, t=t: store_pairs(qbt_ref, t, head_norm(acc, wc, 256))))
    for t in range(B_W // 256):
        wc = 2 * A_W + B_W + t * 256
        def store_kb(acc, wc=wc, t=t):
            kb_ref[:, t * 256:(t + 1) * 256] = head_norm(acc, wc, 256).astype(kb_ref.dtype)
        jobs.append((wc, 256, store_kb))
    for t in range(_N_ROPE // LANES):
        wc = _N_NORM + t * LANES
        jobs.append((wc, LANES, lambda acc, wc=wc, t=t: rotary(t, acc, wc)))
    base = _N_NORM + _N_ROPE
    for g in range(A_GROUPS):
        jobs.append((base + g * A_OUT, A_OUT, lambda acc, g=g: store_group(g, 2, acc)))
    for t in range(B_W // 256):
        jobs.append((base + A_W + t * 256, 256, lambda acc, t=t: store_pairs(vbt_ref, t, acc)))
    jobs.append((base + A_W + B_W, LANES, store_vc))

    product = lambda n: _dot(h, w_ref[:, jobs[n][0]:jobs[n][0] + jobs[n][1]])
    acc_next = product(0)
    for n, (_, _, consume) in enumerate(jobs):
        acc = acc_next
        if n + 1 < len(jobs):
            acc_next = product(n + 1)
        consume(acc)


def _qkv_projection(x2, g1, w_qkv, gain_row, ones_bd, rope, B, S):
    M = x2.shape[0]
    tm = PROJ_TM
    n_w = w_qkv.shape[1]
    per_seq = S // tm
    cos_t, s1_t, s2_t = rope
    r1, r2 = A_PATTERNS[1][1], A_PATTERNS[2][1]
    row = lambda i: (i, 0)
    pos = lambda i: (i % per_seq, 0)
    hm = lambda i: (i // per_seq, 0, i % per_seq, 0)
    hm_t = lambda i: (i // per_seq, 0, 0, i % per_seq)
    cd = _MXU_DTYPE
    return pl.pallas_call(
        _proj_kernel,
        grid=(M // tm,),
        in_specs=[
            pl.BlockSpec((tm, D_MODEL), row),
            _const_spec((1, D_MODEL)),
            _const_spec((D_MODEL, n_w)),
            _const_spec((1, _N_NORM + _N_ROPE)),
            _const_spec((256, 256)),
            pl.BlockSpec((tm, LANES), pos),
            pl.BlockSpec((tm, LANES), pos),
            pl.BlockSpec((tm, LANES), pos),
        ],
        out_specs=[
            pl.BlockSpec((tm, ZA_W), row),
            pl.BlockSpec((1, r1, tm // r1, ZA_W), hm),
            pl.BlockSpec((1, r2, tm // r2, ZA_W), hm),
            pl.BlockSpec((1, B_PAIRS, LANES, tm), hm_t),
            pl.BlockSpec((tm, B_W), row),
            pl.BlockSpec((1, B_PAIRS, LANES, tm), hm_t),
            pl.BlockSpec((1, C_Q_HEADS, HEAD_DIM, tm), hm_t),
            pl.BlockSpec((1, C_KV_HEADS, tm, HEAD_DIM), hm),
            pl.BlockSpec((1, C_KV_HEADS, VT_ROWS, tm), hm_t),
            pl.BlockSpec((1, 1, LANES), lambda i: (i, 0, 0)),
        ],
        out_shape=[
            jax.ShapeDtypeStruct((M, ZA_W), cd),
            jax.ShapeDtypeStruct((B, r1, S // r1, ZA_W), cd),
            jax.ShapeDtypeStruct((B, r2, S // r2, ZA_W), cd),
            jax.ShapeDtypeStruct((B, B_PAIRS, LANES, S), cd),
            jax.ShapeDtypeStruct((M, B_W), cd),
            jax.ShapeDtypeStruct((B, B_PAIRS, LANES, S), cd),
            jax.ShapeDtypeStruct((B, C_Q_HEADS, HEAD_DIM, S), cd),
            jax.ShapeDtypeStruct((B, C_KV_HEADS, S, HEAD_DIM), cd),
            jax.ShapeDtypeStruct((B, C_KV_HEADS, VT_ROWS, S), cd),
            jax.ShapeDtypeStruct((M // tm, 1, LANES), jnp.float32),
        ],
        scratch_shapes=[pltpu.VMEM((A_OUT // LANES, tm, LANES), jnp.float32)],
        compiler_params=_params(("arbitrary",)),
        name="qkv_projection",
    )(x2, g1, w_qkv, gain_row, ones_bd, cos_t, s1_t, s2_t)


A_UNIT = 4 * A_RADIUS
A_SPAN = A_UNIT + 2 * A_RADIUS


def _mixer_a_kernel(q_ref, kp_ref, ko_ref, kn_ref, vp_ref, vo_ref, vn_ref, bias_ref,
                    o_ref, lse_ref, kw_ref, vw_ref, s_ref, *, tl, seq_len):
    l0 = pl.program_id(2) * tl
    kw_ref[0:A_RADIUS] = kp_ref[0, 0]
    kw_ref[A_RADIUS:A_RADIUS + tl] = ko_ref[0, 0]
    kw_ref[A_RADIUS + tl:] = kn_ref[0, 0]
    vw_ref[0:A_RADIUS] = vp_ref[0, 0]
    vw_ref[A_RADIUS:A_RADIUS + tl] = vo_ref[0, 0]
    vw_ref[A_RADIUS + tl:] = vn_ref[0, 0]
    even_q = lax.broadcasted_iota(jnp.int32, (A_UNIT, LANES), 1) < HEAD_DIM
    lane_k = lax.broadcasted_iota(jnp.int32, (A_SPAN, LANES), 1)

    n_units = tl // A_UNIT

    def scores(u, pair):
        off = pl.multiple_of(u * A_UNIT, A_UNIT)
        q = q_ref[0, 0, pl.ds(off, A_UNIT), pair * LANES:(pair + 1) * LANES]
        zero = jnp.zeros_like(q)
        q2 = jnp.concatenate([jnp.where(even_q, q, zero), jnp.where(even_q, zero, q)], axis=0)
        return _dot_nt(kw_ref[pl.ds(off, A_SPAN), pair * LANES:(pair + 1) * LANES], q2)

    s_ref[...] = scores(0, 0)

    def unit(u, carry):
        off = pl.multiple_of(u * A_UNIT, A_UNIT)
        top_ok = l0 + off - A_RADIUS >= 0
        bot_ok = l0 + off + A_UNIT + A_RADIUS <= seq_len
        outs, lses = [], []
        s_next = s_ref[...]
        for pair in range(A_HEADS // 2):
            s_pair = s_next
            if pair + 1 < A_HEADS // 2:
                s_next = scores(u, pair + 1)
            else:
                s_ref[...] = scores(jnp.minimum(u + 1, n_units - 1), 0)
            v_slab = vw_ref[pl.ds(off, A_SPAN), pair * LANES:(pair + 1) * LANES]
            for odd in range(2):
                s = s_pair[:, odd * A_UNIT:(odd + 1) * A_UNIT] + bias_ref[2 * pair + odd]
                s = jnp.concatenate([jnp.where(top_ok, s[:A_RADIUS], NEG), s[A_RADIUS:A_SPAN - A_RADIUS],
                                     jnp.where(bot_ok, s[A_SPAN - A_RADIUS:], NEG)], axis=0)
                m = jnp.max(s, axis=0, keepdims=True)
                p = jnp.exp2(s - m).astype(_MXU_DTYPE)
                den_row = (1 - odd) * HEAD_DIM
                mine = (lane_k >= odd * HEAD_DIM) & (lane_k < (odd + 1) * HEAD_DIM)
                v_aug = jnp.where(mine, v_slab, (lane_k == den_row).astype(v_slab.dtype))
                acc = lax.dot_general(v_aug, p, (((0,), (0,)), ((), ())),
                                      preferred_element_type=jnp.float32)
                den = acc[den_row:den_row + 1]
                outs.append(acc[odd * HEAD_DIM:(odd + 1) * HEAD_DIM] / den)
                lses.append(jnp.broadcast_to(m + jnp.log2(den), (HEAD_DIM, A_UNIT)))
        o_ref[0, 0, pl.ds(off, A_UNIT), :] = jnp.concatenate(outs, axis=0).T
        lse_ref[0, 0, pl.ds(off, A_UNIT), :] = jnp.concatenate(lses, axis=0).T
        return carry

    lax.fori_loop(0, n_units, unit, 0)


def _mixer_a_group(za_g, bias, rate, B, S):
    L = S // rate
    tl = min(1024, L)
    nblk = L // A_RADIUS
    per = tl // A_RADIUS

    def own(section):
        return lambda b, r, l: (b, r, l, section)

    def prev(section):
        return lambda b, r, l: (b, r, jnp.maximum(l * per - 1, 0), section)

    def nxt(section):
        return lambda b, r, l: (b, r, jnp.minimum((l + 1) * per, nblk - 1), section)

    edge = (1, 1, A_RADIUS, A_OUT)
    full = (1, 1, tl, A_OUT)
    return pl.pallas_call(
        functools.partial(_mixer_a_kernel, tl=tl, seq_len=L),
        grid=(B, rate, L // tl),
        in_specs=[
            pl.BlockSpec(full, own(0)),
            pl.BlockSpec(edge, prev(1)), pl.BlockSpec(full, own(1)), pl.BlockSpec(edge, nxt(1)),
            pl.BlockSpec(edge, prev(2)), pl.BlockSpec(full, own(2)), pl.BlockSpec(edge, nxt(2)),
            _const_spec((A_HEADS, A_SPAN, A_UNIT)),
        ],
        out_specs=[pl.BlockSpec(full, own(0)), pl.BlockSpec(full, own(0))],
        out_shape=[jax.ShapeDtypeStruct((B, rate, L, A_OUT), jnp.float32)] * 2,
        scratch_shapes=[pltpu.VMEM((tl + 2 * A_RADIUS, A_OUT), _MXU_DTYPE)] * 2
        + [pltpu.VMEM((A_SPAN, 2 * A_UNIT), jnp.float32)],
        compiler_params=_params(("arbitrary",) * 3),
        name=f"mixer_a_rate{rate}",
    )(za_g, za_g, za_g, za_g, za_g, za_g, za_g, bias)


B_UNIT_ROWS = 4
B_UNIT = B_UNIT_ROWS * GRID_W
B_SPAN_ROWS = B_UNIT_ROWS + B_WIN_ROWS
B_SPAN = B_SPAN_ROWS * GRID_W
B_HALO = (B_WIN_ROWS // 2) * GRID_W
B_TILE_ROWS = 16
B_TILE = B_TILE_ROWS * GRID_W


def _mixer_b_kernel(q_ref, kp_ref, ko_ref, kn_ref, vp_ref, vo_ref, vn_ref, bias_ref,
                    o_ref, kw_ref, vw_ref, *, rows):
    i0 = pl.program_id(1) * B_TILE_ROWS
    kw_ref[0:B_HALO] = kp_ref[0]
    kw_ref[B_HALO:B_HALO + B_TILE] = ko_ref[0]
    kw_ref[B_HALO + B_TILE:] = kn_ref[0]
    vw_ref[:, :, 0:B_HALO] = vp_ref[0]
    vw_ref[:, :, B_HALO:B_HALO + B_TILE] = vo_ref[0]
    vw_ref[:, :, B_HALO + B_TILE:] = vn_ref[0]
    tail = _ones_tail(B_SPAN, _MXU_DTYPE)
    upper = lax.broadcasted_iota(jnp.int32, (LANES, B_UNIT), 0) < HEAD_DIM

    def unit(u, carry):
        i0u = i0 + u * B_UNIT_ROWS
        r0 = jnp.clip(i0u - B_WIN_ROWS // 2, 0, rows - B_SPAN_ROWS)
        off = pl.multiple_of((r0 - (i0 - B_WIN_ROWS // 2)) * GRID_W, LANES)
        variant = jnp.where(i0u == 0, 0, jnp.where(i0u == rows - B_UNIT_ROWS, 2, 1))
        qoff = pl.multiple_of(u * B_UNIT, B_UNIT)

        def scores(pair):
            qt = q_ref[0, pair, :, pl.ds(qoff, B_UNIT)]
            zero = jnp.zeros_like(qt)
            qt2 = jnp.concatenate([jnp.where(upper, qt, zero), jnp.where(upper, zero, qt)], axis=1)
            return _dot(kw_ref[pl.ds(off, B_SPAN), pair * LANES:(pair + 1) * LANES], qt2)

        outs = []
        s_next = scores(0)
        for pair in range(B_PAIRS):
            s_pair = s_next
            if pair + 1 < B_PAIRS:
                s_next = scores(pair + 1)
            for odd in range(2):
                s = s_pair[:, odd * B_UNIT:(odd + 1) * B_UNIT] + bias_ref[variant, 2 * pair + odd]
                p = jnp.exp2(s - jnp.max(s, axis=0, keepdims=True)).astype(_MXU_DTYPE)
                vt = vw_ref[pair, odd * HEAD_DIM:(odd + 1) * HEAD_DIM, pl.ds(off, B_SPAN)]
                acc = _dot(jnp.concatenate([vt, tail], axis=0), p)
                outs.append(acc[:HEAD_DIM] / acc[HEAD_DIM:HEAD_DIM + 1])
        o_ref[0, pl.ds(qoff, B_UNIT), :] = jnp.concatenate(outs, axis=0).T.astype(o_ref.dtype)
        return carry

    lax.fori_loop(0, B_TILE_ROWS // B_UNIT_ROWS, unit, 0)


def _mixer_b(qbt, kb, vbt, bias, B, S):
    rows = S // GRID_W
    nt = rows // B_TILE_ROWS
    per = B_TILE // B_HALO
    nh = S // B_HALO
    kb3 = kb.reshape(B, S, B_W)
    prev = lambda t: jnp.maximum(t * per - 1, 0)
    nxt = lambda t: jnp.minimum((t + 1) * per, nh - 1)
    k_edge, k_own = (1, B_HALO, B_W), (1, B_TILE, B_W)
    t_edge, t_own = (1, B_PAIRS, LANES, B_HALO), (1, B_PAIRS, LANES, B_TILE)
    o = pl.pallas_call(
        functools.partial(_mixer_b_kernel, rows=rows),
        grid=(B, nt),
        in_specs=[
            pl.BlockSpec(t_own, lambda b, t: (b, 0, 0, t)),
            pl.BlockSpec(k_edge, lambda b, t: (b, prev(t), 0)),
            pl.BlockSpec(k_own, lambda b, t: (b, t, 0)),
            pl.BlockSpec(k_edge, lambda b, t: (b, nxt(t), 0)),
            pl.BlockSpec(t_edge, lambda b, t: (b, 0, 0, prev(t))),
            pl.BlockSpec(t_own, lambda b, t: (b, 0, 0, t)),
            pl.BlockSpec(t_edge, lambda b, t: (b, 0, 0, nxt(t))),
            _const_spec(bias.shape),
        ],
        out_specs=pl.BlockSpec(k_own, lambda b, t: (b, t, 0)),
        out_shape=jax.ShapeDtypeStruct((B, S, B_W), _MXU_DTYPE),
        scratch_shapes=[pltpu.VMEM((B_TILE + 2 * B_HALO, B_W), _MXU_DTYPE),
                        pltpu.VMEM((B_PAIRS, LANES, B_TILE + 2 * B_HALO), _MXU_DTYPE)],
        compiler_params=_params(("arbitrary",) * 2),
        name="mixer_b",
    )(qbt, kb3, kb3, kb3, vbt, vbt, vbt, bias)
    return o.reshape(B * S, B_W)


C_TQ = 1024
C_TK = 512
C_UNIT = 512
C_SHIFT_LIMIT = 60.0


def _mixer_c_kernel(q_ref, k_ref, vt_ref, ksq_ref, o_ref, qcat_ref, m_ref, acc_ref, s_ref, *, n_kv):
    acc_ref[...] = jnp.zeros(acc_ref.shape, jnp.float32)
    for g in range(C_GROUP):
        qcat_ref[:, g * C_TQ:(g + 1) * C_TQ] = q_ref[0, g]
    n_units = C_GROUP * C_TQ // C_UNIT

    def keys(j):
        return k_ref[0, 0, pl.ds(pl.multiple_of(j * C_TK, C_TK), C_TK), :]

    def scores(k, u):
        return _dot(k, qcat_ref[:, u * C_UNIT:(u + 1) * C_UNIT])

    def sweep(update):
        s_ref[...] = scores(keys(0), 0)

        def step(j, carry):
            k = keys(j)
            vt = vt_ref[0, 0, :, pl.ds(pl.multiple_of(j * C_TK, C_TK), C_TK)]
            s_next = s_ref[...]
            for u in range(n_units):
                s = s_next
                if u + 1 < n_units:
                    s_next = scores(k, u + 1)
                else:
                    s_ref[...] = scores(keys(jnp.minimum(j + 1, n_kv - 1)), 0)
                update(s, vt, slice(u * C_UNIT, (u + 1) * C_UNIT))
            return carry

        lax.fori_loop(0, n_kv, step, 0)

    def fixed_shift(s, vt, cols):
        acc_ref[:, cols] += _dot(vt, jnp.exp2(s - m_ref[:, cols]).astype(_MXU_DTYPE))

    def running_max(s, vt, cols):
        m_prev = m_ref[:, cols]
        m_new = jnp.maximum(m_prev, jnp.max(s, axis=0, keepdims=True))
        alpha = jnp.exp2(m_prev - m_new)
        p = jnp.exp2(s - m_new).astype(_MXU_DTYPE)
        acc_ref[:, cols] = alpha * acc_ref[:, cols] + _dot(vt, p)
        m_ref[:, cols] = m_new

    qf = qcat_ref[...].astype(jnp.float32)
    bound = jnp.sqrt(jnp.sum(qf * qf, axis=0, keepdims=True) * ksq_ref[0, 0, :, :1])
    small = jnp.max(bound) <= C_SHIFT_LIMIT

    @pl.when(small)
    def _():
        m_ref[...] = bound
        sweep(fixed_shift)

    @pl.when(jnp.logical_not(small))
    def _():
        m_ref[...] = jnp.full(m_ref.shape, -jnp.inf, jnp.float32)
        sweep(running_max)

    o_t = jnp.concatenate(
        [acc_ref[:HEAD_DIM, g * C_TQ:(g + 1) * C_TQ] / acc_ref[HEAD_DIM:HEAD_DIM + 1, g * C_TQ:(g + 1) * C_TQ]
         for g in range(C_GROUP)], axis=0)
    o_ref[0] = o_t.T.astype(o_ref.dtype)


def _mixer_c(qc_t, kc, vc_t, ksq, B, S):
    o = pl.pallas_call(
        functools.partial(_mixer_c_kernel, n_kv=S // C_TK),
        grid=(B, C_KV_HEADS, S // C_TQ),
        in_specs=[
            pl.BlockSpec((1, C_GROUP, HEAD_DIM, C_TQ), lambda b, kv, i: (b, kv, 0, i)),
            pl.BlockSpec((1, 1, S, HEAD_DIM), lambda b, kv, i: (b, kv, 0, 0)),
            pl.BlockSpec((1, 1, VT_ROWS, S), lambda b, kv, i: (b, kv, 0, 0)),
            pl.BlockSpec((1, 1, 1, LANES), lambda b, kv, i: (b, kv, 0, 0)),
        ],
        out_specs=pl.BlockSpec((1, C_TQ, C_GROUP * HEAD_DIM), lambda b, kv, i: (b, i, kv)),
        out_shape=jax.ShapeDtypeStruct((B, S, C_QW), _MXU_DTYPE),
        scratch_shapes=[pltpu.VMEM((HEAD_DIM, C_GROUP * C_TQ), _MXU_DTYPE),
                        pltpu.VMEM((1, C_GROUP * C_TQ), jnp.float32),
                        pltpu.VMEM((VT_ROWS, C_GROUP * C_TQ), jnp.float32),
                        pltpu.VMEM((C_TK, C_UNIT), jnp.float32)],
        compiler_params=_params(("arbitrary",) * 3),
        name="mixer_c",
    )(qc_t, kc, vc_t, ksq)
    return o.reshape(B * S, C_QW)


MERGE_TM = 512


def _merge_kernel(x_ref, g_ref, oa0_ref, oa1_ref, oa2_ref, l0_ref, l1_ref, l2_ref, ob_ref, oc_ref,
                  wg_ref, pa_ref, pb_ref, pc_ref, wo_ref, out_ref, *scratch):
    tm = x_ref.shape[0]

    def token_major(ref, scr):
        rate = ref.shape[1]
        halves = range(A_OUT // LANES)
        for r in range(rate):
            for j in halves:
                scr[j, pl.ds(r, tm // rate, stride=rate), :] = ref[0, r, :, j * LANES:(j + 1) * LANES]
        return jnp.concatenate([scr[j] for j in halves], axis=1)

    x = x_ref[...]
    h = _rms(x, g_ref[...]).astype(_MXU_DTYPE)
    oa0, l0 = oa0_ref[...], l0_ref[...]
    oa1, l1 = token_major(oa1_ref, scratch[0]), token_major(l1_ref, scratch[1])
    oa2, l2 = token_major(oa2_ref, scratch[2]), token_major(l2_ref, scratch[3])
    mx = jnp.maximum(jnp.maximum(l0, l1), l2)
    w0, w1, w2 = jnp.exp2(l0 - mx), jnp.exp2(l1 - mx), jnp.exp2(l2 - mx)
    o_a = (w0 * oa0 + w1 * oa1 + w2 * oa2) / (w0 + w1 + w2)
    merged = jax.nn.sigmoid(_dot(h, wg_ref[:, 0:D_MODEL])) * _dot(o_a.astype(_MXU_DTYPE), pa_ref[...])
    merged += jax.nn.sigmoid(_dot(h, wg_ref[:, D_MODEL:2 * D_MODEL])) * _dot(ob_ref[...], pb_ref[...])
    merged += jax.nn.sigmoid(_dot(h, wg_ref[:, 2 * D_MODEL:3 * D_MODEL])) * _dot(oc_ref[...], pc_ref[...])
    out_ref[...] = x + _dot(merged.astype(_MXU_DTYPE), wo_ref[...])


def _merge(x2, g1, oa, lse, ob, oc, wg, pa, pb, pc, wo, B, S):
    M = x2.shape[0]
    tm = MERGE_TM
    per_seq = S // tm
    row = lambda i: (i, 0)
    tile = lambda w: pl.BlockSpec((tm, w), row)

    def dilated(rate):
        return pl.BlockSpec((1, rate, tm // rate, A_OUT), lambda i: (i // per_seq, 0, i % per_seq, 0))

    r1, r2 = A_PATTERNS[1][1], A_PATTERNS[2][1]
    return pl.pallas_call(
        _merge_kernel,
        grid=(M // tm,),
        in_specs=[tile(D_MODEL), _const_spec((1, D_MODEL)),
                  tile(A_OUT), dilated(r1), dilated(r2), tile(A_OUT), dilated(r1), dilated(r2),
                  tile(B_W), tile(C_QW),
                  _const_spec(wg.shape), _const_spec(pa.shape), _const_spec(pb.shape),
                  _const_spec(pc.shape), _const_spec(wo.shape)],
        out_specs=tile(D_MODEL),
        out_shape=jax.ShapeDtypeStruct((M, D_MODEL), jnp.float32),
        scratch_shapes=[pltpu.VMEM((A_OUT // LANES, tm, LANES), jnp.float32)] * 4,
        compiler_params=_params(("arbitrary",)),
        name="gated_merge",
    )(x2, g1, oa[0].reshape(M, A_OUT), oa[1], oa[2], lse[0].reshape(M, A_OUT), lse[1], lse[2],
      ob, oc, wg, pa, pb, pc, wo)


FFN_TM = 512
FFN_CHUNK = 256


def _ffn_kernel(x_ref, g_ref, wup_ref, wdown_ref, out_ref, act_ref):
    x = x_ref[...]
    xn = _rms(x, g_ref[...]).astype(_MXU_DTYPE)
    for c in range(D_FF // FFN_CHUNK):
        cs = slice(c * FFN_CHUNK, (c + 1) * FFN_CHUNK)
        a = _dot(xn, wup_ref[:, cs])
        b = _dot(xn, wup_ref[:, D_FF + c * FFN_CHUNK:D_FF + (c + 1) * FFN_CHUNK])
        act_ref[:, cs] = (a * jax.nn.sigmoid(a) * b).astype(act_ref.dtype)
    out_ref[...] = x + _dot(act_ref[...], wdown_ref[...])


def _ffn(x2, g2, wup, wdown):
    M = x2.shape[0]
    tm = FFN_TM
    row = lambda i: (i, 0)
    return pl.pallas_call(
        _ffn_kernel,
        grid=(M // tm,),
        in_specs=[pl.BlockSpec((tm, D_MODEL), row), _const_spec((1, D_MODEL)),
                  _const_spec(wup.shape), _const_spec(wdown.shape)],
        out_specs=pl.BlockSpec((tm, D_MODEL), row),
        out_shape=jax.ShapeDtypeStruct((M, D_MODEL), jnp.float32),
        scratch_shapes=[pltpu.VMEM((tm, D_FF), _MXU_DTYPE)],
        compiler_params=_params(("arbitrary",)),
        name="swiglu_ffn",
    )(x2, g2, wup, wdown)


def _t5_bucket(rel):
    half = T5_BUCKETS // 2
    max_exact = half // 2
    ret = jnp.where(rel > 0, half, 0)
    n = jnp.abs(rel)
    nf = jnp.maximum(n, 1).astype(jnp.float32)
    large = max_exact + (jnp.log(nf / max_exact) / math.log(T5_MAX_DIST / max_exact)
                         * (half - max_exact)).astype(jnp.int32)
    large = jnp.minimum(large, half - 1)
    return ret + jnp.where(n < max_exact, n, large)


def _mixer_a_bias(table_g, rate):
    j = jnp.arange(A_SPAN)[:, None]
    i = jnp.arange(A_UNIT)[None, :]
    step = j - A_RADIUS - i
    onehot = (_t5_bucket(step * rate)[:, :, None] == jnp.arange(T5_BUCKETS)).astype(jnp.float32)
    bias = jnp.einsum("jib,bh->hji", onehot, table_g.astype(jnp.float32) * LOG2_E,
                      precision=lax.Precision.HIGHEST)
    return jnp.where((jnp.abs(step) <= A_RADIUS)[None], bias, NEG)


def _mixer_b_bias(rpb, rows):
    c = np.arange(GRID_W)
    c0 = np.clip(c - B_WIN_COLS // 2, 0, GRID_W - B_WIN_COLS)
    col_ok = (c[:, None] >= c0[None, :]) & (c[:, None] < c0[None, :] + B_WIN_COLS)
    dc = np.clip(c[:, None] - c[None, :] + B_WIN_COLS - 1, 0, 2 * B_WIN_COLS - 2)
    n_dr, n_dc = 2 * B_WIN_ROWS - 1, 2 * B_WIN_COLS - 1
    first_query_row = np.array([0, B_WIN_ROWS // 2, rows - B_UNIT_ROWS])
    i = first_query_row[:, None, None] + np.arange(B_UNIT_ROWS)[None, None, :]
    r0 = np.clip(first_query_row - B_WIN_ROWS // 2, 0, rows - B_SPAN_ROWS)
    ik = r0[:, None, None] + np.arange(B_SPAN_ROWS)[None, :, None]
    rs = np.clip(i - B_WIN_ROWS // 2, 0, rows - B_WIN_ROWS)
    row_ok = (ik >= rs) & (ik < rs + B_WIN_ROWS)
    dr = np.clip(ik - i + B_WIN_ROWS - 1, 0, n_dr - 1)
    pick_r = ((dr[..., None] == np.arange(n_dr)) & row_ok[..., None]).astype(np.float32)
    pick_c = ((dc[..., None] == np.arange(n_dc)) & col_ok[..., None]).astype(np.float32)
    big = jnp.einsum("vagd,hdk,xyk->vhaxgy", pick_r, rpb.astype(jnp.float32) * LOG2_E, pick_c,
                     precision=lax.Precision.HIGHEST)
    ok = row_ok[:, None, :, None, :, None] & col_ok[None, None, None, :, None, :]
    return jnp.where(ok, big, NEG).reshape(3, B_HEADS, B_SPAN, B_UNIT)


def _rope_tables(S):
    t = jnp.arange(S)
    inv = ROPE_THETA ** (-jnp.arange(0, ROPE_AXIS_DIM, 2, dtype=jnp.float32) / ROPE_AXIS_DIM)
    d = np.arange(LANES) % HEAD_DIM
    inv_lane = inv[d % (ROPE_AXIS_DIM // 2)][None, :]
    is_col = (d >= ROPE_AXIS_DIM)[None, :]
    pos = jnp.where(is_col, (t % GRID_W)[:, None], (t // GRID_W)[:, None]).astype(jnp.float32)
    ang = pos * inv_lane
    first = ((d % ROPE_AXIS_DIM) < ROPE_AXIS_DIM // 2)[None, :]
    sin = jnp.sin(ang)
    return jnp.cos(ang), jnp.where(first, -sin, 0.0), jnp.where(first, 0.0, sin)


def _block_diag_ones():
    i = np.arange(256) // HEAD_DIM
    return jnp.asarray(i[:, None] == i[None, :], _MXU_DTYPE)


def kernel(x, rel_bias_table, norm1, w_in, qk_gain, nat_rpb, w_br_a, w_br_b, w_br_c, w_o,
           norm2, w_up, w_down):
    B, S, D = x.shape
    depth = w_in.shape[0]
    cd = _MXU_DTYPE
    M = B * S
    x2 = x.reshape(M, D)
    rope = _rope_tables(S)
    ones_bd = _block_diag_ones()
    a_bias = [_mixer_a_bias(rel_bias_table[:, g * A_HEADS:(g + 1) * A_HEADS], rate)
              for g, (_, rate) in enumerate(A_PATTERNS)]
    sec = lambda w, lo, n: w[:, lo:lo + n]
    for l in range(depth):
        w = w_in[l]
        w_qkv = jnp.concatenate(
            [sec(w, _QA, A_W), sec(w, _KA, A_W), sec(w, _QB, B_W), sec(w, _KB, B_W),
             sec(w, _QC, C_QW), sec(w, _KC, C_KVW),
             sec(w, _VA, A_W), sec(w, _VB, B_W), sec(w, _VC, C_KVW)], axis=1).astype(cd)
        gq = qk_gain[l]
        tile = lambda g, n, s: jnp.tile(g * s, n)
        gain_row = jnp.concatenate([
            tile(gq[0], A_W // HEAD_DIM, QK_SCALE * LOG2_E), tile(gq[1], A_W // HEAD_DIM, 1.0),
            tile(gq[2], B_HEADS, QK_SCALE * LOG2_E), tile(gq[3], B_HEADS, 1.0),
            tile(gq[4], C_Q_HEADS, QK_SCALE * LOG2_E), tile(gq[5], C_KV_HEADS, 1.0)])[None, :]
        g1 = norm1[l][None, :]
        za0, za1, za2, qbt, kb, vbt, qc, kc, vc, ksq = _qkv_projection(
            x2, g1, w_qkv, gain_row, ones_bd, rope, B, S)
        ksq = jnp.max(ksq.reshape(B, -1, C_KV_HEADS, HEAD_DIM), axis=(1, 3))
        ksq = jnp.broadcast_to(ksq[:, :, None, None], (B, C_KV_HEADS, 1, LANES))
        za = (za0.reshape(B, 1, S, ZA_W), za1, za2)
        oa, lse = [], []
        for g, (_, rate) in enumerate(A_PATTERNS):
            o_g, l_g = _mixer_a_group(za[g], a_bias[g], rate, B, S)
            oa.append(o_g)
            lse.append(l_g)
        ob = _mixer_b(qbt, kb, vbt, _mixer_b_bias(nat_rpb[l], S // GRID_W), B, S)
        oc = _mixer_c(qc, kc, vc, ksq, B, S)
        x2 = _merge(x2, g1, oa, lse, ob, oc, sec(w, _ZG, N_BRANCH * D).astype(cd),
                    w_br_a[l].astype(cd), w_br_b[l].astype(cd), w_br_c[l].astype(cd), w_o[l].astype(cd), B, S)
        x2 = _ffn(x2, norm2[l][None, :], w_up[l].astype(cd), w_down[l].astype(cd))
    return x2.reshape(B, S, D)
```

```python
import functools
import math

import jax
import jax.numpy as jnp
import numpy as np
from jax import lax
from jax.experimental import pallas as pl
from jax.experimental.pallas import tpu as pltpu

_MXU_DTYPE = jnp.bfloat16

D_MODEL = 1024
HEAD_DIM = 64
GRID_W = 64
RMS_EPS = 1e-6
NEG = -1e30
A_PATTERNS = ((128, 1), (512, 4), (2048, 16))
A_GROUPS = 3
A_HEADS = 4
A_W = A_GROUPS * A_HEADS * HEAD_DIM
A_OUT = A_HEADS * HEAD_DIM
A_RADIUS = 64
B_HEADS = 8
B_W = B_HEADS * HEAD_DIM
B_WIN_ROWS = 8
B_WIN_COLS = 16
C_Q_HEADS = 8
C_KV_HEADS = 2
C_GROUP = C_Q_HEADS // C_KV_HEADS
C_QW = C_Q_HEADS * HEAD_DIM
C_KVW = C_KV_HEADS * HEAD_DIM
ROPE_THETA = 10000.0
ROPE_AXIS_DIM = HEAD_DIM // 2
T5_BUCKETS = 32
T5_MAX_DIST = 1024
N_BRANCH = 3
D_FF = math.ceil(8 * D_MODEL / 3 / 256) * 256
QK_SCALE = HEAD_DIM ** -0.5
LOG2_E = math.log2(math.e)

VMEM_LIMIT_BYTES = 56 * 1024 * 1024
LANES = 128
BF16_SUBLANES = 16
VT_ROWS = HEAD_DIM + BF16_SUBLANES

_OFF = np.cumsum([0, A_W, A_W, A_W, B_W, B_W, B_W, C_QW, C_KVW, C_KVW]).tolist()
(_QA, _KA, _VA, _QB, _KB, _VB, _QC, _KC, _VC, _ZG) = _OFF

ZA_W = 3 * A_OUT
B_PAIRS = B_HEADS // 2


def _params(sem):
    return pltpu.CompilerParams(dimension_semantics=sem, vmem_limit_bytes=VMEM_LIMIT_BYTES)


def _const_spec(shape):
    nd = len(shape)
    return pl.BlockSpec(shape, lambda *_: (0,) * nd, pipeline_mode=pl.Buffered(1))


def _rms(x, g):
    return x * lax.rsqrt(jnp.mean(x * x, axis=-1, keepdims=True) + RMS_EPS) * g


def _dot(a, b):
    return jnp.dot(a, b, preferred_element_type=jnp.float32)


def _dot_nt(a, b):
    return lax.dot_general(a, b, (((1,), (1,)), ((), ())), preferred_element_type=jnp.float32)


def _ones_tail(width, dtype):
    return (lax.broadcasted_iota(jnp.int32, (VT_ROWS - HEAD_DIM, width), 0) == 0).astype(dtype)


PROJ_TM = 512
_N_NORM = 2 * A_W + 2 * B_W
_N_ROPE = C_QW + C_KVW
_N_PLAIN = A_W + B_W + C_KVW


def _proj_kernel(x_ref, g_ref, w_ref, gain_ref, ones_ref, cos_ref, s1_ref, s2_ref,
                 za0_ref, za1_ref, za2_ref, qbt_ref, kb_ref, vbt_ref, qc_ref, kc_ref, vc_ref, ksq_ref,
                 dil_ref):
    tm = x_ref.shape[0]
    h = _rms(x_ref[...], g_ref[...]).astype(_MXU_DTYPE)
    za_refs = (za0_ref, za1_ref, za2_ref)

    def head_norm(acc, c0, width):
        sq = (acc * acc).astype(_MXU_DTYPE)
        ms = _dot(sq, ones_ref[:width, :width]) * (1.0 / HEAD_DIM)
        return acc * lax.rsqrt(ms + RMS_EPS) * gain_ref[:, c0:c0 + width]

    def store_group(g, section, val):
        cols = slice(section * A_OUT, (section + 1) * A_OUT)
        rate = A_PATTERNS[g][1]
        if rate == 1:
            za0_ref[:, cols] = val.astype(za0_ref.dtype)
            return
        for j in range(A_OUT // LANES):
            dil_ref[j] = val[:, j * LANES:(j + 1) * LANES]
        for r in range(rate):
            picked = [dil_ref[j, pl.ds(r, tm // rate, stride=rate), :] for j in range(A_OUT // LANES)]
            za_refs[g][0, r, :, cols] = jnp.concatenate(picked, axis=1).astype(za_refs[g].dtype)

    def store_pairs(ref, t, val):
        vt = val.T.astype(ref.dtype)
        ref[0, 2 * t] = vt[:LANES]
        ref[0, 2 * t + 1] = vt[LANES:]

    def rotary(t, acc, wc):
        y = head_norm(acc, wc, LANES)
        y = y * cos_ref[...] + pltpu.roll(y, LANES - 16, 1) * s1_ref[...] + pltpu.roll(y, 16, 1) * s2_ref[...]
        if t < C_QW // LANES:
            yt = y.T.astype(qc_ref.dtype)
            qc_ref[0, 2 * t] = yt[:HEAD_DIM]
            qc_ref[0, 2 * t + 1] = yt[HEAD_DIM:]
        else:
            y = y.astype(kc_ref.dtype)
            kc_ref[0, 0] = y[:, :HEAD_DIM]
            kc_ref[0, 1] = y[:, HEAD_DIM:]
            norms = _dot(y * y, ones_ref[:LANES, :LANES])
            ksq_ref[0] = jnp.max(norms, axis=0, keepdims=True)

    def store_vc(acc):
        acc_t = acc.T.astype(vc_ref.dtype)
        tail = _ones_tail(tm, vc_ref.dtype)
        for kv in range(C_KV_HEADS):
            vc_ref[0, kv, :HEAD_DIM] = acc_t[kv * HEAD_DIM:(kv + 1) * HEAD_DIM]
            vc_ref[0, kv, HEAD_DIM:] = tail

    jobs = []
    for section in range(2):
        for g in range(A_GROUPS):
            wc = section * A_W + g * A_OUT
            jobs.append((wc, A_OUT, lambda acc, wc=wc, g=g, section=section:
                         store_group(g, section, head_norm(acc, wc, A_OUT))))
    for t in range(B_W // 256):
        wc = 2 * A_W + t * 256
        jobs.append((wc, 256, lambda acc, wc=wc, t=t: store_pairs(qbt_ref, t, head_norm(acc, wc, 256))))
    for t in range(B_W // 256):
        wc = 2 * A_W + B_W + t * 256
        def store_kb(acc, wc=wc, t=t):
            kb_ref[:, t * 256:(t + 1) * 256] = head_norm(acc, wc, 256).astype(kb_ref.dtype)
        jobs.append((wc, 256, store_kb))
    for t in range(_N_ROPE // LANES):
        wc = _N_NORM + t * LANES
        jobs.append((wc, LANES, lambda acc, wc=wc, t=t: rotary(t, acc, wc)))
    base = _N_NORM + _N_ROPE
    for g in range(A_GROUPS):
        jobs.append((base + g * A_OUT, A_OUT, lambda acc, g=g: store_group(g, 2, acc)))
    for t in range(B_W // 256):
        jobs.append((base + A_W + t * 256, 256, lambda acc, t=t: store_pairs(vbt_ref, t, acc)))
    jobs.append((base + A_W + B_W, LANES, store_vc))

    product = lambda n: _dot(h, w_ref[:, jobs[n][0]:jobs[n][0] + jobs[n][1]])
    acc_next = product(0)
    for n, (_, _, consume) in enumerate(jobs):
        acc = acc_next
        if n + 1 < len(jobs):
            acc_next = product(n + 1)
        consume(acc)


def _qkv_projection(x2, g1, w_qkv, gain_row, ones_bd, rope, B, S):
    M = x2.shape[0]
    tm = PROJ_TM
    n_w = w_qkv.shape[1]
    per_seq = S // tm
    cos_t, s1_t, s2_t = rope
    r1, r2 = A_PATTERNS[1][1], A_PATTERNS[2][1]
    row = lambda i: (i, 0)
    pos = lambda i: (i % per_seq, 0)
    hm = lambda i: (i // per_seq, 0, i % per_seq, 0)
    hm_t = lambda i: (i // per_seq, 0, 0, i % per_seq)
    cd = _MXU_DTYPE
    return pl.pallas_call(
        _proj_kernel,
        grid=(M // tm,),
        in_specs=[
            pl.BlockSpec((tm, D_MODEL), row),
            _const_spec((1, D_MODEL)),
            _const_spec((D_MODEL, n_w)),
            _const_spec((1, _N_NORM + _N_ROPE)),
            _const_spec((256, 256)),
            pl.BlockSpec((tm, LANES), pos),
            pl.BlockSpec((tm, LANES), pos),
            pl.BlockSpec((tm, LANES), pos),
        ],
        out_specs=[
            pl.BlockSpec((tm, ZA_W), row),
            pl.BlockSpec((1, r1, tm // r1, ZA_W), hm),
            pl.BlockSpec((1, r2, tm // r2, ZA_W), hm),
            pl.BlockSpec((1, B_PAIRS, LANES, tm), hm_t),
            pl.BlockSpec((tm, B_W), row),
            pl.BlockSpec((1, B_PAIRS, LANES, tm), hm_t),
            pl.BlockSpec((1, C_Q_HEADS, HEAD_DIM, tm), hm_t),
            pl.BlockSpec((1, C_KV_HEADS, tm, HEAD_DIM), hm),
            pl.BlockSpec((1, C_KV_HEADS, VT_ROWS, tm), hm_t),
            pl.BlockSpec((1, 1, LANES), lambda i: (i, 0, 0)),
        ],
        out_shape=[
            jax.ShapeDtypeStruct((M, ZA_W), cd),
            jax.ShapeDtypeStruct((B, r1, S // r1, ZA_W), cd),
            jax.ShapeDtypeStruct((B, r2, S // r2, ZA_W), cd),
            jax.ShapeDtypeStruct((B, B_PAIRS, LANES, S), cd),
            jax.ShapeDtypeStruct((M, B_W), cd),
            jax.ShapeDtypeStruct((B, B_PAIRS, LANES, S), cd),
            jax.ShapeDtypeStruct((B, C_Q_HEADS, HEAD_DIM, S), cd),
            jax.ShapeDtypeStruct((B, C_KV_HEADS, S, HEAD_DIM), cd),
            jax.ShapeDtypeStruct((B, C_KV_HEADS, VT_ROWS, S), cd),
            jax.ShapeDtypeStruct((M // tm, 1, LANES), jnp.float32),
        ],
        scratch_shapes=[pltpu.VMEM((A_OUT // LANES, tm, LANES), jnp.float32)],
        compiler_params=_params(("arbitrary",)),
        name="qkv_projection",
    )(x2, g1, w_qkv, gain_row, ones_bd, cos_t, s1_t, s2_t)


A_UNIT = 4 * A_RADIUS
A_SPAN = A_UNIT + 2 * A_RADIUS


def _mixer_a_kernel(q_ref, kp_ref, ko_ref, kn_ref, vp_ref, vo_ref, vn_ref, bias_ref,
                    o_ref, lse_ref, kw_ref, vw_ref, s_ref, *, tl, seq_len):
    l0 = pl.program_id(2) * tl
    kw_ref[0:A_RADIUS] = kp_ref[0, 0]
    kw_ref[A_RADIUS:A_RADIUS + tl] = ko_ref[0, 0]
    kw_ref[A_RADIUS + tl:] = kn_ref[0, 0]
    vw_ref[0:A_RADIUS] = vp_ref[0, 0]
    vw_ref[A_RADIUS:A_RADIUS + tl] = vo_ref[0, 0]
    vw_ref[A_RADIUS + tl:] = vn_ref[0, 0]
    even_q = lax.broadcasted_iota(jnp.int32, (A_UNIT, LANES), 1) < HEAD_DIM
    lane_k = lax.broadcasted_iota(jnp.int32, (A_SPAN, LANES), 1)

    n_units = tl // A_UNIT

    def scores(u, pair):
        off = pl.multiple_of(u * A_UNIT, A_UNIT)
        q = q_ref[0, 0, pl.ds(off, A_UNIT), pair * LANES:(pair + 1) * LANES]
        zero = jnp.zeros_like(q)
        q2 = jnp.concatenate([jnp.where(even_q, q, zero), jnp.where(even_q, zero, q)], axis=0)
        return _dot_nt(kw_ref[pl.ds(off, A_SPAN), pair * LANES:(pair + 1) * LANES], q2)

    s_ref[...] = scores(0, 0)

    def unit(u, carry):
        off = pl.multiple_of(u * A_UNIT, A_UNIT)
        top_ok = l0 + off - A_RADIUS >= 0
        bot_ok = l0 + off + A_UNIT + A_RADIUS <= seq_len
        outs, lses = [], []
        s_next = s_ref[...]
        for pair in range(A_HEADS // 2):
            s_pair = s_next
            if pair + 1 < A_HEADS // 2:
                s_next = scores(u, pair + 1)
            else:
                s_ref[...] = scores(jnp.minimum(u + 1, n_units - 1), 0)
            v_slab = vw_ref[pl.ds(off, A_SPAN), pair * LANES:(pair + 1) * LANES]
            for odd in range(2):
                s = s_pair[:, odd * A_UNIT:(odd + 1) * A_UNIT] + bias_ref[2 * pair + odd]
                s = jnp.concatenate([jnp.where(top_ok, s[:A_RADIUS], NEG), s[A_RADIUS:A_SPAN - A_RADIUS],
                                     jnp.where(bot_ok, s[A_SPAN - A_RADIUS:], NEG)], axis=0)
                m = jnp.max(s, axis=0, keepdims=True)
                p = jnp.exp2(s - m).astype(_MXU_DTYPE)
                den_row = (1 - odd) * HEAD_DIM
                mine = (lane_k >= odd * HEAD_DIM) & (lane_k < (odd + 1) * HEAD_DIM)
                v_aug = jnp.where(mine, v_slab, (lane_k == den_row).astype(v_slab.dtype))
                acc = lax.dot_general(v_aug, p, (((0,), (0,)), ((), ())),
                                      preferred_element_type=jnp.float32)
                den = acc[den_row:den_row + 1]
                outs.append(acc[odd * HEAD_DIM:(odd + 1) * HEAD_DIM] / den)
                lses.append(jnp.broadcast_to(m + jnp.log2(den), (HEAD_DIM, A_UNIT)))
        o_ref[0, 0, pl.ds(off, A_UNIT), :] = jnp.concatenate(outs, axis=0).T
        lse_ref[0, 0, pl.ds(off, A_UNIT), :] = jnp.concatenate(lses, axis=0).T
        return carry

    lax.fori_loop(0, n_units, unit, 0)


def _mixer_a_group(za_g, bias, rate, B, S):
    L = S // rate
    tl = min(1024, L)
    nblk = L // A_RADIUS
    per = tl // A_RADIUS

    def own(section):
        return lambda b, r, l: (b, r, l, section)

    def prev(section):
        return lambda b, r, l: (b, r, jnp.maximum(l * per - 1, 0), section)

    def nxt(section):
        return lambda b, r, l: (b, r, jnp.minimum((l + 1) * per, nblk - 1), section)

    edge = (1, 1, A_RADIUS, A_OUT)
    full = (1, 1, tl, A_OUT)
    return pl.pallas_call(
        functools.partial(_mixer_a_kernel, tl=tl, seq_len=L),
        grid=(B, rate, L // tl),
        in_specs=[
            pl.BlockSpec(full, own(0)),
            pl.BlockSpec(edge, prev(1)), pl.BlockSpec(full, own(1)), pl.BlockSpec(edge, nxt(1)),
            pl.BlockSpec(edge, prev(2)), pl.BlockSpec(full, own(2)), pl.BlockSpec(edge, nxt(2)),
            _const_spec((A_HEADS, A_SPAN, A_UNIT)),
        ],
        out_specs=[pl.BlockSpec(full, own(0)), pl.BlockSpec(full, own(0))],
        out_shape=[jax.ShapeDtypeStruct((B, rate, L, A_OUT), jnp.float32)] * 2,
        scratch_shapes=[pltpu.VMEM((tl + 2 * A_RADIUS, A_OUT), _MXU_DTYPE)] * 2
        + [pltpu.VMEM((A_SPAN, 2 * A_UNIT), jnp.float32)],
        compiler_params=_params(("arbitrary",) * 3),
        name=f"mixer_a_rate{rate}",
    )(za_g, za_g, za_g, za_g, za_g, za_g, za_g, bias)


B_UNIT_ROWS = 4
B_UNIT = B_UNIT_ROWS * GRID_W
B_SPAN_ROWS = B_UNIT_ROWS + B_WIN_ROWS
B_SPAN = B_SPAN_ROWS * GRID_W
B_HALO = (B_WIN_ROWS // 2) * GRID_W
B_TILE_ROWS = 16
B_TILE = B_TILE_ROWS * GRID_W


def _mixer_b_kernel(q_ref, kp_ref, ko_ref, kn_ref, vp_ref, vo_ref, vn_ref, blocks_ref, sel_ref,
                    o_ref, kw_ref, vw_ref, bias_ref, *, rows):
    @pl.when((pl.program_id(0) == 0) & (pl.program_id(1) == 0))
    def _():
        left = lax.broadcasted_iota(jnp.int32, (GRID_W, LANES), 1) < GRID_W

        def fill(n, carry):
            v, h = n // B_HEADS, n % B_HEADS
            for a in range(B_SPAN_ROWS):
                for gp in range(B_UNIT_ROWS // 2):
                    at = (v * B_SPAN_ROWS + a) * B_UNIT_ROWS + 2 * gp
                    tile = jnp.where(left, blocks_ref[h, sel_ref[at]], blocks_ref[h, sel_ref[at + 1]])
                    bias_ref[v, h, a * GRID_W:(a + 1) * GRID_W, gp * LANES:(gp + 1) * LANES] = tile
            return carry

        lax.fori_loop(0, 3 * B_HEADS, fill, 0)

    i0 = pl.program_id(1) * B_TILE_ROWS
    kw_ref[0:B_HALO] = kp_ref[0]
    kw_ref[B_HALO:B_HALO + B_TILE] = ko_ref[0]
    kw_ref[B_HALO + B_TILE:] = kn_ref[0]
    vw_ref[:, :, 0:B_HALO] = vp_ref[0]
    vw_ref[:, :, B_HALO:B_HALO + B_TILE] = vo_ref[0]
    vw_ref[:, :, B_HALO + B_TILE:] = vn_ref[0]
    tail = _ones_tail(B_SPAN, _MXU_DTYPE)
    upper = lax.broadcasted_iota(jnp.int32, (LANES, B_UNIT), 0) < HEAD_DIM

    def unit(u, carry):
        i0u = i0 + u * B_UNIT_ROWS
        r0 = jnp.clip(i0u - B_WIN_ROWS // 2, 0, rows - B_SPAN_ROWS)
        off = pl.multiple_of((r0 - (i0 - B_WIN_ROWS // 2)) * GRID_W, LANES)
        variant = jnp.where(i0u == 0, 0, jnp.where(i0u == rows - B_UNIT_ROWS, 2, 1))
        qoff = pl.multiple_of(u * B_UNIT, B_UNIT)

        def scores(pair):
            qt = q_ref[0, pair, :, pl.ds(qoff, B_UNIT)]
            zero = jnp.zeros_like(qt)
            qt2 = jnp.concatenate([jnp.where(upper, qt, zero), jnp.where(upper, zero, qt)], axis=1)
            return _dot(kw_ref[pl.ds(off, B_SPAN), pair * LANES:(pair + 1) * LANES], qt2)

        outs = []
        s_next = scores(0)
        for pair in range(B_PAIRS):
            s_pair = s_next
            if pair + 1 < B_PAIRS:
                s_next = scores(pair + 1)
            for odd in range(2):
                s = s_pair[:, odd * B_UNIT:(odd + 1) * B_UNIT] + bias_ref[variant, 2 * pair + odd]
                p = jnp.exp2(s - jnp.max(s, axis=0, keepdims=True)).astype(_MXU_DTYPE)
                vt = vw_ref[pair, odd * HEAD_DIM:(odd + 1) * HEAD_DIM, pl.ds(off, B_SPAN)]
                acc = _dot(jnp.concatenate([vt, tail], axis=0), p)
                outs.append(acc[:HEAD_DIM] / acc[HEAD_DIM:HEAD_DIM + 1])
        o_ref[0, pl.ds(qoff, B_UNIT), :] = jnp.concatenate(outs, axis=0).T.astype(o_ref.dtype)
        return carry

    lax.fori_loop(0, B_TILE_ROWS // B_UNIT_ROWS, unit, 0)


def _mixer_b(qbt, kb, vbt, blocks, B, S):
    rows = S // GRID_W
    sel = jnp.asarray(_mixer_b_block_index(rows))
    nt = rows // B_TILE_ROWS
    per = B_TILE // B_HALO
    nh = S // B_HALO
    kb3 = kb.reshape(B, S, B_W)
    prev = lambda t: jnp.maximum(t * per - 1, 0)
    nxt = lambda t: jnp.minimum((t + 1) * per, nh - 1)
    k_edge, k_own = (1, B_HALO, B_W), (1, B_TILE, B_W)
    t_edge, t_own = (1, B_PAIRS, LANES, B_HALO), (1, B_PAIRS, LANES, B_TILE)
    o = pl.pallas_call(
        functools.partial(_mixer_b_kernel, rows=rows),
        grid=(B, nt),
        in_specs=[
            pl.BlockSpec(t_own, lambda b, t: (b, 0, 0, t)),
            pl.BlockSpec(k_edge, lambda b, t: (b, prev(t), 0)),
            pl.BlockSpec(k_own, lambda b, t: (b, t, 0)),
            pl.BlockSpec(k_edge, lambda b, t: (b, nxt(t), 0)),
            pl.BlockSpec(t_edge, lambda b, t: (b, 0, 0, prev(t))),
            pl.BlockSpec(t_own, lambda b, t: (b, 0, 0, t)),
            pl.BlockSpec(t_edge, lambda b, t: (b, 0, 0, nxt(t))),
            _const_spec(blocks.shape),
            pl.BlockSpec(memory_space=pltpu.SMEM),
        ],
        out_specs=pl.BlockSpec(k_own, lambda b, t: (b, t, 0)),
        out_shape=jax.ShapeDtypeStruct((B, S, B_W), _MXU_DTYPE),
        scratch_shapes=[pltpu.VMEM((B_TILE + 2 * B_HALO, B_W), _MXU_DTYPE),
                        pltpu.VMEM((B_PAIRS, LANES, B_TILE + 2 * B_HALO), _MXU_DTYPE),
                        pltpu.VMEM((3, B_HEADS, B_SPAN, B_UNIT), jnp.float32)],
        compiler_params=_params(("arbitrary",) * 2),
        name="mixer_b",
    )(qbt, kb3, kb3, kb3, vbt, vbt, vbt, blocks, sel)
    return o.reshape(B * S, B_W)


C_TQ = 1024
C_TK = 512
C_UNIT = 512
C_SHIFT_LIMIT = 60.0


def _mixer_c_kernel(q_ref, k_ref, vt_ref, ksq_ref, o_ref, qcat_ref, m_ref, acc_ref, s_ref, *, n_kv):
    acc_ref[...] = jnp.zeros(acc_ref.shape, jnp.float32)
    for g in range(C_GROUP):
        qcat_ref[:, g * C_TQ:(g + 1) * C_TQ] = q_ref[0, g]
    n_units = C_GROUP * C_TQ // C_UNIT

    def keys(j):
        return k_ref[0, 0, pl.ds(pl.multiple_of(j * C_TK, C_TK), C_TK), :]

    def scores(k, u):
        return _dot(k, qcat_ref[:, u * C_UNIT:(u + 1) * C_UNIT])

    def sweep(update):
        s_ref[...] = scores(keys(0), 0)

        def step(j, carry):
            k = keys(j)
            vt = vt_ref[0, 0, :, pl.ds(pl.multiple_of(j * C_TK, C_TK), C_TK)]
            s_next = s_ref[...]
            for u in range(n_units):
                s = s_next
                if u + 1 < n_units:
                    s_next = scores(k, u + 1)
                else:
                    s_ref[...] = scores(keys(jnp.minimum(j + 1, n_kv - 1)), 0)
                update(s, vt, slice(u * C_UNIT, (u + 1) * C_UNIT))
            return carry

        lax.fori_loop(0, n_kv, step, 0)

    def fixed_shift(s, vt, cols):
        acc_ref[:, cols] += _dot(vt, jnp.exp2(s - m_ref[:, cols]).astype(_MXU_DTYPE))

    def running_max(s, vt, cols):
        m_prev = m_ref[:, cols]
        m_new = jnp.maximum(m_prev, jnp.max(s, axis=0, keepdims=True))
        alpha = jnp.exp2(m_prev - m_new)
        p = jnp.exp2(s - m_new).astype(_MXU_DTYPE)
        acc_ref[:, cols] = alpha * acc_ref[:, cols] + _dot(vt, p)
        m_ref[:, cols] = m_new

    qf = qcat_ref[...].astype(jnp.float32)
    bound = jnp.sqrt(jnp.sum(qf * qf, axis=0, keepdims=True) * ksq_ref[0, 0, :, :1])
    small = jnp.max(bound) <= C_SHIFT_LIMIT

    @pl.when(small)
    def _():
        m_ref[...] = bound
        sweep(fixed_shift)

    @pl.when(jnp.logical_not(small))
    def _():
        m_ref[...] = jnp.full(m_ref.shape, -jnp.inf, jnp.float32)
        sweep(running_max)

    o_t = jnp.concatenate(
        [acc_ref[:HEAD_DIM, g * C_TQ:(g + 1) * C_TQ] / acc_ref[HEAD_DIM:HEAD_DIM + 1, g * C_TQ:(g + 1) * C_TQ]
         for g in range(C_GROUP)], axis=0)
    o_ref[0] = o_t.T.astype(o_ref.dtype)


def _mixer_c(qc_t, kc, vc_t, ksq, B, S):
    o = pl.pallas_call(
        functools.partial(_mixer_c_kernel, n_kv=S // C_TK),
        grid=(B, C_KV_HEADS, S // C_TQ),
        in_specs=[
            pl.BlockSpec((1, C_GROUP, HEAD_DIM, C_TQ), lambda b, kv, i: (b, kv, 0, i)),
            pl.BlockSpec((1, 1, S, HEAD_DIM), lambda b, kv, i: (b, kv, 0, 0)),
            pl.BlockSpec((1, 1, VT_ROWS, S), lambda b, kv, i: (b, kv, 0, 0)),
            pl.BlockSpec((1, 1, 1, LANES), lambda b, kv, i: (b, kv, 0, 0)),
        ],
        out_specs=pl.BlockSpec((1, C_TQ, C_GROUP * HEAD_DIM), lambda b, kv, i: (b, i, kv)),
        out_shape=jax.ShapeDtypeStruct((B, S, C_QW), _MXU_DTYPE),
        scratch_shapes=[pltpu.VMEM((HEAD_DIM, C_GROUP * C_TQ), _MXU_DTYPE),
                        pltpu.VMEM((1, C_GROUP * C_TQ), jnp.float32),
                        pltpu.VMEM((VT_ROWS, C_GROUP * C_TQ), jnp.float32),
                        pltpu.VMEM((C_TK, C_UNIT), jnp.float32)],
        compiler_params=_params(("arbitrary",) * 3),
        name="mixer_c",
    )(qc_t, kc, vc_t, ksq)
    return o.reshape(B * S, C_QW)


MERGE_TM = 512


def _merge_kernel(x_ref, g_ref, oa0_ref, oa1_ref, oa2_ref, l0_ref, l1_ref, l2_ref, ob_ref, oc_ref,
                  wg_ref, pa_ref, pb_ref, pc_ref, wo_ref, out_ref, *scratch):
    tm = x_ref.shape[0]

    def token_major(ref, scr):
        rate = ref.shape[1]
        halves = range(A_OUT // LANES)
        for r in range(rate):
            for j in halves:
                scr[j, pl.ds(r, tm // rate, stride=rate), :] = ref[0, r, :, j * LANES:(j + 1) * LANES]
        return jnp.concatenate([scr[j] for j in halves], axis=1)

    x = x_ref[...]
    h = _rms(x, g_ref[...]).astype(_MXU_DTYPE)
    oa0, l0 = oa0_ref[...], l0_ref[...]
    oa1, l1 = token_major(oa1_ref, scratch[0]), token_major(l1_ref, scratch[1])
    oa2, l2 = token_major(oa2_ref, scratch[2]), token_major(l2_ref, scratch[3])
    mx = jnp.maximum(jnp.maximum(l0, l1), l2)
    w0, w1, w2 = jnp.exp2(l0 - mx), jnp.exp2(l1 - mx), jnp.exp2(l2 - mx)
    o_a = (w0 * oa0 + w1 * oa1 + w2 * oa2) / (w0 + w1 + w2)
    merged = jax.nn.sigmoid(_dot(h, wg_ref[:, 0:D_MODEL])) * _dot(o_a.astype(_MXU_DTYPE), pa_ref[...])
    merged += jax.nn.sigmoid(_dot(h, wg_ref[:, D_MODEL:2 * D_MODEL])) * _dot(ob_ref[...], pb_ref[...])
    merged += jax.nn.sigmoid(_dot(h, wg_ref[:, 2 * D_MODEL:3 * D_MODEL])) * _dot(oc_ref[...], pc_ref[...])
    out_ref[...] = x + _dot(merged.astype(_MXU_DTYPE), wo_ref[...])


def _merge(x2, g1, oa, lse, ob, oc, wg, pa, pb, pc, wo, B, S):
    M = x2.shape[0]
    tm = MERGE_TM
    per_seq = S // tm
    row = lambda i: (i, 0)
    tile = lambda w: pl.BlockSpec((tm, w), row)

    def dilated(rate):
        return pl.BlockSpec((1, rate, tm // rate, A_OUT), lambda i: (i // per_seq, 0, i % per_seq, 0))

    r1, r2 = A_PATTERNS[1][1], A_PATTERNS[2][1]
    return pl.pallas_call(
        _merge_kernel,
        grid=(M // tm,),
        in_specs=[tile(D_MODEL), _const_spec((1, D_MODEL)),
                  tile(A_OUT), dilated(r1), dilated(r2), tile(A_OUT), dilated(r1), dilated(r2),
                  tile(B_W), tile(C_QW),
                  _const_spec(wg.shape), _const_spec(pa.shape), _const_spec(pb.shape),
                  _const_spec(pc.shape), _const_spec(wo.shape)],
        out_specs=tile(D_MODEL),
        out_shape=jax.ShapeDtypeStruct((M, D_MODEL), jnp.float32),
        scratch_shapes=[pltpu.VMEM((A_OUT // LANES, tm, LANES), jnp.float32)] * 4,
        compiler_params=_params(("arbitrary",)),
        name="gated_merge",
    )(x2, g1, oa[0].reshape(M, A_OUT), oa[1], oa[2], lse[0].reshape(M, A_OUT), lse[1], lse[2],
      ob, oc, wg, pa, pb, pc, wo)


FFN_TM = 512
FFN_CHUNK = 256


def _ffn_kernel(x_ref, g_ref, wup_ref, wdown_ref, out_ref, act_ref):
    x = x_ref[...]
    xn = _rms(x, g_ref[...]).astype(_MXU_DTYPE)
    for c in range(D_FF // FFN_CHUNK):
        cs = slice(c * FFN_CHUNK, (c + 1) * FFN_CHUNK)
        a = _dot(xn, wup_ref[:, cs])
        b = _dot(xn, wup_ref[:, D_FF + c * FFN_CHUNK:D_FF + (c + 1) * FFN_CHUNK])
        act_ref[:, cs] = (a * jax.nn.sigmoid(a) * b).astype(act_ref.dtype)
    out_ref[...] = x + _dot(act_ref[...], wdown_ref[...])


def _ffn(x2, g2, wup, wdown):
    M = x2.shape[0]
    tm = FFN_TM
    row = lambda i: (i, 0)
    return pl.pallas_call(
        _ffn_kernel,
        grid=(M // tm,),
        in_specs=[pl.BlockSpec((tm, D_MODEL), row), _const_spec((1, D_MODEL)),
                  _const_spec(wup.shape), _const_spec(wdown.shape)],
        out_specs=pl.BlockSpec((tm, D_MODEL), row),
        out_shape=jax.ShapeDtypeStruct((M, D_MODEL), jnp.float32),
        scratch_shapes=[pltpu.VMEM((tm, D_FF), _MXU_DTYPE)],
        compiler_params=_params(("arbitrary",)),
        name="swiglu_ffn",
    )(x2, g2, wup, wdown)


def _t5_bucket(rel):
    half = T5_BUCKETS // 2
    max_exact = half // 2
    ret = jnp.where(rel > 0, half, 0)
    n = jnp.abs(rel)
    nf = jnp.maximum(n, 1).astype(jnp.float32)
    large = max_exact + (jnp.log(nf / max_exact) / math.log(T5_MAX_DIST / max_exact)
                         * (half - max_exact)).astype(jnp.int32)
    large = jnp.minimum(large, half - 1)
    return ret + jnp.where(n < max_exact, n, large)


def _mixer_a_bias(table_g, rate):
    j = jnp.arange(A_SPAN)[:, None]
    i = jnp.arange(A_UNIT)[None, :]
    step = j - A_RADIUS - i
    onehot = (_t5_bucket(step * rate)[:, :, None] == jnp.arange(T5_BUCKETS)).astype(jnp.float32)
    bias = jnp.einsum("jib,bh->hji", onehot, table_g.astype(jnp.float32) * LOG2_E,
                      precision=lax.Precision.HIGHEST)
    return jnp.where((jnp.abs(step) <= A_RADIUS)[None], bias, NEG)


B_N_DR = 2 * B_WIN_ROWS - 1


def _mixer_b_blocks(rpb):
    c = np.arange(GRID_W)
    c0 = np.clip(c - B_WIN_COLS // 2, 0, GRID_W - B_WIN_COLS)
    col_ok = (c[:, None] >= c0[None, :]) & (c[:, None] < c0[None, :] + B_WIN_COLS)
    dc = np.clip(c[:, None] - c[None, :] + B_WIN_COLS - 1, 0, 2 * B_WIN_COLS - 2)
    pick_c = ((dc[..., None] == np.arange(2 * B_WIN_COLS - 1)) & col_ok[..., None]).astype(np.float32)
    blocks = jnp.einsum("hdk,xyk->hdxy", rpb.astype(jnp.float32) * LOG2_E, pick_c,
                        precision=lax.Precision.HIGHEST)
    blocks = jnp.where(col_ok[None, None], blocks, NEG)
    blocks = jnp.concatenate([blocks, jnp.full((B_HEADS, 1, GRID_W, GRID_W), NEG, jnp.float32)], axis=1)
    return jnp.concatenate([blocks, blocks], axis=-1)


def _mixer_b_block_index(rows):
    first_query_row = np.array([0, B_WIN_ROWS // 2, rows - B_UNIT_ROWS])
    i = first_query_row[:, None, None] + np.arange(B_UNIT_ROWS)[None, None, :]
    r0 = np.clip(first_query_row - B_WIN_ROWS // 2, 0, rows - B_SPAN_ROWS)
    ik = r0[:, None, None] + np.arange(B_SPAN_ROWS)[None, :, None]
    rs = np.clip(i - B_WIN_ROWS // 2, 0, rows - B_WIN_ROWS)
    row_ok = (ik >= rs) & (ik < rs + B_WIN_ROWS)
    return np.where(row_ok, ik - i + B_WIN_ROWS - 1, B_N_DR).astype(np.int32).reshape(-1)


def _rope_tables(S):
    t = jnp.arange(S)
    inv = ROPE_THETA ** (-jnp.arange(0, ROPE_AXIS_DIM, 2, dtype=jnp.float32) / ROPE_AXIS_DIM)
    d = np.arange(LANES) % HEAD_DIM
    inv_lane = inv[d % (ROPE_AXIS_DIM // 2)][None, :]
    is_col = (d >= ROPE_AXIS_DIM)[None, :]
    pos = jnp.where(is_col, (t % GRID_W)[:, None], (t // GRID_W)[:, None]).astype(jnp.float32)
    ang = pos * inv_lane
    first = ((d % ROPE_AXIS_DIM) < ROPE_AXIS_DIM // 2)[None, :]
    sin = jnp.sin(ang)
    return jnp.cos(ang), jnp.where(first, -sin, 0.0), jnp.where(first, 0.0, sin)


def _block_diag_ones():
    i = np.arange(256) // HEAD_DIM
    return jnp.asarray(i[:, None] == i[None, :], _MXU_DTYPE)


def kernel(x, rel_bias_table, norm1, w_in, qk_gain, nat_rpb, w_br_a, w_br_b, w_br_c, w_o,
           norm2, w_up, w_down):
    B, S, D = x.shape
    depth = w_in.shape[0]
    cd = _MXU_DTYPE
    M = B * S
    x2 = x.reshape(M, D)
    rope = _rope_tables(S)
    ones_bd = _block_diag_ones()
    a_bias = [_mixer_a_bias(rel_bias_table[:, g * A_HEADS:(g + 1) * A_HEADS], rate)
              for g, (_, rate) in enumerate(A_PATTERNS)]
    sec = lambda w, lo, n: w[:, lo:lo + n]
    for l in range(depth):
        w = w_in[l]
        w_qkv = jnp.concatenate(
            [sec(w, _QA, A_W), sec(w, _KA, A_W), sec(w, _QB, B_W), sec(w, _KB, B_W),
             sec(w, _QC, C_QW), sec(w, _KC, C_KVW),
             sec(w, _VA, A_W), sec(w, _VB, B_W), sec(w, _VC, C_KVW)], axis=1).astype(cd)
        gq = qk_gain[l]
        tile = lambda g, n, s: jnp.tile(g * s, n)
        gain_row = jnp.concatenate([
            tile(gq[0], A_W // HEAD_DIM, QK_SCALE * LOG2_E), tile(gq[1], A_W // HEAD_DIM, 1.0),
            tile(gq[2], B_HEADS, QK_SCALE * LOG2_E), tile(gq[3], B_HEADS, 1.0),
            tile(gq[4], C_Q_HEADS, QK_SCALE * LOG2_E), tile(gq[5], C_KV_HEADS, 1.0)])[None, :]
        g1 = norm1[l][None, :]
        za0, za1, za2, qbt, kb, vbt, qc, kc, vc, ksq = _qkv_projection(
            x2, g1, w_qkv, gain_row, ones_bd, rope, B, S)
        ksq = jnp.max(ksq.reshape(B, -1, C_KV_HEADS, HEAD_DIM), axis=(1, 3))
        ksq = jnp.broadcast_to(ksq[:, :, None, None], (B, C_KV_HEADS, 1, LANES))
        za = (za0.reshape(B, 1, S, ZA_W), za1, za2)
        oa, lse = [], []
        for g, (_, rate) in enumerate(A_PATTERNS):
            o_g, l_g = _mixer_a_group(za[g], a_bias[g], rate, B, S)
            oa.append(o_g)
            lse.append(l_g)
        ob = _mixer_b(qbt, kb, vbt, _mixer_b_blocks(nat_rpb[l]), B, S)
        oc = _mixer_c(qc, kc, vc, ksq, B, S)
        x2 = _merge(x2, g1, oa, lse, ob, oc, sec(w, _ZG, N_BRANCH * D).astype(cd),
                    w_br_a[l].astype(cd), w_br_b[l].astype(cd), w_br_c[l].astype(cd), w_o[l].astype(cd), B, S)
        x2 = _ffn(x2, norm2[l][None, :], w_up[l].astype(cd), w_down[l].astype(cd))
    return x2.reshape(B, S, D)
```

```python
import functools
import math

import jax
import jax.numpy as jnp
import numpy as np
from jax import lax
from jax.experimental import pallas as pl
from jax.experimental.pallas import tpu as pltpu

_MXU_DTYPE = jnp.bfloat16

D_MODEL = 1024
HEAD_DIM = 64
GRID_W = 64
RMS_EPS = 1e-6
NEG = -1e30
A_PATTERNS = ((128, 1), (512, 4), (2048, 16))
A_GROUPS = 3
A_HEADS = 4
A_W = A_GROUPS * A_HEADS * HEAD_DIM
A_OUT = A_HEADS * HEAD_DIM
A_RADIUS = 64
B_HEADS = 8
B_W = B_HEADS * HEAD_DIM
B_WIN_ROWS = 8
B_WIN_COLS = 16
C_Q_HEADS = 8
C_KV_HEADS = 2
C_GROUP = C_Q_HEADS // C_KV_HEADS
C_QW = C_Q_HEADS * HEAD_DIM
C_KVW = C_KV_HEADS * HEAD_DIM
ROPE_THETA = 10000.0
ROPE_AXIS_DIM = HEAD_DIM // 2
T5_BUCKETS = 32
T5_MAX_DIST = 1024
N_BRANCH = 3
D_FF = math.ceil(8 * D_MODEL / 3 / 256) * 256
QK_SCALE = HEAD_DIM ** -0.5
LOG2_E = math.log2(math.e)

VMEM_LIMIT_BYTES = 56 * 1024 * 1024
LANES = 128
BF16_SUBLANES = 16
VT_ROWS = HEAD_DIM + BF16_SUBLANES

_OFF = np.cumsum([0, A_W, A_W, A_W, B_W, B_W, B_W, C_QW, C_KVW, C_KVW]).tolist()
(_QA, _KA, _VA, _QB, _KB, _VB, _QC, _KC, _VC, _ZG) = _OFF

ZA_W = 3 * A_OUT
B_PAIRS = B_HEADS // 2


def _params(sem):
    return pltpu.CompilerParams(dimension_semantics=sem, vmem_limit_bytes=VMEM_LIMIT_BYTES)


def _const_spec(shape):
    nd = len(shape)
    return pl.BlockSpec(shape, lambda *_: (0,) * nd, pipeline_mode=pl.Buffered(1))


def _rms(x, g):
    return x * lax.rsqrt(jnp.mean(x * x, axis=-1, keepdims=True) + RMS_EPS) * g


def _dot(a, b):
    return jnp.dot(a, b, preferred_element_type=jnp.float32)


def _dot_nt(a, b):
    return lax.dot_general(a, b, (((1,), (1,)), ((), ())), preferred_element_type=jnp.float32)


def _ones_tail(width, dtype):
    return (lax.broadcasted_iota(jnp.int32, (VT_ROWS - HEAD_DIM, width), 0) == 0).astype(dtype)


PROJ_TM = 512
_N_NORM = 2 * A_W + 2 * B_W
_N_ROPE = C_QW + C_KVW
_N_PLAIN = A_W + B_W + C_KVW


def _proj_kernel(x_ref, g_ref, w_ref, gain_ref, ones_ref, cos_ref, s1_ref, s2_ref,
                 za0_ref, za1_ref, za2_ref, qbt_ref, kb_ref, vbt_ref, qc_ref, kc_ref, vc_ref, ksq_ref,
                 dil_ref):
    tm = x_ref.shape[0]
    h = _rms(x_ref[...], g_ref[...]).astype(_MXU_DTYPE)
    za_refs = (za0_ref, za1_ref, za2_ref)

    def head_norm(acc, c0, width):
        sq = (acc * acc).astype(_MXU_DTYPE)
        ms = _dot(sq, ones_ref[:width, :width]) * (1.0 / HEAD_DIM)
        return acc * lax.rsqrt(ms + RMS_EPS) * gain_ref[:, c0:c0 + width]

    def store_group(g, section, val):
        cols = slice(section * A_OUT, (section + 1) * A_OUT)
        rate = A_PATTERNS[g][1]
        if rate == 1:
            za0_ref[:, cols] = val.astype(za0_ref.dtype)
            return
        for j in range(A_OUT // LANES):
            dil_ref[j] = val[:, j * LANES:(j + 1) * LANES]
        for r in range(rate):
            picked = [dil_ref[j, pl.ds(r, tm // rate, stride=rate), :] for j in range(A_OUT // LANES)]
            za_refs[g][0, r, :, cols] = jnp.concatenate(picked, axis=1).astype(za_refs[g].dtype)

    def store_pairs(ref, t, val):
        vt = val.T.astype(ref.dtype)
        ref[0, 2 * t] = vt[:LANES]
        ref[0, 2 * t + 1] = vt[LANES:]

    def rotary(t, acc, wc):
        y = head_norm(acc, wc, LANES)
        y = y * cos_ref[...] + pltpu.roll(y, LANES - 16, 1) * s1_ref[...] + pltpu.roll(y, 16, 1) * s2_ref[...]
        if t < C_QW // LANES:
            yt = y.T.astype(qc_ref.dtype)
            qc_ref[0, 2 * t] = yt[:HEAD_DIM]
            qc_ref[0, 2 * t + 1] = yt[HEAD_DIM:]
        else:
            y = y.astype(kc_ref.dtype)
            kc_ref[0, 0] = y[:, :HEAD_DIM]
            kc_ref[0, 1] = y[:, HEAD_DIM:]
            norms = _dot(y * y, ones_ref[:LANES, :LANES])
            ksq_ref[0] = jnp.max(norms, axis=0, keepdims=True)

    def store_vc(acc):
        acc_t = acc.T.astype(vc_ref.dtype)
        tail = _ones_tail(tm, vc_ref.dtype)
        for kv in range(C_KV_HEADS):
            vc_ref[0, kv, :HEAD_DIM] = acc_t[kv * HEAD_DIM:(kv + 1) * HEAD_DIM]
            vc_ref[0, kv, HEAD_DIM:] = tail

    jobs = []
    for section in range(2):
        for g in range(A_GROUPS):
            wc = section * A_W + g * A_OUT
            jobs.append((wc, A_OUT, lambda acc, wc=wc, g=g, section=section:
                         store_group(g, section, head_norm(acc, wc, A_OUT))))
    for t in range(B_W // 256):
        wc = 2 * A_W + t * 256
        jobs.append((wc, 256, lambda acc, wc=wc, t=t: store_pairs(qbt_ref, t, head_norm(acc, wc, 256))))
    for t in range(B_W // 256):
        wc = 2 * A_W + B_W + t * 256
        def store_kb(acc, wc=wc, t=t):
            kb_ref[:, t * 256:(t + 1) * 256] = head_norm(acc, wc, 256).astype(kb_ref.dtype)
        jobs.append((wc, 256, store_kb))
    for t in range(_N_ROPE // LANES):
        wc = _N_NORM + t * LANES
        jobs.append((wc, LANES, lambda acc, wc=wc, t=t: rotary(t, acc, wc)))
    base = _N_NORM + _N_ROPE
    for g in range(A_GROUPS):
        jobs.append((base + g * A_OUT, A_OUT, lambda acc, g=g: store_group(g, 2, acc)))
    for t in range(B_W // 256):
        jobs.append((base + A_W + t * 256, 256, lambda acc, t=t: store_pairs(vbt_ref, t, acc)))
    jobs.append((base + A_W + B_W, LANES, store_vc))

    product = lambda n: _dot(h, w_ref[:, jobs[n][0]:jobs[n][0] + jobs[n][1]])
    acc_next = product(0)
    for n, (_, _, consume) in enumerate(jobs):
        acc = acc_next
        if n + 1 < len(jobs):
            acc_next = product(n + 1)
        consume(acc)


def _qkv_projection(x2, g1, w_qkv, gain_row, ones_bd, rope, B, S):
    M = x2.shape[0]
    tm = PROJ_TM
    n_w = w_qkv.shape[1]
    per_seq = S // tm
    cos_t, s1_t, s2_t = rope
    r1, r2 = A_PATTERNS[1][1], A_PATTERNS[2][1]
    row = lambda i: (i, 0)
    pos = lambda i: (i % per_seq, 0)
    hm = lambda i: (i // per_seq, 0, i % per_seq, 0)
    hm_t = lambda i: (i // per_seq, 0, 0, i % per_seq)
    cd = _MXU_DTYPE
    return pl.pallas_call(
        _proj_kernel,
        grid=(M // tm,),
        in_specs=[
            pl.BlockSpec((tm, D_MODEL), row),
            _const_spec((1, D_MODEL)),
            _const_spec((D_MODEL, n_w)),
            _const_spec((1, _N_NORM + _N_ROPE)),
            _const_spec((256, 256)),
            pl.BlockSpec((tm, LANES), pos),
            pl.BlockSpec((tm, LANES), pos),
            pl.BlockSpec((tm, LANES), pos),
        ],
        out_specs=[
            pl.BlockSpec((tm, ZA_W), row),
            pl.BlockSpec((1, r1, tm // r1, ZA_W), hm),
            pl.BlockSpec((1, r2, tm // r2, ZA_W), hm),
            pl.BlockSpec((1, B_PAIRS, LANES, tm), hm_t),
            pl.BlockSpec((tm, B_W), row),
            pl.BlockSpec((1, B_PAIRS, LANES, tm), hm_t),
            pl.BlockSpec((1, C_Q_HEADS, HEAD_DIM, tm), hm_t),
            pl.BlockSpec((1, C_KV_HEADS, tm, HEAD_DIM), hm),
            pl.BlockSpec((1, C_KV_HEADS, VT_ROWS, tm), hm_t),
            pl.BlockSpec((1, 1, LANES), lambda i: (i, 0, 0)),
        ],
        out_shape=[
            jax.ShapeDtypeStruct((M, ZA_W), cd),
            jax.ShapeDtypeStruct((B, r1, S // r1, ZA_W), cd),
            jax.ShapeDtypeStruct((B, r2, S // r2, ZA_W), cd),
            jax.ShapeDtypeStruct((B, B_PAIRS, LANES, S), cd),
            jax.ShapeDtypeStruct((M, B_W), cd),
            jax.ShapeDtypeStruct((B, B_PAIRS, LANES, S), cd),
            jax.ShapeDtypeStruct((B, C_Q_HEADS, HEAD_DIM, S), cd),
            jax.ShapeDtypeStruct((B, C_KV_HEADS, S, HEAD_DIM), cd),
            jax.ShapeDtypeStruct((B, C_KV_HEADS, VT_ROWS, S), cd),
            jax.ShapeDtypeStruct((M // tm, 1, LANES), jnp.float32),
        ],
        scratch_shapes=[pltpu.VMEM((A_OUT // LANES, tm, LANES), jnp.float32)],
        compiler_params=_params(("arbitrary",)),
        name="qkv_projection",
    )(x2, g1, w_qkv, gain_row, ones_bd, cos_t, s1_t, s2_t)


A_UNIT = 4 * A_RADIUS
A_SPAN = A_UNIT + 2 * A_RADIUS


def _mixer_a_kernel(q_ref, kp_ref, ko_ref, kn_ref, vp_ref, vo_ref, vn_ref, bias_ref,
                    o_ref, lse_ref, kw_ref, vw_ref, s_ref, *, tl, seq_len):
    l0 = pl.program_id(2) * tl
    kw_ref[0:A_RADIUS] = kp_ref[0, 0]
    kw_ref[A_RADIUS:A_RADIUS + tl] = ko_ref[0, 0]
    kw_ref[A_RADIUS + tl:] = kn_ref[0, 0]
    vw_ref[0:A_RADIUS] = vp_ref[0, 0]
    vw_ref[A_RADIUS:A_RADIUS + tl] = vo_ref[0, 0]
    vw_ref[A_RADIUS + tl:] = vn_ref[0, 0]
    even_q = lax.broadcasted_iota(jnp.int32, (A_UNIT, LANES), 1) < HEAD_DIM
    lane_k = lax.broadcasted_iota(jnp.int32, (A_SPAN, LANES), 1)

    n_units = tl // A_UNIT

    def scores(u, pair):
        off = pl.multiple_of(u * A_UNIT, A_UNIT)
        q = q_ref[0, 0, pl.ds(off, A_UNIT), pair * LANES:(pair + 1) * LANES]
        zero = jnp.zeros_like(q)
        q2 = jnp.concatenate([jnp.where(even_q, q, zero), jnp.where(even_q, zero, q)], axis=0)
        return _dot_nt(kw_ref[pl.ds(off, A_SPAN), pair * LANES:(pair + 1) * LANES], q2)

    s_ref[...] = scores(0, 0)

    def unit(u, carry):
        off = pl.multiple_of(u * A_UNIT, A_UNIT)
        top_ok = l0 + off - A_RADIUS >= 0
        bot_ok = l0 + off + A_UNIT + A_RADIUS <= seq_len
        outs, lses = [], []
        s_next = s_ref[...]
        for pair in range(A_HEADS // 2):
            s_pair = s_next
            if pair + 1 < A_HEADS // 2:
                s_next = scores(u, pair + 1)
            else:
                s_ref[...] = scores(jnp.minimum(u + 1, n_units - 1), 0)
            v_slab = vw_ref[pl.ds(off, A_SPAN), pair * LANES:(pair + 1) * LANES]
            for odd in range(2):
                s = s_pair[:, odd * A_UNIT:(odd + 1) * A_UNIT] + bias_ref[2 * pair + odd]
                s = jnp.concatenate([jnp.where(top_ok, s[:A_RADIUS], NEG), s[A_RADIUS:A_SPAN - A_RADIUS],
                                     jnp.where(bot_ok, s[A_SPAN - A_RADIUS:], NEG)], axis=0)
                m = jnp.max(s, axis=0, keepdims=True)
                p = jnp.exp2(s - m).astype(_MXU_DTYPE)
                den_row = (1 - odd) * HEAD_DIM
                mine = (lane_k >= odd * HEAD_DIM) & (lane_k < (odd + 1) * HEAD_DIM)
                v_aug = jnp.where(mine, v_slab, (lane_k == den_row).astype(v_slab.dtype))
                acc = lax.dot_general(v_aug, p, (((0,), (0,)), ((), ())),
                                      preferred_element_type=jnp.float32)
                den = acc[den_row:den_row + 1]
                outs.append(acc[odd * HEAD_DIM:(odd + 1) * HEAD_DIM] / den)
                lses.append(jnp.broadcast_to(m + jnp.log2(den), (HEAD_DIM, A_UNIT)))
        o_ref[0, 0, pl.ds(off, A_UNIT), :] = jnp.concatenate(outs, axis=0).T
        lse_ref[0, 0, pl.ds(off, A_UNIT), :] = jnp.concatenate(lses, axis=0).T
        return carry

    lax.fori_loop(0, n_units, unit, 0, unroll=2)


def _mixer_a_group(za_g, bias, rate, B, S):
    L = S // rate
    tl = min(1024, L)
    nblk = L // A_RADIUS
    per = tl // A_RADIUS

    def own(section):
        return lambda b, r, l: (b, r, l, section)

    def prev(section):
        return lambda b, r, l: (b, r, jnp.maximum(l * per - 1, 0), section)

    def nxt(section):
        return lambda b, r, l: (b, r, jnp.minimum((l + 1) * per, nblk - 1), section)

    edge = (1, 1, A_RADIUS, A_OUT)
    full = (1, 1, tl, A_OUT)
    return pl.pallas_call(
        functools.partial(_mixer_a_kernel, tl=tl, seq_len=L),
        grid=(B, rate, L // tl),
        in_specs=[
            pl.BlockSpec(full, own(0)),
            pl.BlockSpec(edge, prev(1)), pl.BlockSpec(full, own(1)), pl.BlockSpec(edge, nxt(1)),
            pl.BlockSpec(edge, prev(2)), pl.BlockSpec(full, own(2)), pl.BlockSpec(edge, nxt(2)),
            _const_spec((A_HEADS, A_SPAN, A_UNIT)),
        ],
        out_specs=[pl.BlockSpec(full, own(0)), pl.BlockSpec(full, own(0))],
        out_shape=[jax.ShapeDtypeStruct((B, rate, L, A_OUT), jnp.float32)] * 2,
        scratch_shapes=[pltpu.VMEM((tl + 2 * A_RADIUS, A_OUT), _MXU_DTYPE)] * 2
        + [pltpu.VMEM((A_SPAN, 2 * A_UNIT), jnp.float32)],
        compiler_params=_params(("arbitrary",) * 3),
        name=f"mixer_a_rate{rate}",
    )(za_g, za_g, za_g, za_g, za_g, za_g, za_g, bias)


B_UNIT_ROWS = 4
B_UNIT = B_UNIT_ROWS * GRID_W
B_SPAN_ROWS = B_UNIT_ROWS + B_WIN_ROWS
B_SPAN = B_SPAN_ROWS * GRID_W
B_HALO = (B_WIN_ROWS // 2) * GRID_W
B_TILE_ROWS = 16
B_TILE = B_TILE_ROWS * GRID_W


def _mixer_b_kernel(q_ref, kp_ref, ko_ref, kn_ref, vp_ref, vo_ref, vn_ref, blocks_ref, sel_ref,
                    o_ref, kw_ref, vw_ref, bias_ref, s_ref, *, rows):
    @pl.when((pl.program_id(0) == 0) & (pl.program_id(1) == 0))
    def _():
        left = lax.broadcasted_iota(jnp.int32, (GRID_W, LANES), 1) < GRID_W

        def fill(n, carry):
            v, h = n // B_HEADS, n % B_HEADS
            for a in range(B_SPAN_ROWS):
                for gp in range(B_UNIT_ROWS // 2):
                    at = (v * B_SPAN_ROWS + a) * B_UNIT_ROWS + 2 * gp
                    tile = jnp.where(left, blocks_ref[h, sel_ref[at]], blocks_ref[h, sel_ref[at + 1]])
                    bias_ref[v, h, a * GRID_W:(a + 1) * GRID_W, gp * LANES:(gp + 1) * LANES] = tile
            return carry

        lax.fori_loop(0, 3 * B_HEADS, fill, 0)

    i0 = pl.program_id(1) * B_TILE_ROWS
    kw_ref[0:B_HALO] = kp_ref[0]
    kw_ref[B_HALO:B_HALO + B_TILE] = ko_ref[0]
    kw_ref[B_HALO + B_TILE:] = kn_ref[0]
    vw_ref[:, :, 0:B_HALO] = vp_ref[0]
    vw_ref[:, :, B_HALO:B_HALO + B_TILE] = vo_ref[0]
    vw_ref[:, :, B_HALO + B_TILE:] = vn_ref[0]
    tail = _ones_tail(B_SPAN, _MXU_DTYPE)
    upper = lax.broadcasted_iota(jnp.int32, (LANES, B_UNIT), 0) < HEAD_DIM

    n_units = B_TILE_ROWS // B_UNIT_ROWS

    def span_offset(u):
        r0 = jnp.clip(i0 + u * B_UNIT_ROWS - B_WIN_ROWS // 2, 0, rows - B_SPAN_ROWS)
        return pl.multiple_of((r0 - (i0 - B_WIN_ROWS // 2)) * GRID_W, LANES)

    def scores(u, pair):
        qt = q_ref[0, pair, :, pl.ds(pl.multiple_of(u * B_UNIT, B_UNIT), B_UNIT)]
        zero = jnp.zeros_like(qt)
        qt2 = jnp.concatenate([jnp.where(upper, qt, zero), jnp.where(upper, zero, qt)], axis=1)
        return _dot(kw_ref[pl.ds(span_offset(u), B_SPAN), pair * LANES:(pair + 1) * LANES], qt2)

    s_ref[...] = scores(0, 0)

    def unit(u, carry):
        i0u = i0 + u * B_UNIT_ROWS
        off = span_offset(u)
        variant = jnp.where(i0u == 0, 0, jnp.where(i0u == rows - B_UNIT_ROWS, 2, 1))
        qoff = pl.multiple_of(u * B_UNIT, B_UNIT)
        outs = []
        s_next = s_ref[...]
        for pair in range(B_PAIRS):
            s_pair = s_next
            if pair + 1 < B_PAIRS:
                s_next = scores(u, pair + 1)
            else:
                s_ref[...] = scores(jnp.minimum(u + 1, n_units - 1), 0)
            for odd in range(2):
                s = s_pair[:, odd * B_UNIT:(odd + 1) * B_UNIT] + bias_ref[variant, 2 * pair + odd]
                p = jnp.exp2(s - jnp.max(s, axis=0, keepdims=True)).astype(_MXU_DTYPE)
                vt = vw_ref[pair, odd * HEAD_DIM:(odd + 1) * HEAD_DIM, pl.ds(off, B_SPAN)]
                acc = _dot(jnp.concatenate([vt, tail], axis=0), p)
                outs.append(acc[:HEAD_DIM] / acc[HEAD_DIM:HEAD_DIM + 1])
        o_ref[0, pl.ds(qoff, B_UNIT), :] = jnp.concatenate(outs, axis=0).T.astype(o_ref.dtype)
        return carry

    lax.fori_loop(0, n_units, unit, 0, unroll=2)


def _mixer_b(qbt, kb, vbt, blocks, B, S):
    rows = S // GRID_W
    sel = jnp.asarray(_mixer_b_block_index(rows))
    nt = rows // B_TILE_ROWS
    per = B_TILE // B_HALO
    nh = S // B_HALO
    kb3 = kb.reshape(B, S, B_W)
    prev = lambda t: jnp.maximum(t * per - 1, 0)
    nxt = lambda t: jnp.minimum((t + 1) * per, nh - 1)
    k_edge, k_own = (1, B_HALO, B_W), (1, B_TILE, B_W)
    t_edge, t_own = (1, B_PAIRS, LANES, B_HALO), (1, B_PAIRS, LANES, B_TILE)
    o = pl.pallas_call(
        functools.partial(_mixer_b_kernel, rows=rows),
        grid=(B, nt),
        in_specs=[
            pl.BlockSpec(t_own, lambda b, t: (b, 0, 0, t)),
            pl.BlockSpec(k_edge, lambda b, t: (b, prev(t), 0)),
            pl.BlockSpec(k_own, lambda b, t: (b, t, 0)),
            pl.BlockSpec(k_edge, lambda b, t: (b, nxt(t), 0)),
            pl.BlockSpec(t_edge, lambda b, t: (b, 0, 0, prev(t))),
            pl.BlockSpec(t_own, lambda b, t: (b, 0, 0, t)),
            pl.BlockSpec(t_edge, lambda b, t: (b, 0, 0, nxt(t))),
            _const_spec(blocks.shape),
            pl.BlockSpec(memory_space=pltpu.SMEM),
        ],
        out_specs=pl.BlockSpec(k_own, lambda b, t: (b, t, 0)),
        out_shape=jax.ShapeDtypeStruct((B, S, B_W), _MXU_DTYPE),
        scratch_shapes=[pltpu.VMEM((B_TILE + 2 * B_HALO, B_W), _MXU_DTYPE),
                        pltpu.VMEM((B_PAIRS, LANES, B_TILE + 2 * B_HALO), _MXU_DTYPE),
                        pltpu.VMEM((3, B_HEADS, B_SPAN, B_UNIT), jnp.float32),
                        pltpu.VMEM((B_SPAN, 2 * B_UNIT), jnp.float32)],
        compiler_params=_params(("arbitrary",) * 2),
        name="mixer_b",
    )(qbt, kb3, kb3, kb3, vbt, vbt, vbt, blocks, sel)
    return o.reshape(B * S, B_W)


C_TQ = 1024
C_TK = 512
C_UNIT = 512
C_SHIFT_LIMIT = 60.0
C_UNROLL = 4


def _mixer_c_kernel(q_ref, k_ref, vt_ref, ksq_ref, o_ref, qcat_ref, m_ref, acc_ref, s_ref, *, n_kv):
    acc_ref[...] = jnp.zeros(acc_ref.shape, jnp.float32)
    for g in range(C_GROUP):
        qcat_ref[:, g * C_TQ:(g + 1) * C_TQ] = q_ref[0, g]
    n_units = C_GROUP * C_TQ // C_UNIT

    def keys(j):
        return k_ref[0, 0, pl.ds(pl.multiple_of(j * C_TK, C_TK), C_TK), :]

    def scores(k, u):
        return _dot(k, qcat_ref[:, u * C_UNIT:(u + 1) * C_UNIT])

    def sweep(update, unroll):
        s_ref[...] = scores(keys(0), 0)

        def step(j, carry):
            k = keys(j)
            vt = vt_ref[0, 0, :, pl.ds(pl.multiple_of(j * C_TK, C_TK), C_TK)]
            s_next = s_ref[...]
            for u in range(n_units):
                s = s_next
                if u + 1 < n_units:
                    s_next = scores(k, u + 1)
                else:
                    s_ref[...] = scores(keys(jnp.minimum(j + 1, n_kv - 1)), 0)
                update(s, vt, slice(u * C_UNIT, (u + 1) * C_UNIT))
            return carry

        lax.fori_loop(0, n_kv, step, 0, unroll=unroll)

    def fixed_shift(s, vt, cols):
        acc_ref[:, cols] += _dot(vt, jnp.exp2(s - m_ref[:, cols]).astype(_MXU_DTYPE))

    def running_max(s, vt, cols):
        m_prev = m_ref[:, cols]
        m_new = jnp.maximum(m_prev, jnp.max(s, axis=0, keepdims=True))
        alpha = jnp.exp2(m_prev - m_new)
        p = jnp.exp2(s - m_new).astype(_MXU_DTYPE)
        acc_ref[:, cols] = alpha * acc_ref[:, cols] + _dot(vt, p)
        m_ref[:, cols] = m_new

    qf = qcat_ref[...].astype(jnp.float32)
    bound = jnp.sqrt(jnp.sum(qf * qf, axis=0, keepdims=True) * ksq_ref[0, 0, :, :1])
    small = jnp.max(bound) <= C_SHIFT_LIMIT

    @pl.when(small)
    def _():
        m_ref[...] = bound
        sweep(fixed_shift, C_UNROLL)

    @pl.when(jnp.logical_not(small))
    def _():
        m_ref[...] = jnp.full(m_ref.shape, -jnp.inf, jnp.float32)
        sweep(running_max, 1)

    o_t = jnp.concatenate(
        [acc_ref[:HEAD_DIM, g * C_TQ:(g + 1) * C_TQ] / acc_ref[HEAD_DIM:HEAD_DIM + 1, g * C_TQ:(g + 1) * C_TQ]
         for g in range(C_GROUP)], axis=0)
    o_ref[0] = o_t.T.astype(o_ref.dtype)


def _mixer_c(qc_t, kc, vc_t, ksq, B, S):
    o = pl.pallas_call(
        functools.partial(_mixer_c_kernel, n_kv=S // C_TK),
        grid=(B, C_KV_HEADS, S // C_TQ),
        in_specs=[
            pl.BlockSpec((1, C_GROUP, HEAD_DIM, C_TQ), lambda b, kv, i: (b, kv, 0, i)),
            pl.BlockSpec((1, 1, S, HEAD_DIM), lambda b, kv, i: (b, kv, 0, 0)),
            pl.BlockSpec((1, 1, VT_ROWS, S), lambda b, kv, i: (b, kv, 0, 0)),
            pl.BlockSpec((1, 1, 1, LANES), lambda b, kv, i: (b, kv, 0, 0)),
        ],
        out_specs=pl.BlockSpec((1, C_TQ, C_GROUP * HEAD_DIM), lambda b, kv, i: (b, i, kv)),
        out_shape=jax.ShapeDtypeStruct((B, S, C_QW), _MXU_DTYPE),
        scratch_shapes=[pltpu.VMEM((HEAD_DIM, C_GROUP * C_TQ), _MXU_DTYPE),
                        pltpu.VMEM((1, C_GROUP * C_TQ), jnp.float32),
                        pltpu.VMEM((VT_ROWS, C_GROUP * C_TQ), jnp.float32),
                        pltpu.VMEM((C_TK, C_UNIT), jnp.float32)],
        compiler_params=_params(("arbitrary",) * 3),
        name="mixer_c",
    )(qc_t, kc, vc_t, ksq)
    return o.reshape(B * S, C_QW)


MERGE_TM = 512


def _merge_kernel(x_ref, g_ref, oa0_ref, oa1_ref, oa2_ref, l0_ref, l1_ref, l2_ref, ob_ref, oc_ref,
                  wg_ref, pa_ref, pb_ref, pc_ref, wo_ref, out_ref, *scratch):
    tm = x_ref.shape[0]

    def token_major(ref, scr):
        rate = ref.shape[1]
        halves = range(A_OUT // LANES)
        for r in range(rate):
            for j in halves:
                scr[j, pl.ds(r, tm // rate, stride=rate), :] = ref[0, r, :, j * LANES:(j + 1) * LANES]
        return jnp.concatenate([scr[j] for j in halves], axis=1)

    x = x_ref[...]
    h = _rms(x, g_ref[...]).astype(_MXU_DTYPE)
    oa0, l0 = oa0_ref[...], l0_ref[...]
    oa1, l1 = token_major(oa1_ref, scratch[0]), token_major(l1_ref, scratch[1])
    oa2, l2 = token_major(oa2_ref, scratch[2]), token_major(l2_ref, scratch[3])
    mx = jnp.maximum(jnp.maximum(l0, l1), l2)
    w0, w1, w2 = jnp.exp2(l0 - mx), jnp.exp2(l1 - mx), jnp.exp2(l2 - mx)
    o_a = (w0 * oa0 + w1 * oa1 + w2 * oa2) / (w0 + w1 + w2)
    merged = jax.nn.sigmoid(_dot(h, wg_ref[:, 0:D_MODEL])) * _dot(o_a.astype(_MXU_DTYPE), pa_ref[...])
    merged += jax.nn.sigmoid(_dot(h, wg_ref[:, D_MODEL:2 * D_MODEL])) * _dot(ob_ref[...], pb_ref[...])
    merged += jax.nn.sigmoid(_dot(h, wg_ref[:, 2 * D_MODEL:3 * D_MODEL])) * _dot(oc_ref[...], pc_ref[...])
    out_ref[...] = x + _dot(merged.astype(_MXU_DTYPE), wo_ref[...])


def _merge(x2, g1, oa, lse, ob, oc, wg, pa, pb, pc, wo, B, S):
    M = x2.shape[0]
    tm = MERGE_TM
    per_seq = S // tm
    row = lambda i: (i, 0)
    tile = lambda w: pl.BlockSpec((tm, w), row)

    def dilated(rate):
        return pl.BlockSpec((1, rate, tm // rate, A_OUT), lambda i: (i // per_seq, 0, i % per_seq, 0))

    r1, r2 = A_PATTERNS[1][1], A_PATTERNS[2][1]
    return pl.pallas_call(
        _merge_kernel,
        grid=(M // tm,),
        in_specs=[tile(D_MODEL), _const_spec((1, D_MODEL)),
                  tile(A_OUT), dilated(r1), dilated(r2), tile(A_OUT), dilated(r1), dilated(r2),
                  tile(B_W), tile(C_QW),
                  _const_spec(wg.shape), _const_spec(pa.shape), _const_spec(pb.shape),
                  _const_spec(pc.shape), _const_spec(wo.shape)],
        out_specs=tile(D_MODEL),
        out_shape=jax.ShapeDtypeStruct((M, D_MODEL), jnp.float32),
        scratch_shapes=[pltpu.VMEM((A_OUT // LANES, tm, LANES), jnp.float32)] * 4,
        compiler_params=_params(("arbitrary",)),
        name="gated_merge",
    )(x2, g1, oa[0].reshape(M, A_OUT), oa[1], oa[2], lse[0].reshape(M, A_OUT), lse[1], lse[2],
      ob, oc, wg, pa, pb, pc, wo)


FFN_TM = 512
FFN_CHUNK = 256


def _ffn_kernel(x_ref, g_ref, wup_ref, wdown_ref, out_ref, act_ref):
    x = x_ref[...]
    xn = _rms(x, g_ref[...]).astype(_MXU_DTYPE)
    for c in range(D_FF // FFN_CHUNK):
        cs = slice(c * FFN_CHUNK, (c + 1) * FFN_CHUNK)
        a = _dot(xn, wup_ref[:, cs])
        b = _dot(xn, wup_ref[:, D_FF + c * FFN_CHUNK:D_FF + (c + 1) * FFN_CHUNK])
        act_ref[:, cs] = (a * jax.nn.sigmoid(a) * b).astype(act_ref.dtype)
    out_ref[...] = x + _dot(act_ref[...], wdown_ref[...])


def _ffn(x2, g2, wup, wdown):
    M = x2.shape[0]
    tm = FFN_TM
    row = lambda i: (i, 0)
    return pl.pallas_call(
        _ffn_kernel,
        grid=(M // tm,),
        in_specs=[pl.BlockSpec((tm, D_MODEL), row), _const_spec((1, D_MODEL)),
                  _const_spec(wup.shape), _const_spec(wdown.shape)],
        out_specs=pl.BlockSpec((tm, D_MODEL), row),
        out_shape=jax.ShapeDtypeStruct((M, D_MODEL), jnp.float32),
        scratch_shapes=[pltpu.VMEM((tm, D_FF), _MXU_DTYPE)],
        compiler_params=_params(("arbitrary",)),
        name="swiglu_ffn",
    )(x2, g2, wup, wdown)


def _t5_bucket(rel):
    half = T5_BUCKETS // 2
    max_exact = half // 2
    ret = jnp.where(rel > 0, half, 0)
    n = jnp.abs(rel)
    nf = jnp.maximum(n, 1).astype(jnp.float32)
    large = max_exact + (jnp.log(nf / max_exact) / math.log(T5_MAX_DIST / max_exact)
                         * (half - max_exact)).astype(jnp.int32)
    large = jnp.minimum(large, half - 1)
    return ret + jnp.where(n < max_exact, n, large)


def _mixer_a_bias(table_g, rate):
    j = jnp.arange(A_SPAN)[:, None]
    i = jnp.arange(A_UNIT)[None, :]
    step = j - A_RADIUS - i
    onehot = (_t5_bucket(step * rate)[:, :, None] == jnp.arange(T5_BUCKETS)).astype(jnp.float32)
    bias = jnp.einsum("jib,bh->hji", onehot, table_g.astype(jnp.float32) * LOG2_E,
                      precision=lax.Precision.HIGHEST)
    return jnp.where((jnp.abs(step) <= A_RADIUS)[None], bias, NEG)


B_N_DR = 2 * B_WIN_ROWS - 1


def _mixer_b_blocks(rpb):
    c = np.arange(GRID_W)
    c0 = np.clip(c - B_WIN_COLS // 2, 0, GRID_W - B_WIN_COLS)
    col_ok = (c[:, None] >= c0[None, :]) & (c[:, None] < c0[None, :] + B_WIN_COLS)
    dc = np.clip(c[:, None] - c[None, :] + B_WIN_COLS - 1, 0, 2 * B_WIN_COLS - 2)
    pick_c = ((dc[..., None] == np.arange(2 * B_WIN_COLS - 1)) & col_ok[..., None]).astype(np.float32)
    blocks = jnp.einsum("hdk,xyk->hdxy", rpb.astype(jnp.float32) * LOG2_E, pick_c,
                        precision=lax.Precision.HIGHEST)
    blocks = jnp.where(col_ok[None, None], blocks, NEG)
    blocks = jnp.concatenate([blocks, jnp.full((B_HEADS, 1, GRID_W, GRID_W), NEG, jnp.float32)], axis=1)
    return jnp.concatenate([blocks, blocks], axis=-1)


def _mixer_b_block_index(rows):
    first_query_row = np.array([0, B_WIN_ROWS // 2, rows - B_UNIT_ROWS])
    i = first_query_row[:, None, None] + np.arange(B_UNIT_ROWS)[None, None, :]
    r0 = np.clip(first_query_row - B_WIN_ROWS // 2, 0, rows - B_SPAN_ROWS)
    ik = r0[:, None, None] + np.arange(B_SPAN_ROWS)[None, :, None]
    rs = np.clip(i - B_WIN_ROWS // 2, 0, rows - B_WIN_ROWS)
    row_ok = (ik >= rs) & (ik < rs + B_WIN_ROWS)
    return np.where(row_ok, ik - i + B_WIN_ROWS - 1, B_N_DR).astype(np.int32).reshape(-1)


def _rope_tables(S):
    t = jnp.arange(S)
    inv = ROPE_THETA ** (-jnp.arange(0, ROPE_AXIS_DIM, 2, dtype=jnp.float32) / ROPE_AXIS_DIM)
    d = np.arange(LANES) % HEAD_DIM
    inv_lane = inv[d % (ROPE_AXIS_DIM // 2)][None, :]
    is_col = (d >= ROPE_AXIS_DIM)[None, :]
    pos = jnp.where(is_col, (t % GRID_W)[:, None], (t // GRID_W)[:, None]).astype(jnp.float32)
    ang = pos * inv_lane
    first = ((d % ROPE_AXIS_DIM) < ROPE_AXIS_DIM // 2)[None, :]
    sin = jnp.sin(ang)
    return jnp.cos(ang), jnp.where(first, -sin, 0.0), jnp.where(first, 0.0, sin)


def _block_diag_ones():
    i = np.arange(256) // HEAD_DIM
    return jnp.asarray(i[:, None] == i[None, :], _MXU_DTYPE)


def kernel(x, rel_bias_table, norm1, w_in, qk_gain, nat_rpb, w_br_a, w_br_b, w_br_c, w_o,
           norm2, w_up, w_down):
    B, S, D = x.shape
    depth = w_in.shape[0]
    cd = _MXU_DTYPE
    M = B * S
    x2 = x.reshape(M, D)
    rope = _rope_tables(S)
    ones_bd = _block_diag_ones()
    a_bias = [_mixer_a_bias(rel_bias_table[:, g * A_HEADS:(g + 1) * A_HEADS], rate)
              for g, (_, rate) in enumerate(A_PATTERNS)]
    sec = lambda w, lo, n: w[:, lo:lo + n]
    for l in range(depth):
        w = w_in[l]
        w_qkv = jnp.concatenate(
            [sec(w, _QA, A_W), sec(w, _KA, A_W), sec(w, _QB, B_W), sec(w, _KB, B_W),
             sec(w, _QC, C_QW), sec(w, _KC, C_KVW),
             sec(w, _VA, A_W), sec(w, _VB, B_W), sec(w, _VC, C_KVW)], axis=1).astype(cd)
        gq = qk_gain[l]
        tile = lambda g, n, s: jnp.tile(g * s, n)
        gain_row = jnp.concatenate([
            tile(gq[0], A_W // HEAD_DIM, QK_SCALE * LOG2_E), tile(gq[1], A_W // HEAD_DIM, 1.0),
            tile(gq[2], B_HEADS, QK_SCALE * LOG2_E), tile(gq[3], B_HEADS, 1.0),
            tile(gq[4], C_Q_HEADS, QK_SCALE * LOG2_E), tile(gq[5], C_KV_HEADS, 1.0)])[None, :]
        g1 = norm1[l][None, :]
        za0, za1, za2, qbt, kb, vbt, qc, kc, vc, ksq = _qkv_projection(
            x2, g1, w_qkv, gain_row, ones_bd, rope, B, S)
        ksq = jnp.max(ksq.reshape(B, -1, C_KV_HEADS, HEAD_DIM), axis=(1, 3))
        ksq = jnp.broadcast_to(ksq[:, :, None, None], (B, C_KV_HEADS, 1, LANES))
        za = (za0.reshape(B, 1, S, ZA_W), za1, za2)
        oa, lse = [], []
        for g, (_, rate) in enumerate(A_PATTERNS):
            o_g, l_g = _mixer_a_group(za[g], a_bias[g], rate, B, S)
            oa.append(o_g)
            lse.append(l_g)
        ob = _mixer_b(qbt, kb, vbt, _mixer_b_blocks(nat_rpb[l]), B, S)
        oc = _mixer_c(qc, kc, vc, ksq, B, S)
        x2 = _merge(x2, g1, oa, lse, ob, oc, sec(w, _ZG, N_BRANCH * D).astype(cd),
                    w_br_a[l].astype(cd), w_br_b[l].astype(cd), w_br_c[l].astype(cd), w_o[l].astype(cd), B, S)
        x2 = _ffn(x2, norm2[l][None, :], w_up[l].astype(cd), w_down[l].astype(cd))
    return x2.reshape(B, S, D)
```

```python
import functools
import math

import jax
import jax.numpy as jnp
import numpy as np
from jax import lax
from jax.experimental import pallas as pl
from jax.experimental.pallas import tpu as pltpu

_MXU_DTYPE = jnp.bfloat16

D_MODEL = 1024
HEAD_DIM = 64
GRID_W = 64
RMS_EPS = 1e-6
NEG = -1e30
A_PATTERNS = ((128, 1), (512, 4), (2048, 16))
A_GROUPS = 3
A_HEADS = 4
A_W = A_GROUPS * A_HEADS * HEAD_DIM
A_OUT = A_HEADS * HEAD_DIM
A_RADIUS = 64
B_HEADS = 8
B_W = B_HEADS * HEAD_DIM
B_WIN_ROWS = 8
B_WIN_COLS = 16
C_Q_HEADS = 8
C_KV_HEADS = 2
C_GROUP = C_Q_HEADS // C_KV_HEADS
C_QW = C_Q_HEADS * HEAD_DIM
C_KVW = C_KV_HEADS * HEAD_DIM
ROPE_THETA = 10000.0
ROPE_AXIS_DIM = HEAD_DIM // 2
T5_BUCKETS = 32
T5_MAX_DIST = 1024
N_BRANCH = 3
D_FF = math.ceil(8 * D_MODEL / 3 / 256) * 256
QK_SCALE = HEAD_DIM ** -0.5
LOG2_E = math.log2(math.e)

VMEM_LIMIT_BYTES = 56 * 1024 * 1024
LANES = 128
BF16_SUBLANES = 16
VT_ROWS = HEAD_DIM + BF16_SUBLANES

_OFF = np.cumsum([0, A_W, A_W, A_W, B_W, B_W, B_W, C_QW, C_KVW, C_KVW]).tolist()
(_QA, _KA, _VA, _QB, _KB, _VB, _QC, _KC, _VC, _ZG) = _OFF

ZA_W = 3 * A_OUT
B_PAIRS = B_HEADS // 2


def _params(sem):
    return pltpu.CompilerParams(dimension_semantics=sem, vmem_limit_bytes=VMEM_LIMIT_BYTES)


def _const_spec(shape):
    nd = len(shape)
    return pl.BlockSpec(shape, lambda *_: (0,) * nd, pipeline_mode=pl.Buffered(1))


def _rms(x, g):
    return x * lax.rsqrt(jnp.mean(x * x, axis=-1, keepdims=True) + RMS_EPS) * g


def _dot(a, b):
    return jnp.dot(a, b, preferred_element_type=jnp.float32)


def _dot_nt(a, b):
    return lax.dot_general(a, b, (((1,), (1,)), ((), ())), preferred_element_type=jnp.float32)


def _ones_tail(width, dtype):
    return (lax.broadcasted_iota(jnp.int32, (VT_ROWS - HEAD_DIM, width), 0) == 0).astype(dtype)


PROJ_TM = 512
PROJ_W_BLOCK = 1536
_N_NORM = 2 * A_W + 2 * B_W
_N_ROPE = C_QW + C_KVW


def _proj_kernel(x_ref, g_ref, w0_ref, w1_ref, w2_ref, gain_ref, ones_ref, cos_ref, s1_ref, s2_ref,
                 za0_ref, za1_ref, za2_ref, qbt_ref, kb_ref, vbt_ref, qc_ref, kc_ref, vc_ref, ksq_ref,
                 dil_ref):
    tm = x_ref.shape[0]
    w_refs = (w0_ref, w1_ref, w2_ref)
    h = _rms(x_ref[...], g_ref[...]).astype(_MXU_DTYPE)
    za_refs = (za0_ref, za1_ref, za2_ref)

    def head_norm(acc, c0, width):
        sq = (acc * acc).astype(_MXU_DTYPE)
        ms = _dot(sq, ones_ref[:width, :width]) * (1.0 / HEAD_DIM)
        return acc * lax.rsqrt(ms + RMS_EPS) * gain_ref[:, c0:c0 + width]

    def store_group(g, section, val):
        cols = slice(section * A_OUT, (section + 1) * A_OUT)
        rate = A_PATTERNS[g][1]
        if rate == 1:
            za0_ref[:, cols] = val.astype(za0_ref.dtype)
            return
        for j in range(A_OUT // LANES):
            dil_ref[j] = val[:, j * LANES:(j + 1) * LANES]
        for r in range(rate):
            picked = [dil_ref[j, pl.ds(r, tm // rate, stride=rate), :] for j in range(A_OUT // LANES)]
            za_refs[g][0, r, :, cols] = jnp.concatenate(picked, axis=1).astype(za_refs[g].dtype)

    def store_pairs(ref, t, val):
        vt = val.T.astype(ref.dtype)
        ref[0, 2 * t] = vt[:LANES]
        ref[0, 2 * t + 1] = vt[LANES:]

    def rotary(y, width):
        reps = width // LANES
        table = lambda ref: jnp.concatenate([ref[...]] * reps, axis=1) if reps > 1 else ref[...]
        return (y * table(cos_ref) + pltpu.roll(y, width - 16, 1) * table(s1_ref)
                + pltpu.roll(y, 16, 1) * table(s2_ref))

    def store_qc(t, acc, gain_at):
        yt = rotary(head_norm(acc, gain_at, 256), 256).T.astype(qc_ref.dtype)
        for j in range(256 // HEAD_DIM):
            qc_ref[0, 4 * t + j] = yt[j * HEAD_DIM:(j + 1) * HEAD_DIM]

    def store_kc_vc(acc, gain_at):
        y = rotary(head_norm(acc[:, :LANES], gain_at, LANES), LANES).astype(kc_ref.dtype)
        acc_t = acc[:, LANES:].T.astype(vc_ref.dtype)
        tail = _ones_tail(tm, vc_ref.dtype)
        for kv in range(C_KV_HEADS):
            kc_ref[0, kv] = y[:, kv * HEAD_DIM:(kv + 1) * HEAD_DIM]
            vc_ref[0, kv, :HEAD_DIM] = acc_t[kv * HEAD_DIM:(kv + 1) * HEAD_DIM]
            vc_ref[0, kv, HEAD_DIM:] = tail
        norms = _dot(y * y, ones_ref[:LANES, :LANES])
        ksq_ref[0] = jnp.max(norms, axis=0, keepdims=True)

    jobs = []
    for section, src in enumerate((_QA, _KA)):
        for g in range(A_GROUPS):
            gain_at = section * A_W + g * A_OUT
            jobs.append((src + g * A_OUT, A_OUT, lambda acc, gain_at=gain_at, g=g, section=section:
                         store_group(g, section, head_norm(acc, gain_at, A_OUT))))
    for t in range(B_W // 256):
        gain_at = 2 * A_W + t * 256
        jobs.append((_QB + t * 256, 256, lambda acc, gain_at=gain_at, t=t:
                     store_pairs(qbt_ref, t, head_norm(acc, gain_at, 256))))
    for t in range(B_W // 256):
        gain_at = 2 * A_W + B_W + t * 256
        def store_kb(acc, gain_at=gain_at, t=t):
            kb_ref[:, t * 256:(t + 1) * 256] = head_norm(acc, gain_at, 256).astype(kb_ref.dtype)
        jobs.append((_KB + t * 256, 256, store_kb))
    for t in range(C_QW // 256):
        jobs.append((_QC + t * 256, 256, lambda acc, t=t: store_qc(t, acc, _N_NORM + t * 256)))
    jobs.append((_KC, 2 * C_KVW, lambda acc: store_kc_vc(acc, _N_NORM + C_QW)))
    for g in range(A_GROUPS):
        jobs.append((_VA + g * A_OUT, A_OUT, lambda acc, g=g: store_group(g, 2, acc)))
    for t in range(B_W // 256):
        jobs.append((_VB + t * 256, 256, lambda acc, t=t: store_pairs(vbt_ref, t, acc)))

    def product(n):
        src, width, _ = jobs[n]
        blk, col = divmod(src, PROJ_W_BLOCK)
        return _dot(h, w_refs[blk][:, col:col + width].astype(_MXU_DTYPE))

    acc_next = product(0)
    for n, (_, _, consume) in enumerate(jobs):
        acc = acc_next
        if n + 1 < len(jobs):
            acc_next = product(n + 1)
        consume(acc)


def _qkv_projection(x2, g1, w_in, layer, gain_row, ones_bd, rope, B, S):
    M = x2.shape[0]
    tm = PROJ_TM
    per_seq = S // tm
    w_block = lambda j: pl.BlockSpec((None, D_MODEL, PROJ_W_BLOCK), lambda i: (layer, 0, j),
                                     pipeline_mode=pl.Buffered(1))
    cos_t, s1_t, s2_t = rope
    r1, r2 = A_PATTERNS[1][1], A_PATTERNS[2][1]
    row = lambda i: (i, 0)
    pos = lambda i: (i % per_seq, 0)
    hm = lambda i: (i // per_seq, 0, i % per_seq, 0)
    hm_t = lambda i: (i // per_seq, 0, 0, i % per_seq)
    cd = _MXU_DTYPE
    return pl.pallas_call(
        _proj_kernel,
        grid=(M // tm,),
        in_specs=[
            pl.BlockSpec((tm, D_MODEL), row),
            _const_spec((1, D_MODEL)),
            w_block(0), w_block(1), w_block(2),
            _const_spec((1, _N_NORM + _N_ROPE)),
            _const_spec((256, 256)),
            pl.BlockSpec((tm, LANES), pos),
            pl.BlockSpec((tm, LANES), pos),
            pl.BlockSpec((tm, LANES), pos),
        ],
        out_specs=[
            pl.BlockSpec((tm, ZA_W), row),
            pl.BlockSpec((1, r1, tm // r1, ZA_W), hm),
            pl.BlockSpec((1, r2, tm // r2, ZA_W), hm),
            pl.BlockSpec((1, B_PAIRS, LANES, tm), hm_t),
            pl.BlockSpec((tm, B_W), row),
            pl.BlockSpec((1, B_PAIRS, LANES, tm), hm_t),
            pl.BlockSpec((1, C_Q_HEADS, HEAD_DIM, tm), hm_t),
            pl.BlockSpec((1, C_KV_HEADS, tm, HEAD_DIM), hm),
            pl.BlockSpec((1, C_KV_HEADS, VT_ROWS, tm), hm_t),
            pl.BlockSpec((1, 1, LANES), lambda i: (i, 0, 0)),
        ],
        out_shape=[
            jax.ShapeDtypeStruct((M, ZA_W), cd),
            jax.ShapeDtypeStruct((B, r1, S // r1, ZA_W), cd),
            jax.ShapeDtypeStruct((B, r2, S // r2, ZA_W), cd),
            jax.ShapeDtypeStruct((B, B_PAIRS, LANES, S), cd),
            jax.ShapeDtypeStruct((M, B_W), cd),
            jax.ShapeDtypeStruct((B, B_PAIRS, LANES, S), cd),
            jax.ShapeDtypeStruct((B, C_Q_HEADS, HEAD_DIM, S), cd),
            jax.ShapeDtypeStruct((B, C_KV_HEADS, S, HEAD_DIM), cd),
            jax.ShapeDtypeStruct((B, C_KV_HEADS, VT_ROWS, S), cd),
            jax.ShapeDtypeStruct((M // tm, 1, LANES), jnp.float32),
        ],
        scratch_shapes=[pltpu.VMEM((A_OUT // LANES, tm, LANES), jnp.float32)],
        compiler_params=_params(("arbitrary",)),
        name="qkv_projection",
    )(x2, g1, w_in, w_in, w_in, gain_row, ones_bd, cos_t, s1_t, s2_t)


A_UNIT = 4 * A_RADIUS
A_SPAN = A_UNIT + 2 * A_RADIUS


def _mixer_a_kernel(q_ref, kp_ref, ko_ref, kn_ref, vp_ref, vo_ref, vn_ref, bias_ref,
                    o_ref, lse_ref, kw_ref, vw_ref, s_ref, *, tl, seq_len):
    l0 = pl.program_id(2) * tl
    kw_ref[0:A_RADIUS] = kp_ref[0, 0]
    kw_ref[A_RADIUS:A_RADIUS + tl] = ko_ref[0, 0]
    kw_ref[A_RADIUS + tl:] = kn_ref[0, 0]
    vw_ref[0:A_RADIUS] = vp_ref[0, 0]
    vw_ref[A_RADIUS:A_RADIUS + tl] = vo_ref[0, 0]
    vw_ref[A_RADIUS + tl:] = vn_ref[0, 0]
    even_q = lax.broadcasted_iota(jnp.int32, (A_UNIT, LANES), 1) < HEAD_DIM
    lane_k = lax.broadcasted_iota(jnp.int32, (A_SPAN, LANES), 1)

    n_units = tl // A_UNIT

    def scores(u, pair):
        off = pl.multiple_of(u * A_UNIT, A_UNIT)
        q = q_ref[0, 0, pl.ds(off, A_UNIT), pair * LANES:(pair + 1) * LANES]
        zero = jnp.zeros_like(q)
        q2 = jnp.concatenate([jnp.where(even_q, q, zero), jnp.where(even_q, zero, q)], axis=0)
        return _dot_nt(kw_ref[pl.ds(off, A_SPAN), pair * LANES:(pair + 1) * LANES], q2)

    s_ref[...] = scores(0, 0)

    def unit(u, carry):
        off = pl.multiple_of(u * A_UNIT, A_UNIT)
        top_ok = l0 + off - A_RADIUS >= 0
        bot_ok = l0 + off + A_UNIT + A_RADIUS <= seq_len
        outs, lses = [], []
        s_next = s_ref[...]
        for pair in range(A_HEADS // 2):
            s_pair = s_next
            if pair + 1 < A_HEADS // 2:
                s_next = scores(u, pair + 1)
            else:
                s_ref[...] = scores(jnp.minimum(u + 1, n_units - 1), 0)
            v_slab = vw_ref[pl.ds(off, A_SPAN), pair * LANES:(pair + 1) * LANES]
            for odd in range(2):
                s = s_pair[:, odd * A_UNIT:(odd + 1) * A_UNIT] + bias_ref[2 * pair + odd]
                s = jnp.concatenate([jnp.where(top_ok, s[:A_RADIUS], NEG), s[A_RADIUS:A_SPAN - A_RADIUS],
                                     jnp.where(bot_ok, s[A_SPAN - A_RADIUS:], NEG)], axis=0)
                m = jnp.max(s, axis=0, keepdims=True)
                p = jnp.exp2(s - m).astype(_MXU_DTYPE)
                den_row = (1 - odd) * HEAD_DIM
                mine = (lane_k >= odd * HEAD_DIM) & (lane_k < (odd + 1) * HEAD_DIM)
                v_aug = jnp.where(mine, v_slab, (lane_k == den_row).astype(v_slab.dtype))
                acc = lax.dot_general(v_aug, p, (((0,), (0,)), ((), ())),
                                      preferred_element_type=jnp.float32)
                den = acc[den_row:den_row + 1]
                outs.append(acc[odd * HEAD_DIM:(odd + 1) * HEAD_DIM] / den)
                lses.append(jnp.broadcast_to(m + jnp.log2(den), (HEAD_DIM, A_UNIT)))
        o_ref[0, 0, pl.ds(off, A_UNIT), :] = jnp.concatenate(outs, axis=0).T
        lse_ref[0, 0, pl.ds(off, A_UNIT), :] = jnp.concatenate(lses, axis=0).T
        return carry

    lax.fori_loop(0, n_units, unit, 0, unroll=2)


def _mixer_a_group(za_g, bias, rate, B, S):
    L = S // rate
    tl = min(1024, L)
    nblk = L // A_RADIUS
    per = tl // A_RADIUS

    def own(section):
        return lambda b, r, l: (b, r, l, section)

    def prev(section):
        return lambda b, r, l: (b, r, jnp.maximum(l * per - 1, 0), section)

    def nxt(section):
        return lambda b, r, l: (b, r, jnp.minimum((l + 1) * per, nblk - 1), section)

    edge = (1, 1, A_RADIUS, A_OUT)
    full = (1, 1, tl, A_OUT)
    return pl.pallas_call(
        functools.partial(_mixer_a_kernel, tl=tl, seq_len=L),
        grid=(B, rate, L // tl),
        in_specs=[
            pl.BlockSpec(full, own(0)),
            pl.BlockSpec(edge, prev(1)), pl.BlockSpec(full, own(1)), pl.BlockSpec(edge, nxt(1)),
            pl.BlockSpec(edge, prev(2)), pl.BlockSpec(full, own(2)), pl.BlockSpec(edge, nxt(2)),
            _const_spec((A_HEADS, A_SPAN, A_UNIT)),
        ],
        out_specs=[pl.BlockSpec(full, own(0)), pl.BlockSpec(full, own(0))],
        out_shape=[jax.ShapeDtypeStruct((B, rate, L, A_OUT), jnp.float32)] * 2,
        scratch_shapes=[pltpu.VMEM((tl + 2 * A_RADIUS, A_OUT), _MXU_DTYPE)] * 2
        + [pltpu.VMEM((A_SPAN, 2 * A_UNIT), jnp.float32)],
        compiler_params=_params(("arbitrary",) * 3),
        name=f"mixer_a_rate{rate}",
    )(za_g, za_g, za_g, za_g, za_g, za_g, za_g, bias)


B_UNIT_ROWS = 4
B_UNIT = B_UNIT_ROWS * GRID_W
B_SPAN_ROWS = B_UNIT_ROWS + B_WIN_ROWS
B_SPAN = B_SPAN_ROWS * GRID_W
B_HALO = (B_WIN_ROWS // 2) * GRID_W
B_TILE_ROWS = 16
B_TILE = B_TILE_ROWS * GRID_W


def _mixer_b_kernel(q_ref, kp_ref, ko_ref, kn_ref, vp_ref, vo_ref, vn_ref, blocks_ref, sel_ref,
                    o_ref, kw_ref, vw_ref, bias_ref, s_ref, *, rows):
    @pl.when((pl.program_id(0) == 0) & (pl.program_id(1) == 0))
    def _():
        left = lax.broadcasted_iota(jnp.int32, (GRID_W, LANES), 1) < GRID_W

        def fill(n, carry):
            v, h = n // B_HEADS, n % B_HEADS
            for a in range(B_SPAN_ROWS):
                for gp in range(B_UNIT_ROWS // 2):
                    at = (v * B_SPAN_ROWS + a) * B_UNIT_ROWS + 2 * gp
                    tile = jnp.where(left, blocks_ref[h, sel_ref[at]], blocks_ref[h, sel_ref[at + 1]])
                    bias_ref[v, h, a * GRID_W:(a + 1) * GRID_W, gp * LANES:(gp + 1) * LANES] = tile
            return carry

        lax.fori_loop(0, 3 * B_HEADS, fill, 0)

    i0 = pl.program_id(1) * B_TILE_ROWS
    kw_ref[0:B_HALO] = kp_ref[0]
    kw_ref[B_HALO:B_HALO + B_TILE] = ko_ref[0]
    kw_ref[B_HALO + B_TILE:] = kn_ref[0]
    vw_ref[:, :, 0:B_HALO] = vp_ref[0]
    vw_ref[:, :, B_HALO:B_HALO + B_TILE] = vo_ref[0]
    vw_ref[:, :, B_HALO + B_TILE:] = vn_ref[0]
    tail = _ones_tail(B_SPAN, _MXU_DTYPE)
    upper = lax.broadcasted_iota(jnp.int32, (LANES, B_UNIT), 0) < HEAD_DIM

    n_units = B_TILE_ROWS // B_UNIT_ROWS

    def span_offset(u):
        r0 = jnp.clip(i0 + u * B_UNIT_ROWS - B_WIN_ROWS // 2, 0, rows - B_SPAN_ROWS)
        return pl.multiple_of((r0 - (i0 - B_WIN_ROWS // 2)) * GRID_W, LANES)

    def scores(u, pair):
        qt = q_ref[0, pair, :, pl.ds(pl.multiple_of(u * B_UNIT, B_UNIT), B_UNIT)]
        zero = jnp.zeros_like(qt)
        qt2 = jnp.concatenate([jnp.where(upper, qt, zero), jnp.where(upper, zero, qt)], axis=1)
        return _dot(kw_ref[pl.ds(span_offset(u), B_SPAN), pair * LANES:(pair + 1) * LANES], qt2)

    s_ref[...] = scores(0, 0)

    def unit(u, carry):
        i0u = i0 + u * B_UNIT_ROWS
        off = span_offset(u)
        variant = jnp.where(i0u == 0, 0, jnp.where(i0u == rows - B_UNIT_ROWS, 2, 1))
        qoff = pl.multiple_of(u * B_UNIT, B_UNIT)
        outs = []
        s_next = s_ref[...]
        for pair in range(B_PAIRS):
            s_pair = s_next
            if pair + 1 < B_PAIRS:
                s_next = scores(u, pair + 1)
            else:
                s_ref[...] = scores(jnp.minimum(u + 1, n_units - 1), 0)
            for odd in range(2):
                s = s_pair[:, odd * B_UNIT:(odd + 1) * B_UNIT] + bias_ref[variant, 2 * pair + odd]
                p = jnp.exp2(s - jnp.max(s, axis=0, keepdims=True)).astype(_MXU_DTYPE)
                vt = vw_ref[pair, odd * HEAD_DIM:(odd + 1) * HEAD_DIM, pl.ds(off, B_SPAN)]
                acc = _dot(jnp.concatenate([vt, tail], axis=0), p)
                outs.append(acc[:HEAD_DIM] / acc[HEAD_DIM:HEAD_DIM + 1])
        o_ref[0, pl.ds(qoff, B_UNIT), :] = jnp.concatenate(outs, axis=0).T.astype(o_ref.dtype)
        return carry

    lax.fori_loop(0, n_units, unit, 0, unroll=2)


def _mixer_b(qbt, kb, vbt, blocks, B, S):
    rows = S // GRID_W
    sel = jnp.asarray(_mixer_b_block_index(rows))
    nt = rows // B_TILE_ROWS
    per = B_TILE // B_HALO
    nh = S // B_HALO
    kb3 = kb.reshape(B, S, B_W)
    prev = lambda t: jnp.maximum(t * per - 1, 0)
    nxt = lambda t: jnp.minimum((t + 1) * per, nh - 1)
    k_edge, k_own = (1, B_HALO, B_W), (1, B_TILE, B_W)
    t_edge, t_own = (1, B_PAIRS, LANES, B_HALO), (1, B_PAIRS, LANES, B_TILE)
    o = pl.pallas_call(
        functools.partial(_mixer_b_kernel, rows=rows),
        grid=(B, nt),
        in_specs=[
            pl.BlockSpec(t_own, lambda b, t: (b, 0, 0, t)),
            pl.BlockSpec(k_edge, lambda b, t: (b, prev(t), 0)),
            pl.BlockSpec(k_own, lambda b, t: (b, t, 0)),
            pl.BlockSpec(k_edge, lambda b, t: (b, nxt(t), 0)),
            pl.BlockSpec(t_edge, lambda b, t: (b, 0, 0, prev(t))),
            pl.BlockSpec(t_own, lambda b, t: (b, 0, 0, t)),
            pl.BlockSpec(t_edge, lambda b, t: (b, 0, 0, nxt(t))),
            _const_spec(blocks.shape),
            pl.BlockSpec(memory_space=pltpu.SMEM),
        ],
        out_specs=pl.BlockSpec(k_own, lambda b, t: (b, t, 0)),
        out_shape=jax.ShapeDtypeStruct((B, S, B_W), _MXU_DTYPE),
        scratch_shapes=[pltpu.VMEM((B_TILE + 2 * B_HALO, B_W), _MXU_DTYPE),
                        pltpu.VMEM((B_PAIRS, LANES, B_TILE + 2 * B_HALO), _MXU_DTYPE),
                        pltpu.VMEM((3, B_HEADS, B_SPAN, B_UNIT), jnp.float32),
                        pltpu.VMEM((B_SPAN, 2 * B_UNIT), jnp.float32)],
        compiler_params=_params(("arbitrary",) * 2),
        name="mixer_b",
    )(qbt, kb3, kb3, kb3, vbt, vbt, vbt, blocks, sel)
    return o.reshape(B * S, B_W)


C_TQ = 1024
C_TK = 512
C_UNIT = 512
C_SHIFT_LIMIT = 60.0
C_UNROLL = 4


def _mixer_c_kernel(q_ref, k_ref, vt_ref, ksq_ref, o_ref, qcat_ref, m_ref, acc_ref, s_ref, *, n_kv):
    acc_ref[...] = jnp.zeros(acc_ref.shape, jnp.float32)
    for g in range(C_GROUP):
        qcat_ref[:, g * C_TQ:(g + 1) * C_TQ] = q_ref[0, g]
    n_units = C_GROUP * C_TQ // C_UNIT

    def keys(j):
        return k_ref[0, 0, pl.ds(pl.multiple_of(j * C_TK, C_TK), C_TK), :]

    def scores(k, u):
        return _dot(k, qcat_ref[:, u * C_UNIT:(u + 1) * C_UNIT])

    def sweep(update, unroll):
        s_ref[...] = scores(keys(0), 0)

        def step(j, carry):
            k = keys(j)
            vt = vt_ref[0, 0, :, pl.ds(pl.multiple_of(j * C_TK, C_TK), C_TK)]
            s_next = s_ref[...]
            for u in range(n_units):
                s = s_next
                if u + 1 < n_units:
                    s_next = scores(k, u + 1)
                else:
                    s_ref[...] = scores(keys(jnp.minimum(j + 1, n_kv - 1)), 0)
                update(s, vt, slice(u * C_UNIT, (u + 1) * C_UNIT))
            return carry

        lax.fori_loop(0, n_kv, step, 0, unroll=unroll)

    def fixed_shift(s, vt, cols):
        acc_ref[:, cols] += _dot(vt, jnp.exp2(s - m_ref[:, cols]).astype(_MXU_DTYPE))

    def running_max(s, vt, cols):
        m_prev = m_ref[:, cols]
        m_new = jnp.maximum(m_prev, jnp.max(s, axis=0, keepdims=True))
        alpha = jnp.exp2(m_prev - m_new)
        p = jnp.exp2(s - m_new).astype(_MXU_DTYPE)
        acc_ref[:, cols] = alpha * acc_ref[:, cols] + _dot(vt, p)
        m_ref[:, cols] = m_new

    qf = qcat_ref[...].astype(jnp.float32)
    bound = jnp.sqrt(jnp.sum(qf * qf, axis=0, keepdims=True) * ksq_ref[0, 0, :, :1])
    small = jnp.max(bound) <= C_SHIFT_LIMIT

    @pl.when(small)
    def _():
        m_ref[...] = bound
        sweep(fixed_shift, C_UNROLL)

    @pl.when(jnp.logical_not(small))
    def _():
        m_ref[...] = jnp.full(m_ref.shape, -jnp.inf, jnp.float32)
        sweep(running_max, 1)

    o_t = jnp.concatenate(
        [acc_ref[:HEAD_DIM, g * C_TQ:(g + 1) * C_TQ] / acc_ref[HEAD_DIM:HEAD_DIM + 1, g * C_TQ:(g + 1) * C_TQ]
         for g in range(C_GROUP)], axis=0)
    o_ref[0] = o_t.T.astype(o_ref.dtype)


def _mixer_c(qc_t, kc, vc_t, ksq, B, S):
    o = pl.pallas_call(
        functools.partial(_mixer_c_kernel, n_kv=S // C_TK),
        grid=(B, C_KV_HEADS, S // C_TQ),
        in_specs=[
            pl.BlockSpec((1, C_GROUP, HEAD_DIM, C_TQ), lambda b, kv, i: (b, kv, 0, i)),
            pl.BlockSpec((1, 1, S, HEAD_DIM), lambda b, kv, i: (b, kv, 0, 0)),
            pl.BlockSpec((1, 1, VT_ROWS, S), lambda b, kv, i: (b, kv, 0, 0)),
            pl.BlockSpec((1, 1, 1, LANES), lambda b, kv, i: (b, kv, 0, 0)),
        ],
        out_specs=pl.BlockSpec((1, C_TQ, C_GROUP * HEAD_DIM), lambda b, kv, i: (b, i, kv)),
        out_shape=jax.ShapeDtypeStruct((B, S, C_QW), _MXU_DTYPE),
        scratch_shapes=[pltpu.VMEM((HEAD_DIM, C_GROUP * C_TQ), _MXU_DTYPE),
                        pltpu.VMEM((1, C_GROUP * C_TQ), jnp.float32),
                        pltpu.VMEM((VT_ROWS, C_GROUP * C_TQ), jnp.float32),
                        pltpu.VMEM((C_TK, C_UNIT), jnp.float32)],
        compiler_params=_params(("arbitrary",) * 3),
        name="mixer_c",
    )(qc_t, kc, vc_t, ksq)
    return o.reshape(B * S, C_QW)


MERGE_TM = 512
MERGE_GATE_BLOCK = 512


def _merge_kernel(x_ref, g_ref, oa0_ref, oa1_ref, oa2_ref, l0_ref, l1_ref, l2_ref, ob_ref, oc_ref,
                  wg0_ref, wg1_ref, wg2_ref, wg3_ref, wg4_ref, wg5_ref, pa_ref, pb_ref, pc_ref, wo_ref,
                  out_ref, *scratch):
    tm = x_ref.shape[0]
    gate_refs = (wg0_ref, wg1_ref, wg2_ref, wg3_ref, wg4_ref, wg5_ref)
    cast = lambda ref: ref[...].astype(_MXU_DTYPE)

    def token_major(ref, scr):
        rate = ref.shape[1]
        halves = range(A_OUT // LANES)
        for r in range(rate):
            for j in halves:
                scr[j, pl.ds(r, tm // rate, stride=rate), :] = ref[0, r, :, j * LANES:(j + 1) * LANES]
        return jnp.concatenate([scr[j] for j in halves], axis=1)

    x = x_ref[...]
    h = _rms(x, g_ref[...]).astype(_MXU_DTYPE)
    oa0, l0 = oa0_ref[...], l0_ref[...]
    oa1, l1 = token_major(oa1_ref, scratch[0]), token_major(l1_ref, scratch[1])
    oa2, l2 = token_major(oa2_ref, scratch[2]), token_major(l2_ref, scratch[3])
    mx = jnp.maximum(jnp.maximum(l0, l1), l2)
    w0, w1, w2 = jnp.exp2(l0 - mx), jnp.exp2(l1 - mx), jnp.exp2(l2 - mx)
    o_a = (w0 * oa0 + w1 * oa1 + w2 * oa2) / (w0 + w1 + w2)
    branches = (_dot(o_a.astype(_MXU_DTYPE), cast(pa_ref)), _dot(ob_ref[...], cast(pb_ref)),
                _dot(oc_ref[...], cast(pc_ref)))
    parts = []
    for part in range(D_MODEL // MERGE_GATE_BLOCK):
        cols = slice(part * MERGE_GATE_BLOCK, (part + 1) * MERGE_GATE_BLOCK)
        merged = None
        for b in range(N_BRANCH):
            gate = jax.nn.sigmoid(_dot(h, cast(gate_refs[b * (D_MODEL // MERGE_GATE_BLOCK) + part])))
            term = gate * branches[b][:, cols]
            merged = term if merged is None else merged + term
        parts.append(merged.astype(_MXU_DTYPE))
    out_ref[...] = x + _dot(jnp.concatenate(parts, axis=1), cast(wo_ref))


def _merge(x2, g1, oa, lse, ob, oc, w_in, w_br_a, w_br_b, w_br_c, w_o, layer, B, S):
    M = x2.shape[0]
    tm = MERGE_TM
    per_seq = S // tm
    row = lambda i: (i, 0)
    tile = lambda w: pl.BlockSpec((tm, w), row)
    gate0 = _ZG // MERGE_GATE_BLOCK

    def layer_weight(w, block=None, at=0):
        block = w.shape[2] if block is None else block
        return pl.BlockSpec((None, w.shape[1], block), lambda i: (layer, 0, at), pipeline_mode=pl.Buffered(1))

    def dilated(rate):
        return pl.BlockSpec((1, rate, tm // rate, A_OUT), lambda i: (i // per_seq, 0, i % per_seq, 0))

    r1, r2 = A_PATTERNS[1][1], A_PATTERNS[2][1]
    return pl.pallas_call(
        _merge_kernel,
        grid=(M // tm,),
        in_specs=[tile(D_MODEL), _const_spec((1, D_MODEL)),
                  tile(A_OUT), dilated(r1), dilated(r2), tile(A_OUT), dilated(r1), dilated(r2),
                  tile(B_W), tile(C_QW),
                  *[layer_weight(w_in, MERGE_GATE_BLOCK, gate0 + j)
                    for j in range(N_BRANCH * D_MODEL // MERGE_GATE_BLOCK)],
                  layer_weight(w_br_a), layer_weight(w_br_b), layer_weight(w_br_c), layer_weight(w_o)],
        out_specs=tile(D_MODEL),
        out_shape=jax.ShapeDtypeStruct((M, D_MODEL), jnp.float32),
        scratch_shapes=[pltpu.VMEM((A_OUT // LANES, tm, LANES), jnp.float32)] * 4,
        compiler_params=_params(("arbitrary",)),
        name="gated_merge",
    )(x2, g1, oa[0].reshape(M, A_OUT), oa[1], oa[2], lse[0].reshape(M, A_OUT), lse[1], lse[2],
      ob, oc, *([w_in] * (N_BRANCH * D_MODEL // MERGE_GATE_BLOCK)), w_br_a, w_br_b, w_br_c, w_o)


FFN_TM = 512
FFN_CHUNK = 256
FFN_DOWN_CHUNK = D_FF // 2


def _ffn_kernel(x_ref, g_ref, wup_ref, wdown_ref, out_ref, act_ref):
    x = x_ref[...]
    xn = _rms(x, g_ref[...]).astype(_MXU_DTYPE)
    for c in range(D_FF // FFN_CHUNK):
        cs = slice(c * FFN_CHUNK, (c + 1) * FFN_CHUNK)
        a = _dot(xn, wup_ref[:, cs].astype(_MXU_DTYPE))
        b = _dot(xn, wup_ref[:, D_FF + c * FFN_CHUNK:D_FF + (c + 1) * FFN_CHUNK].astype(_MXU_DTYPE))
        act_ref[:, cs] = (a * jax.nn.sigmoid(a) * b).astype(act_ref.dtype)
    y = x
    for c in range(D_FF // FFN_DOWN_CHUNK):
        rows = slice(c * FFN_DOWN_CHUNK, (c + 1) * FFN_DOWN_CHUNK)
        y = y + _dot(act_ref[:, rows], wdown_ref[rows, :].astype(_MXU_DTYPE))
    out_ref[...] = y


def _ffn(x2, g2, w_up, w_down, layer):
    M = x2.shape[0]
    tm = FFN_TM
    row = lambda i: (i, 0)
    layer_weight = lambda w: pl.BlockSpec((None,) + w.shape[1:], lambda i: (layer, 0, 0),
                                          pipeline_mode=pl.Buffered(1))
    return pl.pallas_call(
        _ffn_kernel,
        grid=(M // tm,),
        in_specs=[pl.BlockSpec((tm, D_MODEL), row), _const_spec((1, D_MODEL)),
                  layer_weight(w_up), layer_weight(w_down)],
        out_specs=pl.BlockSpec((tm, D_MODEL), row),
        out_shape=jax.ShapeDtypeStruct((M, D_MODEL), jnp.float32),
        scratch_shapes=[pltpu.VMEM((tm, D_FF), _MXU_DTYPE)],
        compiler_params=_params(("arbitrary",)),
        name="swiglu_ffn",
    )(x2, g2, w_up, w_down)


def _t5_bucket(rel):
    half = T5_BUCKETS // 2
    max_exact = half // 2
    ret = jnp.where(rel > 0, half, 0)
    n = jnp.abs(rel)
    nf = jnp.maximum(n, 1).astype(jnp.float32)
    large = max_exact + (jnp.log(nf / max_exact) / math.log(T5_MAX_DIST / max_exact)
                         * (half - max_exact)).astype(jnp.int32)
    large = jnp.minimum(large, half - 1)
    return ret + jnp.where(n < max_exact, n, large)


def _mixer_a_bias(table_g, rate):
    j = jnp.arange(A_SPAN)[:, None]
    i = jnp.arange(A_UNIT)[None, :]
    step = j - A_RADIUS - i
    onehot = (_t5_bucket(step * rate)[:, :, None] == jnp.arange(T5_BUCKETS)).astype(jnp.float32)
    bias = jnp.einsum("jib,bh->hji", onehot, table_g.astype(jnp.float32) * LOG2_E,
                      precision=lax.Precision.HIGHEST)
    return jnp.where((jnp.abs(step) <= A_RADIUS)[None], bias, NEG)


B_N_DR = 2 * B_WIN_ROWS - 1


def _mixer_b_blocks(rpb):
    c = np.arange(GRID_W)
    c0 = np.clip(c - B_WIN_COLS // 2, 0, GRID_W - B_WIN_COLS)
    col_ok = (c[:, None] >= c0[None, :]) & (c[:, None] < c0[None, :] + B_WIN_COLS)
    dc = np.clip(c[:, None] - c[None, :] + B_WIN_COLS - 1, 0, 2 * B_WIN_COLS - 2)
    pick_c = ((dc[..., None] == np.arange(2 * B_WIN_COLS - 1)) & col_ok[..., None]).astype(np.float32)
    blocks = jnp.einsum("hdk,xyk->hdxy", rpb.astype(jnp.float32) * LOG2_E, pick_c,
                        precision=lax.Precision.HIGHEST)
    blocks = jnp.where(col_ok[None, None], blocks, NEG)
    blocks = jnp.concatenate([blocks, jnp.full((B_HEADS, 1, GRID_W, GRID_W), NEG, jnp.float32)], axis=1)
    return jnp.concatenate([blocks, blocks], axis=-1)


def _mixer_b_block_index(rows):
    first_query_row = np.array([0, B_WIN_ROWS // 2, rows - B_UNIT_ROWS])
    i = first_query_row[:, None, None] + np.arange(B_UNIT_ROWS)[None, None, :]
    r0 = np.clip(first_query_row - B_WIN_ROWS // 2, 0, rows - B_SPAN_ROWS)
    ik = r0[:, None, None] + np.arange(B_SPAN_ROWS)[None, :, None]
    rs = np.clip(i - B_WIN_ROWS // 2, 0, rows - B_WIN_ROWS)
    row_ok = (ik >= rs) & (ik < rs + B_WIN_ROWS)
    return np.where(row_ok, ik - i + B_WIN_ROWS - 1, B_N_DR).astype(np.int32).reshape(-1)


def _rope_tables(S):
    t = jnp.arange(S)
    inv = ROPE_THETA ** (-jnp.arange(0, ROPE_AXIS_DIM, 2, dtype=jnp.float32) / ROPE_AXIS_DIM)
    d = np.arange(LANES) % HEAD_DIM
    inv_lane = inv[d % (ROPE_AXIS_DIM // 2)][None, :]
    is_col = (d >= ROPE_AXIS_DIM)[None, :]
    pos = jnp.where(is_col, (t % GRID_W)[:, None], (t // GRID_W)[:, None]).astype(jnp.float32)
    ang = pos * inv_lane
    first = ((d % ROPE_AXIS_DIM) < ROPE_AXIS_DIM // 2)[None, :]
    sin = jnp.sin(ang)
    return jnp.cos(ang), jnp.where(first, -sin, 0.0), jnp.where(first, 0.0, sin)


def _block_diag_ones():
    i = np.arange(256) // HEAD_DIM
    return jnp.asarray(i[:, None] == i[None, :], _MXU_DTYPE)


def kernel(x, rel_bias_table, norm1, w_in, qk_gain, nat_rpb, w_br_a, w_br_b, w_br_c, w_o,
           norm2, w_up, w_down):
    B, S, D = x.shape
    depth = w_in.shape[0]
    M = B * S
    x2 = x.reshape(M, D)
    rope = _rope_tables(S)
    ones_bd = _block_diag_ones()
    a_bias = [_mixer_a_bias(rel_bias_table[:, g * A_HEADS:(g + 1) * A_HEADS], rate)
              for g, (_, rate) in enumerate(A_PATTERNS)]
    for l in range(depth):
        gq = qk_gain[l]
        tile = lambda g, n, s: jnp.tile(g * s, n)
        gain_row = jnp.concatenate([
            tile(gq[0], A_W // HEAD_DIM, QK_SCALE * LOG2_E), tile(gq[1], A_W // HEAD_DIM, 1.0),
            tile(gq[2], B_HEADS, QK_SCALE * LOG2_E), tile(gq[3], B_HEADS, 1.0),
            tile(gq[4], C_Q_HEADS, QK_SCALE * LOG2_E), tile(gq[5], C_KV_HEADS, 1.0)])[None, :]
        g1 = norm1[l][None, :]
        za0, za1, za2, qbt, kb, vbt, qc, kc, vc, ksq = _qkv_projection(
            x2, g1, w_in, l, gain_row, ones_bd, rope, B, S)
        ksq = jnp.max(ksq.reshape(B, -1, C_KV_HEADS, HEAD_DIM), axis=(1, 3))
        ksq = jnp.broadcast_to(ksq[:, :, None, None], (B, C_KV_HEADS, 1, LANES))
        za = (za0.reshape(B, 1, S, ZA_W), za1, za2)
        oa, lse = [], []
        for g, (_, rate) in enumerate(A_PATTERNS):
            o_g, l_g = _mixer_a_group(za[g], a_bias[g], rate, B, S)
            oa.append(o_g)
            lse.append(l_g)
        ob = _mixer_b(qbt, kb, vbt, _mixer_b_blocks(nat_rpb[l]), B, S)
        oc = _mixer_c(qc, kc, vc, ksq, B, S)
        x2 = _merge(x2, g1, oa, lse, ob, oc, w_in, w_br_a, w_br_b, w_br_c, w_o, l, B, S)
        x2 = _ffn(x2, norm2[l][None, :], w_up, w_down, l)
    return x2.reshape(B, S, D)
```

```python
import functools
import math

import jax
import jax.numpy as jnp
import numpy as np
from jax import lax
from jax.experimental import pallas as pl
from jax.experimental.pallas import tpu as pltpu

_MXU_DTYPE = jnp.bfloat16

D_MODEL = 1024
HEAD_DIM = 64
GRID_W = 64
RMS_EPS = 1e-6
NEG = -1e30
A_PATTERNS = ((128, 1), (512, 4), (2048, 16))
A_GROUPS = 3
A_HEADS = 4
A_W = A_GROUPS * A_HEADS * HEAD_DIM
A_OUT = A_HEADS * HEAD_DIM
A_RADIUS = 64
B_HEADS = 8
B_W = B_HEADS * HEAD_DIM
B_WIN_ROWS = 8
B_WIN_COLS = 16
C_Q_HEADS = 8
C_KV_HEADS = 2
C_GROUP = C_Q_HEADS // C_KV_HEADS
C_QW = C_Q_HEADS * HEAD_DIM
C_KVW = C_KV_HEADS * HEAD_DIM
ROPE_THETA = 10000.0
ROPE_AXIS_DIM = HEAD_DIM // 2
T5_BUCKETS = 32
T5_MAX_DIST = 1024
N_BRANCH = 3
D_FF = math.ceil(8 * D_MODEL / 3 / 256) * 256
QK_SCALE = HEAD_DIM ** -0.5
LOG2_E = math.log2(math.e)

VMEM_LIMIT_BYTES = 56 * 1024 * 1024
LANES = 128
BF16_SUBLANES = 16
VT_ROWS = HEAD_DIM + BF16_SUBLANES
SHIFT_LIMIT = 60.0
ROUNDING_SLACK = 1.02

_OFF = np.cumsum([0, A_W, A_W, A_W, B_W, B_W, B_W, C_QW, C_KVW, C_KVW]).tolist()
(_QA, _KA, _VA, _QB, _KB, _VB, _QC, _KC, _VC, _ZG) = _OFF

ZA_W = 3 * A_OUT
B_PAIRS = B_HEADS // 2


def _params(sem):
    return pltpu.CompilerParams(dimension_semantics=sem, vmem_limit_bytes=VMEM_LIMIT_BYTES)


def _const_spec(shape):
    nd = len(shape)
    return pl.BlockSpec(shape, lambda *_: (0,) * nd, pipeline_mode=pl.Buffered(1))


def _rms(x, g):
    return x * lax.rsqrt(jnp.mean(x * x, axis=-1, keepdims=True) + RMS_EPS) * g


def _dot(a, b):
    return jnp.dot(a, b, preferred_element_type=jnp.float32)


def _dot_nt(a, b):
    return lax.dot_general(a, b, (((1,), (1,)), ((), ())), preferred_element_type=jnp.float32)


def _ones_tail(width, dtype):
    return (lax.broadcasted_iota(jnp.int32, (VT_ROWS - HEAD_DIM, width), 0) == 0).astype(dtype)


PROJ_TM = 512
PROJ_W_BLOCK = 1536
_N_NORM = 2 * A_W + 2 * B_W
_N_ROPE = C_QW + C_KVW


def _proj_kernel(x_ref, g_ref, w0_ref, w1_ref, w2_ref, gain_ref, ones_ref, cos_ref, s1_ref, s2_ref,
                 za0_ref, za1_ref, za2_ref, qbt_ref, kb_ref, vbt_ref, qc_ref, kc_ref, vc_ref, dil_ref):
    tm = x_ref.shape[0]
    w_refs = (w0_ref, w1_ref, w2_ref)
    h = _rms(x_ref[...], g_ref[...]).astype(_MXU_DTYPE)
    za_refs = (za0_ref, za1_ref, za2_ref)

    def head_norm(acc, c0, width):
        sq = (acc * acc).astype(_MXU_DTYPE)
        ms = _dot(sq, ones_ref[:width, :width]) * (1.0 / HEAD_DIM)
        return acc * lax.rsqrt(ms + RMS_EPS) * gain_ref[:, c0:c0 + width]

    def store_group(g, section, val):
        cols = slice(section * A_OUT, (section + 1) * A_OUT)
        rate = A_PATTERNS[g][1]
        if rate == 1:
            za0_ref[:, cols] = val.astype(za0_ref.dtype)
            return
        for j in range(A_OUT // LANES):
            dil_ref[j] = val[:, j * LANES:(j + 1) * LANES]
        for r in range(rate):
            picked = [dil_ref[j, pl.ds(r, tm // rate, stride=rate), :] for j in range(A_OUT // LANES)]
            za_refs[g][0, r, :, cols] = jnp.concatenate(picked, axis=1).astype(za_refs[g].dtype)

    def store_pairs(ref, t, val):
        vt = val.T.astype(ref.dtype)
        ref[0, 2 * t] = vt[:LANES]
        ref[0, 2 * t + 1] = vt[LANES:]

    def rotary(y, width):
        reps = width // LANES
        table = lambda ref: jnp.concatenate([ref[...]] * reps, axis=1) if reps > 1 else ref[...]
        return (y * table(cos_ref) + pltpu.roll(y, width - 16, 1) * table(s1_ref)
                + pltpu.roll(y, 16, 1) * table(s2_ref))

    def store_qc(t, acc, gain_at):
        yt = rotary(head_norm(acc, gain_at, 256), 256).T.astype(qc_ref.dtype)
        for j in range(256 // HEAD_DIM):
            qc_ref[0, 4 * t + j] = yt[j * HEAD_DIM:(j + 1) * HEAD_DIM]

    def store_kc_vc(acc, gain_at):
        y = rotary(head_norm(acc[:, :LANES], gain_at, LANES), LANES).astype(kc_ref.dtype)
        acc_t = acc[:, LANES:].T.astype(vc_ref.dtype)
        tail = _ones_tail(tm, vc_ref.dtype)
        for kv in range(C_KV_HEADS):
            kc_ref[0, kv] = y[:, kv * HEAD_DIM:(kv + 1) * HEAD_DIM]
            vc_ref[0, kv, :HEAD_DIM] = acc_t[kv * HEAD_DIM:(kv + 1) * HEAD_DIM]
            vc_ref[0, kv, HEAD_DIM:] = tail

    jobs = []
    for section, src in enumerate((_QA, _KA)):
        for g in range(A_GROUPS):
            gain_at = section * A_W + g * A_OUT
            jobs.append((src + g * A_OUT, A_OUT, lambda acc, gain_at=gain_at, g=g, section=section:
                         store_group(g, section, head_norm(acc, gain_at, A_OUT))))
    for t in range(B_W // 256):
        gain_at = 2 * A_W + t * 256
        jobs.append((_QB + t * 256, 256, lambda acc, gain_at=gain_at, t=t:
                     store_pairs(qbt_ref, t, head_norm(acc, gain_at, 256))))
    for t in range(B_W // 256):
        gain_at = 2 * A_W + B_W + t * 256
        def store_kb(acc, gain_at=gain_at, t=t):
            kb_ref[:, t * 256:(t + 1) * 256] = head_norm(acc, gain_at, 256).astype(kb_ref.dtype)
        jobs.append((_KB + t * 256, 256, store_kb))
    for t in range(C_QW // 256):
        jobs.append((_QC + t * 256, 256, lambda acc, t=t: store_qc(t, acc, _N_NORM + t * 256)))
    jobs.append((_KC, 2 * C_KVW, lambda acc: store_kc_vc(acc, _N_NORM + C_QW)))
    for g in range(A_GROUPS):
        jobs.append((_VA + g * A_OUT, A_OUT, lambda acc, g=g: store_group(g, 2, acc)))
    for t in range(B_W // 256):
        jobs.append((_VB + t * 256, 256, lambda acc, t=t: store_pairs(vbt_ref, t, acc)))

    def product(n):
        src, width, _ = jobs[n]
        blk, col = divmod(src, PROJ_W_BLOCK)
        return _dot(h, w_refs[blk][:, col:col + width].astype(_MXU_DTYPE))

    acc_next = product(0)
    for n, (_, _, consume) in enumerate(jobs):
        acc = acc_next
        if n + 1 < len(jobs):
            acc_next = product(n + 1)
        consume(acc)


def _qkv_projection(x2, g1, w_in, layer, gain_row, ones_bd, rope, B, S):
    M = x2.shape[0]
    tm = PROJ_TM
    per_seq = S // tm
    w_block = lambda j: pl.BlockSpec((None, D_MODEL, PROJ_W_BLOCK), lambda i: (layer, 0, j),
                                     pipeline_mode=pl.Buffered(1))
    cos_t, s1_t, s2_t = rope
    r1, r2 = A_PATTERNS[1][1], A_PATTERNS[2][1]
    row = lambda i: (i, 0)
    pos = lambda i: (i % per_seq, 0)
    hm = lambda i: (i // per_seq, 0, i % per_seq, 0)
    hm_t = lambda i: (i // per_seq, 0, 0, i % per_seq)
    cd = _MXU_DTYPE
    return pl.pallas_call(
        _proj_kernel,
        grid=(M // tm,),
        in_specs=[
            pl.BlockSpec((tm, D_MODEL), row),
            _const_spec((1, D_MODEL)),
            w_block(0), w_block(1), w_block(2),
            _const_spec((1, _N_NORM + _N_ROPE)),
            _const_spec((256, 256)),
            pl.BlockSpec((tm, LANES), pos),
            pl.BlockSpec((tm, LANES), pos),
            pl.BlockSpec((tm, LANES), pos),
        ],
        out_specs=[
            pl.BlockSpec((tm, ZA_W), row),
            pl.BlockSpec((1, r1, tm // r1, ZA_W), hm),
            pl.BlockSpec((1, r2, tm // r2, ZA_W), hm),
            pl.BlockSpec((1, B_PAIRS, LANES, tm), hm_t),
            pl.BlockSpec((tm, B_W), row),
            pl.BlockSpec((1, B_PAIRS, LANES, tm), hm_t),
            pl.BlockSpec((1, C_Q_HEADS, HEAD_DIM, tm), hm_t),
            pl.BlockSpec((1, C_KV_HEADS, tm, HEAD_DIM), hm),
            pl.BlockSpec((1, C_KV_HEADS, VT_ROWS, tm), hm_t),
        ],
        out_shape=[
            jax.ShapeDtypeStruct((M, ZA_W), cd),
            jax.ShapeDtypeStruct((B, r1, S // r1, ZA_W), cd),
            jax.ShapeDtypeStruct((B, r2, S // r2, ZA_W), cd),
            jax.ShapeDtypeStruct((B, B_PAIRS, LANES, S), cd),
            jax.ShapeDtypeStruct((M, B_W), cd),
            jax.ShapeDtypeStruct((B, B_PAIRS, LANES, S), cd),
            jax.ShapeDtypeStruct((B, C_Q_HEADS, HEAD_DIM, S), cd),
            jax.ShapeDtypeStruct((B, C_KV_HEADS, S, HEAD_DIM), cd),
            jax.ShapeDtypeStruct((B, C_KV_HEADS, VT_ROWS, S), cd),
        ],
        scratch_shapes=[pltpu.VMEM((A_OUT // LANES, tm, LANES), jnp.float32)],
        compiler_params=_params(("arbitrary",)),
        name="qkv_projection",
    )(x2, g1, w_in, w_in, w_in, gain_row, ones_bd, cos_t, s1_t, s2_t)


A_UNIT = 4 * A_RADIUS
A_SPAN = A_UNIT + 2 * A_RADIUS


def _mixer_a_kernel(q_ref, kp_ref, ko_ref, kn_ref, vp_ref, vo_ref, vn_ref, bias_ref, shift_ref,
                    o_ref, lse_ref, kw_ref, vw_ref, s_ref, *, tl, seq_len):
    l0 = pl.program_id(2) * tl
    kw_ref[0:A_RADIUS] = kp_ref[0, 0]
    kw_ref[A_RADIUS:A_RADIUS + tl] = ko_ref[0, 0]
    kw_ref[A_RADIUS + tl:] = kn_ref[0, 0]
    vw_ref[0:A_RADIUS] = vp_ref[0, 0]
    vw_ref[A_RADIUS:A_RADIUS + tl] = vo_ref[0, 0]
    vw_ref[A_RADIUS + tl:] = vn_ref[0, 0]
    even_q = lax.broadcasted_iota(jnp.int32, (A_UNIT, LANES), 1) < HEAD_DIM
    lane_k = lax.broadcasted_iota(jnp.int32, (A_SPAN, LANES), 1)

    n_units = tl // A_UNIT

    def scores(u, pair):
        off = pl.multiple_of(u * A_UNIT, A_UNIT)
        q = q_ref[0, 0, pl.ds(off, A_UNIT), pair * LANES:(pair + 1) * LANES]
        zero = jnp.zeros_like(q)
        q2 = jnp.concatenate([jnp.where(even_q, q, zero), jnp.where(even_q, zero, q)], axis=0)
        return _dot_nt(kw_ref[pl.ds(off, A_SPAN), pair * LANES:(pair + 1) * LANES], q2)

    s_ref[...] = scores(0, 0)

    def unit(fixed_shift, u, carry):
        off = pl.multiple_of(u * A_UNIT, A_UNIT)
        top_ok = l0 + off - A_RADIUS >= 0
        bot_ok = l0 + off + A_UNIT + A_RADIUS <= seq_len
        outs, lses = [], []
        s_next = s_ref[...]
        for pair in range(A_HEADS // 2):
            s_pair = s_next
            if pair + 1 < A_HEADS // 2:
                s_next = scores(u, pair + 1)
            else:
                s_ref[...] = scores(jnp.minimum(u + 1, n_units - 1), 0)
            v_slab = vw_ref[pl.ds(off, A_SPAN), pair * LANES:(pair + 1) * LANES]
            for odd in range(2):
                s = s_pair[:, odd * A_UNIT:(odd + 1) * A_UNIT] + bias_ref[2 * pair + odd]
                s = jnp.concatenate([jnp.where(top_ok, s[:A_RADIUS], NEG), s[A_RADIUS:A_SPAN - A_RADIUS],
                                     jnp.where(bot_ok, s[A_SPAN - A_RADIUS:], NEG)], axis=0)
                if fixed_shift:
                    m = shift_ref[0]
                    p = jnp.exp2(s).astype(_MXU_DTYPE)
                else:
                    m = jnp.max(s, axis=0, keepdims=True)
                    p = jnp.exp2(s - m).astype(_MXU_DTYPE)
                den_row = (1 - odd) * HEAD_DIM
                mine = (lane_k >= odd * HEAD_DIM) & (lane_k < (odd + 1) * HEAD_DIM)
                v_aug = jnp.where(mine, v_slab, (lane_k == den_row).astype(v_slab.dtype))
                acc = lax.dot_general(v_aug, p, (((0,), (0,)), ((), ())),
                                      preferred_element_type=jnp.float32)
                den = acc[den_row:den_row + 1]
                outs.append(acc[odd * HEAD_DIM:(odd + 1) * HEAD_DIM] / den)
                lses.append(jnp.broadcast_to(m + jnp.log2(den), (HEAD_DIM, A_UNIT)))
        o_ref[0, 0, pl.ds(off, A_UNIT), :] = jnp.concatenate(outs, axis=0).T
        lse_ref[0, 0, pl.ds(off, A_UNIT), :] = jnp.concatenate(lses, axis=0).T
        return carry

    small = shift_ref[1] > 0.5

    @pl.when(small)
    def _():
        lax.fori_loop(0, n_units, functools.partial(unit, True), 0, unroll=2)

    @pl.when(jnp.logical_not(small))
    def _():
        lax.fori_loop(0, n_units, functools.partial(unit, False), 0)


def _mixer_a_group(za_g, bias, shift, rate, B, S):
    L = S // rate
    tl = min(1024, L)
    nblk = L // A_RADIUS
    per = tl // A_RADIUS

    def own(section):
        return lambda b, r, l: (b, r, l, section)

    def prev(section):
        return lambda b, r, l: (b, r, jnp.maximum(l * per - 1, 0), section)

    def nxt(section):
        return lambda b, r, l: (b, r, jnp.minimum((l + 1) * per, nblk - 1), section)

    edge = (1, 1, A_RADIUS, A_OUT)
    full = (1, 1, tl, A_OUT)
    return pl.pallas_call(
        functools.partial(_mixer_a_kernel, tl=tl, seq_len=L),
        grid=(B, rate, L // tl),
        in_specs=[
            pl.BlockSpec(full, own(0)),
            pl.BlockSpec(edge, prev(1)), pl.BlockSpec(full, own(1)), pl.BlockSpec(edge, nxt(1)),
            pl.BlockSpec(edge, prev(2)), pl.BlockSpec(full, own(2)), pl.BlockSpec(edge, nxt(2)),
            _const_spec((A_HEADS, A_SPAN, A_UNIT)),
            pl.BlockSpec(memory_space=pltpu.SMEM),
        ],
        out_specs=[pl.BlockSpec(full, own(0)), pl.BlockSpec(full, own(0))],
        out_shape=[jax.ShapeDtypeStruct((B, rate, L, A_OUT), jnp.float32)] * 2,
        scratch_shapes=[pltpu.VMEM((tl + 2 * A_RADIUS, A_OUT), _MXU_DTYPE)] * 2
        + [pltpu.VMEM((A_SPAN, 2 * A_UNIT), jnp.float32)],
        compiler_params=_params(("arbitrary",) * 3),
        name=f"mixer_a_rate{rate}",
    )(za_g, za_g, za_g, za_g, za_g, za_g, za_g, bias, shift)


B_UNIT_ROWS = 4
B_UNIT = B_UNIT_ROWS * GRID_W
B_SPAN_ROWS = B_UNIT_ROWS + B_WIN_ROWS
B_SPAN = B_SPAN_ROWS * GRID_W
B_HALO = (B_WIN_ROWS // 2) * GRID_W
B_TILE_ROWS = 16
B_TILE = B_TILE_ROWS * GRID_W


def _mixer_b_kernel(q_ref, kp_ref, ko_ref, kn_ref, vp_ref, vo_ref, vn_ref, blocks_ref, sel_ref, shift_ref,
                    o_ref, kw_ref, vw_ref, bias_ref, s_ref, *, rows):
    @pl.when((pl.program_id(0) == 0) & (pl.program_id(1) == 0))
    def _():
        left = lax.broadcasted_iota(jnp.int32, (GRID_W, LANES), 1) < GRID_W

        def fill(n, carry):
            v, h = n // B_HEADS, n % B_HEADS
            for a in range(B_SPAN_ROWS):
                for gp in range(B_UNIT_ROWS // 2):
                    at = (v * B_SPAN_ROWS + a) * B_UNIT_ROWS + 2 * gp
                    tile = jnp.where(left, blocks_ref[h, sel_ref[at]], blocks_ref[h, sel_ref[at + 1]])
                    bias_ref[v, h, a * GRID_W:(a + 1) * GRID_W, gp * LANES:(gp + 1) * LANES] = tile
            return carry

        lax.fori_loop(0, 3 * B_HEADS, fill, 0)

    i0 = pl.program_id(1) * B_TILE_ROWS
    kw_ref[0:B_HALO] = kp_ref[0]
    kw_ref[B_HALO:B_HALO + B_TILE] = ko_ref[0]
    kw_ref[B_HALO + B_TILE:] = kn_ref[0]
    vw_ref[:, :, 0:B_HALO] = vp_ref[0]
    vw_ref[:, :, B_HALO:B_HALO + B_TILE] = vo_ref[0]
    vw_ref[:, :, B_HALO + B_TILE:] = vn_ref[0]
    tail = _ones_tail(B_SPAN, _MXU_DTYPE)
    upper = lax.broadcasted_iota(jnp.int32, (LANES, B_UNIT), 0) < HEAD_DIM

    n_units = B_TILE_ROWS // B_UNIT_ROWS

    def span_offset(u):
        r0 = jnp.clip(i0 + u * B_UNIT_ROWS - B_WIN_ROWS // 2, 0, rows - B_SPAN_ROWS)
        return pl.multiple_of((r0 - (i0 - B_WIN_ROWS // 2)) * GRID_W, LANES)

    def scores(u, pair):
        qt = q_ref[0, pair, :, pl.ds(pl.multiple_of(u * B_UNIT, B_UNIT), B_UNIT)]
        zero = jnp.zeros_like(qt)
        qt2 = jnp.concatenate([jnp.where(upper, qt, zero), jnp.where(upper, zero, qt)], axis=1)
        return _dot(kw_ref[pl.ds(span_offset(u), B_SPAN), pair * LANES:(pair + 1) * LANES], qt2)

    s_ref[...] = scores(0, 0)

    def unit(fixed_shift, u, carry):
        i0u = i0 + u * B_UNIT_ROWS
        off = span_offset(u)
        variant = jnp.where(i0u == 0, 0, jnp.where(i0u == rows - B_UNIT_ROWS, 2, 1))
        qoff = pl.multiple_of(u * B_UNIT, B_UNIT)
        outs = []
        s_next = s_ref[...]
        for pair in range(B_PAIRS):
            s_pair = s_next
            if pair + 1 < B_PAIRS:
                s_next = scores(u, pair + 1)
            else:
                s_ref[...] = scores(jnp.minimum(u + 1, n_units - 1), 0)
            for odd in range(2):
                s = s_pair[:, odd * B_UNIT:(odd + 1) * B_UNIT] + bias_ref[variant, 2 * pair + odd]
                if not fixed_shift:
                    s = s - jnp.max(s, axis=0, keepdims=True)
                p = jnp.exp2(s).astype(_MXU_DTYPE)
                vt = vw_ref[pair, odd * HEAD_DIM:(odd + 1) * HEAD_DIM, pl.ds(off, B_SPAN)]
                acc = _dot(jnp.concatenate([vt, tail], axis=0), p)
                outs.append(acc[:HEAD_DIM] / acc[HEAD_DIM:HEAD_DIM + 1])
        o_ref[0, pl.ds(qoff, B_UNIT), :] = jnp.concatenate(outs, axis=0).T.astype(o_ref.dtype)
        return carry

    small = shift_ref[1] > 0.5

    @pl.when(small)
    def _():
        lax.fori_loop(0, n_units, functools.partial(unit, True), 0, unroll=2)

    @pl.when(jnp.logical_not(small))
    def _():
        lax.fori_loop(0, n_units, functools.partial(unit, False), 0)


def _mixer_b(qbt, kb, vbt, blocks, shift, B, S):
    rows = S // GRID_W
    sel = jnp.asarray(_mixer_b_block_index(rows))
    nt = rows // B_TILE_ROWS
    per = B_TILE // B_HALO
    nh = S // B_HALO
    kb3 = kb.reshape(B, S, B_W)
    prev = lambda t: jnp.maximum(t * per - 1, 0)
    nxt = lambda t: jnp.minimum((t + 1) * per, nh - 1)
    k_edge, k_own = (1, B_HALO, B_W), (1, B_TILE, B_W)
    t_edge, t_own = (1, B_PAIRS, LANES, B_HALO), (1, B_PAIRS, LANES, B_TILE)
    o = pl.pallas_call(
        functools.partial(_mixer_b_kernel, rows=rows),
        grid=(B, nt),
        in_specs=[
            pl.BlockSpec(t_own, lambda b, t: (b, 0, 0, t)),
            pl.BlockSpec(k_edge, lambda b, t: (b, prev(t), 0)),
            pl.BlockSpec(k_own, lambda b, t: (b, t, 0)),
            pl.BlockSpec(k_edge, lambda b, t: (b, nxt(t), 0)),
            pl.BlockSpec(t_edge, lambda b, t: (b, 0, 0, prev(t))),
            pl.BlockSpec(t_own, lambda b, t: (b, 0, 0, t)),
            pl.BlockSpec(t_edge, lambda b, t: (b, 0, 0, nxt(t))),
            _const_spec(blocks.shape),
            pl.BlockSpec(memory_space=pltpu.SMEM),
            pl.BlockSpec(memory_space=pltpu.SMEM),
        ],
        out_specs=pl.BlockSpec(k_own, lambda b, t: (b, t, 0)),
        out_shape=jax.ShapeDtypeStruct((B, S, B_W), _MXU_DTYPE),
        scratch_shapes=[pltpu.VMEM((B_TILE + 2 * B_HALO, B_W), _MXU_DTYPE),
                        pltpu.VMEM((B_PAIRS, LANES, B_TILE + 2 * B_HALO), _MXU_DTYPE),
                        pltpu.VMEM((3, B_HEADS, B_SPAN, B_UNIT), jnp.float32),
                        pltpu.VMEM((B_SPAN, 2 * B_UNIT), jnp.float32)],
        compiler_params=_params(("arbitrary",) * 2),
        name="mixer_b",
    )(qbt, kb3, kb3, kb3, vbt, vbt, vbt, blocks, sel, shift)
    return o.reshape(B * S, B_W)


C_TQ = 1024
C_TK = 512
C_UNIT = 512
C_UNROLL = 4


def _mixer_c_kernel(q_ref, k_ref, vt_ref, shift_ref, o_ref, qcat_ref, m_ref, acc_ref, s_ref, *, n_kv):
    acc_ref[...] = jnp.zeros(acc_ref.shape, jnp.float32)
    for g in range(C_GROUP):
        qcat_ref[:, g * C_TQ:(g + 1) * C_TQ] = q_ref[0, g]
    n_units = C_GROUP * C_TQ // C_UNIT

    def keys(j):
        return k_ref[0, 0, pl.ds(pl.multiple_of(j * C_TK, C_TK), C_TK), :]

    def scores(k, u):
        return _dot(k, qcat_ref[:, u * C_UNIT:(u + 1) * C_UNIT])

    def sweep(update, unroll):
        s_ref[...] = scores(keys(0), 0)

        def step(j, carry):
            k = keys(j)
            vt = vt_ref[0, 0, :, pl.ds(pl.multiple_of(j * C_TK, C_TK), C_TK)]
            s_next = s_ref[...]
            for u in range(n_units):
                s = s_next
                if u + 1 < n_units:
                    s_next = scores(k, u + 1)
                else:
                    s_ref[...] = scores(keys(jnp.minimum(j + 1, n_kv - 1)), 0)
                update(s, vt, slice(u * C_UNIT, (u + 1) * C_UNIT))
            return carry

        lax.fori_loop(0, n_kv, step, 0, unroll=unroll)

    def fixed_shift(s, vt, cols):
        acc_ref[:, cols] += _dot(vt, jnp.exp2(s - shift_ref[0]).astype(_MXU_DTYPE))

    def running_max(s, vt, cols):
        m_prev = m_ref[:, cols]
        m_new = jnp.maximum(m_prev, jnp.max(s, axis=0, keepdims=True))
        alpha = jnp.exp2(m_prev - m_new)
        p = jnp.exp2(s - m_new).astype(_MXU_DTYPE)
        acc_ref[:, cols] = alpha * acc_ref[:, cols] + _dot(vt, p)
        m_ref[:, cols] = m_new

    small = shift_ref[1] > 0.5

    @pl.when(small)
    def _():
        sweep(fixed_shift, C_UNROLL)

    @pl.when(jnp.logical_not(small))
    def _():
        m_ref[...] = jnp.full(m_ref.shape, -jnp.inf, jnp.float32)
        sweep(running_max, 1)

    o_t = jnp.concatenate(
        [acc_ref[:HEAD_DIM, g * C_TQ:(g + 1) * C_TQ] / acc_ref[HEAD_DIM:HEAD_DIM + 1, g * C_TQ:(g + 1) * C_TQ]
         for g in range(C_GROUP)], axis=0)
    o_ref[0] = o_t.T.astype(o_ref.dtype)


def _mixer_c(qc_t, kc, vc_t, shift, B, S):
    o = pl.pallas_call(
        functools.partial(_mixer_c_kernel, n_kv=S // C_TK),
        grid=(B, C_KV_HEADS, S // C_TQ),
        in_specs=[
            pl.BlockSpec((1, C_GROUP, HEAD_DIM, C_TQ), lambda b, kv, i: (b, kv, 0, i)),
            pl.BlockSpec((1, 1, S, HEAD_DIM), lambda b, kv, i: (b, kv, 0, 0)),
            pl.BlockSpec((1, 1, VT_ROWS, S), lambda b, kv, i: (b, kv, 0, 0)),
            pl.BlockSpec(memory_space=pltpu.SMEM),
        ],
        out_specs=pl.BlockSpec((1, C_TQ, C_GROUP * HEAD_DIM), lambda b, kv, i: (b, i, kv)),
        out_shape=jax.ShapeDtypeStruct((B, S, C_QW), _MXU_DTYPE),
        scratch_shapes=[pltpu.VMEM((HEAD_DIM, C_GROUP * C_TQ), _MXU_DTYPE),
                        pltpu.VMEM((1, C_GROUP * C_TQ), jnp.float32),
                        pltpu.VMEM((VT_ROWS, C_GROUP * C_TQ), jnp.float32),
                        pltpu.VMEM((C_TK, C_UNIT), jnp.float32)],
        compiler_params=_params(("arbitrary",) * 3),
        name="mixer_c",
    )(qc_t, kc, vc_t, shift)
    return o.reshape(B * S, C_QW)


MERGE_TM = 512
MERGE_GATE_BLOCK = 512


def _merge_kernel(x_ref, g_ref, oa0_ref, oa1_ref, oa2_ref, l0_ref, l1_ref, l2_ref, ob_ref, oc_ref,
                  wg0_ref, wg1_ref, wg2_ref, wg3_ref, wg4_ref, wg5_ref, pa_ref, pb_ref, pc_ref, wo_ref,
                  out_ref, *scratch):
    tm = x_ref.shape[0]
    gate_refs = (wg0_ref, wg1_ref, wg2_ref, wg3_ref, wg4_ref, wg5_ref)
    cast = lambda ref: ref[...].astype(_MXU_DTYPE)

    def token_major(ref, scr):
        rate = ref.shape[1]
        halves = range(A_OUT // LANES)
        for r in range(rate):
            for j in halves:
                scr[j, pl.ds(r, tm // rate, stride=rate), :] = ref[0, r, :, j * LANES:(j + 1) * LANES]
        return jnp.concatenate([scr[j] for j in halves], axis=1)

    x = x_ref[...]
    h = _rms(x, g_ref[...]).astype(_MXU_DTYPE)
    oa0, l0 = oa0_ref[...], l0_ref[...]
    oa1, l1 = token_major(oa1_ref, scratch[0]), token_major(l1_ref, scratch[1])
    oa2, l2 = token_major(oa2_ref, scratch[2]), token_major(l2_ref, scratch[3])
    mx = jnp.maximum(jnp.maximum(l0, l1), l2)
    w0, w1, w2 = jnp.exp2(l0 - mx), jnp.exp2(l1 - mx), jnp.exp2(l2 - mx)
    o_a = (w0 * oa0 + w1 * oa1 + w2 * oa2) / (w0 + w1 + w2)
    branches = (_dot(o_a.astype(_MXU_DTYPE), cast(pa_ref)), _dot(ob_ref[...], cast(pb_ref)),
                _dot(oc_ref[...], cast(pc_ref)))
    parts = []
    for part in range(D_MODEL // MERGE_GATE_BLOCK):
        cols = slice(part * MERGE_GATE_BLOCK, (part + 1) * MERGE_GATE_BLOCK)
        merged = None
        for b in range(N_BRANCH):
            gate = jax.nn.sigmoid(_dot(h, cast(gate_refs[b * (D_MODEL // MERGE_GATE_BLOCK) + part])))
            term = gate * branches[b][:, cols]
            merged = term if merged is None else merged + term
        parts.append(merged.astype(_MXU_DTYPE))
    out_ref[...] = x + _dot(jnp.concatenate(parts, axis=1), cast(wo_ref))


def _merge(x2, g1, oa, lse, ob, oc, w_in, w_br_a, w_br_b, w_br_c, w_o, layer, B, S):
    M = x2.shape[0]
    tm = MERGE_TM
    per_seq = S // tm
    row = lambda i: (i, 0)
    tile = lambda w: pl.BlockSpec((tm, w), row)
    gate0 = _ZG // MERGE_GATE_BLOCK

    def layer_weight(w, block=None, at=0):
        block = w.shape[2] if block is None else block
        return pl.BlockSpec((None, w.shape[1], block), lambda i: (layer, 0, at), pipeline_mode=pl.Buffered(1))

    def dilated(rate):
        return pl.BlockSpec((1, rate, tm // rate, A_OUT), lambda i: (i // per_seq, 0, i % per_seq, 0))

    r1, r2 = A_PATTERNS[1][1], A_PATTERNS[2][1]
    return pl.pallas_call(
        _merge_kernel,
        grid=(M // tm,),
        in_specs=[tile(D_MODEL), _const_spec((1, D_MODEL)),
                  tile(A_OUT), dilated(r1), dilated(r2), tile(A_OUT), dilated(r1), dilated(r2),
                  tile(B_W), tile(C_QW),
                  *[layer_weight(w_in, MERGE_GATE_BLOCK, gate0 + j)
                    for j in range(N_BRANCH * D_MODEL // MERGE_GATE_BLOCK)],
                  layer_weight(w_br_a), layer_weight(w_br_b), layer_weight(w_br_c), layer_weight(w_o)],
        out_specs=tile(D_MODEL),
        out_shape=jax.ShapeDtypeStruct((M, D_MODEL), jnp.float32),
        scratch_shapes=[pltpu.VMEM((A_OUT // LANES, tm, LANES), jnp.float32)] * 4,
        compiler_params=_params(("arbitrary",)),
        name="gated_merge",
    )(x2, g1, oa[0].reshape(M, A_OUT), oa[1], oa[2], lse[0].reshape(M, A_OUT), lse[1], lse[2],
      ob, oc, *([w_in] * (N_BRANCH * D_MODEL // MERGE_GATE_BLOCK)), w_br_a, w_br_b, w_br_c, w_o)


FFN_TM = 512
FFN_CHUNK = 256
FFN_DOWN_CHUNK = D_FF // 2


def _ffn_kernel(x_ref, g_ref, wup_ref, wdown_ref, out_ref, act_ref):
    x = x_ref[...]
    xn = _rms(x, g_ref[...]).astype(_MXU_DTYPE)
    for c in range(D_FF // FFN_CHUNK):
        cs = slice(c * FFN_CHUNK, (c + 1) * FFN_CHUNK)
        a = _dot(xn, wup_ref[:, cs].astype(_MXU_DTYPE))
        b = _dot(xn, wup_ref[:, D_FF + c * FFN_CHUNK:D_FF + (c + 1) * FFN_CHUNK].astype(_MXU_DTYPE))
        act_ref[:, cs] = (a * jax.nn.sigmoid(a) * b).astype(act_ref.dtype)
    y = x
    for c in range(D_FF // FFN_DOWN_CHUNK):
        rows = slice(c * FFN_DOWN_CHUNK, (c + 1) * FFN_DOWN_CHUNK)
        y = y + _dot(act_ref[:, rows], wdown_ref[rows, :].astype(_MXU_DTYPE))
    out_ref[...] = y


def _ffn(x2, g2, w_up, w_down, layer):
    M = x2.shape[0]
    tm = FFN_TM
    row = lambda i: (i, 0)
    layer_weight = lambda w: pl.BlockSpec((None,) + w.shape[1:], lambda i: (layer, 0, 0),
                                          pipeline_mode=pl.Buffered(1))
    return pl.pallas_call(
        _ffn_kernel,
        grid=(M // tm,),
        in_specs=[pl.BlockSpec((tm, D_MODEL), row), _const_spec((1, D_MODEL)),
                  layer_weight(w_up), layer_weight(w_down)],
        out_specs=pl.BlockSpec((tm, D_MODEL), row),
        out_shape=jax.ShapeDtypeStruct((M, D_MODEL), jnp.float32),
        scratch_shapes=[pltpu.VMEM((tm, D_FF), _MXU_DTYPE)],
        compiler_params=_params(("arbitrary",)),
        name="swiglu_ffn",
    )(x2, g2, w_up, w_down)


def _t5_bucket(rel):
    half = T5_BUCKETS // 2
    max_exact = half // 2
    ret = jnp.where(rel > 0, half, 0)
    n = jnp.abs(rel)
    nf = jnp.maximum(n, 1).astype(jnp.float32)
    large = max_exact + (jnp.log(nf / max_exact) / math.log(T5_MAX_DIST / max_exact)
                         * (half - max_exact)).astype(jnp.int32)
    large = jnp.minimum(large, half - 1)
    return ret + jnp.where(n < max_exact, n, large)


def _mixer_a_bias(table_g, rate):
    j = jnp.arange(A_SPAN)[:, None]
    i = jnp.arange(A_UNIT)[None, :]
    step = j - A_RADIUS - i
    onehot = (_t5_bucket(step * rate)[:, :, None] == jnp.arange(T5_BUCKETS)).astype(jnp.float32)
    bias = jnp.einsum("jib,bh->hji", onehot, table_g.astype(jnp.float32) * LOG2_E,
                      precision=lax.Precision.HIGHEST)
    return jnp.where((jnp.abs(step) <= A_RADIUS)[None], bias, NEG)


B_N_DR = 2 * B_WIN_ROWS - 1


def _mixer_b_blocks(rpb, shift):
    c = np.arange(GRID_W)
    c0 = np.clip(c - B_WIN_COLS // 2, 0, GRID_W - B_WIN_COLS)
    col_ok = (c[:, None] >= c0[None, :]) & (c[:, None] < c0[None, :] + B_WIN_COLS)
    dc = np.clip(c[:, None] - c[None, :] + B_WIN_COLS - 1, 0, 2 * B_WIN_COLS - 2)
    pick_c = ((dc[..., None] == np.arange(2 * B_WIN_COLS - 1)) & col_ok[..., None]).astype(np.float32)
    blocks = jnp.einsum("hdk,xyk->hdxy", rpb.astype(jnp.float32) * LOG2_E, pick_c,
                        precision=lax.Precision.HIGHEST)
    blocks = jnp.where(col_ok[None, None], blocks - shift, NEG)
    blocks = jnp.concatenate([blocks, jnp.full((B_HEADS, 1, GRID_W, GRID_W), NEG, jnp.float32)], axis=1)
    return jnp.concatenate([blocks, blocks], axis=-1)


def _mixer_b_block_index(rows):
    first_query_row = np.array([0, B_WIN_ROWS // 2, rows - B_UNIT_ROWS])
    i = first_query_row[:, None, None] + np.arange(B_UNIT_ROWS)[None, None, :]
    r0 = np.clip(first_query_row - B_WIN_ROWS // 2, 0, rows - B_SPAN_ROWS)
    ik = r0[:, None, None] + np.arange(B_SPAN_ROWS)[None, :, None]
    rs = np.clip(i - B_WIN_ROWS // 2, 0, rows - B_WIN_ROWS)
    row_ok = (ik >= rs) & (ik < rs + B_WIN_ROWS)
    return np.where(row_ok, ik - i + B_WIN_ROWS - 1, B_N_DR).astype(np.int32).reshape(-1)


def _rope_tables(S):
    rows = S // GRID_W
    inv = ROPE_THETA ** (-jnp.arange(0, ROPE_AXIS_DIM, 2, dtype=jnp.float32) / ROPE_AXIS_DIM)
    d = np.arange(LANES) % HEAD_DIM
    inv_lane = inv[d % (ROPE_AXIS_DIM // 2)][None, :]
    is_col = (d >= ROPE_AXIS_DIM)[None, None, :]
    first = ((d % ROPE_AXIS_DIM) < ROPE_AXIS_DIM // 2)[None, :]
    tables = []
    for n in (rows, GRID_W):
        ang = jnp.arange(n, dtype=jnp.float32)[:, None] * inv_lane
        sin = jnp.sin(ang)
        tables.append((jnp.cos(ang), jnp.where(first, -sin, 0.0), jnp.where(first, 0.0, sin)))
    return tuple(jnp.where(is_col, by_col[None, :, :], by_row[:, None, :]).reshape(S, LANES)
                 for by_row, by_col in zip(*tables))


def _softmax_shift(gain_q, gain_k, bias_abs_max):
    bound = (HEAD_DIM * QK_SCALE * LOG2_E * ROUNDING_SLACK * jnp.max(jnp.abs(gain_q)) * jnp.max(jnp.abs(gain_k))
             + LOG2_E * bias_abs_max)
    small = bound <= SHIFT_LIMIT
    return jnp.stack([jnp.where(small, bound, 0.0), small.astype(jnp.float32)]).astype(jnp.float32)


def _block_diag_ones():
    i = np.arange(256) // HEAD_DIM
    return jnp.asarray(i[:, None] == i[None, :], _MXU_DTYPE)


def kernel(x, rel_bias_table, norm1, w_in, qk_gain, nat_rpb, w_br_a, w_br_b, w_br_c, w_o,
           norm2, w_up, w_down):
    B, S, D = x.shape
    depth = w_in.shape[0]
    M = B * S
    x2 = x.reshape(M, D)
    rope = _rope_tables(S)
    ones_bd = _block_diag_ones()
    a_bias = [_mixer_a_bias(rel_bias_table[:, g * A_HEADS:(g + 1) * A_HEADS], rate)
              for g, (_, rate) in enumerate(A_PATTERNS)]
    for l in range(depth):
        gq = qk_gain[l]
        tile = lambda g, n, s: jnp.tile(g * s, n)
        gain_row = jnp.concatenate([
            tile(gq[0], A_W // HEAD_DIM, QK_SCALE * LOG2_E), tile(gq[1], A_W // HEAD_DIM, 1.0),
            tile(gq[2], B_HEADS, QK_SCALE * LOG2_E), tile(gq[3], B_HEADS, 1.0),
            tile(gq[4], C_Q_HEADS, QK_SCALE * LOG2_E), tile(gq[5], C_KV_HEADS, 1.0)])[None, :]
        g1 = norm1[l][None, :]
        za0, za1, za2, qbt, kb, vbt, qc, kc, vc = _qkv_projection(
            x2, g1, w_in, l, gain_row, ones_bd, rope, B, S)
        za =(za0.reshape(B, 1, S, ZA_W), za1, za2)
        oa, lse = [], []
        for g, (_, rate) in enumerate(A_PATTERNS):
            table_g = rel_bias_table[:, g * A_HEADS:(g + 1) * A_HEADS]
            shift_a = _softmax_shift(gq[0], gq[1], jnp.max(jnp.abs(table_g)))
            o_g, l_g = _mixer_a_group(za[g], a_bias[g] - shift_a[0], shift_a, rate, B, S)
            oa.append(o_g)
            lse.append(l_g)
        shift_b = _softmax_shift(gq[2], gq[3], jnp.max(jnp.abs(nat_rpb[l])))
        ob = _mixer_b(qbt, kb, vbt, _mixer_b_blocks(nat_rpb[l], shift_b[0]), shift_b, B, S)
        oc = _mixer_c(qc, kc, vc, _softmax_shift(gq[4], gq[5], 0.0), B, S)
        x2 = _merge(x2, g1, oa, lse, ob, oc, w_in, w_br_a, w_br_b, w_br_c, w_o, l, B, S)
        x2 = _ffn(x2, norm2[l][None, :], w_up, w_down, l)
    return x2.reshape(B, S, D)
```

```python
import functools
import math

import jax
import jax.numpy as jnp
import numpy as np
from jax import lax
from jax.experimental import pallas as pl
from jax.experimental.pallas import tpu as pltpu

_MXU_DTYPE = jnp.bfloat16

D_MODEL = 1024
HEAD_DIM = 64
GRID_W = 64
RMS_EPS = 1e-6
NEG = -1e30
A_PATTERNS = ((128, 1), (512, 4), (2048, 16))
A_GROUPS = 3
A_HEADS = 4
A_W = A_GROUPS * A_HEADS * HEAD_DIM
A_OUT = A_HEADS * HEAD_DIM
A_RADIUS = 64
B_HEADS = 8
B_W = B_HEADS * HEAD_DIM
B_WIN_ROWS = 8
B_WIN_COLS = 16
C_Q_HEADS = 8
C_KV_HEADS = 2
C_GROUP = C_Q_HEADS // C_KV_HEADS
C_QW = C_Q_HEADS * HEAD_DIM
C_KVW = C_KV_HEADS * HEAD_DIM
ROPE_THETA = 10000.0
ROPE_AXIS_DIM = HEAD_DIM // 2
T5_BUCKETS = 32
T5_MAX_DIST = 1024
N_BRANCH = 3
D_FF = math.ceil(8 * D_MODEL / 3 / 256) * 256
QK_SCALE = HEAD_DIM ** -0.5
LOG2_E = math.log2(math.e)

VMEM_LIMIT_BYTES = 56 * 1024 * 1024
LANES = 128
BF16_SUBLANES = 16
VT_ROWS = HEAD_DIM + BF16_SUBLANES
SHIFT_LIMIT = 60.0
ROUNDING_SLACK = 1.02

_OFF = np.cumsum([0, A_W, A_W, A_W, B_W, B_W, B_W, C_QW, C_KVW, C_KVW]).tolist()
(_QA, _KA, _VA, _QB, _KB, _VB, _QC, _KC, _VC, _ZG) = _OFF

ZA_W = 3 * A_OUT
B_PAIRS = B_HEADS // 2


def _params(sem):
    return pltpu.CompilerParams(dimension_semantics=sem, vmem_limit_bytes=VMEM_LIMIT_BYTES)


def _const_spec(shape):
    nd = len(shape)
    return pl.BlockSpec(shape, lambda *_: (0,) * nd, pipeline_mode=pl.Buffered(1))


def _rms(x, g):
    return x * lax.rsqrt(jnp.mean(x * x, axis=-1, keepdims=True) + RMS_EPS) * g


def _dot(a, b):
    return jnp.dot(a, b, preferred_element_type=jnp.float32)


def _dot_nt(a, b):
    return lax.dot_general(a, b, (((1,), (1,)), ((), ())), preferred_element_type=jnp.float32)


def _ones_tail(width, dtype):
    return (lax.broadcasted_iota(jnp.int32, (VT_ROWS - HEAD_DIM, width), 0) == 0).astype(dtype)


PROJ_TM = 512
PROJ_W_BLOCK = 1536
_N_NORM = 2 * A_W + 2 * B_W
_N_ROPE = C_QW + C_KVW


def _proj_kernel(x_ref, g_ref, w0_ref, w1_ref, w2_ref, gain_ref, ones_ref, cos_ref, s1_ref, s2_ref,
                 za0_ref, za1_ref, za2_ref, qbt_ref, kb_ref, vbt_ref, qc_ref, kc_ref, vc_ref, dil_ref):
    tm = x_ref.shape[0]
    w_refs = (w0_ref, w1_ref, w2_ref)
    h = _rms(x_ref[...], g_ref[...]).astype(_MXU_DTYPE)
    za_refs = (za0_ref, za1_ref, za2_ref)

    def head_norm(acc, c0, width):
        sq = (acc * acc).astype(_MXU_DTYPE)
        ms = _dot(sq, ones_ref[:width, :width]) * (1.0 / HEAD_DIM)
        return acc * lax.rsqrt(ms + RMS_EPS) * gain_ref[:, c0:c0 + width]

    def store_group(g, section, val):
        cols = slice(section * A_OUT, (section + 1) * A_OUT)
        rate = A_PATTERNS[g][1]
        if rate == 1:
            za0_ref[:, cols] = val.astype(za0_ref.dtype)
            return
        for j in range(A_OUT // LANES):
            dil_ref[j] = val[:, j * LANES:(j + 1) * LANES]
        for r in range(rate):
            picked = [dil_ref[j, pl.ds(r, tm // rate, stride=rate), :] for j in range(A_OUT // LANES)]
            za_refs[g][0, r, :, cols] = jnp.concatenate(picked, axis=1).astype(za_refs[g].dtype)

    def store_pairs(ref, t, val):
        vt = val.T.astype(ref.dtype)
        ref[0, 2 * t] = vt[:LANES]
        ref[0, 2 * t + 1] = vt[LANES:]

    def rotary(y, width):
        reps = width // LANES
        table = lambda ref: jnp.concatenate([ref[...]] * reps, axis=1) if reps > 1 else ref[...]
        return (y * table(cos_ref) + pltpu.roll(y, width - 16, 1) * table(s1_ref)
                + pltpu.roll(y, 16, 1) * table(s2_ref))

    def store_qc(t, acc, gain_at):
        yt = rotary(head_norm(acc, gain_at, 256), 256).T.astype(qc_ref.dtype)
        for j in range(256 // HEAD_DIM):
            qc_ref[0, 4 * t + j] = yt[j * HEAD_DIM:(j + 1) * HEAD_DIM]

    def store_kc_vc(acc, gain_at):
        y = rotary(head_norm(acc[:, :LANES], gain_at, LANES), LANES)
        acc_t = acc[:, LANES:].T.astype(vc_ref.dtype)
        tail = _ones_tail(tm, vc_ref.dtype)
        lane = lax.broadcasted_iota(jnp.int32, (tm, LANES), 1)
        one_col = (lane == HEAD_DIM).astype(jnp.float32)
        for kv in range(C_KV_HEADS):
            head = y if kv == 0 else pltpu.roll(y, LANES - kv * HEAD_DIM, 1)
            kc_ref[0, kv] = jnp.where(lane < HEAD_DIM, head, one_col).astype(kc_ref.dtype)
            vc_ref[0, kv, :HEAD_DIM] = acc_t[kv * HEAD_DIM:(kv + 1) * HEAD_DIM]
            vc_ref[0, kv, HEAD_DIM:] = tail

    jobs = []
    for section, src in enumerate((_QA, _KA)):
        for g in range(A_GROUPS):
            gain_at = section * A_W + g * A_OUT
            jobs.append((src + g * A_OUT, A_OUT, lambda acc, gain_at=gain_at, g=g, section=section:
                         store_group(g, section, head_norm(acc, gain_at, A_OUT))))
    for t in range(B_W // 256):
        gain_at = 2 * A_W + t * 256
        jobs.append((_QB + t * 256, 256, lambda acc, gain_at=gain_at, t=t:
                     store_pairs(qbt_ref, t, head_norm(acc, gain_at, 256))))
    for t in range(B_W // 256):
        gain_at = 2 * A_W + B_W + t * 256
        def store_kb(acc, gain_at=gain_at, t=t):
            kb_ref[:, t * 256:(t + 1) * 256] = head_norm(acc, gain_at, 256).astype(kb_ref.dtype)
        jobs.append((_KB + t * 256, 256, store_kb))
    for t in range(C_QW // 256):
        jobs.append((_QC + t * 256, 256, lambda acc, t=t: store_qc(t, acc, _N_NORM + t * 256)))
    jobs.append((_KC, 2 * C_KVW, lambda acc: store_kc_vc(acc, _N_NORM + C_QW)))
    for g in range(A_GROUPS):
        jobs.append((_VA + g * A_OUT, A_OUT, lambda acc, g=g: store_group(g, 2, acc)))
    for t in range(B_W // 256):
        jobs.append((_VB + t * 256, 256, lambda acc, t=t: store_pairs(vbt_ref, t, acc)))

    def product(n):
        src, width, _ = jobs[n]
        blk, col = divmod(src, PROJ_W_BLOCK)
        return _dot(h, w_refs[blk][:, col:col + width].astype(_MXU_DTYPE))

    acc_next = product(0)
    for n, (_, _, consume) in enumerate(jobs):
        acc = acc_next
        if n + 1 < len(jobs):
            acc_next = product(n + 1)
        consume(acc)


def _qkv_projection(x2, g1, w_in, layer, gain_row, ones_bd, rope, B, S):
    M = x2.shape[0]
    tm = PROJ_TM
    per_seq = S // tm
    w_block = lambda j: pl.BlockSpec((None, D_MODEL, PROJ_W_BLOCK), lambda i: (layer, 0, j),
                                     pipeline_mode=pl.Buffered(1))
    cos_t, s1_t, s2_t = rope
    r1, r2 = A_PATTERNS[1][1], A_PATTERNS[2][1]
    row = lambda i: (i, 0)
    pos = lambda i: (i % per_seq, 0)
    hm = lambda i: (i // per_seq, 0, i % per_seq, 0)
    hm_t = lambda i: (i // per_seq, 0, 0, i % per_seq)
    cd = _MXU_DTYPE
    return pl.pallas_call(
        _proj_kernel,
        grid=(M // tm,),
        in_specs=[
            pl.BlockSpec((tm, D_MODEL), row),
            _const_spec((1, D_MODEL)),
            w_block(0), w_block(1), w_block(2),
            _const_spec((1, _N_NORM + _N_ROPE)),
            _const_spec((256, 256)),
            pl.BlockSpec((tm, LANES), pos),
            pl.BlockSpec((tm, LANES), pos),
            pl.BlockSpec((tm, LANES), pos),
        ],
        out_specs=[
            pl.BlockSpec((tm, ZA_W), row),
            pl.BlockSpec((1, r1, tm // r1, ZA_W), hm),
            pl.BlockSpec((1, r2, tm // r2, ZA_W), hm),
            pl.BlockSpec((1, B_PAIRS, LANES, tm), hm_t),
            pl.BlockSpec((tm, B_W), row),
            pl.BlockSpec((1, B_PAIRS, LANES, tm), hm_t),
            pl.BlockSpec((1, C_Q_HEADS, HEAD_DIM, tm), hm_t),
            pl.BlockSpec((1, C_KV_HEADS, tm, LANES), hm),
            pl.BlockSpec((1, C_KV_HEADS, VT_ROWS, tm), hm_t),
        ],
        out_shape=[
            jax.ShapeDtypeStruct((M, ZA_W), cd),
            jax.ShapeDtypeStruct((B, r1, S // r1, ZA_W), cd),
            jax.ShapeDtypeStruct((B, r2, S // r2, ZA_W), cd),
            jax.ShapeDtypeStruct((B, B_PAIRS, LANES, S), cd),
            jax.ShapeDtypeStruct((M, B_W), cd),
            jax.ShapeDtypeStruct((B, B_PAIRS, LANES, S), cd),
            jax.ShapeDtypeStruct((B, C_Q_HEADS, HEAD_DIM, S), cd),
            jax.ShapeDtypeStruct((B, C_KV_HEADS, S, LANES), cd),
            jax.ShapeDtypeStruct((B, C_KV_HEADS, VT_ROWS, S), cd),
        ],
        scratch_shapes=[pltpu.VMEM((A_OUT // LANES, tm, LANES), jnp.float32)],
        compiler_params=_params(("arbitrary",)),
        name="qkv_projection",
    )(x2, g1, w_in, w_in, w_in, gain_row, ones_bd, cos_t, s1_t, s2_t)


A_UNIT = 4 * A_RADIUS
A_SPAN = A_UNIT + 2 * A_RADIUS


def _mixer_a_kernel(q_ref, kp_ref, ko_ref, kn_ref, vp_ref, vo_ref, vn_ref, bias_ref, shift_ref,
                    o_ref, lse_ref, kw_ref, vw_ref, s_ref, *, tl, seq_len):
    l0 = pl.program_id(2) * tl
    kw_ref[0:A_RADIUS] = kp_ref[0, 0]
    kw_ref[A_RADIUS:A_RADIUS + tl] = ko_ref[0, 0]
    kw_ref[A_RADIUS + tl:] = kn_ref[0, 0]
    vw_ref[0:A_RADIUS] = vp_ref[0, 0]
    vw_ref[A_RADIUS:A_RADIUS + tl] = vo_ref[0, 0]
    vw_ref[A_RADIUS + tl:] = vn_ref[0, 0]
    even_q = lax.broadcasted_iota(jnp.int32, (A_UNIT, LANES), 1) < HEAD_DIM
    lane_k = lax.broadcasted_iota(jnp.int32, (A_SPAN, LANES), 1)

    n_units = tl // A_UNIT

    def scores(u, pair):
        off = pl.multiple_of(u * A_UNIT, A_UNIT)
        q = q_ref[0, 0, pl.ds(off, A_UNIT), pair * LANES:(pair + 1) * LANES]
        zero = jnp.zeros_like(q)
        q2 = jnp.concatenate([jnp.where(even_q, q, zero), jnp.where(even_q, zero, q)], axis=0)
        return _dot_nt(kw_ref[pl.ds(off, A_SPAN), pair * LANES:(pair + 1) * LANES], q2)

    s_ref[...] = scores(0, 0)

    def unit(fixed_shift, u, carry):
        off = pl.multiple_of(u * A_UNIT, A_UNIT)
        top_ok = l0 + off - A_RADIUS >= 0
        bot_ok = l0 + off + A_UNIT + A_RADIUS <= seq_len
        outs, lses = [], []
        s_next = s_ref[...]
        for pair in range(A_HEADS // 2):
            s_pair = s_next
            if pair + 1 < A_HEADS // 2:
                s_next = scores(u, pair + 1)
            else:
                s_ref[...] = scores(jnp.minimum(u + 1, n_units - 1), 0)
            v_slab = vw_ref[pl.ds(off, A_SPAN), pair * LANES:(pair + 1) * LANES]
            for odd in range(2):
                s = s_pair[:, odd * A_UNIT:(odd + 1) * A_UNIT] + bias_ref[2 * pair + odd]
                s = jnp.concatenate([jnp.where(top_ok, s[:A_RADIUS], NEG), s[A_RADIUS:A_SPAN - A_RADIUS],
                                     jnp.where(bot_ok, s[A_SPAN - A_RADIUS:], NEG)], axis=0)
                if fixed_shift:
                    m = shift_ref[0]
                    p = jnp.exp2(s).astype(_MXU_DTYPE)
                else:
                    m = jnp.max(s, axis=0, keepdims=True)
                    p = jnp.exp2(s - m).astype(_MXU_DTYPE)
                den_row = (1 - odd) * HEAD_DIM
                mine = (lane_k >= odd * HEAD_DIM) & (lane_k < (odd + 1) * HEAD_DIM)
                v_aug = jnp.where(mine, v_slab, (lane_k == den_row).astype(v_slab.dtype))
                acc = lax.dot_general(v_aug, p, (((0,), (0,)), ((), ())),
                                      preferred_element_type=jnp.float32)
                den = acc[den_row:den_row + 1]
                outs.append(acc[odd * HEAD_DIM:(odd + 1) * HEAD_DIM] / den)
                lses.append(jnp.broadcast_to(m + jnp.log2(den), (HEAD_DIM, A_UNIT)))
        o_ref[0, 0, pl.ds(off, A_UNIT), :] = jnp.concatenate(outs, axis=0).T
        lse_ref[0, 0, pl.ds(off, A_UNIT), :] = jnp.concatenate(lses, axis=0).T
        return carry

    small = shift_ref[1] > 0.5

    @pl.when(small)
    def _():
        lax.fori_loop(0, n_units, functools.partial(unit, True), 0, unroll=2)

    @pl.when(jnp.logical_not(small))
    def _():
        lax.fori_loop(0, n_units, functools.partial(unit, False), 0)


def _mixer_a_group(za_g, bias, shift, rate, B, S):
    L = S // rate
    tl = min(1024, L)
    nblk = L // A_RADIUS
    per = tl // A_RADIUS

    def own(section):
        return lambda b, r, l: (b, r, l, section)

    def prev(section):
        return lambda b, r, l: (b, r, jnp.maximum(l * per - 1, 0), section)

    def nxt(section):
        return lambda b, r, l: (b, r, jnp.minimum((l + 1) * per, nblk - 1), section)

    edge = (1, 1, A_RADIUS, A_OUT)
    full = (1, 1, tl, A_OUT)
    return pl.pallas_call(
        functools.partial(_mixer_a_kernel, tl=tl, seq_len=L),
        grid=(B, rate, L // tl),
        in_specs=[
            pl.BlockSpec(full, own(0)),
            pl.BlockSpec(edge, prev(1)), pl.BlockSpec(full, own(1)), pl.BlockSpec(edge, nxt(1)),
            pl.BlockSpec(edge, prev(2)), pl.BlockSpec(full, own(2)), pl.BlockSpec(edge, nxt(2)),
            _const_spec((A_HEADS, A_SPAN, A_UNIT)),
            pl.BlockSpec(memory_space=pltpu.SMEM),
        ],
        out_specs=[pl.BlockSpec(full, own(0)), pl.BlockSpec(full, own(0))],
        out_shape=[jax.ShapeDtypeStruct((B, rate, L, A_OUT), jnp.float32)] * 2,
        scratch_shapes=[pltpu.VMEM((tl + 2 * A_RADIUS, A_OUT), _MXU_DTYPE)] * 2
        + [pltpu.VMEM((A_SPAN, 2 * A_UNIT), jnp.float32)],
        compiler_params=_params(("arbitrary",) * 3),
        name=f"mixer_a_rate{rate}",
    )(za_g, za_g, za_g, za_g, za_g, za_g, za_g, bias, shift)


B_UNIT_ROWS = 4
B_UNIT = B_UNIT_ROWS * GRID_W
B_SPAN_ROWS = B_UNIT_ROWS + B_WIN_ROWS
B_SPAN = B_SPAN_ROWS * GRID_W
B_HALO = (B_WIN_ROWS // 2) * GRID_W
B_TILE_ROWS = 16
B_TILE = B_TILE_ROWS * GRID_W


def _mixer_b_kernel(q_ref, kp_ref, ko_ref, kn_ref, vp_ref, vo_ref, vn_ref, blocks_ref, sel_ref, shift_ref,
                    o_ref, kw_ref, vw_ref, bias_ref, s_ref, *, rows):
    @pl.when((pl.program_id(0) == 0) & (pl.program_id(1) == 0))
    def _():
        left = lax.broadcasted_iota(jnp.int32, (GRID_W, LANES), 1) < GRID_W

        def fill(n, carry):
            v, h = n // B_HEADS, n % B_HEADS
            for a in range(B_SPAN_ROWS):
                for gp in range(B_UNIT_ROWS // 2):
                    at = (v * B_SPAN_ROWS + a) * B_UNIT_ROWS + 2 * gp
                    tile = jnp.where(left, blocks_ref[h, sel_ref[at]], blocks_ref[h, sel_ref[at + 1]])
                    bias_ref[v, h, a * GRID_W:(a + 1) * GRID_W, gp * LANES:(gp + 1) * LANES] = tile
            return carry

        lax.fori_loop(0, 3 * B_HEADS, fill, 0)

    i0 = pl.program_id(1) * B_TILE_ROWS
    kw_ref[0:B_HALO] = kp_ref[0]
    kw_ref[B_HALO:B_HALO + B_TILE] = ko_ref[0]
    kw_ref[B_HALO + B_TILE:] = kn_ref[0]
    vw_ref[:, :, 0:B_HALO] = vp_ref[0]
    vw_ref[:, :, B_HALO:B_HALO + B_TILE] = vo_ref[0]
    vw_ref[:, :, B_HALO + B_TILE:] = vn_ref[0]
    tail = _ones_tail(B_SPAN, _MXU_DTYPE)
    upper = lax.broadcasted_iota(jnp.int32, (LANES, B_UNIT), 0) < HEAD_DIM

    n_units = B_TILE_ROWS // B_UNIT_ROWS

    def span_offset(u):
        r0 = jnp.clip(i0 + u * B_UNIT_ROWS - B_WIN_ROWS // 2, 0, rows - B_SPAN_ROWS)
        return pl.multiple_of((r0 - (i0 - B_WIN_ROWS // 2)) * GRID_W, LANES)

    def scores(u, pair):
        qt = q_ref[0, pair, :, pl.ds(pl.multiple_of(u * B_UNIT, B_UNIT), B_UNIT)]
        zero = jnp.zeros_like(qt)
        qt2 = jnp.concatenate([jnp.where(upper, qt, zero), jnp.where(upper, zero, qt)], axis=1)
        return _dot(kw_ref[pl.ds(span_offset(u), B_SPAN), pair * LANES:(pair + 1) * LANES], qt2)

    s_ref[...] = scores(0, 0)

    def unit(fixed_shift, u, carry):
        i0u = i0 + u * B_UNIT_ROWS
        off = span_offset(u)
        variant = jnp.where(i0u == 0, 0, jnp.where(i0u == rows - B_UNIT_ROWS, 2, 1))
        qoff = pl.multiple_of(u * B_UNIT, B_UNIT)
        outs = []
        s_next = s_ref[...]
        for pair in range(B_PAIRS):
            s_pair = s_next
            if pair + 1 < B_PAIRS:
                s_next = scores(u, pair + 1)
            else:
                s_ref[...] = scores(jnp.minimum(u + 1, n_units - 1), 0)
            for odd in range(2):
                s = s_pair[:, odd * B_UNIT:(odd + 1) * B_UNIT] + bias_ref[variant, 2 * pair + odd]
                if not fixed_shift:
                    s = s - jnp.max(s, axis=0, keepdims=True)
                p = jnp.exp2(s).astype(_MXU_DTYPE)
                vt = vw_ref[pair, odd * HEAD_DIM:(odd + 1) * HEAD_DIM, pl.ds(off, B_SPAN)]
                acc = _dot(jnp.concatenate([vt, tail], axis=0), p)
                outs.append(acc[:HEAD_DIM] / acc[HEAD_DIM:HEAD_DIM + 1])
        o_ref[0, pl.ds(qoff, B_UNIT), :] = jnp.concatenate(outs, axis=0).T.astype(o_ref.dtype)
        return carry

    small = shift_ref[1] > 0.5

    @pl.when(small)
    def _():
        lax.fori_loop(0, n_units, functools.partial(unit, True), 0, unroll=2)

    @pl.when(jnp.logical_not(small))
    def _():
        lax.fori_loop(0, n_units, functools.partial(unit, False), 0)


def _mixer_b(qbt, kb, vbt, blocks, shift, B, S):
    rows = S // GRID_W
    sel = jnp.asarray(_mixer_b_block_index(rows))
    nt = rows // B_TILE_ROWS
    per = B_TILE // B_HALO
    nh = S // B_HALO
    kb3 = kb.reshape(B, S, B_W)
    prev = lambda t: jnp.maximum(t * per - 1, 0)
    nxt = lambda t: jnp.minimum((t + 1) * per, nh - 1)
    k_edge, k_own = (1, B_HALO, B_W), (1, B_TILE, B_W)
    t_edge, t_own = (1, B_PAIRS, LANES, B_HALO), (1, B_PAIRS, LANES, B_TILE)
    o = pl.pallas_call(
        functools.partial(_mixer_b_kernel, rows=rows),
        grid=(B, nt),
        in_specs=[
            pl.BlockSpec(t_own, lambda b, t: (b, 0, 0, t)),
            pl.BlockSpec(k_edge, lambda b, t: (b, prev(t), 0)),
            pl.BlockSpec(k_own, lambda b, t: (b, t, 0)),
            pl.BlockSpec(k_edge, lambda b, t: (b, nxt(t), 0)),
            pl.BlockSpec(t_edge, lambda b, t: (b, 0, 0, prev(t))),
            pl.BlockSpec(t_own, lambda b, t: (b, 0, 0, t)),
            pl.BlockSpec(t_edge, lambda b, t: (b, 0, 0, nxt(t))),
            _const_spec(blocks.shape),
            pl.BlockSpec(memory_space=pltpu.SMEM),
            pl.BlockSpec(memory_space=pltpu.SMEM),
        ],
        out_specs=pl.BlockSpec(k_own, lambda b, t: (b, t, 0)),
        out_shape=jax.ShapeDtypeStruct((B, S, B_W), _MXU_DTYPE),
        scratch_shapes=[pltpu.VMEM((B_TILE + 2 * B_HALO, B_W), _MXU_DTYPE),
                        pltpu.VMEM((B_PAIRS, LANES, B_TILE + 2 * B_HALO), _MXU_DTYPE),
                        pltpu.VMEM((3, B_HEADS, B_SPAN, B_UNIT), jnp.float32),
                        pltpu.VMEM((B_SPAN, 2 * B_UNIT), jnp.float32)],
        compiler_params=_params(("arbitrary",) * 2),
        name="mixer_b",
    )(qbt, kb3, kb3, kb3, vbt, vbt, vbt, blocks, sel, shift)
    return o.reshape(B * S, B_W)


C_TQ = 1024
C_TK = 512
C_UNIT = 512
C_UNROLL = 4


def _mixer_c_kernel(q_ref, k_ref, vt_ref, shift_ref, o_ref, qcat_ref, m_ref, acc_ref, s_ref, l_ref, *, n_kv):
    acc_ref[...] = jnp.zeros(acc_ref.shape, jnp.float32)
    for g in range(C_GROUP):
        qcat_ref[:HEAD_DIM, g * C_TQ:(g + 1) * C_TQ] = q_ref[0, g]
    below = lax.broadcasted_iota(jnp.int32, (LANES - HEAD_DIM, C_GROUP * C_TQ), 0) == 0
    qcat_ref[HEAD_DIM:] = jnp.where(below, -shift_ref[0], 0.0).astype(qcat_ref.dtype)
    n_units = C_GROUP * C_TQ // C_UNIT

    def keys(j):
        return k_ref[0, 0, pl.ds(pl.multiple_of(j * C_TK, C_TK), C_TK), :]

    def scores(k, u):
        return _dot(k, qcat_ref[:, u * C_UNIT:(u + 1) * C_UNIT])

    def sweep(update, unroll):
        s_ref[...] = scores(keys(0), 0)

        def step(j, carry):
            k = keys(j)
            vt = vt_ref[0, 0, :, pl.ds(pl.multiple_of(j * C_TK, C_TK), C_TK)]
            s_next = s_ref[...]
            for u in range(n_units):
                s = s_next
                if u + 1 < n_units:
                    s_next = scores(k, u + 1)
                else:
                    s_ref[...] = scores(keys(jnp.minimum(j + 1, n_kv - 1)), 0)
                update(s, vt, slice(u * C_UNIT, (u + 1) * C_UNIT))
            return carry

        lax.fori_loop(0, n_kv, step, 0, unroll=unroll)

    def fixed_shift(s, vt, cols):
        p = jnp.exp2(s)
        l_ref[:, cols] += jnp.sum(p.reshape(C_TK // 8, 8, C_UNIT), axis=0)
        acc_ref[:HEAD_DIM, cols] += _dot(vt[:HEAD_DIM], p.astype(_MXU_DTYPE))

    def running_max(s, vt, cols):
        m_prev = m_ref[:, cols]
        m_new = jnp.maximum(m_prev, jnp.max(s, axis=0, keepdims=True))
        alpha = jnp.exp2(m_prev - m_new)
        p = jnp.exp2(s - m_new).astype(_MXU_DTYPE)
        acc_ref[:, cols] = alpha * acc_ref[:, cols] + _dot(vt, p)
        m_ref[:, cols] = m_new

    small = shift_ref[1] > 0.5

    l_ref[...] = jnp.zeros(l_ref.shape, jnp.float32)

    @pl.when(small)
    def _():
        sweep(fixed_shift, C_UNROLL)

    @pl.when(jnp.logical_not(small))
    def _():
        m_ref[...] = jnp.full(m_ref.shape, -jnp.inf, jnp.float32)
        sweep(running_max, 1)

    den = jnp.where(small, jnp.sum(l_ref[...], axis=0, keepdims=True), acc_ref[HEAD_DIM:HEAD_DIM + 1])
    o_t = jnp.concatenate(
        [acc_ref[:HEAD_DIM, g * C_TQ:(g + 1) * C_TQ] / den[:, g * C_TQ:(g + 1) * C_TQ]
         for g in range(C_GROUP)], axis=0)
    o_ref[0] = o_t.T.astype(o_ref.dtype)


def _mixer_c(qc_t, kc, vc_t, shift, B, S):
    o = pl.pallas_call(
        functools.partial(_mixer_c_kernel, n_kv=S // C_TK),
        grid=(B, C_KV_HEADS, S // C_TQ),
        in_specs=[
            pl.BlockSpec((1, C_GROUP, HEAD_DIM, C_TQ), lambda b, kv, i: (b, kv, 0, i)),
            pl.BlockSpec((1, 1, S, LANES), lambda b, kv, i: (b, kv, 0, 0)),
            pl.BlockSpec((1, 1, VT_ROWS, S), lambda b, kv, i: (b, kv, 0, 0)),
            pl.BlockSpec(memory_space=pltpu.SMEM),
        ],
        out_specs=pl.BlockSpec((1, C_TQ, C_GROUP * HEAD_DIM), lambda b, kv, i: (b, i, kv)),
        out_shape=jax.ShapeDtypeStruct((B, S, C_QW), _MXU_DTYPE),
        scratch_shapes=[pltpu.VMEM((LANES, C_GROUP * C_TQ), _MXU_DTYPE),
                        pltpu.VMEM((1, C_GROUP * C_TQ), jnp.float32),
                        pltpu.VMEM((VT_ROWS, C_GROUP * C_TQ), jnp.float32),
                        pltpu.VMEM((C_TK, C_UNIT), jnp.float32),
                        pltpu.VMEM((8, C_GROUP * C_TQ), jnp.float32)],
        compiler_params=_params(("arbitrary",) * 3),
        name="mixer_c",
    )(qc_t, kc, vc_t, shift)
    return o.reshape(B * S, C_QW)


MERGE_TM = 512
MERGE_GATE_BLOCK = 512


def _merge_kernel(x_ref, g_ref, oa0_ref, oa1_ref, oa2_ref, l0_ref, l1_ref, l2_ref, ob_ref, oc_ref,
                  wg0_ref, wg1_ref, wg2_ref, wg3_ref, wg4_ref, wg5_ref, pa_ref, pb_ref, pc_ref, wo_ref,
                  out_ref, *scratch):
    tm = x_ref.shape[0]
    gate_refs = (wg0_ref, wg1_ref, wg2_ref, wg3_ref, wg4_ref, wg5_ref)
    cast = lambda ref: ref[...].astype(_MXU_DTYPE)

    def token_major(ref, scr):
        rate = ref.shape[1]
        halves = range(A_OUT // LANES)
        for r in range(rate):
            for j in halves:
                scr[j, pl.ds(r, tm // rate, stride=rate), :] = ref[0, r, :, j * LANES:(j + 1) * LANES]
        return jnp.concatenate([scr[j] for j in halves], axis=1)

    x = x_ref[...]
    h = _rms(x, g_ref[...]).astype(_MXU_DTYPE)
    n_parts = D_MODEL // MERGE_GATE_BLOCK
    branches = [None, _dot(ob_ref[...], cast(pb_ref)), _dot(oc_ref[...], cast(pc_ref))]
    gates = {(b, part): jax.nn.sigmoid(_dot(h, cast(gate_refs[b * n_parts + part])))
             for b in (1, 2, 0) for part in range(n_parts)}
    oa0, l0 = oa0_ref[...], l0_ref[...]
    oa1, l1 = token_major(oa1_ref, scratch[0]), token_major(l1_ref, scratch[1])
    oa2, l2 = token_major(oa2_ref, scratch[2]), token_major(l2_ref, scratch[3])
    mx = jnp.maximum(jnp.maximum(l0, l1), l2)
    w0, w1, w2 = jnp.exp2(l0 - mx), jnp.exp2(l1 - mx), jnp.exp2(l2 - mx)
    o_a = (w0 * oa0 + w1 * oa1 + w2 * oa2) / (w0 + w1 + w2)
    branches[0] = _dot(o_a.astype(_MXU_DTYPE), cast(pa_ref))
    parts = []
    for part in range(n_parts):
        cols = slice(part * MERGE_GATE_BLOCK, (part + 1) * MERGE_GATE_BLOCK)
        merged = sum(gates[b, part] * branches[b][:, cols] for b in range(1, N_BRANCH))
        parts.append((merged + gates[0, part] * branches[0][:, cols]).astype(_MXU_DTYPE))
    out_ref[...] = x + _dot(jnp.concatenate(parts, axis=1), cast(wo_ref))


def _merge(x2, g1, oa, lse, ob, oc, w_in, w_br_a, w_br_b, w_br_c, w_o, layer, B, S):
    M = x2.shape[0]
    tm = MERGE_TM
    per_seq = S // tm
    row = lambda i: (i, 0)
    tile = lambda w: pl.BlockSpec((tm, w), row)
    gate0 = _ZG // MERGE_GATE_BLOCK

    def layer_weight(w, block=None, at=0):
        block = w.shape[2] if block is None else block
        return pl.BlockSpec((None, w.shape[1], block), lambda i: (layer, 0, at), pipeline_mode=pl.Buffered(1))

    def dilated(rate):
        return pl.BlockSpec((1, rate, tm // rate, A_OUT), lambda i: (i // per_seq, 0, i % per_seq, 0))

    r1, r2 = A_PATTERNS[1][1], A_PATTERNS[2][1]
    return pl.pallas_call(
        _merge_kernel,
        grid=(M // tm,),
        in_specs=[tile(D_MODEL), _const_spec((1, D_MODEL)),
                  tile(A_OUT), dilated(r1), dilated(r2), tile(A_OUT), dilated(r1), dilated(r2),
                  tile(B_W), tile(C_QW),
                  *[layer_weight(w_in, MERGE_GATE_BLOCK, gate0 + j)
                    for j in range(N_BRANCH * D_MODEL // MERGE_GATE_BLOCK)],
                  layer_weight(w_br_a), layer_weight(w_br_b), layer_weight(w_br_c), layer_weight(w_o)],
        out_specs=tile(D_MODEL),
        out_shape=jax.ShapeDtypeStruct((M, D_MODEL), jnp.float32),
        scratch_shapes=[pltpu.VMEM((A_OUT // LANES, tm, LANES), jnp.float32)] * 4,
        compiler_params=_params(("arbitrary",)),
        name="gated_merge",
    )(x2, g1, oa[0].reshape(M, A_OUT), oa[1], oa[2], lse[0].reshape(M, A_OUT), lse[1], lse[2],
      ob, oc, *([w_in] * (N_BRANCH * D_MODEL // MERGE_GATE_BLOCK)), w_br_a, w_br_b, w_br_c, w_o)


FFN_TM = 512
FFN_CHUNK = 256
FFN_DOWN_CHUNK = D_FF // 2


def _ffn_kernel(x_ref, g_ref, wup_ref, wdown_ref, out_ref, act_ref):
    x = x_ref[...]
    xn = _rms(x, g_ref[...]).astype(_MXU_DTYPE)
    for c in range(D_FF // FFN_CHUNK):
        cs = slice(c * FFN_CHUNK, (c + 1) * FFN_CHUNK)
        a = _dot(xn, wup_ref[:, cs].astype(_MXU_DTYPE))
        b = _dot(xn, wup_ref[:, D_FF + c * FFN_CHUNK:D_FF + (c + 1) * FFN_CHUNK].astype(_MXU_DTYPE))
        act_ref[:, cs] = (a * jax.nn.sigmoid(a) * b).astype(act_ref.dtype)
    y = x
    for c in range(D_FF // FFN_DOWN_CHUNK):
        rows = slice(c * FFN_DOWN_CHUNK, (c + 1) * FFN_DOWN_CHUNK)
        y = y + _dot(act_ref[:, rows], wdown_ref[rows, :].astype(_MXU_DTYPE))
    out_ref[...] = y


def _ffn(x2, g2, w_up, w_down, layer):
    M = x2.shape[0]
    tm = FFN_TM
    row = lambda i: (i, 0)
    layer_weight = lambda w: pl.BlockSpec((None,) + w.shape[1:], lambda i: (layer, 0, 0),
                                          pipeline_mode=pl.Buffered(1))
    return pl.pallas_call(
        _ffn_kernel,
        grid=(M // tm,),
        in_specs=[pl.BlockSpec((tm, D_MODEL), row), _const_spec((1, D_MODEL)),
                  layer_weight(w_up), layer_weight(w_down)],
        out_specs=pl.BlockSpec((tm, D_MODEL), row),
        out_shape=jax.ShapeDtypeStruct((M, D_MODEL), jnp.float32),
        scratch_shapes=[pltpu.VMEM((tm, D_FF), _MXU_DTYPE)],
        compiler_params=_params(("arbitrary",)),
        name="swiglu_ffn",
    )(x2, g2, w_up, w_down)


def _t5_bucket(rel):
    half = T5_BUCKETS // 2
    max_exact = half // 2
    ret = jnp.where(rel > 0, half, 0)
    n = jnp.abs(rel)
    nf = jnp.maximum(n, 1).astype(jnp.float32)
    large = max_exact + (jnp.log(nf / max_exact) / math.log(T5_MAX_DIST / max_exact)
                         * (half - max_exact)).astype(jnp.int32)
    large = jnp.minimum(large, half - 1)
    return ret + jnp.where(n < max_exact, n, large)


def _mixer_a_bias(table_g, rate):
    j = jnp.arange(A_SPAN)[:, None]
    i = jnp.arange(A_UNIT)[None, :]
    step = j - A_RADIUS - i
    onehot = (_t5_bucket(step * rate)[:, :, None] == jnp.arange(T5_BUCKETS)).astype(jnp.float32)
    bias = jnp.einsum("jib,bh->hji", onehot, table_g.astype(jnp.float32) * LOG2_E,
                      precision=lax.Precision.HIGHEST)
    return jnp.where((jnp.abs(step) <= A_RADIUS)[None], bias, NEG)


B_N_DR = 2 * B_WIN_ROWS - 1


def _mixer_b_blocks(rpb, shift):
    c = np.arange(GRID_W)
    c0 = np.clip(c - B_WIN_COLS // 2, 0, GRID_W - B_WIN_COLS)
    col_ok = (c[:, None] >= c0[None, :]) & (c[:, None] < c0[None, :] + B_WIN_COLS)
    dc = np.clip(c[:, None] - c[None, :] + B_WIN_COLS - 1, 0, 2 * B_WIN_COLS - 2)
    pick_c = ((dc[..., None] == np.arange(2 * B_WIN_COLS - 1)) & col_ok[..., None]).astype(np.float32)
    blocks = jnp.einsum("hdk,xyk->hdxy", rpb.astype(jnp.float32) * LOG2_E, pick_c,
                        precision=lax.Precision.HIGHEST)
    blocks = jnp.where(col_ok[None, None], blocks - shift, NEG)
    blocks = jnp.concatenate([blocks, jnp.full((B_HEADS, 1, GRID_W, GRID_W), NEG, jnp.float32)], axis=1)
    return jnp.concatenate([blocks, blocks], axis=-1)


def _mixer_b_block_index(rows):
    first_query_row = np.array([0, B_WIN_ROWS // 2, rows - B_UNIT_ROWS])
    i = first_query_row[:, None, None] + np.arange(B_UNIT_ROWS)[None, None, :]
    r0 = np.clip(first_query_row - B_WIN_ROWS // 2, 0, rows - B_SPAN_ROWS)
    ik = r0[:, None, None] + np.arange(B_SPAN_ROWS)[None, :, None]
    rs = np.clip(i - B_WIN_ROWS // 2, 0, rows - B_WIN_ROWS)
    row_ok = (ik >= rs) & (ik < rs + B_WIN_ROWS)
    return np.where(row_ok, ik - i + B_WIN_ROWS - 1, B_N_DR).astype(np.int32).reshape(-1)


def _rope_tables(S):
    rows = S // GRID_W
    inv = ROPE_THETA ** (-jnp.arange(0, ROPE_AXIS_DIM, 2, dtype=jnp.float32) / ROPE_AXIS_DIM)
    d = np.arange(LANES) % HEAD_DIM
    inv_lane = inv[d % (ROPE_AXIS_DIM // 2)][None, :]
    is_col = (d >= ROPE_AXIS_DIM)[None, None, :]
    first = ((d % ROPE_AXIS_DIM) < ROPE_AXIS_DIM // 2)[None, :]
    tables = []
    for n in (rows, GRID_W):
        ang = jnp.arange(n, dtype=jnp.float32)[:, None] * inv_lane
        sin = jnp.sin(ang)
        tables.append((jnp.cos(ang), jnp.where(first, -sin, 0.0), jnp.where(first, 0.0, sin)))
    return tuple(jnp.where(is_col, by_col[None, :, :], by_row[:, None, :]).reshape(S, LANES)
                 for by_row, by_col in zip(*tables))


def _softmax_shift(gain_q, gain_k, bias_abs_max):
    bound = (HEAD_DIM * QK_SCALE * LOG2_E * ROUNDING_SLACK * jnp.max(jnp.abs(gain_q)) * jnp.max(jnp.abs(gain_k))
             + LOG2_E * bias_abs_max)
    small = bound <= SHIFT_LIMIT
    return jnp.stack([jnp.where(small, bound, 0.0), small.astype(jnp.float32)]).astype(jnp.float32)


def _block_diag_ones():
    i = np.arange(256) // HEAD_DIM
    return jnp.asarray(i[:, None] == i[None, :], _MXU_DTYPE)


def kernel(x, rel_bias_table, norm1, w_in, qk_gain, nat_rpb, w_br_a, w_br_b, w_br_c, w_o,
           norm2, w_up, w_down):
    B, S, D = x.shape
    depth = w_in.shape[0]
    M = B * S
    x2 = x.reshape(M, D)
    rope = _rope_tables(S)
    ones_bd = _block_diag_ones()
    a_bias = [_mixer_a_bias(rel_bias_table[:, g * A_HEADS:(g + 1) * A_HEADS], rate)
              for g, (_, rate) in enumerate(A_PATTERNS)]
    for l in range(depth):
        gq = qk_gain[l]
        tile = lambda g, n, s: jnp.tile(g * s, n)
        gain_row = jnp.concatenate([
            tile(gq[0], A_W // HEAD_DIM, QK_SCALE * LOG2_E), tile(gq[1], A_W // HEAD_DIM, 1.0),
            tile(gq[2], B_HEADS, QK_SCALE * LOG2_E), tile(gq[3], B_HEADS, 1.0),
            tile(gq[4], C_Q_HEADS, QK_SCALE * LOG2_E), tile(gq[5], C_KV_HEADS, 1.0)])[None, :]
        g1 = norm1[l][None, :]
        za0, za1, za2, qbt, kb, vbt, qc, kc, vc = _qkv_projection(
            x2, g1, w_in, l, gain_row, ones_bd, rope, B, S)
        za =(za0.reshape(B, 1, S, ZA_W), za1, za2)
        oa, lse = [], []
        for g, (_, rate) in enumerate(A_PATTERNS):
            table_g = rel_bias_table[:, g * A_HEADS:(g + 1) * A_HEADS]
            shift_a = _softmax_shift(gq[0], gq[1], jnp.max(jnp.abs(table_g)))
            o_g, l_g = _mixer_a_group(za[g], a_bias[g] - shift_a[0], shift_a, rate, B, S)
            oa.append(o_g)
            lse.append(l_g)
        shift_b = _softmax_shift(gq[2], gq[3], jnp.max(jnp.abs(nat_rpb[l])))
        ob = _mixer_b(qbt, kb, vbt, _mixer_b_blocks(nat_rpb[l], shift_b[0]), shift_b, B, S)
        oc = _mixer_c(qc, kc, vc, _softmax_shift(gq[4], gq[5], 0.0), B, S)
        x2 = _merge(x2, g1, oa, lse, ob, oc, w_in, w_br_a, w_br_b, w_br_c, w_o, l, B, S)
        x2 = _ffn(x2, norm2[l][None, :], w_up, w_down, l)
    return x2.reshape(B, S, D)
```

```python
import functools
import math

import jax
import jax.numpy as jnp
import numpy as np
from jax import lax
from jax.experimental import pallas as pl
from jax.experimental.pallas import tpu as pltpu

_MXU_DTYPE = jnp.bfloat16

D_MODEL = 1024
HEAD_DIM = 64
GRID_W = 64
RMS_EPS = 1e-6
NEG = -1e30
A_PATTERNS = ((128, 1), (512, 4), (2048, 16))
A_GROUPS = 3
A_HEADS = 4
A_W = A_GROUPS * A_HEADS * HEAD_DIM
A_OUT = A_HEADS * HEAD_DIM
A_RADIUS = 64
B_HEADS = 8
B_W = B_HEADS * HEAD_DIM
B_WIN_ROWS = 8
B_WIN_COLS = 16
C_Q_HEADS = 8
C_KV_HEADS = 2
C_GROUP = C_Q_HEADS // C_KV_HEADS
C_QW = C_Q_HEADS * HEAD_DIM
C_KVW = C_KV_HEADS * HEAD_DIM
ROPE_THETA = 10000.0
ROPE_AXIS_DIM = HEAD_DIM // 2
T5_BUCKETS = 32
T5_MAX_DIST = 1024
N_BRANCH = 3
D_FF = math.ceil(8 * D_MODEL / 3 / 256) * 256
QK_SCALE = HEAD_DIM ** -0.5
LOG2_E = math.log2(math.e)

VMEM_LIMIT_BYTES = 56 * 1024 * 1024
LANES = 128
BF16_SUBLANES = 16
VT_ROWS = HEAD_DIM + BF16_SUBLANES
SHIFT_LIMIT = 60.0
ROUNDING_SLACK = 1.02

_OFF = np.cumsum([0, A_W, A_W, A_W, B_W, B_W, B_W, C_QW, C_KVW, C_KVW]).tolist()
(_QA, _KA, _VA, _QB, _KB, _VB, _QC, _KC, _VC, _ZG) = _OFF

ZA_W = 3 * A_OUT
B_PAIRS = B_HEADS // 2


def _params(sem):
    return pltpu.CompilerParams(dimension_semantics=sem, vmem_limit_bytes=VMEM_LIMIT_BYTES)


def _const_spec(shape):
    nd = len(shape)
    return pl.BlockSpec(shape, lambda *_: (0,) * nd, pipeline_mode=pl.Buffered(1))


def _rms(x, g):
    return x * lax.rsqrt(jnp.mean(x * x, axis=-1, keepdims=True) + RMS_EPS) * g


def _dot(a, b):
    return jnp.dot(a, b, preferred_element_type=jnp.float32)


def _dot_nt(a, b):
    return lax.dot_general(a, b, (((1,), (1,)), ((), ())), preferred_element_type=jnp.float32)


def _ones_tail(width, dtype):
    return (lax.broadcasted_iota(jnp.int32, (VT_ROWS - HEAD_DIM, width), 0) == 0).astype(dtype)


PROJ_TM = 512
PROJ_W_BLOCK = 1536
_N_NORM = 2 * A_W + 2 * B_W
_N_ROPE = C_QW + C_KVW


def _proj_kernel(x_ref, g_ref, w0_ref, w1_ref, w2_ref, gain_ref, ones_ref, cos_ref, s1_ref, s2_ref,
                 za0_ref, za1_ref, za2_ref, qbt_ref, kb_ref, vbt_ref, qc_ref, kc_ref, vc_ref, dil_ref):
    tm = x_ref.shape[0]
    w_refs = (w0_ref, w1_ref, w2_ref)
    h = _rms(x_ref[...], g_ref[...]).astype(_MXU_DTYPE)
    za_refs = (za0_ref, za1_ref, za2_ref)

    def head_norm(acc, c0, width):
        sq = (acc * acc).astype(_MXU_DTYPE)
        ms = _dot(sq, ones_ref[:width, :width]) * (1.0 / HEAD_DIM)
        return acc * lax.rsqrt(ms + RMS_EPS) * gain_ref[:, c0:c0 + width]

    def store_group(g, section, val):
        cols = slice(section * A_OUT, (section + 1) * A_OUT)
        rate = A_PATTERNS[g][1]
        if rate == 1:
            za0_ref[:, cols] = val.astype(za0_ref.dtype)
            return
        for j in range(A_OUT // LANES):
            dil_ref[j] = val[:, j * LANES:(j + 1) * LANES]
        for r in range(rate):
            picked = [dil_ref[j, pl.ds(r, tm // rate, stride=rate), :] for j in range(A_OUT // LANES)]
            za_refs[g][0, r, :, cols] = jnp.concatenate(picked, axis=1).astype(za_refs[g].dtype)

    def store_pairs(ref, t, val):
        vt = val.T.astype(ref.dtype)
        ref[0, 2 * t] = vt[:LANES]
        ref[0, 2 * t + 1] = vt[LANES:]

    def rotary(y, width):
        reps = width // LANES
        table = lambda ref: jnp.concatenate([ref[...]] * reps, axis=1) if reps > 1 else ref[...]
        return (y * table(cos_ref) + pltpu.roll(y, width - 16, 1) * table(s1_ref)
                + pltpu.roll(y, 16, 1) * table(s2_ref))

    def store_qc(t, acc, gain_at):
        yt = rotary(head_norm(acc, gain_at, 256), 256).T.astype(qc_ref.dtype)
        for j in range(256 // HEAD_DIM):
            qc_ref[0, 4 * t + j] = yt[j * HEAD_DIM:(j + 1) * HEAD_DIM]

    def store_kc_vc(acc, gain_at):
        y = rotary(head_norm(acc[:, :LANES], gain_at, LANES), LANES)
        acc_t = acc[:, LANES:].T.astype(vc_ref.dtype)
        tail = _ones_tail(tm, vc_ref.dtype)
        lane = lax.broadcasted_iota(jnp.int32, (tm, LANES), 1)
        one_col = (lane == HEAD_DIM).astype(jnp.float32)
        for kv in range(C_KV_HEADS):
            head = y if kv == 0 else pltpu.roll(y, LANES - kv * HEAD_DIM, 1)
            kc_ref[0, kv] = jnp.where(lane < HEAD_DIM, head, one_col).astype(kc_ref.dtype)
            vc_ref[0, kv, :HEAD_DIM] = acc_t[kv * HEAD_DIM:(kv + 1) * HEAD_DIM]
            vc_ref[0, kv, HEAD_DIM:] = tail

    jobs = []
    for section, src in enumerate((_QA, _KA)):
        for g in range(A_GROUPS):
            gain_at = section * A_W + g * A_OUT
            jobs.append((src + g * A_OUT, A_OUT, lambda acc, gain_at=gain_at, g=g, section=section:
                         store_group(g, section, head_norm(acc, gain_at, A_OUT))))
    for t in range(B_W // 256):
        gain_at = 2 * A_W + t * 256
        jobs.append((_QB + t * 256, 256, lambda acc, gain_at=gain_at, t=t:
                     store_pairs(qbt_ref, t, head_norm(acc, gain_at, 256))))
    for t in range(B_W // 256):
        gain_at = 2 * A_W + B_W + t * 256
        def store_kb(acc, gain_at=gain_at, t=t):
            kb_ref[:, t * 256:(t + 1) * 256] = head_norm(acc, gain_at, 256).astype(kb_ref.dtype)
        jobs.append((_KB + t * 256, 256, store_kb))
    for t in range(C_QW // 256):
        jobs.append((_QC + t * 256, 256, lambda acc, t=t: store_qc(t, acc, _N_NORM + t * 256)))
    jobs.append((_KC, 2 * C_KVW, lambda acc: store_kc_vc(acc, _N_NORM + C_QW)))
    for g in range(A_GROUPS):
        jobs.append((_VA + g * A_OUT, A_OUT, lambda acc, g=g: store_group(g, 2, acc)))
    for t in range(B_W // 256):
        jobs.append((_VB + t * 256, 256, lambda acc, t=t: store_pairs(vbt_ref, t, acc)))

    def product(n):
        src, width, _ = jobs[n]
        blk, col = divmod(src, PROJ_W_BLOCK)
        return _dot(h, w_refs[blk][:, col:col + width].astype(_MXU_DTYPE))

    acc_next = product(0)
    for n, (_, _, consume) in enumerate(jobs):
        acc = acc_next
        if n + 1 < len(jobs):
            acc_next = product(n + 1)
        consume(acc)


def _qkv_projection(x2, g1, w_in, layer, gain_row, ones_bd, rope, B, S):
    M = x2.shape[0]
    tm = PROJ_TM
    per_seq = S // tm
    w_block = lambda j: pl.BlockSpec((None, D_MODEL, PROJ_W_BLOCK), lambda i: (layer, 0, j),
                                     pipeline_mode=pl.Buffered(1))
    cos_t, s1_t, s2_t = rope
    r1, r2 = A_PATTERNS[1][1], A_PATTERNS[2][1]
    row = lambda i: (i, 0)
    pos = lambda i: (i % per_seq, 0)
    hm = lambda i: (i // per_seq, 0, i % per_seq, 0)
    hm_t = lambda i: (i // per_seq, 0, 0, i % per_seq)
    cd = _MXU_DTYPE
    return pl.pallas_call(
        _proj_kernel,
        grid=(M // tm,),
        in_specs=[
            pl.BlockSpec((tm, D_MODEL), row),
            _const_spec((1, D_MODEL)),
            w_block(0), w_block(1), w_block(2),
            _const_spec((1, _N_NORM + _N_ROPE)),
            _const_spec((256, 256)),
            pl.BlockSpec((tm, LANES), pos),
            pl.BlockSpec((tm, LANES), pos),
            pl.BlockSpec((tm, LANES), pos),
        ],
        out_specs=[
            pl.BlockSpec((tm, ZA_W), row),
            pl.BlockSpec((1, r1, tm // r1, ZA_W), hm),
            pl.BlockSpec((1, r2, tm // r2, ZA_W), hm),
            pl.BlockSpec((1, B_PAIRS, LANES, tm), hm_t),
            pl.BlockSpec((tm, B_W), row),
            pl.BlockSpec((1, B_PAIRS, LANES, tm), hm_t),
            pl.BlockSpec((1, C_Q_HEADS, HEAD_DIM, tm), hm_t),
            pl.BlockSpec((1, C_KV_HEADS, tm, LANES), hm),
            pl.BlockSpec((1, C_KV_HEADS, VT_ROWS, tm), hm_t),
        ],
        out_shape=[
            jax.ShapeDtypeStruct((M, ZA_W), cd),
            jax.ShapeDtypeStruct((B, r1, S // r1, ZA_W), cd),
            jax.ShapeDtypeStruct((B, r2, S // r2, ZA_W), cd),
            jax.ShapeDtypeStruct((B, B_PAIRS, LANES, S), cd),
            jax.ShapeDtypeStruct((M, B_W), cd),
            jax.ShapeDtypeStruct((B, B_PAIRS, LANES, S), cd),
            jax.ShapeDtypeStruct((B, C_Q_HEADS, HEAD_DIM, S), cd),
            jax.ShapeDtypeStruct((B, C_KV_HEADS, S, LANES), cd),
            jax.ShapeDtypeStruct((B, C_KV_HEADS, VT_ROWS, S), cd),
        ],
        scratch_shapes=[pltpu.VMEM((A_OUT // LANES, tm, LANES), jnp.float32)],
        compiler_params=_params(("arbitrary",)),
        name="qkv_projection",
    )(x2, g1, w_in, w_in, w_in, gain_row, ones_bd, cos_t, s1_t, s2_t)


A_UNIT = 4 * A_RADIUS
A_SPAN = A_UNIT + 2 * A_RADIUS


def _mixer_a_kernel(q_ref, kp_ref, ko_ref, kn_ref, vp_ref, vo_ref, vn_ref, bias_ref, shift_ref,
                    o_ref, lse_ref, kw_ref, vw_ref, s_ref, *, tl, seq_len):
    l0 = pl.program_id(2) * tl
    kw_ref[0:A_RADIUS] = kp_ref[0, 0]
    kw_ref[A_RADIUS:A_RADIUS + tl] = ko_ref[0, 0]
    kw_ref[A_RADIUS + tl:] = kn_ref[0, 0]
    vw_ref[0:A_RADIUS] = vp_ref[0, 0]
    vw_ref[A_RADIUS:A_RADIUS + tl] = vo_ref[0, 0]
    vw_ref[A_RADIUS + tl:] = vn_ref[0, 0]
    even_q = lax.broadcasted_iota(jnp.int32, (A_UNIT, LANES), 1) < HEAD_DIM
    lane_k = lax.broadcasted_iota(jnp.int32, (A_SPAN, LANES), 1)

    n_units = tl // A_UNIT

    def scores(u, pair):
        off = pl.multiple_of(u * A_UNIT, A_UNIT)
        q = q_ref[0, 0, pl.ds(off, A_UNIT), pair * LANES:(pair + 1) * LANES]
        zero = jnp.zeros_like(q)
        q2 = jnp.concatenate([jnp.where(even_q, q, zero), jnp.where(even_q, zero, q)], axis=0)
        return _dot_nt(kw_ref[pl.ds(off, A_SPAN), pair * LANES:(pair + 1) * LANES], q2)

    s_ref[...] = scores(0, 0)

    def unit(fixed_shift, u, carry):
        off = pl.multiple_of(u * A_UNIT, A_UNIT)
        top_ok = l0 + off - A_RADIUS >= 0
        bot_ok = l0 + off + A_UNIT + A_RADIUS <= seq_len
        outs, lses = [], []
        s_next = s_ref[...]
        for pair in range(A_HEADS // 2):
            s_pair = s_next
            if pair + 1 < A_HEADS // 2:
                s_next = scores(u, pair + 1)
            else:
                s_ref[...] = scores(jnp.minimum(u + 1, n_units - 1), 0)
            v_slab = vw_ref[pl.ds(off, A_SPAN), pair * LANES:(pair + 1) * LANES]
            for odd in range(2):
                s = s_pair[:, odd * A_UNIT:(odd + 1) * A_UNIT] + bias_ref[2 * pair + odd]
                s = jnp.concatenate([jnp.where(top_ok, s[:A_RADIUS], NEG), s[A_RADIUS:A_SPAN - A_RADIUS],
                                     jnp.where(bot_ok, s[A_SPAN - A_RADIUS:], NEG)], axis=0)
                if fixed_shift:
                    m = shift_ref[0]
                    p = jnp.exp2(s).astype(_MXU_DTYPE)
                else:
                    m = jnp.max(s, axis=0, keepdims=True)
                    p = jnp.exp2(s - m).astype(_MXU_DTYPE)
                den_row = (1 - odd) * HEAD_DIM
                mine = (lane_k >= odd * HEAD_DIM) & (lane_k < (odd + 1) * HEAD_DIM)
                v_aug = jnp.where(mine, v_slab, (lane_k == den_row).astype(v_slab.dtype))
                acc = lax.dot_general(v_aug, p, (((0,), (0,)), ((), ())),
                                      preferred_element_type=jnp.float32)
                den = acc[den_row:den_row + 1]
                outs.append(acc[odd * HEAD_DIM:(odd + 1) * HEAD_DIM] / den)
                lses.append(jnp.broadcast_to(m + jnp.log2(den), (HEAD_DIM, A_UNIT)))
        o_ref[0, 0, pl.ds(off, A_UNIT), :] = jnp.concatenate(outs, axis=0).T
        lse_ref[0, 0, pl.ds(off, A_UNIT), :] = jnp.concatenate(lses, axis=0).T
        return carry

    small = shift_ref[1] > 0.5

    @pl.when(small)
    def _():
        lax.fori_loop(0, n_units, functools.partial(unit, True), 0, unroll=2)

    @pl.when(jnp.logical_not(small))
    def _():
        lax.fori_loop(0, n_units, functools.partial(unit, False), 0)


def _mixer_a_group(za_g, bias, shift, rate, B, S):
    L = S // rate
    tl = min(1024, L)
    nblk = L // A_RADIUS
    per = tl // A_RADIUS

    def own(section):
        return lambda b, r, l: (b, r, l, section)

    def prev(section):
        return lambda b, r, l: (b, r, jnp.maximum(l * per - 1, 0), section)

    def nxt(section):
        return lambda b, r, l: (b, r, jnp.minimum((l + 1) * per, nblk - 1), section)

    edge = (1, 1, A_RADIUS, A_OUT)
    full = (1, 1, tl, A_OUT)
    return pl.pallas_call(
        functools.partial(_mixer_a_kernel, tl=tl, seq_len=L),
        grid=(B, rate, L // tl),
        in_specs=[
            pl.BlockSpec(full, own(0)),
            pl.BlockSpec(edge, prev(1)), pl.BlockSpec(full, own(1)), pl.BlockSpec(edge, nxt(1)),
            pl.BlockSpec(edge, prev(2)), pl.BlockSpec(full, own(2)), pl.BlockSpec(edge, nxt(2)),
            _const_spec((A_HEADS, A_SPAN, A_UNIT)),
            pl.BlockSpec(memory_space=pltpu.SMEM),
        ],
        out_specs=[pl.BlockSpec(full, own(0)), pl.BlockSpec(full, own(0))],
        out_shape=[jax.ShapeDtypeStruct((B, rate, L, A_OUT), jnp.float32)] * 2,
        scratch_shapes=[pltpu.VMEM((tl + 2 * A_RADIUS, A_OUT), _MXU_DTYPE)] * 2
        + [pltpu.VMEM((A_SPAN, 2 * A_UNIT), jnp.float32)],
        compiler_params=_params(("arbitrary",) * 3),
        name=f"mixer_a_rate{rate}",
    )(za_g, za_g, za_g, za_g, za_g, za_g, za_g, bias, shift)


B_UNIT_ROWS = 4
B_UNIT = B_UNIT_ROWS * GRID_W
B_SPAN_ROWS = B_UNIT_ROWS + B_WIN_ROWS
B_SPAN = B_SPAN_ROWS * GRID_W
B_HALO = (B_WIN_ROWS // 2) * GRID_W
B_TILE_ROWS = 16
B_TILE = B_TILE_ROWS * GRID_W


def _mixer_b_kernel(q_ref, kp_ref, ko_ref, kn_ref, vp_ref, vo_ref, vn_ref, blocks_ref, sel_ref, shift_ref,
                    o_ref, kw_ref, vw_ref, bias_ref, s_ref, *, rows):
    @pl.when((pl.program_id(0) == 0) & (pl.program_id(1) == 0))
    def _():
        left = lax.broadcasted_iota(jnp.int32, (GRID_W, LANES), 1) < GRID_W

        def fill(n, carry):
            v, h = n // B_HEADS, n % B_HEADS
            for a in range(B_SPAN_ROWS):
                for gp in range(B_UNIT_ROWS // 2):
                    at = (v * B_SPAN_ROWS + a) * B_UNIT_ROWS + 2 * gp
                    tile = jnp.where(left, blocks_ref[h, sel_ref[at]], blocks_ref[h, sel_ref[at + 1]])
                    bias_ref[v, h, a * GRID_W:(a + 1) * GRID_W, gp * LANES:(gp + 1) * LANES] = tile
            return carry

        lax.fori_loop(0, 3 * B_HEADS, fill, 0)

    i0 = pl.program_id(1) * B_TILE_ROWS
    kw_ref[0:B_HALO] = kp_ref[0]
    kw_ref[B_HALO:B_HALO + B_TILE] = ko_ref[0]
    kw_ref[B_HALO + B_TILE:] = kn_ref[0]
    vw_ref[:, :, 0:B_HALO] = vp_ref[0]
    vw_ref[:, :, B_HALO:B_HALO + B_TILE] = vo_ref[0]
    vw_ref[:, :, B_HALO + B_TILE:] = vn_ref[0]
    tail = _ones_tail(B_SPAN, _MXU_DTYPE)
    upper = lax.broadcasted_iota(jnp.int32, (LANES, B_UNIT), 0) < HEAD_DIM

    n_units = B_TILE_ROWS // B_UNIT_ROWS

    def span_offset(u):
        r0 = jnp.clip(i0 + u * B_UNIT_ROWS - B_WIN_ROWS // 2, 0, rows - B_SPAN_ROWS)
        return pl.multiple_of((r0 - (i0 - B_WIN_ROWS // 2)) * GRID_W, LANES)

    def scores(u, pair):
        qt = q_ref[0, pair, :, pl.ds(pl.multiple_of(u * B_UNIT, B_UNIT), B_UNIT)]
        zero = jnp.zeros_like(qt)
        qt2 = jnp.concatenate([jnp.where(upper, qt, zero), jnp.where(upper, zero, qt)], axis=1)
        return _dot(kw_ref[pl.ds(span_offset(u), B_SPAN), pair * LANES:(pair + 1) * LANES], qt2)

    s_ref[...] = scores(0, 0)

    def unit(fixed_shift, u, carry):
        i0u = i0 + u * B_UNIT_ROWS
        off = span_offset(u)
        variant = jnp.where(i0u == 0, 0, jnp.where(i0u == rows - B_UNIT_ROWS, 2, 1))
        qoff = pl.multiple_of(u * B_UNIT, B_UNIT)
        outs = []
        s_next = s_ref[...]
        for pair in range(B_PAIRS):
            s_pair = s_next
            if pair + 1 < B_PAIRS:
                s_next = scores(u, pair + 1)
            else:
                s_ref[...] = scores(jnp.minimum(u + 1, n_units - 1), 0)
            for odd in range(2):
                s = s_pair[:, odd * B_UNIT:(odd + 1) * B_UNIT] + bias_ref[variant, 2 * pair + odd]
                if not fixed_shift:
                    s = s - jnp.max(s, axis=0, keepdims=True)
                p = jnp.exp2(s).astype(_MXU_DTYPE)
                vt = vw_ref[pair, odd * HEAD_DIM:(odd + 1) * HEAD_DIM, pl.ds(off, B_SPAN)]
                acc = _dot(jnp.concatenate([vt, tail], axis=0), p)
                outs.append(acc[:HEAD_DIM] / acc[HEAD_DIM:HEAD_DIM + 1])
        o_ref[0, pl.ds(qoff, B_UNIT), :] = jnp.concatenate(outs, axis=0).T.astype(o_ref.dtype)
        return carry

    small = shift_ref[1] > 0.5

    @pl.when(small)
    def _():
        lax.fori_loop(0, n_units, functools.partial(unit, True), 0, unroll=2)

    @pl.when(jnp.logical_not(small))
    def _():
        lax.fori_loop(0, n_units, functools.partial(unit, False), 0)


def _mixer_b(qbt, kb, vbt, blocks, shift, B, S):
    rows = S // GRID_W
    sel = jnp.asarray(_mixer_b_block_index(rows))
    nt = rows // B_TILE_ROWS
    per = B_TILE // B_HALO
    nh = S // B_HALO
    kb3 = kb.reshape(B, S, B_W)
    prev = lambda t: jnp.maximum(t * per - 1, 0)
    nxt = lambda t: jnp.minimum((t + 1) * per, nh - 1)
    k_edge, k_own = (1, B_HALO, B_W), (1, B_TILE, B_W)
    t_edge, t_own = (1, B_PAIRS, LANES, B_HALO), (1, B_PAIRS, LANES, B_TILE)
    o = pl.pallas_call(
        functools.partial(_mixer_b_kernel, rows=rows),
        grid=(B, nt),
        in_specs=[
            pl.BlockSpec(t_own, lambda b, t: (b, 0, 0, t)),
            pl.BlockSpec(k_edge, lambda b, t: (b, prev(t), 0)),
            pl.BlockSpec(k_own, lambda b, t: (b, t, 0)),
            pl.BlockSpec(k_edge, lambda b, t: (b, nxt(t), 0)),
            pl.BlockSpec(t_edge, lambda b, t: (b, 0, 0, prev(t))),
            pl.BlockSpec(t_own, lambda b, t: (b, 0, 0, t)),
            pl.BlockSpec(t_edge, lambda b, t: (b, 0, 0, nxt(t))),
            _const_spec(blocks.shape),
            pl.BlockSpec(memory_space=pltpu.SMEM),
            pl.BlockSpec(memory_space=pltpu.SMEM),
        ],
        out_specs=pl.BlockSpec(k_own, lambda b, t: (b, t, 0)),
        out_shape=jax.ShapeDtypeStruct((B, S, B_W), _MXU_DTYPE),
        scratch_shapes=[pltpu.VMEM((B_TILE + 2 * B_HALO, B_W), _MXU_DTYPE),
                        pltpu.VMEM((B_PAIRS, LANES, B_TILE + 2 * B_HALO), _MXU_DTYPE),
                        pltpu.VMEM((3, B_HEADS, B_SPAN, B_UNIT), jnp.float32),
                        pltpu.VMEM((B_SPAN, 2 * B_UNIT), jnp.float32)],
        compiler_params=_params(("arbitrary",) * 2),
        name="mixer_b",
    )(qbt, kb3, kb3, kb3, vbt, vbt, vbt, blocks, sel, shift)
    return o.reshape(B * S, B_W)


C_TQ = 1024
C_TK = 512
C_UNIT = 512
C_UNROLL = 4


def _mixer_c_kernel(q_ref, k_ref, vt_ref, shift_ref, o_ref, qcat_ref, m_ref, acc_ref, s_ref, *, n_kv):
    acc_ref[...] = jnp.zeros(acc_ref.shape, jnp.float32)
    for g in range(C_GROUP):
        qcat_ref[:HEAD_DIM, g * C_TQ:(g + 1) * C_TQ] = q_ref[0, g]
    below = lax.broadcasted_iota(jnp.int32, (LANES - HEAD_DIM, C_GROUP * C_TQ), 0) == 0
    qcat_ref[HEAD_DIM:] = jnp.where(below, -shift_ref[0], 0.0).astype(qcat_ref.dtype)
    n_units = C_GROUP * C_TQ // C_UNIT

    def keys(j):
        return k_ref[0, 0, pl.ds(pl.multiple_of(j * C_TK, C_TK), C_TK), :]

    def scores(k, u):
        return _dot(k, qcat_ref[:, u * C_UNIT:(u + 1) * C_UNIT])

    def sweep(update, unroll):
        s_ref[...] = scores(keys(0), 0)

        def step(j, carry):
            k = keys(j)
            vt = vt_ref[0, 0, :, pl.ds(pl.multiple_of(j * C_TK, C_TK), C_TK)]
            s_next = s_ref[...]
            for u in range(n_units):
                s = s_next
                if u + 1 < n_units:
                    s_next = scores(k, u + 1)
                else:
                    s_ref[...] = scores(keys(jnp.minimum(j + 1, n_kv - 1)), 0)
                update(s, vt, slice(u * C_UNIT, (u + 1) * C_UNIT))
            return carry

        lax.fori_loop(0, n_kv, step, 0, unroll=unroll)

    def fixed_shift(s, vt, cols):
        acc_ref[:, cols] += _dot(vt, jnp.exp2(s).astype(_MXU_DTYPE))

    def running_max(s, vt, cols):
        m_prev = m_ref[:, cols]
        m_new = jnp.maximum(m_prev, jnp.max(s, axis=0, keepdims=True))
        alpha = jnp.exp2(m_prev - m_new)
        p = jnp.exp2(s - m_new).astype(_MXU_DTYPE)
        acc_ref[:, cols] = alpha * acc_ref[:, cols] + _dot(vt, p)
        m_ref[:, cols] = m_new

    small = shift_ref[1] > 0.5

    @pl.when(small)
    def _():
        sweep(fixed_shift, C_UNROLL)

    @pl.when(jnp.logical_not(small))
    def _():
        m_ref[...] = jnp.full(m_ref.shape, -jnp.inf, jnp.float32)
        sweep(running_max, 1)

    o_t = jnp.concatenate(
        [acc_ref[:HEAD_DIM, g * C_TQ:(g + 1) * C_TQ] / acc_ref[HEAD_DIM:HEAD_DIM + 1, g * C_TQ:(g + 1) * C_TQ]
         for g in range(C_GROUP)], axis=0)
    o_ref[0] = o_t.T.astype(o_ref.dtype)


def _mixer_c(qc_t, kc, vc_t, shift, B, S):
    o = pl.pallas_call(
        functools.partial(_mixer_c_kernel, n_kv=S // C_TK),
        grid=(B, C_KV_HEADS, S // C_TQ),
        in_specs=[
            pl.BlockSpec((1, C_GROUP, HEAD_DIM, C_TQ), lambda b, kv, i: (b, kv, 0, i)),
            pl.BlockSpec((1, 1, S, LANES), lambda b, kv, i: (b, kv, 0, 0)),
            pl.BlockSpec((1, 1, VT_ROWS, S), lambda b, kv, i: (b, kv, 0, 0)),
            pl.BlockSpec(memory_space=pltpu.SMEM),
        ],
        out_specs=pl.BlockSpec((1, C_TQ, C_GROUP * HEAD_DIM), lambda b, kv, i: (b, i, kv)),
        out_shape=jax.ShapeDtypeStruct((B, S, C_QW), _MXU_DTYPE),
        scratch_shapes=[pltpu.VMEM((LANES, C_GROUP * C_TQ), _MXU_DTYPE),
                        pltpu.VMEM((1, C_GROUP * C_TQ), jnp.float32),
                        pltpu.VMEM((VT_ROWS, C_GROUP * C_TQ), jnp.float32),
                        pltpu.VMEM((C_TK, C_UNIT), jnp.float32)],
        compiler_params=_params(("arbitrary",) * 3),
        name="mixer_c",
    )(qc_t, kc, vc_t, shift)
    return o.reshape(B * S, C_QW)


MERGE_TM = 512
MERGE_GATE_BLOCK = 512


def _merge_kernel(x_ref, g_ref, oa0_ref, oa1_ref, oa2_ref, l0_ref, l1_ref, l2_ref, ob_ref, oc_ref,
                  wg0_ref, wg1_ref, wg2_ref, wg3_ref, wg4_ref, wg5_ref, pa_ref, pb_ref, pc_ref, wo_ref,
                  out_ref, *scratch):
    tm = x_ref.shape[0]
    gate_refs = (wg0_ref, wg1_ref, wg2_ref, wg3_ref, wg4_ref, wg5_ref)
    cast = lambda ref: ref[...].astype(_MXU_DTYPE)

    def token_major(ref, scr):
        rate = ref.shape[1]
        halves = range(A_OUT // LANES)
        for r in range(rate):
            for j in halves:
                scr[j, pl.ds(r, tm // rate, stride=rate), :] = ref[0, r, :, j * LANES:(j + 1) * LANES]
        return jnp.concatenate([scr[j] for j in halves], axis=1)

    x = x_ref[...]
    h = _rms(x, g_ref[...]).astype(_MXU_DTYPE)
    n_parts = D_MODEL // MERGE_GATE_BLOCK
    branches = [None, _dot(ob_ref[...], cast(pb_ref)), _dot(oc_ref[...], cast(pc_ref))]
    gates = {(b, part): jax.nn.sigmoid(_dot(h, cast(gate_refs[b * n_parts + part])))
             for b in (1, 2, 0) for part in range(n_parts)}
    oa0, l0 = oa0_ref[...], l0_ref[...]
    oa1, l1 = token_major(oa1_ref, scratch[0]), token_major(l1_ref, scratch[1])
    oa2, l2 = token_major(oa2_ref, scratch[2]), token_major(l2_ref, scratch[3])
    mx = jnp.maximum(jnp.maximum(l0, l1), l2)
    w0, w1, w2 = jnp.exp2(l0 - mx), jnp.exp2(l1 - mx), jnp.exp2(l2 - mx)
    o_a = (w0 * oa0 + w1 * oa1 + w2 * oa2) / (w0 + w1 + w2)
    branches[0] = _dot(o_a.astype(_MXU_DTYPE), cast(pa_ref))
    parts = []
    for part in range(n_parts):
        cols = slice(part * MERGE_GATE_BLOCK, (part + 1) * MERGE_GATE_BLOCK)
        merged = sum(gates[b, part] * branches[b][:, cols] for b in range(1, N_BRANCH))
        parts.append((merged + gates[0, part] * branches[0][:, cols]).astype(_MXU_DTYPE))
    out_ref[...] = x + _dot(jnp.concatenate(parts, axis=1), cast(wo_ref))


def _merge(x2, g1, oa, lse, ob, oc, w_in, w_br_a, w_br_b, w_br_c, w_o, layer, B, S):
    M = x2.shape[0]
    tm = MERGE_TM
    per_seq = S // tm
    row = lambda i: (i, 0)
    tile = lambda w: pl.BlockSpec((tm, w), row)
    gate0 = _ZG // MERGE_GATE_BLOCK

    def layer_weight(w, block=None, at=0):
        block = w.shape[2] if block is None else block
        return pl.BlockSpec((None, w.shape[1], block), lambda i: (layer, 0, at), pipeline_mode=pl.Buffered(1))

    def dilated(rate):
        return pl.BlockSpec((1, rate, tm // rate, A_OUT), lambda i: (i // per_seq, 0, i % per_seq, 0))

    r1, r2 = A_PATTERNS[1][1], A_PATTERNS[2][1]
    return pl.pallas_call(
        _merge_kernel,
        grid=(M // tm,),
        in_specs=[tile(D_MODEL), _const_spec((1, D_MODEL)),
                  tile(A_OUT), dilated(r1), dilated(r2), tile(A_OUT), dilated(r1), dilated(r2),
                  tile(B_W), tile(C_QW),
                  *[layer_weight(w_in, MERGE_GATE_BLOCK, gate0 + j)
                    for j in range(N_BRANCH * D_MODEL // MERGE_GATE_BLOCK)],
                  layer_weight(w_br_a), layer_weight(w_br_b), layer_weight(w_br_c), layer_weight(w_o)],
        out_specs=tile(D_MODEL),
        out_shape=jax.ShapeDtypeStruct((M, D_MODEL), jnp.float32),
        scratch_shapes=[pltpu.VMEM((A_OUT // LANES, tm, LANES), jnp.float32)] * 4,
        compiler_params=_params(("arbitrary",)),
        name="gated_merge",
    )(x2, g1, oa[0].reshape(M, A_OUT), oa[1], oa[2], lse[0].reshape(M, A_OUT), lse[1], lse[2],
      ob, oc, *([w_in] * (N_BRANCH * D_MODEL // MERGE_GATE_BLOCK)), w_br_a, w_br_b, w_br_c, w_o)


FFN_TM = 512
FFN_CHUNK = 256
FFN_DOWN_CHUNK = D_FF // 2


def _ffn_kernel(x_ref, g_ref, wup_ref, wdown_ref, out_ref, act_ref):
    x = x_ref[...]
    xn = _rms(x, g_ref[...]).astype(_MXU_DTYPE)
    for c in range(D_FF // FFN_CHUNK):
        cs = slice(c * FFN_CHUNK, (c + 1) * FFN_CHUNK)
        a = _dot(xn, wup_ref[:, cs].astype(_MXU_DTYPE))
        b = _dot(xn, wup_ref[:, D_FF + c * FFN_CHUNK:D_FF + (c + 1) * FFN_CHUNK].astype(_MXU_DTYPE))
        act_ref[:, cs] = (a * jax.nn.sigmoid(a) * b).astype(act_ref.dtype)
    y = x
    for c in range(D_FF // FFN_DOWN_CHUNK):
        rows = slice(c * FFN_DOWN_CHUNK, (c + 1) * FFN_DOWN_CHUNK)
        y = y + _dot(act_ref[:, rows], wdown_ref[rows, :].astype(_MXU_DTYPE))
    out_ref[...] = y


def _ffn(x2, g2, w_up, w_down, layer):
    M = x2.shape[0]
    tm = FFN_TM
    row = lambda i: (i, 0)
    layer_weight = lambda w: pl.BlockSpec((None,) + w.shape[1:], lambda i: (layer, 0, 0),
                                          pipeline_mode=pl.Buffered(1))
    return pl.pallas_call(
        _ffn_kernel,
        grid=(M // tm,),
        in_specs=[pl.BlockSpec((tm, D_MODEL), row), _const_spec((1, D_MODEL)),
                  layer_weight(w_up), layer_weight(w_down)],
        out_specs=pl.BlockSpec((tm, D_MODEL), row),
        out_shape=jax.ShapeDtypeStruct((M, D_MODEL), jnp.float32),
        scratch_shapes=[pltpu.VMEM((tm, D_FF), _MXU_DTYPE)],
        compiler_params=_params(("arbitrary",)),
        name="swiglu_ffn",
    )(x2, g2, w_up, w_down)


def _t5_bucket(rel):
    half = T5_BUCKETS // 2
    max_exact = half // 2
    ret = jnp.where(rel > 0, half, 0)
    n = jnp.abs(rel)
    nf = jnp.maximum(n, 1).astype(jnp.float32)
    large = max_exact + (jnp.log(nf / max_exact) / math.log(T5_MAX_DIST / max_exact)
                         * (half - max_exact)).astype(jnp.int32)
    large = jnp.minimum(large, half - 1)
    return ret + jnp.where(n < max_exact, n, large)


def _mixer_a_bias(table_g, rate):
    j = jnp.arange(A_SPAN)[:, None]
    i = jnp.arange(A_UNIT)[None, :]
    step = j - A_RADIUS - i
    onehot = (_t5_bucket(step * rate)[:, :, None] == jnp.arange(T5_BUCKETS)).astype(jnp.float32)
    bias = jnp.einsum("jib,bh->hji", onehot, table_g.astype(jnp.float32) * LOG2_E,
                      precision=lax.Precision.HIGHEST)
    return jnp.where((jnp.abs(step) <= A_RADIUS)[None], bias, NEG)


B_N_DR = 2 * B_WIN_ROWS - 1


def _mixer_b_blocks(rpb, shift):
    c = np.arange(GRID_W)
    c0 = np.clip(c - B_WIN_COLS // 2, 0, GRID_W - B_WIN_COLS)
    col_ok = (c[:, None] >= c0[None, :]) & (c[:, None] < c0[None, :] + B_WIN_COLS)
    dc = np.clip(c[:, None] - c[None, :] + B_WIN_COLS - 1, 0, 2 * B_WIN_COLS - 2)
    pick_c = ((dc[..., None] == np.arange(2 * B_WIN_COLS - 1)) & col_ok[..., None]).astype(np.float32)
    blocks = jnp.einsum("hdk,xyk->hdxy", rpb.astype(jnp.float32) * LOG2_E, pick_c,
                        precision=lax.Precision.HIGHEST)
    blocks = jnp.where(col_ok[None, None], blocks - shift, NEG)
    blocks = jnp.concatenate([blocks, jnp.full((B_HEADS, 1, GRID_W, GRID_W), NEG, jnp.float32)], axis=1)
    return jnp.concatenate([blocks, blocks], axis=-1)


def _mixer_b_block_index(rows):
    first_query_row = np.array([0, B_WIN_ROWS // 2, rows - B_UNIT_ROWS])
    i = first_query_row[:, None, None] + np.arange(B_UNIT_ROWS)[None, None, :]
    r0 = np.clip(first_query_row - B_WIN_ROWS // 2, 0, rows - B_SPAN_ROWS)
    ik = r0[:, None, None] + np.arange(B_SPAN_ROWS)[None, :, None]
    rs = np.clip(i - B_WIN_ROWS // 2, 0, rows - B_WIN_ROWS)
    row_ok = (ik >= rs) & (ik < rs + B_WIN_ROWS)
    return np.where(row_ok, ik - i + B_WIN_ROWS - 1, B_N_DR).astype(np.int32).reshape(-1)


def _rope_tables(S):
    rows = S // GRID_W
    inv = ROPE_THETA ** (-jnp.arange(0, ROPE_AXIS_DIM, 2, dtype=jnp.float32) / ROPE_AXIS_DIM)
    d = np.arange(LANES) % HEAD_DIM
    inv_lane = inv[d % (ROPE_AXIS_DIM // 2)][None, :]
    is_col = (d >= ROPE_AXIS_DIM)[None, None, :]
    first = ((d % ROPE_AXIS_DIM) < ROPE_AXIS_DIM // 2)[None, :]
    tables = []
    for n in (rows, GRID_W):
        ang = jnp.arange(n, dtype=jnp.float32)[:, None] * inv_lane
        sin = jnp.sin(ang)
        tables.append((jnp.cos(ang), jnp.where(first, -sin, 0.0), jnp.where(first, 0.0, sin)))
    return tuple(jnp.where(is_col, by_col[None, :, :], by_row[:, None, :]).reshape(S, LANES)
                 for by_row, by_col in zip(*tables))


def _softmax_shift(gain_q, gain_k, bias_abs_max):
    bound = (HEAD_DIM * QK_SCALE * LOG2_E * ROUNDING_SLACK * jnp.max(jnp.abs(gain_q)) * jnp.max(jnp.abs(gain_k))
             + LOG2_E * bias_abs_max)
    small = bound <= SHIFT_LIMIT
    return jnp.stack([jnp.where(small, bound, 0.0), small.astype(jnp.float32)]).astype(jnp.float32)


def _block_diag_ones():
    i = np.arange(256) // HEAD_DIM
    return jnp.asarray(i[:, None] == i[None, :], _MXU_DTYPE)


def kernel(x, rel_bias_table, norm1, w_in, qk_gain, nat_rpb, w_br_a, w_br_b, w_br_c, w_o,
           norm2, w_up, w_down):
    B, S, D = x.shape
    depth = w_in.shape[0]
    M = B * S
    x2 = x.reshape(M, D)
    rope = _rope_tables(S)
    ones_bd = _block_diag_ones()
    a_bias = [_mixer_a_bias(rel_bias_table[:, g * A_HEADS:(g + 1) * A_HEADS], rate)
              for g, (_, rate) in enumerate(A_PATTERNS)]
    for l in range(depth):
        gq = qk_gain[l]
        tile = lambda g, n, s: jnp.tile(g * s, n)
        gain_row = jnp.concatenate([
            tile(gq[0], A_W // HEAD_DIM, QK_SCALE * LOG2_E), tile(gq[1], A_W // HEAD_DIM, 1.0),
            tile(gq[2], B_HEADS, QK_SCALE * LOG2_E), tile(gq[3], B_HEADS, 1.0),
            tile(gq[4], C_Q_HEADS, QK_SCALE * LOG2_E), tile(gq[5], C_KV_HEADS, 1.0)])[None, :]
        g1 = norm1[l][None, :]
        za0, za1, za2, qbt, kb, vbt, qc, kc, vc = _qkv_projection(
            x2, g1, w_in, l, gain_row, ones_bd, rope, B, S)
        za =(za0.reshape(B, 1, S, ZA_W), za1, za2)
        oa, lse = [], []
        for g, (_, rate) in enumerate(A_PATTERNS):
            table_g = rel_bias_table[:, g * A_HEADS:(g + 1) * A_HEADS]
            shift_a = _softmax_shift(gq[0], gq[1], jnp.max(jnp.abs(table_g)))
            o_g, l_g = _mixer_a_group(za[g], a_bias[g] - shift_a[0], shift_a, rate, B, S)
            oa.append(o_g)
            lse.append(l_g)
        shift_b = _softmax_shift(gq[2], gq[3], jnp.max(jnp.abs(nat_rpb[l])))
        ob = _mixer_b(qbt, kb, vbt, _mixer_b_blocks(nat_rpb[l], shift_b[0]), shift_b, B, S)
        oc = _mixer_c(qc, kc, vc, _softmax_shift(gq[4], gq[5], 0.0), B, S)
        x2 = _merge(x2, g1, oa, lse, ob, oc, w_in, w_br_a, w_br_b, w_br_c, w_o, l, B, S)
        x2 = _ffn(x2, norm2[l][None, :], w_up, w_down, l)
    return x2.reshape(B, S, D)
```

```python
import functools
import math

import jax
import jax.numpy as jnp
import numpy as np
from jax import lax
from jax.experimental import pallas as pl
from jax.experimental.pallas import tpu as pltpu

_MXU_DTYPE = jnp.bfloat16

D_MODEL = 1024
HEAD_DIM = 64
GRID_W = 64
RMS_EPS = 1e-6
NEG = -1e30
A_PATTERNS = ((128, 1), (512, 4), (2048, 16))
A_GROUPS = 3
A_HEADS = 4
A_W = A_GROUPS * A_HEADS * HEAD_DIM
A_OUT = A_HEADS * HEAD_DIM
A_RADIUS = 64
B_HEADS = 8
B_W = B_HEADS * HEAD_DIM
B_WIN_ROWS = 8
B_WIN_COLS = 16
C_Q_HEADS = 8
C_KV_HEADS = 2
C_GROUP = C_Q_HEADS // C_KV_HEADS
C_QW = C_Q_HEADS * HEAD_DIM
C_KVW = C_KV_HEADS * HEAD_DIM
ROPE_THETA = 10000.0
ROPE_AXIS_DIM = HEAD_DIM // 2
T5_BUCKETS = 32
T5_MAX_DIST = 1024
N_BRANCH = 3
D_FF = math.ceil(8 * D_MODEL / 3 / 256) * 256
QK_SCALE = HEAD_DIM ** -0.5
LOG2_E = math.log2(math.e)

VMEM_LIMIT_BYTES = 56 * 1024 * 1024
LANES = 128
BF16_SUBLANES = 16
VT_ROWS = HEAD_DIM + BF16_SUBLANES
SHIFT_LIMIT = 60.0
ROUNDING_SLACK = 1.02

_OFF = np.cumsum([0, A_W, A_W, A_W, B_W, B_W, B_W, C_QW, C_KVW, C_KVW]).tolist()
(_QA, _KA, _VA, _QB, _KB, _VB, _QC, _KC, _VC, _ZG) = _OFF

ZA_W = 3 * A_OUT
B_PAIRS = B_HEADS // 2


def _params(sem):
    return pltpu.CompilerParams(dimension_semantics=sem, vmem_limit_bytes=VMEM_LIMIT_BYTES)


def _const_spec(shape):
    nd = len(shape)
    return pl.BlockSpec(shape, lambda *_: (0,) * nd, pipeline_mode=pl.Buffered(1))


def _rms(x, g):
    return x * lax.rsqrt(jnp.mean(x * x, axis=-1, keepdims=True) + RMS_EPS) * g


def _dot(a, b):
    return jnp.dot(a, b, preferred_element_type=jnp.float32)


def _dot_nt(a, b):
    return lax.dot_general(a, b, (((1,), (1,)), ((), ())), preferred_element_type=jnp.float32)


def _ones_tail(width, dtype):
    return (lax.broadcasted_iota(jnp.int32, (VT_ROWS - HEAD_DIM, width), 0) == 0).astype(dtype)


PROJ_TM = 1024
PROJ_W_BLOCK = 1536
_N_NORM = 2 * A_W + 2 * B_W
_N_ROPE = C_QW + C_KVW


def _proj_kernel(x_ref, g_ref, w0_ref, w1_ref, w2_ref, gain_ref, ones_ref, cos_ref, s1_ref, s2_ref,
                 za0_ref, za1_ref, za2_ref, qbt_ref, kb_ref, vbt_ref, qc_ref, kc_ref, vc_ref, dil_ref):
    tm = x_ref.shape[0]
    w_refs = (w0_ref, w1_ref, w2_ref)
    h = _rms(x_ref[...], g_ref[...]).astype(_MXU_DTYPE)
    za_refs = (za0_ref, za1_ref, za2_ref)

    def head_norm(acc, c0, width):
        sq = (acc * acc).astype(_MXU_DTYPE)
        ms = _dot(sq, ones_ref[:width, :width]) * (1.0 / HEAD_DIM)
        return acc * lax.rsqrt(ms + RMS_EPS) * gain_ref[:, c0:c0 + width]

    def store_group(g, section, val):
        cols = slice(section * A_OUT, (section + 1) * A_OUT)
        rate = A_PATTERNS[g][1]
        if rate == 1:
            za0_ref[:, cols] = val.astype(za0_ref.dtype)
            return
        for j in range(A_OUT // LANES):
            dil_ref[j] = val[:, j * LANES:(j + 1) * LANES]
        for r in range(rate):
            picked = [dil_ref[j, pl.ds(r, tm // rate, stride=rate), :] for j in range(A_OUT // LANES)]
            za_refs[g][0, r, :, cols] = jnp.concatenate(picked, axis=1).astype(za_refs[g].dtype)

    def store_pairs(ref, t, val):
        vt = val.T.astype(ref.dtype)
        ref[0, 2 * t] = vt[:LANES]
        ref[0, 2 * t + 1] = vt[LANES:]

    def rotary(y, width):
        reps = width // LANES
        table = lambda ref: jnp.concatenate([ref[...]] * reps, axis=1) if reps > 1 else ref[...]
        return (y * table(cos_ref) + pltpu.roll(y, width - 16, 1) * table(s1_ref)
                + pltpu.roll(y, 16, 1) * table(s2_ref))

    def store_qc(t, acc, gain_at):
        yt = rotary(head_norm(acc, gain_at, 256), 256).T.astype(qc_ref.dtype)
        for j in range(256 // HEAD_DIM):
            qc_ref[0, 4 * t + j] = yt[j * HEAD_DIM:(j + 1) * HEAD_DIM]

    def store_kc_vc(acc, gain_at):
        y = rotary(head_norm(acc[:, :LANES], gain_at, LANES), LANES).astype(kc_ref.dtype)
        acc_t = acc[:, LANES:].T.astype(vc_ref.dtype)
        tail = _ones_tail(tm, vc_ref.dtype)
        for kv in range(C_KV_HEADS):
            kc_ref[0, kv] = y[:, kv * HEAD_DIM:(kv + 1) * HEAD_DIM]
            vc_ref[0, kv, :HEAD_DIM] = acc_t[kv * HEAD_DIM:(kv + 1) * HEAD_DIM]
            vc_ref[0, kv, HEAD_DIM:] = tail

    jobs = []
    for section, src in enumerate((_QA, _KA)):
        for g in range(A_GROUPS):
            gain_at = section * A_W + g * A_OUT
            jobs.append((src + g * A_OUT, A_OUT, lambda acc, gain_at=gain_at, g=g, section=section:
                         store_group(g, section, head_norm(acc, gain_at, A_OUT))))
    for t in range(B_W // 256):
        gain_at = 2 * A_W + t * 256
        jobs.append((_QB + t * 256, 256, lambda acc, gain_at=gain_at, t=t:
                     store_pairs(qbt_ref, t, head_norm(acc, gain_at, 256))))
    for t in range(B_W // 256):
        gain_at = 2 * A_W + B_W + t * 256
        def store_kb(acc, gain_at=gain_at, t=t):
            kb_ref[:, t * 256:(t + 1) * 256] = head_norm(acc, gain_at, 256).astype(kb_ref.dtype)
        jobs.append((_KB + t * 256, 256, store_kb))
    for t in range(C_QW // 256):
        jobs.append((_QC + t * 256, 256, lambda acc, t=t: store_qc(t, acc, _N_NORM + t * 256)))
    jobs.append((_KC, 2 * C_KVW, lambda acc: store_kc_vc(acc, _N_NORM + C_QW)))
    for g in range(A_GROUPS):
        jobs.append((_VA + g * A_OUT, A_OUT, lambda acc, g=g: store_group(g, 2, acc)))
    for t in range(B_W // 256):
        jobs.append((_VB + t * 256, 256, lambda acc, t=t: store_pairs(vbt_ref, t, acc)))

    def product(n):
        src, width, _ = jobs[n]
        blk, col = divmod(src, PROJ_W_BLOCK)
        return _dot(h, w_refs[blk][:, col:col + width].astype(_MXU_DTYPE))

    acc_next = product(0)
    for n, (_, _, consume) in enumerate(jobs):
        acc = acc_next
        if n + 1 < len(jobs):
            acc_next = product(n + 1)
        consume(acc)


def _qkv_projection(x2, g1, w_in, layer, gain_row, ones_bd, rope, B, S):
    M = x2.shape[0]
    tm = PROJ_TM
    per_seq = S // tm
    w_block = lambda j: pl.BlockSpec((None, D_MODEL, PROJ_W_BLOCK), lambda i: (layer, 0, j),
                                     pipeline_mode=pl.Buffered(1))
    cos_t, s1_t, s2_t = rope
    r1, r2 = A_PATTERNS[1][1], A_PATTERNS[2][1]
    row = lambda i: (i, 0)
    pos = lambda i: (i % per_seq, 0)
    hm = lambda i: (i // per_seq, 0, i % per_seq, 0)
    hm_t = lambda i: (i // per_seq, 0, 0, i % per_seq)
    cd = _MXU_DTYPE
    return pl.pallas_call(
        _proj_kernel,
        grid=(M // tm,),
        in_specs=[
            pl.BlockSpec((tm, D_MODEL), row),
            _const_spec((1, D_MODEL)),
            w_block(0), w_block(1), w_block(2),
            _const_spec((1, _N_NORM + _N_ROPE)),
            _const_spec((256, 256)),
            pl.BlockSpec((tm, LANES), pos),
            pl.BlockSpec((tm, LANES), pos),
            pl.BlockSpec((tm, LANES), pos),
        ],
        out_specs=[
            pl.BlockSpec((tm, ZA_W), row),
            pl.BlockSpec((1, r1, tm // r1, ZA_W), hm),
            pl.BlockSpec((1, r2, tm // r2, ZA_W), hm),
            pl.BlockSpec((1, B_PAIRS, LANES, tm), hm_t),
            pl.BlockSpec((tm, B_W), row),
            pl.BlockSpec((1, B_PAIRS, LANES, tm), hm_t),
            pl.BlockSpec((1, C_Q_HEADS, HEAD_DIM, tm), hm_t),
            pl.BlockSpec((1, C_KV_HEADS, tm, HEAD_DIM), hm),
            pl.BlockSpec((1, C_KV_HEADS, VT_ROWS, tm), hm_t),
        ],
        out_shape=[
            jax.ShapeDtypeStruct((M, ZA_W), cd),
            jax.ShapeDtypeStruct((B, r1, S // r1, ZA_W), cd),
            jax.ShapeDtypeStruct((B, r2, S // r2, ZA_W), cd),
            jax.ShapeDtypeStruct((B, B_PAIRS, LANES, S), cd),
            jax.ShapeDtypeStruct((M, B_W), cd),
            jax.ShapeDtypeStruct((B, B_PAIRS, LANES, S), cd),
            jax.ShapeDtypeStruct((B, C_Q_HEADS, HEAD_DIM, S), cd),
            jax.ShapeDtypeStruct((B, C_KV_HEADS, S, HEAD_DIM), cd),
            jax.ShapeDtypeStruct((B, C_KV_HEADS, VT_ROWS, S), cd),
        ],
        scratch_shapes=[pltpu.VMEM((A_OUT // LANES, tm, LANES), jnp.float32)],
        compiler_params=_params(("arbitrary",)),
        name="qkv_projection",
    )(x2, g1, w_in, w_in, w_in, gain_row, ones_bd, cos_t, s1_t, s2_t)


A_UNIT = 4 * A_RADIUS
A_SPAN = A_UNIT + 2 * A_RADIUS


def _mixer_a_kernel(q_ref, kp_ref, ko_ref, kn_ref, vp_ref, vo_ref, vn_ref, bias_ref, shift_ref,
                    o_ref, lse_ref, kw_ref, vw_ref, s_ref, *, tl, seq_len):
    l0 = pl.program_id(2) * tl
    kw_ref[0:A_RADIUS] = kp_ref[0, 0]
    kw_ref[A_RADIUS:A_RADIUS + tl] = ko_ref[0, 0]
    kw_ref[A_RADIUS + tl:] = kn_ref[0, 0]
    vw_ref[0:A_RADIUS] = vp_ref[0, 0]
    vw_ref[A_RADIUS:A_RADIUS + tl] = vo_ref[0, 0]
    vw_ref[A_RADIUS + tl:] = vn_ref[0, 0]
    even_q = lax.broadcasted_iota(jnp.int32, (A_UNIT, LANES), 1) < HEAD_DIM
    lane_k = lax.broadcasted_iota(jnp.int32, (A_SPAN, LANES), 1)

    n_units = tl // A_UNIT

    def scores(u, pair):
        off = pl.multiple_of(u * A_UNIT, A_UNIT)
        q = q_ref[0, 0, pl.ds(off, A_UNIT), pair * LANES:(pair + 1) * LANES]
        zero = jnp.zeros_like(q)
        q2 = jnp.concatenate([jnp.where(even_q, q, zero), jnp.where(even_q, zero, q)], axis=0)
        return _dot_nt(kw_ref[pl.ds(off, A_SPAN), pair * LANES:(pair + 1) * LANES], q2)

    s_ref[...] = scores(0, 0)

    def unit(fixed_shift, u, carry):
        off = pl.multiple_of(u * A_UNIT, A_UNIT)
        top_ok = l0 + off - A_RADIUS >= 0
        bot_ok = l0 + off + A_UNIT + A_RADIUS <= seq_len
        outs, lses = [], []
        s_next = s_ref[...]
        for pair in range(A_HEADS // 2):
            s_pair = s_next
            if pair + 1 < A_HEADS // 2:
                s_next = scores(u, pair + 1)
            else:
                s_ref[...] = scores(jnp.minimum(u + 1, n_units - 1), 0)
            v_slab = vw_ref[pl.ds(off, A_SPAN), pair * LANES:(pair + 1) * LANES]
            for odd in range(2):
                s = s_pair[:, odd * A_UNIT:(odd + 1) * A_UNIT] + bias_ref[2 * pair + odd]
                s = jnp.concatenate([jnp.where(top_ok, s[:A_RADIUS], NEG), s[A_RADIUS:A_SPAN - A_RADIUS],
                                     jnp.where(bot_ok, s[A_SPAN - A_RADIUS:], NEG)], axis=0)
                if fixed_shift:
                    m = shift_ref[0]
                    p = jnp.exp2(s).astype(_MXU_DTYPE)
                else:
                    m = jnp.max(s, axis=0, keepdims=True)
                    p = jnp.exp2(s - m).astype(_MXU_DTYPE)
                den_row = (1 - odd) * HEAD_DIM
                mine = (lane_k >= odd * HEAD_DIM) & (lane_k < (odd + 1) * HEAD_DIM)
                v_aug = jnp.where(mine, v_slab, (lane_k == den_row).astype(v_slab.dtype))
                acc = lax.dot_general(v_aug, p, (((0,), (0,)), ((), ())),
                                      preferred_element_type=jnp.float32)
                den = acc[den_row:den_row + 1]
                outs.append(acc[odd * HEAD_DIM:(odd + 1) * HEAD_DIM] / den)
                lses.append(jnp.broadcast_to(m + jnp.log2(den), (HEAD_DIM, A_UNIT)))
        o_ref[0, 0, pl.ds(off, A_UNIT), :] = jnp.concatenate(outs, axis=0).T
        lse_ref[0, 0, pl.ds(off, A_UNIT), :] = jnp.concatenate(lses, axis=0).T
        return carry

    small = shift_ref[1] > 0.5

    @pl.when(small)
    def _():
        lax.fori_loop(0, n_units, functools.partial(unit, True), 0, unroll=2)

    @pl.when(jnp.logical_not(small))
    def _():
        lax.fori_loop(0, n_units, functools.partial(unit, False), 0)


def _mixer_a_group(za_g, bias, shift, rate, B, S):
    L = S // rate
    tl = min(1024, L)
    nblk = L // A_RADIUS
    per = tl // A_RADIUS

    def own(section):
        return lambda b, r, l: (b, r, l, section)

    def prev(section):
        return lambda b, r, l: (b, r, jnp.maximum(l * per - 1, 0), section)

    def nxt(section):
        return lambda b, r, l: (b, r, jnp.minimum((l + 1) * per, nblk - 1), section)

    edge = (1, 1, A_RADIUS, A_OUT)
    full = (1, 1, tl, A_OUT)
    return pl.pallas_call(
        functools.partial(_mixer_a_kernel, tl=tl, seq_len=L),
        grid=(B, rate, L // tl),
        in_specs=[
            pl.BlockSpec(full, own(0)),
            pl.BlockSpec(edge, prev(1)), pl.BlockSpec(full, own(1)), pl.BlockSpec(edge, nxt(1)),
            pl.BlockSpec(edge, prev(2)), pl.BlockSpec(full, own(2)), pl.BlockSpec(edge, nxt(2)),
            _const_spec((A_HEADS, A_SPAN, A_UNIT)),
            pl.BlockSpec(memory_space=pltpu.SMEM),
        ],
        out_specs=[pl.BlockSpec(full, own(0)), pl.BlockSpec(full, own(0))],
        out_shape=[jax.ShapeDtypeStruct((B, rate, L, A_OUT), jnp.float32)] * 2,
        scratch_shapes=[pltpu.VMEM((tl + 2 * A_RADIUS, A_OUT), _MXU_DTYPE)] * 2
        + [pltpu.VMEM((A_SPAN, 2 * A_UNIT), jnp.float32)],
        compiler_params=_params(("arbitrary",) * 3),
        name=f"mixer_a_rate{rate}",
    )(za_g, za_g, za_g, za_g, za_g, za_g, za_g, bias, shift)


B_UNIT_ROWS = 4
B_UNIT = B_UNIT_ROWS * GRID_W
B_SPAN_ROWS = B_UNIT_ROWS + B_WIN_ROWS
B_SPAN = B_SPAN_ROWS * GRID_W
B_HALO = (B_WIN_ROWS // 2) * GRID_W
B_TILE_ROWS = 16
B_TILE = B_TILE_ROWS * GRID_W


def _mixer_b_kernel(q_ref, kp_ref, ko_ref, kn_ref, vp_ref, vo_ref, vn_ref, blocks_ref, sel_ref, shift_ref,
                    o_ref, kw_ref, vw_ref, bias_ref, s_ref, *, rows):
    @pl.when((pl.program_id(0) == 0) & (pl.program_id(1) == 0))
    def _():
        left = lax.broadcasted_iota(jnp.int32, (GRID_W, LANES), 1) < GRID_W

        def fill(n, carry):
            v, h = n // B_HEADS, n % B_HEADS
            for a in range(B_SPAN_ROWS):
                for gp in range(B_UNIT_ROWS // 2):
                    at = (v * B_SPAN_ROWS + a) * B_UNIT_ROWS + 2 * gp
                    tile = jnp.where(left, blocks_ref[h, sel_ref[at]], blocks_ref[h, sel_ref[at + 1]])
                    bias_ref[v, h, a * GRID_W:(a + 1) * GRID_W, gp * LANES:(gp + 1) * LANES] = tile
            return carry

        lax.fori_loop(0, 3 * B_HEADS, fill, 0)

    i0 = pl.program_id(1) * B_TILE_ROWS
    kw_ref[0:B_HALO] = kp_ref[0]
    kw_ref[B_HALO:B_HALO + B_TILE] = ko_ref[0]
    kw_ref[B_HALO + B_TILE:] = kn_ref[0]
    vw_ref[:, :, 0:B_HALO] = vp_ref[0]
    vw_ref[:, :, B_HALO:B_HALO + B_TILE] = vo_ref[0]
    vw_ref[:, :, B_HALO + B_TILE:] = vn_ref[0]
    tail = _ones_tail(B_SPAN, _MXU_DTYPE)
    upper = lax.broadcasted_iota(jnp.int32, (LANES, B_UNIT), 0) < HEAD_DIM

    n_units = B_TILE_ROWS // B_UNIT_ROWS

    def span_offset(u):
        r0 = jnp.clip(i0 + u * B_UNIT_ROWS - B_WIN_ROWS // 2, 0, rows - B_SPAN_ROWS)
        return pl.multiple_of((r0 - (i0 - B_WIN_ROWS // 2)) * GRID_W, LANES)

    def scores(u, pair):
        qt = q_ref[0, pair, :, pl.ds(pl.multiple_of(u * B_UNIT, B_UNIT), B_UNIT)]
        zero = jnp.zeros_like(qt)
        qt2 = jnp.concatenate([jnp.where(upper, qt, zero), jnp.where(upper, zero, qt)], axis=1)
        return _dot(kw_ref[pl.ds(span_offset(u), B_SPAN), pair * LANES:(pair + 1) * LANES], qt2)

    s_ref[...] = scores(0, 0)

    def unit(fixed_shift, u, carry):
        i0u = i0 + u * B_UNIT_ROWS
        off = span_offset(u)
        variant = jnp.where(i0u == 0, 0, jnp.where(i0u == rows - B_UNIT_ROWS, 2, 1))
        qoff = pl.multiple_of(u * B_UNIT, B_UNIT)
        outs = []
        s_next = s_ref[...]
        for pair in range(B_PAIRS):
            s_pair = s_next
            if pair + 1 < B_PAIRS:
                s_next = scores(u, pair + 1)
            else:
                s_ref[...] = scores(jnp.minimum(u + 1, n_units - 1), 0)
            for odd in range(2):
                s = s_pair[:, odd * B_UNIT:(odd + 1) * B_UNIT] + bias_ref[variant, 2 * pair + odd]
                if not fixed_shift:
                    s = s - jnp.max(s, axis=0, keepdims=True)
                p = jnp.exp2(s).astype(_MXU_DTYPE)
                vt = vw_ref[pair, odd * HEAD_DIM:(odd + 1) * HEAD_DIM, pl.ds(off, B_SPAN)]
                acc = _dot(jnp.concatenate([vt, tail], axis=0), p)
                outs.append(acc[:HEAD_DIM] / acc[HEAD_DIM:HEAD_DIM + 1])
        o_ref[0, pl.ds(qoff, B_UNIT), :] = jnp.concatenate(outs, axis=0).T.astype(o_ref.dtype)
        return carry

    small = shift_ref[1] > 0.5

    @pl.when(small)
    def _():
        lax.fori_loop(0, n_units, functools.partial(unit, True), 0, unroll=2)

    @pl.when(jnp.logical_not(small))
    def _():
        lax.fori_loop(0, n_units, functools.partial(unit, False), 0)


def _mixer_b(qbt, kb, vbt, blocks, shift, B, S):
    rows = S // GRID_W
    sel = jnp.asarray(_mixer_b_block_index(rows))
    nt = rows // B_TILE_ROWS
    per = B_TILE // B_HALO
    nh = S // B_HALO
    kb3 = kb.reshape(B, S, B_W)
    prev = lambda t: jnp.maximum(t * per - 1, 0)
    nxt = lambda t: jnp.minimum((t + 1) * per, nh - 1)
    k_edge, k_own = (1, B_HALO, B_W), (1, B_TILE, B_W)
    t_edge, t_own = (1, B_PAIRS, LANES, B_HALO), (1, B_PAIRS, LANES, B_TILE)
    o = pl.pallas_call(
        functools.partial(_mixer_b_kernel, rows=rows),
        grid=(B, nt),
        in_specs=[
            pl.BlockSpec(t_own, lambda b, t: (b, 0, 0, t)),
            pl.BlockSpec(k_edge, lambda b, t: (b, prev(t), 0)),
            pl.BlockSpec(k_own, lambda b, t: (b, t, 0)),
            pl.BlockSpec(k_edge, lambda b, t: (b, nxt(t), 0)),
            pl.BlockSpec(t_edge, lambda b, t: (b, 0, 0, prev(t))),
            pl.BlockSpec(t_own, lambda b, t: (b, 0, 0, t)),
            pl.BlockSpec(t_edge, lambda b, t: (b, 0, 0, nxt(t))),
            _const_spec(blocks.shape),
            pl.BlockSpec(memory_space=pltpu.SMEM),
            pl.BlockSpec(memory_space=pltpu.SMEM),
        ],
        out_specs=pl.BlockSpec(k_own, lambda b, t: (b, t, 0)),
        out_shape=jax.ShapeDtypeStruct((B, S, B_W), _MXU_DTYPE),
        scratch_shapes=[pltpu.VMEM((B_TILE + 2 * B_HALO, B_W), _MXU_DTYPE),
                        pltpu.VMEM((B_PAIRS, LANES, B_TILE + 2 * B_HALO), _MXU_DTYPE),
                        pltpu.VMEM((3, B_HEADS, B_SPAN, B_UNIT), jnp.float32),
                        pltpu.VMEM((B_SPAN, 2 * B_UNIT), jnp.float32)],
        compiler_params=_params(("arbitrary",) * 2),
        name="mixer_b",
    )(qbt, kb3, kb3, kb3, vbt, vbt, vbt, blocks, sel, shift)
    return o.reshape(B * S, B_W)


C_TQ = 1024
C_TK = 512
C_UNIT = 512
C_UNROLL = 4


def _mixer_c_kernel(q_ref, k_ref, vt_ref, shift_ref, o_ref, qcat_ref, m_ref, acc_ref, s_ref, *, n_kv):
    acc_ref[...] = jnp.zeros(acc_ref.shape, jnp.float32)
    for g in range(C_GROUP):
        qcat_ref[:, g * C_TQ:(g + 1) * C_TQ] = q_ref[0, g]
    n_units = C_GROUP * C_TQ // C_UNIT

    def keys(j):
        return k_ref[0, 0, pl.ds(pl.multiple_of(j * C_TK, C_TK), C_TK), :]

    def scores(k, u):
        return _dot(k, qcat_ref[:, u * C_UNIT:(u + 1) * C_UNIT])

    def sweep(update, unroll):
        s_ref[...] = scores(keys(0), 0)

        def step(j, carry):
            k = keys(j)
            vt = vt_ref[0, 0, :, pl.ds(pl.multiple_of(j * C_TK, C_TK), C_TK)]
            s_next = s_ref[...]
            for u in range(n_units):
                s = s_next
                if u + 1 < n_units:
                    s_next = scores(k, u + 1)
                else:
                    s_ref[...] = scores(keys(jnp.minimum(j + 1, n_kv - 1)), 0)
                update(s, vt, slice(u * C_UNIT, (u + 1) * C_UNIT))
            return carry

        lax.fori_loop(0, n_kv, step, 0, unroll=unroll)

    def fixed_shift(s, vt, cols):
        acc_ref[:, cols] += _dot(vt, jnp.exp2(s - shift_ref[0]).astype(_MXU_DTYPE))

    def running_max(s, vt, cols):
        m_prev = m_ref[:, cols]
        m_new = jnp.maximum(m_prev, jnp.max(s, axis=0, keepdims=True))
        alpha = jnp.exp2(m_prev - m_new)
        p = jnp.exp2(s - m_new).astype(_MXU_DTYPE)
        acc_ref[:, cols] = alpha * acc_ref[:, cols] + _dot(vt, p)
        m_ref[:, cols] = m_new

    small = shift_ref[1] > 0.5

    @pl.when(small)
    def _():
        sweep(fixed_shift, C_UNROLL)

    @pl.when(jnp.logical_not(small))
    def _():
        m_ref[...] = jnp.full(m_ref.shape, -jnp.inf, jnp.float32)
        sweep(running_max, 1)

    o_t = jnp.concatenate(
        [acc_ref[:HEAD_DIM, g * C_TQ:(g + 1) * C_TQ] / acc_ref[HEAD_DIM:HEAD_DIM + 1, g * C_TQ:(g + 1) * C_TQ]
         for g in range(C_GROUP)], axis=0)
    o_ref[0] = o_t.T.astype(o_ref.dtype)


def _mixer_c(qc_t, kc, vc_t, shift, B, S):
    o = pl.pallas_call(
        functools.partial(_mixer_c_kernel, n_kv=S // C_TK),
        grid=(B, C_KV_HEADS, S // C_TQ),
        in_specs=[
            pl.BlockSpec((1, C_GROUP, HEAD_DIM, C_TQ), lambda b, kv, i: (b, kv, 0, i)),
            pl.BlockSpec((1, 1, S, HEAD_DIM), lambda b, kv, i: (b, kv, 0, 0)),
            pl.BlockSpec((1, 1, VT_ROWS, S), lambda b, kv, i: (b, kv, 0, 0)),
            pl.BlockSpec(memory_space=pltpu.SMEM),
        ],
        out_specs=pl.BlockSpec((1, C_TQ, C_GROUP * HEAD_DIM), lambda b, kv, i: (b, i, kv)),
        out_shape=jax.ShapeDtypeStruct((B, S, C_QW), _MXU_DTYPE),
        scratch_shapes=[pltpu.VMEM((HEAD_DIM, C_GROUP * C_TQ), _MXU_DTYPE),
                        pltpu.VMEM((1, C_GROUP * C_TQ), jnp.float32),
                        pltpu.VMEM((VT_ROWS, C_GROUP * C_TQ), jnp.float32),
                        pltpu.VMEM((C_TK, C_UNIT), jnp.float32)],
        compiler_params=_params(("arbitrary",) * 3),
        name="mixer_c",
    )(qc_t, kc, vc_t, shift)
    return o.reshape(B * S, C_QW)


MERGE_TM = 512
MERGE_GATE_BLOCK = 512


def _merge_kernel(x_ref, g_ref, oa0_ref, oa1_ref, oa2_ref, l0_ref, l1_ref, l2_ref, ob_ref, oc_ref,
                  wg0_ref, wg1_ref, wg2_ref, wg3_ref, wg4_ref, wg5_ref, pa_ref, pb_ref, pc_ref, wo_ref,
                  out_ref, *scratch):
    tm = x_ref.shape[0]
    gate_refs = (wg0_ref, wg1_ref, wg2_ref, wg3_ref, wg4_ref, wg5_ref)
    cast = lambda ref: ref[...].astype(_MXU_DTYPE)

    def token_major(ref, scr):
        rate = ref.shape[1]
        halves = range(A_OUT // LANES)
        for r in range(rate):
            for j in halves:
                scr[j, pl.ds(r, tm // rate, stride=rate), :] = ref[0, r, :, j * LANES:(j + 1) * LANES]
        return jnp.concatenate([scr[j] for j in halves], axis=1)

    x = x_ref[...]
    h = _rms(x, g_ref[...]).astype(_MXU_DTYPE)
    n_parts = D_MODEL // MERGE_GATE_BLOCK
    branches = [None, _dot(ob_ref[...], cast(pb_ref)), _dot(oc_ref[...], cast(pc_ref))]
    gates = {(b, part): jax.nn.sigmoid(_dot(h, cast(gate_refs[b * n_parts + part])))
             for b in (1, 2, 0) for part in range(n_parts)}
    oa0, l0 = oa0_ref[...], l0_ref[...]
    oa1, l1 = token_major(oa1_ref, scratch[0]), token_major(l1_ref, scratch[1])
    oa2, l2 = token_major(oa2_ref, scratch[2]), token_major(l2_ref, scratch[3])
    mx = jnp.maximum(jnp.maximum(l0, l1), l2)
    w0, w1, w2 = jnp.exp2(l0 - mx), jnp.exp2(l1 - mx), jnp.exp2(l2 - mx)
    o_a = (w0 * oa0 + w1 * oa1 + w2 * oa2) / (w0 + w1 + w2)
    branches[0] = _dot(o_a.astype(_MXU_DTYPE), cast(pa_ref))
    parts = []
    for part in range(n_parts):
        cols = slice(part * MERGE_GATE_BLOCK, (part + 1) * MERGE_GATE_BLOCK)
        merged = sum(gates[b, part] * branches[b][:, cols] for b in range(1, N_BRANCH))
        parts.append((merged + gates[0, part] * branches[0][:, cols]).astype(_MXU_DTYPE))
    out_ref[...] = x + _dot(jnp.concatenate(parts, axis=1), cast(wo_ref))


def _merge(x2, g1, oa, lse, ob, oc, w_in, w_br_a, w_br_b, w_br_c, w_o, layer, B, S):
    M = x2.shape[0]
    tm = MERGE_TM
    per_seq = S // tm
    row = lambda i: (i, 0)
    tile = lambda w: pl.BlockSpec((tm, w), row)
    gate0 = _ZG // MERGE_GATE_BLOCK

    def layer_weight(w, block=None, at=0):
        block = w.shape[2] if block is None else block
        return pl.BlockSpec((None, w.shape[1], block), lambda i: (layer, 0, at), pipeline_mode=pl.Buffered(1))

    def dilated(rate):
        return pl.BlockSpec((1, rate, tm // rate, A_OUT), lambda i: (i // per_seq, 0, i % per_seq, 0))

    r1, r2 = A_PATTERNS[1][1], A_PATTERNS[2][1]
    return pl.pallas_call(
        _merge_kernel,
        grid=(M // tm,),
        in_specs=[tile(D_MODEL), _const_spec((1, D_MODEL)),
                  tile(A_OUT), dilated(r1), dilated(r2), tile(A_OUT), dilated(r1), dilated(r2),
                  tile(B_W), tile(C_QW),
                  *[layer_weight(w_in, MERGE_GATE_BLOCK, gate0 + j)
                    for j in range(N_BRANCH * D_MODEL // MERGE_GATE_BLOCK)],
                  layer_weight(w_br_a), layer_weight(w_br_b), layer_weight(w_br_c), layer_weight(w_o)],
        out_specs=tile(D_MODEL),
        out_shape=jax.ShapeDtypeStruct((M, D_MODEL), jnp.float32),
        scratch_shapes=[pltpu.VMEM((A_OUT // LANES, tm, LANES), jnp.float32)] * 4,
        compiler_params=_params(("arbitrary",)),
        name="gated_merge",
    )(x2, g1, oa[0].reshape(M, A_OUT), oa[1], oa[2], lse[0].reshape(M, A_OUT), lse[1], lse[2],
      ob, oc, *([w_in] * (N_BRANCH * D_MODEL // MERGE_GATE_BLOCK)), w_br_a, w_br_b, w_br_c, w_o)


FFN_TM = 512
FFN_CHUNK = 256
FFN_DOWN_CHUNK = D_FF // 2


def _ffn_kernel(x_ref, g_ref, wup_ref, wdown_ref, out_ref, act_ref):
    x = x_ref[...]
    xn = _rms(x, g_ref[...]).astype(_MXU_DTYPE)
    for c in range(D_FF // FFN_CHUNK):
        cs = slice(c * FFN_CHUNK, (c + 1) * FFN_CHUNK)
        a = _dot(xn, wup_ref[:, cs].astype(_MXU_DTYPE))
        b = _dot(xn, wup_ref[:, D_FF + c * FFN_CHUNK:D_FF + (c + 1) * FFN_CHUNK].astype(_MXU_DTYPE))
        act_ref[:, cs] = (a * jax.nn.sigmoid(a) * b).astype(act_ref.dtype)
    y = x
    for c in range(D_FF // FFN_DOWN_CHUNK):
        rows = slice(c * FFN_DOWN_CHUNK, (c + 1) * FFN_DOWN_CHUNK)
        y = y + _dot(act_ref[:, rows], wdown_ref[rows, :].astype(_MXU_DTYPE))
    out_ref[...] = y


def _ffn(x2, g2, w_up, w_down, layer):
    M = x2.shape[0]
    tm = FFN_TM
    row = lambda i: (i, 0)
    layer_weight = lambda w: pl.BlockSpec((None,) + w.shape[1:], lambda i: (layer, 0, 0),
                                          pipeline_mode=pl.Buffered(1))
    return pl.pallas_call(
        _ffn_kernel,
        grid=(M // tm,),
        in_specs=[pl.BlockSpec((tm, D_MODEL), row), _const_spec((1, D_MODEL)),
                  layer_weight(w_up), layer_weight(w_down)],
        out_specs=pl.BlockSpec((tm, D_MODEL), row),
        out_shape=jax.ShapeDtypeStruct((M, D_MODEL), jnp.float32),
        scratch_shapes=[pltpu.VMEM((tm, D_FF), _MXU_DTYPE)],
        compiler_params=_params(("arbitrary",)),
        name="swiglu_ffn",
    )(x2, g2, w_up, w_down)


def _t5_bucket(rel):
    half = T5_BUCKETS // 2
    max_exact = half // 2
    ret = jnp.where(rel > 0, half, 0)
    n = jnp.abs(rel)
    nf = jnp.maximum(n, 1).astype(jnp.float32)
    large = max_exact + (jnp.log(nf / max_exact) / math.log(T5_MAX_DIST / max_exact)
                         * (half - max_exact)).astype(jnp.int32)
    large = jnp.minimum(large, half - 1)
    return ret + jnp.where(n < max_exact, n, large)


def _mixer_a_bias(table_g, rate):
    j = jnp.arange(A_SPAN)[:, None]
    i = jnp.arange(A_UNIT)[None, :]
    step = j - A_RADIUS - i
    onehot = (_t5_bucket(step * rate)[:, :, None] == jnp.arange(T5_BUCKETS)).astype(jnp.float32)
    bias = jnp.einsum("jib,bh->hji", onehot, table_g.astype(jnp.float32) * LOG2_E,
                      precision=lax.Precision.HIGHEST)
    return jnp.where((jnp.abs(step) <= A_RADIUS)[None], bias, NEG)


B_N_DR = 2 * B_WIN_ROWS - 1


def _mixer_b_blocks(rpb, shift):
    c = np.arange(GRID_W)
    c0 = np.clip(c - B_WIN_COLS // 2, 0, GRID_W - B_WIN_COLS)
    col_ok = (c[:, None] >= c0[None, :]) & (c[:, None] < c0[None, :] + B_WIN_COLS)
    dc = np.clip(c[:, None] - c[None, :] + B_WIN_COLS - 1, 0, 2 * B_WIN_COLS - 2)
    pick_c = ((dc[..., None] == np.arange(2 * B_WIN_COLS - 1)) & col_ok[..., None]).astype(np.float32)
    blocks = jnp.einsum("hdk,xyk->hdxy", rpb.astype(jnp.float32) * LOG2_E, pick_c,
                        precision=lax.Precision.HIGHEST)
    blocks = jnp.where(col_ok[None, None], blocks - shift, NEG)
    blocks = jnp.concatenate([blocks, jnp.full((B_HEADS, 1, GRID_W, GRID_W), NEG, jnp.float32)], axis=1)
    return jnp.concatenate([blocks, blocks], axis=-1)


def _mixer_b_block_index(rows):
    first_query_row = np.array([0, B_WIN_ROWS // 2, rows - B_UNIT_ROWS])
    i = first_query_row[:, None, None] + np.arange(B_UNIT_ROWS)[None, None, :]
    r0 = np.clip(first_query_row - B_WIN_ROWS // 2, 0, rows - B_SPAN_ROWS)
    ik = r0[:, None, None] + np.arange(B_SPAN_ROWS)[None, :, None]
    rs = np.clip(i - B_WIN_ROWS // 2, 0, rows - B_WIN_ROWS)
    row_ok = (ik >= rs) & (ik < rs + B_WIN_ROWS)
    return np.where(row_ok, ik - i + B_WIN_ROWS - 1, B_N_DR).astype(np.int32).reshape(-1)


def _rope_tables(S):
    rows = S // GRID_W
    inv = ROPE_THETA ** (-jnp.arange(0, ROPE_AXIS_DIM, 2, dtype=jnp.float32) / ROPE_AXIS_DIM)
    d = np.arange(LANES) % HEAD_DIM
    inv_lane = inv[d % (ROPE_AXIS_DIM // 2)][None, :]
    is_col = (d >= ROPE_AXIS_DIM)[None, None, :]
    first = ((d % ROPE_AXIS_DIM) < ROPE_AXIS_DIM // 2)[None, :]
    tables = []
    for n in (rows, GRID_W):
        ang = jnp.arange(n, dtype=jnp.float32)[:, None] * inv_lane
        sin = jnp.sin(ang)
        tables.append((jnp.cos(ang), jnp.where(first, -sin, 0.0), jnp.where(first, 0.0, sin)))
    return tuple(jnp.where(is_col, by_col[None, :, :], by_row[:, None, :]).reshape(S, LANES)
                 for by_row, by_col in zip(*tables))


def _softmax_shift(gain_q, gain_k, bias_abs_max):
    bound = (HEAD_DIM * QK_SCALE * LOG2_E * ROUNDING_SLACK * jnp.max(jnp.abs(gain_q)) * jnp.max(jnp.abs(gain_k))
             + LOG2_E * bias_abs_max)
    small = bound <= SHIFT_LIMIT
    return jnp.stack([jnp.where(small, bound, 0.0), small.astype(jnp.float32)]).astype(jnp.float32)


def _block_diag_ones():
    i = np.arange(256) // HEAD_DIM
    return jnp.asarray(i[:, None] == i[None, :], _MXU_DTYPE)


def kernel(x, rel_bias_table, norm1, w_in, qk_gain, nat_rpb, w_br_a, w_br_b, w_br_c, w_o,
           norm2, w_up, w_down):
    B, S, D = x.shape
    depth = w_in.shape[0]
    M = B * S
    x2 = x.reshape(M, D)
    rope = _rope_tables(S)
    ones_bd = _block_diag_ones()
    a_bias = [_mixer_a_bias(rel_bias_table[:, g * A_HEADS:(g + 1) * A_HEADS], rate)
              for g, (_, rate) in enumerate(A_PATTERNS)]
    for l in range(depth):
        gq = qk_gain[l]
        tile = lambda g, n, s: jnp.tile(g * s, n)
        gain_row = jnp.concatenate([
            tile(gq[0], A_W // HEAD_DIM, QK_SCALE * LOG2_E), tile(gq[1], A_W // HEAD_DIM, 1.0),
            tile(gq[2], B_HEADS, QK_SCALE * LOG2_E), tile(gq[3], B_HEADS, 1.0),
            tile(gq[4], C_Q_HEADS, QK_SCALE * LOG2_E), tile(gq[5], C_KV_HEADS, 1.0)])[None, :]
        g1 = norm1[l][None, :]
        za0, za1, za2, qbt, kb, vbt, qc, kc, vc = _qkv_projection(
            x2, g1, w_in, l, gain_row, ones_bd, rope, B, S)
        za =(za0.reshape(B, 1, S, ZA_W), za1, za2)
        oa, lse = [], []
        for g, (_, rate) in enumerate(A_PATTERNS):
            table_g = rel_bias_table[:, g * A_HEADS:(g + 1) * A_HEADS]
            shift_a = _softmax_shift(gq[0], gq[1], jnp.max(jnp.abs(table_g)))
            o_g, l_g = _mixer_a_group(za[g], a_bias[g] - shift_a[0], shift_a, rate, B, S)
            oa.append(o_g)
            lse.append(l_g)
        shift_b = _softmax_shift(gq[2], gq[3], jnp.max(jnp.abs(nat_rpb[l])))
        ob = _mixer_b(qbt, kb, vbt, _mixer_b_blocks(nat_rpb[l], shift_b[0]), shift_b, B, S)
        oc = _mixer_c(qc, kc, vc, _softmax_shift(gq[4], gq[5], 0.0), B, S)
        x2 = _merge(x2, g1, oa, lse, ob, oc, w_in, w_br_a, w_br_b, w_br_c, w_o, l, B, S)
        x2 = _ffn(x2, norm2[l][None, :], w_up, w_down, l)
    return x2.reshape(B, S, D)
```

```python
import functools
import math

import jax
import jax.numpy as jnp
import numpy as np
from jax import lax
from jax.experimental import pallas as pl
from jax.experimental.pallas import tpu as pltpu

_MXU_DTYPE = jnp.bfloat16

D_MODEL = 1024
HEAD_DIM = 64
GRID_W = 64
RMS_EPS = 1e-6
NEG = -1e30
A_PATTERNS = ((128, 1), (512, 4), (2048, 16))
A_GROUPS = 3
A_HEADS = 4
A_W = A_GROUPS * A_HEADS * HEAD_DIM
A_OUT = A_HEADS * HEAD_DIM
A_RADIUS = 64
B_HEADS = 8
B_W = B_HEADS * HEAD_DIM
B_WIN_ROWS = 8
B_WIN_COLS = 16
C_Q_HEADS = 8
C_KV_HEADS = 2
C_GROUP = C_Q_HEADS // C_KV_HEADS
C_QW = C_Q_HEADS * HEAD_DIM
C_KVW = C_KV_HEADS * HEAD_DIM
ROPE_THETA = 10000.0
ROPE_AXIS_DIM = HEAD_DIM // 2
T5_BUCKETS = 32
T5_MAX_DIST = 1024
N_BRANCH = 3
D_FF = math.ceil(8 * D_MODEL / 3 / 256) * 256
QK_SCALE = HEAD_DIM ** -0.5
LOG2_E = math.log2(math.e)

VMEM_LIMIT_BYTES = 56 * 1024 * 1024
LANES = 128
BF16_SUBLANES = 16
VT_ROWS = HEAD_DIM + BF16_SUBLANES
SHIFT_LIMIT = 60.0
ROUNDING_SLACK = 1.02

_OFF = np.cumsum([0, A_W, A_W, A_W, B_W, B_W, B_W, C_QW, C_KVW, C_KVW]).tolist()
(_QA, _KA, _VA, _QB, _KB, _VB, _QC, _KC, _VC, _ZG) = _OFF

ZA_W = 3 * A_OUT
B_PAIRS = B_HEADS // 2


def _params(sem):
    return pltpu.CompilerParams(dimension_semantics=sem, vmem_limit_bytes=VMEM_LIMIT_BYTES)


def _const_spec(shape):
    nd = len(shape)
    return pl.BlockSpec(shape, lambda *_: (0,) * nd, pipeline_mode=pl.Buffered(1))


def _rms(x, g):
    return x * lax.rsqrt(jnp.mean(x * x, axis=-1, keepdims=True) + RMS_EPS) * g


def _dot(a, b):
    return jnp.dot(a, b, preferred_element_type=jnp.float32)


def _dot_nt(a, b):
    return lax.dot_general(a, b, (((1,), (1,)), ((), ())), preferred_element_type=jnp.float32)


def _ones_tail(width, dtype):
    return (lax.broadcasted_iota(jnp.int32, (VT_ROWS - HEAD_DIM, width), 0) == 0).astype(dtype)


PROJ_TM = 1024
PROJ_W_BLOCK = 1536
_N_NORM = 2 * A_W + 2 * B_W
_N_ROPE = C_QW + C_KVW


def _proj_kernel(x_ref, g_ref, w0_ref, w1_ref, w2_ref, gain_ref, ones_ref, cos_ref, s1_ref, s2_ref,
                 za0_ref, za1_ref, za2_ref, qbt_ref, kb_ref, vbt_ref, qc_ref, kc_ref, vc_ref, dil_ref):
    tm = x_ref.shape[0]
    w_refs = (w0_ref, w1_ref, w2_ref)
    h = _rms(x_ref[...], g_ref[...]).astype(_MXU_DTYPE)
    za_refs = (za0_ref, za1_ref, za2_ref)

    def head_norm(acc, c0, width):
        sq = (acc * acc).astype(_MXU_DTYPE)
        ms = _dot(sq, ones_ref[:width, :width]) * (1.0 / HEAD_DIM)
        return acc * lax.rsqrt(ms + RMS_EPS) * gain_ref[:, c0:c0 + width]

    def store_group(g, section, val):
        cols = slice(section * A_OUT, (section + 1) * A_OUT)
        rate = A_PATTERNS[g][1]
        if rate == 1:
            za0_ref[:, cols] = val.astype(za0_ref.dtype)
            return
        for j in range(A_OUT // LANES):
            dil_ref[j] = val[:, j * LANES:(j + 1) * LANES]
        for r in range(rate):
            picked = [dil_ref[j, pl.ds(r, tm // rate, stride=rate), :] for j in range(A_OUT // LANES)]
            za_refs[g][0, r, :, cols] = jnp.concatenate(picked, axis=1).astype(za_refs[g].dtype)

    def store_pairs(ref, t, val):
        vt = val.T.astype(ref.dtype)
        ref[0, 2 * t] = vt[:LANES]
        ref[0, 2 * t + 1] = vt[LANES:]

    def rotary(y, width):
        reps = width // LANES
        table = lambda ref: jnp.concatenate([ref[...]] * reps, axis=1) if reps > 1 else ref[...]
        return (y * table(cos_ref) + pltpu.roll(y, width - 16, 1) * table(s1_ref)
                + pltpu.roll(y, 16, 1) * table(s2_ref))

    def store_qc(t, acc, gain_at):
        yt = rotary(head_norm(acc, gain_at, 256), 256).T.astype(qc_ref.dtype)
        for j in range(256 // HEAD_DIM):
            qc_ref[0, 4 * t + j] = yt[j * HEAD_DIM:(j + 1) * HEAD_DIM]

    def store_kc_vc(acc, gain_at):
        y = rotary(head_norm(acc[:, :LANES], gain_at, LANES), LANES).astype(kc_ref.dtype)
        acc_t = acc[:, LANES:].T.astype(vc_ref.dtype)
        tail = _ones_tail(tm, vc_ref.dtype)
        for kv in range(C_KV_HEADS):
            kc_ref[0, kv] = y[:, kv * HEAD_DIM:(kv + 1) * HEAD_DIM]
            vc_ref[0, kv, :HEAD_DIM] = acc_t[kv * HEAD_DIM:(kv + 1) * HEAD_DIM]
            vc_ref[0, kv, HEAD_DIM:] = tail

    jobs = []
    for section, src in enumerate((_QA, _KA)):
        for g in range(A_GROUPS):
            gain_at = section * A_W + g * A_OUT
            jobs.append((src + g * A_OUT, A_OUT, lambda acc, gain_at=gain_at, g=g, section=section:
                         store_group(g, section, head_norm(acc, gain_at, A_OUT))))
    for t in range(B_W // 256):
        gain_at = 2 * A_W + t * 256
        jobs.append((_QB + t * 256, 256, lambda acc, gain_at=gain_at, t=t:
                     store_pairs(qbt_ref, t, head_norm(acc, gain_at, 256))))
    for t in range(B_W // 256):
        gain_at = 2 * A_W + B_W + t * 256
        def store_kb(acc, gain_at=gain_at, t=t):
            kb_ref[:, t * 256:(t + 1) * 256] = head_norm(acc, gain_at, 256).astype(kb_ref.dtype)
        jobs.append((_KB + t * 256, 256, store_kb))
    for t in range(C_QW // 256):
        jobs.append((_QC + t * 256, 256, lambda acc, t=t: store_qc(t, acc, _N_NORM + t * 256)))
    jobs.append((_KC, 2 * C_KVW, lambda acc: store_kc_vc(acc, _N_NORM + C_QW)))
    for g in range(A_GROUPS):
        jobs.append((_VA + g * A_OUT, A_OUT, lambda acc, g=g: store_group(g, 2, acc)))
    for t in range(B_W // 256):
        jobs.append((_VB + t * 256, 256, lambda acc, t=t: store_pairs(vbt_ref, t, acc)))

    def product(n):
        src, width, _ = jobs[n]
        blk, col = divmod(src, PROJ_W_BLOCK)
        return _dot(h, w_refs[blk][:, col:col + width].astype(_MXU_DTYPE))

    acc_next = product(0)
    for n, (_, _, consume) in enumerate(jobs):
        acc = acc_next
        if n + 1 < len(jobs):
            acc_next = product(n + 1)
        consume(acc)


def _qkv_projection(x2, g1, w_in, layer, gain_row, ones_bd, rope, B, S):
    M = x2.shape[0]
    tm = PROJ_TM
    per_seq = S // tm
    w_block = lambda j: pl.BlockSpec((None, D_MODEL, PROJ_W_BLOCK), lambda i: (layer, 0, j),
                                     pipeline_mode=pl.Buffered(1))
    cos_t, s1_t, s2_t = rope
    r1, r2 = A_PATTERNS[1][1], A_PATTERNS[2][1]
    row = lambda i: (i, 0)
    pos = lambda i: (i % per_seq, 0)
    hm = lambda i: (i // per_seq, 0, i % per_seq, 0)
    hm_t = lambda i: (i // per_seq, 0, 0, i % per_seq)
    cd = _MXU_DTYPE
    return pl.pallas_call(
        _proj_kernel,
        grid=(M // tm,),
        in_specs=[
            pl.BlockSpec((tm, D_MODEL), row),
            _const_spec((1, D_MODEL)),
            w_block(0), w_block(1), w_block(2),
            _const_spec((1, _N_NORM + _N_ROPE)),
            _const_spec((256, 256)),
            pl.BlockSpec((tm, LANES), pos),
            pl.BlockSpec((tm, LANES), pos),
            pl.BlockSpec((tm, LANES), pos),
        ],
        out_specs=[
            pl.BlockSpec((tm, ZA_W), row),
            pl.BlockSpec((1, r1, tm // r1, ZA_W), hm),
            pl.BlockSpec((1, r2, tm // r2, ZA_W), hm),
            pl.BlockSpec((1, B_PAIRS, LANES, tm), hm_t),
            pl.BlockSpec((tm, B_W), row),
            pl.BlockSpec((1, B_PAIRS, LANES, tm), hm_t),
            pl.BlockSpec((1, C_Q_HEADS, HEAD_DIM, tm), hm_t),
            pl.BlockSpec((1, C_KV_HEADS, tm, HEAD_DIM), hm),
            pl.BlockSpec((1, C_KV_HEADS, VT_ROWS, tm), hm_t),
        ],
        out_shape=[
            jax.ShapeDtypeStruct((M, ZA_W), cd),
            jax.ShapeDtypeStruct((B, r1, S // r1, ZA_W), cd),
            jax.ShapeDtypeStruct((B, r2, S // r2, ZA_W), cd),
            jax.ShapeDtypeStruct((B, B_PAIRS, LANES, S), cd),
            jax.ShapeDtypeStruct((M, B_W), cd),
            jax.ShapeDtypeStruct((B, B_PAIRS, LANES, S), cd),
            jax.ShapeDtypeStruct((B, C_Q_HEADS, HEAD_DIM, S), cd),
            jax.ShapeDtypeStruct((B, C_KV_HEADS, S, HEAD_DIM), cd),
            jax.ShapeDtypeStruct((B, C_KV_HEADS, VT_ROWS, S), cd),
        ],
        scratch_shapes=[pltpu.VMEM((A_OUT // LANES, tm, LANES), jnp.float32)],
        compiler_params=_params(("arbitrary",)),
        name="qkv_projection",
    )(x2, g1, w_in, w_in, w_in, gain_row, ones_bd, cos_t, s1_t, s2_t)


A_UNIT = 4 * A_RADIUS
A_SPAN = A_UNIT + 2 * A_RADIUS


def _mixer_a_kernel(q_ref, kp_ref, ko_ref, kn_ref, vp_ref, vo_ref, vn_ref, bias_ref, shift_ref,
                    o_ref, lse_ref, kw_ref, vw_ref, s_ref, *, tl, seq_len):
    l0 = pl.program_id(2) * tl
    kw_ref[0:A_RADIUS] = kp_ref[0, 0]
    kw_ref[A_RADIUS:A_RADIUS + tl] = ko_ref[0, 0]
    kw_ref[A_RADIUS + tl:] = kn_ref[0, 0]
    vw_ref[0:A_RADIUS] = vp_ref[0, 0]
    vw_ref[A_RADIUS:A_RADIUS + tl] = vo_ref[0, 0]
    vw_ref[A_RADIUS + tl:] = vn_ref[0, 0]
    even_q = lax.broadcasted_iota(jnp.int32, (A_UNIT, LANES), 1) < HEAD_DIM
    lane_k = lax.broadcasted_iota(jnp.int32, (A_SPAN, LANES), 1)

    n_units = tl // A_UNIT

    def scores(u, pair):
        off = pl.multiple_of(u * A_UNIT, A_UNIT)
        q = q_ref[0, 0, pl.ds(off, A_UNIT), pair * LANES:(pair + 1) * LANES]
        zero = jnp.zeros_like(q)
        q2 = jnp.concatenate([jnp.where(even_q, q, zero), jnp.where(even_q, zero, q)], axis=0)
        return _dot_nt(kw_ref[pl.ds(off, A_SPAN), pair * LANES:(pair + 1) * LANES], q2)

    s_ref[...] = scores(0, 0)

    def unit(fixed_shift, u, carry):
        off = pl.multiple_of(u * A_UNIT, A_UNIT)
        top_ok = l0 + off - A_RADIUS >= 0
        bot_ok = l0 + off + A_UNIT + A_RADIUS <= seq_len
        outs, lses = [], []
        s_next = s_ref[...]
        for pair in range(A_HEADS // 2):
            s_pair = s_next
            if pair + 1 < A_HEADS // 2:
                s_next = scores(u, pair + 1)
            else:
                s_ref[...] = scores(jnp.minimum(u + 1, n_units - 1), 0)
            v_slab = vw_ref[pl.ds(off, A_SPAN), pair * LANES:(pair + 1) * LANES]
            for odd in range(2):
                s = s_pair[:, odd * A_UNIT:(odd + 1) * A_UNIT] + bias_ref[2 * pair + odd]
                s = jnp.concatenate([jnp.where(top_ok, s[:A_RADIUS], NEG), s[A_RADIUS:A_SPAN - A_RADIUS],
                                     jnp.where(bot_ok, s[A_SPAN - A_RADIUS:], NEG)], axis=0)
                if fixed_shift:
                    m = shift_ref[0]
                    p = jnp.exp2(s).astype(_MXU_DTYPE)
                else:
                    m = jnp.max(s, axis=0, keepdims=True)
                    p = jnp.exp2(s - m).astype(_MXU_DTYPE)
                den_row = (1 - odd) * HEAD_DIM
                mine = (lane_k >= odd * HEAD_DIM) & (lane_k < (odd + 1) * HEAD_DIM)
                v_aug = jnp.where(mine, v_slab, (lane_k == den_row).astype(v_slab.dtype))
                acc = lax.dot_general(v_aug, p, (((0,), (0,)), ((), ())),
                                      preferred_element_type=jnp.float32)
                den = acc[den_row:den_row + 1]
                outs.append(acc[odd * HEAD_DIM:(odd + 1) * HEAD_DIM] / den)
                lses.append(jnp.broadcast_to(m + jnp.log2(den), (HEAD_DIM, A_UNIT)))
        o_ref[0, 0, pl.ds(off, A_UNIT), :] = jnp.concatenate(outs, axis=0).T
        lse_ref[0, 0, pl.ds(off, A_UNIT), :] = jnp.concatenate(lses, axis=0).T
        return carry

    small = shift_ref[1] > 0.5

    @pl.when(small)
    def _():
        lax.fori_loop(0, n_units, functools.partial(unit, True), 0, unroll=2)

    @pl.when(jnp.logical_not(small))
    def _():
        lax.fori_loop(0, n_units, functools.partial(unit, False), 0)


def _mixer_a_group(za_g, bias, shift, rate, B, S):
    L = S // rate
    tl = min(1024, L)
    nblk = L // A_RADIUS
    per = tl // A_RADIUS

    def own(section):
        return lambda b, r, l: (b, r, l, section)

    def prev(section):
        return lambda b, r, l: (b, r, jnp.maximum(l * per - 1, 0), section)

    def nxt(section):
        return lambda b, r, l: (b, r, jnp.minimum((l + 1) * per, nblk - 1), section)

    edge = (1, 1, A_RADIUS, A_OUT)
    full = (1, 1, tl, A_OUT)
    return pl.pallas_call(
        functools.partial(_mixer_a_kernel, tl=tl, seq_len=L),
        grid=(B, rate, L // tl),
        in_specs=[
            pl.BlockSpec(full, own(0)),
            pl.BlockSpec(edge, prev(1)), pl.BlockSpec(full, own(1)), pl.BlockSpec(edge, nxt(1)),
            pl.BlockSpec(edge, prev(2)), pl.BlockSpec(full, own(2)), pl.BlockSpec(edge, nxt(2)),
            _const_spec((A_HEADS, A_SPAN, A_UNIT)),
            pl.BlockSpec(memory_space=pltpu.SMEM),
        ],
        out_specs=[pl.BlockSpec(full, own(0)), pl.BlockSpec(full, own(0))],
        out_shape=[jax.ShapeDtypeStruct((B, rate, L, A_OUT), jnp.float32)] * 2,
        scratch_shapes=[pltpu.VMEM((tl + 2 * A_RADIUS, A_OUT), _MXU_DTYPE)] * 2
        + [pltpu.VMEM((A_SPAN, 2 * A_UNIT), jnp.float32)],
        compiler_params=_params(("arbitrary",) * 3),
        name=f"mixer_a_rate{rate}",
    )(za_g, za_g, za_g, za_g, za_g, za_g, za_g, bias, shift)


B_UNIT_ROWS = 4
B_UNIT = B_UNIT_ROWS * GRID_W
B_SPAN_ROWS = B_UNIT_ROWS + B_WIN_ROWS
B_SPAN = B_SPAN_ROWS * GRID_W
B_HALO = (B_WIN_ROWS // 2) * GRID_W
B_TILE_ROWS = 16
B_TILE = B_TILE_ROWS * GRID_W


def _mixer_b_kernel(q_ref, kp_ref, ko_ref, kn_ref, vp_ref, vo_ref, vn_ref, blocks_ref, sel_ref, shift_ref,
                    o_ref, kw_ref, vw_ref, bias_ref, s_ref, *, rows):
    @pl.when((pl.program_id(0) == 0) & (pl.program_id(1) == 0))
    def _():
        left = lax.broadcasted_iota(jnp.int32, (GRID_W, LANES), 1) < GRID_W

        def fill(n, carry):
            v, h = n // B_HEADS, n % B_HEADS
            for a in range(B_SPAN_ROWS):
                for gp in range(B_UNIT_ROWS // 2):
                    at = (v * B_SPAN_ROWS + a) * B_UNIT_ROWS + 2 * gp
                    tile = jnp.where(left, blocks_ref[h, sel_ref[at]], blocks_ref[h, sel_ref[at + 1]])
                    bias_ref[v, h, a * GRID_W:(a + 1) * GRID_W, gp * LANES:(gp + 1) * LANES] = tile
            return carry

        lax.fori_loop(0, 3 * B_HEADS, fill, 0)

    i0 = pl.program_id(1) * B_TILE_ROWS
    kw_ref[0:B_HALO] = kp_ref[0]
    kw_ref[B_HALO:B_HALO + B_TILE] = ko_ref[0]
    kw_ref[B_HALO + B_TILE:] = kn_ref[0]
    vw_ref[:, :, 0:B_HALO] = vp_ref[0]
    vw_ref[:, :, B_HALO:B_HALO + B_TILE] = vo_ref[0]
    vw_ref[:, :, B_HALO + B_TILE:] = vn_ref[0]
    tail = _ones_tail(B_SPAN, _MXU_DTYPE)
    upper = lax.broadcasted_iota(jnp.int32, (LANES, B_UNIT), 0) < HEAD_DIM

    n_units = B_TILE_ROWS // B_UNIT_ROWS

    def span_offset(u):
        r0 = jnp.clip(i0 + u * B_UNIT_ROWS - B_WIN_ROWS // 2, 0, rows - B_SPAN_ROWS)
        return pl.multiple_of((r0 - (i0 - B_WIN_ROWS // 2)) * GRID_W, LANES)

    def scores(u, pair):
        qt = q_ref[0, pair, :, pl.ds(pl.multiple_of(u * B_UNIT, B_UNIT), B_UNIT)]
        zero = jnp.zeros_like(qt)
        qt2 = jnp.concatenate([jnp.where(upper, qt, zero), jnp.where(upper, zero, qt)], axis=1)
        return _dot(kw_ref[pl.ds(span_offset(u), B_SPAN), pair * LANES:(pair + 1) * LANES], qt2)

    s_ref[...] = scores(0, 0)

    def unit(fixed_shift, u, carry):
        i0u = i0 + u * B_UNIT_ROWS
        off = span_offset(u)
        variant = jnp.where(i0u == 0, 0, jnp.where(i0u == rows - B_UNIT_ROWS, 2, 1))
        qoff = pl.multiple_of(u * B_UNIT, B_UNIT)
        outs = []
        s_next = s_ref[...]
        for pair in range(B_PAIRS):
            s_pair = s_next
            if pair + 1 < B_PAIRS:
                s_next = scores(u, pair + 1)
            else:
                s_ref[...] = scores(jnp.minimum(u + 1, n_units - 1), 0)
            for odd in range(2):
                s = s_pair[:, odd * B_UNIT:(odd + 1) * B_UNIT] + bias_ref[variant, 2 * pair + odd]
                if not fixed_shift:
                    s = s - jnp.max(s, axis=0, keepdims=True)
                p = jnp.exp2(s).astype(_MXU_DTYPE)
                vt = vw_ref[pair, odd * HEAD_DIM:(odd + 1) * HEAD_DIM, pl.ds(off, B_SPAN)]
                acc = _dot(jnp.concatenate([vt, tail], axis=0), p)
                outs.append(acc[:HEAD_DIM] / acc[HEAD_DIM:HEAD_DIM + 1])
        o_ref[0, pl.ds(qoff, B_UNIT), :] = jnp.concatenate(outs, axis=0).T.astype(o_ref.dtype)
        return carry

    small = shift_ref[1] > 0.5

    @pl.when(small)
    def _():
        lax.fori_loop(0, n_units, functools.partial(unit, True), 0, unroll=2)

    @pl.when(jnp.logical_not(small))
    def _():
        lax.fori_loop(0, n_units, functools.partial(unit, False), 0)


def _mixer_b(qbt, kb, vbt, blocks, shift, B, S):
    rows = S // GRID_W
    sel = jnp.asarray(_mixer_b_block_index(rows))
    nt = rows // B_TILE_ROWS
    per = B_TILE // B_HALO
    nh = S // B_HALO
    kb3 = kb.reshape(B, S, B_W)
    prev = lambda t: jnp.maximum(t * per - 1, 0)
    nxt = lambda t: jnp.minimum((t + 1) * per, nh - 1)
    k_edge, k_own = (1, B_HALO, B_W), (1, B_TILE, B_W)
    t_edge, t_own = (1, B_PAIRS, LANES, B_HALO), (1, B_PAIRS, LANES, B_TILE)
    o = pl.pallas_call(
        functools.partial(_mixer_b_kernel, rows=rows),
        grid=(B, nt),
        in_specs=[
            pl.BlockSpec(t_own, lambda b, t: (b, 0, 0, t)),
            pl.BlockSpec(k_edge, lambda b, t: (b, prev(t), 0)),
            pl.BlockSpec(k_own, lambda b, t: (b, t, 0)),
            pl.BlockSpec(k_edge, lambda b, t: (b, nxt(t), 0)),
            pl.BlockSpec(t_edge, lambda b, t: (b, 0, 0, prev(t))),
            pl.BlockSpec(t_own, lambda b, t: (b, 0, 0, t)),
            pl.BlockSpec(t_edge, lambda b, t: (b, 0, 0, nxt(t))),
            _const_spec(blocks.shape),
            pl.BlockSpec(memory_space=pltpu.SMEM),
            pl.BlockSpec(memory_space=pltpu.SMEM),
        ],
        out_specs=pl.BlockSpec(k_own, lambda b, t: (b, t, 0)),
        out_shape=jax.ShapeDtypeStruct((B, S, B_W), _MXU_DTYPE),
        scratch_shapes=[pltpu.VMEM((B_TILE + 2 * B_HALO, B_W), _MXU_DTYPE),
                        pltpu.VMEM((B_PAIRS, LANES, B_TILE + 2 * B_HALO), _MXU_DTYPE),
                        pltpu.VMEM((3, B_HEADS, B_SPAN, B_UNIT), jnp.float32),
                        pltpu.VMEM((B_SPAN, 2 * B_UNIT), jnp.float32)],
        compiler_params=_params(("arbitrary",) * 2),
        name="mixer_b",
    )(qbt, kb3, kb3, kb3, vbt, vbt, vbt, blocks, sel, shift)
    return o.reshape(B * S, B_W)


C_TQ = 2048
C_TK = 512
C_UNIT = 512
C_UNROLL = 2


def _mixer_c_kernel(q_ref, k_ref, vt_ref, shift_ref, o_ref, qcat_ref, m_ref, acc_ref, s_ref, *, n_kv):
    acc_ref[...] = jnp.zeros(acc_ref.shape, jnp.float32)
    for g in range(C_GROUP):
        qcat_ref[:, g * C_TQ:(g + 1) * C_TQ] = q_ref[0, g]
    n_units = C_GROUP * C_TQ // C_UNIT

    def keys(j):
        return k_ref[0, 0, pl.ds(pl.multiple_of(j * C_TK, C_TK), C_TK), :]

    def scores(k, u):
        return _dot(k, qcat_ref[:, u * C_UNIT:(u + 1) * C_UNIT])

    def sweep(update, unroll):
        s_ref[...] = scores(keys(0), 0)

        def step(j, carry):
            k = keys(j)
            vt = vt_ref[0, 0, :, pl.ds(pl.multiple_of(j * C_TK, C_TK), C_TK)]
            s_next = s_ref[...]
            for u in range(n_units):
                s = s_next
                if u + 1 < n_units:
                    s_next = scores(k, u + 1)
                else:
                    s_ref[...] = scores(keys(jnp.minimum(j + 1, n_kv - 1)), 0)
                update(s, vt, slice(u * C_UNIT, (u + 1) * C_UNIT))
            return carry

        lax.fori_loop(0, n_kv, step, 0, unroll=unroll)

    def fixed_shift(s, vt, cols):
        acc_ref[:, cols] += _dot(vt, jnp.exp2(s - shift_ref[0]).astype(_MXU_DTYPE))

    def running_max(s, vt, cols):
        m_prev = m_ref[:, cols]
        m_new = jnp.maximum(m_prev, jnp.max(s, axis=0, keepdims=True))
        alpha = jnp.exp2(m_prev - m_new)
        p = jnp.exp2(s - m_new).astype(_MXU_DTYPE)
        acc_ref[:, cols] = alpha * acc_ref[:, cols] + _dot(vt, p)
        m_ref[:, cols] = m_new

    small = shift_ref[1] > 0.5

    @pl.when(small)
    def _():
        sweep(fixed_shift, C_UNROLL)

    @pl.when(jnp.logical_not(small))
    def _():
        m_ref[...] = jnp.full(m_ref.shape, -jnp.inf, jnp.float32)
        sweep(running_max, 1)

    o_t = jnp.concatenate(
        [acc_ref[:HEAD_DIM, g * C_TQ:(g + 1) * C_TQ] / acc_ref[HEAD_DIM:HEAD_DIM + 1, g * C_TQ:(g + 1) * C_TQ]
         for g in range(C_GROUP)], axis=0)
    o_ref[0] = o_t.T.astype(o_ref.dtype)


def _mixer_c(qc_t, kc, vc_t, shift, B, S):
    o = pl.pallas_call(
        functools.partial(_mixer_c_kernel, n_kv=S // C_TK),
        grid=(B, C_KV_HEADS, S // C_TQ),
        in_specs=[
            pl.BlockSpec((1, C_GROUP, HEAD_DIM, C_TQ), lambda b, kv, i: (b, kv, 0, i)),
            pl.BlockSpec((1, 1, S, HEAD_DIM), lambda b, kv, i: (b, kv, 0, 0)),
            pl.BlockSpec((1, 1, VT_ROWS, S), lambda b, kv, i: (b, kv, 0, 0)),
            pl.BlockSpec(memory_space=pltpu.SMEM),
        ],
        out_specs=pl.BlockSpec((1, C_TQ, C_GROUP * HEAD_DIM), lambda b, kv, i: (b, i, kv)),
        out_shape=jax.ShapeDtypeStruct((B, S, C_QW), _MXU_DTYPE),
        scratch_shapes=[pltpu.VMEM((HEAD_DIM, C_GROUP * C_TQ), _MXU_DTYPE),
                        pltpu.VMEM((1, C_GROUP * C_TQ), jnp.float32),
                        pltpu.VMEM((VT_ROWS, C_GROUP * C_TQ), jnp.float32),
                        pltpu.VMEM((C_TK, C_UNIT), jnp.float32)],
        compiler_params=_params(("arbitrary",) * 3),
        name="mixer_c",
    )(qc_t, kc, vc_t, shift)
    return o.reshape(B * S, C_QW)


MERGE_TM = 512
MERGE_GATE_BLOCK = 512


def _merge_kernel(x_ref, g_ref, oa0_ref, oa1_ref, oa2_ref, l0_ref, l1_ref, l2_ref, ob_ref, oc_ref,
                  wg0_ref, wg1_ref, wg2_ref, wg3_ref, wg4_ref, wg5_ref, pa_ref, pb_ref, pc_ref, wo_ref,
                  out_ref, *scratch):
    tm = x_ref.shape[0]
    gate_refs = (wg0_ref, wg1_ref, wg2_ref, wg3_ref, wg4_ref, wg5_ref)
    cast = lambda ref: ref[...].astype(_MXU_DTYPE)

    def token_major(ref, scr):
        rate = ref.shape[1]
        halves = range(A_OUT // LANES)
        for r in range(rate):
            for j in halves:
                scr[j, pl.ds(r, tm // rate, stride=rate), :] = ref[0, r, :, j * LANES:(j + 1) * LANES]
        return jnp.concatenate([scr[j] for j in halves], axis=1)

    x = x_ref[...]
    h = _rms(x, g_ref[...]).astype(_MXU_DTYPE)
    n_parts = D_MODEL // MERGE_GATE_BLOCK
    branches = [None, _dot(ob_ref[...], cast(pb_ref)), _dot(oc_ref[...], cast(pc_ref))]
    gates = {(b, part): jax.nn.sigmoid(_dot(h, cast(gate_refs[b * n_parts + part])))
             for b in (1, 2, 0) for part in range(n_parts)}
    oa0, l0 = oa0_ref[...], l0_ref[...]
    oa1, l1 = token_major(oa1_ref, scratch[0]), token_major(l1_ref, scratch[1])
    oa2, l2 = token_major(oa2_ref, scratch[2]), token_major(l2_ref, scratch[3])
    mx = jnp.maximum(jnp.maximum(l0, l1), l2)
    w0, w1, w2 = jnp.exp2(l0 - mx), jnp.exp2(l1 - mx), jnp.exp2(l2 - mx)
    o_a = (w0 * oa0 + w1 * oa1 + w2 * oa2) / (w0 + w1 + w2)
    branches[0] = _dot(o_a.astype(_MXU_DTYPE), cast(pa_ref))
    parts = []
    for part in range(n_parts):
        cols = slice(part * MERGE_GATE_BLOCK, (part + 1) * MERGE_GATE_BLOCK)
        merged = sum(gates[b, part] * branches[b][:, cols] for b in range(1, N_BRANCH))
        parts.append((merged + gates[0, part] * branches[0][:, cols]).astype(_MXU_DTYPE))
    out_ref[...] = x + _dot(jnp.concatenate(parts, axis=1), cast(wo_ref))


def _merge(x2, g1, oa, lse, ob, oc, w_in, w_br_a, w_br_b, w_br_c, w_o, layer, B, S):
    M = x2.shape[0]
    tm = MERGE_TM
    per_seq = S // tm
    row = lambda i: (i, 0)
    tile = lambda w: pl.BlockSpec((tm, w), row)
    gate0 = _ZG // MERGE_GATE_BLOCK

    def layer_weight(w, block=None, at=0):
        block = w.shape[2] if block is None else block
        return pl.BlockSpec((None, w.shape[1], block), lambda i: (layer, 0, at), pipeline_mode=pl.Buffered(1))

    def dilated(rate):
        return pl.BlockSpec((1, rate, tm // rate, A_OUT), lambda i: (i // per_seq, 0, i % per_seq, 0))

    r1, r2 = A_PATTERNS[1][1], A_PATTERNS[2][1]
    return pl.pallas_call(
        _merge_kernel,
        grid=(M // tm,),
        in_specs=[tile(D_MODEL), _const_spec((1, D_MODEL)),
                  tile(A_OUT), dilated(r1), dilated(r2), tile(A_OUT), dilated(r1), dilated(r2),
                  tile(B_W), tile(C_QW),
                  *[layer_weight(w_in, MERGE_GATE_BLOCK, gate0 + j)
                    for j in range(N_BRANCH * D_MODEL // MERGE_GATE_BLOCK)],
                  layer_weight(w_br_a), layer_weight(w_br_b), layer_weight(w_br_c), layer_weight(w_o)],
        out_specs=tile(D_MODEL),
        out_shape=jax.ShapeDtypeStruct((M, D_MODEL), jnp.float32),
        scratch_shapes=[pltpu.VMEM((A_OUT // LANES, tm, LANES), jnp.float32)] * 4,
        compiler_params=_params(("arbitrary",)),
        name="gated_merge",
    )(x2, g1, oa[0].reshape(M, A_OUT), oa[1], oa[2], lse[0].reshape(M, A_OUT), lse[1], lse[2],
      ob, oc, *([w_in] * (N_BRANCH * D_MODEL // MERGE_GATE_BLOCK)), w_br_a, w_br_b, w_br_c, w_o)


FFN_TM = 512
FFN_CHUNK = 256
FFN_DOWN_CHUNK = D_FF // 2


def _ffn_kernel(x_ref, g_ref, wup_ref, wdown_ref, out_ref, act_ref):
    x = x_ref[...]
    xn = _rms(x, g_ref[...]).astype(_MXU_DTYPE)
    for c in range(D_FF // FFN_CHUNK):
        cs = slice(c * FFN_CHUNK, (c + 1) * FFN_CHUNK)
        a = _dot(xn, wup_ref[:, cs].astype(_MXU_DTYPE))
        b = _dot(xn, wup_ref[:, D_FF + c * FFN_CHUNK:D_FF + (c + 1) * FFN_CHUNK].astype(_MXU_DTYPE))
        act_ref[:, cs] = (a * jax.nn.sigmoid(a) * b).astype(act_ref.dtype)
    y = x
    for c in range(D_FF // FFN_DOWN_CHUNK):
        rows = slice(c * FFN_DOWN_CHUNK, (c + 1) * FFN_DOWN_CHUNK)
        y = y + _dot(act_ref[:, rows], wdown_ref[rows, :].astype(_MXU_DTYPE))
    out_ref[...] = y


def _ffn(x2, g2, w_up, w_down, layer):
    M = x2.shape[0]
    tm = FFN_TM
    row = lambda i: (i, 0)
    layer_weight = lambda w: pl.BlockSpec((None,) + w.shape[1:], lambda i: (layer, 0, 0),
                                          pipeline_mode=pl.Buffered(1))
    return pl.pallas_call(
        _ffn_kernel,
        grid=(M // tm,),
        in_specs=[pl.BlockSpec((tm, D_MODEL), row), _const_spec((1, D_MODEL)),
                  layer_weight(w_up), layer_weight(w_down)],
        out_specs=pl.BlockSpec((tm, D_MODEL), row),
        out_shape=jax.ShapeDtypeStruct((M, D_MODEL), jnp.float32),
        scratch_shapes=[pltpu.VMEM((tm, D_FF), _MXU_DTYPE)],
        compiler_params=_params(("arbitrary",)),
        name="swiglu_ffn",
    )(x2, g2, w_up, w_down)


def _t5_bucket(rel):
    half = T5_BUCKETS // 2
    max_exact = half // 2
    ret = jnp.where(rel > 0, half, 0)
    n = jnp.abs(rel)
    nf = jnp.maximum(n, 1).astype(jnp.float32)
    large = max_exact + (jnp.log(nf / max_exact) / math.log(T5_MAX_DIST / max_exact)
                         * (half - max_exact)).astype(jnp.int32)
    large = jnp.minimum(large, half - 1)
    return ret + jnp.where(n < max_exact, n, large)


def _mixer_a_bias(table_g, rate):
    n_off = A_SPAN + A_UNIT - 1
    steps = jnp.arange(n_off) - (A_UNIT - 1) - A_RADIUS
    onehot = (_t5_bucket(steps * rate)[:, None] == jnp.arange(T5_BUCKETS)).astype(jnp.float32)
    per_off = jnp.einsum("tb,bh->ht", onehot, table_g.astype(jnp.float32) * LOG2_E,
                         precision=lax.Precision.HIGHEST)
    period = n_off + 1
    per_off = jnp.pad(per_off, ((0, 0), (0, period - n_off)))
    hankel = jnp.tile(per_off, (1, A_SPAN + 1))[:, :A_SPAN * (period + 1)].reshape(A_HEADS, A_SPAN, period + 1)
    bias = hankel[:, :, :A_UNIT][:, :, ::-1]
    step = np.arange(A_SPAN)[:, None] - A_RADIUS - np.arange(A_UNIT)[None, :]
    return jnp.where((np.abs(step) <= A_RADIUS)[None], bias, NEG)


B_N_DR = 2 * B_WIN_ROWS - 1


def _mixer_b_blocks(rpb, shift):
    c = np.arange(GRID_W)
    c0 = np.clip(c - B_WIN_COLS // 2, 0, GRID_W - B_WIN_COLS)
    col_ok = (c[:, None] >= c0[None, :]) & (c[:, None] < c0[None, :] + B_WIN_COLS)
    dc = np.clip(c[:, None] - c[None, :] + B_WIN_COLS - 1, 0, 2 * B_WIN_COLS - 2)
    pick_c = ((dc[..., None] == np.arange(2 * B_WIN_COLS - 1)) & col_ok[..., None]).astype(np.float32)
    blocks = jnp.einsum("hdk,xyk->hdxy", rpb.astype(jnp.float32) * LOG2_E, pick_c,
                        precision=lax.Precision.HIGHEST)
    blocks = jnp.where(col_ok[None, None], blocks - shift, NEG)
    blocks = jnp.concatenate([blocks, jnp.full((B_HEADS, 1, GRID_W, GRID_W), NEG, jnp.float32)], axis=1)
    return jnp.concatenate([blocks, blocks], axis=-1)


def _mixer_b_block_index(rows):
    first_query_row = np.array([0, B_WIN_ROWS // 2, rows - B_UNIT_ROWS])
    i = first_query_row[:, None, None] + np.arange(B_UNIT_ROWS)[None, None, :]
    r0 = np.clip(first_query_row - B_WIN_ROWS // 2, 0, rows - B_SPAN_ROWS)
    ik = r0[:, None, None] + np.arange(B_SPAN_ROWS)[None, :, None]
    rs = np.clip(i - B_WIN_ROWS // 2, 0, rows - B_WIN_ROWS)
    row_ok = (ik >= rs) & (ik < rs + B_WIN_ROWS)
    return np.where(row_ok, ik - i + B_WIN_ROWS - 1, B_N_DR).astype(np.int32).reshape(-1)


def _rope_tables(S):
    rows = S // GRID_W
    inv = ROPE_THETA ** (-jnp.arange(0, ROPE_AXIS_DIM, 2, dtype=jnp.float32) / ROPE_AXIS_DIM)
    d = np.arange(LANES) % HEAD_DIM
    inv_lane = inv[d % (ROPE_AXIS_DIM // 2)][None, :]
    is_col = (d >= ROPE_AXIS_DIM)[None, None, :]
    first = ((d % ROPE_AXIS_DIM) < ROPE_AXIS_DIM // 2)[None, :]
    tables = []
    for n in (rows, GRID_W):
        ang = jnp.arange(n, dtype=jnp.float32)[:, None] * inv_lane
        sin = jnp.sin(ang)
        tables.append((jnp.cos(ang), jnp.where(first, -sin, 0.0), jnp.where(first, 0.0, sin)))
    return tuple(jnp.where(is_col, by_col[None, :, :], by_row[:, None, :]).reshape(S, LANES)
                 for by_row, by_col in zip(*tables))


def _softmax_shift(gain_q, gain_k, bias_abs_max):
    bound = (HEAD_DIM * QK_SCALE * LOG2_E * ROUNDING_SLACK * jnp.max(jnp.abs(gain_q)) * jnp.max(jnp.abs(gain_k))
             + LOG2_E * bias_abs_max)
    small = bound <= SHIFT_LIMIT
    return jnp.stack([jnp.where(small, bound, 0.0), small.astype(jnp.float32)]).astype(jnp.float32)


def _block_diag_ones():
    i = np.arange(256) // HEAD_DIM
    return jnp.asarray(i[:, None] == i[None, :], _MXU_DTYPE)


def kernel(x, rel_bias_table, norm1, w_in, qk_gain, nat_rpb, w_br_a, w_br_b, w_br_c, w_o,
           norm2, w_up, w_down):
    B, S, D = x.shape
    depth = w_in.shape[0]
    M = B * S
    x2 = x.reshape(M, D)
    rope = _rope_tables(S)
    ones_bd = _block_diag_ones()
    a_bias = [_mixer_a_bias(rel_bias_table[:, g * A_HEADS:(g + 1) * A_HEADS], rate)
              for g, (_, rate) in enumerate(A_PATTERNS)]
    for l in range(depth):
        gq = qk_gain[l]
        tile = lambda g, n, s: jnp.tile(g * s, n)
        gain_row = jnp.concatenate([
            tile(gq[0], A_W // HEAD_DIM, QK_SCALE * LOG2_E), tile(gq[1], A_W // HEAD_DIM, 1.0),
            tile(gq[2], B_HEADS, QK_SCALE * LOG2_E), tile(gq[3], B_HEADS, 1.0),
            tile(gq[4], C_Q_HEADS, QK_SCALE * LOG2_E), tile(gq[5], C_KV_HEADS, 1.0)])[None, :]
        g1 = norm1[l][None, :]
        za0, za1, za2, qbt, kb, vbt, qc, kc, vc = _qkv_projection(
            x2, g1, w_in, l, gain_row, ones_bd, rope, B, S)
        za =(za0.reshape(B, 1, S, ZA_W), za1, za2)
        oa, lse = [], []
        for g, (_, rate) in enumerate(A_PATTERNS):
            table_g = rel_bias_table[:, g * A_HEADS:(g + 1) * A_HEADS]
            shift_a = _softmax_shift(gq[0], gq[1], jnp.max(jnp.abs(table_g)))
            o_g, l_g = _mixer_a_group(za[g], a_bias[g] - shift_a[0], shift_a, rate, B, S)
            oa.append(o_g)
            lse.append(l_g)
        shift_b = _softmax_shift(gq[2], gq[3], jnp.max(jnp.abs(nat_rpb[l])))
        ob = _mixer_b(qbt, kb, vbt, _mixer_b_blocks(nat_rpb[l], shift_b[0]), shift_b, B, S)
        oc = _mixer_c(qc, kc, vc, _softmax_shift(gq[4], gq[5], 0.0), B, S)
        x2 = _merge(x2, g1, oa, lse, ob, oc, w_in, w_br_a, w_br_b, w_br_c, w_o, l, B, S)
        x2 = _ffn(x2, norm2[l][None, :], w_up, w_down, l)
    return x2.reshape(B, S, D)
```

```python
import functools
import math

import jax
import jax.numpy as jnp
import numpy as np
from jax import lax
from jax.experimental import pallas as pl
from jax.experimental.pallas import tpu as pltpu

_MXU_DTYPE = jnp.bfloat16

D_MODEL = 1024
HEAD_DIM = 64
GRID_W = 64
RMS_EPS = 1e-6
NEG = -1e30
A_PATTERNS = ((128, 1), (512, 4), (2048, 16))
A_GROUPS = 3
A_HEADS = 4
A_W = A_GROUPS * A_HEADS * HEAD_DIM
A_OUT = A_HEADS * HEAD_DIM
A_RADIUS = 64
B_HEADS = 8
B_W = B_HEADS * HEAD_DIM
B_WIN_ROWS = 8
B_WIN_COLS = 16
C_Q_HEADS = 8
C_KV_HEADS = 2
C_GROUP = C_Q_HEADS // C_KV_HEADS
C_QW = C_Q_HEADS * HEAD_DIM
C_KVW = C_KV_HEADS * HEAD_DIM
ROPE_THETA = 10000.0
ROPE_AXIS_DIM = HEAD_DIM // 2
T5_BUCKETS = 32
T5_MAX_DIST = 1024
N_BRANCH = 3
D_FF = math.ceil(8 * D_MODEL / 3 / 256) * 256
QK_SCALE = HEAD_DIM ** -0.5
LOG2_E = math.log2(math.e)

VMEM_LIMIT_BYTES = 56 * 1024 * 1024
LANES = 128
BF16_SUBLANES = 16
VT_ROWS = HEAD_DIM + BF16_SUBLANES
SHIFT_LIMIT = 60.0
ROUNDING_SLACK = 1.02

_OFF = np.cumsum([0, A_W, A_W, A_W, B_W, B_W, B_W, C_QW, C_KVW, C_KVW]).tolist()
(_QA, _KA, _VA, _QB, _KB, _VB, _QC, _KC, _VC, _ZG) = _OFF

ZA_W = 3 * A_OUT
B_PAIRS = B_HEADS // 2


def _params(sem):
    return pltpu.CompilerParams(dimension_semantics=sem, vmem_limit_bytes=VMEM_LIMIT_BYTES)


def _const_spec(shape):
    nd = len(shape)
    return pl.BlockSpec(shape, lambda *_: (0,) * nd, pipeline_mode=pl.Buffered(1))


def _rms(x, g):
    return x * lax.rsqrt(jnp.mean(x * x, axis=-1, keepdims=True) + RMS_EPS) * g


def _dot(a, b):
    return jnp.dot(a, b, preferred_element_type=jnp.float32)


def _dot_nt(a, b):
    return lax.dot_general(a, b, (((1,), (1,)), ((), ())), preferred_element_type=jnp.float32)


def _ones_tail(width, dtype):
    return (lax.broadcasted_iota(jnp.int32, (VT_ROWS - HEAD_DIM, width), 0) == 0).astype(dtype)


PROJ_TM = 1024
PROJ_W_BLOCK = 1536
_N_NORM = 2 * A_W + 2 * B_W
_N_ROPE = C_QW + C_KVW


def _proj_kernel(x_ref, g_ref, w0_ref, w1_ref, w2_ref, gain_ref, ones_ref, cos_ref, s1_ref, s2_ref,
                 za0_ref, za1_ref, za2_ref, qbt_ref, kb_ref, vbt_ref, qc_ref, kc_ref, vc_ref, dil_ref):
    tm = x_ref.shape[0]
    w_refs = (w0_ref, w1_ref, w2_ref)
    h = _rms(x_ref[...], g_ref[...]).astype(_MXU_DTYPE)
    za_refs = (za0_ref, za1_ref, za2_ref)

    def head_norm(acc, c0, width):
        sq = (acc * acc).astype(_MXU_DTYPE)
        ms = _dot(sq, ones_ref[:width, :width]) * (1.0 / HEAD_DIM)
        return acc * lax.rsqrt(ms + RMS_EPS) * gain_ref[:, c0:c0 + width]

    def store_group(g, section, val):
        cols = slice(section * A_OUT, (section + 1) * A_OUT)
        rate = A_PATTERNS[g][1]
        if rate == 1:
            za0_ref[:, cols] = val.astype(za0_ref.dtype)
            return
        for j in range(A_OUT // LANES):
            dil_ref[j] = val[:, j * LANES:(j + 1) * LANES]
        for r in range(rate):
            picked = [dil_ref[j, pl.ds(r, tm // rate, stride=rate), :] for j in range(A_OUT // LANES)]
            za_refs[g][0, r, :, cols] = jnp.concatenate(picked, axis=1).astype(za_refs[g].dtype)

    def store_pairs(ref, t, val):
        vt = val.T.astype(ref.dtype)
        ref[0, 2 * t] = vt[:LANES]
        ref[0, 2 * t + 1] = vt[LANES:]

    def rotary(y, width):
        reps = width // LANES
        table = lambda ref: jnp.concatenate([ref[...]] * reps, axis=1) if reps > 1 else ref[...]
        return (y * table(cos_ref) + pltpu.roll(y, width - 16, 1) * table(s1_ref)
                + pltpu.roll(y, 16, 1) * table(s2_ref))

    def store_qc(t, acc, gain_at):
        yt = rotary(head_norm(acc, gain_at, 256), 256).T.astype(qc_ref.dtype)
        for j in range(256 // HEAD_DIM):
            qc_ref[0, 4 * t + j] = yt[j * HEAD_DIM:(j + 1) * HEAD_DIM]

    def store_kc_vc(acc, gain_at):
        y = rotary(head_norm(acc[:, :LANES], gain_at, LANES), LANES).astype(kc_ref.dtype)
        acc_t = acc[:, LANES:].T.astype(vc_ref.dtype)
        tail = _ones_tail(tm, vc_ref.dtype)
        for kv in range(C_KV_HEADS):
            kc_ref[0, kv] = y[:, kv * HEAD_DIM:(kv + 1) * HEAD_DIM]
            vc_ref[0, kv, :HEAD_DIM] = acc_t[kv * HEAD_DIM:(kv + 1) * HEAD_DIM]
            vc_ref[0, kv, HEAD_DIM:] = tail

    jobs = []
    for section, src in enumerate((_QA, _KA)):
        for g in range(A_GROUPS):
            gain_at = section * A_W + g * A_OUT
            jobs.append((src + g * A_OUT, A_OUT, lambda acc, gain_at=gain_at, g=g, section=section:
                         store_group(g, section, head_norm(acc, gain_at, A_OUT))))
    for t in range(B_W // 256):
        gain_at = 2 * A_W + t * 256
        jobs.append((_QB + t * 256, 256, lambda acc, gain_at=gain_at, t=t:
                     store_pairs(qbt_ref, t, head_norm(acc, gain_at, 256))))
    for t in range(B_W // 256):
        gain_at = 2 * A_W + B_W + t * 256
        def store_kb(acc, gain_at=gain_at, t=t):
            kb_ref[:, t * 256:(t + 1) * 256] = head_norm(acc, gain_at, 256).astype(kb_ref.dtype)
        jobs.append((_KB + t * 256, 256, store_kb))
    for t in range(C_QW // 256):
        jobs.append((_QC + t * 256, 256, lambda acc, t=t: store_qc(t, acc, _N_NORM + t * 256)))
    jobs.append((_KC, 2 * C_KVW, lambda acc: store_kc_vc(acc, _N_NORM + C_QW)))
    for g in range(A_GROUPS):
        jobs.append((_VA + g * A_OUT, A_OUT, lambda acc, g=g: store_group(g, 2, acc)))
    for t in range(B_W // 256):
        jobs.append((_VB + t * 256, 256, lambda acc, t=t: store_pairs(vbt_ref, t, acc)))

    def product(n):
        src, width, _ = jobs[n]
        blk, col = divmod(src, PROJ_W_BLOCK)
        return _dot(h, w_refs[blk][:, col:col + width].astype(_MXU_DTYPE))

    acc_next = product(0)
    for n, (_, _, consume) in enumerate(jobs):
        acc = acc_next
        if n + 1 < len(jobs):
            acc_next = product(n + 1)
        consume(acc)


def _qkv_projection(x2, g1, w_in, layer, gain_row, ones_bd, rope, B, S):
    M = x2.shape[0]
    tm = PROJ_TM
    per_seq = S // tm
    w_block = lambda j: pl.BlockSpec((None, D_MODEL, PROJ_W_BLOCK), lambda i: (layer, 0, j),
                                     pipeline_mode=pl.Buffered(1))
    cos_t, s1_t, s2_t = rope
    r1, r2 = A_PATTERNS[1][1], A_PATTERNS[2][1]
    row = lambda i: (i, 0)
    pos = lambda i: (i % per_seq, 0)
    hm = lambda i: (i // per_seq, 0, i % per_seq, 0)
    hm_t = lambda i: (i // per_seq, 0, 0, i % per_seq)
    cd = _MXU_DTYPE
    return pl.pallas_call(
        _proj_kernel,
        grid=(M // tm,),
        in_specs=[
            pl.BlockSpec((tm, D_MODEL), row),
            _const_spec((1, D_MODEL)),
            w_block(0), w_block(1), w_block(2),
            _const_spec((1, _N_NORM + _N_ROPE)),
            _const_spec((256, 256)),
            pl.BlockSpec((tm, LANES), pos),
            pl.BlockSpec((tm, LANES), pos),
            pl.BlockSpec((tm, LANES), pos),
        ],
        out_specs=[
            pl.BlockSpec((tm, ZA_W), row),
            pl.BlockSpec((1, r1, tm // r1, ZA_W), hm),
            pl.BlockSpec((1, r2, tm // r2, ZA_W), hm),
            pl.BlockSpec((1, B_PAIRS, LANES, tm), hm_t),
            pl.BlockSpec((tm, B_W), row),
            pl.BlockSpec((1, B_PAIRS, LANES, tm), hm_t),
            pl.BlockSpec((1, C_Q_HEADS, HEAD_DIM, tm), hm_t),
            pl.BlockSpec((1, C_KV_HEADS, tm, HEAD_DIM), hm),
            pl.BlockSpec((1, C_KV_HEADS, VT_ROWS, tm), hm_t),
        ],
        out_shape=[
            jax.ShapeDtypeStruct((M, ZA_W), cd),
            jax.ShapeDtypeStruct((B, r1, S // r1, ZA_W), cd),
            jax.ShapeDtypeStruct((B, r2, S // r2, ZA_W), cd),
            jax.ShapeDtypeStruct((B, B_PAIRS, LANES, S), cd),
            jax.ShapeDtypeStruct((M, B_W), cd),
            jax.ShapeDtypeStruct((B, B_PAIRS, LANES, S), cd),
            jax.ShapeDtypeStruct((B, C_Q_HEADS, HEAD_DIM, S), cd),
            jax.ShapeDtypeStruct((B, C_KV_HEADS, S, HEAD_DIM), cd),
            jax.ShapeDtypeStruct((B, C_KV_HEADS, VT_ROWS, S), cd),
        ],
        scratch_shapes=[pltpu.VMEM((A_OUT // LANES, tm, LANES), jnp.float32)],
        compiler_params=_params(("arbitrary",)),
        name="qkv_projection",
    )(x2, g1, w_in, w_in, w_in, gain_row, ones_bd, cos_t, s1_t, s2_t)


A_UNIT = 4 * A_RADIUS
A_SPAN = A_UNIT + 2 * A_RADIUS


def _mixer_a_kernel(q_ref, kp_ref, ko_ref, kn_ref, vp_ref, vo_ref, vn_ref, bias_ref, shift_ref,
                    o_ref, lse_ref, kw_ref, vw_ref, s_ref, *, tl, seq_len):
    l0 = pl.program_id(2) * tl
    kw_ref[0:A_RADIUS] = kp_ref[0, 0]
    kw_ref[A_RADIUS:A_RADIUS + tl] = ko_ref[0, 0]
    kw_ref[A_RADIUS + tl:] = kn_ref[0, 0]
    vw_ref[0:A_RADIUS] = vp_ref[0, 0]
    vw_ref[A_RADIUS:A_RADIUS + tl] = vo_ref[0, 0]
    vw_ref[A_RADIUS + tl:] = vn_ref[0, 0]
    even_q = lax.broadcasted_iota(jnp.int32, (A_UNIT, LANES), 1) < HEAD_DIM
    lane_k = lax.broadcasted_iota(jnp.int32, (A_SPAN, LANES), 1)

    n_units = tl // A_UNIT

    def scores(u, pair):
        off = pl.multiple_of(u * A_UNIT, A_UNIT)
        q = q_ref[0, 0, pl.ds(off, A_UNIT), pair * LANES:(pair + 1) * LANES]
        zero = jnp.zeros_like(q)
        q2 = jnp.concatenate([jnp.where(even_q, q, zero), jnp.where(even_q, zero, q)], axis=0)
        return _dot_nt(kw_ref[pl.ds(off, A_SPAN), pair * LANES:(pair + 1) * LANES], q2)

    s_ref[...] = scores(0, 0)

    def unit(fixed_shift, u, carry):
        off = pl.multiple_of(u * A_UNIT, A_UNIT)
        top_ok = l0 + off - A_RADIUS >= 0
        bot_ok = l0 + off + A_UNIT + A_RADIUS <= seq_len
        outs, lses = [], []
        s_next = s_ref[...]
        for pair in range(A_HEADS // 2):
            s_pair = s_next
            if pair + 1 < A_HEADS // 2:
                s_next = scores(u, pair + 1)
            else:
                s_ref[...] = scores(jnp.minimum(u + 1, n_units - 1), 0)
            v_slab = vw_ref[pl.ds(off, A_SPAN), pair * LANES:(pair + 1) * LANES]
            for odd in range(2):
                s = s_pair[:, odd * A_UNIT:(odd + 1) * A_UNIT] + bias_ref[2 * pair + odd]
                s = jnp.concatenate([jnp.where(top_ok, s[:A_RADIUS], NEG), s[A_RADIUS:A_SPAN - A_RADIUS],
                                     jnp.where(bot_ok, s[A_SPAN - A_RADIUS:], NEG)], axis=0)
                if fixed_shift:
                    m = shift_ref[0]
                    p = jnp.exp2(s).astype(_MXU_DTYPE)
                else:
                    m = jnp.max(s, axis=0, keepdims=True)
                    p = jnp.exp2(s - m).astype(_MXU_DTYPE)
                den_row = (1 - odd) * HEAD_DIM
                mine = (lane_k >= odd * HEAD_DIM) & (lane_k < (odd + 1) * HEAD_DIM)
                v_aug = jnp.where(mine, v_slab, (lane_k == den_row).astype(v_slab.dtype))
                acc = lax.dot_general(v_aug, p, (((0,), (0,)), ((), ())),
                                      preferred_element_type=jnp.float32)
                den = acc[den_row:den_row + 1]
                outs.append(acc[odd * HEAD_DIM:(odd + 1) * HEAD_DIM] / den)
                lses.append(jnp.broadcast_to(m + jnp.log2(den), (HEAD_DIM, A_UNIT)))
        o_ref[0, 0, pl.ds(off, A_UNIT), :] = jnp.concatenate(outs, axis=0).T
        lse_ref[0, 0, pl.ds(off, A_UNIT), :] = jnp.concatenate(lses, axis=0).T
        return carry

    small = shift_ref[1] > 0.5

    @pl.when(small)
    def _():
        lax.fori_loop(0, n_units, functools.partial(unit, True), 0, unroll=2)

    @pl.when(jnp.logical_not(small))
    def _():
        lax.fori_loop(0, n_units, functools.partial(unit, False), 0)


def _mixer_a_group(za_g, bias, shift, rate, B, S):
    L = S // rate
    tl = min(1024, L)
    nblk = L // A_RADIUS
    per = tl // A_RADIUS

    def own(section):
        return lambda b, r, l: (b, r, l, section)

    def prev(section):
        return lambda b, r, l: (b, r, jnp.maximum(l * per - 1, 0), section)

    def nxt(section):
        return lambda b, r, l: (b, r, jnp.minimum((l + 1) * per, nblk - 1), section)

    edge = (1, 1, A_RADIUS, A_OUT)
    full = (1, 1, tl, A_OUT)
    return pl.pallas_call(
        functools.partial(_mixer_a_kernel, tl=tl, seq_len=L),
        grid=(B, rate, L // tl),
        in_specs=[
            pl.BlockSpec(full, own(0)),
            pl.BlockSpec(edge, prev(1)), pl.BlockSpec(full, own(1)), pl.BlockSpec(edge, nxt(1)),
            pl.BlockSpec(edge, prev(2)), pl.BlockSpec(full, own(2)), pl.BlockSpec(edge, nxt(2)),
            _const_spec((A_HEADS, A_SPAN, A_UNIT)),
            pl.BlockSpec(memory_space=pltpu.SMEM),
        ],
        out_specs=[pl.BlockSpec(full, own(0)), pl.BlockSpec(full, own(0))],
        out_shape=[jax.ShapeDtypeStruct((B, rate, L, A_OUT), jnp.float32)] * 2,
        scratch_shapes=[pltpu.VMEM((tl + 2 * A_RADIUS, A_OUT), _MXU_DTYPE)] * 2
        + [pltpu.VMEM((A_SPAN, 2 * A_UNIT), jnp.float32)],
        compiler_params=_params(("arbitrary",) * 3),
        name=f"mixer_a_rate{rate}",
    )(za_g, za_g, za_g, za_g, za_g, za_g, za_g, bias, shift)


B_UNIT_ROWS = 4
B_UNIT = B_UNIT_ROWS * GRID_W
B_SPAN_ROWS = B_UNIT_ROWS + B_WIN_ROWS
B_SPAN = B_SPAN_ROWS * GRID_W
B_HALO = (B_WIN_ROWS // 2) * GRID_W
B_TILE_ROWS = 32
B_TILE = B_TILE_ROWS * GRID_W


def _mixer_b_kernel(q_ref, kp_ref, ko_ref, kn_ref, vp_ref, vo_ref, vn_ref, blocks_ref, sel_ref, shift_ref,
                    o_ref, kw_ref, vw_ref, bias_ref, s_ref, *, rows):
    @pl.when((pl.program_id(0) == 0) & (pl.program_id(1) == 0))
    def _():
        left = lax.broadcasted_iota(jnp.int32, (GRID_W, LANES), 1) < GRID_W

        def fill(n, carry):
            v, h = n // B_HEADS, n % B_HEADS
            for a in range(B_SPAN_ROWS):
                for gp in range(B_UNIT_ROWS // 2):
                    at = (v * B_SPAN_ROWS + a) * B_UNIT_ROWS + 2 * gp
                    tile = jnp.where(left, blocks_ref[h, sel_ref[at]], blocks_ref[h, sel_ref[at + 1]])
                    bias_ref[v, h, a * GRID_W:(a + 1) * GRID_W, gp * LANES:(gp + 1) * LANES] = tile
            return carry

        lax.fori_loop(0, 3 * B_HEADS, fill, 0)

    i0 = pl.program_id(1) * B_TILE_ROWS
    kw_ref[0:B_HALO] = kp_ref[0]
    kw_ref[B_HALO:B_HALO + B_TILE] = ko_ref[0]
    kw_ref[B_HALO + B_TILE:] = kn_ref[0]
    vw_ref[:, :, 0:B_HALO] = vp_ref[0]
    vw_ref[:, :, B_HALO:B_HALO + B_TILE] = vo_ref[0]
    vw_ref[:, :, B_HALO + B_TILE:] = vn_ref[0]
    tail = _ones_tail(B_SPAN, _MXU_DTYPE)
    upper = lax.broadcasted_iota(jnp.int32, (LANES, B_UNIT), 0) < HEAD_DIM

    n_units = B_TILE_ROWS // B_UNIT_ROWS

    def span_offset(u):
        r0 = jnp.clip(i0 + u * B_UNIT_ROWS - B_WIN_ROWS // 2, 0, rows - B_SPAN_ROWS)
        return pl.multiple_of((r0 - (i0 - B_WIN_ROWS // 2)) * GRID_W, LANES)

    def scores(u, pair):
        qt = q_ref[0, pair, :, pl.ds(pl.multiple_of(u * B_UNIT, B_UNIT), B_UNIT)]
        zero = jnp.zeros_like(qt)
        qt2 = jnp.concatenate([jnp.where(upper, qt, zero), jnp.where(upper, zero, qt)], axis=1)
        return _dot(kw_ref[pl.ds(span_offset(u), B_SPAN), pair * LANES:(pair + 1) * LANES], qt2)

    s_ref[...] = scores(0, 0)

    def unit(fixed_shift, u, carry):
        i0u = i0 + u * B_UNIT_ROWS
        off = span_offset(u)
        variant = jnp.where(i0u == 0, 0, jnp.where(i0u == rows - B_UNIT_ROWS, 2, 1))
        qoff = pl.multiple_of(u * B_UNIT, B_UNIT)
        outs = []
        s_next = s_ref[...]
        for pair in range(B_PAIRS):
            s_pair = s_next
            if pair + 1 < B_PAIRS:
                s_next = scores(u, pair + 1)
            else:
                s_ref[...] = scores(jnp.minimum(u + 1, n_units - 1), 0)
            for odd in range(2):
                s = s_pair[:, odd * B_UNIT:(odd + 1) * B_UNIT] + bias_ref[variant, 2 * pair + odd]
                if not fixed_shift:
                    s = s - jnp.max(s, axis=0, keepdims=True)
                p = jnp.exp2(s).astype(_MXU_DTYPE)
                vt = vw_ref[pair, odd * HEAD_DIM:(odd + 1) * HEAD_DIM, pl.ds(off, B_SPAN)]
                acc = _dot(jnp.concatenate([vt, tail], axis=0), p)
                outs.append(acc[:HEAD_DIM] / acc[HEAD_DIM:HEAD_DIM + 1])
        o_ref[0, pl.ds(qoff, B_UNIT), :] = jnp.concatenate(outs, axis=0).T.astype(o_ref.dtype)
        return carry

    small = shift_ref[1] > 0.5

    @pl.when(small)
    def _():
        lax.fori_loop(0, n_units, functools.partial(unit, True), 0, unroll=2)

    @pl.when(jnp.logical_not(small))
    def _():
        lax.fori_loop(0, n_units, functools.partial(unit, False), 0)


def _mixer_b(qbt, kb, vbt, blocks, shift, B, S):
    rows = S // GRID_W
    sel = jnp.asarray(_mixer_b_block_index(rows))
    nt = rows // B_TILE_ROWS
    per = B_TILE // B_HALO
    nh = S // B_HALO
    kb3 = kb.reshape(B, S, B_W)
    prev = lambda t: jnp.maximum(t * per - 1, 0)
    nxt = lambda t: jnp.minimum((t + 1) * per, nh - 1)
    k_edge, k_own = (1, B_HALO, B_W), (1, B_TILE, B_W)
    t_edge, t_own = (1, B_PAIRS, LANES, B_HALO), (1, B_PAIRS, LANES, B_TILE)
    o = pl.pallas_call(
        functools.partial(_mixer_b_kernel, rows=rows),
        grid=(B, nt),
        in_specs=[
            pl.BlockSpec(t_own, lambda b, t: (b, 0, 0, t)),
            pl.BlockSpec(k_edge, lambda b, t: (b, prev(t), 0)),
            pl.BlockSpec(k_own, lambda b, t: (b, t, 0)),
            pl.BlockSpec(k_edge, lambda b, t: (b, nxt(t), 0)),
            pl.BlockSpec(t_edge, lambda b, t: (b, 0, 0, prev(t))),
            pl.BlockSpec(t_own, lambda b, t: (b, 0, 0, t)),
            pl.BlockSpec(t_edge, lambda b, t: (b, 0, 0, nxt(t))),
            _const_spec(blocks.shape),
            pl.BlockSpec(memory_space=pltpu.SMEM),
            pl.BlockSpec(memory_space=pltpu.SMEM),
        ],
        out_specs=pl.BlockSpec(k_own, lambda b, t: (b, t, 0)),
        out_shape=jax.ShapeDtypeStruct((B, S, B_W), _MXU_DTYPE),
        scratch_shapes=[pltpu.VMEM((B_TILE + 2 * B_HALO, B_W), _MXU_DTYPE),
                        pltpu.VMEM((B_PAIRS, LANES, B_TILE + 2 * B_HALO), _MXU_DTYPE),
                        pltpu.VMEM((3, B_HEADS, B_SPAN, B_UNIT), jnp.float32),
                        pltpu.VMEM((B_SPAN, 2 * B_UNIT), jnp.float32)],
        compiler_params=_params(("arbitrary",) * 2),
        name="mixer_b",
    )(qbt, kb3, kb3, kb3, vbt, vbt, vbt, blocks, sel, shift)
    return o.reshape(B * S, B_W)


C_TQ = 4096
C_TK = 512
C_UNIT = 512
C_UNROLL = 1


def _mixer_c_kernel(q_ref, k_ref, vt_ref, shift_ref, o_ref, qcat_ref, m_ref, acc_ref, s_ref, *, n_kv):
    acc_ref[...] = jnp.zeros(acc_ref.shape, jnp.float32)
    for g in range(C_GROUP):
        qcat_ref[:, g * C_TQ:(g + 1) * C_TQ] = q_ref[0, g]
    n_units = C_GROUP * C_TQ // C_UNIT

    def keys(j):
        return k_ref[0, 0, pl.ds(pl.multiple_of(j * C_TK, C_TK), C_TK), :]

    def scores(k, u):
        return _dot(k, qcat_ref[:, u * C_UNIT:(u + 1) * C_UNIT])

    def sweep(update, unroll):
        s_ref[...] = scores(keys(0), 0)

        def step(j, carry):
            k = keys(j)
            vt = vt_ref[0, 0, :, pl.ds(pl.multiple_of(j * C_TK, C_TK), C_TK)]
            s_next = s_ref[...]
            for u in range(n_units):
                s = s_next
                if u + 1 < n_units:
                    s_next = scores(k, u + 1)
                else:
                    s_ref[...] = scores(keys(jnp.minimum(j + 1, n_kv - 1)), 0)
                update(s, vt, slice(u * C_UNIT, (u + 1) * C_UNIT))
            return carry

        lax.fori_loop(0, n_kv, step, 0, unroll=unroll)

    def fixed_shift(s, vt, cols):
        acc_ref[:, cols] += _dot(vt, jnp.exp2(s - shift_ref[0]).astype(_MXU_DTYPE))

    def running_max(s, vt, cols):
        m_prev = m_ref[:, cols]
        m_new = jnp.maximum(m_prev, jnp.max(s, axis=0, keepdims=True))
        alpha = jnp.exp2(m_prev - m_new)
        p = jnp.exp2(s - m_new).astype(_MXU_DTYPE)
        acc_ref[:, cols] = alpha * acc_ref[:, cols] + _dot(vt, p)
        m_ref[:, cols] = m_new

    small = shift_ref[1] > 0.5

    @pl.when(small)
    def _():
        sweep(fixed_shift, C_UNROLL)

    @pl.when(jnp.logical_not(small))
    def _():
        m_ref[...] = jnp.full(m_ref.shape, -jnp.inf, jnp.float32)
        sweep(running_max, 1)

    o_t = jnp.concatenate(
        [acc_ref[:HEAD_DIM, g * C_TQ:(g + 1) * C_TQ] / acc_ref[HEAD_DIM:HEAD_DIM + 1, g * C_TQ:(g + 1) * C_TQ]
         for g in range(C_GROUP)], axis=0)
    o_ref[0] = o_t.T.astype(o_ref.dtype)


def _mixer_c(qc_t, kc, vc_t, shift, B, S):
    o = pl.pallas_call(
        functools.partial(_mixer_c_kernel, n_kv=S // C_TK),
        grid=(B, C_KV_HEADS, S // C_TQ),
        in_specs=[
            pl.BlockSpec((1, C_GROUP, HEAD_DIM, C_TQ), lambda b, kv, i: (b, kv, 0, i)),
            pl.BlockSpec((1, 1, S, HEAD_DIM), lambda b, kv, i: (b, kv, 0, 0)),
            pl.BlockSpec((1, 1, VT_ROWS, S), lambda b, kv, i: (b, kv, 0, 0)),
            pl.BlockSpec(memory_space=pltpu.SMEM),
        ],
        out_specs=pl.BlockSpec((1, C_TQ, C_GROUP * HEAD_DIM), lambda b, kv, i: (b, i, kv)),
        out_shape=jax.ShapeDtypeStruct((B, S, C_QW), _MXU_DTYPE),
        scratch_shapes=[pltpu.VMEM((HEAD_DIM, C_GROUP * C_TQ), _MXU_DTYPE),
                        pltpu.VMEM((1, C_GROUP * C_TQ), jnp.float32),
                        pltpu.VMEM((VT_ROWS, C_GROUP * C_TQ), jnp.float32),
                        pltpu.VMEM((C_TK, C_UNIT), jnp.float32)],
        compiler_params=_params(("arbitrary",) * 3),
        name="mixer_c",
    )(qc_t, kc, vc_t, shift)
    return o.reshape(B * S, C_QW)


MERGE_TM = 512
MERGE_GATE_BLOCK = 512


def _merge_kernel(x_ref, g_ref, oa0_ref, oa1_ref, oa2_ref, l0_ref, l1_ref, l2_ref, ob_ref, oc_ref,
                  wg0_ref, wg1_ref, wg2_ref, wg3_ref, wg4_ref, wg5_ref, pa_ref, pb_ref, pc_ref, wo_ref,
                  out_ref, *scratch):
    tm = x_ref.shape[0]
    gate_refs = (wg0_ref, wg1_ref, wg2_ref, wg3_ref, wg4_ref, wg5_ref)
    cast = lambda ref: ref[...].astype(_MXU_DTYPE)

    def token_major(ref, scr):
        rate = ref.shape[1]
        halves = range(A_OUT // LANES)
        for r in range(rate):
            for j in halves:
                scr[j, pl.ds(r, tm // rate, stride=rate), :] = ref[0, r, :, j * LANES:(j + 1) * LANES]
        return jnp.concatenate([scr[j] for j in halves], axis=1)

    x = x_ref[...]
    h = _rms(x, g_ref[...]).astype(_MXU_DTYPE)
    n_parts = D_MODEL // MERGE_GATE_BLOCK
    branches = [None, _dot(ob_ref[...], cast(pb_ref)), _dot(oc_ref[...], cast(pc_ref))]
    gates = {(b, part): jax.nn.sigmoid(_dot(h, cast(gate_refs[b * n_parts + part])))
             for b in (1, 2, 0) for part in range(n_parts)}
    oa0, l0 = oa0_ref[...], l0_ref[...]
    oa1, l1 = token_major(oa1_ref, scratch[0]), token_major(l1_ref, scratch[1])
    oa2, l2 = token_major(oa2_ref, scratch[2]), token_major(l2_ref, scratch[3])
    mx = jnp.maximum(jnp.maximum(l0, l1), l2)
    w0, w1, w2 = jnp.exp2(l0 - mx), jnp.exp2(l1 - mx), jnp.exp2(l2 - mx)
    o_a = (w0 * oa0 + w1 * oa1 + w2 * oa2) / (w0 + w1 + w2)
    branches[0] = _dot(o_a.astype(_MXU_DTYPE), cast(pa_ref))
    parts = []
    for part in range(n_parts):
        cols = slice(part * MERGE_GATE_BLOCK, (part + 1) * MERGE_GATE_BLOCK)
        merged = sum(gates[b, part] * branches[b][:, cols] for b in range(1, N_BRANCH))
        parts.append((merged + gates[0, part] * branches[0][:, cols]).astype(_MXU_DTYPE))
    out_ref[...] = x + _dot(jnp.concatenate(parts, axis=1), cast(wo_ref))


def _merge(x2, g1, oa, lse, ob, oc, w_in, w_br_a, w_br_b, w_br_c, w_o, layer, B, S):
    M = x2.shape[0]
    tm = MERGE_TM
    per_seq = S // tm
    row = lambda i: (i, 0)
    tile = lambda w: pl.BlockSpec((tm, w), row)
    gate0 = _ZG // MERGE_GATE_BLOCK

    def layer_weight(w, block=None, at=0):
        block = w.shape[2] if block is None else block
        return pl.BlockSpec((None, w.shape[1], block), lambda i: (layer, 0, at), pipeline_mode=pl.Buffered(1))

    def dilated(rate):
        return pl.BlockSpec((1, rate, tm // rate, A_OUT), lambda i: (i // per_seq, 0, i % per_seq, 0))

    r1, r2 = A_PATTERNS[1][1], A_PATTERNS[2][1]
    return pl.pallas_call(
        _merge_kernel,
        grid=(M // tm,),
        in_specs=[tile(D_MODEL), _const_spec((1, D_MODEL)),
                  tile(A_OUT), dilated(r1), dilated(r2), tile(A_OUT), dilated(r1), dilated(r2),
                  tile(B_W), tile(C_QW),
                  *[layer_weight(w_in, MERGE_GATE_BLOCK, gate0 + j)
                    for j in range(N_BRANCH * D_MODEL // MERGE_GATE_BLOCK)],
                  layer_weight(w_br_a), layer_weight(w_br_b), layer_weight(w_br_c), layer_weight(w_o)],
        out_specs=tile(D_MODEL),
        out_shape=jax.ShapeDtypeStruct((M, D_MODEL), jnp.float32),
        scratch_shapes=[pltpu.VMEM((A_OUT // LANES, tm, LANES), jnp.float32)] * 4,
        compiler_params=_params(("arbitrary",)),
        name="gated_merge",
    )(x2, g1, oa[0].reshape(M, A_OUT), oa[1], oa[2], lse[0].reshape(M, A_OUT), lse[1], lse[2],
      ob, oc, *([w_in] * (N_BRANCH * D_MODEL // MERGE_GATE_BLOCK)), w_br_a, w_br_b, w_br_c, w_o)


FFN_TM = 512
FFN_CHUNK = 256
FFN_DOWN_CHUNK = D_FF // 2


def _ffn_kernel(x_ref, g_ref, wup_ref, wdown_ref, out_ref, act_ref):
    x = x_ref[...]
    xn = _rms(x, g_ref[...]).astype(_MXU_DTYPE)
    for c in range(D_FF // FFN_CHUNK):
        cs = slice(c * FFN_CHUNK, (c + 1) * FFN_CHUNK)
        a = _dot(xn, wup_ref[:, cs].astype(_MXU_DTYPE))
        b = _dot(xn, wup_ref[:, D_FF + c * FFN_CHUNK:D_FF + (c + 1) * FFN_CHUNK].astype(_MXU_DTYPE))
        act_ref[:, cs] = (a * jax.nn.sigmoid(a) * b).astype(act_ref.dtype)
    y = x
    for c in range(D_FF // FFN_DOWN_CHUNK):
        rows = slice(c * FFN_DOWN_CHUNK, (c + 1) * FFN_DOWN_CHUNK)
        y = y + _dot(act_ref[:, rows], wdown_ref[rows, :].astype(_MXU_DTYPE))
    out_ref[...] = y


def _ffn(x2, g2, w_up, w_down, layer):
    M = x2.shape[0]
    tm = FFN_TM
    row = lambda i: (i, 0)
    layer_weight = lambda w: pl.BlockSpec((None,) + w.shape[1:], lambda i: (layer, 0, 0),
                                          pipeline_mode=pl.Buffered(1))
    return pl.pallas_call(
        _ffn_kernel,
        grid=(M // tm,),
        in_specs=[pl.BlockSpec((tm, D_MODEL), row), _const_spec((1, D_MODEL)),
                  layer_weight(w_up), layer_weight(w_down)],
        out_specs=pl.BlockSpec((tm, D_MODEL), row),
        out_shape=jax.ShapeDtypeStruct((M, D_MODEL), jnp.float32),
        scratch_shapes=[pltpu.VMEM((tm, D_FF), _MXU_DTYPE)],
        compiler_params=_params(("arbitrary",)),
        name="swiglu_ffn",
    )(x2, g2, w_up, w_down)


def _t5_bucket(rel):
    half = T5_BUCKETS // 2
    max_exact = half // 2
    ret = jnp.where(rel > 0, half, 0)
    n = jnp.abs(rel)
    nf = jnp.maximum(n, 1).astype(jnp.float32)
    large = max_exact + (jnp.log(nf / max_exact) / math.log(T5_MAX_DIST / max_exact)
                         * (half - max_exact)).astype(jnp.int32)
    large = jnp.minimum(large, half - 1)
    return ret + jnp.where(n < max_exact, n, large)


def _mixer_a_bias(table_g, rate):
    j = jnp.arange(A_SPAN)[:, None]
    i = jnp.arange(A_UNIT)[None, :]
    step = j - A_RADIUS - i
    onehot = (_t5_bucket(step * rate)[:, :, None] == jnp.arange(T5_BUCKETS)).astype(jnp.float32)
    bias = jnp.einsum("jib,bh->hji", onehot, table_g.astype(jnp.float32) * LOG2_E,
                      precision=lax.Precision.HIGHEST)
    return jnp.where((jnp.abs(step) <= A_RADIUS)[None], bias, NEG)


B_N_DR = 2 * B_WIN_ROWS - 1


def _mixer_b_blocks(rpb, shift):
    c = np.arange(GRID_W)
    c0 = np.clip(c - B_WIN_COLS // 2, 0, GRID_W - B_WIN_COLS)
    col_ok = (c[:, None] >= c0[None, :]) & (c[:, None] < c0[None, :] + B_WIN_COLS)
    dc = np.clip(c[:, None] - c[None, :] + B_WIN_COLS - 1, 0, 2 * B_WIN_COLS - 2)
    pick_c = ((dc[..., None] == np.arange(2 * B_WIN_COLS - 1)) & col_ok[..., None]).astype(np.float32)
    blocks = jnp.einsum("hdk,xyk->hdxy", rpb.astype(jnp.float32) * LOG2_E, pick_c,
                        precision=lax.Precision.HIGHEST)
    blocks = jnp.where(col_ok[None, None], blocks - shift, NEG)
    blocks = jnp.concatenate([blocks, jnp.full((B_HEADS, 1, GRID_W, GRID_W), NEG, jnp.float32)], axis=1)
    return jnp.concatenate([blocks, blocks], axis=-1)


def _mixer_b_block_index(rows):
    first_query_row = np.array([0, B_WIN_ROWS // 2, rows - B_UNIT_ROWS])
    i = first_query_row[:, None, None] + np.arange(B_UNIT_ROWS)[None, None, :]
    r0 = np.clip(first_query_row - B_WIN_ROWS // 2, 0, rows - B_SPAN_ROWS)
    ik = r0[:, None, None] + np.arange(B_SPAN_ROWS)[None, :, None]
    rs = np.clip(i - B_WIN_ROWS // 2, 0, rows - B_WIN_ROWS)
    row_ok = (ik >= rs) & (ik < rs + B_WIN_ROWS)
    return np.where(row_ok, ik - i + B_WIN_ROWS - 1, B_N_DR).astype(np.int32).reshape(-1)


def _rope_tables(S):
    rows = S // GRID_W
    inv = ROPE_THETA ** (-jnp.arange(0, ROPE_AXIS_DIM, 2, dtype=jnp.float32) / ROPE_AXIS_DIM)
    d = np.arange(LANES) % HEAD_DIM
    inv_lane = inv[d % (ROPE_AXIS_DIM // 2)][None, :]
    is_col = (d >= ROPE_AXIS_DIM)[None, None, :]
    first = ((d % ROPE_AXIS_DIM) < ROPE_AXIS_DIM // 2)[None, :]
    tables = []
    for n in (rows, GRID_W):
        ang = jnp.arange(n, dtype=jnp.float32)[:, None] * inv_lane
        sin = jnp.sin(ang)
        tables.append((jnp.cos(ang), jnp.where(first, -sin, 0.0), jnp.where(first, 0.0, sin)))
    return tuple(jnp.where(is_col, by_col[None, :, :], by_row[:, None, :]).reshape(S, LANES)
                 for by_row, by_col in zip(*tables))


def _softmax_shift(gain_q, gain_k, bias_abs_max):
    bound = (HEAD_DIM * QK_SCALE * LOG2_E * ROUNDING_SLACK * jnp.max(jnp.abs(gain_q)) * jnp.max(jnp.abs(gain_k))
             + LOG2_E * bias_abs_max)
    small = bound <= SHIFT_LIMIT
    return jnp.stack([jnp.where(small, bound, 0.0), small.astype(jnp.float32)]).astype(jnp.float32)


def _block_diag_ones():
    i = np.arange(256) // HEAD_DIM
    return jnp.asarray(i[:, None] == i[None, :], _MXU_DTYPE)


def kernel(x, rel_bias_table, norm1, w_in, qk_gain, nat_rpb, w_br_a, w_br_b, w_br_c, w_o,
           norm2, w_up, w_down):
    B, S, D = x.shape
    depth = w_in.shape[0]
    M = B * S
    x2 = x.reshape(M, D)
    rope = _rope_tables(S)
    ones_bd = _block_diag_ones()
    a_bias = [_mixer_a_bias(rel_bias_table[:, g * A_HEADS:(g + 1) * A_HEADS], rate)
              for g, (_, rate) in enumerate(A_PATTERNS)]
    for l in range(depth):
        gq = qk_gain[l]
        tile = lambda g, n, s: jnp.tile(g * s, n)
        gain_row = jnp.concatenate([
            tile(gq[0], A_W // HEAD_DIM, QK_SCALE * LOG2_E), tile(gq[1], A_W // HEAD_DIM, 1.0),
            tile(gq[2], B_HEADS, QK_SCALE * LOG2_E), tile(gq[3], B_HEADS, 1.0),
            tile(gq[4], C_Q_HEADS, QK_SCALE * LOG2_E), tile(gq[5], C_KV_HEADS, 1.0)])[None, :]
        g1 = norm1[l][None, :]
        za0, za1, za2, qbt, kb, vbt, qc, kc, vc = _qkv_projection(
            x2, g1, w_in, l, gain_row, ones_bd, rope, B, S)
        za =(za0.reshape(B, 1, S, ZA_W), za1, za2)
        oa, lse = [], []
        for g, (_, rate) in enumerate(A_PATTERNS):
            table_g = rel_bias_table[:, g * A_HEADS:(g + 1) * A_HEADS]
            shift_a = _softmax_shift(gq[0], gq[1], jnp.max(jnp.abs(table_g)))
            o_g, l_g = _mixer_a_group(za[g], a_bias[g] - shift_a[0], shift_a, rate, B, S)
            oa.append(o_g)
            lse.append(l_g)
        shift_b = _softmax_shift(gq[2], gq[3], jnp.max(jnp.abs(nat_rpb[l])))
        ob = _mixer_b(qbt, kb, vbt, _mixer_b_blocks(nat_rpb[l], shift_b[0]), shift_b, B, S)
        oc = _mixer_c(qc, kc, vc, _softmax_shift(gq[4], gq[5], 0.0), B, S)
        x2 = _merge(x2, g1, oa, lse, ob, oc, w_in, w_br_a, w_br_b, w_br_c, w_o, l, B, S)
        x2 = _ffn(x2, norm2[l][None, :], w_up, w_down, l)
    return x2.reshape(B, S, D)
```

```python
import functools
import math

import jax
import jax.numpy as jnp
import numpy as np
from jax import lax
from jax.experimental import pallas as pl
from jax.experimental.pallas import tpu as pltpu

_MXU_DTYPE = jnp.bfloat16

D_MODEL = 1024
HEAD_DIM = 64
GRID_W = 64
RMS_EPS = 1e-6
NEG = -1e30
A_PATTERNS = ((128, 1), (512, 4), (2048, 16))
A_GROUPS = 3
A_HEADS = 4
A_W = A_GROUPS * A_HEADS * HEAD_DIM
A_OUT = A_HEADS * HEAD_DIM
A_RADIUS = 64
B_HEADS = 8
B_W = B_HEADS * HEAD_DIM
B_WIN_ROWS = 8
B_WIN_COLS = 16
C_Q_HEADS = 8
C_KV_HEADS = 2
C_GROUP = C_Q_HEADS // C_KV_HEADS
C_QW = C_Q_HEADS * HEAD_DIM
C_KVW = C_KV_HEADS * HEAD_DIM
ROPE_THETA = 10000.0
ROPE_AXIS_DIM = HEAD_DIM // 2
T5_BUCKETS = 32
T5_MAX_DIST = 1024
N_BRANCH = 3
D_FF = math.ceil(8 * D_MODEL / 3 / 256) * 256
QK_SCALE = HEAD_DIM ** -0.5
LOG2_E = math.log2(math.e)

V7X_VMEM_BYTES = 64 * 1024 * 1024
VMEM_LIMIT_BYTES = V7X_VMEM_BYTES * 7 // 8
LANES = 128
BF16_SUBLANES = 16
MXU_COLS = 256
VT_ROWS = HEAD_DIM + BF16_SUBLANES
SHIFT_LIMIT = 60.0
ROUNDING_SLACK = 1.02

_OFF = np.cumsum([0, A_W, A_W, A_W, B_W, B_W, B_W, C_QW, C_KVW, C_KVW]).tolist()
(_QA, _KA, _VA, _QB, _KB, _VB, _QC, _KC, _VC, _ZG) = _OFF

ZA_W = 3 * A_OUT
B_PAIRS = B_HEADS // 2


def _params(sem):
    return pltpu.CompilerParams(dimension_semantics=sem, vmem_limit_bytes=VMEM_LIMIT_BYTES)


def _const_spec(shape):
    nd = len(shape)
    return pl.BlockSpec(shape, lambda *_: (0,) * nd, pipeline_mode=pl.Buffered(1))


def _rms(x, g):
    return x * lax.rsqrt(jnp.mean(x * x, axis=-1, keepdims=True) + RMS_EPS) * g


def _dot(a, b):
    return jnp.dot(a, b, preferred_element_type=jnp.float32)


def _dot_nt(a, b):
    return lax.dot_general(a, b, (((1,), (1,)), ((), ())), preferred_element_type=jnp.float32)


def _ones_tail(width, dtype):
    return (lax.broadcasted_iota(jnp.int32, (VT_ROWS - HEAD_DIM, width), 0) == 0).astype(dtype)


PROJ_TM = 1024
PROJ_TILE = MXU_COLS
PROJ_W_BLOCK = 1536
_N_NORM = 2 * A_W + 2 * B_W
_N_ROPE = C_QW + C_KVW


def _proj_kernel(x_ref, g_ref, w0_ref, w1_ref, w2_ref, gain_ref, ones_ref, cos_ref, s1_ref, s2_ref,
                 za0_ref, za1_ref, za2_ref, qbt_ref, kb_ref, vbt_ref, qc_ref, kc_ref, vc_ref, dil_ref):
    tm = x_ref.shape[0]
    w_refs = (w0_ref, w1_ref, w2_ref)
    h = _rms(x_ref[...], g_ref[...]).astype(_MXU_DTYPE)
    za_refs = (za0_ref, za1_ref, za2_ref)

    def head_norm(acc, c0, width):
        sq = (acc * acc).astype(_MXU_DTYPE)
        ms = _dot(sq, ones_ref[:width, :width]) * (1.0 / HEAD_DIM)
        return acc * lax.rsqrt(ms + RMS_EPS) * gain_ref[:, c0:c0 + width]

    def store_group(g, section, val):
        cols = slice(section * A_OUT, (section + 1) * A_OUT)
        rate = A_PATTERNS[g][1]
        if rate == 1:
            za0_ref[:, cols] = val.astype(za0_ref.dtype)
            return
        for j in range(A_OUT // LANES):
            dil_ref[j] = val[:, j * LANES:(j + 1) * LANES]
        for r in range(rate):
            picked = [dil_ref[j, pl.ds(r, tm // rate, stride=rate), :] for j in range(A_OUT // LANES)]
            za_refs[g][0, r, :, cols] = jnp.concatenate(picked, axis=1).astype(za_refs[g].dtype)

    def store_pairs(ref, t, val):
        vt = val.T.astype(ref.dtype)
        ref[0, 2 * t] = vt[:LANES]
        ref[0, 2 * t + 1] = vt[LANES:]

    def rotary(y, width):
        reps = width // LANES
        table = lambda ref: jnp.concatenate([ref[...]] * reps, axis=1) if reps > 1 else ref[...]
        return (y * table(cos_ref) + pltpu.roll(y, width - 16, 1) * table(s1_ref)
                + pltpu.roll(y, 16, 1) * table(s2_ref))

    def store_qc(t, acc, gain_at):
        yt = rotary(head_norm(acc, gain_at, PROJ_TILE), PROJ_TILE).T.astype(qc_ref.dtype)
        heads = PROJ_TILE // HEAD_DIM
        for j in range(heads):
            qc_ref[0, heads * t + j] = yt[j * HEAD_DIM:(j + 1) * HEAD_DIM]

    def store_kc_vc(acc, gain_at):
        y = rotary(head_norm(acc[:, :LANES], gain_at, LANES), LANES).astype(kc_ref.dtype)
        acc_t = acc[:, LANES:].T.astype(vc_ref.dtype)
        tail = _ones_tail(tm, vc_ref.dtype)
        for kv in range(C_KV_HEADS):
            kc_ref[0, kv] = y[:, kv * HEAD_DIM:(kv + 1) * HEAD_DIM]
            vc_ref[0, kv, :HEAD_DIM] = acc_t[kv * HEAD_DIM:(kv + 1) * HEAD_DIM]
            vc_ref[0, kv, HEAD_DIM:] = tail

    jobs = []
    for section, src in enumerate((_QA, _KA)):
        for g in range(A_GROUPS):
            gain_at = section * A_W + g * A_OUT
            jobs.append((src + g * A_OUT, A_OUT, lambda acc, gain_at=gain_at, g=g, section=section:
                         store_group(g, section, head_norm(acc, gain_at, A_OUT))))
    for t in range(B_W // PROJ_TILE):
        gain_at = 2 * A_W + t * PROJ_TILE
        jobs.append((_QB + t * PROJ_TILE, PROJ_TILE, lambda acc, gain_at=gain_at, t=t:
                     store_pairs(qbt_ref, t, head_norm(acc, gain_at, PROJ_TILE))))
    for t in range(B_W // PROJ_TILE):
        gain_at = 2 * A_W + B_W + t * PROJ_TILE
        def store_kb(acc, gain_at=gain_at, t=t):
            kb_ref[:, t * PROJ_TILE:(t + 1) * PROJ_TILE] = head_norm(acc, gain_at, PROJ_TILE).astype(kb_ref.dtype)
        jobs.append((_KB + t * PROJ_TILE, PROJ_TILE, store_kb))
    for t in range(C_QW // PROJ_TILE):
        jobs.append((_QC + t * PROJ_TILE, PROJ_TILE,
                     lambda acc, t=t: store_qc(t, acc, _N_NORM + t * PROJ_TILE)))
    jobs.append((_KC, 2 * C_KVW, lambda acc: store_kc_vc(acc, _N_NORM + C_QW)))
    for g in range(A_GROUPS):
        jobs.append((_VA + g * A_OUT, A_OUT, lambda acc, g=g: store_group(g, 2, acc)))
    for t in range(B_W // PROJ_TILE):
        jobs.append((_VB + t * PROJ_TILE, PROJ_TILE, lambda acc, t=t: store_pairs(vbt_ref, t, acc)))

    def product(n):
        src, width, _ = jobs[n]
        blk, col = divmod(src, PROJ_W_BLOCK)
        return _dot(h, w_refs[blk][:, col:col + width].astype(_MXU_DTYPE))

    acc_next = product(0)
    for n, (_, _, consume) in enumerate(jobs):
        acc = acc_next
        if n + 1 < len(jobs):
            acc_next = product(n + 1)
        consume(acc)


def _qkv_projection(x2, g1, w_in, layer, gain_row, ones_bd, rope, B, S):
    M = x2.shape[0]
    tm = PROJ_TM
    per_seq = S // tm
    w_block = lambda j: pl.BlockSpec((None, D_MODEL, PROJ_W_BLOCK), lambda i: (layer, 0, j),
                                     pipeline_mode=pl.Buffered(1))
    cos_t, s1_t, s2_t = rope
    r1, r2 = A_PATTERNS[1][1], A_PATTERNS[2][1]
    row = lambda i: (i, 0)
    pos = lambda i: (i % per_seq, 0)
    hm = lambda i: (i // per_seq, 0, i % per_seq, 0)
    hm_t = lambda i: (i // per_seq, 0, 0, i % per_seq)
    cd = _MXU_DTYPE
    return pl.pallas_call(
        _proj_kernel,
        grid=(M // tm,),
        in_specs=[
            pl.BlockSpec((tm, D_MODEL), row),
            _const_spec((1, D_MODEL)),
            w_block(0), w_block(1), w_block(2),
            _const_spec((1, _N_NORM + _N_ROPE)),
            _const_spec((PROJ_TILE, PROJ_TILE)),
            pl.BlockSpec((tm, LANES), pos),
            pl.BlockSpec((tm, LANES), pos),
            pl.BlockSpec((tm, LANES), pos),
        ],
        out_specs=[
            pl.BlockSpec((tm, ZA_W), row),
            pl.BlockSpec((1, r1, tm // r1, ZA_W), hm),
            pl.BlockSpec((1, r2, tm // r2, ZA_W), hm),
            pl.BlockSpec((1, B_PAIRS, LANES, tm), hm_t),
            pl.BlockSpec((tm, B_W), row),
            pl.BlockSpec((1, B_PAIRS, LANES, tm), hm_t),
            pl.BlockSpec((1, C_Q_HEADS, HEAD_DIM, tm), hm_t),
            pl.BlockSpec((1, C_KV_HEADS, tm, HEAD_DIM), hm),
            pl.BlockSpec((1, C_KV_HEADS, VT_ROWS, tm), hm_t),
        ],
        out_shape=[
            jax.ShapeDtypeStruct((M, ZA_W), cd),
            jax.ShapeDtypeStruct((B, r1, S // r1, ZA_W), cd),
            jax.ShapeDtypeStruct((B, r2, S // r2, ZA_W), cd),
            jax.ShapeDtypeStruct((B, B_PAIRS, LANES, S), cd),
            jax.ShapeDtypeStruct((M, B_W), cd),
            jax.ShapeDtypeStruct((B, B_PAIRS, LANES, S), cd),
            jax.ShapeDtypeStruct((B, C_Q_HEADS, HEAD_DIM, S), cd),
            jax.ShapeDtypeStruct((B, C_KV_HEADS, S, HEAD_DIM), cd),
            jax.ShapeDtypeStruct((B, C_KV_HEADS, VT_ROWS, S), cd),
        ],
        scratch_shapes=[pltpu.VMEM((A_OUT // LANES, tm, LANES), jnp.float32)],
        compiler_params=_params(("arbitrary",)),
        name="qkv_projection",
    )(x2, g1, w_in, w_in, w_in, gain_row, ones_bd, cos_t, s1_t, s2_t)


A_UNIT = 4 * A_RADIUS
A_SPAN = A_UNIT + 2 * A_RADIUS
A_TILE = 2048


def _mixer_a_kernel(q_ref, kp_ref, ko_ref, kn_ref, vp_ref, vo_ref, vn_ref, bias_ref, shift_ref,
                    o_ref, lse_ref, kw_ref, vw_ref, s_ref, *, tl, seq_len):
    l0 = pl.program_id(2) * tl
    kw_ref[0:A_RADIUS] = kp_ref[0, 0]
    kw_ref[A_RADIUS:A_RADIUS + tl] = ko_ref[0, 0]
    kw_ref[A_RADIUS + tl:] = kn_ref[0, 0]
    vw_ref[0:A_RADIUS] = vp_ref[0, 0]
    vw_ref[A_RADIUS:A_RADIUS + tl] = vo_ref[0, 0]
    vw_ref[A_RADIUS + tl:] = vn_ref[0, 0]
    even_q = lax.broadcasted_iota(jnp.int32, (A_UNIT, LANES), 1) < HEAD_DIM
    lane_k = lax.broadcasted_iota(jnp.int32, (A_SPAN, LANES), 1)

    n_units = tl // A_UNIT

    def scores(u, pair):
        off = pl.multiple_of(u * A_UNIT, A_UNIT)
        q = q_ref[0, 0, pl.ds(off, A_UNIT), pair * LANES:(pair + 1) * LANES]
        zero = jnp.zeros_like(q)
        q2 = jnp.concatenate([jnp.where(even_q, q, zero), jnp.where(even_q, zero, q)], axis=0)
        return _dot_nt(kw_ref[pl.ds(off, A_SPAN), pair * LANES:(pair + 1) * LANES], q2)

    s_ref[...] = scores(0, 0)

    def unit(fixed_shift, u, carry):
        off = pl.multiple_of(u * A_UNIT, A_UNIT)
        top_ok = l0 + off - A_RADIUS >= 0
        bot_ok = l0 + off + A_UNIT + A_RADIUS <= seq_len
        outs, lses = [], []
        s_next = s_ref[...]
        for pair in range(A_HEADS // 2):
            s_pair = s_next
            if pair + 1 < A_HEADS // 2:
                s_next = scores(u, pair + 1)
            else:
                s_ref[...] = scores(jnp.minimum(u + 1, n_units - 1), 0)
            v_slab = vw_ref[pl.ds(off, A_SPAN), pair * LANES:(pair + 1) * LANES]
            for odd in range(2):
                s = s_pair[:, odd * A_UNIT:(odd + 1) * A_UNIT] + bias_ref[2 * pair + odd]
                s = jnp.concatenate([jnp.where(top_ok, s[:A_RADIUS], NEG), s[A_RADIUS:A_SPAN - A_RADIUS],
                                     jnp.where(bot_ok, s[A_SPAN - A_RADIUS:], NEG)], axis=0)
                if fixed_shift:
                    m = shift_ref[0]
                    p = jnp.exp2(s).astype(_MXU_DTYPE)
                else:
                    m = jnp.max(s, axis=0, keepdims=True)
                    p = jnp.exp2(s - m).astype(_MXU_DTYPE)
                den_row = (1 - odd) * HEAD_DIM
                mine = (lane_k >= odd * HEAD_DIM) & (lane_k < (odd + 1) * HEAD_DIM)
                v_aug = jnp.where(mine, v_slab, (lane_k == den_row).astype(v_slab.dtype))
                acc = lax.dot_general(v_aug, p, (((0,), (0,)), ((), ())),
                                      preferred_element_type=jnp.float32)
                den = acc[den_row:den_row + 1]
                outs.append(acc[odd * HEAD_DIM:(odd + 1) * HEAD_DIM] / den)
                lses.append(jnp.broadcast_to(m + jnp.log2(den), (HEAD_DIM, A_UNIT)))
        o_ref[0, 0, pl.ds(off, A_UNIT), :] = jnp.concatenate(outs, axis=0).T
        lse_ref[0, 0, pl.ds(off, A_UNIT), :] = jnp.concatenate(lses, axis=0).T
        return carry

    small = shift_ref[1] > 0.5

    @pl.when(small)
    def _():
        lax.fori_loop(0, n_units, functools.partial(unit, True), 0, unroll=2)

    @pl.when(jnp.logical_not(small))
    def _():
        lax.fori_loop(0, n_units, functools.partial(unit, False), 0)


def _mixer_a_group(za_g, bias, shift, rate, B, S):
    L = S // rate
    tl = min(A_TILE, L)
    nblk = L // A_RADIUS
    per = tl // A_RADIUS

    def own(section):
        return lambda b, r, l: (b, r, l, section)

    def prev(section):
        return lambda b, r, l: (b, r, jnp.maximum(l * per - 1, 0), section)

    def nxt(section):
        return lambda b, r, l: (b, r, jnp.minimum((l + 1) * per, nblk - 1), section)

    edge = (1, 1, A_RADIUS, A_OUT)
    full = (1, 1, tl, A_OUT)
    return pl.pallas_call(
        functools.partial(_mixer_a_kernel, tl=tl, seq_len=L),
        grid=(B, rate, L // tl),
        in_specs=[
            pl.BlockSpec(full, own(0)),
            pl.BlockSpec(edge, prev(1)), pl.BlockSpec(full, own(1)), pl.BlockSpec(edge, nxt(1)),
            pl.BlockSpec(edge, prev(2)), pl.BlockSpec(full, own(2)), pl.BlockSpec(edge, nxt(2)),
            _const_spec((A_HEADS, A_SPAN, A_UNIT)),
            pl.BlockSpec(memory_space=pltpu.SMEM),
        ],
        out_specs=[pl.BlockSpec(full, own(0)), pl.BlockSpec(full, own(0))],
        out_shape=[jax.ShapeDtypeStruct((B, rate, L, A_OUT), jnp.float32)] * 2,
        scratch_shapes=[pltpu.VMEM((tl + 2 * A_RADIUS, A_OUT), _MXU_DTYPE)] * 2
        + [pltpu.VMEM((A_SPAN, 2 * A_UNIT), jnp.float32)],
        compiler_params=_params(("arbitrary",) * 3),
        name=f"mixer_a_rate{rate}",
    )(za_g, za_g, za_g, za_g, za_g, za_g, za_g, bias, shift)


B_UNIT_ROWS = 4
B_UNIT = B_UNIT_ROWS * GRID_W
B_SPAN_ROWS = B_UNIT_ROWS + B_WIN_ROWS
B_SPAN = B_SPAN_ROWS * GRID_W
B_HALO = (B_WIN_ROWS // 2) * GRID_W
B_TILE_ROWS = 32
B_TILE = B_TILE_ROWS * GRID_W


def _mixer_b_kernel(q_ref, kp_ref, ko_ref, kn_ref, vp_ref, vo_ref, vn_ref, blocks_ref, sel_ref, shift_ref,
                    o_ref, kw_ref, vw_ref, bias_ref, s_ref, *, rows):
    @pl.when((pl.program_id(0) == 0) & (pl.program_id(1) == 0))
    def _():
        left = lax.broadcasted_iota(jnp.int32, (GRID_W, LANES), 1) < GRID_W

        def fill(n, carry):
            v, h = n // B_HEADS, n % B_HEADS
            for a in range(B_SPAN_ROWS):
                for gp in range(B_UNIT_ROWS // 2):
                    at = (v * B_SPAN_ROWS + a) * B_UNIT_ROWS + 2 * gp
                    tile = jnp.where(left, blocks_ref[h, sel_ref[at]], blocks_ref[h, sel_ref[at + 1]])
                    bias_ref[v, h, a * GRID_W:(a + 1) * GRID_W, gp * LANES:(gp + 1) * LANES] = tile
            return carry

        lax.fori_loop(0, 3 * B_HEADS, fill, 0)

    i0 = pl.program_id(1) * B_TILE_ROWS
    kw_ref[0:B_HALO] = kp_ref[0]
    kw_ref[B_HALO:B_HALO + B_TILE] = ko_ref[0]
    kw_ref[B_HALO + B_TILE:] = kn_ref[0]
    vw_ref[:, :, 0:B_HALO] = vp_ref[0]
    vw_ref[:, :, B_HALO:B_HALO + B_TILE] = vo_ref[0]
    vw_ref[:, :, B_HALO + B_TILE:] = vn_ref[0]
    tail = _ones_tail(B_SPAN, _MXU_DTYPE)
    upper = lax.broadcasted_iota(jnp.int32, (LANES, B_UNIT), 0) < HEAD_DIM

    n_units = B_TILE_ROWS // B_UNIT_ROWS

    def span_offset(u):
        r0 = jnp.clip(i0 + u * B_UNIT_ROWS - B_WIN_ROWS // 2, 0, rows - B_SPAN_ROWS)
        return pl.multiple_of((r0 - (i0 - B_WIN_ROWS // 2)) * GRID_W, LANES)

    def scores(u, pair):
        qt = q_ref[0, pair, :, pl.ds(pl.multiple_of(u * B_UNIT, B_UNIT), B_UNIT)]
        zero = jnp.zeros_like(qt)
        qt2 = jnp.concatenate([jnp.where(upper, qt, zero), jnp.where(upper, zero, qt)], axis=1)
        return _dot(kw_ref[pl.ds(span_offset(u), B_SPAN), pair * LANES:(pair + 1) * LANES], qt2)

    s_ref[...] = scores(0, 0)

    def unit(fixed_shift, u, carry):
        i0u = i0 + u * B_UNIT_ROWS
        off = span_offset(u)
        variant = jnp.where(i0u == 0, 0, jnp.where(i0u == rows - B_UNIT_ROWS, 2, 1))
        qoff = pl.multiple_of(u * B_UNIT, B_UNIT)
        outs = []
        s_next = s_ref[...]
        for pair in range(B_PAIRS):
            s_pair = s_next
            if pair + 1 < B_PAIRS:
                s_next = scores(u, pair + 1)
            else:
                s_ref[...] = scores(jnp.minimum(u + 1, n_units - 1), 0)
            for odd in range(2):
                s = s_pair[:, odd * B_UNIT:(odd + 1) * B_UNIT] + bias_ref[variant, 2 * pair + odd]
                if not fixed_shift:
                    s = s - jnp.max(s, axis=0, keepdims=True)
                p = jnp.exp2(s).astype(_MXU_DTYPE)
                vt = vw_ref[pair, odd * HEAD_DIM:(odd + 1) * HEAD_DIM, pl.ds(off, B_SPAN)]
                acc = _dot(jnp.concatenate([vt, tail], axis=0), p)
                outs.append(acc[:HEAD_DIM] / acc[HEAD_DIM:HEAD_DIM + 1])
        o_ref[0, pl.ds(qoff, B_UNIT), :] = jnp.concatenate(outs, axis=0).T.astype(o_ref.dtype)
        return carry

    small = shift_ref[1] > 0.5

    @pl.when(small)
    def _():
        lax.fori_loop(0, n_units, functools.partial(unit, True), 0, unroll=4)

    @pl.when(jnp.logical_not(small))
    def _():
        lax.fori_loop(0, n_units, functools.partial(unit, False), 0)


def _mixer_b(qbt, kb, vbt, blocks, shift, B, S):
    rows = S // GRID_W
    sel = jnp.asarray(_mixer_b_block_index(rows))
    nt = rows // B_TILE_ROWS
    per = B_TILE // B_HALO
    nh = S // B_HALO
    kb3 = kb.reshape(B, S, B_W)
    prev = lambda t: jnp.maximum(t * per - 1, 0)
    nxt = lambda t: jnp.minimum((t + 1) * per, nh - 1)
    k_edge, k_own = (1, B_HALO, B_W), (1, B_TILE, B_W)
    t_edge, t_own = (1, B_PAIRS, LANES, B_HALO), (1, B_PAIRS, LANES, B_TILE)
    o = pl.pallas_call(
        functools.partial(_mixer_b_kernel, rows=rows),
        grid=(B, nt),
        in_specs=[
            pl.BlockSpec(t_own, lambda b, t: (b, 0, 0, t)),
            pl.BlockSpec(k_edge, lambda b, t: (b, prev(t), 0)),
            pl.BlockSpec(k_own, lambda b, t: (b, t, 0)),
            pl.BlockSpec(k_edge, lambda b, t: (b, nxt(t), 0)),
            pl.BlockSpec(t_edge, lambda b, t: (b, 0, 0, prev(t))),
            pl.BlockSpec(t_own, lambda b, t: (b, 0, 0, t)),
            pl.BlockSpec(t_edge, lambda b, t: (b, 0, 0, nxt(t))),
            _const_spec(blocks.shape),
            pl.BlockSpec(memory_space=pltpu.SMEM),
            pl.BlockSpec(memory_space=pltpu.SMEM),
        ],
        out_specs=pl.BlockSpec(k_own, lambda b, t: (b, t, 0)),
        out_shape=jax.ShapeDtypeStruct((B, S, B_W), _MXU_DTYPE),
        scratch_shapes=[pltpu.VMEM((B_TILE + 2 * B_HALO, B_W), _MXU_DTYPE),
                        pltpu.VMEM((B_PAIRS, LANES, B_TILE + 2 * B_HALO), _MXU_DTYPE),
                        pltpu.VMEM((3, B_HEADS, B_SPAN, B_UNIT), jnp.float32),
                        pltpu.VMEM((B_SPAN, 2 * B_UNIT), jnp.float32)],
        compiler_params=_params(("arbitrary",) * 2),
        name="mixer_b",
    )(qbt, kb3, kb3, kb3, vbt, vbt, vbt, blocks, sel, shift)
    return o.reshape(B * S, B_W)


C_TQ = 4096
C_TK = 512
C_UNIT = 512
C_UNROLL = 1


def _mixer_c_kernel(q_ref, k_ref, vt_ref, shift_ref, o_ref, qcat_ref, m_ref, acc_ref, s_ref, *, n_kv):
    acc_ref[...] = jnp.zeros(acc_ref.shape, jnp.float32)
    for g in range(C_GROUP):
        qcat_ref[:, g * C_TQ:(g + 1) * C_TQ] = q_ref[0, g]
    n_units = C_GROUP * C_TQ // C_UNIT

    def keys(j):
        return k_ref[0, 0, pl.ds(pl.multiple_of(j * C_TK, C_TK), C_TK), :]

    def scores(k, u):
        return _dot(k, qcat_ref[:, u * C_UNIT:(u + 1) * C_UNIT])

    def sweep(update, unroll):
        s_ref[...] = scores(keys(0), 0)

        def step(j, carry):
            k = keys(j)
            vt = vt_ref[0, 0, :, pl.ds(pl.multiple_of(j * C_TK, C_TK), C_TK)]
            s_next = s_ref[...]
            for u in range(n_units):
                s = s_next
                if u + 1 < n_units:
                    s_next = scores(k, u + 1)
                else:
                    s_ref[...] = scores(keys(jnp.minimum(j + 1, n_kv - 1)), 0)
                update(s, vt, slice(u * C_UNIT, (u + 1) * C_UNIT))
            return carry

        lax.fori_loop(0, n_kv, step, 0, unroll=unroll)

    def fixed_shift(s, vt, cols):
        acc_ref[:, cols] += _dot(vt, jnp.exp2(s - shift_ref[0]).astype(_MXU_DTYPE))

    def running_max(s, vt, cols):
        m_prev = m_ref[:, cols]
        m_new = jnp.maximum(m_prev, jnp.max(s, axis=0, keepdims=True))
        alpha = jnp.exp2(m_prev - m_new)
        p = jnp.exp2(s - m_new).astype(_MXU_DTYPE)
        acc_ref[:, cols] = alpha * acc_ref[:, cols] + _dot(vt, p)
        m_ref[:, cols] = m_new

    small = shift_ref[1] > 0.5

    @pl.when(small)
    def _():
        sweep(fixed_shift, C_UNROLL)

    @pl.when(jnp.logical_not(small))
    def _():
        m_ref[...] = jnp.full(m_ref.shape, -jnp.inf, jnp.float32)
        sweep(running_max, 1)

    o_t = jnp.concatenate(
        [acc_ref[:HEAD_DIM, g * C_TQ:(g + 1) * C_TQ] / acc_ref[HEAD_DIM:HEAD_DIM + 1, g * C_TQ:(g + 1) * C_TQ]
         for g in range(C_GROUP)], axis=0)
    o_ref[0] = o_t.T.astype(o_ref.dtype)


def _mixer_c(qc_t, kc, vc_t, shift, B, S):
    o = pl.pallas_call(
        functools.partial(_mixer_c_kernel, n_kv=S // C_TK),
        grid=(B, C_KV_HEADS, S // C_TQ),
        in_specs=[
            pl.BlockSpec((1, C_GROUP, HEAD_DIM, C_TQ), lambda b, kv, i: (b, kv, 0, i)),
            pl.BlockSpec((1, 1, S, HEAD_DIM), lambda b, kv, i: (b, kv, 0, 0)),
            pl.BlockSpec((1, 1, VT_ROWS, S), lambda b, kv, i: (b, kv, 0, 0)),
            pl.BlockSpec(memory_space=pltpu.SMEM),
        ],
        out_specs=pl.BlockSpec((1, C_TQ, C_GROUP * HEAD_DIM), lambda b, kv, i: (b, i, kv)),
        out_shape=jax.ShapeDtypeStruct((B, S, C_QW), _MXU_DTYPE),
        scratch_shapes=[pltpu.VMEM((HEAD_DIM, C_GROUP * C_TQ), _MXU_DTYPE),
                        pltpu.VMEM((1, C_GROUP * C_TQ), jnp.float32),
                        pltpu.VMEM((VT_ROWS, C_GROUP * C_TQ), jnp.float32),
                        pltpu.VMEM((C_TK, C_UNIT), jnp.float32)],
        compiler_params=_params(("arbitrary",) * 3),
        name="mixer_c",
    )(qc_t, kc, vc_t, shift)
    return o.reshape(B * S, C_QW)


MERGE_TM = 512
MERGE_GATE_BLOCK = 512


def _merge_kernel(x_ref, g_ref, oa0_ref, oa1_ref, oa2_ref, l0_ref, l1_ref, l2_ref, ob_ref, oc_ref,
                  wg0_ref, wg1_ref, wg2_ref, wg3_ref, wg4_ref, wg5_ref, pa_ref, pb_ref, pc_ref, wo_ref,
                  out_ref, *scratch):
    tm = x_ref.shape[0]
    gate_refs = (wg0_ref, wg1_ref, wg2_ref, wg3_ref, wg4_ref, wg5_ref)
    cast = lambda ref: ref[...].astype(_MXU_DTYPE)

    def token_major(ref, scr):
        rate = ref.shape[1]
        halves = range(A_OUT // LANES)
        for r in range(rate):
            for j in halves:
                scr[j, pl.ds(r, tm // rate, stride=rate), :] = ref[0, r, :, j * LANES:(j + 1) * LANES]
        return jnp.concatenate([scr[j] for j in halves], axis=1)

    x = x_ref[...]
    h = _rms(x, g_ref[...]).astype(_MXU_DTYPE)
    n_parts = D_MODEL // MERGE_GATE_BLOCK
    branches = [None, _dot(ob_ref[...], cast(pb_ref)), _dot(oc_ref[...], cast(pc_ref))]
    gates = {(b, part): jax.nn.sigmoid(_dot(h, cast(gate_refs[b * n_parts + part])))
             for b in (1, 2, 0) for part in range(n_parts)}
    oa0, l0 = oa0_ref[...], l0_ref[...]
    oa1, l1 = token_major(oa1_ref, scratch[0]), token_major(l1_ref, scratch[1])
    oa2, l2 = token_major(oa2_ref, scratch[2]), token_major(l2_ref, scratch[3])
    mx = jnp.maximum(jnp.maximum(l0, l1), l2)
    w0, w1, w2 = jnp.exp2(l0 - mx), jnp.exp2(l1 - mx), jnp.exp2(l2 - mx)
    o_a = (w0 * oa0 + w1 * oa1 + w2 * oa2) / (w0 + w1 + w2)
    branches[0] = _dot(o_a.astype(_MXU_DTYPE), cast(pa_ref))
    parts = []
    for part in range(n_parts):
        cols = slice(part * MERGE_GATE_BLOCK, (part + 1) * MERGE_GATE_BLOCK)
        merged = sum(gates[b, part] * branches[b][:, cols] for b in range(1, N_BRANCH))
        parts.append((merged + gates[0, part] * branches[0][:, cols]).astype(_MXU_DTYPE))
    out_ref[...] = x + _dot(jnp.concatenate(parts, axis=1), cast(wo_ref))


def _merge(x2, g1, oa, lse, ob, oc, w_in, w_br_a, w_br_b, w_br_c, w_o, layer, B, S):
    M = x2.shape[0]
    tm = MERGE_TM
    per_seq = S // tm
    row = lambda i: (i, 0)
    tile = lambda w: pl.BlockSpec((tm, w), row)
    gate0 = _ZG // MERGE_GATE_BLOCK

    def layer_weight(w, block=None, at=0):
        block = w.shape[2] if block is None else block
        return pl.BlockSpec((None, w.shape[1], block), lambda i: (layer, 0, at), pipeline_mode=pl.Buffered(1))

    def dilated(rate):
        return pl.BlockSpec((1, rate, tm // rate, A_OUT), lambda i: (i // per_seq, 0, i % per_seq, 0))

    r1, r2 = A_PATTERNS[1][1], A_PATTERNS[2][1]
    return pl.pallas_call(
        _merge_kernel,
        grid=(M // tm,),
        in_specs=[tile(D_MODEL), _const_spec((1, D_MODEL)),
                  tile(A_OUT), dilated(r1), dilated(r2), tile(A_OUT), dilated(r1), dilated(r2),
                  tile(B_W), tile(C_QW),
                  *[layer_weight(w_in, MERGE_GATE_BLOCK, gate0 + j)
                    for j in range(N_BRANCH * D_MODEL // MERGE_GATE_BLOCK)],
                  layer_weight(w_br_a), layer_weight(w_br_b), layer_weight(w_br_c), layer_weight(w_o)],
        out_specs=tile(D_MODEL),
        out_shape=jax.ShapeDtypeStruct((M, D_MODEL), jnp.float32),
        scratch_shapes=[pltpu.VMEM((A_OUT // LANES, tm, LANES), jnp.float32)] * 4,
        compiler_params=_params(("arbitrary",)),
        name="gated_merge",
    )(x2, g1, oa[0].reshape(M, A_OUT), oa[1], oa[2], lse[0].reshape(M, A_OUT), lse[1], lse[2],
      ob, oc, *([w_in] * (N_BRANCH * D_MODEL // MERGE_GATE_BLOCK)), w_br_a, w_br_b, w_br_c, w_o)


FFN_TM = 512
FFN_CHUNK = 256
FFN_DOWN_CHUNK = D_FF // 2


def _ffn_kernel(x_ref, g_ref, wup_ref, wdown_ref, out_ref, act_ref):
    x = x_ref[...]
    xn = _rms(x, g_ref[...]).astype(_MXU_DTYPE)
    for c in range(D_FF // FFN_CHUNK):
        cs = slice(c * FFN_CHUNK, (c + 1) * FFN_CHUNK)
        a = _dot(xn, wup_ref[:, cs].astype(_MXU_DTYPE))
        b = _dot(xn, wup_ref[:, D_FF + c * FFN_CHUNK:D_FF + (c + 1) * FFN_CHUNK].astype(_MXU_DTYPE))
        act_ref[:, cs] = (a * jax.nn.sigmoid(a) * b).astype(act_ref.dtype)
    y = x
    for c in range(D_FF // FFN_DOWN_CHUNK):
        rows = slice(c * FFN_DOWN_CHUNK, (c + 1) * FFN_DOWN_CHUNK)
        y = y + _dot(act_ref[:, rows], wdown_ref[rows, :].astype(_MXU_DTYPE))
    out_ref[...] = y


def _ffn(x2, g2, w_up, w_down, layer):
    M = x2.shape[0]
    tm = FFN_TM
    row = lambda i: (i, 0)
    layer_weight = lambda w: pl.BlockSpec((None,) + w.shape[1:], lambda i: (layer, 0, 0),
                                          pipeline_mode=pl.Buffered(1))
    return pl.pallas_call(
        _ffn_kernel,
        grid=(M // tm,),
        in_specs=[pl.BlockSpec((tm, D_MODEL), row), _const_spec((1, D_MODEL)),
                  layer_weight(w_up), layer_weight(w_down)],
        out_specs=pl.BlockSpec((tm, D_MODEL), row),
        out_shape=jax.ShapeDtypeStruct((M, D_MODEL), jnp.float32),
        scratch_shapes=[pltpu.VMEM((tm, D_FF), _MXU_DTYPE)],
        compiler_params=_params(("arbitrary",)),
        name="swiglu_ffn",
    )(x2, g2, w_up, w_down)


def _t5_bucket(rel):
    half = T5_BUCKETS // 2
    max_exact = half // 2
    ret = jnp.where(rel > 0, half, 0)
    n = jnp.abs(rel)
    nf = jnp.maximum(n, 1).astype(jnp.float32)
    large = max_exact + (jnp.log(nf / max_exact) / math.log(T5_MAX_DIST / max_exact)
                         * (half - max_exact)).astype(jnp.int32)
    large = jnp.minimum(large, half - 1)
    return ret + jnp.where(n < max_exact, n, large)


def _mixer_a_bias(table_g, rate):
    j = jnp.arange(A_SPAN)[:, None]
    i = jnp.arange(A_UNIT)[None, :]
    step = j - A_RADIUS - i
    onehot = (_t5_bucket(step * rate)[:, :, None] == jnp.arange(T5_BUCKETS)).astype(jnp.float32)
    bias = jnp.einsum("jib,bh->hji", onehot, table_g.astype(jnp.float32) * LOG2_E,
                      precision=lax.Precision.HIGHEST)
    return jnp.where((jnp.abs(step) <= A_RADIUS)[None], bias, NEG)


B_N_DR = 2 * B_WIN_ROWS - 1


def _mixer_b_blocks(rpb, shift):
    c = np.arange(GRID_W)
    c0 = np.clip(c - B_WIN_COLS // 2, 0, GRID_W - B_WIN_COLS)
    col_ok = (c[:, None] >= c0[None, :]) & (c[:, None] < c0[None, :] + B_WIN_COLS)
    dc = np.clip(c[:, None] - c[None, :] + B_WIN_COLS - 1, 0, 2 * B_WIN_COLS - 2)
    pick_c = ((dc[..., None] == np.arange(2 * B_WIN_COLS - 1)) & col_ok[..., None]).astype(np.float32)
    blocks = jnp.einsum("hdk,xyk->hdxy", rpb.astype(jnp.float32) * LOG2_E, pick_c,
                        precision=lax.Precision.HIGHEST)
    blocks = jnp.where(col_ok[None, None], blocks - shift, NEG)
    blocks = jnp.concatenate([blocks, jnp.full((B_HEADS, 1, GRID_W, GRID_W), NEG, jnp.float32)], axis=1)
    return jnp.concatenate([blocks, blocks], axis=-1)


def _mixer_b_block_index(rows):
    first_query_row = np.array([0, B_WIN_ROWS // 2, rows - B_UNIT_ROWS])
    i = first_query_row[:, None, None] + np.arange(B_UNIT_ROWS)[None, None, :]
    r0 = np.clip(first_query_row - B_WIN_ROWS // 2, 0, rows - B_SPAN_ROWS)
    ik = r0[:, None, None] + np.arange(B_SPAN_ROWS)[None, :, None]
    rs = np.clip(i - B_WIN_ROWS // 2, 0, rows - B_WIN_ROWS)
    row_ok = (ik >= rs) & (ik < rs + B_WIN_ROWS)
    return np.where(row_ok, ik - i + B_WIN_ROWS - 1, B_N_DR).astype(np.int32).reshape(-1)


def _rope_tables(S):
    rows = S // GRID_W
    inv = ROPE_THETA ** (-jnp.arange(0, ROPE_AXIS_DIM, 2, dtype=jnp.float32) / ROPE_AXIS_DIM)
    d = np.arange(LANES) % HEAD_DIM
    inv_lane = inv[d % (ROPE_AXIS_DIM // 2)][None, :]
    is_col = (d >= ROPE_AXIS_DIM)[None, None, :]
    first = ((d % ROPE_AXIS_DIM) < ROPE_AXIS_DIM // 2)[None, :]
    tables = []
    for n in (rows, GRID_W):
        ang = jnp.arange(n, dtype=jnp.float32)[:, None] * inv_lane
        sin = jnp.sin(ang)
        tables.append((jnp.cos(ang), jnp.where(first, -sin, 0.0), jnp.where(first, 0.0, sin)))
    return tuple(jnp.where(is_col, by_col[None, :, :], by_row[:, None, :]).reshape(S, LANES)
                 for by_row, by_col in zip(*tables))


def _softmax_shift(gain_q, gain_k, bias_abs_max):
    bound = (HEAD_DIM * QK_SCALE * LOG2_E * ROUNDING_SLACK * jnp.max(jnp.abs(gain_q)) * jnp.max(jnp.abs(gain_k))
             + LOG2_E * bias_abs_max)
    small = bound <= SHIFT_LIMIT
    return jnp.stack([jnp.where(small, bound, 0.0), small.astype(jnp.float32)]).astype(jnp.float32)


def _block_diag_ones():
    i = np.arange(PROJ_TILE) // HEAD_DIM
    return jnp.asarray(i[:, None] == i[None, :], _MXU_DTYPE)


def kernel(x, rel_bias_table, norm1, w_in, qk_gain, nat_rpb, w_br_a, w_br_b, w_br_c, w_o,
           norm2, w_up, w_down):
    B, S, D = x.shape
    depth = w_in.shape[0]
    M = B * S
    x2 = x.reshape(M, D)
    rope = _rope_tables(S)
    ones_bd = _block_diag_ones()
    a_bias = [_mixer_a_bias(rel_bias_table[:, g * A_HEADS:(g + 1) * A_HEADS], rate)
              for g, (_, rate) in enumerate(A_PATTERNS)]
    for l in range(depth):
        gq = qk_gain[l]
        tile = lambda g, n, s: jnp.tile(g * s, n)
        gain_row = jnp.concatenate([
            tile(gq[0], A_W // HEAD_DIM, QK_SCALE * LOG2_E), tile(gq[1], A_W // HEAD_DIM, 1.0),
            tile(gq[2], B_HEADS, QK_SCALE * LOG2_E), tile(gq[3], B_HEADS, 1.0),
            tile(gq[4], C_Q_HEADS, QK_SCALE * LOG2_E), tile(gq[5], C_KV_HEADS, 1.0)])[None, :]
        g1 = norm1[l][None, :]
        za0, za1, za2, qbt, kb, vbt, qc, kc, vc = _qkv_projection(
            x2, g1, w_in, l, gain_row, ones_bd, rope, B, S)
        za =(za0.reshape(B, 1, S, ZA_W), za1, za2)
        oa, lse = [], []
        for g, (_, rate) in enumerate(A_PATTERNS):
            table_g = rel_bias_table[:, g * A_HEADS:(g + 1) * A_HEADS]
            shift_a = _softmax_shift(gq[0], gq[1], jnp.max(jnp.abs(table_g)))
            o_g, l_g = _mixer_a_group(za[g], a_bias[g] - shift_a[0], shift_a, rate, B, S)
            oa.append(o_g)
            lse.append(l_g)
        shift_b = _softmax_shift(gq[2], gq[3], jnp.max(jnp.abs(nat_rpb[l])))
        ob = _mixer_b(qbt, kb, vbt, _mixer_b_blocks(nat_rpb[l], shift_b[0]), shift_b, B, S)
        oc = _mixer_c(qc, kc, vc, _softmax_shift(gq[4], gq[5], 0.0), B, S)
        x2 = _merge(x2, g1, oa, lse, ob, oc, w_in, w_br_a, w_br_b, w_br_c, w_o, l, B, S)
        x2 = _ffn(x2, norm2[l][None, :], w_up, w_down, l)
    return x2.reshape(B, S, D)
```

```python
import functools
import math

import jax
import jax.numpy as jnp
import numpy as np
from jax import lax
from jax.experimental import pallas as pl
from jax.experimental.pallas import tpu as pltpu

_MXU_DTYPE = jnp.bfloat16

D_MODEL = 1024
HEAD_DIM = 64
GRID_W = 64
RMS_EPS = 1e-6
NEG = -1e30
A_PATTERNS = ((128, 1), (512, 4), (2048, 16))
A_GROUPS = 3
A_HEADS = 4
A_W = A_GROUPS * A_HEADS * HEAD_DIM
A_OUT = A_HEADS * HEAD_DIM
A_RADIUS = 64
B_HEADS = 8
B_W = B_HEADS * HEAD_DIM
B_WIN_ROWS = 8
B_WIN_COLS = 16
C_Q_HEADS = 8
C_KV_HEADS = 2
C_GROUP = C_Q_HEADS // C_KV_HEADS
C_QW = C_Q_HEADS * HEAD_DIM
C_KVW = C_KV_HEADS * HEAD_DIM
ROPE_THETA = 10000.0
ROPE_AXIS_DIM = HEAD_DIM // 2
T5_BUCKETS = 32
T5_MAX_DIST = 1024
N_BRANCH = 3
D_FF = math.ceil(8 * D_MODEL / 3 / 256) * 256
QK_SCALE = HEAD_DIM ** -0.5
LOG2_E = math.log2(math.e)

V7X_VMEM_BYTES = 64 * 1024 * 1024
VMEM_LIMIT_BYTES = V7X_VMEM_BYTES * 7 // 8
LANES = 128
BF16_SUBLANES = 16
MXU_COLS = 256
VT_ROWS = HEAD_DIM + BF16_SUBLANES
SHIFT_LIMIT = 60.0
ROUNDING_SLACK = 1.02

_OFF = np.cumsum([0, A_W, A_W, A_W, B_W, B_W, B_W, C_QW, C_KVW, C_KVW]).tolist()
(_QA, _KA, _VA, _QB, _KB, _VB, _QC, _KC, _VC, _ZG) = _OFF

ZA_W = 3 * A_OUT
B_PAIRS = B_HEADS // 2


def _params(sem):
    return pltpu.CompilerParams(dimension_semantics=sem, vmem_limit_bytes=VMEM_LIMIT_BYTES)


def _const_spec(shape):
    nd = len(shape)
    return pl.BlockSpec(shape, lambda *_: (0,) * nd, pipeline_mode=pl.Buffered(1))


def _rms(x, g):
    return x * lax.rsqrt(jnp.mean(x * x, axis=-1, keepdims=True) + RMS_EPS) * g


def _dot(a, b):
    return jnp.dot(a, b, preferred_element_type=jnp.float32)


def _dot_nt(a, b):
    return lax.dot_general(a, b, (((1,), (1,)), ((), ())), preferred_element_type=jnp.float32)


def _ones_tail(width, dtype):
    return (lax.broadcasted_iota(jnp.int32, (VT_ROWS - HEAD_DIM, width), 0) == 0).astype(dtype)


PROJ_TM = 1024
PROJ_TILE = MXU_COLS
PROJ_W_BLOCK = 1536
_N_NORM = 2 * A_W + 2 * B_W
_N_ROPE = C_QW + C_KVW


def _proj_kernel(x_ref, g_ref, w0_ref, w1_ref, w2_ref, gain_ref, ones_ref, cos_ref, s1_ref, s2_ref,
                 za0_ref, za1_ref, za2_ref, qbt_ref, kb_ref, vbt_ref, qc_ref, kc_ref, vc_ref, dil_ref):
    tm = x_ref.shape[0]
    w_refs = (w0_ref, w1_ref, w2_ref)
    h = _rms(x_ref[...], g_ref[...]).astype(_MXU_DTYPE)
    za_refs = (za0_ref, za1_ref, za2_ref)

    def head_norm(acc, c0, width):
        sq = (acc * acc).astype(_MXU_DTYPE)
        ms = _dot(sq, ones_ref[:width, :width]) * (1.0 / HEAD_DIM)
        return acc * lax.rsqrt(ms + RMS_EPS) * gain_ref[:, c0:c0 + width]

    def store_group(g, section, val):
        cols = slice(section * A_OUT, (section + 1) * A_OUT)
        rate = A_PATTERNS[g][1]
        if rate == 1:
            za0_ref[:, cols] = val.astype(za0_ref.dtype)
            return
        for j in range(A_OUT // LANES):
            dil_ref[j] = val[:, j * LANES:(j + 1) * LANES]
        for r in range(rate):
            picked = [dil_ref[j, pl.ds(r, tm // rate, stride=rate), :] for j in range(A_OUT // LANES)]
            za_refs[g][0, r, :, cols] = jnp.concatenate(picked, axis=1).astype(za_refs[g].dtype)

    def store_pairs(ref, t, val):
        vt = val.T.astype(ref.dtype)
        ref[0, 2 * t] = vt[:LANES]
        ref[0, 2 * t + 1] = vt[LANES:]

    def rotary(y, width):
        reps = width // LANES
        table = lambda ref: jnp.concatenate([ref[...]] * reps, axis=1) if reps > 1 else ref[...]
        return (y * table(cos_ref) + pltpu.roll(y, width - 16, 1) * table(s1_ref)
                + pltpu.roll(y, 16, 1) * table(s2_ref))

    def store_qc(t, acc, gain_at):
        yt = rotary(head_norm(acc, gain_at, PROJ_TILE), PROJ_TILE).T.astype(qc_ref.dtype)
        heads = PROJ_TILE // HEAD_DIM
        for j in range(heads):
            qc_ref[0, heads * t + j] = yt[j * HEAD_DIM:(j + 1) * HEAD_DIM]

    def store_kc_vc(acc, gain_at):
        y = rotary(head_norm(acc[:, :LANES], gain_at, LANES), LANES).astype(kc_ref.dtype)
        acc_t = acc[:, LANES:].T.astype(vc_ref.dtype)
        tail = _ones_tail(tm, vc_ref.dtype)
        for kv in range(C_KV_HEADS):
            kc_ref[0, kv] = y[:, kv * HEAD_DIM:(kv + 1) * HEAD_DIM]
            vc_ref[0, kv, :HEAD_DIM] = acc_t[kv * HEAD_DIM:(kv + 1) * HEAD_DIM]
            vc_ref[0, kv, HEAD_DIM:] = tail

    jobs = []
    for section, src in enumerate((_QA, _KA)):
        for g in range(A_GROUPS):
            gain_at = section * A_W + g * A_OUT
            jobs.append((src + g * A_OUT, A_OUT, lambda acc, gain_at=gain_at, g=g, section=section:
                         store_group(g, section, head_norm(acc, gain_at, A_OUT))))
    for t in range(B_W // PROJ_TILE):
        gain_at = 2 * A_W + t * PROJ_TILE
        jobs.append((_QB + t * PROJ_TILE, PROJ_TILE, lambda acc, gain_at=gain_at, t=t:
                     store_pairs(qbt_ref, t, head_norm(acc, gain_at, PROJ_TILE))))
    for t in range(B_W // PROJ_TILE):
        gain_at = 2 * A_W + B_W + t * PROJ_TILE
        def store_kb(acc, gain_at=gain_at, t=t):
            kb_ref[:, t * PROJ_TILE:(t + 1) * PROJ_TILE] = head_norm(acc, gain_at, PROJ_TILE).astype(kb_ref.dtype)
        jobs.append((_KB + t * PROJ_TILE, PROJ_TILE, store_kb))
    for t in range(C_QW // PROJ_TILE):
        jobs.append((_QC + t * PROJ_TILE, PROJ_TILE,
                     lambda acc, t=t: store_qc(t, acc, _N_NORM + t * PROJ_TILE)))
    jobs.append((_KC, 2 * C_KVW, lambda acc: store_kc_vc(acc, _N_NORM + C_QW)))
    for g in range(A_GROUPS):
        jobs.append((_VA + g * A_OUT, A_OUT, lambda acc, g=g: store_group(g, 2, acc)))
    for t in range(B_W // PROJ_TILE):
        jobs.append((_VB + t * PROJ_TILE, PROJ_TILE, lambda acc, t=t: store_pairs(vbt_ref, t, acc)))

    def product(n):
        src, width, _ = jobs[n]
        blk, col = divmod(src, PROJ_W_BLOCK)
        return _dot(h, w_refs[blk][:, col:col + width].astype(_MXU_DTYPE))

    acc_next = product(0)
    for n, (_, _, consume) in enumerate(jobs):
        acc = acc_next
        if n + 1 < len(jobs):
            acc_next = product(n + 1)
        consume(acc)


def _qkv_projection(x2, g1, w_in, layer, gain_row, ones_bd, rope, B, S):
    M = x2.shape[0]
    tm = PROJ_TM
    per_seq = S // tm
    w_block = lambda j: pl.BlockSpec((None, D_MODEL, PROJ_W_BLOCK), lambda i: (layer, 0, j),
                                     pipeline_mode=pl.Buffered(1))
    cos_t, s1_t, s2_t = rope
    r1, r2 = A_PATTERNS[1][1], A_PATTERNS[2][1]
    row = lambda i: (i, 0)
    pos = lambda i: (i % per_seq, 0)
    hm = lambda i: (i // per_seq, 0, i % per_seq, 0)
    hm_t = lambda i: (i // per_seq, 0, 0, i % per_seq)
    cd = _MXU_DTYPE
    return pl.pallas_call(
        _proj_kernel,
        grid=(M // tm,),
        in_specs=[
            pl.BlockSpec((tm, D_MODEL), row),
            _const_spec((1, D_MODEL)),
            w_block(0), w_block(1), w_block(2),
            _const_spec((1, _N_NORM + _N_ROPE)),
            _const_spec((PROJ_TILE, PROJ_TILE)),
            pl.BlockSpec((tm, LANES), pos),
            pl.BlockSpec((tm, LANES), pos),
            pl.BlockSpec((tm, LANES), pos),
        ],
        out_specs=[
            pl.BlockSpec((tm, ZA_W), row),
            pl.BlockSpec((1, r1, tm // r1, ZA_W), hm),
            pl.BlockSpec((1, r2, tm // r2, ZA_W), hm),
            pl.BlockSpec((1, B_PAIRS, LANES, tm), hm_t),
            pl.BlockSpec((tm, B_W), row),
            pl.BlockSpec((1, B_PAIRS, LANES, tm), hm_t),
            pl.BlockSpec((1, C_Q_HEADS, HEAD_DIM, tm), hm_t),
            pl.BlockSpec((1, C_KV_HEADS, tm, HEAD_DIM), hm),
            pl.BlockSpec((1, C_KV_HEADS, VT_ROWS, tm), hm_t),
        ],
        out_shape=[
            jax.ShapeDtypeStruct((M, ZA_W), cd),
            jax.ShapeDtypeStruct((B, r1, S // r1, ZA_W), cd),
            jax.ShapeDtypeStruct((B, r2, S // r2, ZA_W), cd),
            jax.ShapeDtypeStruct((B, B_PAIRS, LANES, S), cd),
            jax.ShapeDtypeStruct((M, B_W), cd),
            jax.ShapeDtypeStruct((B, B_PAIRS, LANES, S), cd),
            jax.ShapeDtypeStruct((B, C_Q_HEADS, HEAD_DIM, S), cd),
            jax.ShapeDtypeStruct((B, C_KV_HEADS, S, HEAD_DIM), cd),
            jax.ShapeDtypeStruct((B, C_KV_HEADS, VT_ROWS, S), cd),
        ],
        scratch_shapes=[pltpu.VMEM((A_OUT // LANES, tm, LANES), jnp.float32)],
        compiler_params=_params(("arbitrary",)),
        name="qkv_projection",
    )(x2, g1, w_in, w_in, w_in, gain_row, ones_bd, cos_t, s1_t, s2_t)


A_UNIT = 4 * A_RADIUS
A_SPAN = A_UNIT + 2 * A_RADIUS
A_TILE = 2048


def _mixer_a_kernel(q_ref, kp_ref, ko_ref, kn_ref, vp_ref, vo_ref, vn_ref, bias_ref, shift_ref,
                    o_ref, lse_ref, kw_ref, vw_ref, s_ref, *, tl, seq_len):
    l0 = pl.program_id(2) * tl
    kw_ref[0:A_RADIUS] = kp_ref[0, 0]
    kw_ref[A_RADIUS:A_RADIUS + tl] = ko_ref[0, 0]
    kw_ref[A_RADIUS + tl:] = kn_ref[0, 0]
    vw_ref[0:A_RADIUS] = vp_ref[0, 0]
    vw_ref[A_RADIUS:A_RADIUS + tl] = vo_ref[0, 0]
    vw_ref[A_RADIUS + tl:] = vn_ref[0, 0]
    even_q = lax.broadcasted_iota(jnp.int32, (A_UNIT, LANES), 1) < HEAD_DIM
    lane_k = lax.broadcasted_iota(jnp.int32, (A_SPAN, LANES), 1)

    n_units = tl // A_UNIT

    def scores(u, pair):
        off = pl.multiple_of(u * A_UNIT, A_UNIT)
        q = q_ref[0, 0, pl.ds(off, A_UNIT), pair * LANES:(pair + 1) * LANES]
        zero = jnp.zeros_like(q)
        q2 = jnp.concatenate([jnp.where(even_q, q, zero), jnp.where(even_q, zero, q)], axis=0)
        return _dot_nt(kw_ref[pl.ds(off, A_SPAN), pair * LANES:(pair + 1) * LANES], q2)

    s_ref[...] = scores(0, 0)

    def unit(fixed_shift, u, carry):
        off = pl.multiple_of(u * A_UNIT, A_UNIT)
        top_ok = l0 + off - A_RADIUS >= 0
        bot_ok = l0 + off + A_UNIT + A_RADIUS <= seq_len
        outs, lses = [], []
        s_next = s_ref[...]
        for pair in range(A_HEADS // 2):
            s_pair = s_next
            if pair + 1 < A_HEADS // 2:
                s_next = scores(u, pair + 1)
            else:
                s_ref[...] = scores(jnp.minimum(u + 1, n_units - 1), 0)
            v_slab = vw_ref[pl.ds(off, A_SPAN), pair * LANES:(pair + 1) * LANES]
            for odd in range(2):
                s = s_pair[:, odd * A_UNIT:(odd + 1) * A_UNIT] + bias_ref[2 * pair + odd]
                s = jnp.concatenate([jnp.where(top_ok, s[:A_RADIUS], NEG), s[A_RADIUS:A_SPAN - A_RADIUS],
                                     jnp.where(bot_ok, s[A_SPAN - A_RADIUS:], NEG)], axis=0)
                if fixed_shift:
                    m = shift_ref[0]
                    p = jnp.exp2(s).astype(_MXU_DTYPE)
                else:
                    m = jnp.max(s, axis=0, keepdims=True)
                    p = jnp.exp2(s - m).astype(_MXU_DTYPE)
                den_row = (1 - odd) * HEAD_DIM
                mine = (lane_k >= odd * HEAD_DIM) & (lane_k < (odd + 1) * HEAD_DIM)
                v_aug = jnp.where(mine, v_slab, (lane_k == den_row).astype(v_slab.dtype))
                acc = lax.dot_general(v_aug, p, (((0,), (0,)), ((), ())),
                                      preferred_element_type=jnp.float32)
                den = acc[den_row:den_row + 1]
                outs.append(acc[odd * HEAD_DIM:(odd + 1) * HEAD_DIM] / den)
                lses.append(jnp.broadcast_to(m + jnp.log2(den), (HEAD_DIM, A_UNIT)))
        o_ref[0, 0, pl.ds(off, A_UNIT), :] = jnp.concatenate(outs, axis=0).T
        lse_ref[0, 0, pl.ds(off, A_UNIT), :] = jnp.concatenate(lses, axis=0).T
        return carry

    small = shift_ref[1] > 0.5

    @pl.when(small)
    def _():
        lax.fori_loop(0, n_units, functools.partial(unit, True), 0, unroll=min(4, n_units))

    @pl.when(jnp.logical_not(small))
    def _():
        lax.fori_loop(0, n_units, functools.partial(unit, False), 0)


def _mixer_a_group(za_g, bias, shift, rate, B, S):
    L = S // rate
    tl = min(A_TILE, L)
    nblk = L // A_RADIUS
    per = tl // A_RADIUS

    def own(section):
        return lambda b, r, l: (b, r, l, section)

    def prev(section):
        return lambda b, r, l: (b, r, jnp.maximum(l * per - 1, 0), section)

    def nxt(section):
        return lambda b, r, l: (b, r, jnp.minimum((l + 1) * per, nblk - 1), section)

    edge = (1, 1, A_RADIUS, A_OUT)
    full = (1, 1, tl, A_OUT)
    return pl.pallas_call(
        functools.partial(_mixer_a_kernel, tl=tl, seq_len=L),
        grid=(B, rate, L // tl),
        in_specs=[
            pl.BlockSpec(full, own(0)),
            pl.BlockSpec(edge, prev(1)), pl.BlockSpec(full, own(1)), pl.BlockSpec(edge, nxt(1)),
            pl.BlockSpec(edge, prev(2)), pl.BlockSpec(full, own(2)), pl.BlockSpec(edge, nxt(2)),
            _const_spec((A_HEADS, A_SPAN, A_UNIT)),
            pl.BlockSpec(memory_space=pltpu.SMEM),
        ],
        out_specs=[pl.BlockSpec(full, own(0)), pl.BlockSpec(full, own(0))],
        out_shape=[jax.ShapeDtypeStruct((B, rate, L, A_OUT), jnp.float32)] * 2,
        scratch_shapes=[pltpu.VMEM((tl + 2 * A_RADIUS, A_OUT), _MXU_DTYPE)] * 2
        + [pltpu.VMEM((A_SPAN, 2 * A_UNIT), jnp.float32)],
        compiler_params=_params(("arbitrary",) * 3),
        name=f"mixer_a_rate{rate}",
    )(za_g, za_g, za_g, za_g, za_g, za_g, za_g, bias, shift)


B_UNIT_ROWS = 4
B_UNIT = B_UNIT_ROWS * GRID_W
B_SPAN_ROWS = B_UNIT_ROWS + B_WIN_ROWS
B_SPAN = B_SPAN_ROWS * GRID_W
B_HALO = (B_WIN_ROWS // 2) * GRID_W
B_TILE_ROWS = 32
B_TILE = B_TILE_ROWS * GRID_W


def _mixer_b_kernel(q_ref, kp_ref, ko_ref, kn_ref, vp_ref, vo_ref, vn_ref, blocks_ref, sel_ref, shift_ref,
                    o_ref, kw_ref, vw_ref, bias_ref, s_ref, *, rows):
    @pl.when((pl.program_id(0) == 0) & (pl.program_id(1) == 0))
    def _():
        left = lax.broadcasted_iota(jnp.int32, (GRID_W, LANES), 1) < GRID_W

        def fill(n, carry):
            v, h = n // B_HEADS, n % B_HEADS
            for a in range(B_SPAN_ROWS):
                for gp in range(B_UNIT_ROWS // 2):
                    at = (v * B_SPAN_ROWS + a) * B_UNIT_ROWS + 2 * gp
                    tile = jnp.where(left, blocks_ref[h, sel_ref[at]], blocks_ref[h, sel_ref[at + 1]])
                    bias_ref[v, h, a * GRID_W:(a + 1) * GRID_W, gp * LANES:(gp + 1) * LANES] = tile
            return carry

        lax.fori_loop(0, 3 * B_HEADS, fill, 0)

    i0 = pl.program_id(1) * B_TILE_ROWS
    kw_ref[0:B_HALO] = kp_ref[0]
    kw_ref[B_HALO:B_HALO + B_TILE] = ko_ref[0]
    kw_ref[B_HALO + B_TILE:] = kn_ref[0]
    vw_ref[:, :, 0:B_HALO] = vp_ref[0]
    vw_ref[:, :, B_HALO:B_HALO + B_TILE] = vo_ref[0]
    vw_ref[:, :, B_HALO + B_TILE:] = vn_ref[0]
    tail = _ones_tail(B_SPAN, _MXU_DTYPE)
    upper = lax.broadcasted_iota(jnp.int32, (LANES, B_UNIT), 0) < HEAD_DIM

    n_units = B_TILE_ROWS // B_UNIT_ROWS

    def span_offset(u):
        r0 = jnp.clip(i0 + u * B_UNIT_ROWS - B_WIN_ROWS // 2, 0, rows - B_SPAN_ROWS)
        return pl.multiple_of((r0 - (i0 - B_WIN_ROWS // 2)) * GRID_W, LANES)

    def scores(u, pair):
        qt = q_ref[0, pair, :, pl.ds(pl.multiple_of(u * B_UNIT, B_UNIT), B_UNIT)]
        zero = jnp.zeros_like(qt)
        qt2 = jnp.concatenate([jnp.where(upper, qt, zero), jnp.where(upper, zero, qt)], axis=1)
        return _dot(kw_ref[pl.ds(span_offset(u), B_SPAN), pair * LANES:(pair + 1) * LANES], qt2)

    s_ref[...] = scores(0, 0)

    def unit(fixed_shift, u, carry):
        i0u = i0 + u * B_UNIT_ROWS
        off = span_offset(u)
        variant = jnp.where(i0u == 0, 0, jnp.where(i0u == rows - B_UNIT_ROWS, 2, 1))
        qoff = pl.multiple_of(u * B_UNIT, B_UNIT)
        outs = []
        s_next = s_ref[...]
        for pair in range(B_PAIRS):
            s_pair = s_next
            if pair + 1 < B_PAIRS:
                s_next = scores(u, pair + 1)
            else:
                s_ref[...] = scores(jnp.minimum(u + 1, n_units - 1), 0)
            for odd in range(2):
                s = s_pair[:, odd * B_UNIT:(odd + 1) * B_UNIT] + bias_ref[variant, 2 * pair + odd]
                if not fixed_shift:
                    s = s - jnp.max(s, axis=0, keepdims=True)
                p = jnp.exp2(s).astype(_MXU_DTYPE)
                vt = vw_ref[pair, odd * HEAD_DIM:(odd + 1) * HEAD_DIM, pl.ds(off, B_SPAN)]
                acc = _dot(jnp.concatenate([vt, tail], axis=0), p)
                outs.append(acc[:HEAD_DIM] / acc[HEAD_DIM:HEAD_DIM + 1])
        o_ref[0, pl.ds(qoff, B_UNIT), :] = jnp.concatenate(outs, axis=0).T.astype(o_ref.dtype)
        return carry

    small = shift_ref[1] > 0.5

    @pl.when(small)
    def _():
        lax.fori_loop(0, n_units, functools.partial(unit, True), 0, unroll=n_units)

    @pl.when(jnp.logical_not(small))
    def _():
        lax.fori_loop(0, n_units, functools.partial(unit, False), 0)


def _mixer_b(qbt, kb, vbt, blocks, shift, B, S):
    rows = S // GRID_W
    sel = jnp.asarray(_mixer_b_block_index(rows))
    nt = rows // B_TILE_ROWS
    per = B_TILE // B_HALO
    nh = S // B_HALO
    kb3 = kb.reshape(B, S, B_W)
    prev = lambda t: jnp.maximum(t * per - 1, 0)
    nxt = lambda t: jnp.minimum((t + 1) * per, nh - 1)
    k_edge, k_own = (1, B_HALO, B_W), (1, B_TILE, B_W)
    t_edge, t_own = (1, B_PAIRS, LANES, B_HALO), (1, B_PAIRS, LANES, B_TILE)
    o = pl.pallas_call(
        functools.partial(_mixer_b_kernel, rows=rows),
        grid=(B, nt),
        in_specs=[
            pl.BlockSpec(t_own, lambda b, t: (b, 0, 0, t)),
            pl.BlockSpec(k_edge, lambda b, t: (b, prev(t), 0)),
            pl.BlockSpec(k_own, lambda b, t: (b, t, 0)),
            pl.BlockSpec(k_edge, lambda b, t: (b, nxt(t), 0)),
            pl.BlockSpec(t_edge, lambda b, t: (b, 0, 0, prev(t))),
            pl.BlockSpec(t_own, lambda b, t: (b, 0, 0, t)),
            pl.BlockSpec(t_edge, lambda b, t: (b, 0, 0, nxt(t))),
            _const_spec(blocks.shape),
            pl.BlockSpec(memory_space=pltpu.SMEM),
            pl.BlockSpec(memory_space=pltpu.SMEM),
        ],
        out_specs=pl.BlockSpec(k_own, lambda b, t: (b, t, 0)),
        out_shape=jax.ShapeDtypeStruct((B, S, B_W), _MXU_DTYPE),
        scratch_shapes=[pltpu.VMEM((B_TILE + 2 * B_HALO, B_W), _MXU_DTYPE),
                        pltpu.VMEM((B_PAIRS, LANES, B_TILE + 2 * B_HALO), _MXU_DTYPE),
                        pltpu.VMEM((3, B_HEADS, B_SPAN, B_UNIT), jnp.float32),
                        pltpu.VMEM((B_SPAN, 2 * B_UNIT), jnp.float32)],
        compiler_params=_params(("arbitrary",) * 2),
        name="mixer_b",
    )(qbt, kb3, kb3, kb3, vbt, vbt, vbt, blocks, sel, shift)
    return o.reshape(B * S, B_W)


C_TQ = 4096
C_TK = 512
C_UNIT = 512
C_UNROLL = 1


def _mixer_c_kernel(q_ref, k_ref, vt_ref, shift_ref, o_ref, qcat_ref, m_ref, acc_ref, s_ref, *, n_kv):
    acc_ref[...] = jnp.zeros(acc_ref.shape, jnp.float32)
    for g in range(C_GROUP):
        qcat_ref[:, g * C_TQ:(g + 1) * C_TQ] = q_ref[0, g]
    n_units = C_GROUP * C_TQ // C_UNIT

    def keys(j):
        return k_ref[0, 0, pl.ds(pl.multiple_of(j * C_TK, C_TK), C_TK), :]

    def scores(k, u):
        return _dot(k, qcat_ref[:, u * C_UNIT:(u + 1) * C_UNIT])

    def sweep(update, unroll):
        s_ref[...] = scores(keys(0), 0)

        def step(j, carry):
            k = keys(j)
            vt = vt_ref[0, 0, :, pl.ds(pl.multiple_of(j * C_TK, C_TK), C_TK)]
            s_next = s_ref[...]
            for u in range(n_units):
                s = s_next
                if u + 1 < n_units:
                    s_next = scores(k, u + 1)
                else:
                    s_ref[...] = scores(keys(jnp.minimum(j + 1, n_kv - 1)), 0)
                update(s, vt, slice(u * C_UNIT, (u + 1) * C_UNIT))
            return carry

        lax.fori_loop(0, n_kv, step, 0, unroll=unroll)

    def fixed_shift(s, vt, cols):
        acc_ref[:, cols] += _dot(vt, jnp.exp2(s - shift_ref[0]).astype(_MXU_DTYPE))

    def running_max(s, vt, cols):
        m_prev = m_ref[:, cols]
        m_new = jnp.maximum(m_prev, jnp.max(s, axis=0, keepdims=True))
        alpha = jnp.exp2(m_prev - m_new)
        p = jnp.exp2(s - m_new).astype(_MXU_DTYPE)
        acc_ref[:, cols] = alpha * acc_ref[:, cols] + _dot(vt, p)
        m_ref[:, cols] = m_new

    small = shift_ref[1] > 0.5

    @pl.when(small)
    def _():
        sweep(fixed_shift, C_UNROLL)

    @pl.when(jnp.logical_not(small))
    def _():
        m_ref[...] = jnp.full(m_ref.shape, -jnp.inf, jnp.float32)
        sweep(running_max, 1)

    o_t = jnp.concatenate(
        [acc_ref[:HEAD_DIM, g * C_TQ:(g + 1) * C_TQ] / acc_ref[HEAD_DIM:HEAD_DIM + 1, g * C_TQ:(g + 1) * C_TQ]
         for g in range(C_GROUP)], axis=0)
    o_ref[0] = o_t.T.astype(o_ref.dtype)


def _mixer_c(qc_t, kc, vc_t, shift, B, S):
    o = pl.pallas_call(
        functools.partial(_mixer_c_kernel, n_kv=S // C_TK),
        grid=(B, C_KV_HEADS, S // C_TQ),
        in_specs=[
            pl.BlockSpec((1, C_GROUP, HEAD_DIM, C_TQ), lambda b, kv, i: (b, kv, 0, i)),
            pl.BlockSpec((1, 1, S, HEAD_DIM), lambda b, kv, i: (b, kv, 0, 0)),
            pl.BlockSpec((1, 1, VT_ROWS, S), lambda b, kv, i: (b, kv, 0, 0)),
            pl.BlockSpec(memory_space=pltpu.SMEM),
        ],
        out_specs=pl.BlockSpec((1, C_TQ, C_GROUP * HEAD_DIM), lambda b, kv, i: (b, i, kv)),
        out_shape=jax.ShapeDtypeStruct((B, S, C_QW), _MXU_DTYPE),
        scratch_shapes=[pltpu.VMEM((HEAD_DIM, C_GROUP * C_TQ), _MXU_DTYPE),
                        pltpu.VMEM((1, C_GROUP * C_TQ), jnp.float32),
                        pltpu.VMEM((VT_ROWS, C_GROUP * C_TQ), jnp.float32),
                        pltpu.VMEM((C_TK, C_UNIT), jnp.float32)],
        compiler_params=_params(("arbitrary",) * 3),
        name="mixer_c",
    )(qc_t, kc, vc_t, shift)
    return o.reshape(B * S, C_QW)


MERGE_TM = 512
MERGE_GATE_BLOCK = 512


def _merge_kernel(x_ref, g_ref, oa0_ref, oa1_ref, oa2_ref, l0_ref, l1_ref, l2_ref, ob_ref, oc_ref,
                  wg0_ref, wg1_ref, wg2_ref, wg3_ref, wg4_ref, wg5_ref, pa_ref, pb_ref, pc_ref, wo_ref,
                  out_ref, *scratch):
    tm = x_ref.shape[0]
    gate_refs = (wg0_ref, wg1_ref, wg2_ref, wg3_ref, wg4_ref, wg5_ref)
    cast = lambda ref: ref[...].astype(_MXU_DTYPE)

    def token_major(ref, scr):
        rate = ref.shape[1]
        halves = range(A_OUT // LANES)
        for r in range(rate):
            for j in halves:
                scr[j, pl.ds(r, tm // rate, stride=rate), :] = ref[0, r, :, j * LANES:(j + 1) * LANES]
        return jnp.concatenate([scr[j] for j in halves], axis=1)

    x = x_ref[...]
    h = _rms(x, g_ref[...]).astype(_MXU_DTYPE)
    n_parts = D_MODEL // MERGE_GATE_BLOCK
    branches = [None, _dot(ob_ref[...], cast(pb_ref)), _dot(oc_ref[...], cast(pc_ref))]
    gates = {(b, part): jax.nn.sigmoid(_dot(h, cast(gate_refs[b * n_parts + part])))
             for b in (1, 2, 0) for part in range(n_parts)}
    oa0, l0 = oa0_ref[...], l0_ref[...]
    oa1, l1 = token_major(oa1_ref, scratch[0]), token_major(l1_ref, scratch[1])
    oa2, l2 = token_major(oa2_ref, scratch[2]), token_major(l2_ref, scratch[3])
    mx = jnp.maximum(jnp.maximum(l0, l1), l2)
    w0, w1, w2 = jnp.exp2(l0 - mx), jnp.exp2(l1 - mx), jnp.exp2(l2 - mx)
    o_a = (w0 * oa0 + w1 * oa1 + w2 * oa2) / (w0 + w1 + w2)
    branches[0] = _dot(o_a.astype(_MXU_DTYPE), cast(pa_ref))
    parts = []
    for part in range(n_parts):
        cols = slice(part * MERGE_GATE_BLOCK, (part + 1) * MERGE_GATE_BLOCK)
        merged = sum(gates[b, part] * branches[b][:, cols] for b in range(1, N_BRANCH))
        parts.append((merged + gates[0, part] * branches[0][:, cols]).astype(_MXU_DTYPE))
    out_ref[...] = x + _dot(jnp.concatenate(parts, axis=1), cast(wo_ref))


def _merge(x2, g1, oa, lse, ob, oc, w_in, w_br_a, w_br_b, w_br_c, w_o, layer, B, S):
    M = x2.shape[0]
    tm = MERGE_TM
    per_seq = S // tm
    row = lambda i: (i, 0)
    tile = lambda w: pl.BlockSpec((tm, w), row)
    gate0 = _ZG // MERGE_GATE_BLOCK

    def layer_weight(w, block=None, at=0):
        block = w.shape[2] if block is None else block
        return pl.BlockSpec((None, w.shape[1], block), lambda i: (layer, 0, at), pipeline_mode=pl.Buffered(1))

    def dilated(rate):
        return pl.BlockSpec((1, rate, tm // rate, A_OUT), lambda i: (i // per_seq, 0, i % per_seq, 0))

    r1, r2 = A_PATTERNS[1][1], A_PATTERNS[2][1]
    return pl.pallas_call(
        _merge_kernel,
        grid=(M // tm,),
        in_specs=[tile(D_MODEL), _const_spec((1, D_MODEL)),
                  tile(A_OUT), dilated(r1), dilated(r2), tile(A_OUT), dilated(r1), dilated(r2),
                  tile(B_W), tile(C_QW),
                  *[layer_weight(w_in, MERGE_GATE_BLOCK, gate0 + j)
                    for j in range(N_BRANCH * D_MODEL // MERGE_GATE_BLOCK)],
                  layer_weight(w_br_a), layer_weight(w_br_b), layer_weight(w_br_c), layer_weight(w_o)],
        out_specs=tile(D_MODEL),
        out_shape=jax.ShapeDtypeStruct((M, D_MODEL), jnp.float32),
        scratch_shapes=[pltpu.VMEM((A_OUT // LANES, tm, LANES), jnp.float32)] * 4,
        compiler_params=_params(("arbitrary",)),
        name="gated_merge",
    )(x2, g1, oa[0].reshape(M, A_OUT), oa[1], oa[2], lse[0].reshape(M, A_OUT), lse[1], lse[2],
      ob, oc, *([w_in] * (N_BRANCH * D_MODEL // MERGE_GATE_BLOCK)), w_br_a, w_br_b, w_br_c, w_o)


FFN_TM = 512
FFN_CHUNK = 256
FFN_DOWN_CHUNK = D_FF // 2


def _ffn_kernel(x_ref, g_ref, wup_ref, wdown_ref, out_ref, act_ref):
    x = x_ref[...]
    xn = _rms(x, g_ref[...]).astype(_MXU_DTYPE)
    for c in range(D_FF // FFN_CHUNK):
        cs = slice(c * FFN_CHUNK, (c + 1) * FFN_CHUNK)
        a = _dot(xn, wup_ref[:, cs].astype(_MXU_DTYPE))
        b = _dot(xn, wup_ref[:, D_FF + c * FFN_CHUNK:D_FF + (c + 1) * FFN_CHUNK].astype(_MXU_DTYPE))
        act_ref[:, cs] = (a * jax.nn.sigmoid(a) * b).astype(act_ref.dtype)
    y = x
    for c in range(D_FF // FFN_DOWN_CHUNK):
        rows = slice(c * FFN_DOWN_CHUNK, (c + 1) * FFN_DOWN_CHUNK)
        y = y + _dot(act_ref[:, rows], wdown_ref[rows, :].astype(_MXU_DTYPE))
    out_ref[...] = y


def _ffn(x2, g2, w_up, w_down, layer):
    M = x2.shape[0]
    tm = FFN_TM
    row = lambda i: (i, 0)
    layer_weight = lambda w: pl.BlockSpec((None,) + w.shape[1:], lambda i: (layer, 0, 0),
                                          pipeline_mode=pl.Buffered(1))
    return pl.pallas_call(
        _ffn_kernel,
        grid=(M // tm,),
        in_specs=[pl.BlockSpec((tm, D_MODEL), row), _const_spec((1, D_MODEL)),
                  layer_weight(w_up), layer_weight(w_down)],
        out_specs=pl.BlockSpec((tm, D_MODEL), row),
        out_shape=jax.ShapeDtypeStruct((M, D_MODEL), jnp.float32),
        scratch_shapes=[pltpu.VMEM((tm, D_FF), _MXU_DTYPE)],
        compiler_params=_params(("arbitrary",)),
        name="swiglu_ffn",
    )(x2, g2, w_up, w_down)


def _t5_bucket(rel):
    half = T5_BUCKETS // 2
    max_exact = half // 2
    ret = jnp.where(rel > 0, half, 0)
    n = jnp.abs(rel)
    nf = jnp.maximum(n, 1).astype(jnp.float32)
    large = max_exact + (jnp.log(nf / max_exact) / math.log(T5_MAX_DIST / max_exact)
                         * (half - max_exact)).astype(jnp.int32)
    large = jnp.minimum(large, half - 1)
    return ret + jnp.where(n < max_exact, n, large)


def _mixer_a_bias(table_g, rate):
    j = jnp.arange(A_SPAN)[:, None]
    i = jnp.arange(A_UNIT)[None, :]
    step = j - A_RADIUS - i
    onehot = (_t5_bucket(step * rate)[:, :, None] == jnp.arange(T5_BUCKETS)).astype(jnp.float32)
    bias = jnp.einsum("jib,bh->hji", onehot, table_g.astype(jnp.float32) * LOG2_E,
                      precision=lax.Precision.HIGHEST)
    return jnp.where((jnp.abs(step) <= A_RADIUS)[None], bias, NEG)


B_N_DR = 2 * B_WIN_ROWS - 1


def _mixer_b_blocks(rpb, shift):
    c = np.arange(GRID_W)
    c0 = np.clip(c - B_WIN_COLS // 2, 0, GRID_W - B_WIN_COLS)
    col_ok = (c[:, None] >= c0[None, :]) & (c[:, None] < c0[None, :] + B_WIN_COLS)
    dc = np.clip(c[:, None] - c[None, :] + B_WIN_COLS - 1, 0, 2 * B_WIN_COLS - 2)
    pick_c = ((dc[..., None] == np.arange(2 * B_WIN_COLS - 1)) & col_ok[..., None]).astype(np.float32)
    blocks = jnp.einsum("hdk,xyk->hdxy", rpb.astype(jnp.float32) * LOG2_E, pick_c,
                        precision=lax.Precision.HIGHEST)
    blocks = jnp.where(col_ok[None, None], blocks - shift, NEG)
    blocks = jnp.concatenate([blocks, jnp.full((B_HEADS, 1, GRID_W, GRID_W), NEG, jnp.float32)], axis=1)
    return jnp.concatenate([blocks, blocks], axis=-1)


def _mixer_b_block_index(rows):
    first_query_row = np.array([0, B_WIN_ROWS // 2, rows - B_UNIT_ROWS])
    i = first_query_row[:, None, None] + np.arange(B_UNIT_ROWS)[None, None, :]
    r0 = np.clip(first_query_row - B_WIN_ROWS // 2, 0, rows - B_SPAN_ROWS)
    ik = r0[:, None, None] + np.arange(B_SPAN_ROWS)[None, :, None]
    rs = np.clip(i - B_WIN_ROWS // 2, 0, rows - B_WIN_ROWS)
    row_ok = (ik >= rs) & (ik < rs + B_WIN_ROWS)
    return np.where(row_ok, ik - i + B_WIN_ROWS - 1, B_N_DR).astype(np.int32).reshape(-1)


def _rope_tables(S):
    rows = S // GRID_W
    inv = ROPE_THETA ** (-jnp.arange(0, ROPE_AXIS_DIM, 2, dtype=jnp.float32) / ROPE_AXIS_DIM)
    d = np.arange(LANES) % HEAD_DIM
    inv_lane = inv[d % (ROPE_AXIS_DIM // 2)][None, :]
    is_col = (d >= ROPE_AXIS_DIM)[None, None, :]
    first = ((d % ROPE_AXIS_DIM) < ROPE_AXIS_DIM // 2)[None, :]
    tables = []
    for n in (rows, GRID_W):
        ang = jnp.arange(n, dtype=jnp.float32)[:, None] * inv_lane
        sin = jnp.sin(ang)
        tables.append((jnp.cos(ang), jnp.where(first, -sin, 0.0), jnp.where(first, 0.0, sin)))
    return tuple(jnp.where(is_col, by_col[None, :, :], by_row[:, None, :]).reshape(S, LANES)
                 for by_row, by_col in zip(*tables))


def _softmax_shift(gain_q, gain_k, bias_abs_max):
    bound = (HEAD_DIM * QK_SCALE * LOG2_E * ROUNDING_SLACK * jnp.max(jnp.abs(gain_q)) * jnp.max(jnp.abs(gain_k))
             + LOG2_E * bias_abs_max)
    small = bound <= SHIFT_LIMIT
    return jnp.stack([jnp.where(small, bound, 0.0), small.astype(jnp.float32)]).astype(jnp.float32)


def _block_diag_ones():
    i = np.arange(PROJ_TILE) // HEAD_DIM
    return jnp.asarray(i[:, None] == i[None, :], _MXU_DTYPE)


def kernel(x, rel_bias_table, norm1, w_in, qk_gain, nat_rpb, w_br_a, w_br_b, w_br_c, w_o,
           norm2, w_up, w_down):
    B, S, D = x.shape
    depth = w_in.shape[0]
    M = B * S
    x2 = x.reshape(M, D)
    rope = _rope_tables(S)
    ones_bd = _block_diag_ones()
    a_bias = [_mixer_a_bias(rel_bias_table[:, g * A_HEADS:(g + 1) * A_HEADS], rate)
              for g, (_, rate) in enumerate(A_PATTERNS)]
    for l in range(depth):
        gq = qk_gain[l]
        tile = lambda g, n, s: jnp.tile(g * s, n)
        gain_row = jnp.concatenate([
            tile(gq[0], A_W // HEAD_DIM, QK_SCALE * LOG2_E), tile(gq[1], A_W // HEAD_DIM, 1.0),
            tile(gq[2], B_HEADS, QK_SCALE * LOG2_E), tile(gq[3], B_HEADS, 1.0),
            tile(gq[4], C_Q_HEADS, QK_SCALE * LOG2_E), tile(gq[5], C_KV_HEADS, 1.0)])[None, :]
        g1 = norm1[l][None, :]
        za0, za1, za2, qbt, kb, vbt, qc, kc, vc = _qkv_projection(
            x2, g1, w_in, l, gain_row, ones_bd, rope, B, S)
        za =(za0.reshape(B, 1, S, ZA_W), za1, za2)
        oa, lse = [], []
        for g, (_, rate) in enumerate(A_PATTERNS):
            table_g = rel_bias_table[:, g * A_HEADS:(g + 1) * A_HEADS]
            shift_a = _softmax_shift(gq[0], gq[1], jnp.max(jnp.abs(table_g)))
            o_g, l_g = _mixer_a_group(za[g], a_bias[g] - shift_a[0], shift_a, rate, B, S)
            oa.append(o_g)
            lse.append(l_g)
        shift_b = _softmax_shift(gq[2], gq[3], jnp.max(jnp.abs(nat_rpb[l])))
        ob = _mixer_b(qbt, kb, vbt, _mixer_b_blocks(nat_rpb[l], shift_b[0]), shift_b, B, S)
        oc = _mixer_c(qc, kc, vc, _softmax_shift(gq[4], gq[5], 0.0), B, S)
        x2 = _merge(x2, g1, oa, lse, ob, oc, w_in, w_br_a, w_br_b, w_br_c, w_o, l, B, S)
        x2 = _ffn(x2, norm2[l][None, :], w_up, w_down, l)
    return x2.reshape(B, S, D)
```

```python
import functools
import math

import jax
import jax.numpy as jnp
import numpy as np
from jax import lax
from jax.experimental import pallas as pl
from jax.experimental.pallas import tpu as pltpu

_MXU_DTYPE = jnp.bfloat16

D_MODEL = 1024
HEAD_DIM = 64
GRID_W = 64
RMS_EPS = 1e-6
NEG = -1e30
A_PATTERNS = ((128, 1), (512, 4), (2048, 16))
A_GROUPS = 3
A_HEADS = 4
A_W = A_GROUPS * A_HEADS * HEAD_DIM
A_OUT = A_HEADS * HEAD_DIM
A_RADIUS = 64
B_HEADS = 8
B_W = B_HEADS * HEAD_DIM
B_WIN_ROWS = 8
B_WIN_COLS = 16
C_Q_HEADS = 8
C_KV_HEADS = 2
C_GROUP = C_Q_HEADS // C_KV_HEADS
C_QW = C_Q_HEADS * HEAD_DIM
C_KVW = C_KV_HEADS * HEAD_DIM
ROPE_THETA = 10000.0
ROPE_AXIS_DIM = HEAD_DIM // 2
T5_BUCKETS = 32
T5_MAX_DIST = 1024
N_BRANCH = 3
D_FF = math.ceil(8 * D_MODEL / 3 / 256) * 256
QK_SCALE = HEAD_DIM ** -0.5
LOG2_E = math.log2(math.e)

V7X_VMEM_BYTES = 64 * 1024 * 1024
VMEM_LIMIT_BYTES = V7X_VMEM_BYTES * 7 // 8
LANES = 128
BF16_SUBLANES = 16
MXU_COLS = 256
VT_ROWS = HEAD_DIM + BF16_SUBLANES
SHIFT_LIMIT = 60.0
ROUNDING_SLACK = 1.02

_OFF = np.cumsum([0, A_W, A_W, A_W, B_W, B_W, B_W, C_QW, C_KVW, C_KVW]).tolist()
(_QA, _KA, _VA, _QB, _KB, _VB, _QC, _KC, _VC, _ZG) = _OFF

ZA_W = 3 * A_OUT
B_PAIRS = B_HEADS // 2


def _params(sem):
    return pltpu.CompilerParams(dimension_semantics=sem, vmem_limit_bytes=VMEM_LIMIT_BYTES)


def _const_spec(shape):
    nd = len(shape)
    return pl.BlockSpec(shape, lambda *_: (0,) * nd, pipeline_mode=pl.Buffered(1))


def _rms(x, g):
    return x * lax.rsqrt(jnp.mean(x * x, axis=-1, keepdims=True) + RMS_EPS) * g


def _dot(a, b):
    return jnp.dot(a, b, preferred_element_type=jnp.float32)


def _dot_nt(a, b):
    return lax.dot_general(a, b, (((1,), (1,)), ((), ())), preferred_element_type=jnp.float32)


def _ones_tail(width, dtype):
    return (lax.broadcasted_iota(jnp.int32, (VT_ROWS - HEAD_DIM, width), 0) == 0).astype(dtype)


PROJ_TM = 1024
PROJ_TILE = MXU_COLS
PROJ_W_BLOCK = 1536
_N_NORM = 2 * A_W + 2 * B_W
_N_ROPE = C_QW + C_KVW


def _proj_kernel(x_ref, g_ref, w0_ref, w1_ref, w2_ref, gain_ref, ones_ref, cos_ref, s1_ref, s2_ref,
                 za0_ref, za1_ref, za2_ref, qbt_ref, kb_ref, vbt_ref, qc_ref, kc_ref, vc_ref, dil_ref):
    tm = x_ref.shape[0]
    w_refs = (w0_ref, w1_ref, w2_ref)
    h = _rms(x_ref[...], g_ref[...]).astype(_MXU_DTYPE)
    za_refs = (za0_ref, za1_ref, za2_ref)

    def head_norm(acc, c0, width):
        sq = (acc * acc).astype(_MXU_DTYPE)
        ms = _dot(sq, ones_ref[:width, :width]) * (1.0 / HEAD_DIM)
        return acc * lax.rsqrt(ms + RMS_EPS) * gain_ref[:, c0:c0 + width]

    def store_group(g, section, val):
        cols = slice(section * A_OUT, (section + 1) * A_OUT)
        rate = A_PATTERNS[g][1]
        if rate == 1:
            za0_ref[:, cols] = val.astype(za0_ref.dtype)
            return
        for j in range(A_OUT // LANES):
            dil_ref[j] = val[:, j * LANES:(j + 1) * LANES]
        for r in range(rate):
            picked = [dil_ref[j, pl.ds(r, tm // rate, stride=rate), :] for j in range(A_OUT // LANES)]
            za_refs[g][0, r, :, cols] = jnp.concatenate(picked, axis=1).astype(za_refs[g].dtype)

    def store_pairs(ref, t, val):
        vt = val.T.astype(ref.dtype)
        ref[0, 2 * t] = vt[:LANES]
        ref[0, 2 * t + 1] = vt[LANES:]

    def rotary(y, width):
        reps = width // LANES
        table = lambda ref: jnp.concatenate([ref[...]] * reps, axis=1) if reps > 1 else ref[...]
        return (y * table(cos_ref) + pltpu.roll(y, width - 16, 1) * table(s1_ref)
                + pltpu.roll(y, 16, 1) * table(s2_ref))

    def store_qc(t, acc, gain_at):
        yt = rotary(head_norm(acc, gain_at, PROJ_TILE), PROJ_TILE).T.astype(qc_ref.dtype)
        heads = PROJ_TILE // HEAD_DIM
        for j in range(heads):
            qc_ref[0, heads * t + j] = yt[j * HEAD_DIM:(j + 1) * HEAD_DIM]

    def store_kc_vc(acc, gain_at):
        y = rotary(head_norm(acc[:, :LANES], gain_at, LANES), LANES).astype(kc_ref.dtype)
        acc_t = acc[:, LANES:].T.astype(vc_ref.dtype)
        tail = _ones_tail(tm, vc_ref.dtype)
        for kv in range(C_KV_HEADS):
            kc_ref[0, kv] = y[:, kv * HEAD_DIM:(kv + 1) * HEAD_DIM]
            vc_ref[0, kv, :HEAD_DIM] = acc_t[kv * HEAD_DIM:(kv + 1) * HEAD_DIM]
            vc_ref[0, kv, HEAD_DIM:] = tail

    jobs = []
    for section, src in enumerate((_QA, _KA)):
        for g in range(A_GROUPS):
            gain_at = section * A_W + g * A_OUT
            jobs.append((src + g * A_OUT, A_OUT, lambda acc, gain_at=gain_at, g=g, section=section:
                         store_group(g, section, head_norm(acc, gain_at, A_OUT))))
    for t in range(B_W // PROJ_TILE):
        gain_at = 2 * A_W + t * PROJ_TILE
        jobs.append((_QB + t * PROJ_TILE, PROJ_TILE, lambda acc, gain_at=gain_at, t=t:
                     store_pairs(qbt_ref, t, head_norm(acc, gain_at, PROJ_TILE))))
    for t in range(B_W // PROJ_TILE):
        gain_at = 2 * A_W + B_W + t * PROJ_TILE
        def store_kb(acc, gain_at=gain_at, t=t):
            kb_ref[:, t * PROJ_TILE:(t + 1) * PROJ_TILE] = head_norm(acc, gain_at, PROJ_TILE).astype(kb_ref.dtype)
        jobs.append((_KB + t * PROJ_TILE, PROJ_TILE, store_kb))
    for t in range(C_QW // PROJ_TILE):
        jobs.append((_QC + t * PROJ_TILE, PROJ_TILE,
                     lambda acc, t=t: store_qc(t, acc, _N_NORM + t * PROJ_TILE)))
    jobs.append((_KC, 2 * C_KVW, lambda acc: store_kc_vc(acc, _N_NORM + C_QW)))
    for g in range(A_GROUPS):
        jobs.append((_VA + g * A_OUT, A_OUT, lambda acc, g=g: store_group(g, 2, acc)))
    for t in range(B_W // PROJ_TILE):
        jobs.append((_VB + t * PROJ_TILE, PROJ_TILE, lambda acc, t=t: store_pairs(vbt_ref, t, acc)))

    def product(n):
        src, width, _ = jobs[n]
        blk, col = divmod(src, PROJ_W_BLOCK)
        return _dot(h, w_refs[blk][:, col:col + width].astype(_MXU_DTYPE))

    acc_next = product(0)
    for n, (_, _, consume) in enumerate(jobs):
        acc = acc_next
        if n + 1 < len(jobs):
            acc_next = product(n + 1)
        consume(acc)


def _qkv_projection(x2, g1, w_in, layer, gain_row, ones_bd, rope, B, S):
    M = x2.shape[0]
    tm = PROJ_TM
    per_seq = S // tm
    w_block = lambda j: pl.BlockSpec((None, D_MODEL, PROJ_W_BLOCK), lambda i: (layer, 0, j),
                                     pipeline_mode=pl.Buffered(1))
    cos_t, s1_t, s2_t = rope
    r1, r2 = A_PATTERNS[1][1], A_PATTERNS[2][1]
    row = lambda i: (i, 0)
    pos = lambda i: (i % per_seq, 0)
    hm = lambda i: (i // per_seq, 0, i % per_seq, 0)
    hm_t = lambda i: (i // per_seq, 0, 0, i % per_seq)
    cd = _MXU_DTYPE
    return pl.pallas_call(
        _proj_kernel,
        grid=(M // tm,),
        in_specs=[
            pl.BlockSpec((tm, D_MODEL), row),
            _const_spec((1, D_MODEL)),
            w_block(0), w_block(1), w_block(2),
            _const_spec((1, _N_NORM + _N_ROPE)),
            _const_spec((PROJ_TILE, PROJ_TILE)),
            pl.BlockSpec((tm, LANES), pos),
            pl.BlockSpec((tm, LANES), pos),
            pl.BlockSpec((tm, LANES), pos),
        ],
        out_specs=[
            pl.BlockSpec((tm, ZA_W), row),
            pl.BlockSpec((1, r1, tm // r1, ZA_W), hm),
            pl.BlockSpec((1, r2, tm // r2, ZA_W), hm),
            pl.BlockSpec((1, B_PAIRS, LANES, tm), hm_t),
            pl.BlockSpec((tm, B_W), row),
            pl.BlockSpec((1, B_PAIRS, LANES, tm), hm_t),
            pl.BlockSpec((1, C_Q_HEADS, HEAD_DIM, tm), hm_t),
            pl.BlockSpec((1, C_KV_HEADS, tm, HEAD_DIM), hm),
            pl.BlockSpec((1, C_KV_HEADS, VT_ROWS, tm), hm_t),
        ],
        out_shape=[
            jax.ShapeDtypeStruct((M, ZA_W), cd),
            jax.ShapeDtypeStruct((B, r1, S // r1, ZA_W), cd),
            jax.ShapeDtypeStruct((B, r2, S // r2, ZA_W), cd),
            jax.ShapeDtypeStruct((B, B_PAIRS, LANES, S), cd),
            jax.ShapeDtypeStruct((M, B_W), cd),
            jax.ShapeDtypeStruct((B, B_PAIRS, LANES, S), cd),
            jax.ShapeDtypeStruct((B, C_Q_HEADS, HEAD_DIM, S), cd),
            jax.ShapeDtypeStruct((B, C_KV_HEADS, S, HEAD_DIM), cd),
            jax.ShapeDtypeStruct((B, C_KV_HEADS, VT_ROWS, S), cd),
        ],
        scratch_shapes=[pltpu.VMEM((A_OUT // LANES, tm, LANES), jnp.float32)],
        compiler_params=_params(("arbitrary",)),
        name="qkv_projection",
    )(x2, g1, w_in, w_in, w_in, gain_row, ones_bd, cos_t, s1_t, s2_t)


A_UNIT = 4 * A_RADIUS
A_SPAN = A_UNIT + 2 * A_RADIUS
A_TILE = 2048


def _mixer_a_kernel(q_ref, kp_ref, ko_ref, kn_ref, vp_ref, vo_ref, vn_ref, bias_ref, shift_ref,
                    o_ref, lse_ref, kw_ref, vw_ref, s_ref, *, tl, seq_len):
    n_res = q_ref.shape[1]
    l0 = pl.program_id(2) * tl
    for res in range(n_res):
        kw_ref[res, 0:A_RADIUS] = kp_ref[0, res]
        kw_ref[res, A_RADIUS:A_RADIUS + tl] = ko_ref[0, res]
        kw_ref[res, A_RADIUS + tl:] = kn_ref[0, res]
        vw_ref[res, 0:A_RADIUS] = vp_ref[0, res]
        vw_ref[res, A_RADIUS:A_RADIUS + tl] = vo_ref[0, res]
        vw_ref[res, A_RADIUS + tl:] = vn_ref[0, res]
    even_q = lax.broadcasted_iota(jnp.int32, (A_UNIT, LANES), 1) < HEAD_DIM
    lane_k = lax.broadcasted_iota(jnp.int32, (A_SPAN, LANES), 1)

    per_res = tl // A_UNIT
    n_units = n_res * per_res

    def place(n):
        return n // per_res, pl.multiple_of((n % per_res) * A_UNIT, A_UNIT)

    def scores(n, pair):
        res, off = place(n)
        q = q_ref[0, res, pl.ds(off, A_UNIT), pair * LANES:(pair + 1) * LANES]
        zero = jnp.zeros_like(q)
        q2 = jnp.concatenate([jnp.where(even_q, q, zero), jnp.where(even_q, zero, q)], axis=0)
        return _dot_nt(kw_ref[res, pl.ds(off, A_SPAN), pair * LANES:(pair + 1) * LANES], q2)

    s_ref[...] = scores(0, 0)

    def unit(fixed_shift, u, carry):
        res, off = place(u)
        top_ok = l0 + off - A_RADIUS >= 0
        bot_ok = l0 + off + A_UNIT + A_RADIUS <= seq_len
        outs, lses = [], []
        s_next = s_ref[...]
        for pair in range(A_HEADS // 2):
            s_pair = s_next
            if pair + 1 < A_HEADS // 2:
                s_next = scores(u, pair + 1)
            else:
                s_ref[...] = scores(jnp.minimum(u + 1, n_units - 1), 0)
            v_slab = vw_ref[res, pl.ds(off, A_SPAN), pair * LANES:(pair + 1) * LANES]
            for odd in range(2):
                s = s_pair[:, odd * A_UNIT:(odd + 1) * A_UNIT] + bias_ref[2 * pair + odd]
                s = jnp.concatenate([jnp.where(top_ok, s[:A_RADIUS], NEG), s[A_RADIUS:A_SPAN - A_RADIUS],
                                     jnp.where(bot_ok, s[A_SPAN - A_RADIUS:], NEG)], axis=0)
                if fixed_shift:
                    m = shift_ref[0]
                    p = jnp.exp2(s).astype(_MXU_DTYPE)
                else:
                    m = jnp.max(s, axis=0, keepdims=True)
                    p = jnp.exp2(s - m).astype(_MXU_DTYPE)
                den_row = (1 - odd) * HEAD_DIM
                mine = (lane_k >= odd * HEAD_DIM) & (lane_k < (odd + 1) * HEAD_DIM)
                v_aug = jnp.where(mine, v_slab, (lane_k == den_row).astype(v_slab.dtype))
                acc = lax.dot_general(v_aug, p, (((0,), (0,)), ((), ())),
                                      preferred_element_type=jnp.float32)
                den = acc[den_row:den_row + 1]
                outs.append(acc[odd * HEAD_DIM:(odd + 1) * HEAD_DIM] / den)
                lses.append(jnp.broadcast_to(m + jnp.log2(den), (HEAD_DIM, A_UNIT)))
        o_ref[0, res, pl.ds(off, A_UNIT), :] = jnp.concatenate(outs, axis=0).T
        lse_ref[0, res, pl.ds(off, A_UNIT), :] = jnp.concatenate(lses, axis=0).T
        return carry

    small = shift_ref[1] > 0.5

    @pl.when(small)
    def _():
        lax.fori_loop(0, n_units, functools.partial(unit, True), 0, unroll=min(4, n_units))

    @pl.when(jnp.logical_not(small))
    def _():
        lax.fori_loop(0, n_units, functools.partial(unit, False), 0)


def _mixer_a_group(za_g, bias, shift, rate, B, S):
    L = S // rate
    tl = min(A_TILE, L)
    n_res = min(rate, A_TILE // tl)
    nblk = L // A_RADIUS
    per = tl // A_RADIUS

    def own(section):
        return lambda b, r, l: (b, r, l, section)

    def prev(section):
        return lambda b, r, l: (b, r, jnp.maximum(l * per - 1, 0), section)

    def nxt(section):
        return lambda b, r, l: (b, r, jnp.minimum((l + 1) * per, nblk - 1), section)

    edge = (1, n_res, A_RADIUS, A_OUT)
    full = (1, n_res, tl, A_OUT)
    return pl.pallas_call(
        functools.partial(_mixer_a_kernel, tl=tl, seq_len=L),
        grid=(B, rate // n_res, L // tl),
        in_specs=[
            pl.BlockSpec(full, own(0)),
            pl.BlockSpec(edge, prev(1)), pl.BlockSpec(full, own(1)), pl.BlockSpec(edge, nxt(1)),
            pl.BlockSpec(edge, prev(2)), pl.BlockSpec(full, own(2)), pl.BlockSpec(edge, nxt(2)),
            _const_spec((A_HEADS, A_SPAN, A_UNIT)),
            pl.BlockSpec(memory_space=pltpu.SMEM),
        ],
        out_specs=[pl.BlockSpec(full, own(0)), pl.BlockSpec(full, own(0))],
        out_shape=[jax.ShapeDtypeStruct((B, rate, L, A_OUT), jnp.float32)] * 2,
        scratch_shapes=[pltpu.VMEM((n_res, tl + 2 * A_RADIUS, A_OUT), _MXU_DTYPE)] * 2
        + [pltpu.VMEM((A_SPAN, 2 * A_UNIT), jnp.float32)],
        compiler_params=_params(("arbitrary",) * 3),
        name=f"mixer_a_rate{rate}",
    )(za_g, za_g, za_g, za_g, za_g, za_g, za_g, bias, shift)


B_UNIT_ROWS = 4
B_UNIT = B_UNIT_ROWS * GRID_W
B_SPAN_ROWS = B_UNIT_ROWS + B_WIN_ROWS
B_SPAN = B_SPAN_ROWS * GRID_W
B_HALO = (B_WIN_ROWS // 2) * GRID_W
B_TILE_ROWS = 32
B_TILE = B_TILE_ROWS * GRID_W


def _mixer_b_kernel(q_ref, kp_ref, ko_ref, kn_ref, vp_ref, vo_ref, vn_ref, blocks_ref, sel_ref, shift_ref,
                    o_ref, kw_ref, vw_ref, bias_ref, s_ref, *, rows):
    @pl.when((pl.program_id(0) == 0) & (pl.program_id(1) == 0))
    def _():
        left = lax.broadcasted_iota(jnp.int32, (GRID_W, LANES), 1) < GRID_W

        def fill(n, carry):
            v, h = n // B_HEADS, n % B_HEADS
            for a in range(B_SPAN_ROWS):
                for gp in range(B_UNIT_ROWS // 2):
                    at = (v * B_SPAN_ROWS + a) * B_UNIT_ROWS + 2 * gp
                    tile = jnp.where(left, blocks_ref[h, sel_ref[at]], blocks_ref[h, sel_ref[at + 1]])
                    bias_ref[v, h, a * GRID_W:(a + 1) * GRID_W, gp * LANES:(gp + 1) * LANES] = tile
            return carry

        lax.fori_loop(0, 3 * B_HEADS, fill, 0)

    i0 = pl.program_id(1) * B_TILE_ROWS
    kw_ref[0:B_HALO] = kp_ref[0]
    kw_ref[B_HALO:B_HALO + B_TILE] = ko_ref[0]
    kw_ref[B_HALO + B_TILE:] = kn_ref[0]
    vw_ref[:, :, 0:B_HALO] = vp_ref[0]
    vw_ref[:, :, B_HALO:B_HALO + B_TILE] = vo_ref[0]
    vw_ref[:, :, B_HALO + B_TILE:] = vn_ref[0]
    tail = _ones_tail(B_SPAN, _MXU_DTYPE)
    upper = lax.broadcasted_iota(jnp.int32, (LANES, B_UNIT), 0) < HEAD_DIM

    n_units = B_TILE_ROWS // B_UNIT_ROWS

    def span_offset(u):
        r0 = jnp.clip(i0 + u * B_UNIT_ROWS - B_WIN_ROWS // 2, 0, rows - B_SPAN_ROWS)
        return pl.multiple_of((r0 - (i0 - B_WIN_ROWS // 2)) * GRID_W, LANES)

    def scores(u, pair):
        qt = q_ref[0, pair, :, pl.ds(pl.multiple_of(u * B_UNIT, B_UNIT), B_UNIT)]
        zero = jnp.zeros_like(qt)
        qt2 = jnp.concatenate([jnp.where(upper, qt, zero), jnp.where(upper, zero, qt)], axis=1)
        return _dot(kw_ref[pl.ds(span_offset(u), B_SPAN), pair * LANES:(pair + 1) * LANES], qt2)

    s_ref[...] = scores(0, 0)

    def unit(fixed_shift, u, carry):
        i0u = i0 + u * B_UNIT_ROWS
        off = span_offset(u)
        variant = jnp.where(i0u == 0, 0, jnp.where(i0u == rows - B_UNIT_ROWS, 2, 1))
        qoff = pl.multiple_of(u * B_UNIT, B_UNIT)
        outs = []
        s_next = s_ref[...]
        for pair in range(B_PAIRS):
            s_pair = s_next
            if pair + 1 < B_PAIRS:
                s_next = scores(u, pair + 1)
            else:
                s_ref[...] = scores(jnp.minimum(u + 1, n_units - 1), 0)
            for odd in range(2):
                s = s_pair[:, odd * B_UNIT:(odd + 1) * B_UNIT] + bias_ref[variant, 2 * pair + odd]
                if not fixed_shift:
                    s = s - jnp.max(s, axis=0, keepdims=True)
                p = jnp.exp2(s).astype(_MXU_DTYPE)
                vt = vw_ref[pair, odd * HEAD_DIM:(odd + 1) * HEAD_DIM, pl.ds(off, B_SPAN)]
                acc = _dot(jnp.concatenate([vt, tail], axis=0), p)
                outs.append(acc[:HEAD_DIM] / acc[HEAD_DIM:HEAD_DIM + 1])
        o_ref[0, pl.ds(qoff, B_UNIT), :] = jnp.concatenate(outs, axis=0).T.astype(o_ref.dtype)
        return carry

    small = shift_ref[1] > 0.5

    @pl.when(small)
    def _():
        lax.fori_loop(0, n_units, functools.partial(unit, True), 0, unroll=n_units)

    @pl.when(jnp.logical_not(small))
    def _():
        lax.fori_loop(0, n_units, functools.partial(unit, False), 0)


def _mixer_b(qbt, kb, vbt, blocks, shift, B, S):
    rows = S // GRID_W
    sel = jnp.asarray(_mixer_b_block_index(rows))
    nt = rows // B_TILE_ROWS
    per = B_TILE // B_HALO
    nh = S // B_HALO
    kb3 = kb.reshape(B, S, B_W)
    prev = lambda t: jnp.maximum(t * per - 1, 0)
    nxt = lambda t: jnp.minimum((t + 1) * per, nh - 1)
    k_edge, k_own = (1, B_HALO, B_W), (1, B_TILE, B_W)
    t_edge, t_own = (1, B_PAIRS, LANES, B_HALO), (1, B_PAIRS, LANES, B_TILE)
    o = pl.pallas_call(
        functools.partial(_mixer_b_kernel, rows=rows),
        grid=(B, nt),
        in_specs=[
            pl.BlockSpec(t_own, lambda b, t: (b, 0, 0, t)),
            pl.BlockSpec(k_edge, lambda b, t: (b, prev(t), 0)),
            pl.BlockSpec(k_own, lambda b, t: (b, t, 0)),
            pl.BlockSpec(k_edge, lambda b, t: (b, nxt(t), 0)),
            pl.BlockSpec(t_edge, lambda b, t: (b, 0, 0, prev(t))),
            pl.BlockSpec(t_own, lambda b, t: (b, 0, 0, t)),
            pl.BlockSpec(t_edge, lambda b, t: (b, 0, 0, nxt(t))),
            _const_spec(blocks.shape),
            pl.BlockSpec(memory_space=pltpu.SMEM),
            pl.BlockSpec(memory_space=pltpu.SMEM),
        ],
        out_specs=pl.BlockSpec(k_own, lambda b, t: (b, t, 0)),
        out_shape=jax.ShapeDtypeStruct((B, S, B_W), _MXU_DTYPE),
        scratch_shapes=[pltpu.VMEM((B_TILE + 2 * B_HALO, B_W), _MXU_DTYPE),
                        pltpu.VMEM((B_PAIRS, LANES, B_TILE + 2 * B_HALO), _MXU_DTYPE),
                        pltpu.VMEM((3, B_HEADS, B_SPAN, B_UNIT), jnp.float32),
                        pltpu.VMEM((B_SPAN, 2 * B_UNIT), jnp.float32)],
        compiler_params=_params(("arbitrary",) * 2),
        name="mixer_b",
    )(qbt, kb3, kb3, kb3, vbt, vbt, vbt, blocks, sel, shift)
    return o.reshape(B * S, B_W)


C_TQ = 4096
C_TK = 512
C_UNIT = 512
C_UNROLL = 1


def _mixer_c_kernel(q_ref, k_ref, vt_ref, shift_ref, o_ref, qcat_ref, m_ref, acc_ref, s_ref, *, n_kv):
    acc_ref[...] = jnp.zeros(acc_ref.shape, jnp.float32)
    for g in range(C_GROUP):
        qcat_ref[:, g * C_TQ:(g + 1) * C_TQ] = q_ref[0, g]
    n_units = C_GROUP * C_TQ // C_UNIT

    def keys(j):
        return k_ref[0, 0, pl.ds(pl.multiple_of(j * C_TK, C_TK), C_TK), :]

    def scores(k, u):
        return _dot(k, qcat_ref[:, u * C_UNIT:(u + 1) * C_UNIT])

    def sweep(update, unroll):
        s_ref[...] = scores(keys(0), 0)

        def step(j, carry):
            k = keys(j)
            vt = vt_ref[0, 0, :, pl.ds(pl.multiple_of(j * C_TK, C_TK), C_TK)]
            s_next = s_ref[...]
            for u in range(n_units):
                s = s_next
                if u + 1 < n_units:
                    s_next = scores(k, u + 1)
                else:
                    s_ref[...] = scores(keys(jnp.minimum(j + 1, n_kv - 1)), 0)
                update(s, vt, slice(u * C_UNIT, (u + 1) * C_UNIT))
            return carry

        lax.fori_loop(0, n_kv, step, 0, unroll=unroll)

    def fixed_shift(s, vt, cols):
        acc_ref[:, cols] += _dot(vt, jnp.exp2(s - shift_ref[0]).astype(_MXU_DTYPE))

    def running_max(s, vt, cols):
        m_prev = m_ref[:, cols]
        m_new = jnp.maximum(m_prev, jnp.max(s, axis=0, keepdims=True))
        alpha = jnp.exp2(m_prev - m_new)
        p = jnp.exp2(s - m_new).astype(_MXU_DTYPE)
        acc_ref[:, cols] = alpha * acc_ref[:, cols] + _dot(vt, p)
        m_ref[:, cols] = m_new

    small = shift_ref[1] > 0.5

    @pl.when(small)
    def _():
        sweep(fixed_shift, C_UNROLL)

    @pl.when(jnp.logical_not(small))
    def _():
        m_ref[...] = jnp.full(m_ref.shape, -jnp.inf, jnp.float32)
        sweep(running_max, 1)

    o_t = jnp.concatenate(
        [acc_ref[:HEAD_DIM, g * C_TQ:(g + 1) * C_TQ] / acc_ref[HEAD_DIM:HEAD_DIM + 1, g * C_TQ:(g + 1) * C_TQ]
         for g in range(C_GROUP)], axis=0)
    o_ref[0] = o_t.T.astype(o_ref.dtype)


def _mixer_c(qc_t, kc, vc_t, shift, B, S):
    o = pl.pallas_call(
        functools.partial(_mixer_c_kernel, n_kv=S // C_TK),
        grid=(B, C_KV_HEADS, S // C_TQ),
        in_specs=[
            pl.BlockSpec((1, C_GROUP, HEAD_DIM, C_TQ), lambda b, kv, i: (b, kv, 0, i)),
            pl.BlockSpec((1, 1, S, HEAD_DIM), lambda b, kv, i: (b, kv, 0, 0)),
            pl.BlockSpec((1, 1, VT_ROWS, S), lambda b, kv, i: (b, kv, 0, 0)),
            pl.BlockSpec(memory_space=pltpu.SMEM),
        ],
        out_specs=pl.BlockSpec((1, C_TQ, C_GROUP * HEAD_DIM), lambda b, kv, i: (b, i, kv)),
        out_shape=jax.ShapeDtypeStruct((B, S, C_QW), _MXU_DTYPE),
        scratch_shapes=[pltpu.VMEM((HEAD_DIM, C_GROUP * C_TQ), _MXU_DTYPE),
                        pltpu.VMEM((1, C_GROUP * C_TQ), jnp.float32),
                        pltpu.VMEM((VT_ROWS, C_GROUP * C_TQ), jnp.float32),
                        pltpu.VMEM((C_TK, C_UNIT), jnp.float32)],
        compiler_params=_params(("arbitrary",) * 3),
        name="mixer_c",
    )(qc_t, kc, vc_t, shift)
    return o.reshape(B * S, C_QW)


MERGE_TM = 512
MERGE_GATE_BLOCK = 512


def _merge_kernel(x_ref, g_ref, oa0_ref, oa1_ref, oa2_ref, l0_ref, l1_ref, l2_ref, ob_ref, oc_ref,
                  wg0_ref, wg1_ref, wg2_ref, wg3_ref, wg4_ref, wg5_ref, pa_ref, pb_ref, pc_ref, wo_ref,
                  out_ref, *scratch):
    tm = x_ref.shape[0]
    gate_refs = (wg0_ref, wg1_ref, wg2_ref, wg3_ref, wg4_ref, wg5_ref)
    cast = lambda ref: ref[...].astype(_MXU_DTYPE)

    def token_major(ref, scr):
        rate = ref.shape[1]
        halves = range(A_OUT // LANES)
        for r in range(rate):
            for j in halves:
                scr[j, pl.ds(r, tm // rate, stride=rate), :] = ref[0, r, :, j * LANES:(j + 1) * LANES]
        return jnp.concatenate([scr[j] for j in halves], axis=1)

    x = x_ref[...]
    h = _rms(x, g_ref[...]).astype(_MXU_DTYPE)
    n_parts = D_MODEL // MERGE_GATE_BLOCK
    branches = [None, _dot(ob_ref[...], cast(pb_ref)), _dot(oc_ref[...], cast(pc_ref))]
    gates = {(b, part): jax.nn.sigmoid(_dot(h, cast(gate_refs[b * n_parts + part])))
             for b in (1, 2, 0) for part in range(n_parts)}
    oa0, l0 = oa0_ref[...], l0_ref[...]
    oa1, l1 = token_major(oa1_ref, scratch[0]), token_major(l1_ref, scratch[1])
    oa2, l2 = token_major(oa2_ref, scratch[2]), token_major(l2_ref, scratch[3])
    mx = jnp.maximum(jnp.maximum(l0, l1), l2)
    w0, w1, w2 = jnp.exp2(l0 - mx), jnp.exp2(l1 - mx), jnp.exp2(l2 - mx)
    o_a = (w0 * oa0 + w1 * oa1 + w2 * oa2) / (w0 + w1 + w2)
    branches[0] = _dot(o_a.astype(_MXU_DTYPE), cast(pa_ref))
    parts = []
    for part in range(n_parts):
        cols = slice(part * MERGE_GATE_BLOCK, (part + 1) * MERGE_GATE_BLOCK)
        merged = sum(gates[b, part] * branches[b][:, cols] for b in range(1, N_BRANCH))
        parts.append((merged + gates[0, part] * branches[0][:, cols]).astype(_MXU_DTYPE))
    out_ref[...] = x + _dot(jnp.concatenate(parts, axis=1), cast(wo_ref))


def _merge(x2, g1, oa, lse, ob, oc, w_in, w_br_a, w_br_b, w_br_c, w_o, layer, B, S):
    M = x2.shape[0]
    tm = MERGE_TM
    per_seq = S // tm
    row = lambda i: (i, 0)
    tile = lambda w: pl.BlockSpec((tm, w), row)
    gate0 = _ZG // MERGE_GATE_BLOCK

    def layer_weight(w, block=None, at=0):
        block = w.shape[2] if block is None else block
        return pl.BlockSpec((None, w.shape[1], block), lambda i: (layer, 0, at), pipeline_mode=pl.Buffered(1))

    def dilated(rate):
        return pl.BlockSpec((1, rate, tm // rate, A_OUT), lambda i: (i // per_seq, 0, i % per_seq, 0))

    r1, r2 = A_PATTERNS[1][1], A_PATTERNS[2][1]
    return pl.pallas_call(
        _merge_kernel,
        grid=(M // tm,),
        in_specs=[tile(D_MODEL), _const_spec((1, D_MODEL)),
                  tile(A_OUT), dilated(r1), dilated(r2), tile(A_OUT), dilated(r1), dilated(r2),
                  tile(B_W), tile(C_QW),
                  *[layer_weight(w_in, MERGE_GATE_BLOCK, gate0 + j)
                    for j in range(N_BRANCH * D_MODEL // MERGE_GATE_BLOCK)],
                  layer_weight(w_br_a), layer_weight(w_br_b), layer_weight(w_br_c), layer_weight(w_o)],
        out_specs=tile(D_MODEL),
        out_shape=jax.ShapeDtypeStruct((M, D_MODEL), jnp.float32),
        scratch_shapes=[pltpu.VMEM((A_OUT // LANES, tm, LANES), jnp.float32)] * 4,
        compiler_params=_params(("arbitrary",)),
        name="gated_merge",
    )(x2, g1, oa[0].reshape(M, A_OUT), oa[1], oa[2], lse[0].reshape(M, A_OUT), lse[1], lse[2],
      ob, oc, *([w_in] * (N_BRANCH * D_MODEL // MERGE_GATE_BLOCK)), w_br_a, w_br_b, w_br_c, w_o)


FFN_TM = 512
FFN_CHUNK = 256
FFN_DOWN_CHUNK = D_FF // 2


def _ffn_kernel(x_ref, g_ref, wup_ref, wdown_ref, out_ref, act_ref):
    x = x_ref[...]
    xn = _rms(x, g_ref[...]).astype(_MXU_DTYPE)
    for c in range(D_FF // FFN_CHUNK):
        cs = slice(c * FFN_CHUNK, (c + 1) * FFN_CHUNK)
        a = _dot(xn, wup_ref[:, cs].astype(_MXU_DTYPE))
        b = _dot(xn, wup_ref[:, D_FF + c * FFN_CHUNK:D_FF + (c + 1) * FFN_CHUNK].astype(_MXU_DTYPE))
        act_ref[:, cs] = (a * jax.nn.sigmoid(a) * b).astype(act_ref.dtype)
    y = x
    for c in range(D_FF // FFN_DOWN_CHUNK):
        rows = slice(c * FFN_DOWN_CHUNK, (c + 1) * FFN_DOWN_CHUNK)
        y = y + _dot(act_ref[:, rows], wdown_ref[rows, :].astype(_MXU_DTYPE))
    out_ref[...] = y


def _ffn(x2, g2, w_up, w_down, layer):
    M = x2.shape[0]
    tm = FFN_TM
    row = lambda i: (i, 0)
    layer_weight = lambda w: pl.BlockSpec((None,) + w.shape[1:], lambda i: (layer, 0, 0),
                                          pipeline_mode=pl.Buffered(1))
    return pl.pallas_call(
        _ffn_kernel,
        grid=(M // tm,),
        in_specs=[pl.BlockSpec((tm, D_MODEL), row), _const_spec((1, D_MODEL)),
                  layer_weight(w_up), layer_weight(w_down)],
        out_specs=pl.BlockSpec((tm, D_MODEL), row),
        out_shape=jax.ShapeDtypeStruct((M, D_MODEL), jnp.float32),
        scratch_shapes=[pltpu.VMEM((tm, D_FF), _MXU_DTYPE)],
        compiler_params=_params(("arbitrary",)),
        name="swiglu_ffn",
    )(x2, g2, w_up, w_down)


def _t5_bucket(rel):
    half = T5_BUCKETS // 2
    max_exact = half // 2
    ret = jnp.where(rel > 0, half, 0)
    n = jnp.abs(rel)
    nf = jnp.maximum(n, 1).astype(jnp.float32)
    large = max_exact + (jnp.log(nf / max_exact) / math.log(T5_MAX_DIST / max_exact)
                         * (half - max_exact)).astype(jnp.int32)
    large = jnp.minimum(large, half - 1)
    return ret + jnp.where(n < max_exact, n, large)


def _mixer_a_bias(table_g, rate):
    j = jnp.arange(A_SPAN)[:, None]
    i = jnp.arange(A_UNIT)[None, :]
    step = j - A_RADIUS - i
    onehot = (_t5_bucket(step * rate)[:, :, None] == jnp.arange(T5_BUCKETS)).astype(jnp.float32)
    bias = jnp.einsum("jib,bh->hji", onehot, table_g.astype(jnp.float32) * LOG2_E,
                      precision=lax.Precision.HIGHEST)
    return jnp.where((jnp.abs(step) <= A_RADIUS)[None], bias, NEG)


B_N_DR = 2 * B_WIN_ROWS - 1


def _mixer_b_blocks(rpb, shift):
    c = np.arange(GRID_W)
    c0 = np.clip(c - B_WIN_COLS // 2, 0, GRID_W - B_WIN_COLS)
    col_ok = (c[:, None] >= c0[None, :]) & (c[:, None] < c0[None, :] + B_WIN_COLS)
    dc = np.clip(c[:, None] - c[None, :] + B_WIN_COLS - 1, 0, 2 * B_WIN_COLS - 2)
    pick_c = ((dc[..., None] == np.arange(2 * B_WIN_COLS - 1)) & col_ok[..., None]).astype(np.float32)
    blocks = jnp.einsum("hdk,xyk->hdxy", rpb.astype(jnp.float32) * LOG2_E, pick_c,
                        precision=lax.Precision.HIGHEST)
    blocks = jnp.where(col_ok[None, None], blocks - shift, NEG)
    blocks = jnp.concatenate([blocks, jnp.full((B_HEADS, 1, GRID_W, GRID_W), NEG, jnp.float32)], axis=1)
    return jnp.concatenate([blocks, blocks], axis=-1)


def _mixer_b_block_index(rows):
    first_query_row = np.array([0, B_WIN_ROWS // 2, rows - B_UNIT_ROWS])
    i = first_query_row[:, None, None] + np.arange(B_UNIT_ROWS)[None, None, :]
    r0 = np.clip(first_query_row - B_WIN_ROWS // 2, 0, rows - B_SPAN_ROWS)
    ik = r0[:, None, None] + np.arange(B_SPAN_ROWS)[None, :, None]
    rs = np.clip(i - B_WIN_ROWS // 2, 0, rows - B_WIN_ROWS)
    row_ok = (ik >= rs) & (ik < rs + B_WIN_ROWS)
    return np.where(row_ok, ik - i + B_WIN_ROWS - 1, B_N_DR).astype(np.int32).reshape(-1)


def _rope_tables(S):
    rows = S // GRID_W
    inv = ROPE_THETA ** (-jnp.arange(0, ROPE_AXIS_DIM, 2, dtype=jnp.float32) / ROPE_AXIS_DIM)
    d = np.arange(LANES) % HEAD_DIM
    inv_lane = inv[d % (ROPE_AXIS_DIM // 2)][None, :]
    is_col = (d >= ROPE_AXIS_DIM)[None, None, :]
    first = ((d % ROPE_AXIS_DIM) < ROPE_AXIS_DIM // 2)[None, :]
    tables = []
    for n in (rows, GRID_W):
        ang = jnp.arange(n, dtype=jnp.float32)[:, None] * inv_lane
        sin = jnp.sin(ang)
        tables.append((jnp.cos(ang), jnp.where(first, -sin, 0.0), jnp.where(first, 0.0, sin)))
    return tuple(jnp.where(is_col, by_col[None, :, :], by_row[:, None, :]).reshape(S, LANES)
                 for by_row, by_col in zip(*tables))


def _softmax_shift(gain_q, gain_k, bias_abs_max):
    bound = (HEAD_DIM * QK_SCALE * LOG2_E * ROUNDING_SLACK * jnp.max(jnp.abs(gain_q)) * jnp.max(jnp.abs(gain_k))
             + LOG2_E * bias_abs_max)
    small = bound <= SHIFT_LIMIT
    return jnp.stack([jnp.where(small, bound, 0.0), small.astype(jnp.float32)]).astype(jnp.float32)


def _block_diag_ones():
    i = np.arange(PROJ_TILE) // HEAD_DIM
    return jnp.asarray(i[:, None] == i[None, :], _MXU_DTYPE)


def kernel(x, rel_bias_table, norm1, w_in, qk_gain, nat_rpb, w_br_a, w_br_b, w_br_c, w_o,
           norm2, w_up, w_down):
    B, S, D = x.shape
    depth = w_in.shape[0]
    M = B * S
    x2 = x.reshape(M, D)
    rope = _rope_tables(S)
    ones_bd = _block_diag_ones()
    a_bias = [_mixer_a_bias(rel_bias_table[:, g * A_HEADS:(g + 1) * A_HEADS], rate)
              for g, (_, rate) in enumerate(A_PATTERNS)]
    for l in range(depth):
        gq = qk_gain[l]
        tile = lambda g, n, s: jnp.tile(g * s, n)
        gain_row = jnp.concatenate([
            tile(gq[0], A_W // HEAD_DIM, QK_SCALE * LOG2_E), tile(gq[1], A_W // HEAD_DIM, 1.0),
            tile(gq[2], B_HEADS, QK_SCALE * LOG2_E), tile(gq[3], B_HEADS, 1.0),
            tile(gq[4], C_Q_HEADS, QK_SCALE * LOG2_E), tile(gq[5], C_KV_HEADS, 1.0)])[None, :]
        g1 = norm1[l][None, :]
        za0, za1, za2, qbt, kb, vbt, qc, kc, vc = _qkv_projection(
            x2, g1, w_in, l, gain_row, ones_bd, rope, B, S)
        za =(za0.reshape(B, 1, S, ZA_W), za1, za2)
        oa, lse = [], []
        for g, (_, rate) in enumerate(A_PATTERNS):
            table_g = rel_bias_table[:, g * A_HEADS:(g + 1) * A_HEADS]
            shift_a = _softmax_shift(gq[0], gq[1], jnp.max(jnp.abs(table_g)))
            o_g, l_g = _mixer_a_group(za[g], a_bias[g] - shift_a[0], shift_a, rate, B, S)
            oa.append(o_g)
            lse.append(l_g)
        shift_b = _softmax_shift(gq[2], gq[3], jnp.max(jnp.abs(nat_rpb[l])))
        ob = _mixer_b(qbt, kb, vbt, _mixer_b_blocks(nat_rpb[l], shift_b[0]), shift_b, B, S)
        oc = _mixer_c(qc, kc, vc, _softmax_shift(gq[4], gq[5], 0.0), B, S)
        x2 = _merge(x2, g1, oa, lse, ob, oc, w_in, w_br_a, w_br_b, w_br_c, w_o, l, B, S)
        x2 = _ffn(x2, norm2[l][None, :], w_up, w_down, l)
    return x2.reshape(B, S, D)
```

```python
import functools
import math

import jax
import jax.numpy as jnp
import numpy as np
from jax import lax
from jax.experimental import pallas as pl
from jax.experimental.pallas import tpu as pltpu

_MXU_DTYPE = jnp.bfloat16

D_MODEL = 1024
HEAD_DIM = 64
GRID_W = 64
RMS_EPS = 1e-6
NEG = -1e30
A_PATTERNS = ((128, 1), (512, 4), (2048, 16))
A_GROUPS = 3
A_HEADS = 4
A_W = A_GROUPS * A_HEADS * HEAD_DIM
A_OUT = A_HEADS * HEAD_DIM
A_RADIUS = 64
B_HEADS = 8
B_W = B_HEADS * HEAD_DIM
B_WIN_ROWS = 8
B_WIN_COLS = 16
C_Q_HEADS = 8
C_KV_HEADS = 2
C_GROUP = C_Q_HEADS // C_KV_HEADS
C_QW = C_Q_HEADS * HEAD_DIM
C_KVW = C_KV_HEADS * HEAD_DIM
ROPE_THETA = 10000.0
ROPE_AXIS_DIM = HEAD_DIM // 2
T5_BUCKETS = 32
T5_MAX_DIST = 1024
N_BRANCH = 3
D_FF = math.ceil(8 * D_MODEL / 3 / 256) * 256
QK_SCALE = HEAD_DIM ** -0.5
LOG2_E = math.log2(math.e)

V7X_VMEM_BYTES = 64 * 1024 * 1024
VMEM_LIMIT_BYTES = V7X_VMEM_BYTES * 7 // 8
LANES = 128
BF16_SUBLANES = 16
MXU_COLS = 256
VT_ROWS = HEAD_DIM + BF16_SUBLANES
SHIFT_LIMIT = 60.0
ROUNDING_SLACK = 1.02

_OFF = np.cumsum([0, A_W, A_W, A_W, B_W, B_W, B_W, C_QW, C_KVW, C_KVW]).tolist()
(_QA, _KA, _VA, _QB, _KB, _VB, _QC, _KC, _VC, _ZG) = _OFF

ZA_W = 3 * A_OUT
B_PAIRS = B_HEADS // 2


def _params(sem):
    return pltpu.CompilerParams(dimension_semantics=sem, vmem_limit_bytes=VMEM_LIMIT_BYTES)


def _const_spec(shape):
    nd = len(shape)
    return pl.BlockSpec(shape, lambda *_: (0,) * nd, pipeline_mode=pl.Buffered(1))


def _rms(x, g):
    return x * lax.rsqrt(jnp.mean(x * x, axis=-1, keepdims=True) + RMS_EPS) * g


def _dot(a, b):
    return jnp.dot(a, b, preferred_element_type=jnp.float32)


def _dot_nt(a, b):
    return lax.dot_general(a, b, (((1,), (1,)), ((), ())), preferred_element_type=jnp.float32)


def _ones_tail(width, dtype):
    return (lax.broadcasted_iota(jnp.int32, (VT_ROWS - HEAD_DIM, width), 0) == 0).astype(dtype)


PROJ_TM = 1024
PROJ_TILE = MXU_COLS
PROJ_W_BLOCK = 1536
_N_NORM = 2 * A_W + 2 * B_W
_N_ROPE = C_QW + C_KVW


def _proj_kernel(x_ref, g_ref, w0_ref, w1_ref, w2_ref, gain_ref, ones_ref, cos_ref, s1_ref, s2_ref,
                 za0_ref, za1_ref, za2_ref, qbt_ref, kb_ref, vbt_ref, qc_ref, kc_ref, vc_ref, dil_ref):
    tm = x_ref.shape[0]
    w_refs = (w0_ref, w1_ref, w2_ref)
    h = _rms(x_ref[...], g_ref[...]).astype(_MXU_DTYPE)
    za_refs = (za0_ref, za1_ref, za2_ref)

    def head_norm(acc, c0, width):
        sq = (acc * acc).astype(_MXU_DTYPE)
        ms = _dot(sq, ones_ref[:width, :width]) * (1.0 / HEAD_DIM)
        return acc * lax.rsqrt(ms + RMS_EPS) * gain_ref[:, c0:c0 + width]

    def store_group(g, section, val):
        cols = slice(section * A_OUT, (section + 1) * A_OUT)
        rate = A_PATTERNS[g][1]
        if rate == 1:
            za0_ref[:, cols] = val.astype(za0_ref.dtype)
            return
        for j in range(A_OUT // LANES):
            dil_ref[j] = val[:, j * LANES:(j + 1) * LANES]
        for r in range(rate):
            picked = [dil_ref[j, pl.ds(r, tm // rate, stride=rate), :] for j in range(A_OUT // LANES)]
            za_refs[g][0, r, :, cols] = jnp.concatenate(picked, axis=1).astype(za_refs[g].dtype)

    def store_pairs(ref, t, val):
        vt = val.T.astype(ref.dtype)
        ref[0, 2 * t] = vt[:LANES]
        ref[0, 2 * t + 1] = vt[LANES:]

    def rotary(y, width):
        reps = width // LANES
        table = lambda ref: jnp.concatenate([ref[...]] * reps, axis=1) if reps > 1 else ref[...]
        return (y * table(cos_ref) + pltpu.roll(y, width - 16, 1) * table(s1_ref)
                + pltpu.roll(y, 16, 1) * table(s2_ref))

    def store_qc(t, acc, gain_at):
        yt = rotary(head_norm(acc, gain_at, PROJ_TILE), PROJ_TILE).T.astype(qc_ref.dtype)
        heads = PROJ_TILE // HEAD_DIM
        for j in range(heads):
            qc_ref[0, heads * t + j] = yt[j * HEAD_DIM:(j + 1) * HEAD_DIM]

    def store_kc_vc(acc, gain_at):
        y = rotary(head_norm(acc[:, :LANES], gain_at, LANES), LANES).astype(kc_ref.dtype)
        acc_t = acc[:, LANES:].T.astype(vc_ref.dtype)
        tail = _ones_tail(tm, vc_ref.dtype)
        for kv in range(C_KV_HEADS):
            kc_ref[0, kv] = y[:, kv * HEAD_DIM:(kv + 1) * HEAD_DIM]
            vc_ref[0, kv, :HEAD_DIM] = acc_t[kv * HEAD_DIM:(kv + 1) * HEAD_DIM]
            vc_ref[0, kv, HEAD_DIM:] = tail

    jobs = []
    for section, src in enumerate((_QA, _KA)):
        for g in range(A_GROUPS):
            gain_at = section * A_W + g * A_OUT
            jobs.append((src + g * A_OUT, A_OUT, lambda acc, gain_at=gain_at, g=g, section=section:
                         store_group(g, section, head_norm(acc, gain_at, A_OUT))))
    for t in range(B_W // PROJ_TILE):
        gain_at = 2 * A_W + t * PROJ_TILE
        jobs.append((_QB + t * PROJ_TILE, PROJ_TILE, lambda acc, gain_at=gain_at, t=t:
                     store_pairs(qbt_ref, t, head_norm(acc, gain_at, PROJ_TILE))))
    for t in range(B_W // PROJ_TILE):
        gain_at = 2 * A_W + B_W + t * PROJ_TILE
        def store_kb(acc, gain_at=gain_at, t=t):
            kb_ref[:, t * PROJ_TILE:(t + 1) * PROJ_TILE] = head_norm(acc, gain_at, PROJ_TILE).astype(kb_ref.dtype)
        jobs.append((_KB + t * PROJ_TILE, PROJ_TILE, store_kb))
    for t in range(C_QW // PROJ_TILE):
        jobs.append((_QC + t * PROJ_TILE, PROJ_TILE,
                     lambda acc, t=t: store_qc(t, acc, _N_NORM + t * PROJ_TILE)))
    jobs.append((_KC, 2 * C_KVW, lambda acc: store_kc_vc(acc, _N_NORM + C_QW)))
    for g in range(A_GROUPS):
        jobs.append((_VA + g * A_OUT, A_OUT, lambda acc, g=g: store_group(g, 2, acc)))
    for t in range(B_W // PROJ_TILE):
        jobs.append((_VB + t * PROJ_TILE, PROJ_TILE, lambda acc, t=t: store_pairs(vbt_ref, t, acc)))

    def product(n):
        src, width, _ = jobs[n]
        blk, col = divmod(src, PROJ_W_BLOCK)
        return _dot(h, w_refs[blk][:, col:col + width].astype(_MXU_DTYPE))

    acc_next = product(0)
    for n, (_, _, consume) in enumerate(jobs):
        acc = acc_next
        if n + 1 < len(jobs):
            acc_next = product(n + 1)
        consume(acc)


def _qkv_projection(x2, g1, w_in, layer, gain_row, ones_bd, rope, B, S):
    M = x2.shape[0]
    tm = PROJ_TM
    per_seq = S // tm
    w_block = lambda j: pl.BlockSpec((None, D_MODEL, PROJ_W_BLOCK), lambda i: (layer, 0, j),
                                     pipeline_mode=pl.Buffered(1))
    cos_t, s1_t, s2_t = rope
    r1, r2 = A_PATTERNS[1][1], A_PATTERNS[2][1]
    row = lambda i: (i, 0)
    pos = lambda i: (i % per_seq, 0)
    hm = lambda i: (i // per_seq, 0, i % per_seq, 0)
    hm_t = lambda i: (i // per_seq, 0, 0, i % per_seq)
    cd = _MXU_DTYPE
    return pl.pallas_call(
        _proj_kernel,
        grid=(M // tm,),
        in_specs=[
            pl.BlockSpec((tm, D_MODEL), row),
            _const_spec((1, D_MODEL)),
            w_block(0), w_block(1), w_block(2),
            _const_spec((1, _N_NORM + _N_ROPE)),
            _const_spec((PROJ_TILE, PROJ_TILE)),
            pl.BlockSpec((tm, LANES), pos),
            pl.BlockSpec((tm, LANES), pos),
            pl.BlockSpec((tm, LANES), pos),
        ],
        out_specs=[
            pl.BlockSpec((tm, ZA_W), row),
            pl.BlockSpec((1, r1, tm // r1, ZA_W), hm),
            pl.BlockSpec((1, r2, tm // r2, ZA_W), hm),
            pl.BlockSpec((1, B_PAIRS, LANES, tm), hm_t),
            pl.BlockSpec((tm, B_W), row),
            pl.BlockSpec((1, B_PAIRS, LANES, tm), hm_t),
            pl.BlockSpec((1, C_Q_HEADS, HEAD_DIM, tm), hm_t),
            pl.BlockSpec((1, C_KV_HEADS, tm, HEAD_DIM), hm),
            pl.BlockSpec((1, C_KV_HEADS, VT_ROWS, tm), hm_t),
        ],
        out_shape=[
            jax.ShapeDtypeStruct((M, ZA_W), cd),
            jax.ShapeDtypeStruct((B, r1, S // r1, ZA_W), cd),
            jax.ShapeDtypeStruct((B, r2, S // r2, ZA_W), cd),
            jax.ShapeDtypeStruct((B, B_PAIRS, LANES, S), cd),
            jax.ShapeDtypeStruct((M, B_W), cd),
            jax.ShapeDtypeStruct((B, B_PAIRS, LANES, S), cd),
            jax.ShapeDtypeStruct((B, C_Q_HEADS, HEAD_DIM, S), cd),
            jax.ShapeDtypeStruct((B, C_KV_HEADS, S, HEAD_DIM), cd),
            jax.ShapeDtypeStruct((B, C_KV_HEADS, VT_ROWS, S), cd),
        ],
        scratch_shapes=[pltpu.VMEM((A_OUT // LANES, tm, LANES), jnp.float32)],
        compiler_params=_params(("arbitrary",)),
        name="qkv_projection",
    )(x2, g1, w_in, w_in, w_in, gain_row, ones_bd, cos_t, s1_t, s2_t)


A_UNIT = 4 * A_RADIUS
A_SPAN = A_UNIT + 2 * A_RADIUS
A_TILE = 2048


def _mixer_a_kernel(q_ref, kp_ref, ko_ref, kn_ref, vp_ref, vo_ref, vn_ref, bias_ref, shift_ref,
                    o_ref, lse_ref, kw_ref, vw_ref, s_ref, *, tl, seq_len):
    n_res = q_ref.shape[1]
    l0 = pl.program_id(2) * tl
    for res in range(n_res):
        kw_ref[res, 0:A_RADIUS] = kp_ref[0, res]
        kw_ref[res, A_RADIUS:A_RADIUS + tl] = ko_ref[0, res]
        kw_ref[res, A_RADIUS + tl:] = kn_ref[0, res]
        vw_ref[res, 0:A_RADIUS] = vp_ref[0, res]
        vw_ref[res, A_RADIUS:A_RADIUS + tl] = vo_ref[0, res]
        vw_ref[res, A_RADIUS + tl:] = vn_ref[0, res]
    even_q = lax.broadcasted_iota(jnp.int32, (A_UNIT, LANES), 1) < HEAD_DIM
    lane_k = lax.broadcasted_iota(jnp.int32, (A_SPAN, LANES), 1)

    per_res = tl // A_UNIT
    n_units = n_res * per_res

    def place(n):
        return n // per_res, pl.multiple_of((n % per_res) * A_UNIT, A_UNIT)

    def scores(n, pair):
        res, off = place(n)
        q = q_ref[0, res, pl.ds(off, A_UNIT), pair * LANES:(pair + 1) * LANES]
        zero = jnp.zeros_like(q)
        q2 = jnp.concatenate([jnp.where(even_q, q, zero), jnp.where(even_q, zero, q)], axis=0)
        return _dot_nt(kw_ref[res, pl.ds(off, A_SPAN), pair * LANES:(pair + 1) * LANES], q2)

    s_ref[...] = scores(0, 0)

    def unit(fixed_shift, u, carry):
        res, off = place(u)
        top_ok = l0 + off - A_RADIUS >= 0
        bot_ok = l0 + off + A_UNIT + A_RADIUS <= seq_len
        outs, lses = [], []
        s_next = s_ref[...]
        for pair in range(A_HEADS // 2):
            s_pair = s_next
            if pair + 1 < A_HEADS // 2:
                s_next = scores(u, pair + 1)
            else:
                s_ref[...] = scores(jnp.minimum(u + 1, n_units - 1), 0)
            v_slab = vw_ref[res, pl.ds(off, A_SPAN), pair * LANES:(pair + 1) * LANES]
            for odd in range(2):
                s = s_pair[:, odd * A_UNIT:(odd + 1) * A_UNIT] + bias_ref[2 * pair + odd]
                s = jnp.concatenate([jnp.where(top_ok, s[:A_RADIUS], NEG), s[A_RADIUS:A_SPAN - A_RADIUS],
                                     jnp.where(bot_ok, s[A_SPAN - A_RADIUS:], NEG)], axis=0)
                if fixed_shift:
                    m = shift_ref[0]
                    p = jnp.exp2(s).astype(_MXU_DTYPE)
                else:
                    m = jnp.max(s, axis=0, keepdims=True)
                    p = jnp.exp2(s - m).astype(_MXU_DTYPE)
                den_row = (1 - odd) * HEAD_DIM
                mine = (lane_k >= odd * HEAD_DIM) & (lane_k < (odd + 1) * HEAD_DIM)
                v_aug = jnp.where(mine, v_slab, (lane_k == den_row).astype(v_slab.dtype))
                acc = lax.dot_general(v_aug, p, (((0,), (0,)), ((), ())),
                                      preferred_element_type=jnp.float32)
                den = acc[den_row:den_row + 1]
                outs.append(acc[odd * HEAD_DIM:(odd + 1) * HEAD_DIM] / den)
                lses.append(jnp.broadcast_to(m + jnp.log2(den), (HEAD_DIM, A_UNIT)))
        o_ref[0, res, pl.ds(off, A_UNIT), :] = jnp.concatenate(outs, axis=0).T
        lse_ref[0, res, pl.ds(off, A_UNIT), :] = jnp.concatenate(lses, axis=0).T
        return carry

    small = shift_ref[1] > 0.5

    @pl.when(small)
    def _():
        lax.fori_loop(0, n_units, functools.partial(unit, True), 0, unroll=min(4, n_units))

    @pl.when(jnp.logical_not(small))
    def _():
        lax.fori_loop(0, n_units, functools.partial(unit, False), 0)


def _mixer_a_group(za_g, bias, shift, rate, B, S):
    L = S // rate
    tl = min(A_TILE, L)
    n_res = min(rate, A_TILE // tl)
    nblk = L // A_RADIUS
    per = tl // A_RADIUS

    def own(section):
        return lambda b, r, l: (b, r, l, section)

    def prev(section):
        return lambda b, r, l: (b, r, jnp.maximum(l * per - 1, 0), section)

    def nxt(section):
        return lambda b, r, l: (b, r, jnp.minimum((l + 1) * per, nblk - 1), section)

    edge = (1, n_res, A_RADIUS, A_OUT)
    full = (1, n_res, tl, A_OUT)
    return pl.pallas_call(
        functools.partial(_mixer_a_kernel, tl=tl, seq_len=L),
        grid=(B, rate // n_res, L // tl),
        in_specs=[
            pl.BlockSpec(full, own(0)),
            pl.BlockSpec(edge, prev(1)), pl.BlockSpec(full, own(1)), pl.BlockSpec(edge, nxt(1)),
            pl.BlockSpec(edge, prev(2)), pl.BlockSpec(full, own(2)), pl.BlockSpec(edge, nxt(2)),
            _const_spec((A_HEADS, A_SPAN, A_UNIT)),
            pl.BlockSpec(memory_space=pltpu.SMEM),
        ],
        out_specs=[pl.BlockSpec(full, own(0)), pl.BlockSpec(full, own(0))],
        out_shape=[jax.ShapeDtypeStruct((B, rate, L, A_OUT), jnp.float32)] * 2,
        scratch_shapes=[pltpu.VMEM((n_res, tl + 2 * A_RADIUS, A_OUT), _MXU_DTYPE)] * 2
        + [pltpu.VMEM((A_SPAN, 2 * A_UNIT), jnp.float32)],
        compiler_params=_params(("arbitrary",) * 3),
        name=f"mixer_a_rate{rate}",
    )(za_g, za_g, za_g, za_g, za_g, za_g, za_g, bias, shift)


B_UNIT_ROWS = 4
B_UNIT = B_UNIT_ROWS * GRID_W
B_SPAN_ROWS = B_UNIT_ROWS + B_WIN_ROWS
B_SPAN = B_SPAN_ROWS * GRID_W
B_HALO = (B_WIN_ROWS // 2) * GRID_W
B_TILE_ROWS = 32
B_TILE = B_TILE_ROWS * GRID_W


def _mixer_b_kernel(q_ref, kp_ref, ko_ref, kn_ref, vp_ref, vo_ref, vn_ref, blocks_ref, sel_ref, shift_ref,
                    o_ref, kw_ref, vw_ref, bias_ref, s_ref, *, rows):
    @pl.when((pl.program_id(0) == 0) & (pl.program_id(1) == 0))
    def _():
        left = lax.broadcasted_iota(jnp.int32, (GRID_W, LANES), 1) < GRID_W

        def fill(n, carry):
            v, h = n // B_HEADS, n % B_HEADS
            for a in range(B_SPAN_ROWS):
                for gp in range(B_UNIT_ROWS // 2):
                    at = (v * B_SPAN_ROWS + a) * B_UNIT_ROWS + 2 * gp
                    tile = jnp.where(left, blocks_ref[h, sel_ref[at]], blocks_ref[h, sel_ref[at + 1]])
                    bias_ref[v, h, a * GRID_W:(a + 1) * GRID_W, gp * LANES:(gp + 1) * LANES] = tile
            return carry

        lax.fori_loop(0, 3 * B_HEADS, fill, 0)

    i0 = pl.program_id(1) * B_TILE_ROWS
    kw_ref[0:B_HALO] = kp_ref[0]
    kw_ref[B_HALO:B_HALO + B_TILE] = ko_ref[0]
    kw_ref[B_HALO + B_TILE:] = kn_ref[0]
    vw_ref[:, :, 0:B_HALO] = vp_ref[0]
    vw_ref[:, :, B_HALO:B_HALO + B_TILE] = vo_ref[0]
    vw_ref[:, :, B_HALO + B_TILE:] = vn_ref[0]
    tail = _ones_tail(B_SPAN, _MXU_DTYPE)
    upper = lax.broadcasted_iota(jnp.int32, (LANES, B_UNIT), 0) < HEAD_DIM

    n_units = B_TILE_ROWS // B_UNIT_ROWS

    def span_offset(u):
        r0 = jnp.clip(i0 + u * B_UNIT_ROWS - B_WIN_ROWS // 2, 0, rows - B_SPAN_ROWS)
        return pl.multiple_of((r0 - (i0 - B_WIN_ROWS // 2)) * GRID_W, LANES)

    def scores(u, pair):
        qt = q_ref[0, pair, :, pl.ds(pl.multiple_of(u * B_UNIT, B_UNIT), B_UNIT)]
        zero = jnp.zeros_like(qt)
        qt2 = jnp.concatenate([jnp.where(upper, qt, zero), jnp.where(upper, zero, qt)], axis=1)
        return _dot(kw_ref[pl.ds(span_offset(u), B_SPAN), pair * LANES:(pair + 1) * LANES], qt2)

    s_ref[...] = scores(0, 0)

    def unit(fixed_shift, u, carry):
        i0u = i0 + u * B_UNIT_ROWS
        off = span_offset(u)
        variant = jnp.where(i0u == 0, 0, jnp.where(i0u == rows - B_UNIT_ROWS, 2, 1))
        qoff = pl.multiple_of(u * B_UNIT, B_UNIT)
        outs = []
        s_next = s_ref[...]
        for pair in range(B_PAIRS):
            s_pair = s_next
            if pair + 1 < B_PAIRS:
                s_next = scores(u, pair + 1)
            else:
                s_ref[...] = scores(jnp.minimum(u + 1, n_units - 1), 0)
            for odd in range(2):
                s = s_pair[:, odd * B_UNIT:(odd + 1) * B_UNIT] + bias_ref[variant, 2 * pair + odd]
                if not fixed_shift:
                    s = s - jnp.max(s, axis=0, keepdims=True)
                p = jnp.exp2(s).astype(_MXU_DTYPE)
                vt = vw_ref[pair, odd * HEAD_DIM:(odd + 1) * HEAD_DIM, pl.ds(off, B_SPAN)]
                acc = _dot(jnp.concatenate([vt, tail], axis=0), p)
                outs.append(acc[:HEAD_DIM] / acc[HEAD_DIM:HEAD_DIM + 1])
        o_ref[0, pl.ds(qoff, B_UNIT), :] = jnp.concatenate(outs, axis=0).T.astype(o_ref.dtype)
        return carry

    small = shift_ref[1] > 0.5

    @pl.when(small)
    def _():
        lax.fori_loop(0, n_units, functools.partial(unit, True), 0, unroll=n_units)

    @pl.when(jnp.logical_not(small))
    def _():
        lax.fori_loop(0, n_units, functools.partial(unit, False), 0)


def _mixer_b(qbt, kb, vbt, blocks, shift, B, S):
    rows = S // GRID_W
    sel = jnp.asarray(_mixer_b_block_index(rows))
    nt = rows // B_TILE_ROWS
    per = B_TILE // B_HALO
    nh = S // B_HALO
    kb3 = kb.reshape(B, S, B_W)
    prev = lambda t: jnp.maximum(t * per - 1, 0)
    nxt = lambda t: jnp.minimum((t + 1) * per, nh - 1)
    k_edge, k_own = (1, B_HALO, B_W), (1, B_TILE, B_W)
    t_edge, t_own = (1, B_PAIRS, LANES, B_HALO), (1, B_PAIRS, LANES, B_TILE)
    o = pl.pallas_call(
        functools.partial(_mixer_b_kernel, rows=rows),
        grid=(B, nt),
        in_specs=[
            pl.BlockSpec(t_own, lambda b, t: (b, 0, 0, t)),
            pl.BlockSpec(k_edge, lambda b, t: (b, prev(t), 0)),
            pl.BlockSpec(k_own, lambda b, t: (b, t, 0)),
            pl.BlockSpec(k_edge, lambda b, t: (b, nxt(t), 0)),
            pl.BlockSpec(t_edge, lambda b, t: (b, 0, 0, prev(t))),
            pl.BlockSpec(t_own, lambda b, t: (b, 0, 0, t)),
            pl.BlockSpec(t_edge, lambda b, t: (b, 0, 0, nxt(t))),
            _const_spec(blocks.shape),
            pl.BlockSpec(memory_space=pltpu.SMEM),
            pl.BlockSpec(memory_space=pltpu.SMEM),
        ],
        out_specs=pl.BlockSpec(k_own, lambda b, t: (b, t, 0)),
        out_shape=jax.ShapeDtypeStruct((B, S, B_W), _MXU_DTYPE),
        scratch_shapes=[pltpu.VMEM((B_TILE + 2 * B_HALO, B_W), _MXU_DTYPE),
                        pltpu.VMEM((B_PAIRS, LANES, B_TILE + 2 * B_HALO), _MXU_DTYPE),
                        pltpu.VMEM((3, B_HEADS, B_SPAN, B_UNIT), jnp.float32),
                        pltpu.VMEM((B_SPAN, 2 * B_UNIT), jnp.float32)],
        compiler_params=_params(("arbitrary",) * 2),
        name="mixer_b",
    )(qbt, kb3, kb3, kb3, vbt, vbt, vbt, blocks, sel, shift)
    return o.reshape(B * S, B_W)


C_TQ = 4096
C_TK = 256
C_UNIT = 512
C_UNROLL = 1


def _mixer_c_kernel(q_ref, k_ref, vt_ref, shift_ref, o_ref, qcat_ref, m_ref, acc_ref, s_ref, *, n_kv):
    acc_ref[...] = jnp.zeros(acc_ref.shape, jnp.float32)
    for g in range(C_GROUP):
        qcat_ref[:, g * C_TQ:(g + 1) * C_TQ] = q_ref[0, g]
    n_units = C_GROUP * C_TQ // C_UNIT

    def keys(j):
        return k_ref[0, 0, pl.ds(pl.multiple_of(j * C_TK, C_TK), C_TK), :]

    def scores(k, u):
        return _dot(k, qcat_ref[:, u * C_UNIT:(u + 1) * C_UNIT])

    def sweep(update, unroll):
        s_ref[...] = scores(keys(0), 0)

        def step(j, carry):
            k = keys(j)
            vt = vt_ref[0, 0, :, pl.ds(pl.multiple_of(j * C_TK, C_TK), C_TK)]
            s_next = s_ref[...]
            for u in range(n_units):
                s = s_next
                if u + 1 < n_units:
                    s_next = scores(k, u + 1)
                else:
                    s_ref[...] = scores(keys(jnp.minimum(j + 1, n_kv - 1)), 0)
                update(s, vt, slice(u * C_UNIT, (u + 1) * C_UNIT))
            return carry

        lax.fori_loop(0, n_kv, step, 0, unroll=unroll)

    def fixed_shift(s, vt, cols):
        acc_ref[:, cols] += _dot(vt, jnp.exp2(s - shift_ref[0]).astype(_MXU_DTYPE))

    def running_max(s, vt, cols):
        m_prev = m_ref[:, cols]
        m_new = jnp.maximum(m_prev, jnp.max(s, axis=0, keepdims=True))
        alpha = jnp.exp2(m_prev - m_new)
        p = jnp.exp2(s - m_new).astype(_MXU_DTYPE)
        acc_ref[:, cols] = alpha * acc_ref[:, cols] + _dot(vt, p)
        m_ref[:, cols] = m_new

    small = shift_ref[1] > 0.5

    @pl.when(small)
    def _():
        sweep(fixed_shift, C_UNROLL)

    @pl.when(jnp.logical_not(small))
    def _():
        m_ref[...] = jnp.full(m_ref.shape, -jnp.inf, jnp.float32)
        sweep(running_max, 1)

    o_t = jnp.concatenate(
        [acc_ref[:HEAD_DIM, g * C_TQ:(g + 1) * C_TQ] / acc_ref[HEAD_DIM:HEAD_DIM + 1, g * C_TQ:(g + 1) * C_TQ]
         for g in range(C_GROUP)], axis=0)
    o_ref[0] = o_t.T.astype(o_ref.dtype)


def _mixer_c(qc_t, kc, vc_t, shift, B, S):
    o = pl.pallas_call(
        functools.partial(_mixer_c_kernel, n_kv=S // C_TK),
        grid=(B, C_KV_HEADS, S // C_TQ),
        in_specs=[
            pl.BlockSpec((1, C_GROUP, HEAD_DIM, C_TQ), lambda b, kv, i: (b, kv, 0, i)),
            pl.BlockSpec((1, 1, S, HEAD_DIM), lambda b, kv, i: (b, kv, 0, 0)),
            pl.BlockSpec((1, 1, VT_ROWS, S), lambda b, kv, i: (b, kv, 0, 0)),
            pl.BlockSpec(memory_space=pltpu.SMEM),
        ],
        out_specs=pl.BlockSpec((1, C_TQ, C_GROUP * HEAD_DIM), lambda b, kv, i: (b, i, kv)),
        out_shape=jax.ShapeDtypeStruct((B, S, C_QW), _MXU_DTYPE),
        scratch_shapes=[pltpu.VMEM((HEAD_DIM, C_GROUP * C_TQ), _MXU_DTYPE),
                        pltpu.VMEM((1, C_GROUP * C_TQ), jnp.float32),
                        pltpu.VMEM((VT_ROWS, C_GROUP * C_TQ), jnp.float32),
                        pltpu.VMEM((C_TK, C_UNIT), jnp.float32)],
        compiler_params=_params(("arbitrary",) * 3),
        name="mixer_c",
    )(qc_t, kc, vc_t, shift)
    return o.reshape(B * S, C_QW)


MERGE_TM = 512
MERGE_GATE_BLOCK = 512


def _merge_kernel(x_ref, g_ref, oa0_ref, oa1_ref, oa2_ref, l0_ref, l1_ref, l2_ref, ob_ref, oc_ref,
                  wg0_ref, wg1_ref, wg2_ref, wg3_ref, wg4_ref, wg5_ref, pa_ref, pb_ref, pc_ref, wo_ref,
                  out_ref, *scratch):
    tm = x_ref.shape[0]
    gate_refs = (wg0_ref, wg1_ref, wg2_ref, wg3_ref, wg4_ref, wg5_ref)
    cast = lambda ref: ref[...].astype(_MXU_DTYPE)

    def token_major(ref, scr):
        rate = ref.shape[1]
        halves = range(A_OUT // LANES)
        for r in range(rate):
            for j in halves:
                scr[j, pl.ds(r, tm // rate, stride=rate), :] = ref[0, r, :, j * LANES:(j + 1) * LANES]
        return jnp.concatenate([scr[j] for j in halves], axis=1)

    x = x_ref[...]
    h = _rms(x, g_ref[...]).astype(_MXU_DTYPE)
    n_parts = D_MODEL // MERGE_GATE_BLOCK
    branches = [None, _dot(ob_ref[...], cast(pb_ref)), _dot(oc_ref[...], cast(pc_ref))]
    gates = {(b, part): jax.nn.sigmoid(_dot(h, cast(gate_refs[b * n_parts + part])))
             for b in (1, 2, 0) for part in range(n_parts)}
    oa0, l0 = oa0_ref[...], l0_ref[...]
    oa1, l1 = token_major(oa1_ref, scratch[0]), token_major(l1_ref, scratch[1])
    oa2, l2 = token_major(oa2_ref, scratch[2]), token_major(l2_ref, scratch[3])
    mx = jnp.maximum(jnp.maximum(l0, l1), l2)
    w0, w1, w2 = jnp.exp2(l0 - mx), jnp.exp2(l1 - mx), jnp.exp2(l2 - mx)
    o_a = (w0 * oa0 + w1 * oa1 + w2 * oa2) / (w0 + w1 + w2)
    branches[0] = _dot(o_a.astype(_MXU_DTYPE), cast(pa_ref))
    parts = []
    for part in range(n_parts):
        cols = slice(part * MERGE_GATE_BLOCK, (part + 1) * MERGE_GATE_BLOCK)
        merged = sum(gates[b, part] * branches[b][:, cols] for b in range(1, N_BRANCH))
        parts.append((merged + gates[0, part] * branches[0][:, cols]).astype(_MXU_DTYPE))
    out_ref[...] = x + _dot(jnp.concatenate(parts, axis=1), cast(wo_ref))


def _merge(x2, g1, oa, lse, ob, oc, w_in, w_br_a, w_br_b, w_br_c, w_o, layer, B, S):
    M = x2.shape[0]
    tm = MERGE_TM
    per_seq = S // tm
    row = lambda i: (i, 0)
    tile = lambda w: pl.BlockSpec((tm, w), row)
    gate0 = _ZG // MERGE_GATE_BLOCK

    def layer_weight(w, block=None, at=0):
        block = w.shape[2] if block is None else block
        return pl.BlockSpec((None, w.shape[1], block), lambda i: (layer, 0, at), pipeline_mode=pl.Buffered(1))

    def dilated(rate):
        return pl.BlockSpec((1, rate, tm // rate, A_OUT), lambda i: (i // per_seq, 0, i % per_seq, 0))

    r1, r2 = A_PATTERNS[1][1], A_PATTERNS[2][1]
    return pl.pallas_call(
        _merge_kernel,
        grid=(M // tm,),
        in_specs=[tile(D_MODEL), _const_spec((1, D_MODEL)),
                  tile(A_OUT), dilated(r1), dilated(r2), tile(A_OUT), dilated(r1), dilated(r2),
                  tile(B_W), tile(C_QW),
                  *[layer_weight(w_in, MERGE_GATE_BLOCK, gate0 + j)
                    for j in range(N_BRANCH * D_MODEL // MERGE_GATE_BLOCK)],
                  layer_weight(w_br_a), layer_weight(w_br_b), layer_weight(w_br_c), layer_weight(w_o)],
        out_specs=tile(D_MODEL),
        out_shape=jax.ShapeDtypeStruct((M, D_MODEL), jnp.float32),
        scratch_shapes=[pltpu.VMEM((A_OUT // LANES, tm, LANES), jnp.float32)] * 4,
        compiler_params=_params(("arbitrary",)),
        name="gated_merge",
    )(x2, g1, oa[0].reshape(M, A_OUT), oa[1], oa[2], lse[0].reshape(M, A_OUT), lse[1], lse[2],
      ob, oc, *([w_in] * (N_BRANCH * D_MODEL // MERGE_GATE_BLOCK)), w_br_a, w_br_b, w_br_c, w_o)


FFN_TM = 512
FFN_CHUNK = 256
FFN_DOWN_CHUNK = D_FF // 2


def _ffn_kernel(x_ref, g_ref, wup_ref, wdown_ref, out_ref, act_ref):
    x = x_ref[...]
    xn = _rms(x, g_ref[...]).astype(_MXU_DTYPE)
    for c in range(D_FF // FFN_CHUNK):
        cs = slice(c * FFN_CHUNK, (c + 1) * FFN_CHUNK)
        a = _dot(xn, wup_ref[:, cs].astype(_MXU_DTYPE))
        b = _dot(xn, wup_ref[:, D_FF + c * FFN_CHUNK:D_FF + (c + 1) * FFN_CHUNK].astype(_MXU_DTYPE))
        act_ref[:, cs] = (a * jax.nn.sigmoid(a) * b).astype(act_ref.dtype)
    y = x
    for c in range(D_FF // FFN_DOWN_CHUNK):
        rows = slice(c * FFN_DOWN_CHUNK, (c + 1) * FFN_DOWN_CHUNK)
        y = y + _dot(act_ref[:, rows], wdown_ref[rows, :].astype(_MXU_DTYPE))
    out_ref[...] = y


def _ffn(x2, g2, w_up, w_down, layer):
    M = x2.shape[0]
    tm = FFN_TM
    row = lambda i: (i, 0)
    layer_weight = lambda w: pl.BlockSpec((None,) + w.shape[1:], lambda i: (layer, 0, 0),
                                          pipeline_mode=pl.Buffered(1))
    return pl.pallas_call(
        _ffn_kernel,
        grid=(M // tm,),
        in_specs=[pl.BlockSpec((tm, D_MODEL), row), _const_spec((1, D_MODEL)),
                  layer_weight(w_up), layer_weight(w_down)],
        out_specs=pl.BlockSpec((tm, D_MODEL), row),
        out_shape=jax.ShapeDtypeStruct((M, D_MODEL), jnp.float32),
        scratch_shapes=[pltpu.VMEM((tm, D_FF), _MXU_DTYPE)],
        compiler_params=_params(("arbitrary",)),
        name="swiglu_ffn",
    )(x2, g2, w_up, w_down)


def _t5_bucket(rel):
    half = T5_BUCKETS // 2
    max_exact = half // 2
    ret = jnp.where(rel > 0, half, 0)
    n = jnp.abs(rel)
    nf = jnp.maximum(n, 1).astype(jnp.float32)
    large = max_exact + (jnp.log(nf / max_exact) / math.log(T5_MAX_DIST / max_exact)
                         * (half - max_exact)).astype(jnp.int32)
    large = jnp.minimum(large, half - 1)
    return ret + jnp.where(n < max_exact, n, large)


def _mixer_a_bias(table_g, rate):
    j = jnp.arange(A_SPAN)[:, None]
    i = jnp.arange(A_UNIT)[None, :]
    step = j - A_RADIUS - i
    onehot = (_t5_bucket(step * rate)[:, :, None] == jnp.arange(T5_BUCKETS)).astype(jnp.float32)
    bias = jnp.einsum("jib,bh->hji", onehot, table_g.astype(jnp.float32) * LOG2_E,
                      precision=lax.Precision.HIGHEST)
    return jnp.where((jnp.abs(step) <= A_RADIUS)[None], bias, NEG)


B_N_DR = 2 * B_WIN_ROWS - 1


def _mixer_b_blocks(rpb, shift):
    c = np.arange(GRID_W)
    c0 = np.clip(c - B_WIN_COLS // 2, 0, GRID_W - B_WIN_COLS)
    col_ok = (c[:, None] >= c0[None, :]) & (c[:, None] < c0[None, :] + B_WIN_COLS)
    dc = np.clip(c[:, None] - c[None, :] + B_WIN_COLS - 1, 0, 2 * B_WIN_COLS - 2)
    pick_c = ((dc[..., None] == np.arange(2 * B_WIN_COLS - 1)) & col_ok[..., None]).astype(np.float32)
    blocks = jnp.einsum("hdk,xyk->hdxy", rpb.astype(jnp.float32) * LOG2_E, pick_c,
                        precision=lax.Precision.HIGHEST)
    blocks = jnp.where(col_ok[None, None], blocks - shift, NEG)
    blocks = jnp.concatenate([blocks, jnp.full((B_HEADS, 1, GRID_W, GRID_W), NEG, jnp.float32)], axis=1)
    return jnp.concatenate([blocks, blocks], axis=-1)


def _mixer_b_block_index(rows):
    first_query_row = np.array([0, B_WIN_ROWS // 2, rows - B_UNIT_ROWS])
    i = first_query_row[:, None, None] + np.arange(B_UNIT_ROWS)[None, None, :]
    r0 = np.clip(first_query_row - B_WIN_ROWS // 2, 0, rows - B_SPAN_ROWS)
    ik = r0[:, None, None] + np.arange(B_SPAN_ROWS)[None, :, None]
    rs = np.clip(i - B_WIN_ROWS // 2, 0, rows - B_WIN_ROWS)
    row_ok = (ik >= rs) & (ik < rs + B_WIN_ROWS)
    return np.where(row_ok, ik - i + B_WIN_ROWS - 1, B_N_DR).astype(np.int32).reshape(-1)


def _rope_tables(S):
    rows = S // GRID_W
    inv = ROPE_THETA ** (-jnp.arange(0, ROPE_AXIS_DIM, 2, dtype=jnp.float32) / ROPE_AXIS_DIM)
    d = np.arange(LANES) % HEAD_DIM
    inv_lane = inv[d % (ROPE_AXIS_DIM // 2)][None, :]
    is_col = (d >= ROPE_AXIS_DIM)[None, None, :]
    first = ((d % ROPE_AXIS_DIM) < ROPE_AXIS_DIM // 2)[None, :]
    tables = []
    for n in (rows, GRID_W):
        ang = jnp.arange(n, dtype=jnp.float32)[:, None] * inv_lane
        sin = jnp.sin(ang)
        tables.append((jnp.cos(ang), jnp.where(first, -sin, 0.0), jnp.where(first, 0.0, sin)))
    return tuple(jnp.where(is_col, by_col[None, :, :], by_row[:, None, :]).reshape(S, LANES)
                 for by_row, by_col in zip(*tables))


def _softmax_shift(gain_q, gain_k, bias_abs_max):
    bound = (HEAD_DIM * QK_SCALE * LOG2_E * ROUNDING_SLACK * jnp.max(jnp.abs(gain_q)) * jnp.max(jnp.abs(gain_k))
             + LOG2_E * bias_abs_max)
    small = bound <= SHIFT_LIMIT
    return jnp.stack([jnp.where(small, bound, 0.0), small.astype(jnp.float32)]).astype(jnp.float32)


def _block_diag_ones():
    i = np.arange(PROJ_TILE) // HEAD_DIM
    return jnp.asarray(i[:, None] == i[None, :], _MXU_DTYPE)


def kernel(x, rel_bias_table, norm1, w_in, qk_gain, nat_rpb, w_br_a, w_br_b, w_br_c, w_o,
           norm2, w_up, w_down):
    B, S, D = x.shape
    depth = w_in.shape[0]
    M = B * S
    x2 = x.reshape(M, D)
    rope = _rope_tables(S)
    ones_bd = _block_diag_ones()
    a_bias = [_mixer_a_bias(rel_bias_table[:, g * A_HEADS:(g + 1) * A_HEADS], rate)
              for g, (_, rate) in enumerate(A_PATTERNS)]
    for l in range(depth):
        gq = qk_gain[l]
        tile = lambda g, n, s: jnp.tile(g * s, n)
        gain_row = jnp.concatenate([
            tile(gq[0], A_W // HEAD_DIM, QK_SCALE * LOG2_E), tile(gq[1], A_W // HEAD_DIM, 1.0),
            tile(gq[2], B_HEADS, QK_SCALE * LOG2_E), tile(gq[3], B_HEADS, 1.0),
            tile(gq[4], C_Q_HEADS, QK_SCALE * LOG2_E), tile(gq[5], C_KV_HEADS, 1.0)])[None, :]
        g1 = norm1[l][None, :]
        za0, za1, za2, qbt, kb, vbt, qc, kc, vc = _qkv_projection(
            x2, g1, w_in, l, gain_row, ones_bd, rope, B, S)
        za =(za0.reshape(B, 1, S, ZA_W), za1, za2)
        oa, lse = [], []
        for g, (_, rate) in enumerate(A_PATTERNS):
            table_g = rel_bias_table[:, g * A_HEADS:(g + 1) * A_HEADS]
            shift_a = _softmax_shift(gq[0], gq[1], jnp.max(jnp.abs(table_g)))
            o_g, l_g = _mixer_a_group(za[g], a_bias[g] - shift_a[0], shift_a, rate, B, S)
            oa.append(o_g)
            lse.append(l_g)
        shift_b = _softmax_shift(gq[2], gq[3], jnp.max(jnp.abs(nat_rpb[l])))
        ob = _mixer_b(qbt, kb, vbt, _mixer_b_blocks(nat_rpb[l], shift_b[0]), shift_b, B, S)
        oc = _mixer_c(qc, kc, vc, _softmax_shift(gq[4], gq[5], 0.0), B, S)
        x2 = _merge(x2, g1, oa, lse, ob, oc, w_in, w_br_a, w_br_b, w_br_c, w_o, l, B, S)
        x2 = _ffn(x2, norm2[l][None, :], w_up, w_down, l)
    return x2.reshape(B, S, D)
```

```python
import functools
import math

import jax
import jax.numpy as jnp
import numpy as np
from jax import lax
from jax.experimental import pallas as pl
from jax.experimental.pallas import tpu as pltpu

_MXU_DTYPE = jnp.bfloat16

D_MODEL = 1024
HEAD_DIM = 64
GRID_W = 64
RMS_EPS = 1e-6
NEG = -1e30
A_PATTERNS = ((128, 1), (512, 4), (2048, 16))
A_GROUPS = 3
A_HEADS = 4
A_W = A_GROUPS * A_HEADS * HEAD_DIM
A_OUT = A_HEADS * HEAD_DIM
A_RADIUS = 64
B_HEADS = 8
B_W = B_HEADS * HEAD_DIM
B_WIN_ROWS = 8
B_WIN_COLS = 16
C_Q_HEADS = 8
C_KV_HEADS = 2
C_GROUP = C_Q_HEADS // C_KV_HEADS
C_QW = C_Q_HEADS * HEAD_DIM
C_KVW = C_KV_HEADS * HEAD_DIM
ROPE_THETA = 10000.0
ROPE_AXIS_DIM = HEAD_DIM // 2
T5_BUCKETS = 32
T5_MAX_DIST = 1024
N_BRANCH = 3
D_FF = math.ceil(8 * D_MODEL / 3 / 256) * 256
QK_SCALE = HEAD_DIM ** -0.5
LOG2_E = math.log2(math.e)

V7X_VMEM_BYTES = 64 * 1024 * 1024
VMEM_LIMIT_BYTES = V7X_VMEM_BYTES * 7 // 8
LANES = 128
BF16_SUBLANES = 16
MXU_COLS = 256
VT_ROWS = HEAD_DIM + BF16_SUBLANES
SHIFT_LIMIT = 60.0
ROUNDING_SLACK = 1.02

_OFF = np.cumsum([0, A_W, A_W, A_W, B_W, B_W, B_W, C_QW, C_KVW, C_KVW]).tolist()
(_QA, _KA, _VA, _QB, _KB, _VB, _QC, _KC, _VC, _ZG) = _OFF

ZA_W = 3 * A_OUT
B_PAIRS = B_HEADS // 2


def _params(sem):
    return pltpu.CompilerParams(dimension_semantics=sem, vmem_limit_bytes=VMEM_LIMIT_BYTES)


def _const_spec(shape):
    nd = len(shape)
    return pl.BlockSpec(shape, lambda *_: (0,) * nd, pipeline_mode=pl.Buffered(1))


def _rms(x, g):
    return x * lax.rsqrt(jnp.mean(x * x, axis=-1, keepdims=True) + RMS_EPS) * g


def _dot(a, b):
    return jnp.dot(a, b, preferred_element_type=jnp.float32)


def _dot_nt(a, b):
    return lax.dot_general(a, b, (((1,), (1,)), ((), ())), preferred_element_type=jnp.float32)


def _ones_tail(width, dtype):
    return (lax.broadcasted_iota(jnp.int32, (VT_ROWS - HEAD_DIM, width), 0) == 0).astype(dtype)


PROJ_TM = 1024
PROJ_TILE = MXU_COLS
PROJ_W_BLOCK = 1536
_N_NORM = 2 * A_W + 2 * B_W
_N_ROPE = C_QW + C_KVW


def _proj_kernel(x_ref, g_ref, w0_ref, w1_ref, w2_ref, gain_ref, ones_ref, cos_ref, s1_ref, s2_ref,
                 za0_ref, za1_ref, za2_ref, qbt_ref, kb_ref, vbt_ref, qc_ref, kc_ref, vc_ref, dil_ref):
    tm = x_ref.shape[0]
    w_refs = (w0_ref, w1_ref, w2_ref)
    halves = [_rms(x_ref[r * (tm // 2):(r + 1) * (tm // 2)], g_ref[...]).astype(_MXU_DTYPE) for r in range(2)]
    za_refs = (za0_ref, za1_ref, za2_ref)

    def head_norm(acc, c0, width):
        sq = (acc * acc).astype(_MXU_DTYPE)
        ms = _dot(sq, ones_ref[:width, :width]) * (1.0 / HEAD_DIM)
        return acc * lax.rsqrt(ms + RMS_EPS) * gain_ref[:, c0:c0 + width]

    def store_group(g, section, val):
        cols = slice(section * A_OUT, (section + 1) * A_OUT)
        rate = A_PATTERNS[g][1]
        if rate == 1:
            za0_ref[:, cols] = val.astype(za0_ref.dtype)
            return
        for j in range(A_OUT // LANES):
            dil_ref[j] = val[:, j * LANES:(j + 1) * LANES]
        for r in range(rate):
            picked = [dil_ref[j, pl.ds(r, tm // rate, stride=rate), :] for j in range(A_OUT // LANES)]
            za_refs[g][0, r, :, cols] = jnp.concatenate(picked, axis=1).astype(za_refs[g].dtype)

    def store_pairs(ref, t, val):
        vt = val.T.astype(ref.dtype)
        ref[0, 2 * t] = vt[:LANES]
        ref[0, 2 * t + 1] = vt[LANES:]

    def rotary(y, width):
        reps = width // LANES
        table = lambda ref: jnp.concatenate([ref[...]] * reps, axis=1) if reps > 1 else ref[...]
        return (y * table(cos_ref) + pltpu.roll(y, width - 16, 1) * table(s1_ref)
                + pltpu.roll(y, 16, 1) * table(s2_ref))

    def store_qc(t, acc, gain_at):
        yt = rotary(head_norm(acc, gain_at, PROJ_TILE), PROJ_TILE).T.astype(qc_ref.dtype)
        heads = PROJ_TILE // HEAD_DIM
        for j in range(heads):
            qc_ref[0, heads * t + j] = yt[j * HEAD_DIM:(j + 1) * HEAD_DIM]

    def store_kc_vc(acc, gain_at):
        y = rotary(head_norm(acc[:, :LANES], gain_at, LANES), LANES).astype(kc_ref.dtype)
        acc_t = acc[:, LANES:].T.astype(vc_ref.dtype)
        tail = _ones_tail(tm, vc_ref.dtype)
        for kv in range(C_KV_HEADS):
            kc_ref[0, kv] = y[:, kv * HEAD_DIM:(kv + 1) * HEAD_DIM]
            vc_ref[0, kv, :HEAD_DIM] = acc_t[kv * HEAD_DIM:(kv + 1) * HEAD_DIM]
            vc_ref[0, kv, HEAD_DIM:] = tail

    jobs = []
    for section, src in enumerate((_QA, _KA)):
        for g in range(A_GROUPS):
            gain_at = section * A_W + g * A_OUT
            jobs.append((src + g * A_OUT, A_OUT, lambda acc, gain_at=gain_at, g=g, section=section:
                         store_group(g, section, head_norm(acc, gain_at, A_OUT))))
    for t in range(B_W // PROJ_TILE):
        gain_at = 2 * A_W + t * PROJ_TILE
        jobs.append((_QB + t * PROJ_TILE, PROJ_TILE, lambda acc, gain_at=gain_at, t=t:
                     store_pairs(qbt_ref, t, head_norm(acc, gain_at, PROJ_TILE))))
    for t in range(B_W // PROJ_TILE):
        gain_at = 2 * A_W + B_W + t * PROJ_TILE
        def store_kb(acc, gain_at=gain_at, t=t):
            kb_ref[:, t * PROJ_TILE:(t + 1) * PROJ_TILE] = head_norm(acc, gain_at, PROJ_TILE).astype(kb_ref.dtype)
        jobs.append((_KB + t * PROJ_TILE, PROJ_TILE, store_kb))
    for t in range(C_QW // PROJ_TILE):
        jobs.append((_QC + t * PROJ_TILE, PROJ_TILE,
                     lambda acc, t=t: store_qc(t, acc, _N_NORM + t * PROJ_TILE)))
    jobs.append((_KC, 2 * C_KVW, lambda acc: store_kc_vc(acc, _N_NORM + C_QW)))
    for g in range(A_GROUPS):
        jobs.append((_VA + g * A_OUT, A_OUT, lambda acc, g=g: store_group(g, 2, acc)))
    for t in range(B_W // PROJ_TILE):
        jobs.append((_VB + t * PROJ_TILE, PROJ_TILE, lambda acc, t=t: store_pairs(vbt_ref, t, acc)))

    def product(n):
        src, width, _ = jobs[n]
        blk, col = divmod(src, PROJ_W_BLOCK)
        w = w_refs[blk][:, col:col + width].astype(_MXU_DTYPE)
        return jnp.concatenate([_dot(h, w) for h in halves], axis=0)

    acc_next = product(0)
    for n, (_, _, consume) in enumerate(jobs):
        acc = acc_next
        if n + 1 < len(jobs):
            acc_next = product(n + 1)
        consume(acc)


def _qkv_projection(x2, g1, w_in, layer, gain_row, ones_bd, rope, B, S):
    M = x2.shape[0]
    tm = PROJ_TM
    per_seq = S // tm
    w_block = lambda j: pl.BlockSpec((None, D_MODEL, PROJ_W_BLOCK), lambda i: (layer, 0, j),
                                     pipeline_mode=pl.Buffered(1))
    cos_t, s1_t, s2_t = rope
    r1, r2 = A_PATTERNS[1][1], A_PATTERNS[2][1]
    row = lambda i: (i, 0)
    pos = lambda i: (i % per_seq, 0)
    hm = lambda i: (i // per_seq, 0, i % per_seq, 0)
    hm_t = lambda i: (i // per_seq, 0, 0, i % per_seq)
    cd = _MXU_DTYPE
    return pl.pallas_call(
        _proj_kernel,
        grid=(M // tm,),
        in_specs=[
            pl.BlockSpec((tm, D_MODEL), row),
            _const_spec((1, D_MODEL)),
            w_block(0), w_block(1), w_block(2),
            _const_spec((1, _N_NORM + _N_ROPE)),
            _const_spec((PROJ_TILE, PROJ_TILE)),
            pl.BlockSpec((tm, LANES), pos),
            pl.BlockSpec((tm, LANES), pos),
            pl.BlockSpec((tm, LANES), pos),
        ],
        out_specs=[
            pl.BlockSpec((tm, ZA_W), row),
            pl.BlockSpec((1, r1, tm // r1, ZA_W), hm),
            pl.BlockSpec((1, r2, tm // r2, ZA_W), hm),
            pl.BlockSpec((1, B_PAIRS, LANES, tm), hm_t),
            pl.BlockSpec((tm, B_W), row),
            pl.BlockSpec((1, B_PAIRS, LANES, tm), hm_t),
            pl.BlockSpec((1, C_Q_HEADS, HEAD_DIM, tm), hm_t),
            pl.BlockSpec((1, C_KV_HEADS, tm, HEAD_DIM), hm),
            pl.BlockSpec((1, C_KV_HEADS, VT_ROWS, tm), hm_t),
        ],
        out_shape=[
            jax.ShapeDtypeStruct((M, ZA_W), cd),
            jax.ShapeDtypeStruct((B, r1, S // r1, ZA_W), cd),
            jax.ShapeDtypeStruct((B, r2, S // r2, ZA_W), cd),
            jax.ShapeDtypeStruct((B, B_PAIRS, LANES, S), cd),
            jax.ShapeDtypeStruct((M, B_W), cd),
            jax.ShapeDtypeStruct((B, B_PAIRS, LANES, S), cd),
            jax.ShapeDtypeStruct((B, C_Q_HEADS, HEAD_DIM, S), cd),
            jax.ShapeDtypeStruct((B, C_KV_HEADS, S, HEAD_DIM), cd),
            jax.ShapeDtypeStruct((B, C_KV_HEADS, VT_ROWS, S), cd),
        ],
        scratch_shapes=[pltpu.VMEM((A_OUT // LANES, tm, LANES), jnp.float32)],
        compiler_params=_params(("arbitrary",)),
        name="qkv_projection",
    )(x2, g1, w_in, w_in, w_in, gain_row, ones_bd, cos_t, s1_t, s2_t)


A_UNIT = 4 * A_RADIUS
A_SPAN = A_UNIT + 2 * A_RADIUS
A_TILE = 2048


def _mixer_a_kernel(q_ref, kp_ref, ko_ref, kn_ref, vp_ref, vo_ref, vn_ref, bias_ref, shift_ref,
                    o_ref, lse_ref, kw_ref, vw_ref, s_ref, *, tl, seq_len):
    n_res = q_ref.shape[1]
    l0 = pl.program_id(2) * tl
    for res in range(n_res):
        kw_ref[res, 0:A_RADIUS] = kp_ref[0, res]
        kw_ref[res, A_RADIUS:A_RADIUS + tl] = ko_ref[0, res]
        kw_ref[res, A_RADIUS + tl:] = kn_ref[0, res]
        vw_ref[res, 0:A_RADIUS] = vp_ref[0, res]
        vw_ref[res, A_RADIUS:A_RADIUS + tl] = vo_ref[0, res]
        vw_ref[res, A_RADIUS + tl:] = vn_ref[0, res]
    even_q = lax.broadcasted_iota(jnp.int32, (A_UNIT, LANES), 1) < HEAD_DIM
    lane_k = lax.broadcasted_iota(jnp.int32, (A_SPAN, LANES), 1)

    per_res = tl // A_UNIT
    n_units = n_res * per_res

    def place(n):
        return n // per_res, pl.multiple_of((n % per_res) * A_UNIT, A_UNIT)

    def scores(n, pair):
        res, off = place(n)
        q = q_ref[0, res, pl.ds(off, A_UNIT), pair * LANES:(pair + 1) * LANES]
        zero = jnp.zeros_like(q)
        q2 = jnp.concatenate([jnp.where(even_q, q, zero), jnp.where(even_q, zero, q)], axis=0)
        return _dot_nt(kw_ref[res, pl.ds(off, A_SPAN), pair * LANES:(pair + 1) * LANES], q2)

    s_ref[...] = scores(0, 0)

    def unit(fixed_shift, u, carry):
        res, off = place(u)
        top_ok = l0 + off - A_RADIUS >= 0
        bot_ok = l0 + off + A_UNIT + A_RADIUS <= seq_len
        outs, lses = [], []
        s_next = s_ref[...]
        for pair in range(A_HEADS // 2):
            s_pair = s_next
            if pair + 1 < A_HEADS // 2:
                s_next = scores(u, pair + 1)
            else:
                s_ref[...] = scores(jnp.minimum(u + 1, n_units - 1), 0)
            v_slab = vw_ref[res, pl.ds(off, A_SPAN), pair * LANES:(pair + 1) * LANES]
            for odd in range(2):
                s = s_pair[:, odd * A_UNIT:(odd + 1) * A_UNIT] + bias_ref[2 * pair + odd]
                s = jnp.concatenate([jnp.where(top_ok, s[:A_RADIUS], NEG), s[A_RADIUS:A_SPAN - A_RADIUS],
                                     jnp.where(bot_ok, s[A_SPAN - A_RADIUS:], NEG)], axis=0)
                if fixed_shift:
                    m = shift_ref[0]
                    p = jnp.exp2(s).astype(_MXU_DTYPE)
                else:
                    m = jnp.max(s, axis=0, keepdims=True)
                    p = jnp.exp2(s - m).astype(_MXU_DTYPE)
                den_row = (1 - odd) * HEAD_DIM
                mine = (lane_k >= odd * HEAD_DIM) & (lane_k < (odd + 1) * HEAD_DIM)
                v_aug = jnp.where(mine, v_slab, (lane_k == den_row).astype(v_slab.dtype))
                acc = lax.dot_general(v_aug, p, (((0,), (0,)), ((), ())),
                                      preferred_element_type=jnp.float32)
                den = acc[den_row:den_row + 1]
                outs.append(acc[odd * HEAD_DIM:(odd + 1) * HEAD_DIM] / den)
                lses.append(jnp.broadcast_to(m + jnp.log2(den), (HEAD_DIM, A_UNIT)))
        o_ref[0, res, pl.ds(off, A_UNIT), :] = jnp.concatenate(outs, axis=0).T
        lse_ref[0, res, pl.ds(off, A_UNIT), :] = jnp.concatenate(lses, axis=0).T
        return carry

    small = shift_ref[1] > 0.5

    @pl.when(small)
    def _():
        lax.fori_loop(0, n_units, functools.partial(unit, True), 0, unroll=min(4, n_units))

    @pl.when(jnp.logical_not(small))
    def _():
        lax.fori_loop(0, n_units, functools.partial(unit, False), 0)


def _mixer_a_group(za_g, bias, shift, rate, B, S):
    L = S // rate
    tl = min(A_TILE, L)
    n_res = min(rate, A_TILE // tl)
    nblk = L // A_RADIUS
    per = tl // A_RADIUS

    def own(section):
        return lambda b, r, l: (b, r, l, section)

    def prev(section):
        return lambda b, r, l: (b, r, jnp.maximum(l * per - 1, 0), section)

    def nxt(section):
        return lambda b, r, l: (b, r, jnp.minimum((l + 1) * per, nblk - 1), section)

    edge = (1, n_res, A_RADIUS, A_OUT)
    full = (1, n_res, tl, A_OUT)
    return pl.pallas_call(
        functools.partial(_mixer_a_kernel, tl=tl, seq_len=L),
        grid=(B, rate // n_res, L // tl),
        in_specs=[
            pl.BlockSpec(full, own(0)),
            pl.BlockSpec(edge, prev(1)), pl.BlockSpec(full, own(1)), pl.BlockSpec(edge, nxt(1)),
            pl.BlockSpec(edge, prev(2)), pl.BlockSpec(full, own(2)), pl.BlockSpec(edge, nxt(2)),
            _const_spec((A_HEADS, A_SPAN, A_UNIT)),
            pl.BlockSpec(memory_space=pltpu.SMEM),
        ],
        out_specs=[pl.BlockSpec(full, own(0)), pl.BlockSpec(full, own(0))],
        out_shape=[jax.ShapeDtypeStruct((B, rate, L, A_OUT), jnp.float32)] * 2,
        scratch_shapes=[pltpu.VMEM((n_res, tl + 2 * A_RADIUS, A_OUT), _MXU_DTYPE)] * 2
        + [pltpu.VMEM((A_SPAN, 2 * A_UNIT), jnp.float32)],
        compiler_params=_params(("arbitrary",) * 3),
        name=f"mixer_a_rate{rate}",
    )(za_g, za_g, za_g, za_g, za_g, za_g, za_g, bias, shift)


B_UNIT_ROWS = 4
B_UNIT = B_UNIT_ROWS * GRID_W
B_SPAN_ROWS = B_UNIT_ROWS + B_WIN_ROWS
B_SPAN = B_SPAN_ROWS * GRID_W
B_HALO = (B_WIN_ROWS // 2) * GRID_W
B_TILE_ROWS = 32
B_TILE = B_TILE_ROWS * GRID_W


def _mixer_b_kernel(q_ref, kp_ref, ko_ref, kn_ref, vp_ref, vo_ref, vn_ref, blocks_ref, sel_ref, shift_ref,
                    o_ref, kw_ref, vw_ref, bias_ref, s_ref, *, rows):
    @pl.when((pl.program_id(0) == 0) & (pl.program_id(1) == 0))
    def _():
        left = lax.broadcasted_iota(jnp.int32, (GRID_W, LANES), 1) < GRID_W

        def fill(n, carry):
            v, h = n // B_HEADS, n % B_HEADS
            for a in range(B_SPAN_ROWS):
                for gp in range(B_UNIT_ROWS // 2):
                    at = (v * B_SPAN_ROWS + a) * B_UNIT_ROWS + 2 * gp
                    tile = jnp.where(left, blocks_ref[h, sel_ref[at]], blocks_ref[h, sel_ref[at + 1]])
                    bias_ref[v, h, a * GRID_W:(a + 1) * GRID_W, gp * LANES:(gp + 1) * LANES] = tile
            return carry

        lax.fori_loop(0, 3 * B_HEADS, fill, 0)

    i0 = pl.program_id(1) * B_TILE_ROWS
    kw_ref[0:B_HALO] = kp_ref[0]
    kw_ref[B_HALO:B_HALO + B_TILE] = ko_ref[0]
    kw_ref[B_HALO + B_TILE:] = kn_ref[0]
    vw_ref[:, :, 0:B_HALO] = vp_ref[0]
    vw_ref[:, :, B_HALO:B_HALO + B_TILE] = vo_ref[0]
    vw_ref[:, :, B_HALO + B_TILE:] = vn_ref[0]
    tail = _ones_tail(B_SPAN, _MXU_DTYPE)
    upper = lax.broadcasted_iota(jnp.int32, (LANES, B_UNIT), 0) < HEAD_DIM

    n_units = B_TILE_ROWS // B_UNIT_ROWS

    def span_offset(u):
        r0 = jnp.clip(i0 + u * B_UNIT_ROWS - B_WIN_ROWS // 2, 0, rows - B_SPAN_ROWS)
        return pl.multiple_of((r0 - (i0 - B_WIN_ROWS // 2)) * GRID_W, LANES)

    def scores(u, pair):
        qt = q_ref[0, pair, :, pl.ds(pl.multiple_of(u * B_UNIT, B_UNIT), B_UNIT)]
        zero = jnp.zeros_like(qt)
        qt2 = jnp.concatenate([jnp.where(upper, qt, zero), jnp.where(upper, zero, qt)], axis=1)
        return _dot(kw_ref[pl.ds(span_offset(u), B_SPAN), pair * LANES:(pair + 1) * LANES], qt2)

    s_ref[...] = scores(0, 0)

    def unit(fixed_shift, u, carry):
        i0u = i0 + u * B_UNIT_ROWS
        off = span_offset(u)
        variant = jnp.where(i0u == 0, 0, jnp.where(i0u == rows - B_UNIT_ROWS, 2, 1))
        qoff = pl.multiple_of(u * B_UNIT, B_UNIT)
        outs = []
        s_next = s_ref[...]
        for pair in range(B_PAIRS):
            s_pair = s_next
            if pair + 1 < B_PAIRS:
                s_next = scores(u, pair + 1)
            else:
                s_ref[...] = scores(jnp.minimum(u + 1, n_units - 1), 0)
            for odd in range(2):
                s = s_pair[:, odd * B_UNIT:(odd + 1) * B_UNIT] + bias_ref[variant, 2 * pair + odd]
                if not fixed_shift:
                    s = s - jnp.max(s, axis=0, keepdims=True)
                p = jnp.exp2(s).astype(_MXU_DTYPE)
                vt = vw_ref[pair, odd * HEAD_DIM:(odd + 1) * HEAD_DIM, pl.ds(off, B_SPAN)]
                acc = _dot(jnp.concatenate([vt, tail], axis=0), p)
                outs.append(acc[:HEAD_DIM] / acc[HEAD_DIM:HEAD_DIM + 1])
        o_ref[0, pl.ds(qoff, B_UNIT), :] = jnp.concatenate(outs, axis=0).T.astype(o_ref.dtype)
        return carry

    small = shift_ref[1] > 0.5

    @pl.when(small)
    def _():
        lax.fori_loop(0, n_units, functools.partial(unit, True), 0, unroll=n_units)

    @pl.when(jnp.logical_not(small))
    def _():
        lax.fori_loop(0, n_units, functools.partial(unit, False), 0)


def _mixer_b(qbt, kb, vbt, blocks, shift, B, S):
    rows = S // GRID_W
    sel = jnp.asarray(_mixer_b_block_index(rows))
    nt = rows // B_TILE_ROWS
    per = B_TILE // B_HALO
    nh = S // B_HALO
    kb3 = kb.reshape(B, S, B_W)
    prev = lambda t: jnp.maximum(t * per - 1, 0)
    nxt = lambda t: jnp.minimum((t + 1) * per, nh - 1)
    k_edge, k_own = (1, B_HALO, B_W), (1, B_TILE, B_W)
    t_edge, t_own = (1, B_PAIRS, LANES, B_HALO), (1, B_PAIRS, LANES, B_TILE)
    o = pl.pallas_call(
        functools.partial(_mixer_b_kernel, rows=rows),
        grid=(B, nt),
        in_specs=[
            pl.BlockSpec(t_own, lambda b, t: (b, 0, 0, t)),
            pl.BlockSpec(k_edge, lambda b, t: (b, prev(t), 0)),
            pl.BlockSpec(k_own, lambda b, t: (b, t, 0)),
            pl.BlockSpec(k_edge, lambda b, t: (b, nxt(t), 0)),
            pl.BlockSpec(t_edge, lambda b, t: (b, 0, 0, prev(t))),
            pl.BlockSpec(t_own, lambda b, t: (b, 0, 0, t)),
            pl.BlockSpec(t_edge, lambda b, t: (b, 0, 0, nxt(t))),
            _const_spec(blocks.shape),
            pl.BlockSpec(memory_space=pltpu.SMEM),
            pl.BlockSpec(memory_space=pltpu.SMEM),
        ],
        out_specs=pl.BlockSpec(k_own, lambda b, t: (b, t, 0)),
        out_shape=jax.ShapeDtypeStruct((B, S, B_W), _MXU_DTYPE),
        scratch_shapes=[pltpu.VMEM((B_TILE + 2 * B_HALO, B_W), _MXU_DTYPE),
                        pltpu.VMEM((B_PAIRS, LANES, B_TILE + 2 * B_HALO), _MXU_DTYPE),
                        pltpu.VMEM((3, B_HEADS, B_SPAN, B_UNIT), jnp.float32),
                        pltpu.VMEM((B_SPAN, 2 * B_UNIT), jnp.float32)],
        compiler_params=_params(("arbitrary",) * 2),
        name="mixer_b",
    )(qbt, kb3, kb3, kb3, vbt, vbt, vbt, blocks, sel, shift)
    return o.reshape(B * S, B_W)


C_TQ = 4096
C_TK = 512
C_UNIT = 512
C_UNROLL = 1


def _mixer_c_kernel(q_ref, k_ref, vt_ref, shift_ref, o_ref, qcat_ref, m_ref, acc_ref, s_ref, *, n_kv):
    acc_ref[...] = jnp.zeros(acc_ref.shape, jnp.float32)
    for g in range(C_GROUP):
        qcat_ref[:, g * C_TQ:(g + 1) * C_TQ] = q_ref[0, g]
    n_units = C_GROUP * C_TQ // C_UNIT

    def keys(j):
        return k_ref[0, 0, pl.ds(pl.multiple_of(j * C_TK, C_TK), C_TK), :]

    def scores(k, u):
        return _dot(k, qcat_ref[:, u * C_UNIT:(u + 1) * C_UNIT])

    def sweep(update, unroll):
        s_ref[...] = scores(keys(0), 0)

        def step(j, carry):
            k = keys(j)
            vt = vt_ref[0, 0, :, pl.ds(pl.multiple_of(j * C_TK, C_TK), C_TK)]
            s_next = s_ref[...]
            for u in range(n_units):
                s = s_next
                if u + 1 < n_units:
                    s_next = scores(k, u + 1)
                else:
                    s_ref[...] = scores(keys(jnp.minimum(j + 1, n_kv - 1)), 0)
                update(s, vt, slice(u * C_UNIT, (u + 1) * C_UNIT))
            return carry

        lax.fori_loop(0, n_kv, step, 0, unroll=unroll)

    def fixed_shift(s, vt, cols):
        acc_ref[:, cols] += _dot(vt, jnp.exp2(s - shift_ref[0]).astype(_MXU_DTYPE))

    def running_max(s, vt, cols):
        m_prev = m_ref[:, cols]
        m_new = jnp.maximum(m_prev, jnp.max(s, axis=0, keepdims=True))
        alpha = jnp.exp2(m_prev - m_new)
        p = jnp.exp2(s - m_new).astype(_MXU_DTYPE)
        acc_ref[:, cols] = alpha * acc_ref[:, cols] + _dot(vt, p)
        m_ref[:, cols] = m_new

    small = shift_ref[1] > 0.5

    @pl.when(small)
    def _():
        sweep(fixed_shift, C_UNROLL)

    @pl.when(jnp.logical_not(small))
    def _():
        m_ref[...] = jnp.full(m_ref.shape, -jnp.inf, jnp.float32)
        sweep(running_max, 1)

    o_t = jnp.concatenate(
        [acc_ref[:HEAD_DIM, g * C_TQ:(g + 1) * C_TQ] / acc_ref[HEAD_DIM:HEAD_DIM + 1, g * C_TQ:(g + 1) * C_TQ]
         for g in range(C_GROUP)], axis=0)
    o_ref[0] = o_t.T.astype(o_ref.dtype)


def _mixer_c(qc_t, kc, vc_t, shift, B, S):
    o = pl.pallas_call(
        functools.partial(_mixer_c_kernel, n_kv=S // C_TK),
        grid=(B, C_KV_HEADS, S // C_TQ),
        in_specs=[
            pl.BlockSpec((1, C_GROUP, HEAD_DIM, C_TQ), lambda b, kv, i: (b, kv, 0, i)),
            pl.BlockSpec((1, 1, S, HEAD_DIM), lambda b, kv, i: (b, kv, 0, 0)),
            pl.BlockSpec((1, 1, VT_ROWS, S), lambda b, kv, i: (b, kv, 0, 0)),
            pl.BlockSpec(memory_space=pltpu.SMEM),
        ],
        out_specs=pl.BlockSpec((1, C_TQ, C_GROUP * HEAD_DIM), lambda b, kv, i: (b, i, kv)),
        out_shape=jax.ShapeDtypeStruct((B, S, C_QW), _MXU_DTYPE),
        scratch_shapes=[pltpu.VMEM((HEAD_DIM, C_GROUP * C_TQ), _MXU_DTYPE),
                        pltpu.VMEM((1, C_GROUP * C_TQ), jnp.float32),
                        pltpu.VMEM((VT_ROWS, C_GROUP * C_TQ), jnp.float32),
                        pltpu.VMEM((C_TK, C_UNIT), jnp.float32)],
        compiler_params=_params(("arbitrary",) * 3),
        name="mixer_c",
    )(qc_t, kc, vc_t, shift)
    return o.reshape(B * S, C_QW)


MERGE_TM = 512
MERGE_GATE_BLOCK = 512


def _merge_kernel(x_ref, g_ref, oa0_ref, oa1_ref, oa2_ref, l0_ref, l1_ref, l2_ref, ob_ref, oc_ref,
                  wg0_ref, wg1_ref, wg2_ref, wg3_ref, wg4_ref, wg5_ref, pa_ref, pb_ref, pc_ref, wo_ref,
                  out_ref, *scratch):
    tm = x_ref.shape[0]
    gate_refs = (wg0_ref, wg1_ref, wg2_ref, wg3_ref, wg4_ref, wg5_ref)
    cast = lambda ref: ref[...].astype(_MXU_DTYPE)

    def token_major(ref, scr):
        rate = ref.shape[1]
        halves = range(A_OUT // LANES)
        for r in range(rate):
            for j in halves:
                scr[j, pl.ds(r, tm // rate, stride=rate), :] = ref[0, r, :, j * LANES:(j + 1) * LANES]
        return jnp.concatenate([scr[j] for j in halves], axis=1)

    x = x_ref[...]
    h = _rms(x, g_ref[...]).astype(_MXU_DTYPE)
    n_parts = D_MODEL // MERGE_GATE_BLOCK
    branches = [None, _dot(ob_ref[...], cast(pb_ref)), _dot(oc_ref[...], cast(pc_ref))]
    gates = {(b, part): jax.nn.sigmoid(_dot(h, cast(gate_refs[b * n_parts + part])))
             for b in (1, 2, 0) for part in range(n_parts)}
    oa0, l0 = oa0_ref[...], l0_ref[...]
    oa1, l1 = token_major(oa1_ref, scratch[0]), token_major(l1_ref, scratch[1])
    oa2, l2 = token_major(oa2_ref, scratch[2]), token_major(l2_ref, scratch[3])
    mx = jnp.maximum(jnp.maximum(l0, l1), l2)
    w0, w1, w2 = jnp.exp2(l0 - mx), jnp.exp2(l1 - mx), jnp.exp2(l2 - mx)
    o_a = (w0 * oa0 + w1 * oa1 + w2 * oa2) / (w0 + w1 + w2)
    branches[0] = _dot(o_a.astype(_MXU_DTYPE), cast(pa_ref))
    parts = []
    for part in range(n_parts):
        cols = slice(part * MERGE_GATE_BLOCK, (part + 1) * MERGE_GATE_BLOCK)
        merged = sum(gates[b, part] * branches[b][:, cols] for b in range(1, N_BRANCH))
        parts.append((merged + gates[0, part] * branches[0][:, cols]).astype(_MXU_DTYPE))
    out_ref[...] = x + _dot(jnp.concatenate(parts, axis=1), cast(wo_ref))


def _merge(x2, g1, oa, lse, ob, oc, w_in, w_br_a, w_br_b, w_br_c, w_o, layer, B, S):
    M = x2.shape[0]
    tm = MERGE_TM
    per_seq = S // tm
    row = lambda i: (i, 0)
    tile = lambda w: pl.BlockSpec((tm, w), row)
    gate0 = _ZG // MERGE_GATE_BLOCK

    def layer_weight(w, block=None, at=0):
        block = w.shape[2] if block is None else block
        return pl.BlockSpec((None, w.shape[1], block), lambda i: (layer, 0, at), pipeline_mode=pl.Buffered(1))

    def dilated(rate):
        return pl.BlockSpec((1, rate, tm // rate, A_OUT), lambda i: (i // per_seq, 0, i % per_seq, 0))

    r1, r2 = A_PATTERNS[1][1], A_PATTERNS[2][1]
    return pl.pallas_call(
        _merge_kernel,
        grid=(M // tm,),
        in_specs=[tile(D_MODEL), _const_spec((1, D_MODEL)),
                  tile(A_OUT), dilated(r1), dilated(r2), tile(A_OUT), dilated(r1), dilated(r2),
                  tile(B_W), tile(C_QW),
                  *[layer_weight(w_in, MERGE_GATE_BLOCK, gate0 + j)
                    for j in range(N_BRANCH * D_MODEL // MERGE_GATE_BLOCK)],
                  layer_weight(w_br_a), layer_weight(w_br_b), layer_weight(w_br_c), layer_weight(w_o)],
        out_specs=tile(D_MODEL),
        out_shape=jax.ShapeDtypeStruct((M, D_MODEL), jnp.float32),
        scratch_shapes=[pltpu.VMEM((A_OUT // LANES, tm, LANES), jnp.float32)] * 4,
        compiler_params=_params(("arbitrary",)),
        name="gated_merge",
    )(x2, g1, oa[0].reshape(M, A_OUT), oa[1], oa[2], lse[0].reshape(M, A_OUT), lse[1], lse[2],
      ob, oc, *([w_in] * (N_BRANCH * D_MODEL // MERGE_GATE_BLOCK)), w_br_a, w_br_b, w_br_c, w_o)


FFN_TM = 1024
FFN_CHUNK = 256


def _ffn_kernel(x_ref, g_ref, wup_ref, wdown_ref, out_ref, act_ref):
    x = x_ref[...]
    xn = _rms(x, g_ref[...]).astype(_MXU_DTYPE)
    for c in range(D_FF // FFN_CHUNK):
        cs = slice(c * FFN_CHUNK, (c + 1) * FFN_CHUNK)
        a = _dot(xn, wup_ref[:, cs])
        b = _dot(xn, wup_ref[:, D_FF + c * FFN_CHUNK:D_FF + (c + 1) * FFN_CHUNK])
        act_ref[:, cs] = (a * jax.nn.sigmoid(a) * b).astype(act_ref.dtype)
    out_ref[...] = x + _dot(act_ref[...], wdown_ref[...])


def _ffn(x2, g2, w_up, w_down, layer):
    M = x2.shape[0]
    tm = FFN_TM
    row = lambda i: (i, 0)
    layer_weight = lambda w: pl.BlockSpec((None,) + w.shape[1:], lambda i: (layer, 0, 0),
                                          pipeline_mode=pl.Buffered(1))
    return pl.pallas_call(
        _ffn_kernel,
        grid=(M // tm,),
        in_specs=[pl.BlockSpec((tm, D_MODEL), row), _const_spec((1, D_MODEL)),
                  layer_weight(w_up), layer_weight(w_down)],
        out_specs=pl.BlockSpec((tm, D_MODEL), row),
        out_shape=jax.ShapeDtypeStruct((M, D_MODEL), jnp.float32),
        scratch_shapes=[pltpu.VMEM((tm, D_FF), _MXU_DTYPE)],
        compiler_params=_params(("arbitrary",)),
        name="swiglu_ffn",
    )(x2, g2, w_up, w_down)


def _t5_bucket(rel):
    half = T5_BUCKETS // 2
    max_exact = half // 2
    ret = jnp.where(rel > 0, half, 0)
    n = jnp.abs(rel)
    nf = jnp.maximum(n, 1).astype(jnp.float32)
    large = max_exact + (jnp.log(nf / max_exact) / math.log(T5_MAX_DIST / max_exact)
                         * (half - max_exact)).astype(jnp.int32)
    large = jnp.minimum(large, half - 1)
    return ret + jnp.where(n < max_exact, n, large)


def _mixer_a_bias(table_g, rate):
    j = jnp.arange(A_SPAN)[:, None]
    i = jnp.arange(A_UNIT)[None, :]
    step = j - A_RADIUS - i
    onehot = (_t5_bucket(step * rate)[:, :, None] == jnp.arange(T5_BUCKETS)).astype(jnp.float32)
    bias = jnp.einsum("jib,bh->hji", onehot, table_g.astype(jnp.float32) * LOG2_E,
                      precision=lax.Precision.HIGHEST)
    return jnp.where((jnp.abs(step) <= A_RADIUS)[None], bias, NEG)


B_N_DR = 2 * B_WIN_ROWS - 1


def _mixer_b_blocks(rpb, shift):
    c = np.arange(GRID_W)
    c0 = np.clip(c - B_WIN_COLS // 2, 0, GRID_W - B_WIN_COLS)
    col_ok = (c[:, None] >= c0[None, :]) & (c[:, None] < c0[None, :] + B_WIN_COLS)
    dc = np.clip(c[:, None] - c[None, :] + B_WIN_COLS - 1, 0, 2 * B_WIN_COLS - 2)
    pick_c = ((dc[..., None] == np.arange(2 * B_WIN_COLS - 1)) & col_ok[..., None]).astype(np.float32)
    blocks = jnp.einsum("hdk,xyk->hdxy", rpb.astype(jnp.float32) * LOG2_E, pick_c,
                        precision=lax.Precision.HIGHEST)
    blocks = jnp.where(col_ok[None, None], blocks - shift, NEG)
    blocks = jnp.concatenate([blocks, jnp.full((B_HEADS, 1, GRID_W, GRID_W), NEG, jnp.float32)], axis=1)
    return jnp.concatenate([blocks, blocks], axis=-1)


def _mixer_b_block_index(rows):
    first_query_row = np.array([0, B_WIN_ROWS // 2, rows - B_UNIT_ROWS])
    i = first_query_row[:, None, None] + np.arange(B_UNIT_ROWS)[None, None, :]
    r0 = np.clip(first_query_row - B_WIN_ROWS // 2, 0, rows - B_SPAN_ROWS)
    ik = r0[:, None, None] + np.arange(B_SPAN_ROWS)[None, :, None]
    rs = np.clip(i - B_WIN_ROWS // 2, 0, rows - B_WIN_ROWS)
    row_ok = (ik >= rs) & (ik < rs + B_WIN_ROWS)
    return np.where(row_ok, ik - i + B_WIN_ROWS - 1, B_N_DR).astype(np.int32).reshape(-1)


def _rope_tables(S):
    rows = S // GRID_W
    inv = ROPE_THETA ** (-jnp.arange(0, ROPE_AXIS_DIM, 2, dtype=jnp.float32) / ROPE_AXIS_DIM)
    d = np.arange(LANES) % HEAD_DIM
    inv_lane = inv[d % (ROPE_AXIS_DIM // 2)][None, :]
    is_col = (d >= ROPE_AXIS_DIM)[None, None, :]
    first = ((d % ROPE_AXIS_DIM) < ROPE_AXIS_DIM // 2)[None, :]
    tables = []
    for n in (rows, GRID_W):
        ang = jnp.arange(n, dtype=jnp.float32)[:, None] * inv_lane
        sin = jnp.sin(ang)
        tables.append((jnp.cos(ang), jnp.where(first, -sin, 0.0), jnp.where(first, 0.0, sin)))
    return tuple(jnp.where(is_col, by_col[None, :, :], by_row[:, None, :]).reshape(S, LANES)
                 for by_row, by_col in zip(*tables))


def _softmax_shift(gain_q, gain_k, bias_abs_max):
    bound = (HEAD_DIM * QK_SCALE * LOG2_E * ROUNDING_SLACK * jnp.max(jnp.abs(gain_q)) * jnp.max(jnp.abs(gain_k))
             + LOG2_E * bias_abs_max)
    small = bound <= SHIFT_LIMIT
    return jnp.stack([jnp.where(small, bound, 0.0), small.astype(jnp.float32)]).astype(jnp.float32)


def _block_diag_ones():
    i = np.arange(PROJ_TILE) // HEAD_DIM
    return jnp.asarray(i[:, None] == i[None, :], _MXU_DTYPE)


def kernel(x, rel_bias_table, norm1, w_in, qk_gain, nat_rpb, w_br_a, w_br_b, w_br_c, w_o,
           norm2, w_up, w_down):
    B, S, D = x.shape
    depth = w_in.shape[0]
    M = B * S
    x2 = x.reshape(M, D)
    rope = _rope_tables(S)
    ones_bd = _block_diag_ones()
    a_bias = [_mixer_a_bias(rel_bias_table[:, g * A_HEADS:(g + 1) * A_HEADS], rate)
              for g, (_, rate) in enumerate(A_PATTERNS)]
    w_up_c, w_down_c = w_up.astype(_MXU_DTYPE), w_down.astype(_MXU_DTYPE)
    for l in range(depth):
        gq = qk_gain[l]
        tile = lambda g, n, s: jnp.tile(g * s, n)
        gain_row = jnp.concatenate([
            tile(gq[0], A_W // HEAD_DIM, QK_SCALE * LOG2_E), tile(gq[1], A_W // HEAD_DIM, 1.0),
            tile(gq[2], B_HEADS, QK_SCALE * LOG2_E), tile(gq[3], B_HEADS, 1.0),
            tile(gq[4], C_Q_HEADS, QK_SCALE * LOG2_E), tile(gq[5], C_KV_HEADS, 1.0)])[None, :]
        g1 = norm1[l][None, :]
        za0, za1, za2, qbt, kb, vbt, qc, kc, vc = _qkv_projection(
            x2, g1, w_in, l, gain_row, ones_bd, rope, B, S)
        za =(za0.reshape(B, 1, S, ZA_W), za1, za2)
        oa, lse = [], []
        for g, (_, rate) in enumerate(A_PATTERNS):
            table_g = rel_bias_table[:, g * A_HEADS:(g + 1) * A_HEADS]
            shift_a = _softmax_shift(gq[0], gq[1], jnp.max(jnp.abs(table_g)))
            o_g, l_g = _mixer_a_group(za[g], a_bias[g] - shift_a[0], shift_a, rate, B, S)
            oa.append(o_g)
            lse.append(l_g)
        shift_b = _softmax_shift(gq[2], gq[3], jnp.max(jnp.abs(nat_rpb[l])))
        ob = _mixer_b(qbt, kb, vbt, _mixer_b_blocks(nat_rpb[l], shift_b[0]), shift_b, B, S)
        oc = _mixer_c(qc, kc, vc, _softmax_shift(gq[4], gq[5], 0.0), B, S)
        x2 = _merge(x2, g1, oa, lse, ob, oc, w_in, w_br_a, w_br_b, w_br_c, w_o, l, B, S)
        x2 = _ffn(x2, norm2[l][None, :], w_up_c, w_down_c, l)
    return x2.reshape(B, S, D)
```

```python
import functools
import math

import jax
import jax.numpy as jnp
import numpy as np
from jax import lax
from jax.experimental import pallas as pl
from jax.experimental.pallas import tpu as pltpu

_MXU_DTYPE = jnp.bfloat16

D_MODEL = 1024
HEAD_DIM = 64
GRID_W = 64
RMS_EPS = 1e-6
NEG = -1e30
A_PATTERNS = ((128, 1), (512, 4), (2048, 16))
A_GROUPS = 3
A_HEADS = 4
A_W = A_GROUPS * A_HEADS * HEAD_DIM
A_OUT = A_HEADS * HEAD_DIM
A_RADIUS = 64
B_HEADS = 8
B_W = B_HEADS * HEAD_DIM
B_WIN_ROWS = 8
B_WIN_COLS = 16
C_Q_HEADS = 8
C_KV_HEADS = 2
C_GROUP = C_Q_HEADS // C_KV_HEADS
C_QW = C_Q_HEADS * HEAD_DIM
C_KVW = C_KV_HEADS * HEAD_DIM
ROPE_THETA = 10000.0
ROPE_AXIS_DIM = HEAD_DIM // 2
T5_BUCKETS = 32
T5_MAX_DIST = 1024
N_BRANCH = 3
D_FF = math.ceil(8 * D_MODEL / 3 / 256) * 256
QK_SCALE = HEAD_DIM ** -0.5
LOG2_E = math.log2(math.e)

V7X_VMEM_BYTES = 64 * 1024 * 1024
VMEM_LIMIT_BYTES = V7X_VMEM_BYTES * 7 // 8
LANES = 128
BF16_SUBLANES = 16
MXU_COLS = 256
VT_ROWS = HEAD_DIM + BF16_SUBLANES
SHIFT_LIMIT = 60.0
ROUNDING_SLACK = 1.02

_OFF = np.cumsum([0, A_W, A_W, A_W, B_W, B_W, B_W, C_QW, C_KVW, C_KVW]).tolist()
(_QA, _KA, _VA, _QB, _KB, _VB, _QC, _KC, _VC, _ZG) = _OFF

ZA_W = 3 * A_OUT
B_PAIRS = B_HEADS // 2


def _params(sem):
    return pltpu.CompilerParams(dimension_semantics=sem, vmem_limit_bytes=VMEM_LIMIT_BYTES)


def _const_spec(shape):
    nd = len(shape)
    return pl.BlockSpec(shape, lambda *_: (0,) * nd, pipeline_mode=pl.Buffered(1))


def _rms(x, g):
    return x * lax.rsqrt(jnp.mean(x * x, axis=-1, keepdims=True) + RMS_EPS) * g


def _dot(a, b):
    return jnp.dot(a, b, preferred_element_type=jnp.float32)


def _dot_nt(a, b):
    return lax.dot_general(a, b, (((1,), (1,)), ((), ())), preferred_element_type=jnp.float32)


def _ones_tail(width, dtype):
    return (lax.broadcasted_iota(jnp.int32, (VT_ROWS - HEAD_DIM, width), 0) == 0).astype(dtype)


PROJ_TM = 1024
PROJ_TILE = MXU_COLS
PROJ_W_BLOCK = 1536
_N_NORM = 2 * A_W + 2 * B_W
_N_ROPE = C_QW + C_KVW


def _proj_kernel(x_ref, g_ref, w0_ref, w1_ref, w2_ref, gain_ref, ones_ref, cos_ref, s1_ref, s2_ref,
                 za0_ref, za1_ref, za2_ref, qbt_ref, kb_ref, vbt_ref, qc_ref, kc_ref, vc_ref, dil_ref):
    tm = x_ref.shape[0]
    w_refs = (w0_ref, w1_ref, w2_ref)
    halves = [_rms(x_ref[r * (tm // 2):(r + 1) * (tm // 2)], g_ref[...]).astype(_MXU_DTYPE) for r in range(2)]
    za_refs = (za0_ref, za1_ref, za2_ref)

    def head_norm(acc, c0, width):
        sq = (acc * acc).astype(_MXU_DTYPE)
        ms = _dot(sq, ones_ref[:width, :width]) * (1.0 / HEAD_DIM)
        return acc * lax.rsqrt(ms + RMS_EPS) * gain_ref[:, c0:c0 + width]

    def store_group(g, section, val):
        cols = slice(section * A_OUT, (section + 1) * A_OUT)
        rate = A_PATTERNS[g][1]
        if rate == 1:
            za0_ref[:, cols] = val.astype(za0_ref.dtype)
            return
        for j in range(A_OUT // LANES):
            dil_ref[j] = val[:, j * LANES:(j + 1) * LANES]
        for r in range(rate):
            picked = [dil_ref[j, pl.ds(r, tm // rate, stride=rate), :] for j in range(A_OUT // LANES)]
            za_refs[g][0, r, :, cols] = jnp.concatenate(picked, axis=1).astype(za_refs[g].dtype)

    def store_pairs(ref, t, val):
        vt = val.T.astype(ref.dtype)
        ref[0, 2 * t] = vt[:LANES]
        ref[0, 2 * t + 1] = vt[LANES:]

    def rotary(y, width):
        reps = width // LANES
        table = lambda ref: jnp.concatenate([ref[...]] * reps, axis=1) if reps > 1 else ref[...]
        return (y * table(cos_ref) + pltpu.roll(y, width - 16, 1) * table(s1_ref)
                + pltpu.roll(y, 16, 1) * table(s2_ref))

    def store_qc(t, acc, gain_at):
        yt = rotary(head_norm(acc, gain_at, PROJ_TILE), PROJ_TILE).T.astype(qc_ref.dtype)
        heads = PROJ_TILE // HEAD_DIM
        for j in range(heads):
            qc_ref[0, heads * t + j] = yt[j * HEAD_DIM:(j + 1) * HEAD_DIM]

    def store_kc_vc(acc, gain_at):
        y = rotary(head_norm(acc[:, :LANES], gain_at, LANES), LANES).astype(kc_ref.dtype)
        acc_t = acc[:, LANES:].T.astype(vc_ref.dtype)
        tail = _ones_tail(tm, vc_ref.dtype)
        for kv in range(C_KV_HEADS):
            kc_ref[0, kv] = y[:, kv * HEAD_DIM:(kv + 1) * HEAD_DIM]
            vc_ref[0, kv, :HEAD_DIM] = acc_t[kv * HEAD_DIM:(kv + 1) * HEAD_DIM]
            vc_ref[0, kv, HEAD_DIM:] = tail

    jobs = []
    for section, src in enumerate((_QA, _KA)):
        for g in range(A_GROUPS):
            gain_at = section * A_W + g * A_OUT
            jobs.append((src + g * A_OUT, A_OUT, lambda acc, gain_at=gain_at, g=g, section=section:
                         store_group(g, section, head_norm(acc, gain_at, A_OUT))))
    for t in range(B_W // PROJ_TILE):
        gain_at = 2 * A_W + t * PROJ_TILE
        jobs.append((_QB + t * PROJ_TILE, PROJ_TILE, lambda acc, gain_at=gain_at, t=t:
                     store_pairs(qbt_ref, t, head_norm(acc, gain_at, PROJ_TILE))))
    for t in range(B_W // PROJ_TILE):
        gain_at = 2 * A_W + B_W + t * PROJ_TILE
        def store_kb(acc, gain_at=gain_at, t=t):
            kb_ref[:, t * PROJ_TILE:(t + 1) * PROJ_TILE] = head_norm(acc, gain_at, PROJ_TILE).astype(kb_ref.dtype)
        jobs.append((_KB + t * PROJ_TILE, PROJ_TILE, store_kb))
    for t in range(C_QW // PROJ_TILE):
        jobs.append((_QC + t * PROJ_TILE, PROJ_TILE,
                     lambda acc, t=t: store_qc(t, acc, _N_NORM + t * PROJ_TILE)))
    jobs.append((_KC, 2 * C_KVW, lambda acc: store_kc_vc(acc, _N_NORM + C_QW)))
    for g in range(A_GROUPS):
        jobs.append((_VA + g * A_OUT, A_OUT, lambda acc, g=g: store_group(g, 2, acc)))
    for t in range(B_W // PROJ_TILE):
        jobs.append((_VB + t * PROJ_TILE, PROJ_TILE, lambda acc, t=t: store_pairs(vbt_ref, t, acc)))

    def product(n):
        src, width, _ = jobs[n]
        blk, col = divmod(src, PROJ_W_BLOCK)
        w = w_refs[blk][:, col:col + width].astype(_MXU_DTYPE)
        return jnp.concatenate([_dot(h, w) for h in halves], axis=0)

    acc_next = product(0)
    for n, (_, _, consume) in enumerate(jobs):
        acc = acc_next
        if n + 1 < len(jobs):
            acc_next = product(n + 1)
        consume(acc)


def _qkv_projection(x2, g1, w_in, layer, gain_row, ones_bd, rope, B, S):
    M = x2.shape[0]
    tm = PROJ_TM
    per_seq = S // tm
    w_block = lambda j: pl.BlockSpec((None, D_MODEL, PROJ_W_BLOCK), lambda i: (layer, 0, j),
                                     pipeline_mode=pl.Buffered(1))
    cos_t, s1_t, s2_t = rope
    r1, r2 = A_PATTERNS[1][1], A_PATTERNS[2][1]
    row = lambda i: (i, 0)
    pos = lambda i: (i % per_seq, 0)
    hm = lambda i: (i // per_seq, 0, i % per_seq, 0)
    hm_t = lambda i: (i // per_seq, 0, 0, i % per_seq)
    cd = _MXU_DTYPE
    return pl.pallas_call(
        _proj_kernel,
        grid=(M // tm,),
        in_specs=[
            pl.BlockSpec((tm, D_MODEL), row),
            _const_spec((1, D_MODEL)),
            w_block(0), w_block(1), w_block(2),
            _const_spec((1, _N_NORM + _N_ROPE)),
            _const_spec((PROJ_TILE, PROJ_TILE)),
            pl.BlockSpec((tm, LANES), pos),
            pl.BlockSpec((tm, LANES), pos),
            pl.BlockSpec((tm, LANES), pos),
        ],
        out_specs=[
            pl.BlockSpec((tm, ZA_W), row),
            pl.BlockSpec((1, r1, tm // r1, ZA_W), hm),
            pl.BlockSpec((1, r2, tm // r2, ZA_W), hm),
            pl.BlockSpec((1, B_PAIRS, LANES, tm), hm_t),
            pl.BlockSpec((tm, B_W), row),
            pl.BlockSpec((1, B_PAIRS, LANES, tm), hm_t),
            pl.BlockSpec((1, C_Q_HEADS, HEAD_DIM, tm), hm_t),
            pl.BlockSpec((1, C_KV_HEADS, tm, HEAD_DIM), hm),
            pl.BlockSpec((1, C_KV_HEADS, VT_ROWS, tm), hm_t),
        ],
        out_shape=[
            jax.ShapeDtypeStruct((M, ZA_W), cd),
            jax.ShapeDtypeStruct((B, r1, S // r1, ZA_W), cd),
            jax.ShapeDtypeStruct((B, r2, S // r2, ZA_W), cd),
            jax.ShapeDtypeStruct((B, B_PAIRS, LANES, S), cd),
            jax.ShapeDtypeStruct((M, B_W), cd),
            jax.ShapeDtypeStruct((B, B_PAIRS, LANES, S), cd),
            jax.ShapeDtypeStruct((B, C_Q_HEADS, HEAD_DIM, S), cd),
            jax.ShapeDtypeStruct((B, C_KV_HEADS, S, HEAD_DIM), cd),
            jax.ShapeDtypeStruct((B, C_KV_HEADS, VT_ROWS, S), cd),
        ],
        scratch_shapes=[pltpu.VMEM((A_OUT // LANES, tm, LANES), jnp.float32)],
        compiler_params=_params(("arbitrary",)),
        name="qkv_projection",
    )(x2, g1, w_in, w_in, w_in, gain_row, ones_bd, cos_t, s1_t, s2_t)


A_UNIT = 4 * A_RADIUS
A_SPAN = A_UNIT + 2 * A_RADIUS
A_TILE = 2048


def _mixer_a_kernel(q_ref, kp_ref, ko_ref, kn_ref, vp_ref, vo_ref, vn_ref, bias_ref, shift_ref,
                    o_ref, lse_ref, kw_ref, vw_ref, s_ref, *, tl, seq_len):
    n_res = q_ref.shape[1]
    l0 = pl.program_id(2) * tl
    for res in range(n_res):
        kw_ref[res, 0:A_RADIUS] = kp_ref[0, res]
        kw_ref[res, A_RADIUS:A_RADIUS + tl] = ko_ref[0, res]
        kw_ref[res, A_RADIUS + tl:] = kn_ref[0, res]
        vw_ref[res, 0:A_RADIUS] = vp_ref[0, res]
        vw_ref[res, A_RADIUS:A_RADIUS + tl] = vo_ref[0, res]
        vw_ref[res, A_RADIUS + tl:] = vn_ref[0, res]
    even_q = lax.broadcasted_iota(jnp.int32, (A_UNIT, LANES), 1) < HEAD_DIM
    lane_k = lax.broadcasted_iota(jnp.int32, (A_SPAN, LANES), 1)

    per_res = tl // A_UNIT
    n_units = n_res * per_res

    def place(n):
        return n // per_res, pl.multiple_of((n % per_res) * A_UNIT, A_UNIT)

    def scores(n, pair):
        res, off = place(n)
        q = q_ref[0, res, pl.ds(off, A_UNIT), pair * LANES:(pair + 1) * LANES]
        zero = jnp.zeros_like(q)
        q2 = jnp.concatenate([jnp.where(even_q, q, zero), jnp.where(even_q, zero, q)], axis=0)
        return _dot_nt(kw_ref[res, pl.ds(off, A_SPAN), pair * LANES:(pair + 1) * LANES], q2)

    s_ref[...] = scores(0, 0)

    def unit(fixed_shift, u, carry):
        res, off = place(u)
        top_ok = l0 + off - A_RADIUS >= 0
        bot_ok = l0 + off + A_UNIT + A_RADIUS <= seq_len
        outs, lses = [], []
        s_next = s_ref[...]
        for pair in range(A_HEADS // 2):
            s_pair = s_next
            if pair + 1 < A_HEADS // 2:
                s_next = scores(u, pair + 1)
            else:
                s_ref[...] = scores(jnp.minimum(u + 1, n_units - 1), 0)
            v_slab = vw_ref[res, pl.ds(off, A_SPAN), pair * LANES:(pair + 1) * LANES]
            for odd in range(2):
                s = s_pair[:, odd * A_UNIT:(odd + 1) * A_UNIT] + bias_ref[2 * pair + odd]
                s = jnp.concatenate([jnp.where(top_ok, s[:A_RADIUS], NEG), s[A_RADIUS:A_SPAN - A_RADIUS],
                                     jnp.where(bot_ok, s[A_SPAN - A_RADIUS:], NEG)], axis=0)
                if fixed_shift:
                    m = shift_ref[0]
                    p = jnp.exp2(s).astype(_MXU_DTYPE)
                else:
                    m = jnp.max(s, axis=0, keepdims=True)
                    p = jnp.exp2(s - m).astype(_MXU_DTYPE)
                den_row = (1 - odd) * HEAD_DIM
                mine = (lane_k >= odd * HEAD_DIM) & (lane_k < (odd + 1) * HEAD_DIM)
                v_aug = jnp.where(mine, v_slab, (lane_k == den_row).astype(v_slab.dtype))
                acc = lax.dot_general(v_aug, p, (((0,), (0,)), ((), ())),
                                      preferred_element_type=jnp.float32)
                den = acc[den_row:den_row + 1]
                outs.append(acc[odd * HEAD_DIM:(odd + 1) * HEAD_DIM] / den)
                lses.append(jnp.broadcast_to(m + jnp.log2(den), (HEAD_DIM, A_UNIT)))
        o_ref[0, res, pl.ds(off, A_UNIT), :] = jnp.concatenate(outs, axis=0).T
        lse_ref[0, res, pl.ds(off, A_UNIT), :] = jnp.concatenate(lses, axis=0).T
        return carry

    small = shift_ref[1] > 0.5

    @pl.when(small)
    def _():
        lax.fori_loop(0, n_units, functools.partial(unit, True), 0, unroll=min(4, n_units))

    @pl.when(jnp.logical_not(small))
    def _():
        lax.fori_loop(0, n_units, functools.partial(unit, False), 0)


def _mixer_a_group(za_g, bias, shift, rate, B, S):
    L = S // rate
    tl = min(A_TILE, L)
    n_res = min(rate, A_TILE // tl)
    nblk = L // A_RADIUS
    per = tl // A_RADIUS

    def own(section):
        return lambda b, r, l: (b, r, l, section)

    def prev(section):
        return lambda b, r, l: (b, r, jnp.maximum(l * per - 1, 0), section)

    def nxt(section):
        return lambda b, r, l: (b, r, jnp.minimum((l + 1) * per, nblk - 1), section)

    edge = (1, n_res, A_RADIUS, A_OUT)
    full = (1, n_res, tl, A_OUT)
    return pl.pallas_call(
        functools.partial(_mixer_a_kernel, tl=tl, seq_len=L),
        grid=(B, rate // n_res, L // tl),
        in_specs=[
            pl.BlockSpec(full, own(0)),
            pl.BlockSpec(edge, prev(1)), pl.BlockSpec(full, own(1)), pl.BlockSpec(edge, nxt(1)),
            pl.BlockSpec(edge, prev(2)), pl.BlockSpec(full, own(2)), pl.BlockSpec(edge, nxt(2)),
            _const_spec((A_HEADS, A_SPAN, A_UNIT)),
            pl.BlockSpec(memory_space=pltpu.SMEM),
        ],
        out_specs=[pl.BlockSpec(full, own(0)), pl.BlockSpec(full, own(0))],
        out_shape=[jax.ShapeDtypeStruct((B, rate, L, A_OUT), jnp.float32)] * 2,
        scratch_shapes=[pltpu.VMEM((n_res, tl + 2 * A_RADIUS, A_OUT), _MXU_DTYPE)] * 2
        + [pltpu.VMEM((A_SPAN, 2 * A_UNIT), jnp.float32)],
        compiler_params=_params(("arbitrary",) * 3),
        name=f"mixer_a_rate{rate}",
    )(za_g, za_g, za_g, za_g, za_g, za_g, za_g, bias, shift)


B_UNIT_ROWS = 4
B_UNIT = B_UNIT_ROWS * GRID_W
B_SPAN_ROWS = B_UNIT_ROWS + B_WIN_ROWS
B_SPAN = B_SPAN_ROWS * GRID_W
B_HALO = (B_WIN_ROWS // 2) * GRID_W
B_TILE_ROWS = 32
B_TILE = B_TILE_ROWS * GRID_W


def _mixer_b_kernel(q_ref, kp_ref, ko_ref, kn_ref, vp_ref, vo_ref, vn_ref, blocks_ref, sel_ref, shift_ref,
                    o_ref, kw_ref, vw_ref, bias_ref, s_ref, *, rows):
    @pl.when((pl.program_id(0) == 0) & (pl.program_id(1) == 0))
    def _():
        left = lax.broadcasted_iota(jnp.int32, (GRID_W, LANES), 1) < GRID_W

        def fill(n, carry):
            v, h = n // B_HEADS, n % B_HEADS
            for a in range(B_SPAN_ROWS):
                for gp in range(B_UNIT_ROWS // 2):
                    at = (v * B_SPAN_ROWS + a) * B_UNIT_ROWS + 2 * gp
                    tile = jnp.where(left, blocks_ref[h, sel_ref[at]], blocks_ref[h, sel_ref[at + 1]])
                    bias_ref[v, h, a * GRID_W:(a + 1) * GRID_W, gp * LANES:(gp + 1) * LANES] = tile
            return carry

        lax.fori_loop(0, 3 * B_HEADS, fill, 0)

    i0 = pl.program_id(1) * B_TILE_ROWS
    kw_ref[0:B_HALO] = kp_ref[0]
    kw_ref[B_HALO:B_HALO + B_TILE] = ko_ref[0]
    kw_ref[B_HALO + B_TILE:] = kn_ref[0]
    vw_ref[:, :, 0:B_HALO] = vp_ref[0]
    vw_ref[:, :, B_HALO:B_HALO + B_TILE] = vo_ref[0]
    vw_ref[:, :, B_HALO + B_TILE:] = vn_ref[0]
    tail = _ones_tail(B_SPAN, _MXU_DTYPE)
    upper = lax.broadcasted_iota(jnp.int32, (LANES, B_UNIT), 0) < HEAD_DIM

    n_units = B_TILE_ROWS // B_UNIT_ROWS

    def span_offset(u):
        r0 = jnp.clip(i0 + u * B_UNIT_ROWS - B_WIN_ROWS // 2, 0, rows - B_SPAN_ROWS)
        return pl.multiple_of((r0 - (i0 - B_WIN_ROWS // 2)) * GRID_W, LANES)

    def scores(u, pair):
        qt = q_ref[0, pair, :, pl.ds(pl.multiple_of(u * B_UNIT, B_UNIT), B_UNIT)]
        zero = jnp.zeros_like(qt)
        qt2 = jnp.concatenate([jnp.where(upper, qt, zero), jnp.where(upper, zero, qt)], axis=1)
        return _dot(kw_ref[pl.ds(span_offset(u), B_SPAN), pair * LANES:(pair + 1) * LANES], qt2)

    s_ref[...] = scores(0, 0)

    def unit(fixed_shift, u, carry):
        i0u = i0 + u * B_UNIT_ROWS
        off = span_offset(u)
        variant = jnp.where(i0u == 0, 0, jnp.where(i0u == rows - B_UNIT_ROWS, 2, 1))
        qoff = pl.multiple_of(u * B_UNIT, B_UNIT)
        outs = []
        s_next = s_ref[...]
        for pair in range(B_PAIRS):
            s_pair = s_next
            if pair + 1 < B_PAIRS:
                s_next = scores(u, pair + 1)
            else:
                s_ref[...] = scores(jnp.minimum(u + 1, n_units - 1), 0)
            for odd in range(2):
                s = s_pair[:, odd * B_UNIT:(odd + 1) * B_UNIT] + bias_ref[variant, 2 * pair + odd]
                if not fixed_shift:
                    s = s - jnp.max(s, axis=0, keepdims=True)
                p = jnp.exp2(s).astype(_MXU_DTYPE)
                vt = vw_ref[pair, odd * HEAD_DIM:(odd + 1) * HEAD_DIM, pl.ds(off, B_SPAN)]
                acc = _dot(jnp.concatenate([vt, tail], axis=0), p)
                outs.append(acc[:HEAD_DIM] / acc[HEAD_DIM:HEAD_DIM + 1])
        o_ref[0, pl.ds(qoff, B_UNIT), :] = jnp.concatenate(outs, axis=0).T.astype(o_ref.dtype)
        return carry

    small = shift_ref[1] > 0.5

    @pl.when(small)
    def _():
        lax.fori_loop(0, n_units, functools.partial(unit, True), 0, unroll=n_units)

    @pl.when(jnp.logical_not(small))
    def _():
        lax.fori_loop(0, n_units, functools.partial(unit, False), 0)


def _mixer_b(qbt, kb, vbt, blocks, shift, B, S):
    rows = S // GRID_W
    sel = jnp.asarray(_mixer_b_block_index(rows))
    nt = rows // B_TILE_ROWS
    per = B_TILE // B_HALO
    nh = S // B_HALO
    kb3 = kb.reshape(B, S, B_W)
    prev = lambda t: jnp.maximum(t * per - 1, 0)
    nxt = lambda t: jnp.minimum((t + 1) * per, nh - 1)
    k_edge, k_own = (1, B_HALO, B_W), (1, B_TILE, B_W)
    t_edge, t_own = (1, B_PAIRS, LANES, B_HALO), (1, B_PAIRS, LANES, B_TILE)
    o = pl.pallas_call(
        functools.partial(_mixer_b_kernel, rows=rows),
        grid=(B, nt),
        in_specs=[
            pl.BlockSpec(t_own, lambda b, t: (b, 0, 0, t)),
            pl.BlockSpec(k_edge, lambda b, t: (b, prev(t), 0)),
            pl.BlockSpec(k_own, lambda b, t: (b, t, 0)),
            pl.BlockSpec(k_edge, lambda b, t: (b, nxt(t), 0)),
            pl.BlockSpec(t_edge, lambda b, t: (b, 0, 0, prev(t))),
            pl.BlockSpec(t_own, lambda b, t: (b, 0, 0, t)),
            pl.BlockSpec(t_edge, lambda b, t: (b, 0, 0, nxt(t))),
            _const_spec(blocks.shape),
            pl.BlockSpec(memory_space=pltpu.SMEM),
            pl.BlockSpec(memory_space=pltpu.SMEM),
        ],
        out_specs=pl.BlockSpec(k_own, lambda b, t: (b, t, 0)),
        out_shape=jax.ShapeDtypeStruct((B, S, B_W), _MXU_DTYPE),
        scratch_shapes=[pltpu.VMEM((B_TILE + 2 * B_HALO, B_W), _MXU_DTYPE),
                        pltpu.VMEM((B_PAIRS, LANES, B_TILE + 2 * B_HALO), _MXU_DTYPE),
                        pltpu.VMEM((3, B_HEADS, B_SPAN, B_UNIT), jnp.float32),
                        pltpu.VMEM((B_SPAN, 2 * B_UNIT), jnp.float32)],
        compiler_params=_params(("arbitrary",) * 2),
        name="mixer_b",
    )(qbt, kb3, kb3, kb3, vbt, vbt, vbt, blocks, sel, shift)
    return o.reshape(B * S, B_W)


C_TQ = 4096
C_TK = 512
C_UNIT = 512
C_UNROLL = 1


def _mixer_c_kernel(q_ref, k_ref, vt_ref, shift_ref, o_ref, qcat_ref, m_ref, acc_ref, s_ref, *, n_kv):
    acc_ref[...] = jnp.zeros(acc_ref.shape, jnp.float32)
    for g in range(C_GROUP):
        qcat_ref[:, g * C_TQ:(g + 1) * C_TQ] = q_ref[0, g]
    n_units = C_GROUP * C_TQ // C_UNIT

    def keys(j):
        return k_ref[0, 0, pl.ds(pl.multiple_of(j * C_TK, C_TK), C_TK), :]

    def scores(k, u):
        return _dot(k, qcat_ref[:, u * C_UNIT:(u + 1) * C_UNIT])

    def sweep(update, unroll):
        s_ref[...] = scores(keys(0), 0)

        def step(j, carry):
            k = keys(j)
            vt = vt_ref[0, 0, :, pl.ds(pl.multiple_of(j * C_TK, C_TK), C_TK)]
            s_next = s_ref[...]
            for u in range(n_units):
                s = s_next
                if u + 1 < n_units:
                    s_next = scores(k, u + 1)
                else:
                    s_ref[...] = scores(keys(jnp.minimum(j + 1, n_kv - 1)), 0)
                update(s, vt, slice(u * C_UNIT, (u + 1) * C_UNIT))
            return carry

        lax.fori_loop(0, n_kv, step, 0, unroll=unroll)

    def fixed_shift(s, vt, cols):
        acc_ref[:, cols] += _dot(vt, jnp.exp2(s - shift_ref[0]).astype(_MXU_DTYPE))

    def running_max(s, vt, cols):
        m_prev = m_ref[:, cols]
        m_new = jnp.maximum(m_prev, jnp.max(s, axis=0, keepdims=True))
        alpha = jnp.exp2(m_prev - m_new)
        p = jnp.exp2(s - m_new).astype(_MXU_DTYPE)
        acc_ref[:, cols] = alpha * acc_ref[:, cols] + _dot(vt, p)
        m_ref[:, cols] = m_new

    small = shift_ref[1] > 0.5

    @pl.when(small)
    def _():
        sweep(fixed_shift, C_UNROLL)

    @pl.when(jnp.logical_not(small))
    def _():
        m_ref[...] = jnp.full(m_ref.shape, -jnp.inf, jnp.float32)
        sweep(running_max, 1)

    o_t = jnp.concatenate(
        [acc_ref[:HEAD_DIM, g * C_TQ:(g + 1) * C_TQ] / acc_ref[HEAD_DIM:HEAD_DIM + 1, g * C_TQ:(g + 1) * C_TQ]
         for g in range(C_GROUP)], axis=0)
    o_ref[0] = o_t.T.astype(o_ref.dtype)


def _mixer_c(qc_t, kc, vc_t, shift, B, S):
    o = pl.pallas_call(
        functools.partial(_mixer_c_kernel, n_kv=S // C_TK),
        grid=(B, C_KV_HEADS, S // C_TQ),
        in_specs=[
            pl.BlockSpec((1, C_GROUP, HEAD_DIM, C_TQ), lambda b, kv, i: (b, kv, 0, i)),
            pl.BlockSpec((1, 1, S, HEAD_DIM), lambda b, kv, i: (b, kv, 0, 0)),
            pl.BlockSpec((1, 1, VT_ROWS, S), lambda b, kv, i: (b, kv, 0, 0)),
            pl.BlockSpec(memory_space=pltpu.SMEM),
        ],
        out_specs=pl.BlockSpec((1, C_TQ, C_GROUP * HEAD_DIM), lambda b, kv, i: (b, i, kv)),
        out_shape=jax.ShapeDtypeStruct((B, S, C_QW), _MXU_DTYPE),
        scratch_shapes=[pltpu.VMEM((HEAD_DIM, C_GROUP * C_TQ), _MXU_DTYPE),
                        pltpu.VMEM((1, C_GROUP * C_TQ), jnp.float32),
                        pltpu.VMEM((VT_ROWS, C_GROUP * C_TQ), jnp.float32),
                        pltpu.VMEM((C_TK, C_UNIT), jnp.float32)],
        compiler_params=_params(("arbitrary",) * 3),
        name="mixer_c",
    )(qc_t, kc, vc_t, shift)
    return o.reshape(B * S, C_QW)


MERGE_TM = 512
MERGE_GATE_BLOCK = 512


def _merge_kernel(x_ref, g_ref, oa0_ref, oa1_ref, oa2_ref, l0_ref, l1_ref, l2_ref, ob_ref, oc_ref,
                  wg0_ref, wg1_ref, wg2_ref, wg3_ref, wg4_ref, wg5_ref, pa_ref, pb_ref, pc_ref, wo_ref,
                  out_ref, *scratch):
    tm = x_ref.shape[0]
    gate_refs = (wg0_ref, wg1_ref, wg2_ref, wg3_ref, wg4_ref, wg5_ref)
    cast = lambda ref: ref[...].astype(_MXU_DTYPE)

    def token_major(ref, scr):
        rate = ref.shape[1]
        halves = range(A_OUT // LANES)
        for r in range(rate):
            for j in halves:
                scr[j, pl.ds(r, tm // rate, stride=rate), :] = ref[0, r, :, j * LANES:(j + 1) * LANES]
        return jnp.concatenate([scr[j] for j in halves], axis=1)

    x = x_ref[...]
    h = _rms(x, g_ref[...]).astype(_MXU_DTYPE)
    n_parts = D_MODEL // MERGE_GATE_BLOCK
    branches = [None, _dot(ob_ref[...], cast(pb_ref)), _dot(oc_ref[...], cast(pc_ref))]
    gates = {(b, part): jax.nn.sigmoid(_dot(h, cast(gate_refs[b * n_parts + part])))
             for b in (1, 2, 0) for part in range(n_parts)}
    oa0, l0 = oa0_ref[...], l0_ref[...]
    oa1, l1 = token_major(oa1_ref, scratch[0]), token_major(l1_ref, scratch[1])
    oa2, l2 = token_major(oa2_ref, scratch[2]), token_major(l2_ref, scratch[3])
    mx = jnp.maximum(jnp.maximum(l0, l1), l2)
    w0, w1, w2 = jnp.exp2(l0 - mx), jnp.exp2(l1 - mx), jnp.exp2(l2 - mx)
    o_a = (w0 * oa0 + w1 * oa1 + w2 * oa2) / (w0 + w1 + w2)
    branches[0] = _dot(o_a.astype(_MXU_DTYPE), cast(pa_ref))
    parts = []
    for part in range(n_parts):
        cols = slice(part * MERGE_GATE_BLOCK, (part + 1) * MERGE_GATE_BLOCK)
        merged = sum(gates[b, part] * branches[b][:, cols] for b in range(1, N_BRANCH))
        parts.append((merged + gates[0, part] * branches[0][:, cols]).astype(_MXU_DTYPE))
    out_ref[...] = x + _dot(jnp.concatenate(parts, axis=1), cast(wo_ref))


def _merge(x2, g1, oa, lse, ob, oc, w_in, w_br_a, w_br_b, w_br_c, w_o, layer, B, S):
    M = x2.shape[0]
    tm = MERGE_TM
    per_seq = S // tm
    row = lambda i: (i, 0)
    tile = lambda w: pl.BlockSpec((tm, w), row)
    gate0 = _ZG // MERGE_GATE_BLOCK

    def layer_weight(w, block=None, at=0):
        block = w.shape[2] if block is None else block
        return pl.BlockSpec((None, w.shape[1], block), lambda i: (layer, 0, at), pipeline_mode=pl.Buffered(1))

    def dilated(rate):
        return pl.BlockSpec((1, rate, tm // rate, A_OUT), lambda i: (i // per_seq, 0, i % per_seq, 0))

    r1, r2 = A_PATTERNS[1][1], A_PATTERNS[2][1]
    return pl.pallas_call(
        _merge_kernel,
        grid=(M // tm,),
        in_specs=[tile(D_MODEL), _const_spec((1, D_MODEL)),
                  tile(A_OUT), dilated(r1), dilated(r2), tile(A_OUT), dilated(r1), dilated(r2),
                  tile(B_W), tile(C_QW),
                  *[layer_weight(w_in, MERGE_GATE_BLOCK, gate0 + j)
                    for j in range(N_BRANCH * D_MODEL // MERGE_GATE_BLOCK)],
                  layer_weight(w_br_a), layer_weight(w_br_b), layer_weight(w_br_c), layer_weight(w_o)],
        out_specs=tile(D_MODEL),
        out_shape=jax.ShapeDtypeStruct((M, D_MODEL), jnp.float32),
        scratch_shapes=[pltpu.VMEM((A_OUT // LANES, tm, LANES), jnp.float32)] * 4,
        compiler_params=_params(("arbitrary",)),
        name="gated_merge",
    )(x2, g1, oa[0].reshape(M, A_OUT), oa[1], oa[2], lse[0].reshape(M, A_OUT), lse[1], lse[2],
      ob, oc, *([w_in] * (N_BRANCH * D_MODEL // MERGE_GATE_BLOCK)), w_br_a, w_br_b, w_br_c, w_o)


FFN_TM = 512
FFN_CHUNK = 256
FFN_DOWN_CHUNK = D_FF // 2


def _ffn_kernel(x_ref, g_ref, wup_ref, wdown_ref, out_ref, act_ref):
    x = x_ref[...]
    xn = _rms(x, g_ref[...]).astype(_MXU_DTYPE)
    for c in range(D_FF // FFN_CHUNK):
        cs = slice(c * FFN_CHUNK, (c + 1) * FFN_CHUNK)
        a = _dot(xn, wup_ref[:, cs].astype(_MXU_DTYPE))
        b = _dot(xn, wup_ref[:, D_FF + c * FFN_CHUNK:D_FF + (c + 1) * FFN_CHUNK].astype(_MXU_DTYPE))
        act_ref[:, cs] = (a * jax.nn.sigmoid(a) * b).astype(act_ref.dtype)
    y = x
    for c in range(D_FF // FFN_DOWN_CHUNK):
        rows = slice(c * FFN_DOWN_CHUNK, (c + 1) * FFN_DOWN_CHUNK)
        y = y + _dot(act_ref[:, rows], wdown_ref[rows, :].astype(_MXU_DTYPE))
    out_ref[...] = y


def _ffn(x2, g2, w_up, w_down, layer):
    M = x2.shape[0]
    tm = FFN_TM
    row = lambda i: (i, 0)
    layer_weight = lambda w: pl.BlockSpec((None,) + w.shape[1:], lambda i: (layer, 0, 0),
                                          pipeline_mode=pl.Buffered(1))
    return pl.pallas_call(
        _ffn_kernel,
        grid=(M // tm,),
        in_specs=[pl.BlockSpec((tm, D_MODEL), row), _const_spec((1, D_MODEL)),
                  layer_weight(w_up), layer_weight(w_down)],
        out_specs=pl.BlockSpec((tm, D_MODEL), row),
        out_shape=jax.ShapeDtypeStruct((M, D_MODEL), jnp.float32),
        scratch_shapes=[pltpu.VMEM((tm, D_FF), _MXU_DTYPE)],
        compiler_params=_params(("arbitrary",)),
        name="swiglu_ffn",
    )(x2, g2, w_up, w_down)


def _t5_bucket(rel):
    half = T5_BUCKETS // 2
    max_exact = half // 2
    ret = jnp.where(rel > 0, half, 0)
    n = jnp.abs(rel)
    nf = jnp.maximum(n, 1).astype(jnp.float32)
    large = max_exact + (jnp.log(nf / max_exact) / math.log(T5_MAX_DIST / max_exact)
                         * (half - max_exact)).astype(jnp.int32)
    large = jnp.minimum(large, half - 1)
    return ret + jnp.where(n < max_exact, n, large)


def _mixer_a_bias(table_g, rate):
    j = jnp.arange(A_SPAN)[:, None]
    i = jnp.arange(A_UNIT)[None, :]
    step = j - A_RADIUS - i
    onehot = (_t5_bucket(step * rate)[:, :, None] == jnp.arange(T5_BUCKETS)).astype(jnp.float32)
    bias = jnp.einsum("jib,bh->hji", onehot, table_g.astype(jnp.float32) * LOG2_E,
                      precision=lax.Precision.HIGHEST)
    return jnp.where((jnp.abs(step) <= A_RADIUS)[None], bias, NEG)


B_N_DR = 2 * B_WIN_ROWS - 1


def _mixer_b_blocks(rpb, shift):
    c = np.arange(GRID_W)
    c0 = np.clip(c - B_WIN_COLS // 2, 0, GRID_W - B_WIN_COLS)
    col_ok = (c[:, None] >= c0[None, :]) & (c[:, None] < c0[None, :] + B_WIN_COLS)
    dc = np.clip(c[:, None] - c[None, :] + B_WIN_COLS - 1, 0, 2 * B_WIN_COLS - 2)
    pick_c = ((dc[..., None] == np.arange(2 * B_WIN_COLS - 1)) & col_ok[..., None]).astype(np.float32)
    blocks = jnp.einsum("hdk,xyk->hdxy", rpb.astype(jnp.float32) * LOG2_E, pick_c,
                        precision=lax.Precision.HIGHEST)
    blocks = jnp.where(col_ok[None, None], blocks - shift, NEG)
    blocks = jnp.concatenate([blocks, jnp.full((B_HEADS, 1, GRID_W, GRID_W), NEG, jnp.float32)], axis=1)
    return jnp.concatenate([blocks, blocks], axis=-1)


def _mixer_b_block_index(rows):
    first_query_row = np.array([0, B_WIN_ROWS // 2, rows - B_UNIT_ROWS])
    i = first_query_row[:, None, None] + np.arange(B_UNIT_ROWS)[None, None, :]
    r0 = np.clip(first_query_row - B_WIN_ROWS // 2, 0, rows - B_SPAN_ROWS)
    ik = r0[:, None, None] + np.arange(B_SPAN_ROWS)[None, :, None]
    rs = np.clip(i - B_WIN_ROWS // 2, 0, rows - B_WIN_ROWS)
    row_ok = (ik >= rs) & (ik < rs + B_WIN_ROWS)
    return np.where(row_ok, ik - i + B_WIN_ROWS - 1, B_N_DR).astype(np.int32).reshape(-1)


def _rope_tables(S):
    rows = S // GRID_W
    inv = ROPE_THETA ** (-jnp.arange(0, ROPE_AXIS_DIM, 2, dtype=jnp.float32) / ROPE_AXIS_DIM)
    d = np.arange(LANES) % HEAD_DIM
    inv_lane = inv[d % (ROPE_AXIS_DIM // 2)][None, :]
    is_col = (d >= ROPE_AXIS_DIM)[None, None, :]
    first = ((d % ROPE_AXIS_DIM) < ROPE_AXIS_DIM // 2)[None, :]
    tables = []
    for n in (rows, GRID_W):
        ang = jnp.arange(n, dtype=jnp.float32)[:, None] * inv_lane
        sin = jnp.sin(ang)
        tables.append((jnp.cos(ang), jnp.where(first, -sin, 0.0), jnp.where(first, 0.0, sin)))
    return tuple(jnp.where(is_col, by_col[None, :, :], by_row[:, None, :]).reshape(S, LANES)
                 for by_row, by_col in zip(*tables))


def _softmax_shift(gain_q, gain_k, bias_abs_max):
    bound = (HEAD_DIM * QK_SCALE * LOG2_E * ROUNDING_SLACK * jnp.max(jnp.abs(gain_q)) * jnp.max(jnp.abs(gain_k))
             + LOG2_E * bias_abs_max)
    small = bound <= SHIFT_LIMIT
    return jnp.stack([jnp.where(small, bound, 0.0), small.astype(jnp.float32)]).astype(jnp.float32)


def _block_diag_ones():
    i = np.arange(PROJ_TILE) // HEAD_DIM
    return jnp.asarray(i[:, None] == i[None, :], _MXU_DTYPE)


def kernel(x, rel_bias_table, norm1, w_in, qk_gain, nat_rpb, w_br_a, w_br_b, w_br_c, w_o,
           norm2, w_up, w_down):
    B, S, D = x.shape
    depth = w_in.shape[0]
    M = B * S
    x2 = x.reshape(M, D)
    rope = _rope_tables(S)
    ones_bd = _block_diag_ones()
    a_bias = [_mixer_a_bias(rel_bias_table[:, g * A_HEADS:(g + 1) * A_HEADS], rate)
              for g, (_, rate) in enumerate(A_PATTERNS)]
    for l in range(depth):
        gq = qk_gain[l]
        tile = lambda g, n, s: jnp.tile(g * s, n)
        gain_row = jnp.concatenate([
            tile(gq[0], A_W // HEAD_DIM, QK_SCALE * LOG2_E), tile(gq[1], A_W // HEAD_DIM, 1.0),
            tile(gq[2], B_HEADS, QK_SCALE * LOG2_E), tile(gq[3], B_HEADS, 1.0),
            tile(gq[4], C_Q_HEADS, QK_SCALE * LOG2_E), tile(gq[5], C_KV_HEADS, 1.0)])[None, :]
        g1 = norm1[l][None, :]
        za0, za1, za2, qbt, kb, vbt, qc, kc, vc = _qkv_projection(
            x2, g1, w_in, l, gain_row, ones_bd, rope, B, S)
        za =(za0.reshape(B, 1, S, ZA_W), za1, za2)
        oa, lse = [], []
        for g, (_, rate) in enumerate(A_PATTERNS):
            table_g = rel_bias_table[:, g * A_HEADS:(g + 1) * A_HEADS]
            shift_a = _softmax_shift(gq[0], gq[1], jnp.max(jnp.abs(table_g)))
            o_g, l_g = _mixer_a_group(za[g], a_bias[g] - shift_a[0], shift_a, rate, B, S)
            oa.append(o_g)
            lse.append(l_g)
        shift_b = _softmax_shift(gq[2], gq[3], jnp.max(jnp.abs(nat_rpb[l])))
        ob = _mixer_b(qbt, kb, vbt, _mixer_b_blocks(nat_rpb[l], shift_b[0]), shift_b, B, S)
        oc = _mixer_c(qc, kc, vc, _softmax_shift(gq[4], gq[5], 0.0), B, S)
        x2 = _merge(x2, g1, oa, lse, ob, oc, w_in, w_br_a, w_br_b, w_br_c, w_o, l, B, S)
        x2 = _ffn(x2, norm2[l][None, :], w_up, w_down, l)
    return x2.reshape(B, S, D)
```

```python
import functools
import math

import jax
import jax.numpy as jnp
import numpy as np
from jax import lax
from jax.experimental import pallas as pl
from jax.experimental.pallas import tpu as pltpu

_MXU_DTYPE = jnp.bfloat16

D_MODEL = 1024
HEAD_DIM = 64
GRID_W = 64
RMS_EPS = 1e-6
NEG = -1e30
A_PATTERNS = ((128, 1), (512, 4), (2048, 16))
A_GROUPS = 3
A_HEADS = 4
A_W = A_GROUPS * A_HEADS * HEAD_DIM
A_OUT = A_HEADS * HEAD_DIM
A_RADIUS = 64
B_HEADS = 8
B_W = B_HEADS * HEAD_DIM
B_WIN_ROWS = 8
B_WIN_COLS = 16
C_Q_HEADS = 8
C_KV_HEADS = 2
C_GROUP = C_Q_HEADS // C_KV_HEADS
C_QW = C_Q_HEADS * HEAD_DIM
C_KVW = C_KV_HEADS * HEAD_DIM
ROPE_THETA = 10000.0
ROPE_AXIS_DIM = HEAD_DIM // 2
T5_BUCKETS = 32
T5_MAX_DIST = 1024
N_BRANCH = 3
D_FF = math.ceil(8 * D_MODEL / 3 / 256) * 256
QK_SCALE = HEAD_DIM ** -0.5
LOG2_E = math.log2(math.e)

V7X_VMEM_BYTES = 64 * 1024 * 1024
VMEM_LIMIT_BYTES = V7X_VMEM_BYTES * 7 // 8
LANES = 128
BF16_SUBLANES = 16
MXU_COLS = 256
VT_ROWS = HEAD_DIM + BF16_SUBLANES
SHIFT_LIMIT = 60.0
ROUNDING_SLACK = 1.02

_OFF = np.cumsum([0, A_W, A_W, A_W, B_W, B_W, B_W, C_QW, C_KVW, C_KVW]).tolist()
(_QA, _KA, _VA, _QB, _KB, _VB, _QC, _KC, _VC, _ZG) = _OFF

ZA_W = 3 * A_OUT
B_PAIRS = B_HEADS // 2


def _params(sem):
    return pltpu.CompilerParams(dimension_semantics=sem, vmem_limit_bytes=VMEM_LIMIT_BYTES)


def _const_spec(shape):
    nd = len(shape)
    return pl.BlockSpec(shape, lambda *_: (0,) * nd, pipeline_mode=pl.Buffered(1))


def _rms(x, g):
    return x * lax.rsqrt(jnp.mean(x * x, axis=-1, keepdims=True) + RMS_EPS) * g


def _dot(a, b):
    return jnp.dot(a, b, preferred_element_type=jnp.float32)


def _dot_nt(a, b):
    return lax.dot_general(a, b, (((1,), (1,)), ((), ())), preferred_element_type=jnp.float32)


def _ones_tail(width, dtype):
    return (lax.broadcasted_iota(jnp.int32, (VT_ROWS - HEAD_DIM, width), 0) == 0).astype(dtype)


PROJ_TM = 1024
PROJ_TILE = MXU_COLS
PROJ_W_BLOCK = 1536
_N_NORM = 2 * A_W + 2 * B_W
_N_ROPE = C_QW + C_KVW


def _proj_kernel(x_ref, g_ref, w0_ref, w1_ref, w2_ref, gain_ref, ones_ref, cos_ref, s1_ref, s2_ref,
                 za0_ref, za1_ref, za2_ref, qbt_ref, kb_ref, vbt_ref, qc_ref, kc_ref, vc_ref, dil_ref):
    tm = x_ref.shape[0]
    w_refs = (w0_ref, w1_ref, w2_ref)
    halves = [_rms(x_ref[r * (tm // 2):(r + 1) * (tm // 2)], g_ref[...]).astype(_MXU_DTYPE) for r in range(2)]
    za_refs = (za0_ref, za1_ref, za2_ref)

    def head_norm(acc, c0, width):
        sq = (acc * acc).astype(_MXU_DTYPE)
        ms = _dot(sq, ones_ref[:width, :width]) * (1.0 / HEAD_DIM)
        return acc * lax.rsqrt(ms + RMS_EPS) * gain_ref[:, c0:c0 + width]

    def store_group(g, section, val):
        cols = slice(section * A_OUT, (section + 1) * A_OUT)
        rate = A_PATTERNS[g][1]
        if rate == 1:
            za0_ref[:, cols] = val.astype(za0_ref.dtype)
            return
        for j in range(A_OUT // LANES):
            dil_ref[j] = val[:, j * LANES:(j + 1) * LANES]
        for r in range(rate):
            picked = [dil_ref[j, pl.ds(r, tm // rate, stride=rate), :] for j in range(A_OUT // LANES)]
            za_refs[g][0, r, :, cols] = jnp.concatenate(picked, axis=1).astype(za_refs[g].dtype)

    def store_pairs(ref, t, val):
        vt = val.T.astype(ref.dtype)
        ref[0, 2 * t] = vt[:LANES]
        ref[0, 2 * t + 1] = vt[LANES:]

    def rotary(y, width):
        reps = width // LANES
        table = lambda ref: jnp.concatenate([ref[...]] * reps, axis=1) if reps > 1 else ref[...]
        return (y * table(cos_ref) + pltpu.roll(y, width - 16, 1) * table(s1_ref)
                + pltpu.roll(y, 16, 1) * table(s2_ref))

    def store_qc(t, acc, gain_at):
        yt = rotary(head_norm(acc, gain_at, PROJ_TILE), PROJ_TILE).T.astype(qc_ref.dtype)
        heads = PROJ_TILE // HEAD_DIM
        for j in range(heads):
            qc_ref[0, heads * t + j] = yt[j * HEAD_DIM:(j + 1) * HEAD_DIM]

    def store_kc_vc(acc, gain_at):
        y = rotary(head_norm(acc[:, :LANES], gain_at, LANES), LANES).astype(kc_ref.dtype)
        acc_t = acc[:, LANES:].T.astype(vc_ref.dtype)
        tail = _ones_tail(tm, vc_ref.dtype)
        for kv in range(C_KV_HEADS):
            kc_ref[0, kv] = y[:, kv * HEAD_DIM:(kv + 1) * HEAD_DIM]
            vc_ref[0, kv, :HEAD_DIM] = acc_t[kv * HEAD_DIM:(kv + 1) * HEAD_DIM]
            vc_ref[0, kv, HEAD_DIM:] = tail

    jobs = []
    for section, src in enumerate((_QA, _KA)):
        for g in range(A_GROUPS):
            gain_at = section * A_W + g * A_OUT
            jobs.append((src + g * A_OUT, A_OUT, lambda acc, gain_at=gain_at, g=g, section=section:
                         store_group(g, section, head_norm(acc, gain_at, A_OUT))))
    for t in range(B_W // PROJ_TILE):
        gain_at = 2 * A_W + t * PROJ_TILE
        jobs.append((_QB + t * PROJ_TILE, PROJ_TILE, lambda acc, gain_at=gain_at, t=t:
                     store_pairs(qbt_ref, t, head_norm(acc, gain_at, PROJ_TILE))))
    for t in range(B_W // PROJ_TILE):
        gain_at = 2 * A_W + B_W + t * PROJ_TILE
        def store_kb(acc, gain_at=gain_at, t=t):
            kb_ref[:, t * PROJ_TILE:(t + 1) * PROJ_TILE] = head_norm(acc, gain_at, PROJ_TILE).astype(kb_ref.dtype)
        jobs.append((_KB + t * PROJ_TILE, PROJ_TILE, store_kb))
    for t in range(C_QW // PROJ_TILE):
        jobs.append((_QC + t * PROJ_TILE, PROJ_TILE,
                     lambda acc, t=t: store_qc(t, acc, _N_NORM + t * PROJ_TILE)))
    jobs.append((_KC, 2 * C_KVW, lambda acc: store_kc_vc(acc, _N_NORM + C_QW)))
    for g in range(A_GROUPS):
        jobs.append((_VA + g * A_OUT, A_OUT, lambda acc, g=g: store_group(g, 2, acc)))
    for t in range(B_W // PROJ_TILE):
        jobs.append((_VB + t * PROJ_TILE, PROJ_TILE, lambda acc, t=t: store_pairs(vbt_ref, t, acc)))

    def product(n):
        src, width, _ = jobs[n]
        blk, col = divmod(src, PROJ_W_BLOCK)
        w = w_refs[blk][:, col:col + width].astype(_MXU_DTYPE)
        return jnp.concatenate([_dot(h, w) for h in halves], axis=0)

    acc_next = product(0)
    for n, (_, _, consume) in enumerate(jobs):
        acc = acc_next
        if n + 1 < len(jobs):
            acc_next = product(n + 1)
        consume(acc)


def _qkv_projection(x2, g1, w_in, layer, gain_row, ones_bd, rope, B, S):
    M = x2.shape[0]
    tm = PROJ_TM
    per_seq = S // tm
    w_block = lambda j: pl.BlockSpec((None, D_MODEL, PROJ_W_BLOCK), lambda i: (layer, 0, j),
                                     pipeline_mode=pl.Buffered(1))
    cos_t, s1_t, s2_t = rope
    r1, r2 = A_PATTERNS[1][1], A_PATTERNS[2][1]
    row = lambda i: (i, 0)
    pos = lambda i: (i % per_seq, 0)
    hm = lambda i: (i // per_seq, 0, i % per_seq, 0)
    hm_t = lambda i: (i // per_seq, 0, 0, i % per_seq)
    cd = _MXU_DTYPE
    return pl.pallas_call(
        _proj_kernel,
        grid=(M // tm,),
        in_specs=[
            pl.BlockSpec((tm, D_MODEL), row),
            _const_spec((1, D_MODEL)),
            w_block(0), w_block(1), w_block(2),
            _const_spec((1, _N_NORM + _N_ROPE)),
            _const_spec((PROJ_TILE, PROJ_TILE)),
            pl.BlockSpec((tm, LANES), pos),
            pl.BlockSpec((tm, LANES), pos),
            pl.BlockSpec((tm, LANES), pos),
        ],
        out_specs=[
            pl.BlockSpec((tm, ZA_W), row),
            pl.BlockSpec((1, r1, tm // r1, ZA_W), hm),
            pl.BlockSpec((1, r2, tm // r2, ZA_W), hm),
            pl.BlockSpec((1, B_PAIRS, LANES, tm), hm_t),
            pl.BlockSpec((tm, B_W), row),
            pl.BlockSpec((1, B_PAIRS, LANES, tm), hm_t),
            pl.BlockSpec((1, C_Q_HEADS, HEAD_DIM, tm), hm_t),
            pl.BlockSpec((1, C_KV_HEADS, tm, HEAD_DIM), hm),
            pl.BlockSpec((1, C_KV_HEADS, VT_ROWS, tm), hm_t),
        ],
        out_shape=[
            jax.ShapeDtypeStruct((M, ZA_W), cd),
            jax.ShapeDtypeStruct((B, r1, S // r1, ZA_W), cd),
            jax.ShapeDtypeStruct((B, r2, S // r2, ZA_W), cd),
            jax.ShapeDtypeStruct((B, B_PAIRS, LANES, S), cd),
            jax.ShapeDtypeStruct((M, B_W), cd),
            jax.ShapeDtypeStruct((B, B_PAIRS, LANES, S), cd),
            jax.ShapeDtypeStruct((B, C_Q_HEADS, HEAD_DIM, S), cd),
            jax.ShapeDtypeStruct((B, C_KV_HEADS, S, HEAD_DIM), cd),
            jax.ShapeDtypeStruct((B, C_KV_HEADS, VT_ROWS, S), cd),
        ],
        scratch_shapes=[pltpu.VMEM((A_OUT // LANES, tm, LANES), jnp.float32)],
        compiler_params=_params(("arbitrary",)),
        name="qkv_projection",
    )(x2, g1, w_in, w_in, w_in, gain_row, ones_bd, cos_t, s1_t, s2_t)


A_UNIT = 4 * A_RADIUS
A_SPAN = A_UNIT + 2 * A_RADIUS
A_TILE = 4096


def _mixer_a_kernel(q_ref, kp_ref, ko_ref, kn_ref, vp_ref, vo_ref, vn_ref, bias_ref, shift_ref,
                    o_ref, lse_ref, kw_ref, vw_ref, s_ref, *, tl, seq_len):
    n_res = q_ref.shape[1]
    l0 = pl.program_id(2) * tl
    for res in range(n_res):
        kw_ref[res, 0:A_RADIUS] = kp_ref[0, res]
        kw_ref[res, A_RADIUS:A_RADIUS + tl] = ko_ref[0, res]
        kw_ref[res, A_RADIUS + tl:] = kn_ref[0, res]
        vw_ref[res, 0:A_RADIUS] = vp_ref[0, res]
        vw_ref[res, A_RADIUS:A_RADIUS + tl] = vo_ref[0, res]
        vw_ref[res, A_RADIUS + tl:] = vn_ref[0, res]
    even_q = lax.broadcasted_iota(jnp.int32, (A_UNIT, LANES), 1) < HEAD_DIM
    lane_k = lax.broadcasted_iota(jnp.int32, (A_SPAN, LANES), 1)

    per_res = tl // A_UNIT
    n_units = n_res * per_res

    def place(n):
        return n // per_res, pl.multiple_of((n % per_res) * A_UNIT, A_UNIT)

    def scores(n, pair):
        res, off = place(n)
        q = q_ref[0, res, pl.ds(off, A_UNIT), pair * LANES:(pair + 1) * LANES]
        zero = jnp.zeros_like(q)
        q2 = jnp.concatenate([jnp.where(even_q, q, zero), jnp.where(even_q, zero, q)], axis=0)
        return _dot_nt(kw_ref[res, pl.ds(off, A_SPAN), pair * LANES:(pair + 1) * LANES], q2)

    s_ref[...] = scores(0, 0)

    def unit(fixed_shift, u, carry):
        res, off = place(u)
        top_ok = l0 + off - A_RADIUS >= 0
        bot_ok = l0 + off + A_UNIT + A_RADIUS <= seq_len
        outs, lses = [], []
        s_next = s_ref[...]
        for pair in range(A_HEADS // 2):
            s_pair = s_next
            if pair + 1 < A_HEADS // 2:
                s_next = scores(u, pair + 1)
            else:
                s_ref[...] = scores(jnp.minimum(u + 1, n_units - 1), 0)
            v_slab = vw_ref[res, pl.ds(off, A_SPAN), pair * LANES:(pair + 1) * LANES]
            for odd in range(2):
                s = s_pair[:, odd * A_UNIT:(odd + 1) * A_UNIT] + bias_ref[2 * pair + odd]
                s = jnp.concatenate([jnp.where(top_ok, s[:A_RADIUS], NEG), s[A_RADIUS:A_SPAN - A_RADIUS],
                                     jnp.where(bot_ok, s[A_SPAN - A_RADIUS:], NEG)], axis=0)
                if fixed_shift:
                    m = shift_ref[0]
                    p = jnp.exp2(s).astype(_MXU_DTYPE)
                else:
                    m = jnp.max(s, axis=0, keepdims=True)
                    p = jnp.exp2(s - m).astype(_MXU_DTYPE)
                den_row = (1 - odd) * HEAD_DIM
                mine = (lane_k >= odd * HEAD_DIM) & (lane_k < (odd + 1) * HEAD_DIM)
                v_aug = jnp.where(mine, v_slab, (lane_k == den_row).astype(v_slab.dtype))
                acc = lax.dot_general(v_aug, p, (((0,), (0,)), ((), ())),
                                      preferred_element_type=jnp.float32)
                den = acc[den_row:den_row + 1]
                outs.append(acc[odd * HEAD_DIM:(odd + 1) * HEAD_DIM] / den)
                lses.append(jnp.broadcast_to(m + jnp.log2(den), (HEAD_DIM, A_UNIT)))
        o_ref[0, res, pl.ds(off, A_UNIT), :] = jnp.concatenate(outs, axis=0).T
        lse_ref[0, res, pl.ds(off, A_UNIT), :] = jnp.concatenate(lses, axis=0).T
        return carry

    small = shift_ref[1] > 0.5

    @pl.when(small)
    def _():
        lax.fori_loop(0, n_units, functools.partial(unit, True), 0, unroll=min(4, n_units))

    @pl.when(jnp.logical_not(small))
    def _():
        lax.fori_loop(0, n_units, functools.partial(unit, False), 0)


def _mixer_a_group(za_g, bias, shift, rate, B, S):
    L = S // rate
    tl = min(A_TILE, L)
    n_res = min(rate, A_TILE // tl)
    nblk = L // A_RADIUS
    per = tl // A_RADIUS

    def own(section):
        return lambda b, r, l: (b, r, l, section)

    def prev(section):
        return lambda b, r, l: (b, r, jnp.maximum(l * per - 1, 0), section)

    def nxt(section):
        return lambda b, r, l: (b, r, jnp.minimum((l + 1) * per, nblk - 1), section)

    edge = (1, n_res, A_RADIUS, A_OUT)
    full = (1, n_res, tl, A_OUT)
    return pl.pallas_call(
        functools.partial(_mixer_a_kernel, tl=tl, seq_len=L),
        grid=(B, rate // n_res, L // tl),
        in_specs=[
            pl.BlockSpec(full, own(0)),
            pl.BlockSpec(edge, prev(1)), pl.BlockSpec(full, own(1)), pl.BlockSpec(edge, nxt(1)),
            pl.BlockSpec(edge, prev(2)), pl.BlockSpec(full, own(2)), pl.BlockSpec(edge, nxt(2)),
            _const_spec((A_HEADS, A_SPAN, A_UNIT)),
            pl.BlockSpec(memory_space=pltpu.SMEM),
        ],
        out_specs=[pl.BlockSpec(full, own(0)), pl.BlockSpec(full, own(0))],
        out_shape=[jax.ShapeDtypeStruct((B, rate, L, A_OUT), jnp.float32)] * 2,
        scratch_shapes=[pltpu.VMEM((n_res, tl + 2 * A_RADIUS, A_OUT), _MXU_DTYPE)] * 2
        + [pltpu.VMEM((A_SPAN, 2 * A_UNIT), jnp.float32)],
        compiler_params=_params(("arbitrary",) * 3),
        name=f"mixer_a_rate{rate}",
    )(za_g, za_g, za_g, za_g, za_g, za_g, za_g, bias, shift)


B_UNIT_ROWS = 4
B_UNIT = B_UNIT_ROWS * GRID_W
B_SPAN_ROWS = B_UNIT_ROWS + B_WIN_ROWS
B_SPAN = B_SPAN_ROWS * GRID_W
B_HALO = (B_WIN_ROWS // 2) * GRID_W
B_TILE_ROWS = 32
B_TILE = B_TILE_ROWS * GRID_W


def _mixer_b_kernel(q_ref, kp_ref, ko_ref, kn_ref, vp_ref, vo_ref, vn_ref, blocks_ref, sel_ref, shift_ref,
                    o_ref, kw_ref, vw_ref, bias_ref, s_ref, *, rows):
    @pl.when((pl.program_id(0) == 0) & (pl.program_id(1) == 0))
    def _():
        left = lax.broadcasted_iota(jnp.int32, (GRID_W, LANES), 1) < GRID_W

        def fill(n, carry):
            v, h = n // B_HEADS, n % B_HEADS
            for a in range(B_SPAN_ROWS):
                for gp in range(B_UNIT_ROWS // 2):
                    at = (v * B_SPAN_ROWS + a) * B_UNIT_ROWS + 2 * gp
                    tile = jnp.where(left, blocks_ref[h, sel_ref[at]], blocks_ref[h, sel_ref[at + 1]])
                    bias_ref[v, h, a * GRID_W:(a + 1) * GRID_W, gp * LANES:(gp + 1) * LANES] = tile
            return carry

        lax.fori_loop(0, 3 * B_HEADS, fill, 0)

    i0 = pl.program_id(1) * B_TILE_ROWS
    kw_ref[0:B_HALO] = kp_ref[0]
    kw_ref[B_HALO:B_HALO + B_TILE] = ko_ref[0]
    kw_ref[B_HALO + B_TILE:] = kn_ref[0]
    vw_ref[:, :, 0:B_HALO] = vp_ref[0]
    vw_ref[:, :, B_HALO:B_HALO + B_TILE] = vo_ref[0]
    vw_ref[:, :, B_HALO + B_TILE:] = vn_ref[0]
    tail = _ones_tail(B_SPAN, _MXU_DTYPE)
    upper = lax.broadcasted_iota(jnp.int32, (LANES, B_UNIT), 0) < HEAD_DIM

    n_units = B_TILE_ROWS // B_UNIT_ROWS

    def span_offset(u):
        r0 = jnp.clip(i0 + u * B_UNIT_ROWS - B_WIN_ROWS // 2, 0, rows - B_SPAN_ROWS)
        return pl.multiple_of((r0 - (i0 - B_WIN_ROWS // 2)) * GRID_W, LANES)

    def scores(u, pair):
        qt = q_ref[0, pair, :, pl.ds(pl.multiple_of(u * B_UNIT, B_UNIT), B_UNIT)]
        zero = jnp.zeros_like(qt)
        qt2 = jnp.concatenate([jnp.where(upper, qt, zero), jnp.where(upper, zero, qt)], axis=1)
        return _dot(kw_ref[pl.ds(span_offset(u), B_SPAN), pair * LANES:(pair + 1) * LANES], qt2)

    s_ref[...] = scores(0, 0)

    def unit(fixed_shift, u, carry):
        i0u = i0 + u * B_UNIT_ROWS
        off = span_offset(u)
        variant = jnp.where(i0u == 0, 0, jnp.where(i0u == rows - B_UNIT_ROWS, 2, 1))
        qoff = pl.multiple_of(u * B_UNIT, B_UNIT)
        outs = []
        s_next = s_ref[...]
        for pair in range(B_PAIRS):
            s_pair = s_next
            if pair + 1 < B_PAIRS:
                s_next = scores(u, pair + 1)
            else:
                s_ref[...] = scores(jnp.minimum(u + 1, n_units - 1), 0)
            for odd in range(2):
                s = s_pair[:, odd * B_UNIT:(odd + 1) * B_UNIT] + bias_ref[variant, 2 * pair + odd]
                if not fixed_shift:
                    s = s - jnp.max(s, axis=0, keepdims=True)
                p = jnp.exp2(s).astype(_MXU_DTYPE)
                vt = vw_ref[pair, odd * HEAD_DIM:(odd + 1) * HEAD_DIM, pl.ds(off, B_SPAN)]
                acc = _dot(jnp.concatenate([vt, tail], axis=0), p)
                outs.append(acc[:HEAD_DIM] / acc[HEAD_DIM:HEAD_DIM + 1])
        o_ref[0, pl.ds(qoff, B_UNIT), :] = jnp.concatenate(outs, axis=0).T.astype(o_ref.dtype)
        return carry

    small = shift_ref[1] > 0.5

    @pl.when(small)
    def _():
        lax.fori_loop(0, n_units, functools.partial(unit, True), 0, unroll=n_units)

    @pl.when(jnp.logical_not(small))
    def _():
        lax.fori_loop(0, n_units, functools.partial(unit, False), 0)


def _mixer_b(qbt, kb, vbt, blocks, shift, B, S):
    rows = S // GRID_W
    sel = jnp.asarray(_mixer_b_block_index(rows))
    nt = rows // B_TILE_ROWS
    per = B_TILE // B_HALO
    nh = S // B_HALO
    kb3 = kb.reshape(B, S, B_W)
    prev = lambda t: jnp.maximum(t * per - 1, 0)
    nxt = lambda t: jnp.minimum((t + 1) * per, nh - 1)
    k_edge, k_own = (1, B_HALO, B_W), (1, B_TILE, B_W)
    t_edge, t_own = (1, B_PAIRS, LANES, B_HALO), (1, B_PAIRS, LANES, B_TILE)
    o = pl.pallas_call(
        functools.partial(_mixer_b_kernel, rows=rows),
        grid=(B, nt),
        in_specs=[
            pl.BlockSpec(t_own, lambda b, t: (b, 0, 0, t)),
            pl.BlockSpec(k_edge, lambda b, t: (b, prev(t), 0)),
            pl.BlockSpec(k_own, lambda b, t: (b, t, 0)),
            pl.BlockSpec(k_edge, lambda b, t: (b, nxt(t), 0)),
            pl.BlockSpec(t_edge, lambda b, t: (b, 0, 0, prev(t))),
            pl.BlockSpec(t_own, lambda b, t: (b, 0, 0, t)),
            pl.BlockSpec(t_edge, lambda b, t: (b, 0, 0, nxt(t))),
            _const_spec(blocks.shape),
            pl.BlockSpec(memory_space=pltpu.SMEM),
            pl.BlockSpec(memory_space=pltpu.SMEM),
        ],
        out_specs=pl.BlockSpec(k_own, lambda b, t: (b, t, 0)),
        out_shape=jax.ShapeDtypeStruct((B, S, B_W), _MXU_DTYPE),
        scratch_shapes=[pltpu.VMEM((B_TILE + 2 * B_HALO, B_W), _MXU_DTYPE),
                        pltpu.VMEM((B_PAIRS, LANES, B_TILE + 2 * B_HALO), _MXU_DTYPE),
                        pltpu.VMEM((3, B_HEADS, B_SPAN, B_UNIT), jnp.float32),
                        pltpu.VMEM((B_SPAN, 2 * B_UNIT), jnp.float32)],
        compiler_params=_params(("arbitrary",) * 2),
        name="mixer_b",
    )(qbt, kb3, kb3, kb3, vbt, vbt, vbt, blocks, sel, shift)
    return o.reshape(B * S, B_W)


C_TQ = 4096
C_TK = 512
C_UNIT = 512
C_UNROLL = 1


def _mixer_c_kernel(q_ref, k_ref, vt_ref, shift_ref, o_ref, qcat_ref, m_ref, acc_ref, s_ref, *, n_kv):
    acc_ref[...] = jnp.zeros(acc_ref.shape, jnp.float32)
    for g in range(C_GROUP):
        qcat_ref[:, g * C_TQ:(g + 1) * C_TQ] = q_ref[0, g]
    n_units = C_GROUP * C_TQ // C_UNIT

    def keys(j):
        return k_ref[0, 0, pl.ds(pl.multiple_of(j * C_TK, C_TK), C_TK), :]

    def scores(k, u):
        return _dot(k, qcat_ref[:, u * C_UNIT:(u + 1) * C_UNIT])

    def sweep(update, unroll):
        s_ref[...] = scores(keys(0), 0)

        def step(j, carry):
            k = keys(j)
            vt = vt_ref[0, 0, :, pl.ds(pl.multiple_of(j * C_TK, C_TK), C_TK)]
            s_next = s_ref[...]
            for u in range(n_units):
                s = s_next
                if u + 1 < n_units:
                    s_next = scores(k, u + 1)
                else:
                    s_ref[...] = scores(keys(jnp.minimum(j + 1, n_kv - 1)), 0)
                update(s, vt, slice(u * C_UNIT, (u + 1) * C_UNIT))
            return carry

        lax.fori_loop(0, n_kv, step, 0, unroll=unroll)

    def fixed_shift(s, vt, cols):
        acc_ref[:, cols] += _dot(vt, jnp.exp2(s - shift_ref[0]).astype(_MXU_DTYPE))

    def running_max(s, vt, cols):
        m_prev = m_ref[:, cols]
        m_new = jnp.maximum(m_prev, jnp.max(s, axis=0, keepdims=True))
        alpha = jnp.exp2(m_prev - m_new)
        p = jnp.exp2(s - m_new).astype(_MXU_DTYPE)
        acc_ref[:, cols] = alpha * acc_ref[:, cols] + _dot(vt, p)
        m_ref[:, cols] = m_new

    small = shift_ref[1] > 0.5

    @pl.when(small)
    def _():
        sweep(fixed_shift, C_UNROLL)

    @pl.when(jnp.logical_not(small))
    def _():
        m_ref[...] = jnp.full(m_ref.shape, -jnp.inf, jnp.float32)
        sweep(running_max, 1)

    o_t = jnp.concatenate(
        [acc_ref[:HEAD_DIM, g * C_TQ:(g + 1) * C_TQ] / acc_ref[HEAD_DIM:HEAD_DIM + 1, g * C_TQ:(g + 1) * C_TQ]
         for g in range(C_GROUP)], axis=0)
    o_ref[0] = o_t.T.astype(o_ref.dtype)


def _mixer_c(qc_t, kc, vc_t, shift, B, S):
    o = pl.pallas_call(
        functools.partial(_mixer_c_kernel, n_kv=S // C_TK),
        grid=(B, C_KV_HEADS, S // C_TQ),
        in_specs=[
            pl.BlockSpec((1, C_GROUP, HEAD_DIM, C_TQ), lambda b, kv, i: (b, kv, 0, i)),
            pl.BlockSpec((1, 1, S, HEAD_DIM), lambda b, kv, i: (b, kv, 0, 0)),
            pl.BlockSpec((1, 1, VT_ROWS, S), lambda b, kv, i: (b, kv, 0, 0)),
            pl.BlockSpec(memory_space=pltpu.SMEM),
        ],
        out_specs=pl.BlockSpec((1, C_TQ, C_GROUP * HEAD_DIM), lambda b, kv, i: (b, i, kv)),
        out_shape=jax.ShapeDtypeStruct((B, S, C_QW), _MXU_DTYPE),
        scratch_shapes=[pltpu.VMEM((HEAD_DIM, C_GROUP * C_TQ), _MXU_DTYPE),
                        pltpu.VMEM((1, C_GROUP * C_TQ), jnp.float32),
                        pltpu.VMEM((VT_ROWS, C_GROUP * C_TQ), jnp.float32),
                        pltpu.VMEM((C_TK, C_UNIT), jnp.float32)],
        compiler_params=_params(("arbitrary",) * 3),
        name="mixer_c",
    )(qc_t, kc, vc_t, shift)
    return o.reshape(B * S, C_QW)


MERGE_TM = 512
MERGE_GATE_BLOCK = 512


def _merge_kernel(x_ref, g_ref, oa0_ref, oa1_ref, oa2_ref, l0_ref, l1_ref, l2_ref, ob_ref, oc_ref,
                  wg0_ref, wg1_ref, wg2_ref, wg3_ref, wg4_ref, wg5_ref, pa_ref, pb_ref, pc_ref, wo_ref,
                  out_ref, *scratch):
    tm = x_ref.shape[0]
    gate_refs = (wg0_ref, wg1_ref, wg2_ref, wg3_ref, wg4_ref, wg5_ref)
    cast = lambda ref: ref[...].astype(_MXU_DTYPE)

    def token_major(ref, scr):
        rate = ref.shape[1]
        halves = range(A_OUT // LANES)
        for r in range(rate):
            for j in halves:
                scr[j, pl.ds(r, tm // rate, stride=rate), :] = ref[0, r, :, j * LANES:(j + 1) * LANES]
        return jnp.concatenate([scr[j] for j in halves], axis=1)

    x = x_ref[...]
    h = _rms(x, g_ref[...]).astype(_MXU_DTYPE)
    n_parts = D_MODEL // MERGE_GATE_BLOCK
    branches = [None, _dot(ob_ref[...], cast(pb_ref)), _dot(oc_ref[...], cast(pc_ref))]
    gates = {(b, part): jax.nn.sigmoid(_dot(h, cast(gate_refs[b * n_parts + part])))
             for b in (1, 2, 0) for part in range(n_parts)}
    oa0, l0 = oa0_ref[...], l0_ref[...]
    oa1, l1 = token_major(oa1_ref, scratch[0]), token_major(l1_ref, scratch[1])
    oa2, l2 = token_major(oa2_ref, scratch[2]), token_major(l2_ref, scratch[3])
    mx = jnp.maximum(jnp.maximum(l0, l1), l2)
    w0, w1, w2 = jnp.exp2(l0 - mx), jnp.exp2(l1 - mx), jnp.exp2(l2 - mx)
    o_a = (w0 * oa0 + w1 * oa1 + w2 * oa2) / (w0 + w1 + w2)
    branches[0] = _dot(o_a.astype(_MXU_DTYPE), cast(pa_ref))
    parts = []
    for part in range(n_parts):
        cols = slice(part * MERGE_GATE_BLOCK, (part + 1) * MERGE_GATE_BLOCK)
        merged = sum(gates[b, part] * branches[b][:, cols] for b in range(1, N_BRANCH))
        parts.append((merged + gates[0, part] * branches[0][:, cols]).astype(_MXU_DTYPE))
    out_ref[...] = x + _dot(jnp.concatenate(parts, axis=1), cast(wo_ref))


def _merge(x2, g1, oa, lse, ob, oc, w_in, w_br_a, w_br_b, w_br_c, w_o, layer, B, S):
    M = x2.shape[0]
    tm = MERGE_TM
    per_seq = S // tm
    row = lambda i: (i, 0)
    tile = lambda w: pl.BlockSpec((tm, w), row)
    gate0 = _ZG // MERGE_GATE_BLOCK

    def layer_weight(w, block=None, at=0):
        block = w.shape[2] if block is None else block
        return pl.BlockSpec((None, w.shape[1], block), lambda i: (layer, 0, at), pipeline_mode=pl.Buffered(1))

    def dilated(rate):
        return pl.BlockSpec((1, rate, tm // rate, A_OUT), lambda i: (i // per_seq, 0, i % per_seq, 0))

    r1, r2 = A_PATTERNS[1][1], A_PATTERNS[2][1]
    return pl.pallas_call(
        _merge_kernel,
        grid=(M // tm,),
        in_specs=[tile(D_MODEL), _const_spec((1, D_MODEL)),
                  tile(A_OUT), dilated(r1), dilated(r2), tile(A_OUT), dilated(r1), dilated(r2),
                  tile(B_W), tile(C_QW),
                  *[layer_weight(w_in, MERGE_GATE_BLOCK, gate0 + j)
                    for j in range(N_BRANCH * D_MODEL // MERGE_GATE_BLOCK)],
                  layer_weight(w_br_a), layer_weight(w_br_b), layer_weight(w_br_c), layer_weight(w_o)],
        out_specs=tile(D_MODEL),
        out_shape=jax.ShapeDtypeStruct((M, D_MODEL), jnp.float32),
        scratch_shapes=[pltpu.VMEM((A_OUT // LANES, tm, LANES), jnp.float32)] * 4,
        compiler_params=_params(("arbitrary",)),
        name="gated_merge",
    )(x2, g1, oa[0].reshape(M, A_OUT), oa[1], oa[2], lse[0].reshape(M, A_OUT), lse[1], lse[2],
      ob, oc, *([w_in] * (N_BRANCH * D_MODEL // MERGE_GATE_BLOCK)), w_br_a, w_br_b, w_br_c, w_o)


FFN_TM = 1024
FFN_CHUNK = 256


def _ffn_kernel(x_ref, g_ref, wup_ref, wdown_ref, out_ref, act_ref):
    x = x_ref[...]
    xn = _rms(x, g_ref[...]).astype(_MXU_DTYPE)
    for c in range(D_FF // FFN_CHUNK):
        cs = slice(c * FFN_CHUNK, (c + 1) * FFN_CHUNK)
        a = _dot(xn, wup_ref[:, cs].astype(_MXU_DTYPE))
        b = _dot(xn, wup_ref[:, D_FF + c * FFN_CHUNK:D_FF + (c + 1) * FFN_CHUNK].astype(_MXU_DTYPE))
        act_ref[:, cs] = (a * jax.nn.sigmoid(a) * b).astype(act_ref.dtype)
    out_ref[...] = x + _dot(act_ref[...], wdown_ref[...])


def _ffn(x2, g2, w_up, w_down, layer):
    M = x2.shape[0]
    tm = FFN_TM
    row = lambda i: (i, 0)
    layer_weight = lambda w: pl.BlockSpec((None,) + w.shape[1:], lambda i: (layer, 0, 0),
                                          pipeline_mode=pl.Buffered(1))
    return pl.pallas_call(
        _ffn_kernel,
        grid=(M // tm,),
        in_specs=[pl.BlockSpec((tm, D_MODEL), row), _const_spec((1, D_MODEL)),
                  layer_weight(w_up), layer_weight(w_down)],
        out_specs=pl.BlockSpec((tm, D_MODEL), row),
        out_shape=jax.ShapeDtypeStruct((M, D_MODEL), jnp.float32),
        scratch_shapes=[pltpu.VMEM((tm, D_FF), _MXU_DTYPE)],
        compiler_params=_params(("arbitrary",)),
        name="swiglu_ffn",
    )(x2, g2, w_up, w_down)


def _t5_bucket(rel):
    half = T5_BUCKETS // 2
    max_exact = half // 2
    ret = jnp.where(rel > 0, half, 0)
    n = jnp.abs(rel)
    nf = jnp.maximum(n, 1).astype(jnp.float32)
    large = max_exact + (jnp.log(nf / max_exact) / math.log(T5_MAX_DIST / max_exact)
                         * (half - max_exact)).astype(jnp.int32)
    large = jnp.minimum(large, half - 1)
    return ret + jnp.where(n < max_exact, n, large)


def _mixer_a_bias(table_g, rate):
    j = jnp.arange(A_SPAN)[:, None]
    i = jnp.arange(A_UNIT)[None, :]
    step = j - A_RADIUS - i
    onehot = (_t5_bucket(step * rate)[:, :, None] == jnp.arange(T5_BUCKETS)).astype(jnp.float32)
    bias = jnp.einsum("jib,bh->hji", onehot, table_g.astype(jnp.float32) * LOG2_E,
                      precision=lax.Precision.HIGHEST)
    return jnp.where((jnp.abs(step) <= A_RADIUS)[None], bias, NEG)


B_N_DR = 2 * B_WIN_ROWS - 1


def _mixer_b_blocks(rpb, shift):
    c = np.arange(GRID_W)
    c0 = np.clip(c - B_WIN_COLS // 2, 0, GRID_W - B_WIN_COLS)
    col_ok = (c[:, None] >= c0[None, :]) & (c[:, None] < c0[None, :] + B_WIN_COLS)
    dc = np.clip(c[:, None] - c[None, :] + B_WIN_COLS - 1, 0, 2 * B_WIN_COLS - 2)
    pick_c = ((dc[..., None] == np.arange(2 * B_WIN_COLS - 1)) & col_ok[..., None]).astype(np.float32)
    blocks = jnp.einsum("hdk,xyk->hdxy", rpb.astype(jnp.float32) * LOG2_E, pick_c,
                        precision=lax.Precision.HIGHEST)
    blocks = jnp.where(col_ok[None, None], blocks - shift, NEG)
    blocks = jnp.concatenate([blocks, jnp.full((B_HEADS, 1, GRID_W, GRID_W), NEG, jnp.float32)], axis=1)
    return jnp.concatenate([blocks, blocks], axis=-1)


def _mixer_b_block_index(rows):
    first_query_row = np.array([0, B_WIN_ROWS // 2, rows - B_UNIT_ROWS])
    i = first_query_row[:, None, None] + np.arange(B_UNIT_ROWS)[None, None, :]
    r0 = np.clip(first_query_row - B_WIN_ROWS // 2, 0, rows - B_SPAN_ROWS)
    ik = r0[:, None, None] + np.arange(B_SPAN_ROWS)[None, :, None]
    rs = np.clip(i - B_WIN_ROWS // 2, 0, rows - B_WIN_ROWS)
    row_ok = (ik >= rs) & (ik < rs + B_WIN_ROWS)
    return np.where(row_ok, ik - i + B_WIN_ROWS - 1, B_N_DR).astype(np.int32).reshape(-1)


def _rope_tables(S):
    rows = S // GRID_W
    inv = ROPE_THETA ** (-jnp.arange(0, ROPE_AXIS_DIM, 2, dtype=jnp.float32) / ROPE_AXIS_DIM)
    d = np.arange(LANES) % HEAD_DIM
    inv_lane = inv[d % (ROPE_AXIS_DIM // 2)][None, :]
    is_col = (d >= ROPE_AXIS_DIM)[None, None, :]
    first = ((d % ROPE_AXIS_DIM) < ROPE_AXIS_DIM // 2)[None, :]
    tables = []
    for n in (rows, GRID_W):
        ang = jnp.arange(n, dtype=jnp.float32)[:, None] * inv_lane
        sin = jnp.sin(ang)
        tables.append((jnp.cos(ang), jnp.where(first, -sin, 0.0), jnp.where(first, 0.0, sin)))
    return tuple(jnp.where(is_col, by_col[None, :, :], by_row[:, None, :]).reshape(S, LANES)
                 for by_row, by_col in zip(*tables))


def _softmax_shift(gain_q, gain_k, bias_abs_max):
    bound = (HEAD_DIM * QK_SCALE * LOG2_E * ROUNDING_SLACK * jnp.max(jnp.abs(gain_q)) * jnp.max(jnp.abs(gain_k))
             + LOG2_E * bias_abs_max)
    small = bound <= SHIFT_LIMIT
    return jnp.stack([jnp.where(small, bound, 0.0), small.astype(jnp.float32)]).astype(jnp.float32)


def _block_diag_ones():
    i = np.arange(PROJ_TILE) // HEAD_DIM
    return jnp.asarray(i[:, None] == i[None, :], _MXU_DTYPE)


def kernel(x, rel_bias_table, norm1, w_in, qk_gain, nat_rpb, w_br_a, w_br_b, w_br_c, w_o,
           norm2, w_up, w_down):
    B, S, D = x.shape
    depth = w_in.shape[0]
    M = B * S
    x2 = x.reshape(M, D)
    rope = _rope_tables(S)
    ones_bd = _block_diag_ones()
    a_bias = [_mixer_a_bias(rel_bias_table[:, g * A_HEADS:(g + 1) * A_HEADS], rate)
              for g, (_, rate) in enumerate(A_PATTERNS)]
    w_down_c = w_down.astype(_MXU_DTYPE)
    for l in range(depth):
        gq = qk_gain[l]
        tile = lambda g, n, s: jnp.tile(g * s, n)
        gain_row = jnp.concatenate([
            tile(gq[0], A_W // HEAD_DIM, QK_SCALE * LOG2_E), tile(gq[1], A_W // HEAD_DIM, 1.0),
            tile(gq[2], B_HEADS, QK_SCALE * LOG2_E), tile(gq[3], B_HEADS, 1.0),
            tile(gq[4], C_Q_HEADS, QK_SCALE * LOG2_E), tile(gq[5], C_KV_HEADS, 1.0)])[None, :]
        g1 = norm1[l][None, :]
        za0, za1, za2, qbt, kb, vbt, qc, kc, vc = _qkv_projection(
            x2, g1, w_in, l, gain_row, ones_bd, rope, B, S)
        za =(za0.reshape(B, 1, S, ZA_W), za1, za2)
        oa, lse = [], []
        for g, (_, rate) in enumerate(A_PATTERNS):
            table_g = rel_bias_table[:, g * A_HEADS:(g + 1) * A_HEADS]
            shift_a = _softmax_shift(gq[0], gq[1], jnp.max(jnp.abs(table_g)))
            o_g, l_g = _mixer_a_group(za[g], a_bias[g] - shift_a[0], shift_a, rate, B, S)
            oa.append(o_g)
            lse.append(l_g)
        shift_b = _softmax_shift(gq[2], gq[3], jnp.max(jnp.abs(nat_rpb[l])))
        ob = _mixer_b(qbt, kb, vbt, _mixer_b_blocks(nat_rpb[l], shift_b[0]), shift_b, B, S)
        oc = _mixer_c(qc, kc, vc, _softmax_shift(gq[4], gq[5], 0.0), B, S)
        x2 = _merge(x2, g1, oa, lse, ob, oc, w_in, w_br_a, w_br_b, w_br_c, w_o, l, B, S)
        x2 = _ffn(x2, norm2[l][None, :], w_up, w_down_c, l)
    return x2.reshape(B, S, D)
```

```python
import functools
import math

import jax
import jax.numpy as jnp
import numpy as np
from jax import lax
from jax.experimental import pallas as pl
from jax.experimental.pallas import tpu as pltpu

_MXU_DTYPE = jnp.bfloat16

D_MODEL = 1024
HEAD_DIM = 64
GRID_W = 64
RMS_EPS = 1e-6
NEG = -1e30
A_PATTERNS = ((128, 1), (512, 4), (2048, 16))
A_GROUPS = 3
A_HEADS = 4
A_W = A_GROUPS * A_HEADS * HEAD_DIM
A_OUT = A_HEADS * HEAD_DIM
A_RADIUS = 64
B_HEADS = 8
B_W = B_HEADS * HEAD_DIM
B_WIN_ROWS = 8
B_WIN_COLS = 16
C_Q_HEADS = 8
C_KV_HEADS = 2
C_GROUP = C_Q_HEADS // C_KV_HEADS
C_QW = C_Q_HEADS * HEAD_DIM
C_KVW = C_KV_HEADS * HEAD_DIM
ROPE_THETA = 10000.0
ROPE_AXIS_DIM = HEAD_DIM // 2
T5_BUCKETS = 32
T5_MAX_DIST = 1024
N_BRANCH = 3
D_FF = math.ceil(8 * D_MODEL / 3 / 256) * 256
QK_SCALE = HEAD_DIM ** -0.5
LOG2_E = math.log2(math.e)

V7X_VMEM_BYTES = 64 * 1024 * 1024
VMEM_LIMIT_BYTES = V7X_VMEM_BYTES * 7 // 8
LANES = 128
BF16_SUBLANES = 16
MXU_COLS = 256
VT_ROWS = HEAD_DIM + BF16_SUBLANES
SHIFT_LIMIT = 60.0
ROUNDING_SLACK = 1.02

_OFF = np.cumsum([0, A_W, A_W, A_W, B_W, B_W, B_W, C_QW, C_KVW, C_KVW]).tolist()
(_QA, _KA, _VA, _QB, _KB, _VB, _QC, _KC, _VC, _ZG) = _OFF

ZA_W = 3 * A_OUT
B_PAIRS = B_HEADS // 2


def _params(sem):
    return pltpu.CompilerParams(dimension_semantics=sem, vmem_limit_bytes=VMEM_LIMIT_BYTES)


def _const_spec(shape):
    nd = len(shape)
    return pl.BlockSpec(shape, lambda *_: (0,) * nd, pipeline_mode=pl.Buffered(1))


def _rms(x, g):
    return x * lax.rsqrt(jnp.mean(x * x, axis=-1, keepdims=True) + RMS_EPS) * g


def _dot(a, b):
    return jnp.dot(a, b, preferred_element_type=jnp.float32)


def _dot_nt(a, b):
    return lax.dot_general(a, b, (((1,), (1,)), ((), ())), preferred_element_type=jnp.float32)


def _ones_tail(width, dtype):
    return (lax.broadcasted_iota(jnp.int32, (VT_ROWS - HEAD_DIM, width), 0) == 0).astype(dtype)


PROJ_TM = 1024
PROJ_TILE = MXU_COLS
PROJ_W_BLOCK = 1536
_N_NORM = 2 * A_W + 2 * B_W
_N_ROPE = C_QW + C_KVW


def _proj_kernel(x_ref, g_ref, w0_ref, w1_ref, w2_ref, gain_ref, ones_ref, cos_ref, s1_ref, s2_ref,
                 za0_ref, za1_ref, za2_ref, qbt_ref, kb_ref, vbt_ref, qc_ref, kc_ref, vc_ref, dil_ref):
    tm = x_ref.shape[0]
    w_refs = (w0_ref, w1_ref, w2_ref)
    h = _rms(x_ref[...], g_ref[...]).astype(_MXU_DTYPE)
    za_refs = (za0_ref, za1_ref, za2_ref)

    def head_norm(acc, c0, width):
        sq = (acc * acc).astype(_MXU_DTYPE)
        ms = _dot(sq, ones_ref[:width, :width]) * (1.0 / HEAD_DIM)
        return acc * lax.rsqrt(ms + RMS_EPS) * gain_ref[:, c0:c0 + width]

    def store_group(g, section, val):
        cols = slice(section * A_OUT, (section + 1) * A_OUT)
        rate = A_PATTERNS[g][1]
        if rate == 1:
            za0_ref[:, cols] = val.astype(za0_ref.dtype)
            return
        for j in range(A_OUT // LANES):
            dil_ref[j] = val[:, j * LANES:(j + 1) * LANES]
        for r in range(rate):
            picked = [dil_ref[j, pl.ds(r, tm // rate, stride=rate), :] for j in range(A_OUT // LANES)]
            za_refs[g][0, r, :, cols] = jnp.concatenate(picked, axis=1).astype(za_refs[g].dtype)

    def store_pairs(ref, t, val):
        vt = val.T.astype(ref.dtype)
        ref[0, 2 * t] = vt[:LANES]
        ref[0, 2 * t + 1] = vt[LANES:]

    def rotary(y, width):
        reps = width // LANES
        table = lambda ref: jnp.concatenate([ref[...]] * reps, axis=1) if reps > 1 else ref[...]
        return (y * table(cos_ref) + pltpu.roll(y, width - 16, 1) * table(s1_ref)
                + pltpu.roll(y, 16, 1) * table(s2_ref))

    def store_qc(t, acc, gain_at):
        yt = rotary(head_norm(acc, gain_at, PROJ_TILE), PROJ_TILE).T.astype(qc_ref.dtype)
        heads = PROJ_TILE // HEAD_DIM
        for j in range(heads):
            qc_ref[0, heads * t + j] = yt[j * HEAD_DIM:(j + 1) * HEAD_DIM]

    def store_kc_vc(acc, gain_at):
        y = rotary(head_norm(acc[:, :LANES], gain_at, LANES), LANES).astype(kc_ref.dtype)
        acc_t = acc[:, LANES:].T.astype(vc_ref.dtype)
        tail = _ones_tail(tm, vc_ref.dtype)
        for kv in range(C_KV_HEADS):
            kc_ref[0, kv] = y[:, kv * HEAD_DIM:(kv + 1) * HEAD_DIM]
            vc_ref[0, kv, :HEAD_DIM] = acc_t[kv * HEAD_DIM:(kv + 1) * HEAD_DIM]
            vc_ref[0, kv, HEAD_DIM:] = tail

    jobs = []
    for section, src in enumerate((_QA, _KA)):
        for g in range(A_GROUPS):
            gain_at = section * A_W + g * A_OUT
            jobs.append((src + g * A_OUT, A_OUT, lambda acc, gain_at=gain_at, g=g, section=section:
                         store_group(g, section, head_norm(acc, gain_at, A_OUT))))
    for t in range(B_W // PROJ_TILE):
        gain_at = 2 * A_W + t * PROJ_TILE
        jobs.append((_QB + t * PROJ_TILE, PROJ_TILE, lambda acc, gain_at=gain_at, t=t:
                     store_pairs(qbt_ref, t, head_norm(acc, gain_at, PROJ_TILE))))
    for t in range(B_W // PROJ_TILE):
        gain_at = 2 * A_W + B_W + t * PROJ_TILE
        def store_kb(acc, gain_at=gain_at, t=t):
            kb_ref[:, t * PROJ_TILE:(t + 1) * PROJ_TILE] = head_norm(acc, gain_at, PROJ_TILE).astype(kb_ref.dtype)
        jobs.append((_KB + t * PROJ_TILE, PROJ_TILE, store_kb))
    for t in range(C_QW // PROJ_TILE):
        jobs.append((_QC + t * PROJ_TILE, PROJ_TILE,
                     lambda acc, t=t: store_qc(t, acc, _N_NORM + t * PROJ_TILE)))
    jobs.append((_KC, 2 * C_KVW, lambda acc: store_kc_vc(acc, _N_NORM + C_QW)))
    for g in range(A_GROUPS):
        jobs.append((_VA + g * A_OUT, A_OUT, lambda acc, g=g: store_group(g, 2, acc)))
    for t in range(B_W // PROJ_TILE):
        jobs.append((_VB + t * PROJ_TILE, PROJ_TILE, lambda acc, t=t: store_pairs(vbt_ref, t, acc)))

    def product(n):
        src, width, _ = jobs[n]
        blk, col = divmod(src, PROJ_W_BLOCK)
        return _dot(h, w_refs[blk][:, col:col + width].astype(_MXU_DTYPE))

    acc_next = product(0)
    for n, (_, _, consume) in enumerate(jobs):
        acc = acc_next
        if n + 1 < len(jobs):
            acc_next = product(n + 1)
        consume(acc)


def _qkv_projection(x2, g1, w_in, layer, gain_row, ones_bd, rope, B, S):
    M = x2.shape[0]
    tm = PROJ_TM
    per_seq = S // tm
    w_block = lambda j: pl.BlockSpec((None, D_MODEL, PROJ_W_BLOCK), lambda i: (layer, 0, j),
                                     pipeline_mode=pl.Buffered(1))
    cos_t, s1_t, s2_t = rope
    r1, r2 = A_PATTERNS[1][1], A_PATTERNS[2][1]
    row = lambda i: (i, 0)
    pos = lambda i: (i % per_seq, 0)
    hm = lambda i: (i // per_seq, 0, i % per_seq, 0)
    hm_t = lambda i: (i // per_seq, 0, 0, i % per_seq)
    cd = _MXU_DTYPE
    return pl.pallas_call(
        _proj_kernel,
        grid=(M // tm,),
        in_specs=[
            pl.BlockSpec((tm, D_MODEL), row),
            _const_spec((1, D_MODEL)),
            w_block(0), w_block(1), w_block(2),
            _const_spec((1, _N_NORM + _N_ROPE)),
            _const_spec((PROJ_TILE, PROJ_TILE)),
            pl.BlockSpec((tm, LANES), pos),
            pl.BlockSpec((tm, LANES), pos),
            pl.BlockSpec((tm, LANES), pos),
        ],
        out_specs=[
            pl.BlockSpec((tm, ZA_W), row),
            pl.BlockSpec((1, r1, tm // r1, ZA_W), hm),
            pl.BlockSpec((1, r2, tm // r2, ZA_W), hm),
            pl.BlockSpec((1, B_PAIRS, LANES, tm), hm_t),
            pl.BlockSpec((tm, B_W), row),
            pl.BlockSpec((1, B_PAIRS, LANES, tm), hm_t),
            pl.BlockSpec((1, C_Q_HEADS, HEAD_DIM, tm), hm_t),
            pl.BlockSpec((1, C_KV_HEADS, tm, HEAD_DIM), hm),
            pl.BlockSpec((1, C_KV_HEADS, VT_ROWS, tm), hm_t),
        ],
        out_shape=[
            jax.ShapeDtypeStruct((M, ZA_W), cd),
            jax.ShapeDtypeStruct((B, r1, S // r1, ZA_W), cd),
            jax.ShapeDtypeStruct((B, r2, S // r2, ZA_W), cd),
            jax.ShapeDtypeStruct((B, B_PAIRS, LANES, S), cd),
            jax.ShapeDtypeStruct((M, B_W), cd),
            jax.ShapeDtypeStruct((B, B_PAIRS, LANES, S), cd),
            jax.ShapeDtypeStruct((B, C_Q_HEADS, HEAD_DIM, S), cd),
            jax.ShapeDtypeStruct((B, C_KV_HEADS, S, HEAD_DIM), cd),
            jax.ShapeDtypeStruct((B, C_KV_HEADS, VT_ROWS, S), cd),
        ],
        scratch_shapes=[pltpu.VMEM((A_OUT // LANES, tm, LANES), jnp.float32)],
        compiler_params=_params(("arbitrary",)),
        name="qkv_projection",
    )(x2, g1, w_in, w_in, w_in, gain_row, ones_bd, cos_t, s1_t, s2_t)


A_UNIT = 4 * A_RADIUS
A_SPAN = A_UNIT + 2 * A_RADIUS
A_TILE = 2048


def _mixer_a_kernel(q_ref, kp_ref, ko_ref, kn_ref, vp_ref, vo_ref, vn_ref, bias_ref, shift_ref,
                    o_ref, lse_ref, kw_ref, vw_ref, s_ref, *, tl, seq_len, layer, group):
    n_res = q_ref.shape[1]
    l0 = pl.program_id(2) * tl
    for res in range(n_res):
        kw_ref[res, 0:A_RADIUS] = kp_ref[0, res]
        kw_ref[res, A_RADIUS:A_RADIUS + tl] = ko_ref[0, res]
        kw_ref[res, A_RADIUS + tl:] = kn_ref[0, res]
        vw_ref[res, 0:A_RADIUS] = vp_ref[0, res]
        vw_ref[res, A_RADIUS:A_RADIUS + tl] = vo_ref[0, res]
        vw_ref[res, A_RADIUS + tl:] = vn_ref[0, res]
    even_q = lax.broadcasted_iota(jnp.int32, (A_UNIT, LANES), 1) < HEAD_DIM
    lane_k = lax.broadcasted_iota(jnp.int32, (A_SPAN, LANES), 1)

    per_res = tl // A_UNIT
    n_units = n_res * per_res

    def place(n):
        return n // per_res, pl.multiple_of((n % per_res) * A_UNIT, A_UNIT)

    def scores(n, pair):
        res, off = place(n)
        q = q_ref[0, res, pl.ds(off, A_UNIT), pair * LANES:(pair + 1) * LANES]
        zero = jnp.zeros_like(q)
        q2 = jnp.concatenate([jnp.where(even_q, q, zero), jnp.where(even_q, zero, q)], axis=0)
        return _dot_nt(kw_ref[res, pl.ds(off, A_SPAN), pair * LANES:(pair + 1) * LANES], q2)

    s_ref[...] = scores(0, 0)

    def unit(fixed_shift, u, carry):
        res, off = place(u)
        top_ok = l0 + off - A_RADIUS >= 0
        bot_ok = l0 + off + A_UNIT + A_RADIUS <= seq_len
        outs, lses = [], []
        s_next = s_ref[...]
        for pair in range(A_HEADS // 2):
            s_pair = s_next
            if pair + 1 < A_HEADS // 2:
                s_next = scores(u, pair + 1)
            else:
                s_ref[...] = scores(jnp.minimum(u + 1, n_units - 1), 0)
            v_slab = vw_ref[res, pl.ds(off, A_SPAN), pair * LANES:(pair + 1) * LANES]
            for odd in range(2):
                s = s_pair[:, odd * A_UNIT:(odd + 1) * A_UNIT] + bias_ref[2 * pair + odd]
                s = jnp.concatenate([jnp.where(top_ok, s[:A_RADIUS], NEG), s[A_RADIUS:A_SPAN - A_RADIUS],
                                     jnp.where(bot_ok, s[A_SPAN - A_RADIUS:], NEG)], axis=0)
                if fixed_shift:
                    m = shift_ref[layer, group, 0]
                    p = jnp.exp2(s).astype(_MXU_DTYPE)
                else:
                    m = jnp.max(s, axis=0, keepdims=True)
                    p = jnp.exp2(s - m).astype(_MXU_DTYPE)
                den_row = (1 - odd) * HEAD_DIM
                mine = (lane_k >= odd * HEAD_DIM) & (lane_k < (odd + 1) * HEAD_DIM)
                v_aug = jnp.where(mine, v_slab, (lane_k == den_row).astype(v_slab.dtype))
                acc = lax.dot_general(v_aug, p, (((0,), (0,)), ((), ())),
                                      preferred_element_type=jnp.float32)
                den = acc[den_row:den_row + 1]
                outs.append(acc[odd * HEAD_DIM:(odd + 1) * HEAD_DIM] / den)
                lses.append(jnp.broadcast_to(m + jnp.log2(den), (HEAD_DIM, A_UNIT)))
        o_ref[0, res, pl.ds(off, A_UNIT), :] = jnp.concatenate(outs, axis=0).T
        lse_ref[0, res, pl.ds(off, A_UNIT), :] = jnp.concatenate(lses, axis=0).T
        return carry

    small = shift_ref[layer, group, 1] > 0.5

    @pl.when(small)
    def _():
        lax.fori_loop(0, n_units, functools.partial(unit, True), 0, unroll=min(4, n_units))

    @pl.when(jnp.logical_not(small))
    def _():
        lax.fori_loop(0, n_units, functools.partial(unit, False), 0)


def _mixer_a_group(za_g, bias, shift, layer, group, rate, B, S):
    L = S // rate
    tl = min(A_TILE, L)
    n_res = min(rate, A_TILE // tl)
    nblk = L // A_RADIUS
    per = tl // A_RADIUS

    def own(section):
        return lambda b, r, l: (b, r, l, section)

    def prev(section):
        return lambda b, r, l: (b, r, jnp.maximum(l * per - 1, 0), section)

    def nxt(section):
        return lambda b, r, l: (b, r, jnp.minimum((l + 1) * per, nblk - 1), section)

    edge = (1, n_res, A_RADIUS, A_OUT)
    full = (1, n_res, tl, A_OUT)
    return pl.pallas_call(
        functools.partial(_mixer_a_kernel, tl=tl, seq_len=L, layer=layer, group=group),
        grid=(B, rate // n_res, L // tl),
        in_specs=[
            pl.BlockSpec(full, own(0)),
            pl.BlockSpec(edge, prev(1)), pl.BlockSpec(full, own(1)), pl.BlockSpec(edge, nxt(1)),
            pl.BlockSpec(edge, prev(2)), pl.BlockSpec(full, own(2)), pl.BlockSpec(edge, nxt(2)),
            pl.BlockSpec((None, None, A_HEADS, A_SPAN, A_UNIT), lambda b, r, l: (layer, group, 0, 0, 0),
                         pipeline_mode=pl.Buffered(1)),
            pl.BlockSpec(memory_space=pltpu.SMEM),
        ],
        out_specs=[pl.BlockSpec(full, own(0)), pl.BlockSpec(full, own(0))],
        out_shape=[jax.ShapeDtypeStruct((B, rate, L, A_OUT), jnp.float32)] * 2,
        scratch_shapes=[pltpu.VMEM((n_res, tl + 2 * A_RADIUS, A_OUT), _MXU_DTYPE)] * 2
        + [pltpu.VMEM((A_SPAN, 2 * A_UNIT), jnp.float32)],
        compiler_params=_params(("arbitrary",) * 3),
        name=f"mixer_a_rate{rate}",
    )(za_g, za_g, za_g, za_g, za_g, za_g, za_g, bias, shift)


B_UNIT_ROWS = 4
B_UNIT = B_UNIT_ROWS * GRID_W
B_SPAN_ROWS = B_UNIT_ROWS + B_WIN_ROWS
B_SPAN = B_SPAN_ROWS * GRID_W
B_HALO = (B_WIN_ROWS // 2) * GRID_W
B_TILE_ROWS = 32
B_TILE = B_TILE_ROWS * GRID_W


def _mixer_b_kernel(q_ref, kp_ref, ko_ref, kn_ref, vp_ref, vo_ref, vn_ref, blocks_ref, sel_ref, shift_ref,
                    o_ref, kw_ref, vw_ref, bias_ref, s_ref, *, rows):
    @pl.when((pl.program_id(0) == 0) & (pl.program_id(1) == 0))
    def _():
        left = lax.broadcasted_iota(jnp.int32, (GRID_W, LANES), 1) < GRID_W

        def fill(n, carry):
            v, h = n // B_HEADS, n % B_HEADS
            for a in range(B_SPAN_ROWS):
                for gp in range(B_UNIT_ROWS // 2):
                    at = (v * B_SPAN_ROWS + a) * B_UNIT_ROWS + 2 * gp
                    tile = jnp.where(left, blocks_ref[h, sel_ref[at]], blocks_ref[h, sel_ref[at + 1]])
                    bias_ref[v, h, a * GRID_W:(a + 1) * GRID_W, gp * LANES:(gp + 1) * LANES] = tile
            return carry

        lax.fori_loop(0, 3 * B_HEADS, fill, 0)

    i0 = pl.program_id(1) * B_TILE_ROWS
    kw_ref[0:B_HALO] = kp_ref[0]
    kw_ref[B_HALO:B_HALO + B_TILE] = ko_ref[0]
    kw_ref[B_HALO + B_TILE:] = kn_ref[0]
    vw_ref[:, :, 0:B_HALO] = vp_ref[0]
    vw_ref[:, :, B_HALO:B_HALO + B_TILE] = vo_ref[0]
    vw_ref[:, :, B_HALO + B_TILE:] = vn_ref[0]
    tail = _ones_tail(B_SPAN, _MXU_DTYPE)
    upper = lax.broadcasted_iota(jnp.int32, (LANES, B_UNIT), 0) < HEAD_DIM

    n_units = B_TILE_ROWS // B_UNIT_ROWS

    def span_offset(u):
        r0 = jnp.clip(i0 + u * B_UNIT_ROWS - B_WIN_ROWS // 2, 0, rows - B_SPAN_ROWS)
        return pl.multiple_of((r0 - (i0 - B_WIN_ROWS // 2)) * GRID_W, LANES)

    def scores(u, pair):
        qt = q_ref[0, pair, :, pl.ds(pl.multiple_of(u * B_UNIT, B_UNIT), B_UNIT)]
        zero = jnp.zeros_like(qt)
        qt2 = jnp.concatenate([jnp.where(upper, qt, zero), jnp.where(upper, zero, qt)], axis=1)
        return _dot(kw_ref[pl.ds(span_offset(u), B_SPAN), pair * LANES:(pair + 1) * LANES], qt2)

    s_ref[...] = scores(0, 0)

    def unit(fixed_shift, u, carry):
        i0u = i0 + u * B_UNIT_ROWS
        off = span_offset(u)
        variant = jnp.where(i0u == 0, 0, jnp.where(i0u == rows - B_UNIT_ROWS, 2, 1))
        qoff = pl.multiple_of(u * B_UNIT, B_UNIT)
        outs = []
        s_next = s_ref[...]
        for pair in range(B_PAIRS):
            s_pair = s_next
            if pair + 1 < B_PAIRS:
                s_next = scores(u, pair + 1)
            else:
                s_ref[...] = scores(jnp.minimum(u + 1, n_units - 1), 0)
            for odd in range(2):
                s = s_pair[:, odd * B_UNIT:(odd + 1) * B_UNIT] + bias_ref[variant, 2 * pair + odd]
                if not fixed_shift:
                    s = s - jnp.max(s, axis=0, keepdims=True)
                p = jnp.exp2(s).astype(_MXU_DTYPE)
                vt = vw_ref[pair, odd * HEAD_DIM:(odd + 1) * HEAD_DIM, pl.ds(off, B_SPAN)]
                acc = _dot(jnp.concatenate([vt, tail], axis=0), p)
                outs.append(acc[:HEAD_DIM] / acc[HEAD_DIM:HEAD_DIM + 1])
        o_ref[0, pl.ds(qoff, B_UNIT), :] = jnp.concatenate(outs, axis=0).T.astype(o_ref.dtype)
        return carry

    small = shift_ref[1] > 0.5

    @pl.when(small)
    def _():
        lax.fori_loop(0, n_units, functools.partial(unit, True), 0, unroll=n_units)

    @pl.when(jnp.logical_not(small))
    def _():
        lax.fori_loop(0, n_units, functools.partial(unit, False), 0)


def _mixer_b(qbt, kb, vbt, blocks, shift, B, S):
    rows = S // GRID_W
    sel = jnp.asarray(_mixer_b_block_index(rows))
    nt = rows // B_TILE_ROWS
    per = B_TILE // B_HALO
    nh = S // B_HALO
    kb3 = kb.reshape(B, S, B_W)
    prev = lambda t: jnp.maximum(t * per - 1, 0)
    nxt = lambda t: jnp.minimum((t + 1) * per, nh - 1)
    k_edge, k_own = (1, B_HALO, B_W), (1, B_TILE, B_W)
    t_edge, t_own = (1, B_PAIRS, LANES, B_HALO), (1, B_PAIRS, LANES, B_TILE)
    o = pl.pallas_call(
        functools.partial(_mixer_b_kernel, rows=rows),
        grid=(B, nt),
        in_specs=[
            pl.BlockSpec(t_own, lambda b, t: (b, 0, 0, t)),
            pl.BlockSpec(k_edge, lambda b, t: (b, prev(t), 0)),
            pl.BlockSpec(k_own, lambda b, t: (b, t, 0)),
            pl.BlockSpec(k_edge, lambda b, t: (b, nxt(t), 0)),
            pl.BlockSpec(t_edge, lambda b, t: (b, 0, 0, prev(t))),
            pl.BlockSpec(t_own, lambda b, t: (b, 0, 0, t)),
            pl.BlockSpec(t_edge, lambda b, t: (b, 0, 0, nxt(t))),
            _const_spec(blocks.shape),
            pl.BlockSpec(memory_space=pltpu.SMEM),
            pl.BlockSpec(memory_space=pltpu.SMEM),
        ],
        out_specs=pl.BlockSpec(k_own, lambda b, t: (b, t, 0)),
        out_shape=jax.ShapeDtypeStruct((B, S, B_W), _MXU_DTYPE),
        scratch_shapes=[pltpu.VMEM((B_TILE + 2 * B_HALO, B_W), _MXU_DTYPE),
                        pltpu.VMEM((B_PAIRS, LANES, B_TILE + 2 * B_HALO), _MXU_DTYPE),
                        pltpu.VMEM((3, B_HEADS, B_SPAN, B_UNIT), jnp.float32),
                        pltpu.VMEM((B_SPAN, 2 * B_UNIT), jnp.float32)],
        compiler_params=_params(("arbitrary",) * 2),
        name="mixer_b",
    )(qbt, kb3, kb3, kb3, vbt, vbt, vbt, blocks, sel, shift)
    return o.reshape(B * S, B_W)


C_TQ = 4096
C_TK = 512
C_UNIT = 512
C_UNROLL = 1


def _mixer_c_kernel(q_ref, k_ref, vt_ref, shift_ref, o_ref, qcat_ref, m_ref, acc_ref, s_ref, *, n_kv):
    acc_ref[...] = jnp.zeros(acc_ref.shape, jnp.float32)
    for g in range(C_GROUP):
        qcat_ref[:, g * C_TQ:(g + 1) * C_TQ] = q_ref[0, g]
    n_units = C_GROUP * C_TQ // C_UNIT

    def keys(j):
        return k_ref[0, 0, pl.ds(pl.multiple_of(j * C_TK, C_TK), C_TK), :]

    def scores(k, u):
        return _dot(k, qcat_ref[:, u * C_UNIT:(u + 1) * C_UNIT])

    def sweep(update, unroll):
        s_ref[...] = scores(keys(0), 0)

        def step(j, carry):
            k = keys(j)
            vt = vt_ref[0, 0, :, pl.ds(pl.multiple_of(j * C_TK, C_TK), C_TK)]
            s_next = s_ref[...]
            for u in range(n_units):
                s = s_next
                if u + 1 < n_units:
                    s_next = scores(k, u + 1)
                else:
                    s_ref[...] = scores(keys(jnp.minimum(j + 1, n_kv - 1)), 0)
                update(s, vt, slice(u * C_UNIT, (u + 1) * C_UNIT))
            return carry

        lax.fori_loop(0, n_kv, step, 0, unroll=unroll)

    def fixed_shift(s, vt, cols):
        acc_ref[:, cols] += _dot(vt, jnp.exp2(s - shift_ref[0]).astype(_MXU_DTYPE))

    def running_max(s, vt, cols):
        m_prev = m_ref[:, cols]
        m_new = jnp.maximum(m_prev, jnp.max(s, axis=0, keepdims=True))
        alpha = jnp.exp2(m_prev - m_new)
        p = jnp.exp2(s - m_new).astype(_MXU_DTYPE)
        acc_ref[:, cols] = alpha * acc_ref[:, cols] + _dot(vt, p)
        m_ref[:, cols] = m_new

    small = shift_ref[1] > 0.5

    @pl.when(small)
    def _():
        sweep(fixed_shift, C_UNROLL)

    @pl.when(jnp.logical_not(small))
    def _():
        m_ref[...] = jnp.full(m_ref.shape, -jnp.inf, jnp.float32)
        sweep(running_max, 1)

    o_t = jnp.concatenate(
        [acc_ref[:HEAD_DIM, g * C_TQ:(g + 1) * C_TQ] / acc_ref[HEAD_DIM:HEAD_DIM + 1, g * C_TQ:(g + 1) * C_TQ]
         for g in range(C_GROUP)], axis=0)
    o_ref[0] = o_t.T.astype(o_ref.dtype)


def _mixer_c(qc_t, kc, vc_t, shift, B, S):
    o = pl.pallas_call(
        functools.partial(_mixer_c_kernel, n_kv=S // C_TK),
        grid=(B, C_KV_HEADS, S // C_TQ),
        in_specs=[
            pl.BlockSpec((1, C_GROUP, HEAD_DIM, C_TQ), lambda b, kv, i: (b, kv, 0, i)),
            pl.BlockSpec((1, 1, S, HEAD_DIM), lambda b, kv, i: (b, kv, 0, 0)),
            pl.BlockSpec((1, 1, VT_ROWS, S), lambda b, kv, i: (b, kv, 0, 0)),
            pl.BlockSpec(memory_space=pltpu.SMEM),
        ],
        out_specs=pl.BlockSpec((1, C_TQ, C_GROUP * HEAD_DIM), lambda b, kv, i: (b, i, kv)),
        out_shape=jax.ShapeDtypeStruct((B, S, C_QW), _MXU_DTYPE),
        scratch_shapes=[pltpu.VMEM((HEAD_DIM, C_GROUP * C_TQ), _MXU_DTYPE),
                        pltpu.VMEM((1, C_GROUP * C_TQ), jnp.float32),
                        pltpu.VMEM((VT_ROWS, C_GROUP * C_TQ), jnp.float32),
                        pltpu.VMEM((C_TK, C_UNIT), jnp.float32)],
        compiler_params=_params(("arbitrary",) * 3),
        name="mixer_c",
    )(qc_t, kc, vc_t, shift)
    return o.reshape(B * S, C_QW)


MERGE_TM = 512
MERGE_GATE_BLOCK = 512


def _merge_kernel(x_ref, g_ref, oa0_ref, oa1_ref, oa2_ref, l0_ref, l1_ref, l2_ref, ob_ref, oc_ref,
                  wg0_ref, wg1_ref, wg2_ref, wg3_ref, wg4_ref, wg5_ref, pa_ref, pb_ref, pc_ref, wo_ref,
                  out_ref, *scratch):
    tm = x_ref.shape[0]
    gate_refs = (wg0_ref, wg1_ref, wg2_ref, wg3_ref, wg4_ref, wg5_ref)
    cast = lambda ref: ref[...].astype(_MXU_DTYPE)

    def token_major(ref, scr):
        rate = ref.shape[1]
        halves = range(A_OUT // LANES)
        for r in range(rate):
            for j in halves:
                scr[j, pl.ds(r, tm // rate, stride=rate), :] = ref[0, r, :, j * LANES:(j + 1) * LANES]
        return jnp.concatenate([scr[j] for j in halves], axis=1)

    x = x_ref[...]
    h = _rms(x, g_ref[...]).astype(_MXU_DTYPE)
    n_parts = D_MODEL // MERGE_GATE_BLOCK
    branches = [None, _dot(ob_ref[...], cast(pb_ref)), _dot(oc_ref[...], cast(pc_ref))]
    gates = {(b, part): jax.nn.sigmoid(_dot(h, cast(gate_refs[b * n_parts + part])))
             for b in (1, 2, 0) for part in range(n_parts)}
    oa0, l0 = oa0_ref[...], l0_ref[...]
    oa1, l1 = token_major(oa1_ref, scratch[0]), token_major(l1_ref, scratch[1])
    oa2, l2 = token_major(oa2_ref, scratch[2]), token_major(l2_ref, scratch[3])
    mx = jnp.maximum(jnp.maximum(l0, l1), l2)
    w0, w1, w2 = jnp.exp2(l0 - mx), jnp.exp2(l1 - mx), jnp.exp2(l2 - mx)
    o_a = (w0 * oa0 + w1 * oa1 + w2 * oa2) / (w0 + w1 + w2)
    branches[0] = _dot(o_a.astype(_MXU_DTYPE), cast(pa_ref))
    parts = []
    for part in range(n_parts):
        cols = slice(part * MERGE_GATE_BLOCK, (part + 1) * MERGE_GATE_BLOCK)
        merged = sum(gates[b, part] * branches[b][:, cols] for b in range(1, N_BRANCH))
        parts.append((merged + gates[0, part] * branches[0][:, cols]).astype(_MXU_DTYPE))
    out_ref[...] = x + _dot(jnp.concatenate(parts, axis=1), cast(wo_ref))


def _merge(x2, g1, oa, lse, ob, oc, w_in, w_br_a, w_br_b, w_br_c, w_o, layer, B, S):
    M = x2.shape[0]
    tm = MERGE_TM
    per_seq = S // tm
    row = lambda i: (i, 0)
    tile = lambda w: pl.BlockSpec((tm, w), row)
    gate0 = _ZG // MERGE_GATE_BLOCK

    def layer_weight(w, block=None, at=0):
        block = w.shape[2] if block is None else block
        return pl.BlockSpec((None, w.shape[1], block), lambda i: (layer, 0, at), pipeline_mode=pl.Buffered(1))

    def dilated(rate):
        return pl.BlockSpec((1, rate, tm // rate, A_OUT), lambda i: (i // per_seq, 0, i % per_seq, 0))

    r1, r2 = A_PATTERNS[1][1], A_PATTERNS[2][1]
    return pl.pallas_call(
        _merge_kernel,
        grid=(M // tm,),
        in_specs=[tile(D_MODEL), _const_spec((1, D_MODEL)),
                  tile(A_OUT), dilated(r1), dilated(r2), tile(A_OUT), dilated(r1), dilated(r2),
                  tile(B_W), tile(C_QW),
                  *[layer_weight(w_in, MERGE_GATE_BLOCK, gate0 + j)
                    for j in range(N_BRANCH * D_MODEL // MERGE_GATE_BLOCK)],
                  layer_weight(w_br_a), layer_weight(w_br_b), layer_weight(w_br_c), layer_weight(w_o)],
        out_specs=tile(D_MODEL),
        out_shape=jax.ShapeDtypeStruct((M, D_MODEL), jnp.float32),
        scratch_shapes=[pltpu.VMEM((A_OUT // LANES, tm, LANES), jnp.float32)] * 4,
        compiler_params=_params(("arbitrary",)),
        name="gated_merge",
    )(x2, g1, oa[0].reshape(M, A_OUT), oa[1], oa[2], lse[0].reshape(M, A_OUT), lse[1], lse[2],
      ob, oc, *([w_in] * (N_BRANCH * D_MODEL // MERGE_GATE_BLOCK)), w_br_a, w_br_b, w_br_c, w_o)


FFN_TM = 512
FFN_CHUNK = 256
FFN_DOWN_CHUNK = D_FF // 2


def _ffn_kernel(x_ref, g_ref, wup_ref, wdown_ref, out_ref, act_ref):
    x = x_ref[...]
    xn = _rms(x, g_ref[...]).astype(_MXU_DTYPE)
    for c in range(D_FF // FFN_CHUNK):
        cs = slice(c * FFN_CHUNK, (c + 1) * FFN_CHUNK)
        a = _dot(xn, wup_ref[:, cs].astype(_MXU_DTYPE))
        b = _dot(xn, wup_ref[:, D_FF + c * FFN_CHUNK:D_FF + (c + 1) * FFN_CHUNK].astype(_MXU_DTYPE))
        act_ref[:, cs] = (a * jax.nn.sigmoid(a) * b).astype(act_ref.dtype)
    y = x
    for c in range(D_FF // FFN_DOWN_CHUNK):
        rows = slice(c * FFN_DOWN_CHUNK, (c + 1) * FFN_DOWN_CHUNK)
        y = y + _dot(act_ref[:, rows], wdown_ref[rows, :].astype(_MXU_DTYPE))
    out_ref[...] = y


def _ffn(x2, g2, w_up, w_down, layer):
    M = x2.shape[0]
    tm = FFN_TM
    row = lambda i: (i, 0)
    layer_weight = lambda w: pl.BlockSpec((None,) + w.shape[1:], lambda i: (layer, 0, 0),
                                          pipeline_mode=pl.Buffered(1))
    return pl.pallas_call(
        _ffn_kernel,
        grid=(M // tm,),
        in_specs=[pl.BlockSpec((tm, D_MODEL), row), _const_spec((1, D_MODEL)),
                  layer_weight(w_up), layer_weight(w_down)],
        out_specs=pl.BlockSpec((tm, D_MODEL), row),
        out_shape=jax.ShapeDtypeStruct((M, D_MODEL), jnp.float32),
        scratch_shapes=[pltpu.VMEM((tm, D_FF), _MXU_DTYPE)],
        compiler_params=_params(("arbitrary",)),
        name="swiglu_ffn",
    )(x2, g2, w_up, w_down)


def _t5_bucket(rel):
    half = T5_BUCKETS // 2
    max_exact = half // 2
    ret = jnp.where(rel > 0, half, 0)
    n = jnp.abs(rel)
    nf = jnp.maximum(n, 1).astype(jnp.float32)
    large = max_exact + (jnp.log(nf / max_exact) / math.log(T5_MAX_DIST / max_exact)
                         * (half - max_exact)).astype(jnp.int32)
    large = jnp.minimum(large, half - 1)
    return ret + jnp.where(n < max_exact, n, large)


def _mixer_a_bias(table_g, rate):
    j = jnp.arange(A_SPAN)[:, None]
    i = jnp.arange(A_UNIT)[None, :]
    step = j - A_RADIUS - i
    onehot = (_t5_bucket(step * rate)[:, :, None] == jnp.arange(T5_BUCKETS)).astype(jnp.float32)
    bias = jnp.einsum("jib,bh->hji", onehot, table_g.astype(jnp.float32) * LOG2_E,
                      precision=lax.Precision.HIGHEST)
    return jnp.where((jnp.abs(step) <= A_RADIUS)[None], bias, NEG)


B_N_DR = 2 * B_WIN_ROWS - 1


def _mixer_b_blocks(rpb, shift):
    c = np.arange(GRID_W)
    c0 = np.clip(c - B_WIN_COLS // 2, 0, GRID_W - B_WIN_COLS)
    col_ok = (c[:, None] >= c0[None, :]) & (c[:, None] < c0[None, :] + B_WIN_COLS)
    dc = np.clip(c[:, None] - c[None, :] + B_WIN_COLS - 1, 0, 2 * B_WIN_COLS - 2)
    pick_c = ((dc[..., None] == np.arange(2 * B_WIN_COLS - 1)) & col_ok[..., None]).astype(np.float32)
    blocks = jnp.einsum("hdk,xyk->hdxy", rpb.astype(jnp.float32) * LOG2_E, pick_c,
                        precision=lax.Precision.HIGHEST)
    blocks = jnp.where(col_ok[None, None], blocks - shift, NEG)
    blocks = jnp.concatenate([blocks, jnp.full((B_HEADS, 1, GRID_W, GRID_W), NEG, jnp.float32)], axis=1)
    return jnp.concatenate([blocks, blocks], axis=-1)


def _mixer_b_block_index(rows):
    first_query_row = np.array([0, B_WIN_ROWS // 2, rows - B_UNIT_ROWS])
    i = first_query_row[:, None, None] + np.arange(B_UNIT_ROWS)[None, None, :]
    r0 = np.clip(first_query_row - B_WIN_ROWS // 2, 0, rows - B_SPAN_ROWS)
    ik = r0[:, None, None] + np.arange(B_SPAN_ROWS)[None, :, None]
    rs = np.clip(i - B_WIN_ROWS // 2, 0, rows - B_WIN_ROWS)
    row_ok = (ik >= rs) & (ik < rs + B_WIN_ROWS)
    return np.where(row_ok, ik - i + B_WIN_ROWS - 1, B_N_DR).astype(np.int32).reshape(-1)


def _rope_tables(S):
    rows = S // GRID_W
    inv = ROPE_THETA ** (-jnp.arange(0, ROPE_AXIS_DIM, 2, dtype=jnp.float32) / ROPE_AXIS_DIM)
    d = np.arange(LANES) % HEAD_DIM
    inv_lane = inv[d % (ROPE_AXIS_DIM // 2)][None, :]
    is_col = (d >= ROPE_AXIS_DIM)[None, None, :]
    first = ((d % ROPE_AXIS_DIM) < ROPE_AXIS_DIM // 2)[None, :]
    tables = []
    for n in (rows, GRID_W):
        ang = jnp.arange(n, dtype=jnp.float32)[:, None] * inv_lane
        sin = jnp.sin(ang)
        tables.append((jnp.cos(ang), jnp.where(first, -sin, 0.0), jnp.where(first, 0.0, sin)))
    return tuple(jnp.where(is_col, by_col[None, :, :], by_row[:, None, :]).reshape(S, LANES)
                 for by_row, by_col in zip(*tables))


def _softmax_shift(gain_q, gain_k, bias_abs_max):
    bound = (HEAD_DIM * QK_SCALE * LOG2_E * ROUNDING_SLACK * jnp.max(jnp.abs(gain_q)) * jnp.max(jnp.abs(gain_k))
             + LOG2_E * bias_abs_max)
    small = bound <= SHIFT_LIMIT
    return jnp.stack([jnp.where(small, bound, 0.0), small.astype(jnp.float32)]).astype(jnp.float32)


def _block_diag_ones():
    i = np.arange(PROJ_TILE) // HEAD_DIM
    return jnp.asarray(i[:, None] == i[None, :], _MXU_DTYPE)


def kernel(x, rel_bias_table, norm1, w_in, qk_gain, nat_rpb, w_br_a, w_br_b, w_br_c, w_o,
           norm2, w_up, w_down):
    B, S, D = x.shape
    depth = w_in.shape[0]
    M = B * S
    x2 = x.reshape(M, D)
    rope = _rope_tables(S)
    ones_bd = _block_diag_ones()
    tables = [rel_bias_table[:, g * A_HEADS:(g + 1) * A_HEADS] for g in range(A_GROUPS)]
    shift_a = jnp.stack([jnp.stack([_softmax_shift(qk_gain[l, 0], qk_gain[l, 1], jnp.max(jnp.abs(t)))
                                    for t in tables]) for l in range(depth)])
    a_bias = jnp.stack([_mixer_a_bias(t, rate) for t, (_, rate) in zip(tables, A_PATTERNS)])
    a_bias = a_bias[None] - shift_a[:, :, 0][:, :, None, None, None]
    for l in range(depth):
        gq = qk_gain[l]
        tile = lambda g, n, s: jnp.tile(g * s, n)
        gain_row = jnp.concatenate([
            tile(gq[0], A_W // HEAD_DIM, QK_SCALE * LOG2_E), tile(gq[1], A_W // HEAD_DIM, 1.0),
            tile(gq[2], B_HEADS, QK_SCALE * LOG2_E), tile(gq[3], B_HEADS, 1.0),
            tile(gq[4], C_Q_HEADS, QK_SCALE * LOG2_E), tile(gq[5], C_KV_HEADS, 1.0)])[None, :]
        g1 = norm1[l][None, :]
        za0, za1, za2, qbt, kb, vbt, qc, kc, vc = _qkv_projection(
            x2, g1, w_in, l, gain_row, ones_bd, rope, B, S)
        za =(za0.reshape(B, 1, S, ZA_W), za1, za2)
        oa, lse = [], []
        for g, (_, rate) in enumerate(A_PATTERNS):
            o_g, l_g = _mixer_a_group(za[g], a_bias, shift_a, l, g, rate, B, S)
            oa.append(o_g)
            lse.append(l_g)
        shift_b = _softmax_shift(gq[2], gq[3], jnp.max(jnp.abs(nat_rpb[l])))
        ob = _mixer_b(qbt, kb, vbt, _mixer_b_blocks(nat_rpb[l], shift_b[0]), shift_b, B, S)
        oc = _mixer_c(qc, kc, vc, _softmax_shift(gq[4], gq[5], 0.0), B, S)
        x2 = _merge(x2, g1, oa, lse, ob, oc, w_in, w_br_a, w_br_b, w_br_c, w_o, l, B, S)
        x2 = _ffn(x2, norm2[l][None, :], w_up, w_down, l)
    return x2.reshape(B, S, D)
```

```python
import functools
import math

import jax
import jax.numpy as jnp
import numpy as np
from jax import lax
from jax.experimental import pallas as pl
from jax.experimental.pallas import tpu as pltpu

_MXU_DTYPE = jnp.bfloat16

D_MODEL = 1024
HEAD_DIM = 64
GRID_W = 64
RMS_EPS = 1e-6
NEG = -1e30
A_PATTERNS = ((128, 1), (512, 4), (2048, 16))
A_GROUPS = 3
A_HEADS = 4
A_W = A_GROUPS * A_HEADS * HEAD_DIM
A_OUT = A_HEADS * HEAD_DIM
A_RADIUS = 64
B_HEADS = 8
B_W = B_HEADS * HEAD_DIM
B_WIN_ROWS = 8
B_WIN_COLS = 16
C_Q_HEADS = 8
C_KV_HEADS = 2
C_GROUP = C_Q_HEADS // C_KV_HEADS
C_QW = C_Q_HEADS * HEAD_DIM
C_KVW = C_KV_HEADS * HEAD_DIM
ROPE_THETA = 10000.0
ROPE_AXIS_DIM = HEAD_DIM // 2
T5_BUCKETS = 32
T5_MAX_DIST = 1024
N_BRANCH = 3
D_FF = math.ceil(8 * D_MODEL / 3 / 256) * 256
QK_SCALE = HEAD_DIM ** -0.5
LOG2_E = math.log2(math.e)

V7X_VMEM_BYTES = 64 * 1024 * 1024
VMEM_LIMIT_BYTES = V7X_VMEM_BYTES * 7 // 8
LANES = 128
BF16_SUBLANES = 16
MXU_COLS = 256
VT_ROWS = HEAD_DIM + BF16_SUBLANES
SHIFT_LIMIT = 60.0
ROUNDING_SLACK = 1.02

_OFF = np.cumsum([0, A_W, A_W, A_W, B_W, B_W, B_W, C_QW, C_KVW, C_KVW]).tolist()
(_QA, _KA, _VA, _QB, _KB, _VB, _QC, _KC, _VC, _ZG) = _OFF

ZA_W = 3 * A_OUT
B_PAIRS = B_HEADS // 2


def _params(sem):
    return pltpu.CompilerParams(dimension_semantics=sem, vmem_limit_bytes=VMEM_LIMIT_BYTES)


def _const_spec(shape):
    nd = len(shape)
    return pl.BlockSpec(shape, lambda *_: (0,) * nd, pipeline_mode=pl.Buffered(1))


def _rms(x, g):
    return x * lax.rsqrt(jnp.mean(x * x, axis=-1, keepdims=True) + RMS_EPS) * g


def _dot(a, b):
    return jnp.dot(a, b, preferred_element_type=jnp.float32)


def _dot_nt(a, b):
    return lax.dot_general(a, b, (((1,), (1,)), ((), ())), preferred_element_type=jnp.float32)


def _ones_tail(width, dtype):
    return (lax.broadcasted_iota(jnp.int32, (VT_ROWS - HEAD_DIM, width), 0) == 0).astype(dtype)


PROJ_TM = 1024
PROJ_TILE = MXU_COLS
PROJ_W_BLOCK = 1536
_N_NORM = 2 * A_W + 2 * B_W
_N_ROPE = C_QW + C_KVW


def _proj_kernel(x_ref, g_ref, w0_ref, w1_ref, w2_ref, gain_ref, ones_ref, cos_ref, s1_ref, s2_ref,
                 za0_ref, za1_ref, za2_ref, qbt_ref, kb_ref, vbt_ref, qc_ref, kc_ref, vc_ref, dil_ref):
    tm = x_ref.shape[0]
    w_refs = (w0_ref, w1_ref, w2_ref)
    h = _rms(x_ref[...], g_ref[...]).astype(_MXU_DTYPE)
    za_refs = (za0_ref, za1_ref, za2_ref)

    def head_norm(acc, c0, width):
        sq = (acc * acc).astype(_MXU_DTYPE)
        ms = _dot(sq, ones_ref[:width, :width]) * (1.0 / HEAD_DIM)
        return acc * lax.rsqrt(ms + RMS_EPS) * gain_ref[:, c0:c0 + width]

    def store_group(g, section, val):
        cols = slice(section * A_OUT, (section + 1) * A_OUT)
        rate = A_PATTERNS[g][1]
        if rate == 1:
            za0_ref[:, cols] = val.astype(za0_ref.dtype)
            return
        for j in range(A_OUT // LANES):
            dil_ref[j] = val[:, j * LANES:(j + 1) * LANES]
        for r in range(rate):
            picked = [dil_ref[j, pl.ds(r, tm // rate, stride=rate), :] for j in range(A_OUT // LANES)]
            za_refs[g][0, r, :, cols] = jnp.concatenate(picked, axis=1).astype(za_refs[g].dtype)

    def store_pairs(ref, t, val):
        vt = val.T.astype(ref.dtype)
        ref[0, 2 * t] = vt[:LANES]
        ref[0, 2 * t + 1] = vt[LANES:]

    def rotary(y, width):
        reps = width // LANES
        table = lambda ref: jnp.concatenate([ref[...]] * reps, axis=1) if reps > 1 else ref[...]
        return (y * table(cos_ref) + pltpu.roll(y, width - 16, 1) * table(s1_ref)
                + pltpu.roll(y, 16, 1) * table(s2_ref))

    def store_qc(t, acc, gain_at):
        yt = rotary(head_norm(acc, gain_at, PROJ_TILE), PROJ_TILE).T.astype(qc_ref.dtype)
        heads = PROJ_TILE // HEAD_DIM
        for j in range(heads):
            qc_ref[0, heads * t + j] = yt[j * HEAD_DIM:(j + 1) * HEAD_DIM]

    def store_kc_vc(acc, gain_at):
        y = rotary(head_norm(acc[:, :LANES], gain_at, LANES), LANES).astype(kc_ref.dtype)
        acc_t = acc[:, LANES:].T.astype(vc_ref.dtype)
        tail = _ones_tail(tm, vc_ref.dtype)
        for kv in range(C_KV_HEADS):
            kc_ref[0, kv] = y[:, kv * HEAD_DIM:(kv + 1) * HEAD_DIM]
            vc_ref[0, kv, :HEAD_DIM] = acc_t[kv * HEAD_DIM:(kv + 1) * HEAD_DIM]
            vc_ref[0, kv, HEAD_DIM:] = tail

    jobs = []
    for section, src in enumerate((_QA, _KA)):
        for g in range(A_GROUPS):
            gain_at = section * A_W + g * A_OUT
            jobs.append((src + g * A_OUT, A_OUT, lambda acc, gain_at=gain_at, g=g, section=section:
                         store_group(g, section, head_norm(acc, gain_at, A_OUT))))
    for t in range(B_W // PROJ_TILE):
        gain_at = 2 * A_W + t * PROJ_TILE
        jobs.append((_QB + t * PROJ_TILE, PROJ_TILE, lambda acc, gain_at=gain_at, t=t:
                     store_pairs(qbt_ref, t, head_norm(acc, gain_at, PROJ_TILE))))
    for t in range(B_W // PROJ_TILE):
        gain_at = 2 * A_W + B_W + t * PROJ_TILE
        def store_kb(acc, gain_at=gain_at, t=t):
            kb_ref[:, t * PROJ_TILE:(t + 1) * PROJ_TILE] = head_norm(acc, gain_at, PROJ_TILE).astype(kb_ref.dtype)
        jobs.append((_KB + t * PROJ_TILE, PROJ_TILE, store_kb))
    for t in range(C_QW // PROJ_TILE):
        jobs.append((_QC + t * PROJ_TILE, PROJ_TILE,
                     lambda acc, t=t: store_qc(t, acc, _N_NORM + t * PROJ_TILE)))
    jobs.append((_KC, 2 * C_KVW, lambda acc: store_kc_vc(acc, _N_NORM + C_QW)))
    for g in range(A_GROUPS):
        jobs.append((_VA + g * A_OUT, A_OUT, lambda acc, g=g: store_group(g, 2, acc)))
    for t in range(B_W // PROJ_TILE):
        jobs.append((_VB + t * PROJ_TILE, PROJ_TILE, lambda acc, t=t: store_pairs(vbt_ref, t, acc)))

    def product(n):
        src, width, _ = jobs[n]
        blk, col = divmod(src, PROJ_W_BLOCK)
        return _dot(h, w_refs[blk][:, col:col + width].astype(_MXU_DTYPE))

    acc_next = product(0)
    for n, (_, _, consume) in enumerate(jobs):
        acc = acc_next
        if n + 1 < len(jobs):
            acc_next = product(n + 1)
        consume(acc)


def _qkv_projection(x2, g1, w_in, layer, gain_row, ones_bd, rope, B, S):
    M = x2.shape[0]
    tm = PROJ_TM
    per_seq = S // tm
    w_block = lambda j: pl.BlockSpec((None, D_MODEL, PROJ_W_BLOCK), lambda i: (layer, 0, j),
                                     pipeline_mode=pl.Buffered(1))
    cos_t, s1_t, s2_t = rope
    r1, r2 = A_PATTERNS[1][1], A_PATTERNS[2][1]
    row = lambda i: (i, 0)
    pos = lambda i: (i % per_seq, 0)
    hm = lambda i: (i // per_seq, 0, i % per_seq, 0)
    hm_t = lambda i: (i // per_seq, 0, 0, i % per_seq)
    cd = _MXU_DTYPE
    return pl.pallas_call(
        _proj_kernel,
        grid=(M // tm,),
        in_specs=[
            pl.BlockSpec((tm, D_MODEL), row),
            _const_spec((1, D_MODEL)),
            w_block(0), w_block(1), w_block(2),
            _const_spec((1, _N_NORM + _N_ROPE)),
            _const_spec((PROJ_TILE, PROJ_TILE)),
            pl.BlockSpec((tm, LANES), pos),
            pl.BlockSpec((tm, LANES), pos),
            pl.BlockSpec((tm, LANES), pos),
        ],
        out_specs=[
            pl.BlockSpec((tm, ZA_W), row),
            pl.BlockSpec((1, r1, tm // r1, ZA_W), hm),
            pl.BlockSpec((1, r2, tm // r2, ZA_W), hm),
            pl.BlockSpec((1, B_PAIRS, LANES, tm), hm_t),
            pl.BlockSpec((tm, B_W), row),
            pl.BlockSpec((1, B_PAIRS, LANES, tm), hm_t),
            pl.BlockSpec((1, C_Q_HEADS, HEAD_DIM, tm), hm_t),
            pl.BlockSpec((1, C_KV_HEADS, tm, HEAD_DIM), hm),
            pl.BlockSpec((1, C_KV_HEADS, VT_ROWS, tm), hm_t),
        ],
        out_shape=[
            jax.ShapeDtypeStruct((M, ZA_W), cd),
            jax.ShapeDtypeStruct((B, r1, S // r1, ZA_W), cd),
            jax.ShapeDtypeStruct((B, r2, S // r2, ZA_W), cd),
            jax.ShapeDtypeStruct((B, B_PAIRS, LANES, S), cd),
            jax.ShapeDtypeStruct((M, B_W), cd),
            jax.ShapeDtypeStruct((B, B_PAIRS, LANES, S), cd),
            jax.ShapeDtypeStruct((B, C_Q_HEADS, HEAD_DIM, S), cd),
            jax.ShapeDtypeStruct((B, C_KV_HEADS, S, HEAD_DIM), cd),
            jax.ShapeDtypeStruct((B, C_KV_HEADS, VT_ROWS, S), cd),
        ],
        scratch_shapes=[pltpu.VMEM((A_OUT // LANES, tm, LANES), jnp.float32)],
        compiler_params=_params(("arbitrary",)),
        name="qkv_projection",
    )(x2, g1, w_in, w_in, w_in, gain_row, ones_bd, cos_t, s1_t, s2_t)


A_UNIT = 4 * A_RADIUS
A_SPAN = A_UNIT + 2 * A_RADIUS
A_TILE = 2048


def _mixer_a_kernel(q_ref, kp_ref, ko_ref, kn_ref, vp_ref, vo_ref, vn_ref, bias_ref, shift_ref,
                    o_ref, lse_ref, kw_ref, vw_ref, s_ref, *, tl, seq_len):
    n_res = q_ref.shape[1]
    l0 = pl.program_id(2) * tl
    for res in range(n_res):
        kw_ref[res, 0:A_RADIUS] = kp_ref[0, res]
        kw_ref[res, A_RADIUS:A_RADIUS + tl] = ko_ref[0, res]
        kw_ref[res, A_RADIUS + tl:] = kn_ref[0, res]
        vw_ref[res, 0:A_RADIUS] = vp_ref[0, res]
        vw_ref[res, A_RADIUS:A_RADIUS + tl] = vo_ref[0, res]
        vw_ref[res, A_RADIUS + tl:] = vn_ref[0, res]
    even_q = lax.broadcasted_iota(jnp.int32, (A_UNIT, LANES), 1) < HEAD_DIM
    lane_k = lax.broadcasted_iota(jnp.int32, (A_SPAN, LANES), 1)

    per_res = tl // A_UNIT
    n_units = n_res * per_res

    def place(n):
        return n // per_res, pl.multiple_of((n % per_res) * A_UNIT, A_UNIT)

    def scores(n, pair):
        res, off = place(n)
        q = q_ref[0, res, pl.ds(off, A_UNIT), pair * LANES:(pair + 1) * LANES]
        zero = jnp.zeros_like(q)
        q2 = jnp.concatenate([jnp.where(even_q, q, zero), jnp.where(even_q, zero, q)], axis=0)
        return _dot_nt(kw_ref[res, pl.ds(off, A_SPAN), pair * LANES:(pair + 1) * LANES], q2)

    s_ref[...] = scores(0, 0)

    def unit(fixed_shift, u, carry):
        res, off = place(u)
        top_ok = l0 + off - A_RADIUS >= 0
        bot_ok = l0 + off + A_UNIT + A_RADIUS <= seq_len
        outs, lses = [], []
        s_next = s_ref[...]
        for pair in range(A_HEADS // 2):
            s_pair = s_next
            if pair + 1 < A_HEADS // 2:
                s_next = scores(u, pair + 1)
            else:
                s_ref[...] = scores(jnp.minimum(u + 1, n_units - 1), 0)
            v_slab = vw_ref[res, pl.ds(off, A_SPAN), pair * LANES:(pair + 1) * LANES]
            for odd in range(2):
                s = s_pair[:, odd * A_UNIT:(odd + 1) * A_UNIT] + bias_ref[2 * pair + odd]
                s = jnp.concatenate([jnp.where(top_ok, s[:A_RADIUS], NEG), s[A_RADIUS:A_SPAN - A_RADIUS],
                                     jnp.where(bot_ok, s[A_SPAN - A_RADIUS:], NEG)], axis=0)
                if fixed_shift:
                    m = shift_ref[0]
                    p = jnp.exp2(s).astype(_MXU_DTYPE)
                else:
                    m = jnp.max(s, axis=0, keepdims=True)
                    p = jnp.exp2(s - m).astype(_MXU_DTYPE)
                den_row = (1 - odd) * HEAD_DIM
                mine = (lane_k >= odd * HEAD_DIM) & (lane_k < (odd + 1) * HEAD_DIM)
                v_aug = jnp.where(mine, v_slab, (lane_k == den_row).astype(v_slab.dtype))
                acc = lax.dot_general(v_aug, p, (((0,), (0,)), ((), ())),
                                      preferred_element_type=jnp.float32)
                den = acc[den_row:den_row + 1]
                outs.append(acc[odd * HEAD_DIM:(odd + 1) * HEAD_DIM] / den)
                lses.append(jnp.broadcast_to(m + jnp.log2(den), (HEAD_DIM, A_UNIT)))
        o_ref[0, res, pl.ds(off, A_UNIT), :] = jnp.concatenate(outs, axis=0).T
        lse_ref[0, res, pl.ds(off, A_UNIT), :] = jnp.concatenate(lses, axis=0).T
        return carry

    small = shift_ref[1] > 0.5

    @pl.when(small)
    def _():
        lax.fori_loop(0, n_units, functools.partial(unit, True), 0, unroll=min(4, n_units))

    @pl.when(jnp.logical_not(small))
    def _():
        lax.fori_loop(0, n_units, functools.partial(unit, False), 0)


def _mixer_a_group(za_g, bias, shift, rate, B, S):
    L = S // rate
    tl = min(A_TILE, L)
    n_res = min(rate, A_TILE // tl)
    nblk = L // A_RADIUS
    per = tl // A_RADIUS

    def own(section):
        return lambda b, r, l: (b, r, l, section)

    def prev(section):
        return lambda b, r, l: (b, r, jnp.maximum(l * per - 1, 0), section)

    def nxt(section):
        return lambda b, r, l: (b, r, jnp.minimum((l + 1) * per, nblk - 1), section)

    edge = (1, n_res, A_RADIUS, A_OUT)
    full = (1, n_res, tl, A_OUT)
    return pl.pallas_call(
        functools.partial(_mixer_a_kernel, tl=tl, seq_len=L),
        grid=(B, rate // n_res, L // tl),
        in_specs=[
            pl.BlockSpec(full, own(0)),
            pl.BlockSpec(edge, prev(1)), pl.BlockSpec(full, own(1)), pl.BlockSpec(edge, nxt(1)),
            pl.BlockSpec(edge, prev(2)), pl.BlockSpec(full, own(2)), pl.BlockSpec(edge, nxt(2)),
            _const_spec((A_HEADS, A_SPAN, A_UNIT)),
            pl.BlockSpec(memory_space=pltpu.SMEM),
        ],
        out_specs=[pl.BlockSpec(full, own(0)), pl.BlockSpec(full, own(0))],
        out_shape=[jax.ShapeDtypeStruct((B, rate, L, A_OUT), jnp.float32)] * 2,
        scratch_shapes=[pltpu.VMEM((n_res, tl + 2 * A_RADIUS, A_OUT), _MXU_DTYPE)] * 2
        + [pltpu.VMEM((A_SPAN, 2 * A_UNIT), jnp.float32)],
        compiler_params=_params(("arbitrary",) * 3),
        name=f"mixer_a_rate{rate}",
    )(za_g, za_g, za_g, za_g, za_g, za_g, za_g, bias, shift)


B_UNIT_ROWS = 4
B_UNIT = B_UNIT_ROWS * GRID_W
B_SPAN_ROWS = B_UNIT_ROWS + B_WIN_ROWS
B_SPAN = B_SPAN_ROWS * GRID_W
B_HALO = (B_WIN_ROWS // 2) * GRID_W
B_TILE_ROWS = 32
B_TILE = B_TILE_ROWS * GRID_W


def _mixer_b_kernel(q_ref, kp_ref, ko_ref, kn_ref, vp_ref, vo_ref, vn_ref, blocks_ref, sel_ref, shift_ref,
                    o_ref, kw_ref, vw_ref, bias_ref, s_ref, *, rows):
    @pl.when((pl.program_id(0) == 0) & (pl.program_id(1) == 0))
    def _():
        left = lax.broadcasted_iota(jnp.int32, (GRID_W, LANES), 1) < GRID_W

        def fill(n, carry):
            v, h = n // B_HEADS, n % B_HEADS
            for a in range(B_SPAN_ROWS):
                for gp in range(B_UNIT_ROWS // 2):
                    at = (v * B_SPAN_ROWS + a) * B_UNIT_ROWS + 2 * gp
                    tile = jnp.where(left, blocks_ref[h, sel_ref[at]], blocks_ref[h, sel_ref[at + 1]])
                    bias_ref[v, h, a * GRID_W:(a + 1) * GRID_W, gp * LANES:(gp + 1) * LANES] = tile
            return carry

        lax.fori_loop(0, 3 * B_HEADS, fill, 0)

    i0 = pl.program_id(1) * B_TILE_ROWS
    kw_ref[0:B_HALO] = kp_ref[0]
    kw_ref[B_HALO:B_HALO + B_TILE] = ko_ref[0]
    kw_ref[B_HALO + B_TILE:] = kn_ref[0]
    vw_ref[:, :, 0:B_HALO] = vp_ref[0]
    vw_ref[:, :, B_HALO:B_HALO + B_TILE] = vo_ref[0]
    vw_ref[:, :, B_HALO + B_TILE:] = vn_ref[0]
    tail = _ones_tail(B_SPAN, _MXU_DTYPE)
    upper = lax.broadcasted_iota(jnp.int32, (LANES, B_UNIT), 0) < HEAD_DIM

    n_units = B_TILE_ROWS // B_UNIT_ROWS

    def span_offset(u):
        r0 = jnp.clip(i0 + u * B_UNIT_ROWS - B_WIN_ROWS // 2, 0, rows - B_SPAN_ROWS)
        return pl.multiple_of((r0 - (i0 - B_WIN_ROWS // 2)) * GRID_W, LANES)

    def scores(u, pair):
        qt = q_ref[0, pair, :, pl.ds(pl.multiple_of(u * B_UNIT, B_UNIT), B_UNIT)]
        zero = jnp.zeros_like(qt)
        qt2 = jnp.concatenate([jnp.where(upper, qt, zero), jnp.where(upper, zero, qt)], axis=1)
        return _dot(kw_ref[pl.ds(span_offset(u), B_SPAN), pair * LANES:(pair + 1) * LANES], qt2)

    s_ref[...] = scores(0, 0)

    def unit(fixed_shift, u, carry):
        i0u = i0 + u * B_UNIT_ROWS
        off = span_offset(u)
        variant = jnp.where(i0u == 0, 0, jnp.where(i0u == rows - B_UNIT_ROWS, 2, 1))
        qoff = pl.multiple_of(u * B_UNIT, B_UNIT)
        outs = []
        s_next = s_ref[...]
        for pair in range(B_PAIRS):
            s_pair = s_next
            if pair + 1 < B_PAIRS:
                s_next = scores(u, pair + 1)
            else:
                s_ref[...] = scores(jnp.minimum(u + 1, n_units - 1), 0)
            for odd in range(2):
                s = s_pair[:, odd * B_UNIT:(odd + 1) * B_UNIT] + bias_ref[variant, 2 * pair + odd]
                if not fixed_shift:
                    s = s - jnp.max(s, axis=0, keepdims=True)
                p = jnp.exp2(s).astype(_MXU_DTYPE)
                vt = vw_ref[pair, odd * HEAD_DIM:(odd + 1) * HEAD_DIM, pl.ds(off, B_SPAN)]
                acc = _dot(jnp.concatenate([vt, tail], axis=0), p)
                outs.append(acc[:HEAD_DIM] / acc[HEAD_DIM:HEAD_DIM + 1])
        o_ref[0, pl.ds(qoff, B_UNIT), :] = jnp.concatenate(outs, axis=0).T.astype(o_ref.dtype)
        return carry

    small = shift_ref[1] > 0.5

    @pl.when(small)
    def _():
        lax.fori_loop(0, n_units, functools.partial(unit, True), 0, unroll=n_units)

    @pl.when(jnp.logical_not(small))
    def _():
        lax.fori_loop(0, n_units, functools.partial(unit, False), 0)


def _mixer_b(qbt, kb, vbt, blocks, shift, B, S):
    rows = S // GRID_W
    sel = jnp.asarray(_mixer_b_block_index(rows))
    nt = rows // B_TILE_ROWS
    per = B_TILE // B_HALO
    nh = S // B_HALO
    kb3 = kb.reshape(B, S, B_W)
    prev = lambda t: jnp.maximum(t * per - 1, 0)
    nxt = lambda t: jnp.minimum((t + 1) * per, nh - 1)
    k_edge, k_own = (1, B_HALO, B_W), (1, B_TILE, B_W)
    t_edge, t_own = (1, B_PAIRS, LANES, B_HALO), (1, B_PAIRS, LANES, B_TILE)
    o = pl.pallas_call(
        functools.partial(_mixer_b_kernel, rows=rows),
        grid=(B, nt),
        in_specs=[
            pl.BlockSpec(t_own, lambda b, t: (b, 0, 0, t)),
            pl.BlockSpec(k_edge, lambda b, t: (b, prev(t), 0)),
            pl.BlockSpec(k_own, lambda b, t: (b, t, 0)),
            pl.BlockSpec(k_edge, lambda b, t: (b, nxt(t), 0)),
            pl.BlockSpec(t_edge, lambda b, t: (b, 0, 0, prev(t))),
            pl.BlockSpec(t_own, lambda b, t: (b, 0, 0, t)),
            pl.BlockSpec(t_edge, lambda b, t: (b, 0, 0, nxt(t))),
            _const_spec(blocks.shape),
            pl.BlockSpec(memory_space=pltpu.SMEM),
            pl.BlockSpec(memory_space=pltpu.SMEM),
        ],
        out_specs=pl.BlockSpec(k_own, lambda b, t: (b, t, 0)),
        out_shape=jax.ShapeDtypeStruct((B, S, B_W), _MXU_DTYPE),
        scratch_shapes=[pltpu.VMEM((B_TILE + 2 * B_HALO, B_W), _MXU_DTYPE),
                        pltpu.VMEM((B_PAIRS, LANES, B_TILE + 2 * B_HALO), _MXU_DTYPE),
                        pltpu.VMEM((3, B_HEADS, B_SPAN, B_UNIT), jnp.float32),
                        pltpu.VMEM((B_SPAN, 2 * B_UNIT), jnp.float32)],
        compiler_params=_params(("arbitrary",) * 2),
        name="mixer_b",
    )(qbt, kb3, kb3, kb3, vbt, vbt, vbt, blocks, sel, shift)
    return o.reshape(B * S, B_W)


C_TQ = 4096
C_TK = 512
C_UNIT = 512
C_AHEAD = 2
C_UNROLL = 1


def _mixer_c_kernel(q_ref, k_ref, vt_ref, shift_ref, o_ref, qcat_ref, m_ref, acc_ref, s_ref, *, n_kv):
    acc_ref[...] = jnp.zeros(acc_ref.shape, jnp.float32)
    for g in range(C_GROUP):
        qcat_ref[:, g * C_TQ:(g + 1) * C_TQ] = q_ref[0, g]
    n_units = C_GROUP * C_TQ // C_UNIT

    def keys(j):
        return k_ref[0, 0, pl.ds(pl.multiple_of(j * C_TK, C_TK), C_TK), :]

    def scores(k, u):
        return _dot(k, qcat_ref[:, u * C_UNIT:(u + 1) * C_UNIT])

    def sweep(update, unroll):
        for a in range(C_AHEAD):
            s_ref[a] = scores(keys(0), a)

        def step(j, carry):
            k = keys(j)
            vt = vt_ref[0, 0, :, pl.ds(pl.multiple_of(j * C_TK, C_TK), C_TK)]
            pending = [s_ref[a] for a in range(C_AHEAD)]
            for u in range(n_units):
                s = pending.pop(0)
                ahead = u + C_AHEAD
                if ahead < n_units:
                    pending.append(scores(k, ahead))
                else:
                    s_ref[ahead - n_units] = scores(keys(jnp.minimum(j + 1, n_kv - 1)), ahead - n_units)
                update(s, vt, slice(u * C_UNIT, (u + 1) * C_UNIT))
            return carry

        lax.fori_loop(0, n_kv, step, 0, unroll=unroll)

    def fixed_shift(s, vt, cols):
        acc_ref[:, cols] += _dot(vt, jnp.exp2(s - shift_ref[0]).astype(_MXU_DTYPE))

    def running_max(s, vt, cols):
        m_prev = m_ref[:, cols]
        m_new = jnp.maximum(m_prev, jnp.max(s, axis=0, keepdims=True))
        alpha = jnp.exp2(m_prev - m_new)
        p = jnp.exp2(s - m_new).astype(_MXU_DTYPE)
        acc_ref[:, cols] = alpha * acc_ref[:, cols] + _dot(vt, p)
        m_ref[:, cols] = m_new

    small = shift_ref[1] > 0.5

    @pl.when(small)
    def _():
        sweep(fixed_shift, C_UNROLL)

    @pl.when(jnp.logical_not(small))
    def _():
        m_ref[...] = jnp.full(m_ref.shape, -jnp.inf, jnp.float32)
        sweep(running_max, 1)

    o_t = jnp.concatenate(
        [acc_ref[:HEAD_DIM, g * C_TQ:(g + 1) * C_TQ] / acc_ref[HEAD_DIM:HEAD_DIM + 1, g * C_TQ:(g + 1) * C_TQ]
         for g in range(C_GROUP)], axis=0)
    o_ref[0] = o_t.T.astype(o_ref.dtype)


def _mixer_c(qc_t, kc, vc_t, shift, B, S):
    o = pl.pallas_call(
        functools.partial(_mixer_c_kernel, n_kv=S // C_TK),
        grid=(B, C_KV_HEADS, S // C_TQ),
        in_specs=[
            pl.BlockSpec((1, C_GROUP, HEAD_DIM, C_TQ), lambda b, kv, i: (b, kv, 0, i)),
            pl.BlockSpec((1, 1, S, HEAD_DIM), lambda b, kv, i: (b, kv, 0, 0)),
            pl.BlockSpec((1, 1, VT_ROWS, S), lambda b, kv, i: (b, kv, 0, 0)),
            pl.BlockSpec(memory_space=pltpu.SMEM),
        ],
        out_specs=pl.BlockSpec((1, C_TQ, C_GROUP * HEAD_DIM), lambda b, kv, i: (b, i, kv)),
        out_shape=jax.ShapeDtypeStruct((B, S, C_QW), _MXU_DTYPE),
        scratch_shapes=[pltpu.VMEM((HEAD_DIM, C_GROUP * C_TQ), _MXU_DTYPE),
                        pltpu.VMEM((1, C_GROUP * C_TQ), jnp.float32),
                        pltpu.VMEM((VT_ROWS, C_GROUP * C_TQ), jnp.float32),
                        pltpu.VMEM((C_AHEAD, C_TK, C_UNIT), jnp.float32)],
        compiler_params=_params(("arbitrary",) * 3),
        name="mixer_c",
    )(qc_t, kc, vc_t, shift)
    return o.reshape(B * S, C_QW)


MERGE_TM = 512
MERGE_GATE_BLOCK = 512


def _merge_kernel(x_ref, g_ref, oa0_ref, oa1_ref, oa2_ref, l0_ref, l1_ref, l2_ref, ob_ref, oc_ref,
                  wg0_ref, wg1_ref, wg2_ref, wg3_ref, wg4_ref, wg5_ref, pa_ref, pb_ref, pc_ref, wo_ref,
                  out_ref, *scratch):
    tm = x_ref.shape[0]
    gate_refs = (wg0_ref, wg1_ref, wg2_ref, wg3_ref, wg4_ref, wg5_ref)
    cast = lambda ref: ref[...].astype(_MXU_DTYPE)

    def token_major(ref, scr):
        rate = ref.shape[1]
        halves = range(A_OUT // LANES)
        for r in range(rate):
            for j in halves:
                scr[j, pl.ds(r, tm // rate, stride=rate), :] = ref[0, r, :, j * LANES:(j + 1) * LANES]
        return jnp.concatenate([scr[j] for j in halves], axis=1)

    x = x_ref[...]
    h = _rms(x, g_ref[...]).astype(_MXU_DTYPE)
    n_parts = D_MODEL // MERGE_GATE_BLOCK
    branches = [None, _dot(ob_ref[...], cast(pb_ref)), _dot(oc_ref[...], cast(pc_ref))]
    gates = {(b, part): jax.nn.sigmoid(_dot(h, cast(gate_refs[b * n_parts + part])))
             for b in (1, 2, 0) for part in range(n_parts)}
    oa0, l0 = oa0_ref[...], l0_ref[...]
    oa1, l1 = token_major(oa1_ref, scratch[0]), token_major(l1_ref, scratch[1])
    oa2, l2 = token_major(oa2_ref, scratch[2]), token_major(l2_ref, scratch[3])
    mx = jnp.maximum(jnp.maximum(l0, l1), l2)
    w0, w1, w2 = jnp.exp2(l0 - mx), jnp.exp2(l1 - mx), jnp.exp2(l2 - mx)
    o_a = (w0 * oa0 + w1 * oa1 + w2 * oa2) / (w0 + w1 + w2)
    branches[0] = _dot(o_a.astype(_MXU_DTYPE), cast(pa_ref))
    parts = []
    for part in range(n_parts):
        cols = slice(part * MERGE_GATE_BLOCK, (part + 1) * MERGE_GATE_BLOCK)
        merged = sum(gates[b, part] * branches[b][:, cols] for b in range(1, N_BRANCH))
        parts.append((merged + gates[0, part] * branches[0][:, cols]).astype(_MXU_DTYPE))
    out_ref[...] = x + _dot(jnp.concatenate(parts, axis=1), cast(wo_ref))


def _merge(x2, g1, oa, lse, ob, oc, w_in, w_br_a, w_br_b, w_br_c, w_o, layer, B, S):
    M = x2.shape[0]
    tm = MERGE_TM
    per_seq = S // tm
    row = lambda i: (i, 0)
    tile = lambda w: pl.BlockSpec((tm, w), row)
    gate0 = _ZG // MERGE_GATE_BLOCK

    def layer_weight(w, block=None, at=0):
        block = w.shape[2] if block is None else block
        return pl.BlockSpec((None, w.shape[1], block), lambda i: (layer, 0, at), pipeline_mode=pl.Buffered(1))

    def dilated(rate):
        return pl.BlockSpec((1, rate, tm // rate, A_OUT), lambda i: (i // per_seq, 0, i % per_seq, 0))

    r1, r2 = A_PATTERNS[1][1], A_PATTERNS[2][1]
    return pl.pallas_call(
        _merge_kernel,
        grid=(M // tm,),
        in_specs=[tile(D_MODEL), _const_spec((1, D_MODEL)),
                  tile(A_OUT), dilated(r1), dilated(r2), tile(A_OUT), dilated(r1), dilated(r2),
                  tile(B_W), tile(C_QW),
                  *[layer_weight(w_in, MERGE_GATE_BLOCK, gate0 + j)
                    for j in range(N_BRANCH * D_MODEL // MERGE_GATE_BLOCK)],
                  layer_weight(w_br_a), layer_weight(w_br_b), layer_weight(w_br_c), layer_weight(w_o)],
        out_specs=tile(D_MODEL),
        out_shape=jax.ShapeDtypeStruct((M, D_MODEL), jnp.float32),
        scratch_shapes=[pltpu.VMEM((A_OUT // LANES, tm, LANES), jnp.float32)] * 4,
        compiler_params=_params(("arbitrary",)),
        name="gated_merge",
    )(x2, g1, oa[0].reshape(M, A_OUT), oa[1], oa[2], lse[0].reshape(M, A_OUT), lse[1], lse[2],
      ob, oc, *([w_in] * (N_BRANCH * D_MODEL // MERGE_GATE_BLOCK)), w_br_a, w_br_b, w_br_c, w_o)


FFN_TM = 512
FFN_CHUNK = 256
FFN_DOWN_CHUNK = D_FF // 2


def _ffn_kernel(x_ref, g_ref, wup_ref, wdown_ref, out_ref, act_ref):
    x = x_ref[...]
    xn = _rms(x, g_ref[...]).astype(_MXU_DTYPE)
    for c in range(D_FF // FFN_CHUNK):
        cs = slice(c * FFN_CHUNK, (c + 1) * FFN_CHUNK)
        a = _dot(xn, wup_ref[:, cs].astype(_MXU_DTYPE))
        b = _dot(xn, wup_ref[:, D_FF + c * FFN_CHUNK:D_FF + (c + 1) * FFN_CHUNK].astype(_MXU_DTYPE))
        act_ref[:, cs] = (a * jax.nn.sigmoid(a) * b).astype(act_ref.dtype)
    y = x
    for c in range(D_FF // FFN_DOWN_CHUNK):
        rows = slice(c * FFN_DOWN_CHUNK, (c + 1) * FFN_DOWN_CHUNK)
        y = y + _dot(act_ref[:, rows], wdown_ref[rows, :].astype(_MXU_DTYPE))
    out_ref[...] = y


def _ffn(x2, g2, w_up, w_down, layer):
    M = x2.shape[0]
    tm = FFN_TM
    row = lambda i: (i, 0)
    layer_weight = lambda w: pl.BlockSpec((None,) + w.shape[1:], lambda i: (layer, 0, 0),
                                          pipeline_mode=pl.Buffered(1))
    return pl.pallas_call(
        _ffn_kernel,
        grid=(M // tm,),
        in_specs=[pl.BlockSpec((tm, D_MODEL), row), _const_spec((1, D_MODEL)),
                  layer_weight(w_up), layer_weight(w_down)],
        out_specs=pl.BlockSpec((tm, D_MODEL), row),
        out_shape=jax.ShapeDtypeStruct((M, D_MODEL), jnp.float32),
        scratch_shapes=[pltpu.VMEM((tm, D_FF), _MXU_DTYPE)],
        compiler_params=_params(("arbitrary",)),
        name="swiglu_ffn",
    )(x2, g2, w_up, w_down)


def _t5_bucket(rel):
    half = T5_BUCKETS // 2
    max_exact = half // 2
    ret = jnp.where(rel > 0, half, 0)
    n = jnp.abs(rel)
    nf = jnp.maximum(n, 1).astype(jnp.float32)
    large = max_exact + (jnp.log(nf / max_exact) / math.log(T5_MAX_DIST / max_exact)
                         * (half - max_exact)).astype(jnp.int32)
    large = jnp.minimum(large, half - 1)
    return ret + jnp.where(n < max_exact, n, large)


def _mixer_a_bias(table_g, rate):
    j = jnp.arange(A_SPAN)[:, None]
    i = jnp.arange(A_UNIT)[None, :]
    step = j - A_RADIUS - i
    onehot = (_t5_bucket(step * rate)[:, :, None] == jnp.arange(T5_BUCKETS)).astype(jnp.float32)
    bias = jnp.einsum("jib,bh->hji", onehot, table_g.astype(jnp.float32) * LOG2_E,
                      precision=lax.Precision.HIGHEST)
    return jnp.where((jnp.abs(step) <= A_RADIUS)[None], bias, NEG)


B_N_DR = 2 * B_WIN_ROWS - 1


def _mixer_b_blocks(rpb, shift):
    c = np.arange(GRID_W)
    c0 = np.clip(c - B_WIN_COLS // 2, 0, GRID_W - B_WIN_COLS)
    col_ok = (c[:, None] >= c0[None, :]) & (c[:, None] < c0[None, :] + B_WIN_COLS)
    dc = np.clip(c[:, None] - c[None, :] + B_WIN_COLS - 1, 0, 2 * B_WIN_COLS - 2)
    pick_c = ((dc[..., None] == np.arange(2 * B_WIN_COLS - 1)) & col_ok[..., None]).astype(np.float32)
    blocks = jnp.einsum("hdk,xyk->hdxy", rpb.astype(jnp.float32) * LOG2_E, pick_c,
                        precision=lax.Precision.HIGHEST)
    blocks = jnp.where(col_ok[None, None], blocks - shift, NEG)
    blocks = jnp.concatenate([blocks, jnp.full((B_HEADS, 1, GRID_W, GRID_W), NEG, jnp.float32)], axis=1)
    return jnp.concatenate([blocks, blocks], axis=-1)


def _mixer_b_block_index(rows):
    first_query_row = np.array([0, B_WIN_ROWS // 2, rows - B_UNIT_ROWS])
    i = first_query_row[:, None, None] + np.arange(B_UNIT_ROWS)[None, None, :]
    r0 = np.clip(first_query_row - B_WIN_ROWS // 2, 0, rows - B_SPAN_ROWS)
    ik = r0[:, None, None] + np.arange(B_SPAN_ROWS)[None, :, None]
    rs = np.clip(i - B_WIN_ROWS // 2, 0, rows - B_WIN_ROWS)
    row_ok = (ik >= rs) & (ik < rs + B_WIN_ROWS)
    return np.where(row_ok, ik - i + B_WIN_ROWS - 1, B_N_DR).astype(np.int32).reshape(-1)


def _rope_tables(S):
    rows = S // GRID_W
    inv = ROPE_THETA ** (-jnp.arange(0, ROPE_AXIS_DIM, 2, dtype=jnp.float32) / ROPE_AXIS_DIM)
    d = np.arange(LANES) % HEAD_DIM
    inv_lane = inv[d % (ROPE_AXIS_DIM // 2)][None, :]
    is_col = (d >= ROPE_AXIS_DIM)[None, None, :]
    first = ((d % ROPE_AXIS_DIM) < ROPE_AXIS_DIM // 2)[None, :]
    tables = []
    for n in (rows, GRID_W):
        ang = jnp.arange(n, dtype=jnp.float32)[:, None] * inv_lane
        sin = jnp.sin(ang)
        tables.append((jnp.cos(ang), jnp.where(first, -sin, 0.0), jnp.where(first, 0.0, sin)))
    return tuple(jnp.where(is_col, by_col[None, :, :], by_row[:, None, :]).reshape(S, LANES)
                 for by_row, by_col in zip(*tables))


def _softmax_shift(gain_q, gain_k, bias_abs_max):
    bound = (HEAD_DIM * QK_SCALE * LOG2_E * ROUNDING_SLACK * jnp.max(jnp.abs(gain_q)) * jnp.max(jnp.abs(gain_k))
             + LOG2_E * bias_abs_max)
    small = bound <= SHIFT_LIMIT
    return jnp.stack([jnp.where(small, bound, 0.0), small.astype(jnp.float32)]).astype(jnp.float32)


def _block_diag_ones():
    i = np.arange(PROJ_TILE) // HEAD_DIM
    return jnp.asarray(i[:, None] == i[None, :], _MXU_DTYPE)


def kernel(x, rel_bias_table, norm1, w_in, qk_gain, nat_rpb, w_br_a, w_br_b, w_br_c, w_o,
           norm2, w_up, w_down):
    B, S, D = x.shape
    depth = w_in.shape[0]
    M = B * S
    x2 = x.reshape(M, D)
    rope = _rope_tables(S)
    ones_bd = _block_diag_ones()
    a_bias = [_mixer_a_bias(rel_bias_table[:, g * A_HEADS:(g + 1) * A_HEADS], rate)
              for g, (_, rate) in enumerate(A_PATTERNS)]
    for l in range(depth):
        gq = qk_gain[l]
        tile = lambda g, n, s: jnp.tile(g * s, n)
        gain_row = jnp.concatenate([
            tile(gq[0], A_W // HEAD_DIM, QK_SCALE * LOG2_E), tile(gq[1], A_W // HEAD_DIM, 1.0),
            tile(gq[2], B_HEADS, QK_SCALE * LOG2_E), tile(gq[3], B_HEADS, 1.0),
            tile(gq[4], C_Q_HEADS, QK_SCALE * LOG2_E), tile(gq[5], C_KV_HEADS, 1.0)])[None, :]
        g1 = norm1[l][None, :]
        za0, za1, za2, qbt, kb, vbt, qc, kc, vc = _qkv_projection(
            x2, g1, w_in, l, gain_row, ones_bd, rope, B, S)
        za =(za0.reshape(B, 1, S, ZA_W), za1, za2)
        oa, lse = [], []
        for g, (_, rate) in enumerate(A_PATTERNS):
            table_g = rel_bias_table[:, g * A_HEADS:(g + 1) * A_HEADS]
            shift_a = _softmax_shift(gq[0], gq[1], jnp.max(jnp.abs(table_g)))
            o_g, l_g = _mixer_a_group(za[g], a_bias[g] - shift_a[0], shift_a, rate, B, S)
            oa.append(o_g)
            lse.append(l_g)
        shift_b = _softmax_shift(gq[2], gq[3], jnp.max(jnp.abs(nat_rpb[l])))
        ob = _mixer_b(qbt, kb, vbt, _mixer_b_blocks(nat_rpb[l], shift_b[0]), shift_b, B, S)
        oc = _mixer_c(qc, kc, vc, _softmax_shift(gq[4], gq[5], 0.0), B, S)
        x2 = _merge(x2, g1, oa, lse, ob, oc, w_in, w_br_a, w_br_b, w_br_c, w_o, l, B, S)
        x2 = _ffn(x2, norm2[l][None, :], w_up, w_down, l)
    return x2.reshape(B, S, D)
```
